```python
import math
import jax, jax.numpy as jnp
from jax import lax
import numpy as np

D_MODEL = 1024
BATCH = 32
SEQ = 2048
DEPTH = 2

N_MIXERS = 2
CONV_WIDTH = 3
HEAD_DIM = 64
N_Q_HEADS = D_MODEL // HEAD_DIM
N_KV_HEADS = 4
GROUP = N_Q_HEADS // N_KV_HEADS
WINDOW = 128
BLOCK = 128
D_FF = ((8 * D_MODEL // 3 + 255) // 256) * 256
QKV_WIDTH = (N_Q_HEADS + 2 * N_KV_HEADS) * HEAD_DIM
N_CONV_LAYERS = (DEPTH + 1) // 2
N_ATTN_LAYERS = DEPTH // 2
EPS = 1e-6

kernel_name = "hybrid_shortconv_swa_sink_alibi_swiglu"


def rmsnorm(x, gain):
    xf = x.astype(jnp.float32)
    r = lax.rsqrt(jnp.mean(xf * xf, axis=-1, keepdims=True) + EPS)
    return (xf * r).astype(x.dtype) * gain


def alibi_slopes():
    h = jnp.arange(1, N_Q_HEADS + 1, dtype=jnp.float32)
    return jnp.exp2(-8.0 * h / N_Q_HEADS)


def short_conv_mixer(h, w_in, conv_w, w_out):
    d = h.shape[-1]
    bcx = h @ w_in
    b_gate, c_gate, xv = jnp.split(bcx, 3, axis=-1)
    u = b_gate * xv
    y = lax.conv_general_dilated(
        u, conv_w[:, None, :].astype(u.dtype),
        window_strides=(1,), padding=[(CONV_WIDTH - 1, 0)],
        dimension_numbers=('NWC', 'WIO', 'NWC'), feature_group_count=d)
    return (c_gate * y) @ w_out


def swa_sink_attention(h, w_qkv, q_gain, k_gain, sinks, w_o):
    bsz, s, _ = h.shape
    nb = s // BLOCK
    qkv = h @ w_qkv
    q_end = N_Q_HEADS * HEAD_DIM
    k_end = q_end + N_KV_HEADS * HEAD_DIM
    q = qkv[..., :q_end].reshape(bsz, s, N_KV_HEADS, GROUP, HEAD_DIM)
    k = qkv[..., q_end:k_end].reshape(bsz, s, N_KV_HEADS, HEAD_DIM)
    v = qkv[..., k_end:].reshape(bsz, s, N_KV_HEADS, HEAD_DIM)
    q = rmsnorm(q, q_gain)
    k = rmsnorm(k, k_gain)

    qb = q.reshape(bsz, nb, BLOCK, N_KV_HEADS, GROUP, HEAD_DIM)

    def band(t):
        tp = jnp.pad(t, ((0, 0), (BLOCK, 0), (0, 0), (0, 0)))
        tb = tp.reshape(bsz, nb + 1, BLOCK, N_KV_HEADS, HEAD_DIM)
        return jnp.concatenate([tb[:, :-1], tb[:, 1:]], axis=2)

    kw = band(k)
    vw = band(v)

    scale = 1.0 / math.sqrt(HEAD_DIM)
    scores = jnp.einsum('bnqkgd,bnskd->bnkgqs', qb, kw).astype(jnp.float32) * scale

    qi = jnp.arange(BLOCK)[:, None]
    kj = jnp.arange(2 * BLOCK)[None, :]
    dist = qi + BLOCK - kj
    key_pos = (jnp.arange(nb) * BLOCK - BLOCK)[:, None, None] + kj[None]
    mask = (dist >= 0)[None] & (dist < WINDOW)[None] & (key_pos >= 0)

    slopes = alibi_slopes().reshape(N_KV_HEADS, GROUP)
    alibi = -slopes[:, :, None, None] * dist.astype(jnp.float32)[None, None]
    scores = scores + alibi[None, None]
    scores = jnp.where(mask[None, :, None, None], scores, -jnp.inf)

    sink = sinks.astype(jnp.float32).reshape(N_KV_HEADS, GROUP)
    sink_col = jnp.broadcast_to(sink[None, None, :, :, None, None],
                                scores.shape[:-1] + (1,))
    logits = jnp.concatenate([scores, sink_col], axis=-1)
    p = jax.nn.softmax(logits, axis=-1)[..., :-1]

    out = jnp.einsum('bnkgqs,bnskd->bnqkgd', p.astype(vw.dtype), vw)
    out = out.reshape(bsz, s, N_Q_HEADS * HEAD_DIM)
    return out @ w_o


def swiglu_ffn(h, w_gate_up, w_down):
    gu = h @ w_gate_up
    g, u = jnp.split(gu, 2, axis=-1)
    return (jax.nn.silu(g) * u) @ w_down


def _fwd_setup_inputs(seed: int = 0) -> dict:
    key = jax.random.key(seed)
    ks = jax.random.split(key, 14)
    f32 = jnp.float32
    d = D_MODEL

    def w(k, shape, fan_in):
        return jax.random.normal(k, shape, f32) * (fan_in ** -0.5)

    return {
        "x": jax.random.normal(ks[0], (BATCH, SEQ, d), f32),
        "conv_w_in": w(ks[1], (N_CONV_LAYERS, d, 3 * d), d),
        "conv_w": w(ks[2], (N_CONV_LAYERS, CONV_WIDTH, d), CONV_WIDTH),
        "conv_w_out": w(ks[3], (N_CONV_LAYERS, d, d), d),
        "attn_w_qkv": w(ks[4], (N_ATTN_LAYERS, d, QKV_WIDTH), d),
        "attn_q_gain": 1.0 + 0.05 * jax.random.normal(ks[5], (N_ATTN_LAYERS, HEAD_DIM), f32),
        "attn_k_gain": 1.0 + 0.05 * jax.random.normal(ks[6], (N_ATTN_LAYERS, HEAD_DIM), f32),
        "attn_sinks": 0.5 * jax.random.normal(ks[7], (N_ATTN_LAYERS, N_Q_HEADS), f32),
        "attn_w_o": w(ks[8], (N_ATTN_LAYERS, N_Q_HEADS * HEAD_DIM, d), N_Q_HEADS * HEAD_DIM),
        "norm_mixer": 1.0 + 0.05 * jax.random.normal(ks[9], (DEPTH, d), f32),
        "norm_ffn": 1.0 + 0.05 * jax.random.normal(ks[10], (DEPTH, d), f32),
        "ffn_w_gate_up": w(ks[11], (DEPTH, d, 2 * D_FF), d),
        "ffn_w_down": w(ks[12], (DEPTH, D_FF, d), D_FF),
    }


def _fwd_reference(x, conv_w_in, conv_w, conv_w_out, attn_w_qkv, attn_q_gain, attn_k_gain,
              attn_sinks, attn_w_o, norm_mixer, norm_ffn, ffn_w_gate_up, ffn_w_down):
    for i in range(DEPTH):
        h = rmsnorm(x, norm_mixer[i])
        j = i // N_MIXERS
        if i % N_MIXERS == 0:
            mix = short_conv_mixer(h, conv_w_in[j], conv_w[j], conv_w_out[j])
        else:
            mix = swa_sink_attention(h, attn_w_qkv[j], attn_q_gain[j], attn_k_gain[j],
                                     attn_sinks[j], attn_w_o[j])
        x = x + mix
        h = rmsnorm(x, norm_ffn[i])
        x = x + swiglu_ffn(h, ffn_w_gate_up[i], ffn_w_down[i])
    return x


import jax as _jax
import jax.numpy as _jnp

TWIN_FORMAT = 'train_step'
FWD_PARAMS = ['x', 'conv_w_in', 'conv_w', 'conv_w_out', 'attn_w_qkv', 'attn_q_gain', 'attn_k_gain', 'attn_sinks', 'attn_w_o', 'norm_mixer', 'norm_ffn', 'ffn_w_gate_up', 'ffn_w_down']
TWIN_WEIGHTS = ['conv_w_in', 'conv_w', 'conv_w_out', 'attn_w_qkv', 'attn_q_gain', 'attn_k_gain', 'attn_sinks', 'attn_w_o', 'norm_mixer', 'norm_ffn', 'ffn_w_gate_up', 'ffn_w_down']
TWIN_DIFF_INPUT = 'x'
TWIN_INPUTS = ['x', 'conv_w_in', 'conv_w', 'conv_w_out', 'attn_w_qkv', 'attn_q_gain', 'attn_k_gain', 'attn_sinks', 'attn_w_o', 'norm_mixer', 'norm_ffn', 'ffn_w_gate_up', 'ffn_w_down', 'loss_target', 'm_conv_w_in', 'm_conv_w', 'm_conv_w_out', 'm_attn_w_qkv', 'm_attn_q_gain', 'm_attn_k_gain', 'm_attn_sinks', 'm_attn_w_o', 'm_norm_mixer', 'm_norm_ffn', 'm_ffn_w_gate_up', 'm_ffn_w_down', 'v_conv_w_in', 'v_conv_w', 'v_conv_w_out', 'v_attn_w_qkv', 'v_attn_q_gain', 'v_attn_k_gain', 'v_attn_sinks', 'v_attn_w_o', 'v_norm_mixer', 'v_norm_ffn', 'v_ffn_w_gate_up', 'v_ffn_w_down']
TWIN_OUTPUTS = ['loss', 'grad_x', 'grad_conv_w_in', 'grad_conv_w', 'grad_conv_w_out', 'grad_attn_w_qkv', 'grad_attn_q_gain', 'grad_attn_k_gain', 'grad_attn_sinks', 'grad_attn_w_o', 'grad_norm_mixer', 'grad_norm_ffn', 'grad_ffn_w_gate_up', 'grad_ffn_w_down', 'delta_conv_w_in', 'delta_conv_w', 'delta_conv_w_out', 'delta_attn_w_qkv', 'delta_attn_q_gain', 'delta_attn_k_gain', 'delta_attn_sinks', 'delta_attn_w_o', 'delta_norm_mixer', 'delta_norm_ffn', 'delta_ffn_w_gate_up', 'delta_ffn_w_down', 'new_m_conv_w_in', 'new_m_conv_w', 'new_m_conv_w_out', 'new_m_attn_w_qkv', 'new_m_attn_q_gain', 'new_m_attn_k_gain', 'new_m_attn_sinks', 'new_m_attn_w_o', 'new_m_norm_mixer', 'new_m_norm_ffn', 'new_m_ffn_w_gate_up', 'new_m_ffn_w_down', 'new_v_conv_w_in', 'new_v_conv_w', 'new_v_conv_w_out', 'new_v_attn_w_qkv', 'new_v_attn_q_gain', 'new_v_attn_k_gain', 'new_v_attn_sinks', 'new_v_attn_w_o', 'new_v_norm_mixer', 'new_v_norm_ffn', 'new_v_ffn_w_gate_up', 'new_v_ffn_w_down']
TWIN_LEAF_KINDS = {'loss': 'loss', 'grad_x': 'grad_x', 'grad_conv_w_in': 'grad_w', 'grad_conv_w': 'grad_w', 'grad_conv_w_out': 'grad_w', 'grad_attn_w_qkv': 'grad_w', 'grad_attn_q_gain': 'grad_w', 'grad_attn_k_gain': 'grad_w', 'grad_attn_sinks': 'grad_w', 'grad_attn_w_o': 'grad_w', 'grad_norm_mixer': 'grad_w', 'grad_norm_ffn': 'grad_w', 'grad_ffn_w_gate_up': 'grad_w', 'grad_ffn_w_down': 'grad_w', 'delta_conv_w_in': 'delta_w', 'delta_conv_w': 'delta_w', 'delta_conv_w_out': 'delta_w', 'delta_attn_w_qkv': 'delta_w', 'delta_attn_q_gain': 'delta_w', 'delta_attn_k_gain': 'delta_w', 'delta_attn_sinks': 'delta_w', 'delta_attn_w_o': 'delta_w', 'delta_norm_mixer': 'delta_w', 'delta_norm_ffn': 'delta_w', 'delta_ffn_w_gate_up': 'delta_w', 'delta_ffn_w_down': 'delta_w', 'new_m_conv_w_in': 'new_m', 'new_m_conv_w': 'new_m', 'new_m_conv_w_out': 'new_m', 'new_m_attn_w_qkv': 'new_m', 'new_m_attn_q_gain': 'new_m', 'new_m_attn_k_gain': 'new_m', 'new_m_attn_sinks': 'new_m', 'new_m_attn_w_o': 'new_m', 'new_m_norm_mixer': 'new_m', 'new_m_norm_ffn': 'new_m', 'new_m_ffn_w_gate_up': 'new_m', 'new_m_ffn_w_down': 'new_m', 'new_v_conv_w_in': 'new_v', 'new_v_conv_w': 'new_v', 'new_v_conv_w_out': 'new_v', 'new_v_attn_w_qkv': 'new_v', 'new_v_attn_q_gain': 'new_v', 'new_v_attn_k_gain': 'new_v', 'new_v_attn_sinks': 'new_v', 'new_v_attn_w_o': 'new_v', 'new_v_norm_mixer': 'new_v', 'new_v_norm_ffn': 'new_v', 'new_v_ffn_w_gate_up': 'new_v', 'new_v_ffn_w_down': 'new_v'}


def _forward(args):
    return _fwd_reference(*[args[k] for k in FWD_PARAMS])


def _output_shape():
    out = _jax.eval_shape(lambda: _forward(_fwd_setup_inputs(0)))
    return out.shape, out.dtype

N_MICROBATCH = 1
ADAM_LR = 0.001
ADAM_B1 = 0.9
ADAM_B2 = 0.999
ADAM_EPS = 1e-08
ADAM_WD = 0.01
ADAM_STEP = 10
PER_EXAMPLE_BATCH_AXIS = {'x': 0, 'loss_target': 0}
SHARED_INPUTS = []
_WEIGHT_DTYPES = {'conv_w_in': _jnp.float32, 'conv_w': _jnp.float32, 'conv_w_out': _jnp.float32, 'attn_w_qkv': _jnp.float32, 'attn_q_gain': _jnp.float32, 'attn_k_gain': _jnp.float32, 'attn_sinks': _jnp.float32, 'attn_w_o': _jnp.float32, 'norm_mixer': _jnp.float32, 'norm_ffn': _jnp.float32, 'ffn_w_gate_up': _jnp.float32, 'ffn_w_down': _jnp.float32}
MOMENT_SCALE = {'conv_w_in': 1.625727e+00, 'conv_w': 3.636198e+01, 'conv_w_out': 1.584922e+00, 'attn_w_qkv': 4.011487e-01, 'attn_q_gain': 3.839669e+01, 'attn_k_gain': 3.884174e+01, 'attn_sinks': 6.582845e+01, 'attn_w_o': 3.325360e-01, 'norm_mixer': 1.369217e+02, 'norm_ffn': 4.918009e+01, 'ffn_w_gate_up': 3.000672e-01, 'ffn_w_down': 5.134453e-01}


def _to_microbatches(a, axis):
    t = _jnp.moveaxis(a, axis, 0)
    t = t.reshape((N_MICROBATCH, t.shape[0] // N_MICROBATCH) + t.shape[1:])
    return _jnp.moveaxis(t, 1, axis + 1)


def setup_inputs(seed: int = 0) -> dict:
    inp = _fwd_setup_inputs(seed)
    key = _jax.random.fold_in(_jax.random.key(seed), 7919)
    shape, _ = _output_shape()
    out = dict(inp)
    out["loss_target"] = _jax.random.normal(_jax.random.fold_in(key, 0), shape, _jnp.float32)
    for i, name in enumerate(TWIN_WEIGHTS):
        w = inp[name].astype(_jnp.float32)
        if MOMENT_SCALE is None:
            s = _jnp.sqrt(_jnp.mean(_jnp.square(w)) + 1e-30)
        else:
            s = MOMENT_SCALE[name]
        km, kv = _jax.random.split(_jax.random.fold_in(key, i + 1))
        out[name] = w
        out["m_" + name] = s * _jax.random.normal(km, w.shape, _jnp.float32)
        out["v_" + name] = (s * s) * _jax.random.uniform(kv, w.shape, _jnp.float32, 0.5, 1.5)
    if N_MICROBATCH > 1:
        for name, axis in PER_EXAMPLE_BATCH_AXIS.items():
            out[name] = _to_microbatches(out[name], axis)
    return {'x': out['x'], 'conv_w_in': out['conv_w_in'], 'conv_w': out['conv_w'], 'conv_w_out': out['conv_w_out'], 'attn_w_qkv': out['attn_w_qkv'], 'attn_q_gain': out['attn_q_gain'], 'attn_k_gain': out['attn_k_gain'], 'attn_sinks': out['attn_sinks'], 'attn_w_o': out['attn_w_o'], 'norm_mixer': out['norm_mixer'], 'norm_ffn': out['norm_ffn'], 'ffn_w_gate_up': out['ffn_w_gate_up'], 'ffn_w_down': out['ffn_w_down'], 'loss_target': out['loss_target'], 'm_conv_w_in': out['m_conv_w_in'], 'm_conv_w': out['m_conv_w'], 'm_conv_w_out': out['m_conv_w_out'], 'm_attn_w_qkv': out['m_attn_w_qkv'], 'm_attn_q_gain': out['m_attn_q_gain'], 'm_attn_k_gain': out['m_attn_k_gain'], 'm_attn_sinks': out['m_attn_sinks'], 'm_attn_w_o': out['m_attn_w_o'], 'm_norm_mixer': out['m_norm_mixer'], 'm_norm_ffn': out['m_norm_ffn'], 'm_ffn_w_gate_up': out['m_ffn_w_gate_up'], 'm_ffn_w_down': out['m_ffn_w_down'], 'v_conv_w_in': out['v_conv_w_in'], 'v_conv_w': out['v_conv_w'], 'v_conv_w_out': out['v_conv_w_out'], 'v_attn_w_qkv': out['v_attn_w_qkv'], 'v_attn_q_gain': out['v_attn_q_gain'], 'v_attn_k_gain': out['v_attn_k_gain'], 'v_attn_sinks': out['v_attn_sinks'], 'v_attn_w_o': out['v_attn_w_o'], 'v_norm_mixer': out['v_norm_mixer'], 'v_norm_ffn': out['v_norm_ffn'], 'v_ffn_w_gate_up': out['v_ffn_w_gate_up'], 'v_ffn_w_down': out['v_ffn_w_down']}


def _loss(weights, diff, rest, loss_target):
    with _jax.named_scope("forward"):
        args = {**rest, TWIN_DIFF_INPUT: diff, **{k: w.astype(_WEIGHT_DTYPES[k]) for k, w in weights.items()}}
        y = _forward(args)
    with _jax.named_scope("loss_head"):
        err = _jnp.square(y.astype(_jnp.float32) - loss_target)
        return 0.5 * _jnp.sum(_jnp.mean(err, axis=-1)) if err.ndim else 0.5 * err


def _adamw(w, g, m, v):
    m = ADAM_B1 * m + (1.0 - ADAM_B1) * g
    v = ADAM_B2 * v + (1.0 - ADAM_B2) * _jnp.square(g)
    m_hat = m / (1.0 - ADAM_B1 ** ADAM_STEP)
    v_hat = v / (1.0 - ADAM_B2 ** ADAM_STEP)
    delta = -ADAM_LR * (m_hat / (_jnp.sqrt(v_hat) + ADAM_EPS) + ADAM_WD * w)
    return delta, m, v


def reference(x, conv_w_in, conv_w, conv_w_out, attn_w_qkv, attn_q_gain, attn_k_gain, attn_sinks, attn_w_o, norm_mixer, norm_ffn, ffn_w_gate_up, ffn_w_down, loss_target, m_conv_w_in, m_conv_w, m_conv_w_out, m_attn_w_qkv, m_attn_q_gain, m_attn_k_gain, m_attn_sinks, m_attn_w_o, m_norm_mixer, m_norm_ffn, m_ffn_w_gate_up, m_ffn_w_down, v_conv_w_in, v_conv_w, v_conv_w_out, v_attn_w_qkv, v_attn_q_gain, v_attn_k_gain, v_attn_sinks, v_attn_w_o, v_norm_mixer, v_norm_ffn, v_ffn_w_gate_up, v_ffn_w_down):
    given = dict(x=x, conv_w_in=conv_w_in, conv_w=conv_w, conv_w_out=conv_w_out, attn_w_qkv=attn_w_qkv, attn_q_gain=attn_q_gain, attn_k_gain=attn_k_gain, attn_sinks=attn_sinks, attn_w_o=attn_w_o, norm_mixer=norm_mixer, norm_ffn=norm_ffn, ffn_w_gate_up=ffn_w_gate_up, ffn_w_down=ffn_w_down, loss_target=loss_target, m_conv_w_in=m_conv_w_in, m_conv_w=m_conv_w, m_conv_w_out=m_conv_w_out, m_attn_w_qkv=m_attn_w_qkv, m_attn_q_gain=m_attn_q_gain, m_attn_k_gain=m_attn_k_gain, m_attn_sinks=m_attn_sinks, m_attn_w_o=m_attn_w_o, m_norm_mixer=m_norm_mixer, m_norm_ffn=m_norm_ffn, m_ffn_w_gate_up=m_ffn_w_gate_up, m_ffn_w_down=m_ffn_w_down, v_conv_w_in=v_conv_w_in, v_conv_w=v_conv_w, v_conv_w_out=v_conv_w_out, v_attn_w_qkv=v_attn_w_qkv, v_attn_q_gain=v_attn_q_gain, v_attn_k_gain=v_attn_k_gain, v_attn_sinks=v_attn_sinks, v_attn_w_o=v_attn_w_o, v_norm_mixer=v_norm_mixer, v_norm_ffn=v_norm_ffn, v_ffn_w_gate_up=v_ffn_w_gate_up, v_ffn_w_down=v_ffn_w_down)
    weights = {n: given[n] for n in TWIN_WEIGHTS}
    shared = {n: given[n] for n in SHARED_INPUTS}
    per_example = {n: given[n] for n in ['x']}
    grad_fn = _jax.value_and_grad(_loss, argnums=(0, 1))

    def one_microbatch(ex, loss_target):
        ex = dict(ex)
        diff = ex.pop(TWIN_DIFF_INPUT)
        return grad_fn(weights, diff, {**shared, **ex}, loss_target)

    if N_MICROBATCH == 1:
        loss, (grad_w, grad_x) = one_microbatch(per_example, given["loss_target"])
    else:
        def body(carry, xs):
            loss_sum, grad_sum = carry
            l_k, (gw_k, gx_k) = one_microbatch(xs[0], xs[1])
            with _jax.named_scope("update"):
                return (loss_sum + l_k, _jax.tree.map(_jnp.add, grad_sum, gw_k)), gx_k

        init = (_jnp.zeros((), _jnp.float32), _jax.tree.map(_jnp.zeros_like, weights))
        (loss, grad_w), grad_x = _jax.lax.scan(body, init, (per_example, given["loss_target"]))
    with _jax.named_scope("update"):
        delta_w, new_m, new_v = {}, {}, {}
        for n in TWIN_WEIGHTS:
            delta_w[n], new_m[n], new_v[n] = _adamw(weights[n], grad_w[n], given["m_" + n], given["v_" + n])
    return (loss, grad_x, *[grad_w[n] for n in TWIN_WEIGHTS], *[delta_w[n] for n in TWIN_WEIGHTS],
            *[new_m[n] for n in TWIN_WEIGHTS], *[new_v[n] for n in TWIN_WEIGHTS])
```

```python
import functools

import jax
import jax.numpy as jnp
from jax import lax
from jax.experimental import pallas as pl
from jax.experimental.pallas import tpu as pltpu

F32 = jnp.float32
BF16 = jnp.bfloat16

D_MODEL = 1024
D_FF = 2816
N_Q_HEADS = 16
N_KV_HEADS = 4
HEAD_DIM = 64
WINDOW = 128
BLOCK = 128
EPS = 1e-6
N_CHIPS = 4
LANES = 128
SUBLANES = 8
BF16_ROWS = 16
VMEM_LIMIT = 48 * 1024 * 1024
ADAM_LR, ADAM_B1, ADAM_B2, ADAM_EPS, ADAM_WD, ADAM_STEP = 0.001, 0.9, 0.999, 1e-08, 0.01, 10
ALIBI_SLOPES = tuple(2.0 ** (-8.0 * (h + 1) / N_Q_HEADS) for h in range(N_Q_HEADS))
SMALL_ROWS = 32
MESH = pl.DeviceIdType.MESH

NN = ((1,), (0,))
NT = ((1,), (1,))
TN = ((0,), (0,))


def _dot(a, b, dims):
    return lax.dot_general(a, b, (dims, ((), ())), preferred_element_type=F32)


def _pick(n, cands):
    for c in cands:
        if n % c == 0:
            return c
    raise ValueError((n, cands))


def _params(sem):
    return pltpu.CompilerParams(dimension_semantics=sem, vmem_limit_bytes=VMEM_LIMIT)


def _sds(shape, dtype):
    return jax.ShapeDtypeStruct(shape, dtype)


def _mm_up(name, a, w4, layer, out_dtype=BF16):
    t, k = a.shape
    _, _, _, nq = w4.shape
    tm = _pick(t, (512, 256, 128))

    def body(a_ref, w_ref, o_ref):
        o_ref[...] = _dot(a_ref[...], w_ref[...], NN).astype(o_ref.dtype)

    return pl.pallas_call(
        body, name=name, grid=(N_CHIPS, t // tm),
        in_specs=[pl.BlockSpec((tm, k), lambda q, i: (i, 0)),
                  pl.BlockSpec((None, None, k, nq), lambda q, i: (layer, q, 0, 0))],
        out_specs=pl.BlockSpec((tm, nq), lambda q, i: (i, q)),
        out_shape=_sds((t, N_CHIPS * nq), out_dtype),
        compiler_params=_params(("parallel", "parallel")))(a, w4)


def _mm_down(name, a, w, layer, res):
    t, kf = a.shape
    _, _, n = w.shape
    tm = _pick(t, (512, 256, 128))

    def body(a_ref, w_ref, r_ref, o_ref):
        o_ref[...] = r_ref[...] + _dot(a_ref[...], w_ref[...], NN)

    return pl.pallas_call(
        body, name=name, grid=(t // tm,),
        in_specs=[pl.BlockSpec((tm, kf), lambda i: (i, 0)),
                  pl.BlockSpec((None, kf, n), lambda i: (layer, 0, 0)),
                  pl.BlockSpec((tm, n), lambda i: (i, 0))],
        out_specs=pl.BlockSpec((tm, n), lambda i: (i, 0)),
        out_shape=_sds((t, n), F32),
        compiler_params=_params(("parallel",)))(a, w, res)


def _mm_down_t(name, dx, w, layer, tn):
    t, n = dx.shape
    _, kf, _ = w.shape
    tm = _pick(t, (512, 256, 128))

    def body(a_ref, w_ref, o_ref):
        o_ref[...] = _dot(a_ref[...].astype(BF16), w_ref[...], NT).astype(o_ref.dtype)

    return pl.pallas_call(
        body, name=name, grid=(kf // tn, t // tm),
        in_specs=[pl.BlockSpec((tm, n), lambda j, i: (i, 0)),
                  pl.BlockSpec((None, tn, n), lambda j, i: (layer, j, 0))],
        out_specs=pl.BlockSpec((tm, tn), lambda j, i: (i, j)),
        out_shape=_sds((t, kf), BF16),
        compiler_params=_params(("parallel", "parallel")))(dx, w)


def _mm_up_t(name, dy, w4, layer):
    t, n = dy.shape
    _, _, k, nq = w4.shape
    tm = _pick(t, (256, 128))

    def body(a_ref, w_ref, o_ref):
        acc = _dot(a_ref[:, 0:nq], w_ref[0], NT)
        for q in range(1, N_CHIPS):
            acc = acc + _dot(a_ref[:, q * nq:(q + 1) * nq], w_ref[q], NT)
        o_ref[...] = acc

    return pl.pallas_call(
        body, name=name, grid=(t // tm,),
        in_specs=[pl.BlockSpec((tm, n), lambda i: (i, 0)),
                  pl.BlockSpec((None, N_CHIPS, k, nq), lambda i: (layer, 0, 0, 0))],
        out_specs=pl.BlockSpec((tm, k), lambda i: (i, 0)),
        out_shape=_sds((t, k), F32),
        compiler_params=_params(("parallel",)))(dy, w4)


def _mm_in_t(name, d3, w4):
    _, t, d = d3.shape
    _, _, k, nq = w4.shape
    tm = _pick(t, (256, 128))
    piece = 256
    per_part, per_q = d // piece, nq // piece

    def body(a_ref, w_ref, o_ref):
        acc = None
        for jb in range(3 * per_part):
            a = a_ref[jb // per_part, :, (jb % per_part) * piece:(jb % per_part + 1) * piece]
            w = w_ref[jb // per_q, :, (jb % per_q) * piece:(jb % per_q + 1) * piece]
            term = _dot(a, w, NT)
            acc = term if acc is None else acc + term
        o_ref[...] = acc

    return pl.pallas_call(
        body, name=name, grid=(t // tm,),
        in_specs=[pl.BlockSpec((3, tm, d), lambda i: (0, i, 0)),
                  pl.BlockSpec((None, N_CHIPS, k, nq), lambda i: (0, 0, 0, 0))],
        out_specs=pl.BlockSpec((tm, k), lambda i: (i, 0)),
        out_shape=_sds((t, k), F32),
        compiler_params=_params(("parallel",)))(d3, w4)


def _wgrad(name, a, b, *, grid, a_spec, b_spec, o_spec, out_shape):
    def body(a_ref, b_ref, o_ref):
        @pl.when(pl.program_id(1) == 0)
        def _():
            o_ref[...] = jnp.zeros_like(o_ref)

        o_ref[...] += _dot(a_ref[...].astype(BF16), b_ref[...].astype(BF16), TN)

    return pl.pallas_call(
        body, name=name, grid=grid, in_specs=[a_spec, b_spec], out_specs=o_spec,
        out_shape=_sds(out_shape, F32),
        compiler_params=_params(("parallel", "arbitrary")))(a, b)


def _wgrad_up(name, h, dy):
    t, k = h.shape
    nq = dy.shape[1] // N_CHIPS
    tk = _pick(t, (512, 256, 128))
    return _wgrad(name, h, dy, grid=(N_CHIPS, t // tk),
                  a_spec=pl.BlockSpec((tk, k), lambda q, s: (s, 0)),
                  b_spec=pl.BlockSpec((tk, nq), lambda q, s: (s, q)),
                  o_spec=pl.BlockSpec((None, k, nq), lambda q, s: (q, 0, 0)),
                  out_shape=(N_CHIPS, k, nq))


def _wgrad_in(name, h, d3, nq):
    t, k = h.shape
    d = d3.shape[2]
    piece = 256
    per_part, per_q = d // piece, nq // piece
    tk = _pick(t, (2048, 1024, 512, 256, 128))
    return _wgrad(name, h, d3, grid=(3 * per_part, t // tk),
                  a_spec=pl.BlockSpec((tk, k), lambda j, s: (s, 0)),
                  b_spec=pl.BlockSpec((None, tk, piece), lambda j, s: (j // per_part, s, j % per_part)),
                  o_spec=pl.BlockSpec((None, k, piece), lambda j, s: (j // per_q, 0, j % per_q)),
                  out_shape=(N_CHIPS, k, nq))


def _wgrad_down(name, a, dx, tmw):
    t, kf = a.shape
    n = dx.shape[1]
    tk = _pick(t, (512, 256, 128))
    g = _wgrad(name, a, dx, grid=(kf // tmw, t // tk),
               a_spec=pl.BlockSpec((tk, tmw), lambda j, s: (s, j)),
               b_spec=pl.BlockSpec((tk, n), lambda j, s: (s, 0)),
               o_spec=pl.BlockSpec((tmw, n), lambda j, s: (j, 0)),
               out_shape=(kf, n))
    return g.reshape(N_CHIPS, kf // N_CHIPS, n)


def _rms_fwd(name, x, gain):
    t, d = x.shape
    tm = _pick(t, (512, 256, 128))

    def body(x_ref, g_ref, h_ref):
        xv = x_ref[...]
        r = lax.rsqrt(jnp.mean(xv * xv, axis=-1, keepdims=True) + EPS)
        h_ref[...] = ((xv * r) * g_ref[...]).astype(BF16)

    return pl.pallas_call(
        body, name=name, grid=(t // tm,),
        in_specs=[pl.BlockSpec((tm, d), lambda i: (i, 0)), pl.BlockSpec((1, d), lambda i: (0, 0))],
        out_specs=pl.BlockSpec((tm, d), lambda i: (i, 0)),
        out_shape=_sds((t, d), BF16),
        compiler_params=_params(("parallel",)))(x, gain)


def _rms_bwd(name, dh, x, gain, dres):
    t, d = x.shape
    tm = _pick(t, (512, 256, 128))

    def body(dh_ref, x_ref, g_ref, dr_ref, dx_ref, dg_ref):
        @pl.when(pl.program_id(0) == 0)
        def _():
            dg_ref[...] = jnp.zeros_like(dg_ref)

        xv = x_ref[...]
        r = lax.rsqrt(jnp.mean(xv * xv, axis=-1, keepdims=True) + EPS)
        xhat = xv * r
        dhv = dh_ref[...].astype(F32)
        gd = dhv * g_ref[...]
        dx_ref[...] = dr_ref[...] + r * (gd - xhat * jnp.mean(gd * xhat, axis=-1, keepdims=True))
        dg_ref[...] += (dhv * xhat).reshape(tm // SUBLANES, SUBLANES, d).sum(axis=0)

    return pl.pallas_call(
        body, name=name, grid=(t // tm,),
        in_specs=[pl.BlockSpec((tm, d), lambda i: (i, 0)), pl.BlockSpec((tm, d), lambda i: (i, 0)),
                  pl.BlockSpec((1, d), lambda i: (0, 0)), pl.BlockSpec((tm, d), lambda i: (i, 0))],
        out_specs=[pl.BlockSpec((tm, d), lambda i: (i, 0)), pl.BlockSpec((SUBLANES, d), lambda i: (0, 0))],
        out_shape=[_sds((t, d), F32), _sds((SUBLANES, d), F32)],
        compiler_params=_params(("arbitrary",)))(dh, x, gain, dres)


def _loss(y, tgt):
    t, d = y.shape
    tm = _pick(t, (512, 256, 128))
    n = t // tm

    def body(y_ref, t_ref, dy_ref, l_ref, acc_ref):
        i = pl.program_id(0)

        @pl.when(i == 0)
        def _():
            acc_ref[...] = jnp.zeros_like(acc_ref)

        e = y_ref[...] - t_ref[...]
        dy_ref[...] = e * (1.0 / d)
        acc_ref[...] += (e * e).reshape(tm // SUBLANES, SUBLANES, d).sum(axis=0)

        @pl.when(i == n - 1)
        def _():
            l_ref[...] = jnp.sum(acc_ref[...], keepdims=True) * (0.5 / d)

    return pl.pallas_call(
        body, name="loss", grid=(n,),
        in_specs=[pl.BlockSpec((tm, d), lambda i: (i, 0)), pl.BlockSpec((tm, d), lambda i: (i, 0))],
        out_specs=[pl.BlockSpec((tm, d), lambda i: (i, 0)), pl.BlockSpec((1, 1), lambda i: (0, 0))],
        out_shape=[_sds((t, d), F32), _sds((1, 1), F32)],
        scratch_shapes=[pltpu.VMEM((SUBLANES, d), F32)],
        compiler_params=_params(("arbitrary",)))(y, tgt)


def _swiglu_fwd(name, gu):
    t, n2 = gu.shape
    f = n2 // 2
    tm = _pick(t, (256, 128))

    def body(gu_ref, a_ref):
        g = gu_ref[:, 0:f].astype(F32)
        u = gu_ref[:, f:n2].astype(F32)
        a_ref[...] = (g * (1.0 / (1.0 + jnp.exp(-g))) * u).astype(BF16)

    return pl.pallas_call(
        body, name=name, grid=(t // tm,),
        in_specs=[pl.BlockSpec((tm, n2), lambda i: (i, 0))],
        out_specs=pl.BlockSpec((tm, f), lambda i: (i, 0)),
        out_shape=_sds((t, f), BF16),
        compiler_params=_params(("parallel",)))(gu)


def _swiglu_bwd(name, da, gu):
    t, n2 = gu.shape
    f = n2 // 2
    tm = _pick(t, (256, 128))

    def body(da_ref, gu_ref, o_ref):
        g = gu_ref[:, 0:f].astype(F32)
        u = gu_ref[:, f:n2].astype(F32)
        dav = da_ref[...].astype(F32)
        sg = 1.0 / (1.0 + jnp.exp(-g))
        o_ref[:, 0:f] = (dav * u * (sg * (1.0 + g * (1.0 - sg)))).astype(BF16)
        o_ref[:, f:n2] = (dav * (g * sg)).astype(BF16)

    return pl.pallas_call(
        body, name=name, grid=(t // tm,),
        in_specs=[pl.BlockSpec((tm, f), lambda i: (i, 0)), pl.BlockSpec((tm, n2), lambda i: (i, 0))],
        out_specs=pl.BlockSpec((tm, n2), lambda i: (i, 0)),
        out_shape=_sds((t, n2), BF16),
        compiler_params=_params(("parallel",)))(da, gu)


def _shift_rows(u, k, rows):
    s = u.shape[0]
    if k > 0:
        return jnp.where(rows >= k, pltpu.roll(u, k, 0), 0.0)
    return jnp.where(rows < s + k, pltpu.roll(u, s + k, 0), 0.0)


def _conv_fwd(bcx, cw, nseq, seq):
    t, d3 = bcx.shape
    d = d3 // 3
    cb = 256
    nj = d // cb

    def body(b_ref, c_ref, x_ref, cw_ref, z_ref):
        u = b_ref[...].astype(F32) * x_ref[...].astype(F32)
        rows = lax.broadcasted_iota(jnp.int32, u.shape, 0)
        cwv = cw_ref[...]
        y = cwv[2:3] * u + cwv[1:2] * _shift_rows(u, 1, rows) + cwv[0:1] * _shift_rows(u, 2, rows)
        z_ref[...] = (c_ref[...].astype(F32) * y).astype(BF16)

    return pl.pallas_call(
        body, name="conv_fwd", grid=(nseq, nj),
        in_specs=[pl.BlockSpec((seq, cb), lambda b, j: (b, j)),
                  pl.BlockSpec((seq, cb), lambda b, j: (b, nj + j)),
                  pl.BlockSpec((seq, cb), lambda b, j: (b, 2 * nj + j)),
                  pl.BlockSpec((3, cb), lambda b, j: (0, j))],
        out_specs=pl.BlockSpec((seq, cb), lambda b, j: (b, j)),
        out_shape=_sds((t, d), BF16),
        compiler_params=_params(("parallel", "parallel")))(bcx, bcx, bcx, cw)


def _conv_bwd(dz, bcx, cw, nseq, seq):
    t, d3 = bcx.shape
    d = d3 // 3
    cb = 256
    nj = d // cb

    def body(dz_ref, b_ref, c_ref, x_ref, cw_ref, o_ref, dcw_ref):
        @pl.when(pl.program_id(1) == 0)
        def _():
            dcw_ref[...] = jnp.zeros_like(dcw_ref)

        b = b_ref[...].astype(F32)
        c = c_ref[...].astype(F32)
        xv = x_ref[...].astype(F32)
        dzv = dz_ref[...].astype(F32)
        u = b * xv
        rows = lax.broadcasted_iota(jnp.int32, u.shape, 0)
        u1 = _shift_rows(u, 1, rows)
        u2 = _shift_rows(u, 2, rows)
        cwv = cw_ref[...]
        y = cwv[2:3] * u + cwv[1:2] * u1 + cwv[0:1] * u2
        dyc = dzv * c
        du = cwv[2:3] * dyc + cwv[1:2] * _shift_rows(dyc, -1, rows) + cwv[0:1] * _shift_rows(dyc, -2, rows)
        o_ref[0] = (du * xv).astype(BF16)
        o_ref[1] = (dzv * y).astype(BF16)
        o_ref[2] = (du * b).astype(BF16)
        s0 = jnp.sum(dyc * u2, axis=0, keepdims=True)
        s1 = jnp.sum(dyc * u1, axis=0, keepdims=True)
        s2 = jnp.sum(dyc * u, axis=0, keepdims=True)
        tap = lax.broadcasted_iota(jnp.int32, (3, cb), 0)
        dcw_ref[...] += jnp.where(tap == 0, s0, jnp.where(tap == 1, s1, s2))

    return pl.pallas_call(
        body, name="conv_bwd", grid=(nj, nseq),
        in_specs=[pl.BlockSpec((seq, cb), lambda j, b: (b, j)),
                  pl.BlockSpec((seq, cb), lambda j, b: (b, j)),
                  pl.BlockSpec((seq, cb), lambda j, b: (b, nj + j)),
                  pl.BlockSpec((seq, cb), lambda j, b: (b, 2 * nj + j)),
                  pl.BlockSpec((3, cb), lambda j, b: (0, j))],
        out_specs=[pl.BlockSpec((3, seq, cb), lambda j, b: (0, b, j)),
                   pl.BlockSpec((3, cb), lambda j, b: (0, j))],
        out_shape=[_sds((3, t, d), BF16), _sds((3, d), F32)],
        compiler_params=_params(("parallel", "arbitrary")))(dz, bcx, bcx, bcx, cw)


def _pair_norm(x, gain_pair, low):
    sq = x * x
    ss_lo = jnp.sum(jnp.where(low, sq, 0.0), axis=-1, keepdims=True)
    ss_hi = jnp.sum(jnp.where(low, 0.0, sq), axis=-1, keepdims=True)
    r = lax.rsqrt(jnp.where(low, ss_lo, ss_hi) * (1.0 / HEAD_DIM) + EPS)
    xhat = x * r
    return xhat * gain_pair, xhat, r


def _softmax_sink(qn_b, k_b, slope, sink, distf, mask):
    s = _dot(qn_b, k_b, NT) * (1.0 / (HEAD_DIM ** 0.5)) - slope * distf
    s = jnp.where(mask, s, -1e30)
    m = jnp.maximum(jnp.max(s, axis=-1, keepdims=True), sink)
    e = jnp.exp(s - m)
    es = jnp.exp(sink - m)
    inv = 1.0 / (jnp.sum(e, axis=-1, keepdims=True) + es)
    return e * inv, es * inv


def _kv_pairs(kv_tile, parity, low):
    own = jnp.where(low if parity == 0 else jnp.logical_not(low), kv_tile, 0.0)
    other = pltpu.roll(own, HEAD_DIM, 1)
    return (own, other) if parity == 0 else (other, own)


def _attn_geometry(n):
    q0 = pl.multiple_of(n * BLOCK, BLOCK)
    k0 = pl.multiple_of(jnp.maximum(n - 1, 0) * BLOCK, BLOCK)
    qi = lax.broadcasted_iota(jnp.int32, (BLOCK, 2 * BLOCK), 0)
    kj = lax.broadcasted_iota(jnp.int32, (BLOCK, 2 * BLOCK), 1)
    dist = (q0 - k0) + qi - kj
    mask = jnp.logical_and(dist >= 0, dist < WINDOW)
    return q0, k0, dist.astype(F32), mask


def _attn_fwd(qkv, qg_pair, kg_pair, sinks, nseq, seq):
    t = qkv.shape[0]
    dq = N_Q_HEADS * HEAD_DIM
    dkv = N_KV_HEADS * HEAD_DIM

    def body(sk_ref, qkv_ref, qg_ref, kg_ref, o_ref):
        low = lax.broadcasted_iota(jnp.int32, (1, LANES), 1) < HEAD_DIM
        qg = qg_ref[...]
        kg = kg_ref[...]

        def blk(n, carry):
            q0, k0, distf, mask = _attn_geometry(n)
            for kt in range(dkv // LANES):
                kraw = qkv_ref[pl.ds(k0, 2 * BLOCK), dq + kt * LANES:dq + (kt + 1) * LANES].astype(F32)
                vraw = qkv_ref[pl.ds(k0, 2 * BLOCK), dq + dkv + kt * LANES:dq + dkv + (kt + 1) * LANES].astype(F32)
                kn, _, _ = _pair_norm(kraw, kg, low)
                for par in range(2):
                    kh = 2 * kt + par
                    k_lo, k_hi = [v.astype(BF16) for v in _kv_pairs(kn, par, low)]
                    v_lo, v_hi = [v.astype(BF16) for v in _kv_pairs(vraw, par, low)]
                    for jj in range(2):
                        j = 2 * kh + jj
                        qraw = qkv_ref[pl.ds(q0, BLOCK), j * LANES:(j + 1) * LANES].astype(F32)
                        qn, _, _ = _pair_norm(qraw, qg, low)
                        qn_b = qn.astype(BF16)
                        p0, _ = _softmax_sink(qn_b, k_lo, ALIBI_SLOPES[2 * j], sk_ref[0, 2 * j], distf, mask)
                        p1, _ = _softmax_sink(qn_b, k_hi, ALIBI_SLOPES[2 * j + 1], sk_ref[0, 2 * j + 1], distf, mask)
                        o = _dot(p0.astype(BF16), v_lo, NN) + _dot(p1.astype(BF16), v_hi, NN)
                        o_ref[pl.ds(q0, BLOCK), j * LANES:(j + 1) * LANES] = o.astype(BF16)
            return carry

        lax.fori_loop(0, seq // BLOCK, blk, 0)

    return pl.pallas_call(
        body, name="attn_fwd", grid=(nseq,),
        in_specs=[pl.BlockSpec(memory_space=pltpu.SMEM),
                  pl.BlockSpec((seq, dq + 2 * dkv), lambda b: (b, 0)),
                  pl.BlockSpec((1, LANES), lambda b: (0, 0)),
                  pl.BlockSpec((1, LANES), lambda b: (0, 0))],
        out_specs=pl.BlockSpec((seq, dq), lambda b: (b, 0)),
        out_shape=_sds((t, dq), BF16),
        compiler_params=_params(("parallel",)))(sinks, qkv, qg_pair, kg_pair)


def _attn_bwd(do, qkv, qg_pair, kg_pair, sinks, nseq, seq):
    t = qkv.shape[0]
    dq = N_Q_HEADS * HEAD_DIM
    dkv = N_KV_HEADS * HEAD_DIM
    scale = 1.0 / (HEAD_DIM ** 0.5)

    def body(sk_ref, do_ref, qkv_ref, qg_ref, kg_ref, o_ref, dqg_ref, dkg_ref, dsk_ref, acc_ref):
        @pl.when(pl.program_id(0) == 0)
        def _():
            dqg_ref[...] = jnp.zeros_like(dqg_ref)
            dkg_ref[...] = jnp.zeros_like(dkg_ref)
            dsk_ref[...] = jnp.zeros_like(dsk_ref)

        acc_ref[...] = jnp.zeros_like(acc_ref)
        low = lax.broadcasted_iota(jnp.int32, (1, LANES), 1) < HEAD_DIM
        lane = lax.broadcasted_iota(jnp.int32, (1, LANES), 1)
        qg = qg_ref[...]
        kg = kg_ref[...]

        def blk(n, carry):
            dqg_acc, dkg_acc, dsk_acc = carry
            q0, k0, distf, mask = _attn_geometry(n)
            for kt in range(dkv // LANES):
                kraw = qkv_ref[pl.ds(k0, 2 * BLOCK), dq + kt * LANES:dq + (kt + 1) * LANES].astype(F32)
                vraw = qkv_ref[pl.ds(k0, 2 * BLOCK), dq + dkv + kt * LANES:dq + dkv + (kt + 1) * LANES].astype(F32)
                kn, khat, rk = _pair_norm(kraw, kg, low)
                dk_tile = None
                dv_tile = None
                for par in range(2):
                    kh = 2 * kt + par
                    own = low if par == 0 else jnp.logical_not(low)
                    k_lo, k_hi = [v.astype(BF16) for v in _kv_pairs(kn, par, low)]
                    v_lo, v_hi = [v.astype(BF16) for v in _kv_pairs(vraw, par, low)]
                    dkn_acc = jnp.zeros((2 * BLOCK, LANES), F32)
                    dv_acc = jnp.zeros((2 * BLOCK, LANES), F32)
                    for jj in range(2):
                        j = 2 * kh + jj
                        qraw = qkv_ref[pl.ds(q0, BLOCK), j * LANES:(j + 1) * LANES].astype(F32)
                        qn, qhat, rq = _pair_norm(qraw, qg, low)
                        qn_b = qn.astype(BF16)
                        do_b = do_ref[pl.ds(q0, BLOCK), j * LANES:(j + 1) * LANES]
                        dqn = None
                        dkn_pair = []
                        dv_pair = []
                        for e, (k_e, v_e) in enumerate(((k_lo, v_lo), (k_hi, v_hi))):
                            h = 2 * j + e
                            p, ps = _softmax_sink(qn_b, k_e, ALIBI_SLOPES[h], sk_ref[0, h], distf, mask)
                            dp = _dot(do_b, v_e, NT)
                            dsum = jnp.sum(p * dp, axis=-1, keepdims=True)
                            ds_b = ((p * (dp - dsum)) * scale).astype(BF16)
                            dsk_acc = dsk_acc - jnp.where(lane == h, jnp.sum(ps * dsum, axis=0, keepdims=True), 0.0)
                            term = _dot(ds_b, k_e, NN)
                            dqn = term if dqn is None else dqn + term
                            dkn_pair.append(_dot(ds_b, qn_b, TN))
                            dv_pair.append(_dot(p.astype(BF16), do_b, TN))
                        dkn_acc = dkn_acc + jnp.where(low, dkn_pair[0], dkn_pair[1])
                        dv_acc = dv_acc + jnp.where(low, dv_pair[0], dv_pair[1])
                        dqg_acc = dqg_acc + jnp.sum(dqn * qhat, axis=0, keepdims=True)
                        dqhat = dqn * qg
                        prod = dqhat * qhat
                        m_lo = jnp.sum(jnp.where(low, prod, 0.0), axis=-1, keepdims=True)
                        m_hi = jnp.sum(jnp.where(low, 0.0, prod), axis=-1, keepdims=True)
                        mean = jnp.where(low, m_lo, m_hi) * (1.0 / HEAD_DIM)
                        o_ref[pl.ds(q0, BLOCK), j * LANES:(j + 1) * LANES] = (rq * (dqhat - qhat * mean)).astype(BF16)
                    dkn = dkn_acc + pltpu.roll(dkn_acc, HEAD_DIM, 1)
                    dvh = dv_acc + pltpu.roll(dv_acc, HEAD_DIM, 1)
                    khat_own = jnp.where(own, khat, 0.0)
                    khat_dup = khat_own + pltpu.roll(khat_own, HEAD_DIM, 1)
                    dkg_acc = dkg_acc + jnp.sum(jnp.where(own, dkn * khat_dup, 0.0), axis=0, keepdims=True)
                    dkhat = dkn * kg
                    mean_k = jnp.sum(dkhat * khat_dup, axis=-1, keepdims=True) * (1.0 / LANES)
                    dk_raw = rk * (dkhat - khat_dup * mean_k)
                    dk_tile = jnp.where(own, dk_raw, 0.0) if dk_tile is None else jnp.where(own, dk_raw, dk_tile)
                    dv_tile = jnp.where(own, dvh, 0.0) if dv_tile is None else jnp.where(own, dvh, dv_tile)
                acc_ref[pl.ds(k0, 2 * BLOCK), kt * LANES:(kt + 1) * LANES] += dk_tile
                acc_ref[pl.ds(k0, 2 * BLOCK), dkv + kt * LANES:dkv + (kt + 1) * LANES] += dv_tile
            return dqg_acc, dkg_acc, dsk_acc

        zero = jnp.zeros((1, LANES), F32)
        dqg_acc, dkg_acc, dsk_acc = lax.fori_loop(0, seq // BLOCK, blk, (zero, zero, zero))
        dqg_ref[...] += dqg_acc
        dkg_ref[...] += dkg_acc
        dsk_ref[...] += dsk_acc
        o_ref[:, dq:dq + 2 * dkv] = acc_ref[...].astype(BF16)

    small = pl.BlockSpec((1, LANES), lambda b: (0, 0))
    return pl.pallas_call(
        body, name="attn_bwd", grid=(nseq,),
        in_specs=[pl.BlockSpec(memory_space=pltpu.SMEM),
                  pl.BlockSpec((seq, dq), lambda b: (b, 0)),
                  pl.BlockSpec((seq, dq + 2 * dkv), lambda b: (b, 0)),
                  small, small],
        out_specs=[pl.BlockSpec((seq, dq + 2 * dkv), lambda b: (b, 0)), small, small, small],
        out_shape=[_sds((t, dq + 2 * dkv), BF16), _sds((1, LANES), F32), _sds((1, LANES), F32),
                   _sds((1, LANES), F32)],
        scratch_shapes=[pltpu.VMEM((seq, 2 * dkv), F32)],
        compiler_params=_params(("arbitrary",)))(sinks, do, qkv, qg_pair, kg_pair)


def _place():
    x, y, c = lax.axis_index("x"), lax.axis_index("y"), lax.axis_index("c")
    other_chips = [(1 - x, y), (x, 1 - y), (1 - x, 1 - y)]
    return x, y, c, other_chips


def _half_rows(c, rows):
    rh = rows // 2
    return pl.ds(pl.multiple_of(c * rh, BF16_ROWS), rh)


def _any_specs(n):
    return [pl.BlockSpec(memory_space=pl.ANY)] * n


def _allgather_weights(shards):
    n = len(shards)
    shapes = [s.shape for s in shards]

    def body(*refs):
        ws, outs = refs[:n], refs[n:2 * n]
        send_sems, recv_sems, local_sems = refs[2 * n:]
        x, y, c, other_chips = _place()
        me_chip = 2 * x + y
        sibling = (x, y, 1 - c)

        def landing(u, chip, half):
            return outs[u].at[:, chip, _half_rows(half, shapes[u][1]), :]

        def copy(sem, src, dst, to):
            return pltpu.make_async_remote_copy(src_ref=src, dst_ref=dst, send_sem=send_sems.at[sem],
                                                recv_sem=recv_sems.at[sem], device_id=to, device_id_type=MESH)

        mine, sends = [], []
        for u in range(n):
            cp = pltpu.make_async_copy(ws[u], outs[u].at[:, me_chip], local_sems.at[u])
            cp.start()
            mine.append(cp)
            for k, chip in enumerate(other_chips):
                cp = copy(6 * u + k, ws[u].at[:, _half_rows(c, shapes[u][1]), :], landing(u, me_chip, c), (*chip, c))
                cp.start()
                sends.append(cp)
        for u in range(n):
            for k, chip in enumerate(other_chips):
                got = landing(u, 2 * chip[0] + chip[1], c)
                copy(6 * u + k, got, got, (*chip, c)).wait_recv()
                cp = copy(6 * u + 3 + k, got, got, sibling)
                cp.start()
                sends.append(cp)
        for u in range(n):
            for k, chip in enumerate(other_chips):
                got = landing(u, 2 * chip[0] + chip[1], 1 - c)
                copy(6 * u + 3 + k, got, got, sibling).wait_recv()
        for cp in sends:
            cp.wait_send()
        for cp in mine:
            cp.wait()

    return pl.pallas_call(
        body, name="allgather_weights",
        in_specs=_any_specs(n), out_specs=_any_specs(n),
        out_shape=[_sds((s[0], N_CHIPS, s[1], s[2]), BF16) for s in shapes],
        scratch_shapes=[pltpu.SemaphoreType.DMA((6 * n,)), pltpu.SemaphoreType.DMA((6 * n,)),
                        pltpu.SemaphoreType.DMA((n,))],
    )(*shards)


def _exchange_halves(grads):
    n = len(grads)
    shapes = [g.shape for g in grads]

    def body(*refs):
        gs, outs = refs[:n], refs[n:2 * n]
        send_sems, recv_sems = refs[2 * n:]
        x, y, c, _ = _place()
        sends = []
        for u in range(n):
            cp = pltpu.make_async_remote_copy(
                src_ref=gs[u].at[:, _half_rows(1 - c, shapes[u][1]), :], dst_ref=outs[u],
                send_sem=send_sems.at[u], recv_sem=recv_sems.at[u], device_id=(x, y, 1 - c), device_id_type=MESH)
            cp.start()
            sends.append(cp)
        for cp in sends:
            cp.wait_recv()
        for cp in sends:
            cp.wait_send()

    return pl.pallas_call(
        body, name="exchange_halves",
        in_specs=_any_specs(n), out_specs=_any_specs(n),
        out_shape=[_sds((s[0], s[1] // 2, s[2]), F32) for s in shapes],
        scratch_shapes=[pltpu.SemaphoreType.DMA((n,)), pltpu.SemaphoreType.DMA((n,))],
    )(*grads)


def _sum_halves(name, g, got, place):
    _, r, cdim = g.shape
    rh = r // 2
    rt = _pick(rh, (128, 64, 32, 16))
    nr = rh // rt

    def body(s_ref, g_ref, got_ref, pb_ref, pf_ref):
        s = g_ref[...] + got_ref[...]
        pb_ref[...] = s.astype(BF16)

        @pl.when(pl.program_id(1) == s_ref[1])
        def _():
            pf_ref[...] = s

    grid_spec = pltpu.PrefetchScalarGridSpec(
        num_scalar_prefetch=1, grid=(nr, N_CHIPS),
        in_specs=[pl.BlockSpec((None, rt, cdim), lambda i, q, s: (q, s[0] * nr + i, 0)),
                  pl.BlockSpec((None, rt, cdim), lambda i, q, s: (q, i, 0))],
        out_specs=[pl.BlockSpec((None, rt, cdim), lambda i, q, s: (q, i, 0)),
                   pl.BlockSpec((rt, cdim), lambda i, q, s: (i, 0))])
    return pl.pallas_call(
        body, name=name, grid_spec=grid_spec,
        out_shape=[_sds((N_CHIPS, rh, cdim), BF16), _sds((rh, cdim), F32)],
        compiler_params=_params(("parallel", "arbitrary")))(place, g, got)


def _scatter_partials(partials):
    n = len(partials)
    shapes = [p.shape for p in partials]

    def body(*refs):
        ps, outs = refs[:n], refs[n:2 * n]
        send_sems, recv_sems = refs[2 * n:]
        x, y, c, other_chips = _place()
        sends = []
        for u in range(n):
            for k, chip in enumerate(other_chips):
                cp = pltpu.make_async_remote_copy(
                    src_ref=ps[u].at[2 * chip[0] + chip[1]], dst_ref=outs[u].at[k],
                    send_sem=send_sems.at[3 * u + k], recv_sem=recv_sems.at[3 * u + k],
                    device_id=(*chip, c), device_id_type=MESH)
                cp.start()
                sends.append(cp)
        for cp in sends:
            cp.wait_recv()
        for cp in sends:
            cp.wait_send()

    return pl.pallas_call(
        body, name="scatter_partials",
        in_specs=_any_specs(n), out_specs=_any_specs(n),
        out_shape=[_sds((3, s[1], s[2]), BF16) for s in shapes],
        scratch_shapes=[pltpu.SemaphoreType.DMA((3 * n,)), pltpu.SemaphoreType.DMA((3 * n,))],
    )(*partials)


def _sum_partials(name, own, got):
    rh, cdim = own.shape
    rt = _pick(rh, (128, 64, 32, 16))

    def body(own_ref, got_ref, o_ref):
        o_ref[...] = ((own_ref[...] + got_ref[0].astype(F32)) + got_ref[1].astype(F32)) + got_ref[2].astype(F32)

    return pl.pallas_call(
        body, name=name, grid=(rh // rt,),
        in_specs=[pl.BlockSpec((rt, cdim), lambda i: (i, 0)), pl.BlockSpec((3, rt, cdim), lambda i: (0, i, 0))],
        out_specs=pl.BlockSpec((rt, cdim), lambda i: (i, 0)),
        out_shape=_sds((rh, cdim), F32),
        compiler_params=_params(("parallel",)))(own, got)


def _share_halves(halves, layers):
    n = len(halves)
    shapes = [h.shape for h in halves]
    nw = 1 + max(w for w, _ in layers)
    wshape = {}
    for u, (w, l) in enumerate(layers):
        rh, cdim = shapes[u]
        nl = 1 + max(ll for ww, ll in layers if ww == w)
        wshape[w] = (nl, 2 * rh, cdim)

    def body(*refs):
        hs, outs = refs[:n], refs[n:n + nw]
        send_sems, recv_sems, local_sems = refs[n + nw:]
        x, y, c, _ = _place()
        local, sends = [], []
        for u, (w, l) in enumerate(layers):
            dst = outs[w].at[l, _half_rows(c, 2 * shapes[u][0]), :]
            cp = pltpu.make_async_copy(hs[u], dst, local_sems.at[u])
            cp.start()
            local.append(cp)
            cp = pltpu.make_async_remote_copy(src_ref=hs[u], dst_ref=dst, send_sem=send_sems.at[u],
                                              recv_sem=recv_sems.at[u], device_id=(x, y, 1 - c), device_id_type=MESH)
            cp.start()
            sends.append(cp)
        for cp in sends:
            cp.wait_recv()
        for cp in sends:
            cp.wait_send()
        for cp in local:
            cp.wait()

    return pl.pallas_call(
        body, name="share_halves",
        in_specs=_any_specs(n), out_specs=_any_specs(nw),
        out_shape=[_sds(wshape[w], F32) for w in range(nw)],
        scratch_shapes=[pltpu.SemaphoreType.DMA((n,)), pltpu.SemaphoreType.DMA((n,)),
                        pltpu.SemaphoreType.DMA((n,))],
    )(*halves)


def _gather_blocks(block_ref, all_ref, send_sems, recv_sems):
    x, y, c, _ = _place()
    me = 4 * x + 2 * y + c
    all_ref[me] = block_ref[...]
    sends = []
    for rel in range(1, 8):
        fx, fy, fc = (rel >> 2) & 1, (rel >> 1) & 1, rel & 1
        peer = (x ^ fx, y ^ fy, c ^ fc)
        cp = pltpu.make_async_remote_copy(src_ref=block_ref, dst_ref=all_ref.at[me], send_sem=send_sems.at[rel - 1],
                                          recv_sem=recv_sems.at[rel - 1], device_id=peer, device_id_type=MESH)
        cp.start()
        sends.append(cp)
    for cp in sends:
        cp.wait_recv()
    for cp in sends:
        cp.wait_send()


def _gather_conv_w(cw_block):
    r, d = cw_block.shape

    def body(b_ref, o_ref, all_ref, send_sems, recv_sems):
        _gather_blocks(b_ref, all_ref, send_sems, recv_sems)
        o_ref[...] = (all_ref[0] + all_ref[2]) + (all_ref[4] + all_ref[6])

    vm = pl.BlockSpec(memory_space=pltpu.VMEM)
    return pl.pallas_call(
        body, name="gather_conv_w", in_specs=[vm], out_specs=vm, out_shape=_sds((r, d), F32),
        scratch_shapes=[pltpu.VMEM((8, r, d), F32), pltpu.SemaphoreType.DMA((7,)), pltpu.SemaphoreType.DMA((7,))],
    )(cw_block)


def _adam(w, g, m, v):
    m_new = ADAM_B1 * m + (1.0 - ADAM_B1) * g
    v_new = ADAM_B2 * v + (1.0 - ADAM_B2) * (g * g)
    m_hat = m_new / (1.0 - ADAM_B1 ** ADAM_STEP)
    v_hat = v_new / (1.0 - ADAM_B2 ** ADAM_STEP)
    delta = -ADAM_LR * (m_hat / (jnp.sqrt(v_hat) + ADAM_EPS) + ADAM_WD * w)
    return delta, m_new, v_new


def _small_step(dnm0, dnm1, dnf0, dnf1, dcw, dqg, dkg, dsk, loss, w_blk, m_blk, v_blk):
    d = w_blk.shape[1]

    def body(dnm0_ref, dnm1_ref, dnf0_ref, dnf1_ref, dcw_ref, dqg_ref, dkg_ref, dsk_ref, loss_ref,
             w_ref, m_ref, v_ref, g_ref, dl_ref, mo_ref, vo_ref, blk_ref, all_ref, send_sems, recv_sems):
        blk_ref[...] = jnp.zeros_like(blk_ref)
        blk_ref[0:1, :] = jnp.sum(dnm0_ref[...], axis=0, keepdims=True)
        blk_ref[1:2, :] = jnp.sum(dnm1_ref[...], axis=0, keepdims=True)
        blk_ref[8:9, :] = jnp.sum(dnf0_ref[...], axis=0, keepdims=True)
        blk_ref[9:10, :] = jnp.sum(dnf1_ref[...], axis=0, keepdims=True)
        blk_ref[16:19, :] = dcw_ref[...]
        dqg_v = dqg_ref[...]
        dkg_v = dkg_ref[...]
        blk_ref[24:25, 0:LANES] = dqg_v + pltpu.roll(dqg_v, HEAD_DIM, 1)
        blk_ref[24:25, LANES:2 * LANES] = dkg_v + pltpu.roll(dkg_v, HEAD_DIM, 1)
        blk_ref[24:25, 2 * LANES:3 * LANES] = dsk_ref[...]
        blk_ref[24:25, 3 * LANES:4 * LANES] = jnp.broadcast_to(loss_ref[...], (1, LANES))
        _gather_blocks(blk_ref, all_ref, send_sems, recv_sems)
        g = all_ref[0]
        for dev in range(1, 8):
            g = g + all_ref[dev]
        g_ref[...] = g
        delta, m_new, v_new = _adam(w_ref[...], g, m_ref[...], v_ref[...])
        dl_ref[...] = delta
        mo_ref[...] = m_new
        vo_ref[...] = v_new

    vm = pl.BlockSpec(memory_space=pltpu.VMEM)
    blk = _sds((SMALL_ROWS, d), F32)
    return pl.pallas_call(
        body, name="small_step", in_specs=[vm] * 12, out_specs=[vm] * 4, out_shape=[blk] * 4,
        scratch_shapes=[pltpu.VMEM((SMALL_ROWS, d), F32), pltpu.VMEM((8, SMALL_ROWS, d), F32),
                        pltpu.SemaphoreType.DMA((7,)), pltpu.SemaphoreType.DMA((7,))],
    )(dnm0, dnm1, dnf0, dnf1, dcw, dqg, dkg, dsk, loss, w_blk, m_blk, v_blk)


def _adam_step(name, w, g, m, v):
    nl, r, cdim = w.shape
    rt = _pick(r, (128, 64, 32))

    def body(w_ref, g_ref, m_ref, v_ref, d_ref, mo_ref, vo_ref):
        delta, m_new, v_new = _adam(w_ref[...], g_ref[...], m_ref[...], v_ref[...])
        d_ref[...] = delta
        mo_ref[...] = m_new
        vo_ref[...] = v_new

    spec = pl.BlockSpec((None, rt, cdim), lambda l, i: (l, i, 0))
    return pl.pallas_call(
        body, name=name, grid=(nl, r // rt), in_specs=[spec] * 4, out_specs=[spec] * 3,
        out_shape=[_sds(w.shape, F32)] * 3,
        compiler_params=_params(("parallel", "parallel")))(w, g, m, v)


def _pad_rows(a, rows=SUBLANES):
    return jnp.pad(a, ((0, rows - a.shape[0]), (0, 0)))


def _small_block(nm, nf, cw_local, qg, kg, sk, chip):
    d = nm.shape[1]
    cw_rows = lax.dynamic_update_slice(jnp.zeros((SUBLANES, d), F32), cw_local, (0, chip * cw_local.shape[1]))
    misc = jnp.concatenate([qg, qg, kg, kg, jnp.pad(sk, ((0, 0), (0, LANES - sk.shape[1]))),
                            jnp.zeros((1, d - 3 * LANES), F32)], axis=1)
    return jnp.concatenate([_pad_rows(nm), _pad_rows(nf), cw_rows, _pad_rows(misc)], axis=0)


def _unpack_small(blk, chip, cw_cols):
    cw = lax.dynamic_slice(blk[16:19], (0, chip * cw_cols), (3, cw_cols))[None]
    return dict(norm_mixer=blk[0:2], norm_ffn=blk[8:10], conv_w=cw, attn_q_gain=blk[24:25, 0:HEAD_DIM],
                attn_k_gain=blk[24:25, LANES:LANES + HEAD_DIM], attn_sinks=blk[24:25, 2 * LANES:2 * LANES + N_Q_HEADS])


WEIGHT_NAMES = ("conv_w_in", "conv_w", "conv_w_out", "attn_w_qkv", "attn_q_gain", "attn_k_gain", "attn_sinks",
                "attn_w_o", "norm_mixer", "norm_ffn", "ffn_w_gate_up", "ffn_w_down")
BIG = ("conv_w_in", "conv_w_out", "attn_w_qkv", "attn_w_o", "ffn_w_gate_up", "ffn_w_down")


def kernel(x, conv_w_in, conv_w, conv_w_out, attn_w_qkv, attn_q_gain, attn_k_gain, attn_sinks, attn_w_o, norm_mixer, norm_ffn, ffn_w_gate_up, ffn_w_down, loss_target, m_conv_w_in, m_conv_w, m_conv_w_out, m_attn_w_qkv, m_attn_q_gain, m_attn_k_gain, m_attn_sinks, m_attn_w_o, m_norm_mixer, m_norm_ffn, m_ffn_w_gate_up, m_ffn_w_down, v_conv_w_in, v_conv_w, v_conv_w_out, v_attn_w_qkv, v_attn_q_gain, v_attn_k_gain, v_attn_sinks, v_attn_w_o, v_norm_mixer, v_norm_ffn, v_ffn_w_gate_up, v_ffn_w_down):
    w = dict(conv_w_in=conv_w_in, conv_w=conv_w, conv_w_out=conv_w_out, attn_w_qkv=attn_w_qkv,
             attn_q_gain=attn_q_gain, attn_k_gain=attn_k_gain, attn_sinks=attn_sinks, attn_w_o=attn_w_o,
             norm_mixer=norm_mixer, norm_ffn=norm_ffn, ffn_w_gate_up=ffn_w_gate_up, ffn_w_down=ffn_w_down)
    m = dict(conv_w_in=m_conv_w_in, conv_w=m_conv_w, conv_w_out=m_conv_w_out, attn_w_qkv=m_attn_w_qkv,
             attn_q_gain=m_attn_q_gain, attn_k_gain=m_attn_k_gain, attn_sinks=m_attn_sinks, attn_w_o=m_attn_w_o,
             norm_mixer=m_norm_mixer, norm_ffn=m_norm_ffn, ffn_w_gate_up=m_ffn_w_gate_up, ffn_w_down=m_ffn_w_down)
    v = dict(conv_w_in=v_conv_w_in, conv_w=v_conv_w, conv_w_out=v_conv_w_out, attn_w_qkv=v_attn_w_qkv,
             attn_q_gain=v_attn_q_gain, attn_k_gain=v_attn_k_gain, attn_sinks=v_attn_sinks, attn_w_o=v_attn_w_o,
             norm_mixer=v_norm_mixer, norm_ffn=v_norm_ffn, ffn_w_gate_up=v_ffn_w_gate_up, ffn_w_down=v_ffn_w_down)

    nseq, seq, d = x.shape
    t = nseq * seq
    chip = 2 * lax.axis_index("x") + lax.axis_index("y")
    core = lax.axis_index("c")
    place = jnp.stack([core, chip]).astype(jnp.int32)
    x2 = x.reshape(t, d)
    tgt = loss_target.reshape(t, d)

    cw_block = lax.dynamic_update_slice(jnp.zeros((SUBLANES, d), F32), conv_w[0], (0, chip * conv_w.shape[2]))
    cw_full = _gather_conv_w(cw_block)[0:3]
    w_in, w_out, w_qkv, w_o, w_gu, w_dn = _allgather_weights([w[k].astype(BF16) for k in BIG])
    w_out = w_out.reshape(1, d, d)
    w_o = w_o.reshape(1, d, d)
    w_dn = w_dn.reshape(w_dn.shape[0], D_FF, d)

    qg_pair = jnp.concatenate([attn_q_gain, attn_q_gain], axis=1)
    kg_pair = jnp.concatenate([attn_k_gain, attn_k_gain], axis=1)

    def ffn_fwd(i, xin):
        h = _rms_fwd(f"ffn{i}_norm", xin, norm_ffn[i:i + 1])
        gu = _mm_up(f"ffn{i}_up", h, w_gu, i)
        a = _swiglu_fwd(f"ffn{i}_act", gu)
        return h, gu, a, _mm_down(f"ffn{i}_down", a, w_dn, i, xin)

    def ffn_bwd(i, dxo, xin, h, gu, a):
        g_dn = _wgrad_down(f"ffn{i}_down_wgrad", a, dxo, D_FF // 2)
        da = _mm_down_t(f"ffn{i}_down_dgrad", dxo, w_dn, i, D_FF // 2)
        dgu = _swiglu_bwd(f"ffn{i}_act_bwd", da, gu)
        g_gu = _wgrad_up(f"ffn{i}_up_wgrad", h, dgu)
        dh = _mm_up_t(f"ffn{i}_up_dgrad", dgu, w_gu, i)
        dxi, dgain = _rms_bwd(f"ffn{i}_norm_bwd", dh, xin, norm_ffn[i:i + 1], dxo)
        return dxi, dgain, g_gu, g_dn

    h0 = _rms_fwd("conv_norm", x2, norm_mixer[0:1])
    bcx = _mm_up("conv_in", h0, w_in, 0)
    z = _conv_fwd(bcx, cw_full, nseq, seq)
    x1 = _mm_down("conv_out", z, w_out, 0, x2)
    h1, gu0, a0, x2_ = ffn_fwd(0, x1)
    h2 = _rms_fwd("attn_norm", x2_, norm_mixer[1:2])
    qkv = _mm_up("attn_qkv", h2, w_qkv, 0)
    o = _attn_fwd(qkv, qg_pair, kg_pair, attn_sinks, nseq, seq)
    x3 = _mm_down("attn_out", o, w_o, 0, x2_)
    h3, gu1, a1, x4 = ffn_fwd(1, x3)
    dy, loss_part = _loss(x4, tgt)

    dx3, dnf1, g_gu1, g_dn1 = ffn_bwd(1, dy, x3, h3, gu1, a1)
    g_o = _wgrad_down("attn_out_wgrad", o, dx3, d)
    do = _mm_down_t("attn_out_dgrad", dx3, w_o, 0, d)
    dqkv, dqg, dkg, dsk = _attn_bwd(do, qkv, qg_pair, kg_pair, attn_sinks, nseq, seq)
    g_qkv = _wgrad_up("attn_qkv_wgrad", h2, dqkv)
    dh2 = _mm_up_t("attn_qkv_dgrad", dqkv, w_qkv, 0)
    dx2, dnm1 = _rms_bwd("attn_norm_bwd", dh2, x2_, norm_mixer[1:2], dx3)
    dx1, dnf0, g_gu0, g_dn0 = ffn_bwd(0, dx2, x1, h1, gu0, a0)
    g_out = _wgrad_down("conv_out_wgrad", z, dx1, d)
    dz = _mm_down_t("conv_out_dgrad", dx1, w_out, 0, d)
    dbcx, dcw = _conv_bwd(dz, bcx, cw_full, nseq, seq)
    g_in = _wgrad_in("conv_in_wgrad", h0, dbcx, conv_w_in.shape[2])
    dh0 = _mm_in_t("conv_in_dgrad", dbcx, w_in)
    dx0, dnm0 = _rms_bwd("conv_norm_bwd", dh0, x2, norm_mixer[0:1], dx1)

    units = [g_in, g_out, g_qkv, g_o, g_gu0, g_gu1, g_dn0, g_dn1]
    unit_names = ["in", "out", "qkv", "o", "gu0", "gu1", "dn0", "dn1"]
    layers = [(0, 0), (1, 0), (2, 0), (3, 0), (4, 0), (4, 1), (5, 0), (5, 1)]
    got = _exchange_halves(units)
    chip_sums = [_sum_halves(f"sum_halves_{nm}", g, r, place) for nm, g, r in zip(unit_names, units, got)]
    arrived = _scatter_partials([pb for pb, _ in chip_sums])
    halves = [_sum_partials(f"sum_partials_{nm}", pf, r) for nm, (_, pf), r in zip(unit_names, chip_sums, arrived)]
    grads_big = _share_halves(halves, layers)

    grad, delta, new_m, new_v = {}, {}, {}, {}
    for k, g in zip(BIG, grads_big):
        grad[k] = g
        delta[k], new_m[k], new_v[k] = _adam_step(f"adam_{k}", w[k], g, m[k], v[k])

    def blocks(src):
        return _small_block(src["norm_mixer"], src["norm_ffn"], src["conv_w"][0], src["attn_q_gain"],
                            src["attn_k_gain"], src["attn_sinks"], chip)

    g_blk, d_blk, m_blk, v_blk = _small_step(dnm0, dnm1, dnf0, dnf1, dcw, dqg, dkg, dsk, loss_part,
                                             blocks(w), blocks(m), blocks(v))
    cw_cols = conv_w.shape[2]
    for dst, blk in ((grad, g_blk), (delta, d_blk), (new_m, m_blk), (new_v, v_blk)):
        dst.update(_unpack_small(blk, chip, cw_cols))
    loss = g_blk[24, 3 * LANES]

    return (loss, dx0.reshape(nseq, seq, d), *[grad[k] for k in WEIGHT_NAMES], *[delta[k] for k in WEIGHT_NAMES],
            *[new_m[k] for k in WEIGHT_NAMES], *[new_v[k] for k in WEIGHT_NAMES])
```

```python
import jax
import jax.numpy as jnp
from jax import lax
from jax.experimental import pallas as pl
from jax.experimental.pallas import tpu as pltpu

F32 = jnp.float32
BF16 = jnp.bfloat16

D_MODEL = 1024
D_FF = 2816
N_Q_HEADS = 16
N_KV_HEADS = 4
HEAD_DIM = 64
WINDOW = 128
BLOCK = 128
EPS = 1e-6
N_CHIPS = 4
LANES = 128
SUBLANES = 8
BF16_ROWS = 16
MXU_COLS = 256
VMEM_LIMIT = 48 * 1024 * 1024
ADAM_LR, ADAM_B1, ADAM_B2, ADAM_EPS, ADAM_WD, ADAM_STEP = 0.001, 0.9, 0.999, 1e-08, 0.01, 10
ALIBI_SLOPES = tuple(2.0 ** (-8.0 * (h + 1) / N_Q_HEADS) for h in range(N_Q_HEADS))
SMALL_ROWS = 32
MESH = pl.DeviceIdType.MESH

NN = ((1,), (0,))
NT = ((1,), (1,))
TN = ((0,), (0,))


def _dot(a, b, dims):
    return lax.dot_general(a, b, (dims, ((), ())), preferred_element_type=F32)


def _pick(n, cands):
    for c in cands:
        if n % c == 0:
            return c
    raise ValueError((n, cands))


def _params(sem):
    return pltpu.CompilerParams(dimension_semantics=sem, vmem_limit_bytes=VMEM_LIMIT)


def _sds(shape, dtype):
    return jax.ShapeDtypeStruct(shape, dtype)


def _rms(xv):
    return lax.rsqrt(jnp.mean(xv * xv, axis=-1, keepdims=True) + EPS)


def _sigmoid(g):
    return 1.0 / (1.0 + jnp.exp(-g))


def _mm_up_joined(name, a, w4, tm_pref):
    t, k = a.shape
    _, _, _, nq = w4.shape
    tm = _pick(t, (tm_pref, 256, 128))

    def body(a_ref, w_ref, o_ref, wcat_ref):
        @pl.when(pl.program_id(0) == 0)
        def _():
            for q in range(N_CHIPS):
                wcat_ref[:, q * nq:(q + 1) * nq] = w_ref[q]

        o_ref[...] = _dot(a_ref[...], wcat_ref[...], NN).astype(BF16)

    return pl.pallas_call(
        body, name=name, grid=(t // tm,),
        in_specs=[pl.BlockSpec((tm, k), lambda i: (i, 0)),
                  pl.BlockSpec((None, N_CHIPS, k, nq), lambda i: (0, 0, 0, 0))],
        out_specs=pl.BlockSpec((tm, N_CHIPS * nq), lambda i: (i, 0)),
        out_shape=_sds((t, N_CHIPS * nq), BF16),
        scratch_shapes=[pltpu.VMEM((k, N_CHIPS * nq), BF16)],
        compiler_params=_params(("arbitrary",)))(a, w4)


def _mm_up_swiglu(name, h, w4, layer):
    t, k = h.shape
    _, _, _, nq = w4.shape
    tm = _pick(t, (512, 256, 128))

    def body(h_ref, wg_ref, wu_ref, g_ref, u_ref, a_ref):
        hv = h_ref[...]
        g = _dot(hv, wg_ref[...], NN)
        u = _dot(hv, wu_ref[...], NN)
        g_ref[...] = g.astype(BF16)
        u_ref[...] = u.astype(BF16)
        a_ref[...] = (g * _sigmoid(g) * u).astype(BF16)

    half = N_CHIPS // 2
    out = pl.BlockSpec((tm, nq), lambda j, i: (i, j))
    return pl.pallas_call(
        body, name=name, grid=(half, t // tm),
        in_specs=[pl.BlockSpec((tm, k), lambda j, i: (i, 0)),
                  pl.BlockSpec((None, None, k, nq), lambda j, i: (layer, j, 0, 0)),
                  pl.BlockSpec((None, None, k, nq), lambda j, i: (layer, half + j, 0, 0))],
        out_specs=[out, out, out],
        out_shape=[_sds((t, half * nq), BF16)] * 3,
        compiler_params=_params(("parallel", "parallel")))(h, w4, w4)


def _mm_down_norm(name, a, w, layer, res, gain):
    t, kf = a.shape
    _, _, n = w.shape
    tm = _pick(t, (512, 256, 128))

    def body(a_ref, w_ref, r_ref, g_ref, o_ref, h_ref):
        xo = r_ref[...] + _dot(a_ref[...], w_ref[...], NN)
        o_ref[...] = xo
        h_ref[...] = ((xo * _rms(xo)) * g_ref[...]).astype(BF16)

    row = pl.BlockSpec((tm, n), lambda i: (i, 0))
    return pl.pallas_call(
        body, name=name, grid=(t // tm,),
        in_specs=[pl.BlockSpec((tm, kf), lambda i: (i, 0)),
                  pl.BlockSpec((None, kf, n), lambda i: (layer, 0, 0)),
                  row, pl.BlockSpec((1, n), lambda i: (0, 0))],
        out_specs=[row, row],
        out_shape=[_sds((t, n), F32), _sds((t, n), BF16)],
        compiler_params=_params(("parallel",)))(a, w, res, gain)


def _mm_down_loss(name, a, w, layer, res, tgt):
    t, kf = a.shape
    _, _, n = w.shape
    tm = _pick(t, (512, 256, 128))
    steps = t // tm

    def body(a_ref, w_ref, r_ref, t_ref, dy_ref, l_ref, acc_ref):
        i = pl.program_id(0)

        @pl.when(i == 0)
        def _():
            acc_ref[...] = jnp.zeros_like(acc_ref)

        e = (r_ref[...] + _dot(a_ref[...], w_ref[...], NN)) - t_ref[...]
        dy_ref[...] = e * (1.0 / n)
        acc_ref[...] += (e * e).reshape(tm // SUBLANES, SUBLANES, n).sum(axis=0)

        @pl.when(i == steps - 1)
        def _():
            l_ref[...] = jnp.sum(acc_ref[...], keepdims=True) * (0.5 / n)

    row = pl.BlockSpec((tm, n), lambda i: (i, 0))
    return pl.pallas_call(
        body, name=name, grid=(steps,),
        in_specs=[pl.BlockSpec((tm, kf), lambda i: (i, 0)),
                  pl.BlockSpec((None, kf, n), lambda i: (layer, 0, 0)), row, row],
        out_specs=[row, pl.BlockSpec((1, 1), lambda i: (0, 0))],
        out_shape=[_sds((t, n), F32), _sds((1, 1), F32)],
        scratch_shapes=[pltpu.VMEM((SUBLANES, n), F32)],
        compiler_params=_params(("arbitrary",)))(a, w, res, tgt)


def _mm_down_t(name, dx, w, layer):
    t, n = dx.shape
    _, kf, _ = w.shape
    tm = _pick(t, (512, 256, 128))

    def body(a_ref, w_ref, o_ref):
        o_ref[...] = _dot(a_ref[...].astype(BF16), w_ref[...], NT).astype(BF16)

    return pl.pallas_call(
        body, name=name, grid=(t // tm,),
        in_specs=[pl.BlockSpec((tm, n), lambda i: (i, 0)),
                  pl.BlockSpec((None, kf, n), lambda i: (layer, 0, 0))],
        out_specs=pl.BlockSpec((tm, kf), lambda i: (i, 0)),
        out_shape=_sds((t, kf), BF16),
        compiler_params=_params(("parallel",)))(dx, w)


def _mm_down_t_swiglu(name, dx, w, layer, g, u):
    t, n = dx.shape
    f = g.shape[1]
    tn = f // 2
    tm = _pick(t, (512, 256, 128))

    def body(a_ref, w_ref, g_ref, u_ref, dg_ref, du_ref):
        da = _dot(a_ref[...].astype(BF16), w_ref[...], NT)
        gv = g_ref[...].astype(F32)
        sg = _sigmoid(gv)
        dg_ref[...] = (da * u_ref[...].astype(F32) * (sg * (1.0 + gv * (1.0 - sg)))).astype(BF16)
        du_ref[...] = (da * (gv * sg)).astype(BF16)

    tile = pl.BlockSpec((tm, tn), lambda j, i: (i, j))
    return pl.pallas_call(
        body, name=name, grid=(f // tn, t // tm),
        in_specs=[pl.BlockSpec((tm, n), lambda j, i: (i, 0)),
                  pl.BlockSpec((None, tn, n), lambda j, i: (layer, j, 0)), tile, tile],
        out_specs=[tile, tile],
        out_shape=[_sds((t, f), BF16)] * 2,
        compiler_params=_params(("parallel", "parallel")))(dx, w, g, u)


def _dgrad_norm(name, acts, act_blocks, pieces, w4, layer, x, gain, dres):
    t, d = x.shape
    _, _, k, nq = w4.shape
    tm = _pick(t, (256, 128))
    n_act = len(acts)

    def body(*refs):
        act_refs = refs[:n_act]
        w_ref, x_ref, g_ref, dr_ref, dx_ref, dg_ref = refs[n_act:]

        @pl.when(pl.program_id(0) == 0)
        def _():
            dg_ref[...] = jnp.zeros_like(dg_ref)

        dh = None
        for a_tile, w_tile in pieces(act_refs, w_ref):
            term = _dot(a_tile, w_tile, NT)
            dh = term if dh is None else dh + term
        xv = x_ref[...]
        r = _rms(xv)
        xhat = xv * r
        gd = dh * g_ref[...]
        dx_ref[...] = dr_ref[...] + r * (gd - xhat * jnp.mean(gd * xhat, axis=-1, keepdims=True))
        dg_ref[...] += (dh * xhat).reshape(tm // SUBLANES, SUBLANES, d).sum(axis=0)

    row = pl.BlockSpec((tm, d), lambda i: (i, 0))
    return pl.pallas_call(
        body, name=name, grid=(t // tm,),
        in_specs=[*act_blocks(tm),
                  pl.BlockSpec((None, N_CHIPS, k, nq), lambda i: (layer, 0, 0, 0)),
                  row, pl.BlockSpec((1, d), lambda i: (0, 0)), row],
        out_specs=[row, pl.BlockSpec((SUBLANES, d), lambda i: (0, 0))],
        out_shape=[_sds((t, d), F32), _sds((SUBLANES, d), F32)],
        compiler_params=_params(("arbitrary",)))(*acts, w4, x, gain, dres)


def _dgrad_norm_ffn(name, dg, du, w4, layer, x, gain, dres):
    nq = w4.shape[3]
    f = dg.shape[1]

    def blocks(tm):
        return [pl.BlockSpec((tm, f), lambda i: (i, 0))] * 2

    def pieces(act_refs, w_ref):
        dg_ref, du_ref = act_refs
        return [(dg_ref[:, 0:nq], w_ref[0]), (dg_ref[:, nq:2 * nq], w_ref[1]),
                (du_ref[:, 0:nq], w_ref[2]), (du_ref[:, nq:2 * nq], w_ref[3])]

    return _dgrad_norm(name, [dg, du], blocks, pieces, w4, layer, x, gain, dres)


def _dgrad_norm_qkv(name, dqkv, w4, x, gain, dres):
    nq = w4.shape[3]

    def blocks(tm):
        return [pl.BlockSpec((tm, N_CHIPS * nq), lambda i: (i, 0))]

    def pieces(act_refs, w_ref):
        return [(act_refs[0][:, q * nq:(q + 1) * nq], w_ref[q]) for q in range(N_CHIPS)]

    return _dgrad_norm(name, [dqkv], blocks, pieces, w4, 0, x, gain, dres)


def _dgrad_norm_conv(name, d3, w4, x, gain, dres):
    _, _, d = d3.shape
    nq = w4.shape[3]
    per_part, per_q = d // MXU_COLS, nq // MXU_COLS

    def blocks(tm):
        return [pl.BlockSpec((3, tm, d), lambda i: (0, i, 0))]

    def pieces(act_refs, w_ref):
        out = []
        for jb in range(3 * per_part):
            ca, cw = (jb % per_part) * MXU_COLS, (jb % per_q) * MXU_COLS
            out.append((act_refs[0][jb // per_part, :, ca:ca + MXU_COLS], w_ref[jb // per_q, :, cw:cw + MXU_COLS]))
        return out

    return _dgrad_norm(name, [d3], blocks, pieces, w4, 0, x, gain, dres)


def _wgrad_up2(name, h, dg, du):
    t, k = h.shape
    nq = dg.shape[1] // 2
    tk = _pick(t, (512, 256, 128))
    steps = t // tk
    half = N_CHIPS // 2

    def body(h_ref, dg_ref, du_ref, o_ref):
        q = pl.program_id(0)

        @pl.when(pl.program_id(1) == 0)
        def _():
            o_ref[...] = jnp.zeros_like(o_ref)

        @pl.when(q < half)
        def _():
            o_ref[...] += _dot(h_ref[...], dg_ref[...], TN)

        @pl.when(q >= half)
        def _():
            o_ref[...] += _dot(h_ref[...], du_ref[...], TN)

    return pl.pallas_call(
        body, name=name, grid=(N_CHIPS, steps),
        in_specs=[pl.BlockSpec((tk, k), lambda q, s: (s, 0)),
                  pl.BlockSpec((tk, nq), lambda q, s: (jnp.where(q < half, s, steps - 1), jnp.minimum(q, half - 1))),
                  pl.BlockSpec((tk, nq), lambda q, s: (jnp.where(q >= half, s, 0), jnp.maximum(q - half, 0)))],
        out_specs=pl.BlockSpec((None, k, nq), lambda q, s: (q, 0, 0)),
        out_shape=_sds((N_CHIPS, k, nq), F32),
        compiler_params=_params(("parallel", "arbitrary")))(h, dg, du)


def _wgrad_joined(name, h, dy):
    t, k = h.shape
    nq = dy.shape[1] // N_CHIPS
    tk = _pick(t, (1024, 512, 256, 128))

    def body(h_ref, dy_ref, o_ref):
        @pl.when(pl.program_id(0) == 0)
        def _():
            o_ref[...] = jnp.zeros_like(o_ref)

        res = _dot(h_ref[...], dy_ref[...], TN)
        for q in range(N_CHIPS):
            o_ref[q] += res[:, q * nq:(q + 1) * nq]

    return pl.pallas_call(
        body, name=name, grid=(t // tk,),
        in_specs=[pl.BlockSpec((tk, k), lambda s: (s, 0)), pl.BlockSpec((tk, N_CHIPS * nq), lambda s: (s, 0))],
        out_specs=pl.BlockSpec((N_CHIPS, k, nq), lambda s: (0, 0, 0)),
        out_shape=_sds((N_CHIPS, k, nq), F32),
        compiler_params=_params(("arbitrary",)))(h, dy)


def _wgrad_conv_in(name, h, d3, nq):
    t, k = h.shape
    d = d3.shape[2]
    per_part, per_q = d // MXU_COLS, nq // MXU_COLS
    tk = _pick(t, (512, 256, 128))

    def body(h_ref, d_ref, o_ref):
        @pl.when(pl.program_id(0) == 0)
        def _():
            o_ref[...] = jnp.zeros_like(o_ref)

        hv = h_ref[...]
        for part in range(3):
            res = _dot(hv, d_ref[part], TN)
            for cc in range(per_part):
                jb = part * per_part + cc
                co = (jb % per_q) * MXU_COLS
                o_ref[jb // per_q, :, co:co + MXU_COLS] += res[:, cc * MXU_COLS:(cc + 1) * MXU_COLS]

    return pl.pallas_call(
        body, name=name, grid=(t // tk,),
        in_specs=[pl.BlockSpec((tk, k), lambda s: (s, 0)), pl.BlockSpec((3, tk, d), lambda s: (0, s, 0))],
        out_specs=pl.BlockSpec((N_CHIPS, k, nq), lambda s: (0, 0, 0)),
        out_shape=_sds((N_CHIPS, k, nq), F32),
        compiler_params=_params(("arbitrary",)))(h, d3)


def _wgrad_down(name, a, dx, tmw):
    t, kf = a.shape
    n = dx.shape[1]
    tk = _pick(t, (512, 256, 128))

    def body(a_ref, b_ref, o_ref):
        @pl.when(pl.program_id(1) == 0)
        def _():
            o_ref[...] = jnp.zeros_like(o_ref)

        o_ref[...] += _dot(a_ref[...], b_ref[...].astype(BF16), TN)

    g = pl.pallas_call(
        body, name=name, grid=(kf // tmw, t // tk),
        in_specs=[pl.BlockSpec((tk, tmw), lambda j, s: (s, j)), pl.BlockSpec((tk, n), lambda j, s: (s, 0))],
        out_specs=pl.BlockSpec((tmw, n), lambda j, s: (j, 0)),
        out_shape=_sds((kf, n), F32),
        compiler_params=_params(("parallel", "arbitrary")))(a, dx)
    return g.reshape(N_CHIPS, kf // N_CHIPS, n)


def _rms_fwd(name, x, gain):
    t, d = x.shape
    tm = _pick(t, (512, 256, 128))

    def body(x_ref, g_ref, h_ref):
        xv = x_ref[...]
        h_ref[...] = ((xv * _rms(xv)) * g_ref[...]).astype(BF16)

    return pl.pallas_call(
        body, name=name, grid=(t // tm,),
        in_specs=[pl.BlockSpec((tm, d), lambda i: (i, 0)), pl.BlockSpec((1, d), lambda i: (0, 0))],
        out_specs=pl.BlockSpec((tm, d), lambda i: (i, 0)),
        out_shape=_sds((t, d), BF16),
        compiler_params=_params(("parallel",)))(x, gain)


def _shift_rows(u, k, rows):
    s = u.shape[0]
    if k > 0:
        return jnp.where(rows >= k, pltpu.roll(u, k, 0), 0.0)
    return jnp.where(rows < s + k, pltpu.roll(u, s + k, 0), 0.0)


def _conv_fwd(bcx, cw, nseq, seq):
    t, d3 = bcx.shape
    d = d3 // 3
    cb = MXU_COLS
    nj = d // cb

    def body(b_ref, c_ref, x_ref, cw_ref, z_ref):
        u = b_ref[...].astype(F32) * x_ref[...].astype(F32)
        rows = lax.broadcasted_iota(jnp.int32, u.shape, 0)
        cwv = cw_ref[...]
        y = cwv[2:3] * u + cwv[1:2] * _shift_rows(u, 1, rows) + cwv[0:1] * _shift_rows(u, 2, rows)
        z_ref[...] = (c_ref[...].astype(F32) * y).astype(BF16)

    return pl.pallas_call(
        body, name="conv_fwd", grid=(nseq, nj),
        in_specs=[pl.BlockSpec((seq, cb), lambda b, j: (b, j)),
                  pl.BlockSpec((seq, cb), lambda b, j: (b, nj + j)),
                  pl.BlockSpec((seq, cb), lambda b, j: (b, 2 * nj + j)),
                  pl.BlockSpec((3, cb), lambda b, j: (0, j))],
        out_specs=pl.BlockSpec((seq, cb), lambda b, j: (b, j)),
        out_shape=_sds((t, d), BF16),
        compiler_params=_params(("parallel", "parallel")))(bcx, bcx, bcx, cw)


def _conv_bwd(dz, bcx, cw, nseq, seq):
    t, d3 = bcx.shape
    d = d3 // 3
    cb = MXU_COLS
    nj = d // cb

    def body(dz_ref, b_ref, c_ref, x_ref, cw_ref, o_ref, dcw_ref):
        @pl.when(pl.program_id(1) == 0)
        def _():
            dcw_ref[...] = jnp.zeros_like(dcw_ref)

        b = b_ref[...].astype(F32)
        c = c_ref[...].astype(F32)
        xv = x_ref[...].astype(F32)
        dzv = dz_ref[...].astype(F32)
        u = b * xv
        rows = lax.broadcasted_iota(jnp.int32, u.shape, 0)
        u1 = _shift_rows(u, 1, rows)
        u2 = _shift_rows(u, 2, rows)
        cwv = cw_ref[...]
        y = cwv[2:3] * u + cwv[1:2] * u1 + cwv[0:1] * u2
        dyc = dzv * c
        du = cwv[2:3] * dyc + cwv[1:2] * _shift_rows(dyc, -1, rows) + cwv[0:1] * _shift_rows(dyc, -2, rows)
        o_ref[0] = (du * xv).astype(BF16)
        o_ref[1] = (dzv * y).astype(BF16)
        o_ref[2] = (du * b).astype(BF16)
        s0 = jnp.sum(dyc * u2, axis=0, keepdims=True)
        s1 = jnp.sum(dyc * u1, axis=0, keepdims=True)
        s2 = jnp.sum(dyc * u, axis=0, keepdims=True)
        tap = lax.broadcasted_iota(jnp.int32, (3, cb), 0)
        dcw_ref[...] += jnp.where(tap == 0, s0, jnp.where(tap == 1, s1, s2))

    return pl.pallas_call(
        body, name="conv_bwd", grid=(nj, nseq),
        in_specs=[pl.BlockSpec((seq, cb), lambda j, b: (b, j)),
                  pl.BlockSpec((seq, cb), lambda j, b: (b, j)),
                  pl.BlockSpec((seq, cb), lambda j, b: (b, nj + j)),
                  pl.BlockSpec((seq, cb), lambda j, b: (b, 2 * nj + j)),
                  pl.BlockSpec((3, cb), lambda j, b: (0, j))],
        out_specs=[pl.BlockSpec((3, seq, cb), lambda j, b: (0, b, j)),
                   pl.BlockSpec((3, cb), lambda j, b: (0, j))],
        out_shape=[_sds((3, t, d), BF16), _sds((3, d), F32)],
        compiler_params=_params(("parallel", "arbitrary")))(dz, bcx, bcx, bcx, cw)


def _pair_norm(x, gain_pair, low):
    sq = x * x
    ss_lo = jnp.sum(jnp.where(low, sq, 0.0), axis=-1, keepdims=True)
    ss_hi = jnp.sum(jnp.where(low, 0.0, sq), axis=-1, keepdims=True)
    r = lax.rsqrt(jnp.where(low, ss_lo, ss_hi) * (1.0 / HEAD_DIM) + EPS)
    xhat = x * r
    return xhat * gain_pair, xhat, r


def _softmax_sink(qn_b, k_b, slope, sink, distf, mask):
    s = _dot(qn_b, k_b, NT) * (1.0 / (HEAD_DIM ** 0.5)) - slope * distf
    s = jnp.where(mask, s, -1e30)
    m = jnp.maximum(jnp.max(s, axis=-1, keepdims=True), sink)
    e = jnp.exp(s - m)
    es = jnp.exp(sink - m)
    inv = 1.0 / (jnp.sum(e, axis=-1, keepdims=True) + es)
    return e * inv, es * inv


def _kv_pairs(kv_tile, parity, low):
    own = jnp.where(low if parity == 0 else jnp.logical_not(low), kv_tile, 0.0)
    other = pltpu.roll(own, HEAD_DIM, 1)
    return (own, other) if parity == 0 else (other, own)


def _attn_geometry(n):
    q0 = pl.multiple_of(n * BLOCK, BLOCK)
    k0 = pl.multiple_of(jnp.maximum(n - 1, 0) * BLOCK, BLOCK)
    qi = lax.broadcasted_iota(jnp.int32, (BLOCK, 2 * BLOCK), 0)
    kj = lax.broadcasted_iota(jnp.int32, (BLOCK, 2 * BLOCK), 1)
    dist = (q0 - k0) + qi - kj
    mask = jnp.logical_and(dist >= 0, dist < WINDOW)
    return q0, k0, dist.astype(F32), mask


def _attn_fwd(qkv, qg_pair, kg_pair, sinks, nseq, seq):
    t = qkv.shape[0]
    dq = N_Q_HEADS * HEAD_DIM
    dkv = N_KV_HEADS * HEAD_DIM

    def body(sk_ref, qkv_ref, qg_ref, kg_ref, o_ref):
        low = lax.broadcasted_iota(jnp.int32, (1, LANES), 1) < HEAD_DIM
        qg = qg_ref[...]
        kg = kg_ref[...]

        def blk(n, carry):
            q0, k0, distf, mask = _attn_geometry(n)
            for kt in range(dkv // LANES):
                kraw = qkv_ref[pl.ds(k0, 2 * BLOCK), dq + kt * LANES:dq + (kt + 1) * LANES].astype(F32)
                vraw = qkv_ref[pl.ds(k0, 2 * BLOCK), dq + dkv + kt * LANES:dq + dkv + (kt + 1) * LANES].astype(F32)
                kn, _, _ = _pair_norm(kraw, kg, low)
                for par in range(2):
                    kh = 2 * kt + par
                    k_lo, k_hi = [v.astype(BF16) for v in _kv_pairs(kn, par, low)]
                    v_lo, v_hi = [v.astype(BF16) for v in _kv_pairs(vraw, par, low)]
                    for jj in range(2):
                        j = 2 * kh + jj
                        qraw = qkv_ref[pl.ds(q0, BLOCK), j * LANES:(j + 1) * LANES].astype(F32)
                        qn, _, _ = _pair_norm(qraw, qg, low)
                        qn_b = qn.astype(BF16)
                        p0, _ = _softmax_sink(qn_b, k_lo, ALIBI_SLOPES[2 * j], sk_ref[0, 2 * j], distf, mask)
                        p1, _ = _softmax_sink(qn_b, k_hi, ALIBI_SLOPES[2 * j + 1], sk_ref[0, 2 * j + 1], distf, mask)
                        o = _dot(p0.astype(BF16), v_lo, NN) + _dot(p1.astype(BF16), v_hi, NN)
                        o_ref[pl.ds(q0, BLOCK), j * LANES:(j + 1) * LANES] = o.astype(BF16)
            return carry

        lax.fori_loop(0, seq // BLOCK, blk, 0)

    return pl.pallas_call(
        body, name="attn_fwd", grid=(nseq,),
        in_specs=[pl.BlockSpec(memory_space=pltpu.SMEM),
                  pl.BlockSpec((seq, dq + 2 * dkv), lambda b: (b, 0)),
                  pl.BlockSpec((1, LANES), lambda b: (0, 0)),
                  pl.BlockSpec((1, LANES), lambda b: (0, 0))],
        out_specs=pl.BlockSpec((seq, dq), lambda b: (b, 0)),
        out_shape=_sds((t, dq), BF16),
        compiler_params=_params(("parallel",)))(sinks, qkv, qg_pair, kg_pair)


def _attn_bwd(do, qkv, qg_pair, kg_pair, sinks, nseq, seq):
    t = qkv.shape[0]
    dq = N_Q_HEADS * HEAD_DIM
    dkv = N_KV_HEADS * HEAD_DIM
    scale = 1.0 / (HEAD_DIM ** 0.5)

    def body(sk_ref, do_ref, qkv_ref, qg_ref, kg_ref, o_ref, dqg_ref, dkg_ref, dsk_ref, acc_ref):
        @pl.when(pl.program_id(0) == 0)
        def _():
            dqg_ref[...] = jnp.zeros_like(dqg_ref)
            dkg_ref[...] = jnp.zeros_like(dkg_ref)
            dsk_ref[...] = jnp.zeros_like(dsk_ref)

        acc_ref[...] = jnp.zeros_like(acc_ref)
        low = lax.broadcasted_iota(jnp.int32, (1, LANES), 1) < HEAD_DIM
        lane = lax.broadcasted_iota(jnp.int32, (1, LANES), 1)
        qg = qg_ref[...]
        kg = kg_ref[...]

        def blk(n, carry):
            dqg_acc, dkg_acc, dsk_acc = carry
            q0, k0, distf, mask = _attn_geometry(n)
            for kt in range(dkv // LANES):
                kraw = qkv_ref[pl.ds(k0, 2 * BLOCK), dq + kt * LANES:dq + (kt + 1) * LANES].astype(F32)
                vraw = qkv_ref[pl.ds(k0, 2 * BLOCK), dq + dkv + kt * LANES:dq + dkv + (kt + 1) * LANES].astype(F32)
                kn, khat, rk = _pair_norm(kraw, kg, low)
                dk_tile = None
                dv_tile = None
                for par in range(2):
                    kh = 2 * kt + par
                    own = low if par == 0 else jnp.logical_not(low)
                    k_lo, k_hi = [v.astype(BF16) for v in _kv_pairs(kn, par, low)]
                    v_lo, v_hi = [v.astype(BF16) for v in _kv_pairs(vraw, par, low)]
                    dkn_acc = jnp.zeros((2 * BLOCK, LANES), F32)
                    dv_acc = jnp.zeros((2 * BLOCK, LANES), F32)
                    for jj in range(2):
                        j = 2 * kh + jj
                        qraw = qkv_ref[pl.ds(q0, BLOCK), j * LANES:(j + 1) * LANES].astype(F32)
                        qn, qhat, rq = _pair_norm(qraw, qg, low)
                        qn_b = qn.astype(BF16)
                        do_b = do_ref[pl.ds(q0, BLOCK), j * LANES:(j + 1) * LANES]
                        dqn = None
                        dkn_pair = []
                        dv_pair = []
                        for e, (k_e, v_e) in enumerate(((k_lo, v_lo), (k_hi, v_hi))):
                            h = 2 * j + e
                            p, ps = _softmax_sink(qn_b, k_e, ALIBI_SLOPES[h], sk_ref[0, h], distf, mask)
                            dp = _dot(do_b, v_e, NT)
                            dsum = jnp.sum(p * dp, axis=-1, keepdims=True)
                            ds_b = ((p * (dp - dsum)) * scale).astype(BF16)
                            dsk_acc = dsk_acc - jnp.where(lane == h, jnp.sum(ps * dsum, axis=0, keepdims=True), 0.0)
                            term = _dot(ds_b, k_e, NN)
                            dqn = term if dqn is None else dqn + term
                            dkn_pair.append(_dot(ds_b, qn_b, TN))
                            dv_pair.append(_dot(p.astype(BF16), do_b, TN))
                        dkn_acc = dkn_acc + jnp.where(low, dkn_pair[0], dkn_pair[1])
                        dv_acc = dv_acc + jnp.where(low, dv_pair[0], dv_pair[1])
                        dqg_acc = dqg_acc + jnp.sum(dqn * qhat, axis=0, keepdims=True)
                        dqhat = dqn * qg
                        prod = dqhat * qhat
                        m_lo = jnp.sum(jnp.where(low, prod, 0.0), axis=-1, keepdims=True)
                        m_hi = jnp.sum(jnp.where(low, 0.0, prod), axis=-1, keepdims=True)
                        mean = jnp.where(low, m_lo, m_hi) * (1.0 / HEAD_DIM)
                        o_ref[pl.ds(q0, BLOCK), j * LANES:(j + 1) * LANES] = (rq * (dqhat - qhat * mean)).astype(BF16)
                    dkn = dkn_acc + pltpu.roll(dkn_acc, HEAD_DIM, 1)
                    dvh = dv_acc + pltpu.roll(dv_acc, HEAD_DIM, 1)
                    khat_own = jnp.where(own, khat, 0.0)
                    khat_dup = khat_own + pltpu.roll(khat_own, HEAD_DIM, 1)
                    dkg_acc = dkg_acc + jnp.sum(jnp.where(own, dkn * khat_dup, 0.0), axis=0, keepdims=True)
                    dkhat = dkn * kg
                    mean_k = jnp.sum(dkhat * khat_dup, axis=-1, keepdims=True) * (1.0 / LANES)
                    dk_raw = rk * (dkhat - khat_dup * mean_k)
                    dk_tile = jnp.where(own, dk_raw, 0.0) if dk_tile is None else jnp.where(own, dk_raw, dk_tile)
                    dv_tile = jnp.where(own, dvh, 0.0) if dv_tile is None else jnp.where(own, dvh, dv_tile)
                acc_ref[pl.ds(k0, 2 * BLOCK), kt * LANES:(kt + 1) * LANES] += dk_tile
                acc_ref[pl.ds(k0, 2 * BLOCK), dkv + kt * LANES:dkv + (kt + 1) * LANES] += dv_tile
            return dqg_acc, dkg_acc, dsk_acc

        zero = jnp.zeros((1, LANES), F32)
        dqg_acc, dkg_acc, dsk_acc = lax.fori_loop(0, seq // BLOCK, blk, (zero, zero, zero))
        dqg_ref[...] += dqg_acc
        dkg_ref[...] += dkg_acc
        dsk_ref[...] += dsk_acc
        o_ref[:, dq:dq + 2 * dkv] = acc_ref[...].astype(BF16)

    small = pl.BlockSpec((1, LANES), lambda b: (0, 0))
    return pl.pallas_call(
        body, name="attn_bwd", grid=(nseq,),
        in_specs=[pl.BlockSpec(memory_space=pltpu.SMEM),
                  pl.BlockSpec((seq, dq), lambda b: (b, 0)),
                  pl.BlockSpec((seq, dq + 2 * dkv), lambda b: (b, 0)),
                  small, small],
        out_specs=[pl.BlockSpec((seq, dq + 2 * dkv), lambda b: (b, 0)), small, small, small],
        out_shape=[_sds((t, dq + 2 * dkv), BF16), _sds((1, LANES), F32), _sds((1, LANES), F32),
                   _sds((1, LANES), F32)],
        scratch_shapes=[pltpu.VMEM((seq, 2 * dkv), F32)],
        compiler_params=_params(("arbitrary",)))(sinks, do, qkv, qg_pair, kg_pair)


def _place():
    x, y, c = lax.axis_index("x"), lax.axis_index("y"), lax.axis_index("c")
    other_chips = [(1 - x, y), (x, 1 - y), (1 - x, 1 - y)]
    return x, y, c, other_chips


def _half_rows(c, rows):
    rh = rows // 2
    return pl.ds(pl.multiple_of(c * rh, BF16_ROWS), rh)


def _any_specs(n):
    return [pl.BlockSpec(memory_space=pl.ANY)] * n


def _cast_own(name, w, place):
    nl, r, cdim = w.shape
    rt = _pick(r, (256, 128, 64, 32))

    def body(s_ref, w_ref, o_ref):
        o_ref[...] = w_ref[...].astype(BF16)

    grid_spec = pltpu.PrefetchScalarGridSpec(
        num_scalar_prefetch=1, grid=(nl, r // rt),
        in_specs=[pl.BlockSpec((None, rt, cdim), lambda l, i, s: (l, i, 0))],
        out_specs=pl.BlockSpec((None, None, rt, cdim), lambda l, i, s: (l, s[1], i, 0)))
    return pl.pallas_call(
        body, name=name, grid_spec=grid_spec, out_shape=_sds((nl, N_CHIPS, r, cdim), BF16),
        compiler_params=_params(("parallel", "parallel")))(place, w)


def _allgather_weights(bufs):
    n = len(bufs)
    shapes = [b.shape for b in bufs]

    def body(*refs):
        outs = refs[n:2 * n]
        send_sems, recv_sems = refs[2 * n:]
        x, y, c, other_chips = _place()
        me_chip = 2 * x + y
        sibling = (x, y, 1 - c)

        def rows(u, chip, half):
            return outs[u].at[:, chip, _half_rows(half, shapes[u][2]), :]

        def copy(sem, part, to):
            return pltpu.make_async_remote_copy(src_ref=part, dst_ref=part, send_sem=send_sems.at[sem],
                                                recv_sem=recv_sems.at[sem], device_id=to, device_id_type=MESH)

        sends = []
        for u in range(n):
            for k, chip in enumerate(other_chips):
                cp = copy(6 * u + k, rows(u, me_chip, c), (*chip, c))
                cp.start()
                sends.append(cp)
        for u in range(n):
            for k, chip in enumerate(other_chips):
                got = rows(u, 2 * chip[0] + chip[1], c)
                copy(6 * u + k, got, (*chip, c)).wait_recv()
                cp = copy(6 * u + 3 + k, got, sibling)
                cp.start()
                sends.append(cp)
        for u in range(n):
            for k, chip in enumerate(other_chips):
                copy(6 * u + 3 + k, rows(u, 2 * chip[0] + chip[1], 1 - c), sibling).wait_recv()
        for cp in sends:
            cp.wait_send()

    return pl.pallas_call(
        body, name="allgather_weights",
        in_specs=_any_specs(n), out_specs=_any_specs(n),
        out_shape=[_sds(s, BF16) for s in shapes],
        input_output_aliases={u: u for u in range(n)},
        scratch_shapes=[pltpu.SemaphoreType.DMA((6 * n,)), pltpu.SemaphoreType.DMA((6 * n,))],
    )(*bufs)


def _exchange_halves(grads):
    n = len(grads)
    shapes = [g.shape for g in grads]

    def body(*refs):
        gs, outs = refs[:n], refs[n:2 * n]
        send_sems, recv_sems = refs[2 * n:]
        x, y, c, _ = _place()
        sends = []
        for u in range(n):
            cp = pltpu.make_async_remote_copy(
                src_ref=gs[u].at[:, _half_rows(1 - c, shapes[u][1]), :], dst_ref=outs[u],
                send_sem=send_sems.at[u], recv_sem=recv_sems.at[u], device_id=(x, y, 1 - c), device_id_type=MESH)
            cp.start()
            sends.append(cp)
        for cp in sends:
            cp.wait_recv()
        for cp in sends:
            cp.wait_send()

    return pl.pallas_call(
        body, name="exchange_halves",
        in_specs=_any_specs(n), out_specs=_any_specs(n),
        out_shape=[_sds((s[0], s[1] // 2, s[2]), F32) for s in shapes],
        scratch_shapes=[pltpu.SemaphoreType.DMA((n,)), pltpu.SemaphoreType.DMA((n,))],
    )(*grads)


def _sum_halves(name, g, got, place):
    _, r, cdim = g.shape
    rh = r // 2
    rt = _pick(rh, (128, 64, 32, 16))
    nr = rh // rt

    def body(s_ref, g_ref, got_ref, pb_ref, pf_ref):
        s = g_ref[...] + got_ref[...]
        pb_ref[...] = s.astype(BF16)

        @pl.when(pl.program_id(1) == s_ref[1])
        def _():
            pf_ref[...] = s

    grid_spec = pltpu.PrefetchScalarGridSpec(
        num_scalar_prefetch=1, grid=(nr, N_CHIPS),
        in_specs=[pl.BlockSpec((None, rt, cdim), lambda i, q, s: (q, s[0] * nr + i, 0)),
                  pl.BlockSpec((None, rt, cdim), lambda i, q, s: (q, i, 0))],
        out_specs=[pl.BlockSpec((None, rt, cdim), lambda i, q, s: (q, i, 0)),
                   pl.BlockSpec((rt, cdim), lambda i, q, s: (i, 0))])
    return pl.pallas_call(
        body, name=name, grid_spec=grid_spec,
        out_shape=[_sds((N_CHIPS, rh, cdim), BF16), _sds((rh, cdim), F32)],
        compiler_params=_params(("parallel", "arbitrary")))(place, g, got)


def _scatter_partials(partials):
    n = len(partials)
    shapes = [p.shape for p in partials]

    def body(*refs):
        ps, outs = refs[:n], refs[n:2 * n]
        send_sems, recv_sems = refs[2 * n:]
        x, y, c, other_chips = _place()
        sends = []
        for u in range(n):
            for k, chip in enumerate(other_chips):
                cp = pltpu.make_async_remote_copy(
                    src_ref=ps[u].at[2 * chip[0] + chip[1]], dst_ref=outs[u].at[k],
                    send_sem=send_sems.at[3 * u + k], recv_sem=recv_sems.at[3 * u + k],
                    device_id=(*chip, c), device_id_type=MESH)
                cp.start()
                sends.append(cp)
        for cp in sends:
            cp.wait_recv()
        for cp in sends:
            cp.wait_send()

    return pl.pallas_call(
        body, name="scatter_partials",
        in_specs=_any_specs(n), out_specs=_any_specs(n),
        out_shape=[_sds((3, s[1], s[2]), BF16) for s in shapes],
        scratch_shapes=[pltpu.SemaphoreType.DMA((3 * n,)), pltpu.SemaphoreType.DMA((3 * n,))],
    )(*partials)


def _sum_partials(name, own, got, place, layer, nl, prev):
    rh, cdim = own.shape
    rt = _pick(rh, (128, 64, 32, 16))
    nr = rh // rt

    def body(s_ref, own_ref, got_ref, *rest):
        o_ref = rest[-1]
        o_ref[...] = ((own_ref[...] + got_ref[0].astype(F32)) + got_ref[1].astype(F32)) + got_ref[2].astype(F32)

    in_specs = [pl.BlockSpec((rt, cdim), lambda i, s: (i, 0)), pl.BlockSpec((3, rt, cdim), lambda i, s: (0, i, 0))]
    args = [place, own, got]
    aliases = {}
    if prev is not None:
        in_specs.append(pl.BlockSpec(memory_space=pl.ANY))
        args.append(prev)
        aliases = {3: 0}
    grid_spec = pltpu.PrefetchScalarGridSpec(
        num_scalar_prefetch=1, grid=(nr,), in_specs=in_specs,
        out_specs=pl.BlockSpec((None, rt, cdim), lambda i, s: (layer, s[0] * nr + i, 0)))
    return pl.pallas_call(
        body, name=name, grid_spec=grid_spec, out_shape=_sds((nl, 2 * rh, cdim), F32),
        input_output_aliases=aliases, compiler_params=_params(("parallel",)))(*args)


def _share_halves(bufs):
    n = len(bufs)
    shapes = [b.shape for b in bufs]
    units = [(w, l) for w in range(n) for l in range(shapes[w][0])]

    def body(*refs):
        outs = refs[n:2 * n]
        send_sems, recv_sems = refs[2 * n:]
        x, y, c, _ = _place()
        sends = []
        for u, (w, l) in enumerate(units):
            mine = outs[w].at[l, _half_rows(c, shapes[w][1]), :]
            cp = pltpu.make_async_remote_copy(src_ref=mine, dst_ref=mine, send_sem=send_sems.at[u],
                                              recv_sem=recv_sems.at[u], device_id=(x, y, 1 - c), device_id_type=MESH)
            cp.start()
            sends.append(cp)
        for u, (w, l) in enumerate(units):
            theirs = outs[w].at[l, _half_rows(1 - c, shapes[w][1]), :]
            pltpu.make_async_remote_copy(src_ref=theirs, dst_ref=theirs, send_sem=send_sems.at[u],
                                         recv_sem=recv_sems.at[u], device_id=(x, y, 1 - c),
                                         device_id_type=MESH).wait_recv()
        for cp in sends:
            cp.wait_send()

    return pl.pallas_call(
        body, name="share_halves",
        in_specs=_any_specs(n), out_specs=_any_specs(n),
        out_shape=[_sds(s, F32) for s in shapes],
        input_output_aliases={u: u for u in range(n)},
        scratch_shapes=[pltpu.SemaphoreType.DMA((len(units),)), pltpu.SemaphoreType.DMA((len(units),))],
    )(*bufs)


def _gather_blocks(block_ref, all_ref, send_sems, recv_sems):
    x, y, c, _ = _place()
    me = 4 * x + 2 * y + c
    all_ref[me] = block_ref[...]
    sends = []
    for rel in range(1, 8):
        fx, fy, fc = (rel >> 2) & 1, (rel >> 1) & 1, rel & 1
        peer = (x ^ fx, y ^ fy, c ^ fc)
        cp = pltpu.make_async_remote_copy(src_ref=block_ref, dst_ref=all_ref.at[me], send_sem=send_sems.at[rel - 1],
                                          recv_sem=recv_sems.at[rel - 1], device_id=peer, device_id_type=MESH)
        cp.start()
        sends.append(cp)
    for cp in sends:
        cp.wait_recv()
    for cp in sends:
        cp.wait_send()


def _gather_conv_w(cw_block):
    r, d = cw_block.shape

    def body(b_ref, o_ref, all_ref, send_sems, recv_sems):
        _gather_blocks(b_ref, all_ref, send_sems, recv_sems)
        o_ref[...] = (all_ref[0] + all_ref[2]) + (all_ref[4] + all_ref[6])

    vm = pl.BlockSpec(memory_space=pltpu.VMEM)
    return pl.pallas_call(
        body, name="gather_conv_w", in_specs=[vm], out_specs=vm, out_shape=_sds((r, d), F32),
        scratch_shapes=[pltpu.VMEM((8, r, d), F32), pltpu.SemaphoreType.DMA((7,)), pltpu.SemaphoreType.DMA((7,))],
    )(cw_block)


def _adam(w, g, m, v):
    m_new = ADAM_B1 * m + (1.0 - ADAM_B1) * g
    v_new = ADAM_B2 * v + (1.0 - ADAM_B2) * (g * g)
    m_hat = m_new / (1.0 - ADAM_B1 ** ADAM_STEP)
    v_hat = v_new / (1.0 - ADAM_B2 ** ADAM_STEP)
    delta = -ADAM_LR * (m_hat / (jnp.sqrt(v_hat) + ADAM_EPS) + ADAM_WD * w)
    return delta, m_new, v_new


def _small_step(dnm0, dnm1, dnf0, dnf1, dcw, dqg, dkg, dsk, loss, w_blk, m_blk, v_blk):
    d = w_blk.shape[1]

    def body(dnm0_ref, dnm1_ref, dnf0_ref, dnf1_ref, dcw_ref, dqg_ref, dkg_ref, dsk_ref, loss_ref,
             w_ref, m_ref, v_ref, g_ref, dl_ref, mo_ref, vo_ref, blk_ref, all_ref, send_sems, recv_sems):
        blk_ref[...] = jnp.zeros_like(blk_ref)
        blk_ref[0:1, :] = jnp.sum(dnm0_ref[...], axis=0, keepdims=True)
        blk_ref[1:2, :] = jnp.sum(dnm1_ref[...], axis=0, keepdims=True)
        blk_ref[8:9, :] = jnp.sum(dnf0_ref[...], axis=0, keepdims=True)
        blk_ref[9:10, :] = jnp.sum(dnf1_ref[...], axis=0, keepdims=True)
        blk_ref[16:19, :] = dcw_ref[...]
        dqg_v = dqg_ref[...]
        dkg_v = dkg_ref[...]
        blk_ref[24:25, 0:LANES] = dqg_v + pltpu.roll(dqg_v, HEAD_DIM, 1)
        blk_ref[24:25, LANES:2 * LANES] = dkg_v + pltpu.roll(dkg_v, HEAD_DIM, 1)
        blk_ref[24:25, 2 * LANES:3 * LANES] = dsk_ref[...]
        blk_ref[24:25, 3 * LANES:4 * LANES] = jnp.broadcast_to(loss_ref[...], (1, LANES))
        _gather_blocks(blk_ref, all_ref, send_sems, recv_sems)
        g = all_ref[0]
        for dev in range(1, 8):
            g = g + all_ref[dev]
        g_ref[...] = g
        delta, m_new, v_new = _adam(w_ref[...], g, m_ref[...], v_ref[...])
        dl_ref[...] = delta
        mo_ref[...] = m_new
        vo_ref[...] = v_new

    vm = pl.BlockSpec(memory_space=pltpu.VMEM)
    blk = _sds((SMALL_ROWS, d), F32)
    return pl.pallas_call(
        body, name="small_step", in_specs=[vm] * 12, out_specs=[vm] * 4, out_shape=[blk] * 4,
        scratch_shapes=[pltpu.VMEM((SMALL_ROWS, d), F32), pltpu.VMEM((8, SMALL_ROWS, d), F32),
                        pltpu.SemaphoreType.DMA((7,)), pltpu.SemaphoreType.DMA((7,))],
    )(dnm0, dnm1, dnf0, dnf1, dcw, dqg, dkg, dsk, loss, w_blk, m_blk, v_blk)


def _adam_step(name, w, g, m, v):
    nl, r, cdim = w.shape
    rt = _pick(r, (128, 64, 32))

    def body(w_ref, g_ref, m_ref, v_ref, d_ref, mo_ref, vo_ref):
        delta, m_new, v_new = _adam(w_ref[...], g_ref[...], m_ref[...], v_ref[...])
        d_ref[...] = delta
        mo_ref[...] = m_new
        vo_ref[...] = v_new

    spec = pl.BlockSpec((None, rt, cdim), lambda l, i: (l, i, 0))
    return pl.pallas_call(
        body, name=name, grid=(nl, r // rt), in_specs=[spec] * 4, out_specs=[spec] * 3,
        out_shape=[_sds(w.shape, F32)] * 3,
        compiler_params=_params(("parallel", "parallel")))(w, g, m, v)


def _pad_rows(a, rows=SUBLANES):
    return jnp.pad(a, ((0, rows - a.shape[0]), (0, 0)))


def _small_block(nm, nf, cw_local, qg, kg, sk, chip):
    d = nm.shape[1]
    cw_rows = lax.dynamic_update_slice(jnp.zeros((SUBLANES, d), F32), cw_local, (0, chip * cw_local.shape[1]))
    misc = jnp.concatenate([qg, qg, kg, kg, jnp.pad(sk, ((0, 0), (0, LANES - sk.shape[1]))),
                            jnp.zeros((1, d - 3 * LANES), F32)], axis=1)
    return jnp.concatenate([_pad_rows(nm), _pad_rows(nf), cw_rows, _pad_rows(misc)], axis=0)


def _unpack_small(blk, chip, cw_cols):
    cw = lax.dynamic_slice(blk[16:19], (0, chip * cw_cols), (3, cw_cols))[None]
    return dict(norm_mixer=blk[0:2], norm_ffn=blk[8:10], conv_w=cw, attn_q_gain=blk[24:25, 0:HEAD_DIM],
                attn_k_gain=blk[24:25, LANES:LANES + HEAD_DIM], attn_sinks=blk[24:25, 2 * LANES:2 * LANES + N_Q_HEADS])


WEIGHT_NAMES = ("conv_w_in", "conv_w", "conv_w_out", "attn_w_qkv", "attn_q_gain", "attn_k_gain", "attn_sinks",
                "attn_w_o", "norm_mixer", "norm_ffn", "ffn_w_gate_up", "ffn_w_down")
BIG = ("conv_w_in", "conv_w_out", "attn_w_qkv", "attn_w_o", "ffn_w_gate_up", "ffn_w_down")


def kernel(x, conv_w_in, conv_w, conv_w_out, attn_w_qkv, attn_q_gain, attn_k_gain, attn_sinks, attn_w_o, norm_mixer, norm_ffn, ffn_w_gate_up, ffn_w_down, loss_target, m_conv_w_in, m_conv_w, m_conv_w_out, m_attn_w_qkv, m_attn_q_gain, m_attn_k_gain, m_attn_sinks, m_attn_w_o, m_norm_mixer, m_norm_ffn, m_ffn_w_gate_up, m_ffn_w_down, v_conv_w_in, v_conv_w, v_conv_w_out, v_attn_w_qkv, v_attn_q_gain, v_attn_k_gain, v_attn_sinks, v_attn_w_o, v_norm_mixer, v_norm_ffn, v_ffn_w_gate_up, v_ffn_w_down):
    w = dict(conv_w_in=conv_w_in, conv_w=conv_w, conv_w_out=conv_w_out, attn_w_qkv=attn_w_qkv,
             attn_q_gain=attn_q_gain, attn_k_gain=attn_k_gain, attn_sinks=attn_sinks, attn_w_o=attn_w_o,
             norm_mixer=norm_mixer, norm_ffn=norm_ffn, ffn_w_gate_up=ffn_w_gate_up, ffn_w_down=ffn_w_down)
    m = dict(conv_w_in=m_conv_w_in, conv_w=m_conv_w, conv_w_out=m_conv_w_out, attn_w_qkv=m_attn_w_qkv,
             attn_q_gain=m_attn_q_gain, attn_k_gain=m_attn_k_gain, attn_sinks=m_attn_sinks, attn_w_o=m_attn_w_o,
             norm_mixer=m_norm_mixer, norm_ffn=m_norm_ffn, ffn_w_gate_up=m_ffn_w_gate_up, ffn_w_down=m_ffn_w_down)
    v = dict(conv_w_in=v_conv_w_in, conv_w=v_conv_w, conv_w_out=v_conv_w_out, attn_w_qkv=v_attn_w_qkv,
             attn_q_gain=v_attn_q_gain, attn_k_gain=v_attn_k_gain, attn_sinks=v_attn_sinks, attn_w_o=v_attn_w_o,
             norm_mixer=v_norm_mixer, norm_ffn=v_norm_ffn, ffn_w_gate_up=v_ffn_w_gate_up, ffn_w_down=v_ffn_w_down)

    nseq, seq, d = x.shape
    t = nseq * seq
    chip = 2 * lax.axis_index("x") + lax.axis_index("y")
    core = lax.axis_index("c")
    place = jnp.stack([core, chip]).astype(jnp.int32)
    x0 = x.reshape(t, d)
    tgt = loss_target.reshape(t, d)

    cw_block = lax.dynamic_update_slice(jnp.zeros((SUBLANES, d), F32), conv_w[0], (0, chip * conv_w.shape[2]))
    cw_full = _gather_conv_w(cw_block)[0:3]
    w_in, w_out, w_qkv, w_o, w_gu, w_dn = _allgather_weights([_cast_own(f"cast_{k}", w[k], place) for k in BIG])
    w_out = w_out.reshape(1, d, d)
    w_o = w_o.reshape(1, d, d)
    w_dn = w_dn.reshape(w_dn.shape[0], D_FF, d)

    qg_pair = jnp.concatenate([attn_q_gain, attn_q_gain], axis=1)
    kg_pair = jnp.concatenate([attn_k_gain, attn_k_gain], axis=1)

    def ffn_bwd(i, dxo, xin, h, g, u, a):
        g_dn = _wgrad_down(f"ffn{i}_down_wgrad", a, dxo, D_FF // 2)
        dg, du = _mm_down_t_swiglu(f"ffn{i}_down_dgrad", dxo, w_dn, i, g, u)
        g_gu = _wgrad_up2(f"ffn{i}_up_wgrad", h, dg, du)
        dxi, dgain = _dgrad_norm_ffn(f"ffn{i}_up_dgrad", dg, du, w_gu, i, xin, norm_ffn[i:i + 1], dxo)
        return dxi, dgain, g_gu, g_dn

    h0 = _rms_fwd("conv_norm", x0, norm_mixer[0:1])
    bcx = _mm_up_joined("conv_in", h0, w_in, 512)
    z = _conv_fwd(bcx, cw_full, nseq, seq)
    x1, h1 = _mm_down_norm("conv_out", z, w_out, 0, x0, norm_ffn[0:1])
    g0, u0, a0 = _mm_up_swiglu("ffn0_up", h1, w_gu, 0)
    x2, h2 = _mm_down_norm("ffn0_down", a0, w_dn, 0, x1, norm_mixer[1:2])
    qkv = _mm_up_joined("attn_qkv", h2, w_qkv, 1024)
    o = _attn_fwd(qkv, qg_pair, kg_pair, attn_sinks, nseq, seq)
    x3, h3 = _mm_down_norm("attn_out", o, w_o, 0, x2, norm_ffn[1:2])
    g1, u1, a1 = _mm_up_swiglu("ffn1_up", h3, w_gu, 1)
    dy, loss_part = _mm_down_loss("ffn1_down", a1, w_dn, 1, x3, tgt)

    dx3, dnf1, g_gu1, g_dn1 = ffn_bwd(1, dy, x3, h3, g1, u1, a1)
    g_o = _wgrad_down("attn_out_wgrad", o, dx3, d)
    do = _mm_down_t("attn_out_dgrad", dx3, w_o, 0)
    dqkv, dqg, dkg, dsk = _attn_bwd(do, qkv, qg_pair, kg_pair, attn_sinks, nseq, seq)
    g_qkv = _wgrad_joined("attn_qkv_wgrad", h2, dqkv)
    dx2, dnm1 = _dgrad_norm_qkv("attn_qkv_dgrad", dqkv, w_qkv, x2, norm_mixer[1:2], dx3)
    dx1, dnf0, g_gu0, g_dn0 = ffn_bwd(0, dx2, x1, h1, g0, u0, a0)
    g_out = _wgrad_down("conv_out_wgrad", z, dx1, d)
    dz = _mm_down_t("conv_out_dgrad", dx1, w_out, 0)
    dbcx, dcw = _conv_bwd(dz, bcx, cw_full, nseq, seq)
    g_in = _wgrad_conv_in("conv_in_wgrad", h0, dbcx, conv_w_in.shape[2])
    dx0, dnm0 = _dgrad_norm_conv("conv_in_dgrad", dbcx, w_in, x0, norm_mixer[0:1], dx1)

    units = [g_in, g_out, g_qkv, g_o, g_gu0, g_gu1, g_dn0, g_dn1]
    unit_names = ["in", "out", "qkv", "o", "gu0", "gu1", "dn0", "dn1"]
    layers = [(0, 0), (1, 0), (2, 0), (3, 0), (4, 0), (4, 1), (5, 0), (5, 1)]
    got = _exchange_halves(units)
    chip_sums = [_sum_halves(f"sum_halves_{nm}", g, r, place) for nm, g, r in zip(unit_names, units, got)]
    arrived = _scatter_partials([pb for pb, _ in chip_sums])
    finished = [None] * len(BIG)
    for nm, (wi, l), (_, pf), r in zip(unit_names, layers, chip_sums, arrived):
        finished[wi] = _sum_partials(f"sum_partials_{nm}", pf, r, place, l, w[BIG[wi]].shape[0], finished[wi])
    grads_big = _share_halves(finished)

    grad, delta, new_m, new_v = {}, {}, {}, {}
    for k, g in zip(BIG, grads_big):
        grad[k] = g
        delta[k], new_m[k], new_v[k] = _adam_step(f"adam_{k}", w[k], g, m[k], v[k])

    def blocks(src):
        return _small_block(src["norm_mixer"], src["norm_ffn"], src["conv_w"][0], src["attn_q_gain"],
                            src["attn_k_gain"], src["attn_sinks"], chip)

    g_blk, d_blk, m_blk, v_blk = _small_step(dnm0, dnm1, dnf0, dnf1, dcw, dqg, dkg, dsk, loss_part,
                                             blocks(w), blocks(m), blocks(v))
    cw_cols = conv_w.shape[2]
    for dst, blk in ((grad, g_blk), (delta, d_blk), (new_m, m_blk), (new_v, v_blk)):
        dst.update(_unpack_small(blk, chip, cw_cols))
    loss = g_blk[24, 3 * LANES]

    return (loss, dx0.reshape(nseq, seq, d), *[grad[k] for k in WEIGHT_NAMES], *[delta[k] for k in WEIGHT_NAMES],
            *[new_m[k] for k in WEIGHT_NAMES], *[new_v[k] for k in WEIGHT_NAMES])
```

```python
import jax
import jax.numpy as jnp
from jax import lax
from jax.experimental import pallas as pl
from jax.experimental.pallas import tpu as pltpu
from jax.experimental.pallas import tpu_sc as plsc

F32 = jnp.float32
BF16 = jnp.bfloat16

D_MODEL = 1024
D_FF = 2816
N_Q_HEADS = 16
N_KV_HEADS = 4
HEAD_DIM = 64
WINDOW = 128
BLOCK = 128
EPS = 1e-6
N_CHIPS = 4
LANES = 128
SUBLANES = 8
BF16_ROWS = 16
MXU_COLS = 256
VMEM_LIMIT = 48 * 1024 * 1024
ADAM_LR, ADAM_B1, ADAM_B2, ADAM_EPS, ADAM_WD, ADAM_STEP = 0.001, 0.9, 0.999, 1e-08, 0.01, 10
ALIBI_SLOPES = tuple(2.0 ** (-8.0 * (h + 1) / N_Q_HEADS) for h in range(N_Q_HEADS))
SMALL_ROWS = 32
MESH = pl.DeviceIdType.MESH

NN = ((1,), (0,))
NT = ((1,), (1,))
TN = ((0,), (0,))


def _dot(a, b, dims):
    return lax.dot_general(a, b, (dims, ((), ())), preferred_element_type=F32)


def _pick(n, cands):
    for c in cands:
        if n % c == 0:
            return c
    raise ValueError((n, cands))


def _params(sem):
    return pltpu.CompilerParams(dimension_semantics=sem, vmem_limit_bytes=VMEM_LIMIT)


def _sds(shape, dtype):
    return jax.ShapeDtypeStruct(shape, dtype)


def _rms(xv):
    return lax.rsqrt(jnp.mean(xv * xv, axis=-1, keepdims=True) + EPS)


def _sigmoid(g):
    return 1.0 / (1.0 + jnp.exp(-g))


def _mm_up_joined(name, a, w4, tm_pref):
    t, k = a.shape
    _, _, _, nq = w4.shape
    tm = _pick(t, (tm_pref, 256, 128))

    def body(a_ref, w_ref, o_ref, wcat_ref):
        @pl.when(pl.program_id(0) == 0)
        def _():
            for q in range(N_CHIPS):
                wcat_ref[:, q * nq:(q + 1) * nq] = w_ref[q]

        o_ref[...] = _dot(a_ref[...], wcat_ref[...], NN).astype(BF16)

    return pl.pallas_call(
        body, name=name, grid=(t // tm,),
        in_specs=[pl.BlockSpec((tm, k), lambda i: (i, 0)),
                  pl.BlockSpec((None, N_CHIPS, k, nq), lambda i: (0, 0, 0, 0))],
        out_specs=pl.BlockSpec((tm, N_CHIPS * nq), lambda i: (i, 0)),
        out_shape=_sds((t, N_CHIPS * nq), BF16),
        scratch_shapes=[pltpu.VMEM((k, N_CHIPS * nq), BF16)],
        compiler_params=_params(("arbitrary",)))(a, w4)


def _mm_up_swiglu(name, h, w4, layer):
    t, k = h.shape
    _, _, _, nq = w4.shape
    tm = _pick(t, (512, 256, 128))

    def body(h_ref, wg_ref, wu_ref, g_ref, u_ref, a_ref):
        hv = h_ref[...]
        g = _dot(hv, wg_ref[...], NN)
        u = _dot(hv, wu_ref[...], NN)
        g_ref[...] = g.astype(BF16)
        u_ref[...] = u.astype(BF16)
        a_ref[...] = (g * _sigmoid(g) * u).astype(BF16)

    half = N_CHIPS // 2
    out = pl.BlockSpec((tm, nq), lambda j, i: (i, j))
    return pl.pallas_call(
        body, name=name, grid=(half, t // tm),
        in_specs=[pl.BlockSpec((tm, k), lambda j, i: (i, 0)),
                  pl.BlockSpec((None, None, k, nq), lambda j, i: (layer, j, 0, 0)),
                  pl.BlockSpec((None, None, k, nq), lambda j, i: (layer, half + j, 0, 0))],
        out_specs=[out, out, out],
        out_shape=[_sds((t, half * nq), BF16)] * 3,
        compiler_params=_params(("parallel", "parallel")))(h, w4, w4)


def _mm_down_norm(name, a, w, layer, res, gain):
    t, kf = a.shape
    _, _, n = w.shape
    tm = _pick(t, (512, 256, 128))

    def body(a_ref, w_ref, r_ref, g_ref, o_ref, h_ref):
        xo = r_ref[...] + _dot(a_ref[...], w_ref[...], NN)
        o_ref[...] = xo
        h_ref[...] = ((xo * _rms(xo)) * g_ref[...]).astype(BF16)

    row = pl.BlockSpec((tm, n), lambda i: (i, 0))
    return pl.pallas_call(
        body, name=name, grid=(t // tm,),
        in_specs=[pl.BlockSpec((tm, kf), lambda i: (i, 0)),
                  pl.BlockSpec((None, kf, n), lambda i: (layer, 0, 0)),
                  row, pl.BlockSpec((1, n), lambda i: (0, 0))],
        out_specs=[row, row],
        out_shape=[_sds((t, n), F32), _sds((t, n), BF16)],
        compiler_params=_params(("parallel",)))(a, w, res, gain)


def _mm_down_loss(name, a, w, layer, res, tgt):
    t, kf = a.shape
    _, _, n = w.shape
    tm = _pick(t, (512, 256, 128))
    steps = t // tm

    def body(a_ref, w_ref, r_ref, t_ref, dy_ref, l_ref, acc_ref):
        i = pl.program_id(0)

        @pl.when(i == 0)
        def _():
            acc_ref[...] = jnp.zeros_like(acc_ref)

        e = (r_ref[...] + _dot(a_ref[...], w_ref[...], NN)) - t_ref[...]
        dy_ref[...] = e * (1.0 / n)
        acc_ref[...] += (e * e).reshape(tm // SUBLANES, SUBLANES, n).sum(axis=0)

        @pl.when(i == steps - 1)
        def _():
            l_ref[...] = jnp.sum(acc_ref[...], keepdims=True) * (0.5 / n)

    row = pl.BlockSpec((tm, n), lambda i: (i, 0))
    return pl.pallas_call(
        body, name=name, grid=(steps,),
        in_specs=[pl.BlockSpec((tm, kf), lambda i: (i, 0)),
                  pl.BlockSpec((None, kf, n), lambda i: (layer, 0, 0)), row, row],
        out_specs=[row, pl.BlockSpec((1, 1), lambda i: (0, 0))],
        out_shape=[_sds((t, n), F32), _sds((1, 1), F32)],
        scratch_shapes=[pltpu.VMEM((SUBLANES, n), F32)],
        compiler_params=_params(("arbitrary",)))(a, w, res, tgt)


def _mm_down_t(name, dx, w, layer):
    t, n = dx.shape
    _, kf, _ = w.shape
    tm = _pick(t, (512, 256, 128))

    def body(a_ref, w_ref, o_ref):
        o_ref[...] = _dot(a_ref[...].astype(BF16), w_ref[...], NT).astype(BF16)

    return pl.pallas_call(
        body, name=name, grid=(t // tm,),
        in_specs=[pl.BlockSpec((tm, n), lambda i: (i, 0)),
                  pl.BlockSpec((None, kf, n), lambda i: (layer, 0, 0))],
        out_specs=pl.BlockSpec((tm, kf), lambda i: (i, 0)),
        out_shape=_sds((t, kf), BF16),
        compiler_params=_params(("parallel",)))(dx, w)


def _mm_down_t_swiglu(name, dx, w, layer, g, u):
    t, n = dx.shape
    f = g.shape[1]
    tn = f // 2
    tm = _pick(t, (512, 256, 128))

    def body(a_ref, w_ref, g_ref, u_ref, dg_ref, du_ref):
        da = _dot(a_ref[...].astype(BF16), w_ref[...], NT)
        gv = g_ref[...].astype(F32)
        sg = _sigmoid(gv)
        dg_ref[...] = (da * u_ref[...].astype(F32) * (sg * (1.0 + gv * (1.0 - sg)))).astype(BF16)
        du_ref[...] = (da * (gv * sg)).astype(BF16)

    tile = pl.BlockSpec((tm, tn), lambda j, i: (i, j))
    return pl.pallas_call(
        body, name=name, grid=(f // tn, t // tm),
        in_specs=[pl.BlockSpec((tm, n), lambda j, i: (i, 0)),
                  pl.BlockSpec((None, tn, n), lambda j, i: (layer, j, 0)), tile, tile],
        out_specs=[tile, tile],
        out_shape=[_sds((t, f), BF16)] * 2,
        compiler_params=_params(("parallel", "parallel")))(dx, w, g, u)


def _dgrad_norm(name, acts, act_blocks, pieces, w4, layer, x, gain, dres):
    t, d = x.shape
    _, _, k, nq = w4.shape
    tm = _pick(t, (256, 128))
    n_act = len(acts)

    def body(*refs):
        act_refs = refs[:n_act]
        w_ref, x_ref, g_ref, dr_ref, dx_ref, dg_ref = refs[n_act:]

        @pl.when(pl.program_id(0) == 0)
        def _():
            dg_ref[...] = jnp.zeros_like(dg_ref)

        dh = None
        for a_tile, w_tile in pieces(act_refs, w_ref):
            term = _dot(a_tile, w_tile, NT)
            dh = term if dh is None else dh + term
        xv = x_ref[...]
        r = _rms(xv)
        xhat = xv * r
        gd = dh * g_ref[...]
        dx_ref[...] = dr_ref[...] + r * (gd - xhat * jnp.mean(gd * xhat, axis=-1, keepdims=True))
        dg_ref[...] += (dh * xhat).reshape(tm // SUBLANES, SUBLANES, d).sum(axis=0)

    row = pl.BlockSpec((tm, d), lambda i: (i, 0))
    return pl.pallas_call(
        body, name=name, grid=(t // tm,),
        in_specs=[*act_blocks(tm),
                  pl.BlockSpec((None, N_CHIPS, k, nq), lambda i: (layer, 0, 0, 0)),
                  row, pl.BlockSpec((1, d), lambda i: (0, 0)), row],
        out_specs=[row, pl.BlockSpec((SUBLANES, d), lambda i: (0, 0))],
        out_shape=[_sds((t, d), F32), _sds((SUBLANES, d), F32)],
        compiler_params=_params(("arbitrary",)))(*acts, w4, x, gain, dres)


def _dgrad_norm_ffn(name, dg, du, w4, layer, x, gain, dres):
    nq = w4.shape[3]
    f = dg.shape[1]

    def blocks(tm):
        return [pl.BlockSpec((tm, f), lambda i: (i, 0))] * 2

    def pieces(act_refs, w_ref):
        dg_ref, du_ref = act_refs
        return [(dg_ref[:, 0:nq], w_ref[0]), (dg_ref[:, nq:2 * nq], w_ref[1]),
                (du_ref[:, 0:nq], w_ref[2]), (du_ref[:, nq:2 * nq], w_ref[3])]

    return _dgrad_norm(name, [dg, du], blocks, pieces, w4, layer, x, gain, dres)


def _dgrad_norm_qkv(name, dqkv, w4, x, gain, dres):
    nq = w4.shape[3]

    def blocks(tm):
        return [pl.BlockSpec((tm, N_CHIPS * nq), lambda i: (i, 0))]

    def pieces(act_refs, w_ref):
        return [(act_refs[0][:, q * nq:(q + 1) * nq], w_ref[q]) for q in range(N_CHIPS)]

    return _dgrad_norm(name, [dqkv], blocks, pieces, w4, 0, x, gain, dres)


def _dgrad_norm_conv(name, d3, w4, x, gain, dres):
    _, _, d = d3.shape
    nq = w4.shape[3]
    per_part, per_q = d // MXU_COLS, nq // MXU_COLS

    def blocks(tm):
        return [pl.BlockSpec((3, tm, d), lambda i: (0, i, 0))]

    def pieces(act_refs, w_ref):
        out = []
        for jb in range(3 * per_part):
            ca, cw = (jb % per_part) * MXU_COLS, (jb % per_q) * MXU_COLS
            out.append((act_refs[0][jb // per_part, :, ca:ca + MXU_COLS], w_ref[jb // per_q, :, cw:cw + MXU_COLS]))
        return out

    return _dgrad_norm(name, [d3], blocks, pieces, w4, 0, x, gain, dres)


def _wgrad_up2(name, h, dg, du):
    t, k = h.shape
    nq = dg.shape[1] // 2
    tk = _pick(t, (512, 256, 128))
    steps = t // tk
    half = N_CHIPS // 2

    def body(h_ref, dg_ref, du_ref, o_ref):
        q = pl.program_id(0)

        @pl.when(pl.program_id(1) == 0)
        def _():
            o_ref[...] = jnp.zeros_like(o_ref)

        @pl.when(q < half)
        def _():
            o_ref[...] += _dot(h_ref[...], dg_ref[...], TN)

        @pl.when(q >= half)
        def _():
            o_ref[...] += _dot(h_ref[...], du_ref[...], TN)

    return pl.pallas_call(
        body, name=name, grid=(N_CHIPS, steps),
        in_specs=[pl.BlockSpec((tk, k), lambda q, s: (s, 0)),
                  pl.BlockSpec((tk, nq), lambda q, s: (jnp.where(q < half, s, steps - 1), jnp.minimum(q, half - 1))),
                  pl.BlockSpec((tk, nq), lambda q, s: (jnp.where(q >= half, s, 0), jnp.maximum(q - half, 0)))],
        out_specs=pl.BlockSpec((None, k, nq), lambda q, s: (q, 0, 0)),
        out_shape=_sds((N_CHIPS, k, nq), F32),
        compiler_params=_params(("parallel", "arbitrary")))(h, dg, du)


def _wgrad_joined(name, h, dy):
    t, k = h.shape
    nq = dy.shape[1] // N_CHIPS
    tk = _pick(t, (1024, 512, 256, 128))

    def body(h_ref, dy_ref, o_ref):
        @pl.when(pl.program_id(0) == 0)
        def _():
            o_ref[...] = jnp.zeros_like(o_ref)

        res = _dot(h_ref[...], dy_ref[...], TN)
        for q in range(N_CHIPS):
            o_ref[q] += res[:, q * nq:(q + 1) * nq]

    return pl.pallas_call(
        body, name=name, grid=(t // tk,),
        in_specs=[pl.BlockSpec((tk, k), lambda s: (s, 0)), pl.BlockSpec((tk, N_CHIPS * nq), lambda s: (s, 0))],
        out_specs=pl.BlockSpec((N_CHIPS, k, nq), lambda s: (0, 0, 0)),
        out_shape=_sds((N_CHIPS, k, nq), F32),
        compiler_params=_params(("arbitrary",)))(h, dy)


def _wgrad_conv_in(name, h, d3, nq):
    t, k = h.shape
    d = d3.shape[2]
    per_part, per_q = d // MXU_COLS, nq // MXU_COLS
    tk = _pick(t, (512, 256, 128))

    def body(h_ref, d_ref, o_ref):
        @pl.when(pl.program_id(0) == 0)
        def _():
            o_ref[...] = jnp.zeros_like(o_ref)

        hv = h_ref[...]
        for part in range(3):
            res = _dot(hv, d_ref[part], TN)
            for cc in range(per_part):
                jb = part * per_part + cc
                co = (jb % per_q) * MXU_COLS
                o_ref[jb // per_q, :, co:co + MXU_COLS] += res[:, cc * MXU_COLS:(cc + 1) * MXU_COLS]

    return pl.pallas_call(
        body, name=name, grid=(t // tk,),
        in_specs=[pl.BlockSpec((tk, k), lambda s: (s, 0)), pl.BlockSpec((3, tk, d), lambda s: (0, s, 0))],
        out_specs=pl.BlockSpec((N_CHIPS, k, nq), lambda s: (0, 0, 0)),
        out_shape=_sds((N_CHIPS, k, nq), F32),
        compiler_params=_params(("arbitrary",)))(h, d3)


def _wgrad_down(name, a, dx, tmw):
    t, kf = a.shape
    n = dx.shape[1]
    tk = _pick(t, (512, 256, 128))

    def body(a_ref, b_ref, o_ref):
        @pl.when(pl.program_id(1) == 0)
        def _():
            o_ref[...] = jnp.zeros_like(o_ref)

        o_ref[...] += _dot(a_ref[...], b_ref[...].astype(BF16), TN)

    g = pl.pallas_call(
        body, name=name, grid=(kf // tmw, t // tk),
        in_specs=[pl.BlockSpec((tk, tmw), lambda j, s: (s, j)), pl.BlockSpec((tk, n), lambda j, s: (s, 0))],
        out_specs=pl.BlockSpec((tmw, n), lambda j, s: (j, 0)),
        out_shape=_sds((kf, n), F32),
        compiler_params=_params(("parallel", "arbitrary")))(a, dx)
    return g.reshape(N_CHIPS, kf // N_CHIPS, n)


def _rms_fwd(name, x, gain):
    t, d = x.shape
    tm = _pick(t, (512, 256, 128))

    def body(x_ref, g_ref, h_ref):
        xv = x_ref[...]
        h_ref[...] = ((xv * _rms(xv)) * g_ref[...]).astype(BF16)

    return pl.pallas_call(
        body, name=name, grid=(t // tm,),
        in_specs=[pl.BlockSpec((tm, d), lambda i: (i, 0)), pl.BlockSpec((1, d), lambda i: (0, 0))],
        out_specs=pl.BlockSpec((tm, d), lambda i: (i, 0)),
        out_shape=_sds((t, d), BF16),
        compiler_params=_params(("parallel",)))(x, gain)


def _shift_rows(u, k, rows):
    s = u.shape[0]
    if k > 0:
        return jnp.where(rows >= k, pltpu.roll(u, k, 0), 0.0)
    return jnp.where(rows < s + k, pltpu.roll(u, s + k, 0), 0.0)


def _conv_fwd(bcx, cw, nseq, seq):
    t, d3 = bcx.shape
    d = d3 // 3
    cb = MXU_COLS
    nj = d // cb

    def body(b_ref, c_ref, x_ref, cw_ref, z_ref):
        u = b_ref[...].astype(F32) * x_ref[...].astype(F32)
        rows = lax.broadcasted_iota(jnp.int32, u.shape, 0)
        cwv = cw_ref[...]
        y = cwv[2:3] * u + cwv[1:2] * _shift_rows(u, 1, rows) + cwv[0:1] * _shift_rows(u, 2, rows)
        z_ref[...] = (c_ref[...].astype(F32) * y).astype(BF16)

    return pl.pallas_call(
        body, name="conv_fwd", grid=(nseq, nj),
        in_specs=[pl.BlockSpec((seq, cb), lambda b, j: (b, j)),
                  pl.BlockSpec((seq, cb), lambda b, j: (b, nj + j)),
                  pl.BlockSpec((seq, cb), lambda b, j: (b, 2 * nj + j)),
                  pl.BlockSpec((3, cb), lambda b, j: (0, j))],
        out_specs=pl.BlockSpec((seq, cb), lambda b, j: (b, j)),
        out_shape=_sds((t, d), BF16),
        compiler_params=_params(("parallel", "parallel")))(bcx, bcx, bcx, cw)


def _conv_bwd(dz, bcx, cw, nseq, seq):
    t, d3 = bcx.shape
    d = d3 // 3
    cb = MXU_COLS
    nj = d // cb

    def body(dz_ref, b_ref, c_ref, x_ref, cw_ref, o_ref, dcw_ref):
        @pl.when(pl.program_id(1) == 0)
        def _():
            dcw_ref[...] = jnp.zeros_like(dcw_ref)

        b = b_ref[...].astype(F32)
        c = c_ref[...].astype(F32)
        xv = x_ref[...].astype(F32)
        dzv = dz_ref[...].astype(F32)
        u = b * xv
        rows = lax.broadcasted_iota(jnp.int32, u.shape, 0)
        u1 = _shift_rows(u, 1, rows)
        u2 = _shift_rows(u, 2, rows)
        cwv = cw_ref[...]
        y = cwv[2:3] * u + cwv[1:2] * u1 + cwv[0:1] * u2
        dyc = dzv * c
        du = cwv[2:3] * dyc + cwv[1:2] * _shift_rows(dyc, -1, rows) + cwv[0:1] * _shift_rows(dyc, -2, rows)
        o_ref[0] = (du * xv).astype(BF16)
        o_ref[1] = (dzv * y).astype(BF16)
        o_ref[2] = (du * b).astype(BF16)
        s0 = jnp.sum(dyc * u2, axis=0, keepdims=True)
        s1 = jnp.sum(dyc * u1, axis=0, keepdims=True)
        s2 = jnp.sum(dyc * u, axis=0, keepdims=True)
        tap = lax.broadcasted_iota(jnp.int32, (3, cb), 0)
        dcw_ref[...] += jnp.where(tap == 0, s0, jnp.where(tap == 1, s1, s2))

    return pl.pallas_call(
        body, name="conv_bwd", grid=(nj, nseq),
        in_specs=[pl.BlockSpec((seq, cb), lambda j, b: (b, j)),
                  pl.BlockSpec((seq, cb), lambda j, b: (b, j)),
                  pl.BlockSpec((seq, cb), lambda j, b: (b, nj + j)),
                  pl.BlockSpec((seq, cb), lambda j, b: (b, 2 * nj + j)),
                  pl.BlockSpec((3, cb), lambda j, b: (0, j))],
        out_specs=[pl.BlockSpec((3, seq, cb), lambda j, b: (0, b, j)),
                   pl.BlockSpec((3, cb), lambda j, b: (0, j))],
        out_shape=[_sds((3, t, d), BF16), _sds((3, d), F32)],
        compiler_params=_params(("parallel", "arbitrary")))(dz, bcx, bcx, bcx, cw)


def _pair_norm(x, gain_pair, low):
    sq = x * x
    ss_lo = jnp.sum(jnp.where(low, sq, 0.0), axis=-1, keepdims=True)
    ss_hi = jnp.sum(jnp.where(low, 0.0, sq), axis=-1, keepdims=True)
    r = lax.rsqrt(jnp.where(low, ss_lo, ss_hi) * (1.0 / HEAD_DIM) + EPS)
    xhat = x * r
    return xhat * gain_pair, xhat, r


def _softmax_sink(qn_b, k_b, slope, sink, distf, mask):
    s = _dot(qn_b, k_b, NT) * (1.0 / (HEAD_DIM ** 0.5)) - slope * distf
    s = jnp.where(mask, s, -1e30)
    m = jnp.maximum(jnp.max(s, axis=-1, keepdims=True), sink)
    e = jnp.exp(s - m)
    es = jnp.exp(sink - m)
    inv = 1.0 / (jnp.sum(e, axis=-1, keepdims=True) + es)
    return e * inv, es * inv


def _kv_pairs(kv_tile, parity, low):
    own = jnp.where(low if parity == 0 else jnp.logical_not(low), kv_tile, 0.0)
    other = pltpu.roll(own, HEAD_DIM, 1)
    return (own, other) if parity == 0 else (other, own)


def _attn_geometry(n):
    q0 = pl.multiple_of(n * BLOCK, BLOCK)
    k0 = pl.multiple_of(jnp.maximum(n - 1, 0) * BLOCK, BLOCK)
    qi = lax.broadcasted_iota(jnp.int32, (BLOCK, 2 * BLOCK), 0)
    kj = lax.broadcasted_iota(jnp.int32, (BLOCK, 2 * BLOCK), 1)
    dist = (q0 - k0) + qi - kj
    mask = jnp.logical_and(dist >= 0, dist < WINDOW)
    return q0, k0, dist.astype(F32), mask


def _attn_fwd(qkv, qg_pair, kg_pair, sinks, nseq, seq):
    t = qkv.shape[0]
    dq = N_Q_HEADS * HEAD_DIM
    dkv = N_KV_HEADS * HEAD_DIM

    def body(sk_ref, qkv_ref, qg_ref, kg_ref, o_ref):
        low = lax.broadcasted_iota(jnp.int32, (1, LANES), 1) < HEAD_DIM
        qg = qg_ref[...]
        kg = kg_ref[...]

        def blk(n, carry):
            q0, k0, distf, mask = _attn_geometry(n)
            for kt in range(dkv // LANES):
                kraw = qkv_ref[pl.ds(k0, 2 * BLOCK), dq + kt * LANES:dq + (kt + 1) * LANES].astype(F32)
                vraw = qkv_ref[pl.ds(k0, 2 * BLOCK), dq + dkv + kt * LANES:dq + dkv + (kt + 1) * LANES].astype(F32)
                kn, _, _ = _pair_norm(kraw, kg, low)
                for par in range(2):
                    kh = 2 * kt + par
                    k_lo, k_hi = [v.astype(BF16) for v in _kv_pairs(kn, par, low)]
                    v_lo, v_hi = [v.astype(BF16) for v in _kv_pairs(vraw, par, low)]
                    for jj in range(2):
                        j = 2 * kh + jj
                        qraw = qkv_ref[pl.ds(q0, BLOCK), j * LANES:(j + 1) * LANES].astype(F32)
                        qn, _, _ = _pair_norm(qraw, qg, low)
                        qn_b = qn.astype(BF16)
                        p0, _ = _softmax_sink(qn_b, k_lo, ALIBI_SLOPES[2 * j], sk_ref[0, 2 * j], distf, mask)
                        p1, _ = _softmax_sink(qn_b, k_hi, ALIBI_SLOPES[2 * j + 1], sk_ref[0, 2 * j + 1], distf, mask)
                        o = _dot(p0.astype(BF16), v_lo, NN) + _dot(p1.astype(BF16), v_hi, NN)
                        o_ref[pl.ds(q0, BLOCK), j * LANES:(j + 1) * LANES] = o.astype(BF16)
            return carry

        lax.fori_loop(0, seq // BLOCK, blk, 0)

    return pl.pallas_call(
        body, name="attn_fwd", grid=(nseq,),
        in_specs=[pl.BlockSpec(memory_space=pltpu.SMEM),
                  pl.BlockSpec((seq, dq + 2 * dkv), lambda b: (b, 0)),
                  pl.BlockSpec((1, LANES), lambda b: (0, 0)),
                  pl.BlockSpec((1, LANES), lambda b: (0, 0))],
        out_specs=pl.BlockSpec((seq, dq), lambda b: (b, 0)),
        out_shape=_sds((t, dq), BF16),
        compiler_params=_params(("parallel",)))(sinks, qkv, qg_pair, kg_pair)


def _attn_bwd(do, qkv, qg_pair, kg_pair, sinks, nseq, seq):
    t = qkv.shape[0]
    dq = N_Q_HEADS * HEAD_DIM
    dkv = N_KV_HEADS * HEAD_DIM
    scale = 1.0 / (HEAD_DIM ** 0.5)

    def body(sk_ref, do_ref, qkv_ref, qg_ref, kg_ref, o_ref, dqg_ref, dkg_ref, dsk_ref, acc_ref):
        @pl.when(pl.program_id(0) == 0)
        def _():
            dqg_ref[...] = jnp.zeros_like(dqg_ref)
            dkg_ref[...] = jnp.zeros_like(dkg_ref)
            dsk_ref[...] = jnp.zeros_like(dsk_ref)

        acc_ref[...] = jnp.zeros_like(acc_ref)
        low = lax.broadcasted_iota(jnp.int32, (1, LANES), 1) < HEAD_DIM
        lane = lax.broadcasted_iota(jnp.int32, (1, LANES), 1)
        qg = qg_ref[...]
        kg = kg_ref[...]

        def blk(n, carry):
            dqg_acc, dkg_acc, dsk_acc = carry
            q0, k0, distf, mask = _attn_geometry(n)
            for kt in range(dkv // LANES):
                kraw = qkv_ref[pl.ds(k0, 2 * BLOCK), dq + kt * LANES:dq + (kt + 1) * LANES].astype(F32)
                vraw = qkv_ref[pl.ds(k0, 2 * BLOCK), dq + dkv + kt * LANES:dq + dkv + (kt + 1) * LANES].astype(F32)
                kn, khat, rk = _pair_norm(kraw, kg, low)
                dk_tile = None
                dv_tile = None
                for par in range(2):
                    kh = 2 * kt + par
                    own = low if par == 0 else jnp.logical_not(low)
                    k_lo, k_hi = [v.astype(BF16) for v in _kv_pairs(kn, par, low)]
                    v_lo, v_hi = [v.astype(BF16) for v in _kv_pairs(vraw, par, low)]
                    dkn_acc = jnp.zeros((2 * BLOCK, LANES), F32)
                    dv_acc = jnp.zeros((2 * BLOCK, LANES), F32)
                    for jj in range(2):
                        j = 2 * kh + jj
                        qraw = qkv_ref[pl.ds(q0, BLOCK), j * LANES:(j + 1) * LANES].astype(F32)
                        qn, qhat, rq = _pair_norm(qraw, qg, low)
                        qn_b = qn.astype(BF16)
                        do_b = do_ref[pl.ds(q0, BLOCK), j * LANES:(j + 1) * LANES]
                        dqn = None
                        dkn_pair = []
                        dv_pair = []
                        for e, (k_e, v_e) in enumerate(((k_lo, v_lo), (k_hi, v_hi))):
                            h = 2 * j + e
                            p, ps = _softmax_sink(qn_b, k_e, ALIBI_SLOPES[h], sk_ref[0, h], distf, mask)
                            dp = _dot(do_b, v_e, NT)
                            dsum = jnp.sum(p * dp, axis=-1, keepdims=True)
                            ds_b = ((p * (dp - dsum)) * scale).astype(BF16)
                            dsk_acc = dsk_acc - jnp.where(lane == h, jnp.sum(ps * dsum, axis=0, keepdims=True), 0.0)
                            term = _dot(ds_b, k_e, NN)
                            dqn = term if dqn is None else dqn + term
                            dkn_pair.append(_dot(ds_b, qn_b, TN))
                            dv_pair.append(_dot(p.astype(BF16), do_b, TN))
                        dkn_acc = dkn_acc + jnp.where(low, dkn_pair[0], dkn_pair[1])
                        dv_acc = dv_acc + jnp.where(low, dv_pair[0], dv_pair[1])
                        dqg_acc = dqg_acc + jnp.sum(dqn * qhat, axis=0, keepdims=True)
                        dqhat = dqn * qg
                        prod = dqhat * qhat
                        m_lo = jnp.sum(jnp.where(low, prod, 0.0), axis=-1, keepdims=True)
                        m_hi = jnp.sum(jnp.where(low, 0.0, prod), axis=-1, keepdims=True)
                        mean = jnp.where(low, m_lo, m_hi) * (1.0 / HEAD_DIM)
                        o_ref[pl.ds(q0, BLOCK), j * LANES:(j + 1) * LANES] = (rq * (dqhat - qhat * mean)).astype(BF16)
                    dkn = dkn_acc + pltpu.roll(dkn_acc, HEAD_DIM, 1)
                    dvh = dv_acc + pltpu.roll(dv_acc, HEAD_DIM, 1)
                    khat_own = jnp.where(own, khat, 0.0)
                    khat_dup = khat_own + pltpu.roll(khat_own, HEAD_DIM, 1)
                    dkg_acc = dkg_acc + jnp.sum(jnp.where(own, dkn * khat_dup, 0.0), axis=0, keepdims=True)
                    dkhat = dkn * kg
                    mean_k = jnp.sum(dkhat * khat_dup, axis=-1, keepdims=True) * (1.0 / LANES)
                    dk_raw = rk * (dkhat - khat_dup * mean_k)
                    dk_tile = jnp.where(own, dk_raw, 0.0) if dk_tile is None else jnp.where(own, dk_raw, dk_tile)
                    dv_tile = jnp.where(own, dvh, 0.0) if dv_tile is None else jnp.where(own, dvh, dv_tile)
                acc_ref[pl.ds(k0, 2 * BLOCK), kt * LANES:(kt + 1) * LANES] += dk_tile
                acc_ref[pl.ds(k0, 2 * BLOCK), dkv + kt * LANES:dkv + (kt + 1) * LANES] += dv_tile
            return dqg_acc, dkg_acc, dsk_acc

        zero = jnp.zeros((1, LANES), F32)
        dqg_acc, dkg_acc, dsk_acc = lax.fori_loop(0, seq // BLOCK, blk, (zero, zero, zero))
        dqg_ref[...] += dqg_acc
        dkg_ref[...] += dkg_acc
        dsk_ref[...] += dsk_acc
        o_ref[:, dq:dq + 2 * dkv] = acc_ref[...].astype(BF16)

    small = pl.BlockSpec((1, LANES), lambda b: (0, 0))
    return pl.pallas_call(
        body, name="attn_bwd", grid=(nseq,),
        in_specs=[pl.BlockSpec(memory_space=pltpu.SMEM),
                  pl.BlockSpec((seq, dq), lambda b: (b, 0)),
                  pl.BlockSpec((seq, dq + 2 * dkv), lambda b: (b, 0)),
                  small, small],
        out_specs=[pl.BlockSpec((seq, dq + 2 * dkv), lambda b: (b, 0)), small, small, small],
        out_shape=[_sds((t, dq + 2 * dkv), BF16), _sds((1, LANES), F32), _sds((1, LANES), F32),
                   _sds((1, LANES), F32)],
        scratch_shapes=[pltpu.VMEM((seq, 2 * dkv), F32)],
        compiler_params=_params(("arbitrary",)))(sinks, do, qkv, qg_pair, kg_pair)


def _place():
    x, y, c = lax.axis_index("x"), lax.axis_index("y"), lax.axis_index("c")
    other_chips = [(1 - x, y), (x, 1 - y), (1 - x, 1 - y)]
    return x, y, c, other_chips


def _half_rows(c, rows):
    rh = rows // 2
    return pl.ds(pl.multiple_of(c * rh, BF16_ROWS), rh)


def _any_specs(n):
    return [pl.BlockSpec(memory_space=pl.ANY)] * n


def _cast_own(name, w, place, layer=None):
    nl, r, cdim = w.shape
    first = 0
    if layer is not None:
        nl, first = 1, layer
    rt = _pick(r, (256, 128, 64, 32))

    def body(s_ref, w_ref, o_ref):
        o_ref[...] = w_ref[...].astype(BF16)

    grid_spec = pltpu.PrefetchScalarGridSpec(
        num_scalar_prefetch=1, grid=(nl, r // rt),
        in_specs=[pl.BlockSpec((None, rt, cdim), lambda l, i, s: (first + l, i, 0))],
        out_specs=pl.BlockSpec((None, None, rt, cdim), lambda l, i, s: (l, s[1], i, 0)))
    return pl.pallas_call(
        body, name=name, grid_spec=grid_spec, out_shape=_sds((nl, N_CHIPS, r, cdim), BF16),
        compiler_params=_params(("parallel", "parallel")))(place, w)


def _gather_protocol(outs, shapes, send_sems, recv_sems):
    n = len(outs)
    x, y, c, other_chips = _place()
    me_chip = 2 * x + y
    sibling = (x, y, 1 - c)

    def rows(u, chip, half):
        return outs[u].at[:, chip, _half_rows(half, shapes[u][2]), :]

    def copy(sem, part, to):
        return pltpu.make_async_remote_copy(src_ref=part, dst_ref=part, send_sem=send_sems.at[sem],
                                            recv_sem=recv_sems.at[sem], device_id=to, device_id_type=MESH)

    sends = []
    for u in range(n):
        for k, chip in enumerate(other_chips):
            cp = copy(6 * u + k, rows(u, me_chip, c), (*chip, c))
            cp.start()
            sends.append(cp)
    for u in range(n):
        for k, chip in enumerate(other_chips):
            got = rows(u, 2 * chip[0] + chip[1], c)
            copy(6 * u + k, got, (*chip, c)).wait_recv()
            cp = copy(6 * u + 3 + k, got, sibling)
            cp.start()
            sends.append(cp)
    for u in range(n):
        for k, chip in enumerate(other_chips):
            copy(6 * u + 3 + k, rows(u, 2 * chip[0] + chip[1], 1 - c), sibling).wait_recv()
    for cp in sends:
        cp.wait_send()


def _allgather_sequencer(name, bufs, collective_id):
    n = len(bufs)
    shapes = [b.shape for b in bufs]
    refs = [jax.new_ref(b, memory_space=pltpu.MemorySpace.HBM) for b in bufs]

    @pl.kernel(mesh=plsc.ScalarSubcoreMesh(axis_name="sequencer", num_cores=1), name=name,
               scratch_types=(pltpu.SemaphoreType.DMA((6 * n,)), pltpu.SemaphoreType.DMA((6 * n,))),
               compiler_params=pltpu.CompilerParams(collective_id=collective_id))
    def launch(send_sems, recv_sems):
        x, y, c, other_chips = _place()
        peers = [(*chip, c) for chip in other_chips] + [(x, y, 1 - c)]
        barrier = pltpu.get_barrier_semaphore()
        for peer in peers:
            pl.semaphore_signal(barrier, inc=1, device_id=peer, device_id_type=MESH)
        pl.semaphore_wait(barrier, len(peers))
        _gather_protocol(refs, shapes, send_sems, recv_sems)

    launch()
    return [r[...] for r in refs]


def _allgather_weights(bufs):
    n = len(bufs)
    shapes = [b.shape for b in bufs]

    def body(*refs):
        _gather_protocol(refs[n:2 * n], shapes, *refs[2 * n:])

    return pl.pallas_call(
        body, name="allgather_weights",
        in_specs=_any_specs(n), out_specs=_any_specs(n),
        out_shape=[_sds(s, BF16) for s in shapes],
        input_output_aliases={u: u for u in range(n)},
        scratch_shapes=[pltpu.SemaphoreType.DMA((6 * n,)), pltpu.SemaphoreType.DMA((6 * n,))],
    )(*bufs)


def _exchange_halves(grads):
    n = len(grads)
    shapes = [g.shape for g in grads]

    def body(*refs):
        gs, outs = refs[:n], refs[n:2 * n]
        send_sems, recv_sems = refs[2 * n:]
        x, y, c, _ = _place()
        sends = []
        for u in range(n):
            cp = pltpu.make_async_remote_copy(
                src_ref=gs[u].at[:, _half_rows(1 - c, shapes[u][1]), :], dst_ref=outs[u],
                send_sem=send_sems.at[u], recv_sem=recv_sems.at[u], device_id=(x, y, 1 - c), device_id_type=MESH)
            cp.start()
            sends.append(cp)
        for cp in sends:
            cp.wait_recv()
        for cp in sends:
            cp.wait_send()

    return pl.pallas_call(
        body, name="exchange_halves",
        in_specs=_any_specs(n), out_specs=_any_specs(n),
        out_shape=[_sds((s[0], s[1] // 2, s[2]), F32) for s in shapes],
        scratch_shapes=[pltpu.SemaphoreType.DMA((n,)), pltpu.SemaphoreType.DMA((n,))],
    )(*grads)


def _sum_halves(name, g, got, place):
    _, r, cdim = g.shape
    rh = r // 2
    rt = _pick(rh, (128, 64, 32, 16))
    nr = rh // rt

    def body(s_ref, g_ref, got_ref, pb_ref, pf_ref):
        s = g_ref[...] + got_ref[...]
        pb_ref[...] = s.astype(BF16)

        @pl.when(pl.program_id(1) == s_ref[1])
        def _():
            pf_ref[...] = s

    grid_spec = pltpu.PrefetchScalarGridSpec(
        num_scalar_prefetch=1, grid=(nr, N_CHIPS),
        in_specs=[pl.BlockSpec((None, rt, cdim), lambda i, q, s: (q, s[0] * nr + i, 0)),
                  pl.BlockSpec((None, rt, cdim), lambda i, q, s: (q, i, 0))],
        out_specs=[pl.BlockSpec((None, rt, cdim), lambda i, q, s: (q, i, 0)),
                   pl.BlockSpec((rt, cdim), lambda i, q, s: (i, 0))])
    return pl.pallas_call(
        body, name=name, grid_spec=grid_spec,
        out_shape=[_sds((N_CHIPS, rh, cdim), BF16), _sds((rh, cdim), F32)],
        compiler_params=_params(("parallel", "arbitrary")))(place, g, got)


def _scatter_partials(partials):
    n = len(partials)
    shapes = [p.shape for p in partials]

    def body(*refs):
        ps, outs = refs[:n], refs[n:2 * n]
        send_sems, recv_sems = refs[2 * n:]
        x, y, c, other_chips = _place()
        sends = []
        for u in range(n):
            for k, chip in enumerate(other_chips):
                cp = pltpu.make_async_remote_copy(
                    src_ref=ps[u].at[2 * chip[0] + chip[1]], dst_ref=outs[u].at[k],
                    send_sem=send_sems.at[3 * u + k], recv_sem=recv_sems.at[3 * u + k],
                    device_id=(*chip, c), device_id_type=MESH)
                cp.start()
                sends.append(cp)
        for cp in sends:
            cp.wait_recv()
        for cp in sends:
            cp.wait_send()

    return pl.pallas_call(
        body, name="scatter_partials",
        in_specs=_any_specs(n), out_specs=_any_specs(n),
        out_shape=[_sds((3, s[1], s[2]), BF16) for s in shapes],
        scratch_shapes=[pltpu.SemaphoreType.DMA((3 * n,)), pltpu.SemaphoreType.DMA((3 * n,))],
    )(*partials)


def _sum_partials(name, own, got, place, layer, nl, prev):
    rh, cdim = own.shape
    rt = _pick(rh, (128, 64, 32, 16))
    nr = rh // rt

    def body(s_ref, own_ref, got_ref, *rest):
        o_ref = rest[-1]
        o_ref[...] = ((own_ref[...] + got_ref[0].astype(F32)) + got_ref[1].astype(F32)) + got_ref[2].astype(F32)

    in_specs = [pl.BlockSpec((rt, cdim), lambda i, s: (i, 0)), pl.BlockSpec((3, rt, cdim), lambda i, s: (0, i, 0))]
    args = [place, own, got]
    aliases = {}
    if prev is not None:
        in_specs.append(pl.BlockSpec(memory_space=pl.ANY))
        args.append(prev)
        aliases = {3: 0}
    grid_spec = pltpu.PrefetchScalarGridSpec(
        num_scalar_prefetch=1, grid=(nr,), in_specs=in_specs,
        out_specs=pl.BlockSpec((None, rt, cdim), lambda i, s: (layer, s[0] * nr + i, 0)))
    return pl.pallas_call(
        body, name=name, grid_spec=grid_spec, out_shape=_sds((nl, 2 * rh, cdim), F32),
        input_output_aliases=aliases, compiler_params=_params(("parallel",)))(*args)


def _share_halves(bufs):
    n = len(bufs)
    shapes = [b.shape for b in bufs]
    units = [(w, l) for w in range(n) for l in range(shapes[w][0])]

    def body(*refs):
        outs = refs[n:2 * n]
        send_sems, recv_sems = refs[2 * n:]
        x, y, c, _ = _place()
        sends = []
        for u, (w, l) in enumerate(units):
            mine = outs[w].at[l, _half_rows(c, shapes[w][1]), :]
            cp = pltpu.make_async_remote_copy(src_ref=mine, dst_ref=mine, send_sem=send_sems.at[u],
                                              recv_sem=recv_sems.at[u], device_id=(x, y, 1 - c), device_id_type=MESH)
            cp.start()
            sends.append(cp)
        for u, (w, l) in enumerate(units):
            theirs = outs[w].at[l, _half_rows(1 - c, shapes[w][1]), :]
            pltpu.make_async_remote_copy(src_ref=theirs, dst_ref=theirs, send_sem=send_sems.at[u],
                                         recv_sem=recv_sems.at[u], device_id=(x, y, 1 - c),
                                         device_id_type=MESH).wait_recv()
        for cp in sends:
            cp.wait_send()

    return pl.pallas_call(
        body, name="share_halves",
        in_specs=_any_specs(n), out_specs=_any_specs(n),
        out_shape=[_sds(s, F32) for s in shapes],
        input_output_aliases={u: u for u in range(n)},
        scratch_shapes=[pltpu.SemaphoreType.DMA((len(units),)), pltpu.SemaphoreType.DMA((len(units),))],
    )(*bufs)


def _gather_blocks(block_ref, all_ref, send_sems, recv_sems):
    x, y, c, _ = _place()
    me = 4 * x + 2 * y + c
    all_ref[me] = block_ref[...]
    sends = []
    for rel in range(1, 8):
        fx, fy, fc = (rel >> 2) & 1, (rel >> 1) & 1, rel & 1
        peer = (x ^ fx, y ^ fy, c ^ fc)
        cp = pltpu.make_async_remote_copy(src_ref=block_ref, dst_ref=all_ref.at[me], send_sem=send_sems.at[rel - 1],
                                          recv_sem=recv_sems.at[rel - 1], device_id=peer, device_id_type=MESH)
        cp.start()
        sends.append(cp)
    for cp in sends:
        cp.wait_recv()
    for cp in sends:
        cp.wait_send()


def _gather_conv_w(cw_block):
    r, d = cw_block.shape

    def body(b_ref, o_ref, all_ref, send_sems, recv_sems):
        _gather_blocks(b_ref, all_ref, send_sems, recv_sems)
        o_ref[...] = (all_ref[0] + all_ref[2]) + (all_ref[4] + all_ref[6])

    vm = pl.BlockSpec(memory_space=pltpu.VMEM)
    return pl.pallas_call(
        body, name="gather_conv_w", in_specs=[vm], out_specs=vm, out_shape=_sds((r, d), F32),
        scratch_shapes=[pltpu.VMEM((8, r, d), F32), pltpu.SemaphoreType.DMA((7,)), pltpu.SemaphoreType.DMA((7,))],
    )(cw_block)


def _adam(w, g, m, v):
    m_new = ADAM_B1 * m + (1.0 - ADAM_B1) * g
    v_new = ADAM_B2 * v + (1.0 - ADAM_B2) * (g * g)
    m_hat = m_new / (1.0 - ADAM_B1 ** ADAM_STEP)
    v_hat = v_new / (1.0 - ADAM_B2 ** ADAM_STEP)
    delta = -ADAM_LR * (m_hat / (jnp.sqrt(v_hat) + ADAM_EPS) + ADAM_WD * w)
    return delta, m_new, v_new


def _small_step(dnm0, dnm1, dnf0, dnf1, dcw, dqg, dkg, dsk, loss, w_blk, m_blk, v_blk):
    d = w_blk.shape[1]

    def body(dnm0_ref, dnm1_ref, dnf0_ref, dnf1_ref, dcw_ref, dqg_ref, dkg_ref, dsk_ref, loss_ref,
             w_ref, m_ref, v_ref, g_ref, dl_ref, mo_ref, vo_ref, blk_ref, all_ref, send_sems, recv_sems):
        blk_ref[...] = jnp.zeros_like(blk_ref)
        blk_ref[0:1, :] = jnp.sum(dnm0_ref[...], axis=0, keepdims=True)
        blk_ref[1:2, :] = jnp.sum(dnm1_ref[...], axis=0, keepdims=True)
        blk_ref[8:9, :] = jnp.sum(dnf0_ref[...], axis=0, keepdims=True)
        blk_ref[9:10, :] = jnp.sum(dnf1_ref[...], axis=0, keepdims=True)
        blk_ref[16:19, :] = dcw_ref[...]
        dqg_v = dqg_ref[...]
        dkg_v = dkg_ref[...]
        blk_ref[24:25, 0:LANES] = dqg_v + pltpu.roll(dqg_v, HEAD_DIM, 1)
        blk_ref[24:25, LANES:2 * LANES] = dkg_v + pltpu.roll(dkg_v, HEAD_DIM, 1)
        blk_ref[24:25, 2 * LANES:3 * LANES] = dsk_ref[...]
        blk_ref[24:25, 3 * LANES:4 * LANES] = jnp.broadcast_to(loss_ref[...], (1, LANES))
        _gather_blocks(blk_ref, all_ref, send_sems, recv_sems)
        g = all_ref[0]
        for dev in range(1, 8):
            g = g + all_ref[dev]
        g_ref[...] = g
        delta, m_new, v_new = _adam(w_ref[...], g, m_ref[...], v_ref[...])
        dl_ref[...] = delta
        mo_ref[...] = m_new
        vo_ref[...] = v_new

    vm = pl.BlockSpec(memory_space=pltpu.VMEM)
    blk = _sds((SMALL_ROWS, d), F32)
    return pl.pallas_call(
        body, name="small_step", in_specs=[vm] * 12, out_specs=[vm] * 4, out_shape=[blk] * 4,
        scratch_shapes=[pltpu.VMEM((SMALL_ROWS, d), F32), pltpu.VMEM((8, SMALL_ROWS, d), F32),
                        pltpu.SemaphoreType.DMA((7,)), pltpu.SemaphoreType.DMA((7,))],
    )(dnm0, dnm1, dnf0, dnf1, dcw, dqg, dkg, dsk, loss, w_blk, m_blk, v_blk)


def _adam_step(name, w, g, m, v):
    nl, r, cdim = w.shape
    rt = _pick(r, (128, 64, 32))

    def body(w_ref, g_ref, m_ref, v_ref, d_ref, mo_ref, vo_ref):
        delta, m_new, v_new = _adam(w_ref[...], g_ref[...], m_ref[...], v_ref[...])
        d_ref[...] = delta
        mo_ref[...] = m_new
        vo_ref[...] = v_new

    spec = pl.BlockSpec((None, rt, cdim), lambda l, i: (l, i, 0))
    return pl.pallas_call(
        body, name=name, grid=(nl, r // rt), in_specs=[spec] * 4, out_specs=[spec] * 3,
        out_shape=[_sds(w.shape, F32)] * 3,
        compiler_params=_params(("parallel", "parallel")))(w, g, m, v)


def _pad_rows(a, rows=SUBLANES):
    return jnp.pad(a, ((0, rows - a.shape[0]), (0, 0)))


def _small_block(nm, nf, cw_local, qg, kg, sk, chip):
    d = nm.shape[1]
    cw_rows = lax.dynamic_update_slice(jnp.zeros((SUBLANES, d), F32), cw_local, (0, chip * cw_local.shape[1]))
    misc = jnp.concatenate([qg, qg, kg, kg, jnp.pad(sk, ((0, 0), (0, LANES - sk.shape[1]))),
                            jnp.zeros((1, d - 3 * LANES), F32)], axis=1)
    return jnp.concatenate([_pad_rows(nm), _pad_rows(nf), cw_rows, _pad_rows(misc)], axis=0)


def _unpack_small(blk, chip, cw_cols):
    cw = lax.dynamic_slice(blk[16:19], (0, chip * cw_cols), (3, cw_cols))[None]
    return dict(norm_mixer=blk[0:2], norm_ffn=blk[8:10], conv_w=cw, attn_q_gain=blk[24:25, 0:HEAD_DIM],
                attn_k_gain=blk[24:25, LANES:LANES + HEAD_DIM], attn_sinks=blk[24:25, 2 * LANES:2 * LANES + N_Q_HEADS])


WEIGHT_NAMES = ("conv_w_in", "conv_w", "conv_w_out", "attn_w_qkv", "attn_q_gain", "attn_k_gain", "attn_sinks",
                "attn_w_o", "norm_mixer", "norm_ffn", "ffn_w_gate_up", "ffn_w_down")
BIG = ("conv_w_in", "conv_w_out", "attn_w_qkv", "attn_w_o", "ffn_w_gate_up", "ffn_w_down")


def kernel(x, conv_w_in, conv_w, conv_w_out, attn_w_qkv, attn_q_gain, attn_k_gain, attn_sinks, attn_w_o, norm_mixer, norm_ffn, ffn_w_gate_up, ffn_w_down, loss_target, m_conv_w_in, m_conv_w, m_conv_w_out, m_attn_w_qkv, m_attn_q_gain, m_attn_k_gain, m_attn_sinks, m_attn_w_o, m_norm_mixer, m_norm_ffn, m_ffn_w_gate_up, m_ffn_w_down, v_conv_w_in, v_conv_w, v_conv_w_out, v_attn_w_qkv, v_attn_q_gain, v_attn_k_gain, v_attn_sinks, v_attn_w_o, v_norm_mixer, v_norm_ffn, v_ffn_w_gate_up, v_ffn_w_down):
    w = dict(conv_w_in=conv_w_in, conv_w=conv_w, conv_w_out=conv_w_out, attn_w_qkv=attn_w_qkv,
             attn_q_gain=attn_q_gain, attn_k_gain=attn_k_gain, attn_sinks=attn_sinks, attn_w_o=attn_w_o,
             norm_mixer=norm_mixer, norm_ffn=norm_ffn, ffn_w_gate_up=ffn_w_gate_up, ffn_w_down=ffn_w_down)
    m = dict(conv_w_in=m_conv_w_in, conv_w=m_conv_w, conv_w_out=m_conv_w_out, attn_w_qkv=m_attn_w_qkv,
             attn_q_gain=m_attn_q_gain, attn_k_gain=m_attn_k_gain, attn_sinks=m_attn_sinks, attn_w_o=m_attn_w_o,
             norm_mixer=m_norm_mixer, norm_ffn=m_norm_ffn, ffn_w_gate_up=m_ffn_w_gate_up, ffn_w_down=m_ffn_w_down)
    v = dict(conv_w_in=v_conv_w_in, conv_w=v_conv_w, conv_w_out=v_conv_w_out, attn_w_qkv=v_attn_w_qkv,
             attn_q_gain=v_attn_q_gain, attn_k_gain=v_attn_k_gain, attn_sinks=v_attn_sinks, attn_w_o=v_attn_w_o,
             norm_mixer=v_norm_mixer, norm_ffn=v_norm_ffn, ffn_w_gate_up=v_ffn_w_gate_up, ffn_w_down=v_ffn_w_down)

    nseq, seq, d = x.shape
    t = nseq * seq
    chip = 2 * lax.axis_index("x") + lax.axis_index("y")
    core = lax.axis_index("c")
    place = jnp.stack([core, chip]).astype(jnp.int32)
    x0 = x.reshape(t, d)
    tgt = loss_target.reshape(t, d)

    cw_block = lax.dynamic_update_slice(jnp.zeros((SUBLANES, d), F32), conv_w[0], (0, chip * conv_w.shape[2]))
    cw_full = _gather_conv_w(cw_block)[0:3]
    def cast(k, layer=None):
        return _cast_own(f"cast_{k}" + ("" if layer is None else str(layer)), w[k], place, layer)

    w_in, w_out = _allgather_weights([cast("conv_w_in"), cast("conv_w_out")])
    w_gu0, w_dn0 = _allgather_sequencer("allgather_ffn0", [cast("ffn_w_gate_up", 0), cast("ffn_w_down", 0)], 1)
    w_qkv, w_o, w_gu1, w_dn1 = _allgather_sequencer(
        "allgather_rest", [cast("attn_w_qkv"), cast("attn_w_o"), cast("ffn_w_gate_up", 1), cast("ffn_w_down", 1)], 2)
    w_out = w_out.reshape(1, d, d)
    w_o = w_o.reshape(1, d, d)
    w_gu = [w_gu0, w_gu1]
    w_dn = [w_dn0.reshape(1, D_FF, d), w_dn1.reshape(1, D_FF, d)]

    qg_pair = jnp.concatenate([attn_q_gain, attn_q_gain], axis=1)
    kg_pair = jnp.concatenate([attn_k_gain, attn_k_gain], axis=1)

    def ffn_bwd(i, dxo, xin, h, g, u, a):
        g_dn = _wgrad_down(f"ffn{i}_down_wgrad", a, dxo, D_FF // 2)
        dg, du = _mm_down_t_swiglu(f"ffn{i}_down_dgrad", dxo, w_dn[i], 0, g, u)
        g_gu = _wgrad_up2(f"ffn{i}_up_wgrad", h, dg, du)
        dxi, dgain = _dgrad_norm_ffn(f"ffn{i}_up_dgrad", dg, du, w_gu[i], 0, xin, norm_ffn[i:i + 1], dxo)
        return dxi, dgain, g_gu, g_dn

    h0 = _rms_fwd("conv_norm", x0, norm_mixer[0:1])
    bcx = _mm_up_joined("conv_in", h0, w_in, 512)
    z = _conv_fwd(bcx, cw_full, nseq, seq)
    x1, h1 = _mm_down_norm("conv_out", z, w_out, 0, x0, norm_ffn[0:1])
    g0, u0, a0 = _mm_up_swiglu("ffn0_up", h1, w_gu[0], 0)
    x2, h2 = _mm_down_norm("ffn0_down", a0, w_dn[0], 0, x1, norm_mixer[1:2])
    qkv = _mm_up_joined("attn_qkv", h2, w_qkv, 1024)
    o = _attn_fwd(qkv, qg_pair, kg_pair, attn_sinks, nseq, seq)
    x3, h3 = _mm_down_norm("attn_out", o, w_o, 0, x2, norm_ffn[1:2])
    g1, u1, a1 = _mm_up_swiglu("ffn1_up", h3, w_gu[1], 0)
    dy, loss_part = _mm_down_loss("ffn1_down", a1, w_dn[1], 0, x3, tgt)

    dx3, dnf1, g_gu1, g_dn1 = ffn_bwd(1, dy, x3, h3, g1, u1, a1)
    g_o = _wgrad_down("attn_out_wgrad", o, dx3, d)
    do = _mm_down_t("attn_out_dgrad", dx3, w_o, 0)
    dqkv, dqg, dkg, dsk = _attn_bwd(do, qkv, qg_pair, kg_pair, attn_sinks, nseq, seq)
    g_qkv = _wgrad_joined("attn_qkv_wgrad", h2, dqkv)
    dx2, dnm1 = _dgrad_norm_qkv("attn_qkv_dgrad", dqkv, w_qkv, x2, norm_mixer[1:2], dx3)
    dx1, dnf0, g_gu0, g_dn0 = ffn_bwd(0, dx2, x1, h1, g0, u0, a0)
    g_out = _wgrad_down("conv_out_wgrad", z, dx1, d)
    dz = _mm_down_t("conv_out_dgrad", dx1, w_out, 0)
    dbcx, dcw = _conv_bwd(dz, bcx, cw_full, nseq, seq)
    g_in = _wgrad_conv_in("conv_in_wgrad", h0, dbcx, conv_w_in.shape[2])
    dx0, dnm0 = _dgrad_norm_conv("conv_in_dgrad", dbcx, w_in, x0, norm_mixer[0:1], dx1)

    units = [g_in, g_out, g_qkv, g_o, g_gu0, g_gu1, g_dn0, g_dn1]
    unit_names = ["in", "out", "qkv", "o", "gu0", "gu1", "dn0", "dn1"]
    layers = [(0, 0), (1, 0), (2, 0), (3, 0), (4, 0), (4, 1), (5, 0), (5, 1)]
    got = _exchange_halves(units)
    chip_sums = [_sum_halves(f"sum_halves_{nm}", g, r, place) for nm, g, r in zip(unit_names, units, got)]
    arrived = _scatter_partials([pb for pb, _ in chip_sums])
    finished = [None] * len(BIG)
    for nm, (wi, l), (_, pf), r in zip(unit_names, layers, chip_sums, arrived):
        finished[wi] = _sum_partials(f"sum_partials_{nm}", pf, r, place, l, w[BIG[wi]].shape[0], finished[wi])
    grads_big = _share_halves(finished)

    grad, delta, new_m, new_v = {}, {}, {}, {}
    for k, g in zip(BIG, grads_big):
        grad[k] = g
        delta[k], new_m[k], new_v[k] = _adam_step(f"adam_{k}", w[k], g, m[k], v[k])

    def blocks(src):
        return _small_block(src["norm_mixer"], src["norm_ffn"], src["conv_w"][0], src["attn_q_gain"],
                            src["attn_k_gain"], src["attn_sinks"], chip)

    g_blk, d_blk, m_blk, v_blk = _small_step(dnm0, dnm1, dnf0, dnf1, dcw, dqg, dkg, dsk, loss_part,
                                             blocks(w), blocks(m), blocks(v))
    cw_cols = conv_w.shape[2]
    for dst, blk in ((grad, g_blk), (delta, d_blk), (new_m, m_blk), (new_v, v_blk)):
        dst.update(_unpack_small(blk, chip, cw_cols))
    loss = g_blk[24, 3 * LANES]

    return (loss, dx0.reshape(nseq, seq, d), *[grad[k] for k in WEIGHT_NAMES], *[delta[k] for k in WEIGHT_NAMES],
            *[new_m[k] for k in WEIGHT_NAMES], *[new_v[k] for k in WEIGHT_NAMES])
```

```python
import jax
import jax.numpy as jnp
from jax import lax
from jax.experimental import pallas as pl
from jax.experimental.pallas import tpu as pltpu
from jax.experimental.pallas import tpu_sc as plsc

F32 = jnp.float32
BF16 = jnp.bfloat16

D_MODEL = 1024
D_FF = 2816
N_Q_HEADS = 16
N_KV_HEADS = 4
HEAD_DIM = 64
WINDOW = 128
BLOCK = 128
EPS = 1e-6
N_CHIPS = 4
LANES = 128
SUBLANES = 8
BF16_ROWS = 16
MXU_COLS = 256
VMEM_LIMIT = 48 * 1024 * 1024
ADAM_LR, ADAM_B1, ADAM_B2, ADAM_EPS, ADAM_WD, ADAM_STEP = 0.001, 0.9, 0.999, 1e-08, 0.01, 10
ALIBI_SLOPES = tuple(2.0 ** (-8.0 * (h + 1) / N_Q_HEADS) for h in range(N_Q_HEADS))
SMALL_ROWS = 32
MESH = pl.DeviceIdType.MESH

NN = ((1,), (0,))
NT = ((1,), (1,))
TN = ((0,), (0,))


def _dot(a, b, dims):
    return lax.dot_general(a, b, (dims, ((), ())), preferred_element_type=F32)


def _pick(n, cands):
    for c in cands:
        if n % c == 0:
            return c
    raise ValueError((n, cands))


def _params(sem):
    return pltpu.CompilerParams(dimension_semantics=sem, vmem_limit_bytes=VMEM_LIMIT)


def _sds(shape, dtype):
    return jax.ShapeDtypeStruct(shape, dtype)


def _rms(xv):
    return lax.rsqrt(jnp.mean(xv * xv, axis=-1, keepdims=True) + EPS)


def _sigmoid(g):
    return 1.0 / (1.0 + jnp.exp(-g))


def _mm_up_joined(name, a, w4, tm_pref):
    t, k = a.shape
    _, _, _, nq = w4.shape
    tm = _pick(t, (tm_pref, 256, 128))

    def body(a_ref, w_ref, o_ref, wcat_ref):
        @pl.when(pl.program_id(0) == 0)
        def _():
            for q in range(N_CHIPS):
                wcat_ref[:, q * nq:(q + 1) * nq] = w_ref[q]

        o_ref[...] = _dot(a_ref[...], wcat_ref[...], NN).astype(BF16)

    return pl.pallas_call(
        body, name=name, grid=(t // tm,),
        in_specs=[pl.BlockSpec((tm, k), lambda i: (i, 0)),
                  pl.BlockSpec((None, N_CHIPS, k, nq), lambda i: (0, 0, 0, 0))],
        out_specs=pl.BlockSpec((tm, N_CHIPS * nq), lambda i: (i, 0)),
        out_shape=_sds((t, N_CHIPS * nq), BF16),
        scratch_shapes=[pltpu.VMEM((k, N_CHIPS * nq), BF16)],
        compiler_params=_params(("arbitrary",)))(a, w4)


def _mm_up_swiglu(name, h, w4, layer):
    t, k = h.shape
    _, _, _, nq = w4.shape
    tm = _pick(t, (512, 256, 128))

    def body(h_ref, wg_ref, wu_ref, g_ref, u_ref, a_ref):
        hv = h_ref[...]
        g = _dot(hv, wg_ref[...], NN)
        u = _dot(hv, wu_ref[...], NN)
        g_ref[...] = g.astype(BF16)
        u_ref[...] = u.astype(BF16)
        a_ref[...] = (g * _sigmoid(g) * u).astype(BF16)

    half = N_CHIPS // 2
    out = pl.BlockSpec((tm, nq), lambda j, i: (i, j))
    return pl.pallas_call(
        body, name=name, grid=(half, t // tm),
        in_specs=[pl.BlockSpec((tm, k), lambda j, i: (i, 0)),
                  pl.BlockSpec((None, None, k, nq), lambda j, i: (layer, j, 0, 0)),
                  pl.BlockSpec((None, None, k, nq), lambda j, i: (layer, half + j, 0, 0))],
        out_specs=[out, out, out],
        out_shape=[_sds((t, half * nq), BF16)] * 3,
        compiler_params=_params(("parallel", "parallel")))(h, w4, w4)


def _mm_down_norm(name, a, w, layer, res, gain):
    t, kf = a.shape
    _, _, n = w.shape
    tm = _pick(t, (512, 256, 128))

    def body(a_ref, w_ref, r_ref, g_ref, o_ref, h_ref):
        xo = r_ref[...] + _dot(a_ref[...], w_ref[...], NN)
        o_ref[...] = xo
        h_ref[...] = ((xo * _rms(xo)) * g_ref[...]).astype(BF16)

    row = pl.BlockSpec((tm, n), lambda i: (i, 0))
    return pl.pallas_call(
        body, name=name, grid=(t // tm,),
        in_specs=[pl.BlockSpec((tm, kf), lambda i: (i, 0)),
                  pl.BlockSpec((None, kf, n), lambda i: (layer, 0, 0)),
                  row, pl.BlockSpec((1, n), lambda i: (0, 0))],
        out_specs=[row, row],
        out_shape=[_sds((t, n), F32), _sds((t, n), BF16)],
        compiler_params=_params(("parallel",)))(a, w, res, gain)


def _mm_down_loss(name, a, w, layer, res, tgt):
    t, kf = a.shape
    _, _, n = w.shape
    tm = _pick(t, (512, 256, 128))
    steps = t // tm

    def body(a_ref, w_ref, r_ref, t_ref, dy_ref, l_ref, acc_ref):
        i = pl.program_id(0)

        @pl.when(i == 0)
        def _():
            acc_ref[...] = jnp.zeros_like(acc_ref)

        e = (r_ref[...] + _dot(a_ref[...], w_ref[...], NN)) - t_ref[...]
        dy_ref[...] = e * (1.0 / n)
        acc_ref[...] += (e * e).reshape(tm // SUBLANES, SUBLANES, n).sum(axis=0)

        @pl.when(i == steps - 1)
        def _():
            l_ref[...] = jnp.sum(acc_ref[...], keepdims=True) * (0.5 / n)

    row = pl.BlockSpec((tm, n), lambda i: (i, 0))
    return pl.pallas_call(
        body, name=name, grid=(steps,),
        in_specs=[pl.BlockSpec((tm, kf), lambda i: (i, 0)),
                  pl.BlockSpec((None, kf, n), lambda i: (layer, 0, 0)), row, row],
        out_specs=[row, pl.BlockSpec((1, 1), lambda i: (0, 0))],
        out_shape=[_sds((t, n), F32), _sds((1, 1), F32)],
        scratch_shapes=[pltpu.VMEM((SUBLANES, n), F32)],
        compiler_params=_params(("arbitrary",)))(a, w, res, tgt)


def _mm_down_t(name, dx, w, layer):
    t, n = dx.shape
    _, kf, _ = w.shape
    tm = _pick(t, (512, 256, 128))

    def body(a_ref, w_ref, o_ref):
        o_ref[...] = _dot(a_ref[...].astype(BF16), w_ref[...], NT).astype(BF16)

    return pl.pallas_call(
        body, name=name, grid=(t // tm,),
        in_specs=[pl.BlockSpec((tm, n), lambda i: (i, 0)),
                  pl.BlockSpec((None, kf, n), lambda i: (layer, 0, 0))],
        out_specs=pl.BlockSpec((tm, kf), lambda i: (i, 0)),
        out_shape=_sds((t, kf), BF16),
        compiler_params=_params(("parallel",)))(dx, w)


def _mm_down_t_swiglu(name, dx, w, layer, g, u):
    t, n = dx.shape
    f = g.shape[1]
    tn = f // 2
    tm = _pick(t, (512, 256, 128))

    def body(a_ref, w_ref, g_ref, u_ref, dg_ref, du_ref):
        da = _dot(a_ref[...].astype(BF16), w_ref[...], NT)
        gv = g_ref[...].astype(F32)
        sg = _sigmoid(gv)
        dg_ref[...] = (da * u_ref[...].astype(F32) * (sg * (1.0 + gv * (1.0 - sg)))).astype(BF16)
        du_ref[...] = (da * (gv * sg)).astype(BF16)

    tile = pl.BlockSpec((tm, tn), lambda j, i: (i, j))
    return pl.pallas_call(
        body, name=name, grid=(f // tn, t // tm),
        in_specs=[pl.BlockSpec((tm, n), lambda j, i: (i, 0)),
                  pl.BlockSpec((None, tn, n), lambda j, i: (layer, j, 0)), tile, tile],
        out_specs=[tile, tile],
        out_shape=[_sds((t, f), BF16)] * 2,
        compiler_params=_params(("parallel", "parallel")))(dx, w, g, u)


def _dgrad_norm(name, acts, act_blocks, pieces, w4, layer, x, gain, dres):
    t, d = x.shape
    _, _, k, nq = w4.shape
    tm = _pick(t, (256, 128))
    n_act = len(acts)

    def body(*refs):
        act_refs = refs[:n_act]
        w_ref, x_ref, g_ref, dr_ref, dx_ref, dg_ref = refs[n_act:]

        @pl.when(pl.program_id(0) == 0)
        def _():
            dg_ref[...] = jnp.zeros_like(dg_ref)

        dh = None
        for a_tile, w_tile in pieces(act_refs, w_ref):
            term = _dot(a_tile, w_tile, NT)
            dh = term if dh is None else dh + term
        xv = x_ref[...]
        r = _rms(xv)
        xhat = xv * r
        gd = dh * g_ref[...]
        dx_ref[...] = dr_ref[...] + r * (gd - xhat * jnp.mean(gd * xhat, axis=-1, keepdims=True))
        dg_ref[...] += (dh * xhat).reshape(tm // SUBLANES, SUBLANES, d).sum(axis=0)

    row = pl.BlockSpec((tm, d), lambda i: (i, 0))
    return pl.pallas_call(
        body, name=name, grid=(t // tm,),
        in_specs=[*act_blocks(tm),
                  pl.BlockSpec((None, N_CHIPS, k, nq), lambda i: (layer, 0, 0, 0)),
                  row, pl.BlockSpec((1, d), lambda i: (0, 0)), row],
        out_specs=[row, pl.BlockSpec((SUBLANES, d), lambda i: (0, 0))],
        out_shape=[_sds((t, d), F32), _sds((SUBLANES, d), F32)],
        compiler_params=_params(("arbitrary",)))(*acts, w4, x, gain, dres)


def _dgrad_norm_ffn(name, dg, du, w4, layer, x, gain, dres):
    nq = w4.shape[3]
    f = dg.shape[1]

    def blocks(tm):
        return [pl.BlockSpec((tm, f), lambda i: (i, 0))] * 2

    def pieces(act_refs, w_ref):
        dg_ref, du_ref = act_refs
        return [(dg_ref[:, 0:nq], w_ref[0]), (dg_ref[:, nq:2 * nq], w_ref[1]),
                (du_ref[:, 0:nq], w_ref[2]), (du_ref[:, nq:2 * nq], w_ref[3])]

    return _dgrad_norm(name, [dg, du], blocks, pieces, w4, layer, x, gain, dres)


def _dgrad_norm_qkv(name, dqkv, w4, x, gain, dres):
    nq = w4.shape[3]

    def blocks(tm):
        return [pl.BlockSpec((tm, N_CHIPS * nq), lambda i: (i, 0))]

    def pieces(act_refs, w_ref):
        return [(act_refs[0][:, q * nq:(q + 1) * nq], w_ref[q]) for q in range(N_CHIPS)]

    return _dgrad_norm(name, [dqkv], blocks, pieces, w4, 0, x, gain, dres)


def _dgrad_norm_conv(name, d3, w4, x, gain, dres):
    _, _, d = d3.shape
    nq = w4.shape[3]
    per_part, per_q = d // MXU_COLS, nq // MXU_COLS

    def blocks(tm):
        return [pl.BlockSpec((3, tm, d), lambda i: (0, i, 0))]

    def pieces(act_refs, w_ref):
        out = []
        for jb in range(3 * per_part):
            ca, cw = (jb % per_part) * MXU_COLS, (jb % per_q) * MXU_COLS
            out.append((act_refs[0][jb // per_part, :, ca:ca + MXU_COLS], w_ref[jb // per_q, :, cw:cw + MXU_COLS]))
        return out

    return _dgrad_norm(name, [d3], blocks, pieces, w4, 0, x, gain, dres)


def _wgrad_up2(name, h, dg, du):
    t, k = h.shape
    nq = dg.shape[1] // 2
    tk = _pick(t, (512, 256, 128))
    steps = t // tk
    half = N_CHIPS // 2

    def body(h_ref, dg_ref, du_ref, o_ref):
        q = pl.program_id(0)

        @pl.when(pl.program_id(1) == 0)
        def _():
            o_ref[...] = jnp.zeros_like(o_ref)

        @pl.when(q < half)
        def _():
            o_ref[...] += _dot(h_ref[...], dg_ref[...], TN)

        @pl.when(q >= half)
        def _():
            o_ref[...] += _dot(h_ref[...], du_ref[...], TN)

    return pl.pallas_call(
        body, name=name, grid=(N_CHIPS, steps),
        in_specs=[pl.BlockSpec((tk, k), lambda q, s: (s, 0)),
                  pl.BlockSpec((tk, nq), lambda q, s: (jnp.where(q < half, s, steps - 1), jnp.minimum(q, half - 1))),
                  pl.BlockSpec((tk, nq), lambda q, s: (jnp.where(q >= half, s, 0), jnp.maximum(q - half, 0)))],
        out_specs=pl.BlockSpec((None, k, nq), lambda q, s: (q, 0, 0)),
        out_shape=_sds((N_CHIPS, k, nq), F32),
        compiler_params=_params(("parallel", "arbitrary")))(h, dg, du)


def _wgrad_joined(name, h, dy):
    t, k = h.shape
    nq = dy.shape[1] // N_CHIPS
    tk = _pick(t, (1024, 512, 256, 128))

    def body(h_ref, dy_ref, o_ref):
        @pl.when(pl.program_id(0) == 0)
        def _():
            o_ref[...] = jnp.zeros_like(o_ref)

        res = _dot(h_ref[...], dy_ref[...], TN)
        for q in range(N_CHIPS):
            o_ref[q] += res[:, q * nq:(q + 1) * nq]

    return pl.pallas_call(
        body, name=name, grid=(t // tk,),
        in_specs=[pl.BlockSpec((tk, k), lambda s: (s, 0)), pl.BlockSpec((tk, N_CHIPS * nq), lambda s: (s, 0))],
        out_specs=pl.BlockSpec((N_CHIPS, k, nq), lambda s: (0, 0, 0)),
        out_shape=_sds((N_CHIPS, k, nq), F32),
        compiler_params=_params(("arbitrary",)))(h, dy)


def _wgrad_conv_in(name, h, d3, nq):
    t, k = h.shape
    d = d3.shape[2]
    per_part, per_q = d // MXU_COLS, nq // MXU_COLS
    tk = _pick(t, (512, 256, 128))

    def body(h_ref, d_ref, o_ref):
        @pl.when(pl.program_id(0) == 0)
        def _():
            o_ref[...] = jnp.zeros_like(o_ref)

        hv = h_ref[...]
        for part in range(3):
            res = _dot(hv, d_ref[part], TN)
            for cc in range(per_part):
                jb = part * per_part + cc
                co = (jb % per_q) * MXU_COLS
                o_ref[jb // per_q, :, co:co + MXU_COLS] += res[:, cc * MXU_COLS:(cc + 1) * MXU_COLS]

    return pl.pallas_call(
        body, name=name, grid=(t // tk,),
        in_specs=[pl.BlockSpec((tk, k), lambda s: (s, 0)), pl.BlockSpec((3, tk, d), lambda s: (0, s, 0))],
        out_specs=pl.BlockSpec((N_CHIPS, k, nq), lambda s: (0, 0, 0)),
        out_shape=_sds((N_CHIPS, k, nq), F32),
        compiler_params=_params(("arbitrary",)))(h, d3)


def _wgrad_down(name, a, dx, tmw):
    t, kf = a.shape
    n = dx.shape[1]
    tk = _pick(t, (512, 256, 128))

    def body(a_ref, b_ref, o_ref):
        @pl.when(pl.program_id(1) == 0)
        def _():
            o_ref[...] = jnp.zeros_like(o_ref)

        o_ref[...] += _dot(a_ref[...], b_ref[...].astype(BF16), TN)

    g = pl.pallas_call(
        body, name=name, grid=(kf // tmw, t // tk),
        in_specs=[pl.BlockSpec((tk, tmw), lambda j, s: (s, j)), pl.BlockSpec((tk, n), lambda j, s: (s, 0))],
        out_specs=pl.BlockSpec((tmw, n), lambda j, s: (j, 0)),
        out_shape=_sds((kf, n), F32),
        compiler_params=_params(("parallel", "arbitrary")))(a, dx)
    return g.reshape(N_CHIPS, kf // N_CHIPS, n)


def _rms_fwd(name, x, gain):
    t, d = x.shape
    tm = _pick(t, (512, 256, 128))

    def body(x_ref, g_ref, h_ref):
        xv = x_ref[...]
        h_ref[...] = ((xv * _rms(xv)) * g_ref[...]).astype(BF16)

    return pl.pallas_call(
        body, name=name, grid=(t // tm,),
        in_specs=[pl.BlockSpec((tm, d), lambda i: (i, 0)), pl.BlockSpec((1, d), lambda i: (0, 0))],
        out_specs=pl.BlockSpec((tm, d), lambda i: (i, 0)),
        out_shape=_sds((t, d), BF16),
        compiler_params=_params(("parallel",)))(x, gain)


def _shift_rows(u, k, rows):
    s = u.shape[0]
    if k > 0:
        return jnp.where(rows >= k, pltpu.roll(u, k, 0), 0.0)
    return jnp.where(rows < s + k, pltpu.roll(u, s + k, 0), 0.0)


def _conv_fwd(bcx, cw, nseq, seq):
    t, d3 = bcx.shape
    d = d3 // 3
    cb = MXU_COLS
    nj = d // cb

    def body(b_ref, c_ref, x_ref, cw_ref, z_ref):
        u = b_ref[...].astype(F32) * x_ref[...].astype(F32)
        rows = lax.broadcasted_iota(jnp.int32, u.shape, 0)
        cwv = cw_ref[...]
        y = cwv[2:3] * u + cwv[1:2] * _shift_rows(u, 1, rows) + cwv[0:1] * _shift_rows(u, 2, rows)
        z_ref[...] = (c_ref[...].astype(F32) * y).astype(BF16)

    return pl.pallas_call(
        body, name="conv_fwd", grid=(nseq, nj),
        in_specs=[pl.BlockSpec((seq, cb), lambda b, j: (b, j)),
                  pl.BlockSpec((seq, cb), lambda b, j: (b, nj + j)),
                  pl.BlockSpec((seq, cb), lambda b, j: (b, 2 * nj + j)),
                  pl.BlockSpec((3, cb), lambda b, j: (0, j))],
        out_specs=pl.BlockSpec((seq, cb), lambda b, j: (b, j)),
        out_shape=_sds((t, d), BF16),
        compiler_params=_params(("parallel", "parallel")))(bcx, bcx, bcx, cw)


def _conv_bwd(dz, bcx, cw, nseq, seq):
    t, d3 = bcx.shape
    d = d3 // 3
    cb = MXU_COLS
    nj = d // cb

    def body(dz_ref, b_ref, c_ref, x_ref, cw_ref, o_ref, dcw_ref):
        @pl.when(pl.program_id(1) == 0)
        def _():
            dcw_ref[...] = jnp.zeros_like(dcw_ref)

        b = b_ref[...].astype(F32)
        c = c_ref[...].astype(F32)
        xv = x_ref[...].astype(F32)
        dzv = dz_ref[...].astype(F32)
        u = b * xv
        rows = lax.broadcasted_iota(jnp.int32, u.shape, 0)
        u1 = _shift_rows(u, 1, rows)
        u2 = _shift_rows(u, 2, rows)
        cwv = cw_ref[...]
        y = cwv[2:3] * u + cwv[1:2] * u1 + cwv[0:1] * u2
        dyc = dzv * c
        du = cwv[2:3] * dyc + cwv[1:2] * _shift_rows(dyc, -1, rows) + cwv[0:1] * _shift_rows(dyc, -2, rows)
        o_ref[0] = (du * xv).astype(BF16)
        o_ref[1] = (dzv * y).astype(BF16)
        o_ref[2] = (du * b).astype(BF16)
        s0 = jnp.sum(dyc * u2, axis=0, keepdims=True)
        s1 = jnp.sum(dyc * u1, axis=0, keepdims=True)
        s2 = jnp.sum(dyc * u, axis=0, keepdims=True)
        tap = lax.broadcasted_iota(jnp.int32, (3, cb), 0)
        dcw_ref[...] += jnp.where(tap == 0, s0, jnp.where(tap == 1, s1, s2))

    return pl.pallas_call(
        body, name="conv_bwd", grid=(nj, nseq),
        in_specs=[pl.BlockSpec((seq, cb), lambda j, b: (b, j)),
                  pl.BlockSpec((seq, cb), lambda j, b: (b, j)),
                  pl.BlockSpec((seq, cb), lambda j, b: (b, nj + j)),
                  pl.BlockSpec((seq, cb), lambda j, b: (b, 2 * nj + j)),
                  pl.BlockSpec((3, cb), lambda j, b: (0, j))],
        out_specs=[pl.BlockSpec((3, seq, cb), lambda j, b: (0, b, j)),
                   pl.BlockSpec((3, cb), lambda j, b: (0, j))],
        out_shape=[_sds((3, t, d), BF16), _sds((3, d), F32)],
        compiler_params=_params(("parallel", "arbitrary")))(dz, bcx, bcx, bcx, cw)


def _pair_norm(x, gain_pair, low):
    sq = x * x
    ss_lo = jnp.sum(jnp.where(low, sq, 0.0), axis=-1, keepdims=True)
    ss_hi = jnp.sum(jnp.where(low, 0.0, sq), axis=-1, keepdims=True)
    r = lax.rsqrt(jnp.where(low, ss_lo, ss_hi) * (1.0 / HEAD_DIM) + EPS)
    xhat = x * r
    return xhat * gain_pair, xhat, r


def _softmax_sink(qn_b, k_b, slope, sink, distf, mask):
    s = _dot(qn_b, k_b, NT) * (1.0 / (HEAD_DIM ** 0.5)) - slope * distf
    s = jnp.where(mask, s, -1e30)
    m = jnp.maximum(jnp.max(s, axis=-1, keepdims=True), sink)
    e = jnp.exp(s - m)
    es = jnp.exp(sink - m)
    inv = 1.0 / (jnp.sum(e, axis=-1, keepdims=True) + es)
    return e * inv, es * inv


def _kv_pairs(kv_tile, parity, low):
    own = jnp.where(low if parity == 0 else jnp.logical_not(low), kv_tile, 0.0)
    other = pltpu.roll(own, HEAD_DIM, 1)
    return (own, other) if parity == 0 else (other, own)


def _attn_geometry(n):
    q0 = pl.multiple_of(n * BLOCK, BLOCK)
    k0 = pl.multiple_of(jnp.maximum(n - 1, 0) * BLOCK, BLOCK)
    qi = lax.broadcasted_iota(jnp.int32, (BLOCK, 2 * BLOCK), 0)
    kj = lax.broadcasted_iota(jnp.int32, (BLOCK, 2 * BLOCK), 1)
    dist = (q0 - k0) + qi - kj
    mask = jnp.logical_and(dist >= 0, dist < WINDOW)
    return q0, k0, dist.astype(F32), mask


def _attn_fwd(qkv, qg_pair, kg_pair, sinks, nseq, seq):
    t = qkv.shape[0]
    dq = N_Q_HEADS * HEAD_DIM
    dkv = N_KV_HEADS * HEAD_DIM

    def body(sk_ref, qkv_ref, qg_ref, kg_ref, o_ref):
        low = lax.broadcasted_iota(jnp.int32, (1, LANES), 1) < HEAD_DIM
        qg = qg_ref[...]
        kg = kg_ref[...]

        def blk(n, carry):
            q0, k0, distf, mask = _attn_geometry(n)
            for kt in range(dkv // LANES):
                kraw = qkv_ref[pl.ds(k0, 2 * BLOCK), dq + kt * LANES:dq + (kt + 1) * LANES].astype(F32)
                vraw = qkv_ref[pl.ds(k0, 2 * BLOCK), dq + dkv + kt * LANES:dq + dkv + (kt + 1) * LANES].astype(F32)
                kn, _, _ = _pair_norm(kraw, kg, low)
                for par in range(2):
                    kh = 2 * kt + par
                    k_lo, k_hi = [v.astype(BF16) for v in _kv_pairs(kn, par, low)]
                    v_lo, v_hi = [v.astype(BF16) for v in _kv_pairs(vraw, par, low)]
                    for jj in range(2):
                        j = 2 * kh + jj
                        qraw = qkv_ref[pl.ds(q0, BLOCK), j * LANES:(j + 1) * LANES].astype(F32)
                        qn, _, _ = _pair_norm(qraw, qg, low)
                        qn_b = qn.astype(BF16)
                        p0, _ = _softmax_sink(qn_b, k_lo, ALIBI_SLOPES[2 * j], sk_ref[0, 2 * j], distf, mask)
                        p1, _ = _softmax_sink(qn_b, k_hi, ALIBI_SLOPES[2 * j + 1], sk_ref[0, 2 * j + 1], distf, mask)
                        o = _dot(p0.astype(BF16), v_lo, NN) + _dot(p1.astype(BF16), v_hi, NN)
                        o_ref[pl.ds(q0, BLOCK), j * LANES:(j + 1) * LANES] = o.astype(BF16)
            return carry

        lax.fori_loop(0, seq // BLOCK, blk, 0)

    return pl.pallas_call(
        body, name="attn_fwd", grid=(nseq,),
        in_specs=[pl.BlockSpec(memory_space=pltpu.SMEM),
                  pl.BlockSpec((seq, dq + 2 * dkv), lambda b: (b, 0)),
                  pl.BlockSpec((1, LANES), lambda b: (0, 0)),
                  pl.BlockSpec((1, LANES), lambda b: (0, 0))],
        out_specs=pl.BlockSpec((seq, dq), lambda b: (b, 0)),
        out_shape=_sds((t, dq), BF16),
        compiler_params=_params(("parallel",)))(sinks, qkv, qg_pair, kg_pair)


def _attn_bwd(do, qkv, qg_pair, kg_pair, sinks, nseq, seq):
    t = qkv.shape[0]
    dq = N_Q_HEADS * HEAD_DIM
    dkv = N_KV_HEADS * HEAD_DIM
    scale = 1.0 / (HEAD_DIM ** 0.5)

    def body(sk_ref, do_ref, qkv_ref, qg_ref, kg_ref, o_ref, dqg_ref, dkg_ref, dsk_ref, acc_ref):
        @pl.when(pl.program_id(0) == 0)
        def _():
            dqg_ref[...] = jnp.zeros_like(dqg_ref)
            dkg_ref[...] = jnp.zeros_like(dkg_ref)
            dsk_ref[...] = jnp.zeros_like(dsk_ref)

        acc_ref[...] = jnp.zeros_like(acc_ref)
        low = lax.broadcasted_iota(jnp.int32, (1, LANES), 1) < HEAD_DIM
        lane = lax.broadcasted_iota(jnp.int32, (1, LANES), 1)
        qg = qg_ref[...]
        kg = kg_ref[...]

        def blk(n, carry):
            dqg_acc, dkg_acc, dsk_acc = carry
            q0, k0, distf, mask = _attn_geometry(n)
            for kt in range(dkv // LANES):
                kraw = qkv_ref[pl.ds(k0, 2 * BLOCK), dq + kt * LANES:dq + (kt + 1) * LANES].astype(F32)
                vraw = qkv_ref[pl.ds(k0, 2 * BLOCK), dq + dkv + kt * LANES:dq + dkv + (kt + 1) * LANES].astype(F32)
                kn, khat, rk = _pair_norm(kraw, kg, low)
                dk_tile = None
                dv_tile = None
                for par in range(2):
                    kh = 2 * kt + par
                    own = low if par == 0 else jnp.logical_not(low)
                    k_lo, k_hi = [v.astype(BF16) for v in _kv_pairs(kn, par, low)]
                    v_lo, v_hi = [v.astype(BF16) for v in _kv_pairs(vraw, par, low)]
                    dkn_acc = jnp.zeros((2 * BLOCK, LANES), F32)
                    dv_acc = jnp.zeros((2 * BLOCK, LANES), F32)
                    for jj in range(2):
                        j = 2 * kh + jj
                        qraw = qkv_ref[pl.ds(q0, BLOCK), j * LANES:(j + 1) * LANES].astype(F32)
                        qn, qhat, rq = _pair_norm(qraw, qg, low)
                        qn_b = qn.astype(BF16)
                        do_b = do_ref[pl.ds(q0, BLOCK), j * LANES:(j + 1) * LANES]
                        dqn = None
                        dkn_pair = []
                        dv_pair = []
                        for e, (k_e, v_e) in enumerate(((k_lo, v_lo), (k_hi, v_hi))):
                            h = 2 * j + e
                            p, ps = _softmax_sink(qn_b, k_e, ALIBI_SLOPES[h], sk_ref[0, h], distf, mask)
                            dp = _dot(do_b, v_e, NT)
                            dsum = jnp.sum(p * dp, axis=-1, keepdims=True)
                            ds_b = ((p * (dp - dsum)) * scale).astype(BF16)
                            dsk_acc = dsk_acc - jnp.where(lane == h, jnp.sum(ps * dsum, axis=0, keepdims=True), 0.0)
                            term = _dot(ds_b, k_e, NN)
                            dqn = term if dqn is None else dqn + term
                            dkn_pair.append(_dot(ds_b, qn_b, TN))
                            dv_pair.append(_dot(p.astype(BF16), do_b, TN))
                        dkn_acc = dkn_acc + jnp.where(low, dkn_pair[0], dkn_pair[1])
                        dv_acc = dv_acc + jnp.where(low, dv_pair[0], dv_pair[1])
                        dqg_acc = dqg_acc + jnp.sum(dqn * qhat, axis=0, keepdims=True)
                        dqhat = dqn * qg
                        prod = dqhat * qhat
                        m_lo = jnp.sum(jnp.where(low, prod, 0.0), axis=-1, keepdims=True)
                        m_hi = jnp.sum(jnp.where(low, 0.0, prod), axis=-1, keepdims=True)
                        mean = jnp.where(low, m_lo, m_hi) * (1.0 / HEAD_DIM)
                        o_ref[pl.ds(q0, BLOCK), j * LANES:(j + 1) * LANES] = (rq * (dqhat - qhat * mean)).astype(BF16)
                    dkn = dkn_acc + pltpu.roll(dkn_acc, HEAD_DIM, 1)
                    dvh = dv_acc + pltpu.roll(dv_acc, HEAD_DIM, 1)
                    khat_own = jnp.where(own, khat, 0.0)
                    khat_dup = khat_own + pltpu.roll(khat_own, HEAD_DIM, 1)
                    dkg_acc = dkg_acc + jnp.sum(jnp.where(own, dkn * khat_dup, 0.0), axis=0, keepdims=True)
                    dkhat = dkn * kg
                    mean_k = jnp.sum(dkhat * khat_dup, axis=-1, keepdims=True) * (1.0 / LANES)
                    dk_raw = rk * (dkhat - khat_dup * mean_k)
                    dk_tile = jnp.where(own, dk_raw, 0.0) if dk_tile is None else jnp.where(own, dk_raw, dk_tile)
                    dv_tile = jnp.where(own, dvh, 0.0) if dv_tile is None else jnp.where(own, dvh, dv_tile)
                acc_ref[pl.ds(k0, 2 * BLOCK), kt * LANES:(kt + 1) * LANES] += dk_tile
                acc_ref[pl.ds(k0, 2 * BLOCK), dkv + kt * LANES:dkv + (kt + 1) * LANES] += dv_tile
            return dqg_acc, dkg_acc, dsk_acc

        zero = jnp.zeros((1, LANES), F32)
        dqg_acc, dkg_acc, dsk_acc = lax.fori_loop(0, seq // BLOCK, blk, (zero, zero, zero))
        dqg_ref[...] += dqg_acc
        dkg_ref[...] += dkg_acc
        dsk_ref[...] += dsk_acc
        o_ref[:, dq:dq + 2 * dkv] = acc_ref[...].astype(BF16)

    small = pl.BlockSpec((1, LANES), lambda b: (0, 0))
    return pl.pallas_call(
        body, name="attn_bwd", grid=(nseq,),
        in_specs=[pl.BlockSpec(memory_space=pltpu.SMEM),
                  pl.BlockSpec((seq, dq), lambda b: (b, 0)),
                  pl.BlockSpec((seq, dq + 2 * dkv), lambda b: (b, 0)),
                  small, small],
        out_specs=[pl.BlockSpec((seq, dq + 2 * dkv), lambda b: (b, 0)), small, small, small],
        out_shape=[_sds((t, dq + 2 * dkv), BF16), _sds((1, LANES), F32), _sds((1, LANES), F32),
                   _sds((1, LANES), F32)],
        scratch_shapes=[pltpu.VMEM((seq, 2 * dkv), F32)],
        compiler_params=_params(("arbitrary",)))(sinks, do, qkv, qg_pair, kg_pair)


def _place():
    x, y, c = lax.axis_index("x"), lax.axis_index("y"), lax.axis_index("c")
    other_chips = [(1 - x, y), (x, 1 - y), (1 - x, 1 - y)]
    return x, y, c, other_chips


def _half_rows(c, rows):
    rh = rows // 2
    return pl.ds(pl.multiple_of(c * rh, BF16_ROWS), rh)


def _cast_own(name, w, place, layer=None):
    nl, r, cdim = w.shape
    first = 0
    if layer is not None:
        nl, first = 1, layer
    rt = _pick(r, (256, 128, 64, 32))

    def body(s_ref, w_ref, o_ref):
        o_ref[...] = w_ref[...].astype(BF16)

    grid_spec = pltpu.PrefetchScalarGridSpec(
        num_scalar_prefetch=1, grid=(nl, r // rt),
        in_specs=[pl.BlockSpec((None, rt, cdim), lambda l, i, s: (first + l, i, 0))],
        out_specs=pl.BlockSpec((None, None, rt, cdim), lambda l, i, s: (l, s[1], i, 0)))
    return pl.pallas_call(
        body, name=name, grid_spec=grid_spec, out_shape=_sds((nl, N_CHIPS, r, cdim), BF16),
        compiler_params=_params(("parallel", "parallel")))(place, w)


def _gather_protocol(outs, shapes, send_sems, recv_sems):
    n = len(outs)
    x, y, c, other_chips = _place()
    me_chip = 2 * x + y
    sibling = (x, y, 1 - c)

    def rows(u, chip, half):
        return outs[u].at[:, chip, _half_rows(half, shapes[u][2]), :]

    def copy(sem, part, to):
        return pltpu.make_async_remote_copy(src_ref=part, dst_ref=part, send_sem=send_sems.at[sem],
                                            recv_sem=recv_sems.at[sem], device_id=to, device_id_type=MESH)

    sends = []
    for u in range(n):
        for k, chip in enumerate(other_chips):
            cp = copy(6 * u + k, rows(u, me_chip, c), (*chip, c))
            cp.start()
            sends.append(cp)
    for u in range(n):
        for k, chip in enumerate(other_chips):
            got = rows(u, 2 * chip[0] + chip[1], c)
            copy(6 * u + k, got, (*chip, c)).wait_recv()
            cp = copy(6 * u + 3 + k, got, sibling)
            cp.start()
            sends.append(cp)
    for u in range(n):
        for k, chip in enumerate(other_chips):
            copy(6 * u + 3 + k, rows(u, 2 * chip[0] + chip[1], 1 - c), sibling).wait_recv()
    for cp in sends:
        cp.wait_send()


def _hbm_ref(a):
    return jax.new_ref(a, memory_space=pltpu.MemorySpace.HBM)


def _hbm_empty(shape, dtype):
    return jax.empty_ref(_sds(shape, dtype), memory_space=pltpu.MemorySpace.HBM)


def _sibling_peer():
    x, y, c, _ = _place()
    return [(x, y, 1 - c)]


def _chip_peers():
    x, y, c, other_chips = _place()
    return [(*chip, c) for chip in other_chips]


def _gather_peers():
    return _chip_peers() + _sibling_peer()


def _on_sequencer(name, collective_id, n_sems, peers, protocol):
    @pl.kernel(mesh=plsc.ScalarSubcoreMesh(axis_name="sequencer", num_cores=1), name=name,
               scratch_types=(pltpu.SemaphoreType.DMA((n_sems,)), pltpu.SemaphoreType.DMA((n_sems,))),
               compiler_params=pltpu.CompilerParams(collective_id=collective_id))
    def launch(send_sems, recv_sems):
        barrier = pltpu.get_barrier_semaphore()
        targets = peers()
        for peer in targets:
            pl.semaphore_signal(barrier, inc=1, device_id=peer, device_id_type=MESH)
        pl.semaphore_wait(barrier, len(targets))
        protocol(send_sems, recv_sems)

    launch()


def _seq_allgather(name, collective_id, bufs):
    shapes = [b.shape for b in bufs]
    refs = [_hbm_ref(b) for b in bufs]
    _on_sequencer(name, collective_id, 6 * len(bufs), _gather_peers,
                  lambda send_sems, recv_sems: _gather_protocol(refs, shapes, send_sems, recv_sems))
    return [r[...] for r in refs]


def _exchange_protocol(gs, outs, shapes, send_sems, recv_sems):
    x, y, c, _ = _place()
    sends = []
    for u in range(len(gs)):
        cp = pltpu.make_async_remote_copy(
            src_ref=gs[u].at[:, _half_rows(1 - c, shapes[u][1]), :], dst_ref=outs[u],
            send_sem=send_sems.at[u], recv_sem=recv_sems.at[u], device_id=(x, y, 1 - c), device_id_type=MESH)
        cp.start()
        sends.append(cp)
    for cp in sends:
        cp.wait_recv()
    for cp in sends:
        cp.wait_send()


def _seq_exchange(name, collective_id, grads):
    shapes = [g.shape for g in grads]
    gs = [_hbm_ref(g) for g in grads]
    outs = [_hbm_empty((s[0], s[1] // 2, s[2]), F32) for s in shapes]
    _on_sequencer(name, collective_id, len(grads), _sibling_peer,
                  lambda send_sems, recv_sems: _exchange_protocol(gs, outs, shapes, send_sems, recv_sems))
    return [o[...] for o in outs]


def _sum_halves(name, g, got, place, after):
    _, r, cdim = g.shape
    rh = r // 2
    rt = _pick(rh, (128, 64, 32, 16))
    nr = rh // rt

    def body(s_ref, g_ref, got_ref, after_ref, pb_ref, pf_ref):
        s = g_ref[...] + got_ref[...]
        pb_ref[...] = s.astype(BF16)

        @pl.when(pl.program_id(1) == s_ref[1])
        def _():
            pf_ref[...] = s

    grid_spec = pltpu.PrefetchScalarGridSpec(
        num_scalar_prefetch=1, grid=(nr, N_CHIPS),
        in_specs=[pl.BlockSpec((None, rt, cdim), lambda i, q, s: (q, s[0] * nr + i, 0)),
                  pl.BlockSpec((None, rt, cdim), lambda i, q, s: (q, i, 0)),
                  pl.BlockSpec(memory_space=pl.ANY)],
        out_specs=[pl.BlockSpec((None, rt, cdim), lambda i, q, s: (q, i, 0)),
                   pl.BlockSpec((rt, cdim), lambda i, q, s: (i, 0))])
    return pl.pallas_call(
        body, name=name, grid_spec=grid_spec,
        out_shape=[_sds((N_CHIPS, rh, cdim), BF16), _sds((rh, cdim), F32)],
        compiler_params=_params(("parallel", "arbitrary")))(place, g, got, after)


def _scatter_protocol(ps, outs, send_sems, recv_sems):
    x, y, c, other_chips = _place()
    sends = []
    for u in range(len(ps)):
        for k, chip in enumerate(other_chips):
            cp = pltpu.make_async_remote_copy(
                src_ref=ps[u].at[2 * chip[0] + chip[1]], dst_ref=outs[u].at[k],
                send_sem=send_sems.at[3 * u + k], recv_sem=recv_sems.at[3 * u + k],
                device_id=(*chip, c), device_id_type=MESH)
            cp.start()
            sends.append(cp)
    for cp in sends:
        cp.wait_recv()
    for cp in sends:
        cp.wait_send()


def _seq_scatter(name, collective_id, partials):
    ps = [_hbm_ref(p) for p in partials]
    outs = [_hbm_empty((3, p.shape[1], p.shape[2]), BF16) for p in partials]
    _on_sequencer(name, collective_id, 3 * len(partials), _chip_peers,
                  lambda send_sems, recv_sems: _scatter_protocol(ps, outs, send_sems, recv_sems))
    return [o[...] for o in outs]


def _sum_partials(name, own, got, place, layer, nl, prev, after):
    rh, cdim = own.shape
    rt = _pick(rh, (128, 64, 32, 16))
    nr = rh // rt

    def body(s_ref, own_ref, got_ref, *rest):
        o_ref = rest[-1]
        o_ref[...] = ((own_ref[...] + got_ref[0].astype(F32)) + got_ref[1].astype(F32)) + got_ref[2].astype(F32)

    in_specs = [pl.BlockSpec((rt, cdim), lambda i, s: (i, 0)), pl.BlockSpec((3, rt, cdim), lambda i, s: (0, i, 0)),
                pl.BlockSpec(memory_space=pl.ANY)]
    args = [place, own, got, after]
    aliases = {}
    if prev is not None:
        in_specs.append(pl.BlockSpec(memory_space=pl.ANY))
        args.append(prev)
        aliases = {4: 0}
    grid_spec = pltpu.PrefetchScalarGridSpec(
        num_scalar_prefetch=1, grid=(nr,), in_specs=in_specs,
        out_specs=pl.BlockSpec((None, rt, cdim), lambda i, s: (layer, s[0] * nr + i, 0)))
    return pl.pallas_call(
        body, name=name, grid_spec=grid_spec, out_shape=_sds((nl, 2 * rh, cdim), F32),
        input_output_aliases=aliases, compiler_params=_params(("parallel",)))(*args)


def _share_protocol(outs, shapes, units, send_sems, recv_sems):
    x, y, c, _ = _place()
    sends = []
    for u, (w, l) in enumerate(units):
        mine = outs[w].at[l, _half_rows(c, shapes[w][1]), :]
        cp = pltpu.make_async_remote_copy(src_ref=mine, dst_ref=mine, send_sem=send_sems.at[u],
                                          recv_sem=recv_sems.at[u], device_id=(x, y, 1 - c), device_id_type=MESH)
        cp.start()
        sends.append(cp)
    for u, (w, l) in enumerate(units):
        theirs = outs[w].at[l, _half_rows(1 - c, shapes[w][1]), :]
        pltpu.make_async_remote_copy(src_ref=theirs, dst_ref=theirs, send_sem=send_sems.at[u],
                                     recv_sem=recv_sems.at[u], device_id=(x, y, 1 - c),
                                     device_id_type=MESH).wait_recv()
    for cp in sends:
        cp.wait_send()


def _seq_share(name, collective_id, bufs):
    shapes = [b.shape for b in bufs]
    units = [(w, l) for w in range(len(bufs)) for l in range(shapes[w][0])]
    refs = [_hbm_ref(b) for b in bufs]
    _on_sequencer(name, collective_id, len(units), _sibling_peer,
                  lambda send_sems, recv_sems: _share_protocol(refs, shapes, units, send_sems, recv_sems))
    return [r[...] for r in refs]


def _gather_blocks(block_ref, all_ref, send_sems, recv_sems):
    x, y, c, _ = _place()
    me = 4 * x + 2 * y + c
    all_ref[me] = block_ref[...]
    sends = []
    for rel in range(1, 8):
        fx, fy, fc = (rel >> 2) & 1, (rel >> 1) & 1, rel & 1
        peer = (x ^ fx, y ^ fy, c ^ fc)
        cp = pltpu.make_async_remote_copy(src_ref=block_ref, dst_ref=all_ref.at[me], send_sem=send_sems.at[rel - 1],
                                          recv_sem=recv_sems.at[rel - 1], device_id=peer, device_id_type=MESH)
        cp.start()
        sends.append(cp)
    for cp in sends:
        cp.wait_recv()
    for cp in sends:
        cp.wait_send()


def _gather_conv_w(cw_block):
    r, d = cw_block.shape

    def body(b_ref, o_ref, all_ref, send_sems, recv_sems):
        _gather_blocks(b_ref, all_ref, send_sems, recv_sems)
        o_ref[...] = (all_ref[0] + all_ref[2]) + (all_ref[4] + all_ref[6])

    vm = pl.BlockSpec(memory_space=pltpu.VMEM)
    return pl.pallas_call(
        body, name="gather_conv_w", in_specs=[vm], out_specs=vm, out_shape=_sds((r, d), F32),
        scratch_shapes=[pltpu.VMEM((8, r, d), F32), pltpu.SemaphoreType.DMA((7,)), pltpu.SemaphoreType.DMA((7,))],
    )(cw_block)


def _adam(w, g, m, v):
    m_new = ADAM_B1 * m + (1.0 - ADAM_B1) * g
    v_new = ADAM_B2 * v + (1.0 - ADAM_B2) * (g * g)
    m_hat = m_new / (1.0 - ADAM_B1 ** ADAM_STEP)
    v_hat = v_new / (1.0 - ADAM_B2 ** ADAM_STEP)
    delta = -ADAM_LR * (m_hat / (jnp.sqrt(v_hat) + ADAM_EPS) + ADAM_WD * w)
    return delta, m_new, v_new


def _small_step(dnm0, dnm1, dnf0, dnf1, dcw, dqg, dkg, dsk, loss, w_blk, m_blk, v_blk):
    d = w_blk.shape[1]

    def body(dnm0_ref, dnm1_ref, dnf0_ref, dnf1_ref, dcw_ref, dqg_ref, dkg_ref, dsk_ref, loss_ref,
             w_ref, m_ref, v_ref, g_ref, dl_ref, mo_ref, vo_ref, blk_ref, all_ref, send_sems, recv_sems):
        blk_ref[...] = jnp.zeros_like(blk_ref)
        blk_ref[0:1, :] = jnp.sum(dnm0_ref[...], axis=0, keepdims=True)
        blk_ref[1:2, :] = jnp.sum(dnm1_ref[...], axis=0, keepdims=True)
        blk_ref[8:9, :] = jnp.sum(dnf0_ref[...], axis=0, keepdims=True)
        blk_ref[9:10, :] = jnp.sum(dnf1_ref[...], axis=0, keepdims=True)
        blk_ref[16:19, :] = dcw_ref[...]
        dqg_v = dqg_ref[...]
        dkg_v = dkg_ref[...]
        blk_ref[24:25, 0:LANES] = dqg_v + pltpu.roll(dqg_v, HEAD_DIM, 1)
        blk_ref[24:25, LANES:2 * LANES] = dkg_v + pltpu.roll(dkg_v, HEAD_DIM, 1)
        blk_ref[24:25, 2 * LANES:3 * LANES] = dsk_ref[...]
        blk_ref[24:25, 3 * LANES:4 * LANES] = jnp.broadcast_to(loss_ref[...], (1, LANES))
        _gather_blocks(blk_ref, all_ref, send_sems, recv_sems)
        g = all_ref[0]
        for dev in range(1, 8):
            g = g + all_ref[dev]
        g_ref[...] = g
        delta, m_new, v_new = _adam(w_ref[...], g, m_ref[...], v_ref[...])
        dl_ref[...] = delta
        mo_ref[...] = m_new
        vo_ref[...] = v_new

    vm = pl.BlockSpec(memory_space=pltpu.VMEM)
    blk = _sds((SMALL_ROWS, d), F32)
    return pl.pallas_call(
        body, name="small_step", in_specs=[vm] * 12, out_specs=[vm] * 4, out_shape=[blk] * 4,
        scratch_shapes=[pltpu.VMEM((SMALL_ROWS, d), F32), pltpu.VMEM((8, SMALL_ROWS, d), F32),
                        pltpu.SemaphoreType.DMA((7,)), pltpu.SemaphoreType.DMA((7,))],
    )(dnm0, dnm1, dnf0, dnf1, dcw, dqg, dkg, dsk, loss, w_blk, m_blk, v_blk)


def _adam_step(name, w, g, m, v):
    nl, r, cdim = w.shape
    rt = _pick(r, (128, 64, 32))

    def body(w_ref, g_ref, m_ref, v_ref, d_ref, mo_ref, vo_ref):
        delta, m_new, v_new = _adam(w_ref[...], g_ref[...], m_ref[...], v_ref[...])
        d_ref[...] = delta
        mo_ref[...] = m_new
        vo_ref[...] = v_new

    spec = pl.BlockSpec((None, rt, cdim), lambda l, i: (l, i, 0))
    return pl.pallas_call(
        body, name=name, grid=(nl, r // rt), in_specs=[spec] * 4, out_specs=[spec] * 3,
        out_shape=[_sds(w.shape, F32)] * 3,
        compiler_params=_params(("parallel", "parallel")))(w, g, m, v)


def _pad_rows(a, rows=SUBLANES):
    return jnp.pad(a, ((0, rows - a.shape[0]), (0, 0)))


def _small_block(nm, nf, cw_local, qg, kg, sk, chip):
    d = nm.shape[1]
    cw_rows = lax.dynamic_update_slice(jnp.zeros((SUBLANES, d), F32), cw_local, (0, chip * cw_local.shape[1]))
    misc = jnp.concatenate([qg, qg, kg, kg, jnp.pad(sk, ((0, 0), (0, LANES - sk.shape[1]))),
                            jnp.zeros((1, d - 3 * LANES), F32)], axis=1)
    return jnp.concatenate([_pad_rows(nm), _pad_rows(nf), cw_rows, _pad_rows(misc)], axis=0)


def _unpack_small(blk, chip, cw_cols):
    cw = lax.dynamic_slice(blk[16:19], (0, chip * cw_cols), (3, cw_cols))[None]
    return dict(norm_mixer=blk[0:2], norm_ffn=blk[8:10], conv_w=cw, attn_q_gain=blk[24:25, 0:HEAD_DIM],
                attn_k_gain=blk[24:25, LANES:LANES + HEAD_DIM], attn_sinks=blk[24:25, 2 * LANES:2 * LANES + N_Q_HEADS])


WEIGHT_NAMES = ("conv_w_in", "conv_w", "conv_w_out", "attn_w_qkv", "attn_q_gain", "attn_k_gain", "attn_sinks",
                "attn_w_o", "norm_mixer", "norm_ffn", "ffn_w_gate_up", "ffn_w_down")
BIG = ("conv_w_in", "conv_w_out", "attn_w_qkv", "attn_w_o", "ffn_w_gate_up", "ffn_w_down")


def kernel(x, conv_w_in, conv_w, conv_w_out, attn_w_qkv, attn_q_gain, attn_k_gain, attn_sinks, attn_w_o, norm_mixer, norm_ffn, ffn_w_gate_up, ffn_w_down, loss_target, m_conv_w_in, m_conv_w, m_conv_w_out, m_attn_w_qkv, m_attn_q_gain, m_attn_k_gain, m_attn_sinks, m_attn_w_o, m_norm_mixer, m_norm_ffn, m_ffn_w_gate_up, m_ffn_w_down, v_conv_w_in, v_conv_w, v_conv_w_out, v_attn_w_qkv, v_attn_q_gain, v_attn_k_gain, v_attn_sinks, v_attn_w_o, v_norm_mixer, v_norm_ffn, v_ffn_w_gate_up, v_ffn_w_down):
    w = dict(conv_w_in=conv_w_in, conv_w=conv_w, conv_w_out=conv_w_out, attn_w_qkv=attn_w_qkv,
             attn_q_gain=attn_q_gain, attn_k_gain=attn_k_gain, attn_sinks=attn_sinks, attn_w_o=attn_w_o,
             norm_mixer=norm_mixer, norm_ffn=norm_ffn, ffn_w_gate_up=ffn_w_gate_up, ffn_w_down=ffn_w_down)
    m = dict(conv_w_in=m_conv_w_in, conv_w=m_conv_w, conv_w_out=m_conv_w_out, attn_w_qkv=m_attn_w_qkv,
             attn_q_gain=m_attn_q_gain, attn_k_gain=m_attn_k_gain, attn_sinks=m_attn_sinks, attn_w_o=m_attn_w_o,
             norm_mixer=m_norm_mixer, norm_ffn=m_norm_ffn, ffn_w_gate_up=m_ffn_w_gate_up, ffn_w_down=m_ffn_w_down)
    v = dict(conv_w_in=v_conv_w_in, conv_w=v_conv_w, conv_w_out=v_conv_w_out, attn_w_qkv=v_attn_w_qkv,
             attn_q_gain=v_attn_q_gain, attn_k_gain=v_attn_k_gain, attn_sinks=v_attn_sinks, attn_w_o=v_attn_w_o,
             norm_mixer=v_norm_mixer, norm_ffn=v_norm_ffn, ffn_w_gate_up=v_ffn_w_gate_up, ffn_w_down=v_ffn_w_down)

    nseq, seq, d = x.shape
    t = nseq * seq
    chip = 2 * lax.axis_index("x") + lax.axis_index("y")
    core = lax.axis_index("c")
    place = jnp.stack([core, chip]).astype(jnp.int32)
    x0 = x.reshape(t, d)
    tgt = loss_target.reshape(t, d)

    cw_block = lax.dynamic_update_slice(jnp.zeros((SUBLANES, d), F32), conv_w[0], (0, chip * conv_w.shape[2]))
    cw_full = _gather_conv_w(cw_block)[0:3]
    def cast(k, layer=None):
        return _cast_own(f"cast_{k}" + ("" if layer is None else str(layer)), w[k], place, layer)

    w_in, w_out = _seq_allgather("allgather_conv", 1, [cast("conv_w_in"), cast("conv_w_out")])
    w_gu0, w_dn0 = _seq_allgather("allgather_ffn0", 2, [cast("ffn_w_gate_up", 0), cast("ffn_w_down", 0)])
    w_qkv, w_o, w_gu1, w_dn1 = _seq_allgather(
        "allgather_rest", 3, [cast("attn_w_qkv"), cast("attn_w_o"), cast("ffn_w_gate_up", 1), cast("ffn_w_down", 1)])
    w_out = w_out.reshape(1, d, d)
    w_o = w_o.reshape(1, d, d)
    w_gu = [w_gu0, w_gu1]
    w_dn = [w_dn0.reshape(1, D_FF, d), w_dn1.reshape(1, D_FF, d)]

    qg_pair = jnp.concatenate([attn_q_gain, attn_q_gain], axis=1)
    kg_pair = jnp.concatenate([attn_k_gain, attn_k_gain], axis=1)

    def ffn_bwd(i, dxo, xin, h, g, u, a):
        g_dn = _wgrad_down(f"ffn{i}_down_wgrad", a, dxo, D_FF // 2)
        dg, du = _mm_down_t_swiglu(f"ffn{i}_down_dgrad", dxo, w_dn[i], 0, g, u)
        g_gu = _wgrad_up2(f"ffn{i}_up_wgrad", h, dg, du)
        dxi, dgain = _dgrad_norm_ffn(f"ffn{i}_up_dgrad", dg, du, w_gu[i], 0, xin, norm_ffn[i:i + 1], dxo)
        return dxi, dgain, g_gu, g_dn

    h0 = _rms_fwd("conv_norm", x0, norm_mixer[0:1])
    bcx = _mm_up_joined("conv_in", h0, w_in, 512)
    z = _conv_fwd(bcx, cw_full, nseq, seq)
    x1, h1 = _mm_down_norm("conv_out", z, w_out, 0, x0, norm_ffn[0:1])
    g0, u0, a0 = _mm_up_swiglu("ffn0_up", h1, w_gu[0], 0)
    x2, h2 = _mm_down_norm("ffn0_down", a0, w_dn[0], 0, x1, norm_mixer[1:2])
    qkv = _mm_up_joined("attn_qkv", h2, w_qkv, 1024)
    o = _attn_fwd(qkv, qg_pair, kg_pair, attn_sinks, nseq, seq)
    x3, h3 = _mm_down_norm("attn_out", o, w_o, 0, x2, norm_ffn[1:2])
    g1, u1, a1 = _mm_up_swiglu("ffn1_up", h3, w_gu[1], 0)
    dy, loss_part = _mm_down_loss("ffn1_down", a1, w_dn[1], 0, x3, tgt)

    finished = {k: None for k in BIG}

    def exchange(tag, cid, units):
        return units, _seq_exchange(f"exchange_{tag}", cid, [g for _, _, g in units])

    def scatter(tag, cid, group, after):
        units, got = group
        sums = [_sum_halves(f"sum_halves_{k}{l}", g, r, place, after) for (k, l, g), r in zip(units, got)]
        return units, sums, _seq_scatter(f"scatter_{tag}", cid, [pb for pb, _ in sums])

    def finish(group, after):
        units, sums, arrived = group
        for (k, l, _), (_, pf), r in zip(units, sums, arrived):
            finished[k] = _sum_partials(f"sum_partials_{k}{l}", pf, r, place, l, w[k].shape[0], finished[k], after)

    dx3, dnf1, g_gu1, g_dn1 = ffn_bwd(1, dy, x3, h3, g1, u1, a1)
    ffn1 = exchange("ffn1", 4, [("ffn_w_down", 1, g_dn1), ("ffn_w_gate_up", 1, g_gu1)])
    g_o = _wgrad_down("attn_out_wgrad", o, dx3, d)
    do = _mm_down_t("attn_out_dgrad", dx3, w_o, 0)
    ffn1 = scatter("ffn1", 8, ffn1, do)
    dqkv, dqg, dkg, dsk = _attn_bwd(do, qkv, qg_pair, kg_pair, attn_sinks, nseq, seq)
    g_qkv = _wgrad_joined("attn_qkv_wgrad", h2, dqkv)
    attn = exchange("attn", 5, [("attn_w_o", 0, g_o), ("attn_w_qkv", 0, g_qkv)])
    dx2, dnm1 = _dgrad_norm_qkv("attn_qkv_dgrad", dqkv, w_qkv, x2, norm_mixer[1:2], dx3)
    finish(ffn1, dx2)
    attn = scatter("attn", 9, attn, dx2)
    dx1, dnf0, g_gu0, g_dn0 = ffn_bwd(0, dx2, x1, h1, g0, u0, a0)
    ffn0 = exchange("ffn0", 6, [("ffn_w_down", 0, g_dn0), ("ffn_w_gate_up", 0, g_gu0)])
    g_out = _wgrad_down("conv_out_wgrad", z, dx1, d)
    dz = _mm_down_t("conv_out_dgrad", dx1, w_out, 0)
    finish(attn, dz)
    ffn0 = scatter("ffn0", 10, ffn0, dz)
    dbcx, dcw = _conv_bwd(dz, bcx, cw_full, nseq, seq)
    g_in = _wgrad_conv_in("conv_in_wgrad", h0, dbcx, conv_w_in.shape[2])
    conv = exchange("conv", 7, [("conv_w_out", 0, g_out), ("conv_w_in", 0, g_in)])
    dx0, dnm0 = _dgrad_norm_conv("conv_in_dgrad", dbcx, w_in, x0, norm_mixer[0:1], dx1)
    finish(ffn0, dx0)
    late = ("attn_w_qkv", "attn_w_o", "ffn_w_gate_up", "ffn_w_down")
    grads_late = _seq_share("share_late", 12, [finished[k] for k in late])
    conv = scatter("conv", 11, conv, dx0)

    grad, delta, new_m, new_v = {}, {}, {}, {}

    def adam(k, g):
        grad[k] = g
        delta[k], new_m[k], new_v[k] = _adam_step(f"adam_{k}", w[k], g, m[k], v[k])

    for k, g in zip(late, grads_late):
        adam(k, g)
    finish(conv, delta["ffn_w_gate_up"])
    last = ("conv_w_in", "conv_w_out")
    for k, g in zip(last, _seq_share("share_last", 13, [finished[k] for k in last])):
        adam(k, g)

    def blocks(src):
        return _small_block(src["norm_mixer"], src["norm_ffn"], src["conv_w"][0], src["attn_q_gain"],
                            src["attn_k_gain"], src["attn_sinks"], chip)

    g_blk, d_blk, m_blk, v_blk = _small_step(dnm0, dnm1, dnf0, dnf1, dcw, dqg, dkg, dsk, loss_part,
                                             blocks(w), blocks(m), blocks(v))
    cw_cols = conv_w.shape[2]
    for dst, blk in ((grad, g_blk), (delta, d_blk), (new_m, m_blk), (new_v, v_blk)):
        dst.update(_unpack_small(blk, chip, cw_cols))
    loss = g_blk[24, 3 * LANES]

    return (loss, dx0.reshape(nseq, seq, d), *[grad[k] for k in WEIGHT_NAMES], *[delta[k] for k in WEIGHT_NAMES],
            *[new_m[k] for k in WEIGHT_NAMES], *[new_v[k] for k in WEIGHT_NAMES])
```

```python
import jax
import jax.numpy as jnp
from jax import lax
from jax.experimental import pallas as pl
from jax.experimental.pallas import tpu as pltpu
from jax.experimental.pallas import tpu_sc as plsc

F32 = jnp.float32
BF16 = jnp.bfloat16

D_MODEL = 1024
D_FF = 2816
N_Q_HEADS = 16
N_KV_HEADS = 4
HEAD_DIM = 64
WINDOW = 128
BLOCK = 128
EPS = 1e-6
N_CHIPS = 4
LANES = 128
SUBLANES = 8
BF16_ROWS = 16
MXU_COLS = 256
VMEM_LIMIT = 48 * 1024 * 1024
ADAM_LR, ADAM_B1, ADAM_B2, ADAM_EPS, ADAM_WD, ADAM_STEP = 0.001, 0.9, 0.999, 1e-08, 0.01, 10
ALIBI_SLOPES = tuple(2.0 ** (-8.0 * (h + 1) / N_Q_HEADS) for h in range(N_Q_HEADS))
SMALL_ROWS = 32
MESH = pl.DeviceIdType.MESH

NN = ((1,), (0,))
NT = ((1,), (1,))
TN = ((0,), (0,))


def _dot(a, b, dims):
    return lax.dot_general(a, b, (dims, ((), ())), preferred_element_type=F32)


def _pick(n, cands):
    for c in cands:
        if n % c == 0:
            return c
    raise ValueError((n, cands))


def _params(sem):
    return pltpu.CompilerParams(dimension_semantics=sem, vmem_limit_bytes=VMEM_LIMIT)


def _sds(shape, dtype):
    return jax.ShapeDtypeStruct(shape, dtype)


def _rms(xv):
    return lax.rsqrt(jnp.mean(xv * xv, axis=-1, keepdims=True) + EPS)


def _sigmoid(g):
    return 1.0 / (1.0 + jnp.exp(-g))


def _mm_up_joined(name, a, w4, tm_pref):
    t, k = a.shape
    _, _, _, nq = w4.shape
    tm = _pick(t, (tm_pref, 256, 128))

    def body(a_ref, w_ref, o_ref, wcat_ref):
        @pl.when(pl.program_id(0) == 0)
        def _():
            for q in range(N_CHIPS):
                wcat_ref[:, q * nq:(q + 1) * nq] = w_ref[q]

        o_ref[...] = _dot(a_ref[...], wcat_ref[...], NN).astype(BF16)

    return pl.pallas_call(
        body, name=name, grid=(t // tm,),
        in_specs=[pl.BlockSpec((tm, k), lambda i: (i, 0)),
                  pl.BlockSpec((None, N_CHIPS, k, nq), lambda i: (0, 0, 0, 0))],
        out_specs=pl.BlockSpec((tm, N_CHIPS * nq), lambda i: (i, 0)),
        out_shape=_sds((t, N_CHIPS * nq), BF16),
        scratch_shapes=[pltpu.VMEM((k, N_CHIPS * nq), BF16)],
        compiler_params=_params(("arbitrary",)))(a, w4)


def _mm_up_swiglu(name, h, w4, layer):
    t, k = h.shape
    _, _, _, nq = w4.shape
    tm = _pick(t, (512, 256, 128))

    def body(h_ref, wg_ref, wu_ref, g_ref, u_ref, a_ref):
        hv = h_ref[...]
        g = _dot(hv, wg_ref[...], NN)
        u = _dot(hv, wu_ref[...], NN)
        g_ref[...] = g.astype(BF16)
        u_ref[...] = u.astype(BF16)
        a_ref[...] = (g * _sigmoid(g) * u).astype(BF16)

    half = N_CHIPS // 2
    out = pl.BlockSpec((tm, nq), lambda j, i: (i, j))
    return pl.pallas_call(
        body, name=name, grid=(half, t // tm),
        in_specs=[pl.BlockSpec((tm, k), lambda j, i: (i, 0)),
                  pl.BlockSpec((None, None, k, nq), lambda j, i: (layer, j, 0, 0)),
                  pl.BlockSpec((None, None, k, nq), lambda j, i: (layer, half + j, 0, 0))],
        out_specs=[out, out, out],
        out_shape=[_sds((t, half * nq), BF16)] * 3,
        compiler_params=_params(("parallel", "parallel")))(h, w4, w4)


def _mm_down_norm(name, a, w, layer, res, gain):
    t, kf = a.shape
    _, _, n = w.shape
    tm = _pick(t, (512, 256, 128))

    def body(a_ref, w_ref, r_ref, g_ref, o_ref, h_ref):
        xo = r_ref[...] + _dot(a_ref[...], w_ref[...], NN)
        o_ref[...] = xo
        h_ref[...] = ((xo * _rms(xo)) * g_ref[...]).astype(BF16)

    row = pl.BlockSpec((tm, n), lambda i: (i, 0))
    return pl.pallas_call(
        body, name=name, grid=(t // tm,),
        in_specs=[pl.BlockSpec((tm, kf), lambda i: (i, 0)),
                  pl.BlockSpec((None, kf, n), lambda i: (layer, 0, 0)),
                  row, pl.BlockSpec((1, n), lambda i: (0, 0))],
        out_specs=[row, row],
        out_shape=[_sds((t, n), F32), _sds((t, n), BF16)],
        compiler_params=_params(("parallel",)))(a, w, res, gain)


def _mm_down_loss(name, a, w, layer, res, tgt):
    t, kf = a.shape
    _, _, n = w.shape
    tm = _pick(t, (512, 256, 128))
    steps = t // tm

    def body(a_ref, w_ref, r_ref, t_ref, dy_ref, l_ref, acc_ref):
        i = pl.program_id(0)

        @pl.when(i == 0)
        def _():
            acc_ref[...] = jnp.zeros_like(acc_ref)

        e = (r_ref[...] + _dot(a_ref[...], w_ref[...], NN)) - t_ref[...]
        dy_ref[...] = e * (1.0 / n)
        acc_ref[...] += (e * e).reshape(tm // SUBLANES, SUBLANES, n).sum(axis=0)

        @pl.when(i == steps - 1)
        def _():
            l_ref[...] = jnp.sum(acc_ref[...], keepdims=True) * (0.5 / n)

    row = pl.BlockSpec((tm, n), lambda i: (i, 0))
    return pl.pallas_call(
        body, name=name, grid=(steps,),
        in_specs=[pl.BlockSpec((tm, kf), lambda i: (i, 0)),
                  pl.BlockSpec((None, kf, n), lambda i: (layer, 0, 0)), row, row],
        out_specs=[row, pl.BlockSpec((1, 1), lambda i: (0, 0))],
        out_shape=[_sds((t, n), F32), _sds((1, 1), F32)],
        scratch_shapes=[pltpu.VMEM((SUBLANES, n), F32)],
        compiler_params=_params(("arbitrary",)))(a, w, res, tgt)


def _mm_down_t(name, dx, w, layer):
    t, n = dx.shape
    _, kf, _ = w.shape
    tm = _pick(t, (512, 256, 128))

    def body(a_ref, w_ref, o_ref):
        o_ref[...] = _dot(a_ref[...].astype(BF16), w_ref[...], NT).astype(BF16)

    return pl.pallas_call(
        body, name=name, grid=(t // tm,),
        in_specs=[pl.BlockSpec((tm, n), lambda i: (i, 0)),
                  pl.BlockSpec((None, kf, n), lambda i: (layer, 0, 0))],
        out_specs=pl.BlockSpec((tm, kf), lambda i: (i, 0)),
        out_shape=_sds((t, kf), BF16),
        compiler_params=_params(("parallel",)))(dx, w)


def _mm_down_t_swiglu(name, dx, w, layer, g, u):
    t, n = dx.shape
    f = g.shape[1]
    tn = f // 2
    tm = _pick(t, (512, 256, 128))

    def body(a_ref, w_ref, g_ref, u_ref, dg_ref, du_ref):
        da = _dot(a_ref[...].astype(BF16), w_ref[...], NT)
        gv = g_ref[...].astype(F32)
        sg = _sigmoid(gv)
        dg_ref[...] = (da * u_ref[...].astype(F32) * (sg * (1.0 + gv * (1.0 - sg)))).astype(BF16)
        du_ref[...] = (da * (gv * sg)).astype(BF16)

    tile = pl.BlockSpec((tm, tn), lambda j, i: (i, j))
    return pl.pallas_call(
        body, name=name, grid=(f // tn, t // tm),
        in_specs=[pl.BlockSpec((tm, n), lambda j, i: (i, 0)),
                  pl.BlockSpec((None, tn, n), lambda j, i: (layer, j, 0)), tile, tile],
        out_specs=[tile, tile],
        out_shape=[_sds((t, f), BF16)] * 2,
        compiler_params=_params(("parallel", "parallel")))(dx, w, g, u)


def _dgrad_norm(name, acts, act_blocks, pieces, w4, layer, x, gain, dres):
    t, d = x.shape
    _, _, k, nq = w4.shape
    tm = _pick(t, (256, 128))
    n_act = len(acts)

    def body(*refs):
        act_refs = refs[:n_act]
        w_ref, x_ref, g_ref, dr_ref, dx_ref, dg_ref = refs[n_act:]

        @pl.when(pl.program_id(0) == 0)
        def _():
            dg_ref[...] = jnp.zeros_like(dg_ref)

        dh = None
        for a_tile, w_tile in pieces(act_refs, w_ref):
            term = _dot(a_tile, w_tile, NT)
            dh = term if dh is None else dh + term
        xv = x_ref[...]
        r = _rms(xv)
        xhat = xv * r
        gd = dh * g_ref[...]
        dx_ref[...] = dr_ref[...] + r * (gd - xhat * jnp.mean(gd * xhat, axis=-1, keepdims=True))
        dg_ref[...] += (dh * xhat).reshape(tm // SUBLANES, SUBLANES, d).sum(axis=0)

    row = pl.BlockSpec((tm, d), lambda i: (i, 0))
    return pl.pallas_call(
        body, name=name, grid=(t // tm,),
        in_specs=[*act_blocks(tm),
                  pl.BlockSpec((None, N_CHIPS, k, nq), lambda i: (layer, 0, 0, 0)),
                  row, pl.BlockSpec((1, d), lambda i: (0, 0)), row],
        out_specs=[row, pl.BlockSpec((SUBLANES, d), lambda i: (0, 0))],
        out_shape=[_sds((t, d), F32), _sds((SUBLANES, d), F32)],
        compiler_params=_params(("arbitrary",)))(*acts, w4, x, gain, dres)


def _dgrad_norm_ffn(name, dg, du, w4, layer, x, gain, dres):
    nq = w4.shape[3]
    f = dg.shape[1]

    def blocks(tm):
        return [pl.BlockSpec((tm, f), lambda i: (i, 0))] * 2

    def pieces(act_refs, w_ref):
        dg_ref, du_ref = act_refs
        return [(dg_ref[:, 0:nq], w_ref[0]), (dg_ref[:, nq:2 * nq], w_ref[1]),
                (du_ref[:, 0:nq], w_ref[2]), (du_ref[:, nq:2 * nq], w_ref[3])]

    return _dgrad_norm(name, [dg, du], blocks, pieces, w4, layer, x, gain, dres)


def _dgrad_norm_qkv(name, dqkv, w4, x, gain, dres):
    nq = w4.shape[3]

    def blocks(tm):
        return [pl.BlockSpec((tm, N_CHIPS * nq), lambda i: (i, 0))]

    def pieces(act_refs, w_ref):
        return [(act_refs[0][:, q * nq:(q + 1) * nq], w_ref[q]) for q in range(N_CHIPS)]

    return _dgrad_norm(name, [dqkv], blocks, pieces, w4, 0, x, gain, dres)


def _dgrad_norm_conv(name, d3, w4, x, gain, dres):
    _, _, d = d3.shape
    nq = w4.shape[3]
    per_part, per_q = d // MXU_COLS, nq // MXU_COLS

    def blocks(tm):
        return [pl.BlockSpec((3, tm, d), lambda i: (0, i, 0))]

    def pieces(act_refs, w_ref):
        out = []
        for jb in range(3 * per_part):
            ca, cw = (jb % per_part) * MXU_COLS, (jb % per_q) * MXU_COLS
            out.append((act_refs[0][jb // per_part, :, ca:ca + MXU_COLS], w_ref[jb // per_q, :, cw:cw + MXU_COLS]))
        return out

    return _dgrad_norm(name, [d3], blocks, pieces, w4, 0, x, gain, dres)


def _wgrad_up2(name, h, dg, du):
    t, k = h.shape
    nq = dg.shape[1] // 2
    tk = _pick(t, (512, 256, 128))
    steps = t // tk
    half = N_CHIPS // 2

    def body(h_ref, dg_ref, du_ref, o_ref):
        q = pl.program_id(0)

        @pl.when(pl.program_id(1) == 0)
        def _():
            o_ref[...] = jnp.zeros_like(o_ref)

        @pl.when(q < half)
        def _():
            o_ref[...] += _dot(h_ref[...], dg_ref[...], TN)

        @pl.when(q >= half)
        def _():
            o_ref[...] += _dot(h_ref[...], du_ref[...], TN)

    return pl.pallas_call(
        body, name=name, grid=(N_CHIPS, steps),
        in_specs=[pl.BlockSpec((tk, k), lambda q, s: (s, 0)),
                  pl.BlockSpec((tk, nq), lambda q, s: (jnp.where(q < half, s, steps - 1), jnp.minimum(q, half - 1))),
                  pl.BlockSpec((tk, nq), lambda q, s: (jnp.where(q >= half, s, 0), jnp.maximum(q - half, 0)))],
        out_specs=pl.BlockSpec((None, k, nq), lambda q, s: (q, 0, 0)),
        out_shape=_sds((N_CHIPS, k, nq), F32),
        compiler_params=_params(("parallel", "arbitrary")))(h, dg, du)


def _wgrad_joined(name, h, dy):
    t, k = h.shape
    nq = dy.shape[1] // N_CHIPS
    tk = _pick(t, (1024, 512, 256, 128))

    def body(h_ref, dy_ref, o_ref):
        @pl.when(pl.program_id(0) == 0)
        def _():
            o_ref[...] = jnp.zeros_like(o_ref)

        res = _dot(h_ref[...], dy_ref[...], TN)
        for q in range(N_CHIPS):
            o_ref[q] += res[:, q * nq:(q + 1) * nq]

    return pl.pallas_call(
        body, name=name, grid=(t // tk,),
        in_specs=[pl.BlockSpec((tk, k), lambda s: (s, 0)), pl.BlockSpec((tk, N_CHIPS * nq), lambda s: (s, 0))],
        out_specs=pl.BlockSpec((N_CHIPS, k, nq), lambda s: (0, 0, 0)),
        out_shape=_sds((N_CHIPS, k, nq), F32),
        compiler_params=_params(("arbitrary",)))(h, dy)


def _wgrad_conv_in(name, h, d3, nq):
    t, k = h.shape
    d = d3.shape[2]
    per_part, per_q = d // MXU_COLS, nq // MXU_COLS
    tk = _pick(t, (512, 256, 128))

    def body(h_ref, d_ref, o_ref):
        @pl.when(pl.program_id(0) == 0)
        def _():
            o_ref[...] = jnp.zeros_like(o_ref)

        hv = h_ref[...]
        for part in range(3):
            res = _dot(hv, d_ref[part], TN)
            for cc in range(per_part):
                jb = part * per_part + cc
                co = (jb % per_q) * MXU_COLS
                o_ref[jb // per_q, :, co:co + MXU_COLS] += res[:, cc * MXU_COLS:(cc + 1) * MXU_COLS]

    return pl.pallas_call(
        body, name=name, grid=(t // tk,),
        in_specs=[pl.BlockSpec((tk, k), lambda s: (s, 0)), pl.BlockSpec((3, tk, d), lambda s: (0, s, 0))],
        out_specs=pl.BlockSpec((N_CHIPS, k, nq), lambda s: (0, 0, 0)),
        out_shape=_sds((N_CHIPS, k, nq), F32),
        compiler_params=_params(("arbitrary",)))(h, d3)


def _wgrad_down(name, a, dx, tmw):
    t, kf = a.shape
    n = dx.shape[1]
    tk = _pick(t, (512, 256, 128))

    def body(a_ref, b_ref, o_ref):
        @pl.when(pl.program_id(1) == 0)
        def _():
            o_ref[...] = jnp.zeros_like(o_ref)

        o_ref[...] += _dot(a_ref[...], b_ref[...].astype(BF16), TN)

    g = pl.pallas_call(
        body, name=name, grid=(kf // tmw, t // tk),
        in_specs=[pl.BlockSpec((tk, tmw), lambda j, s: (s, j)), pl.BlockSpec((tk, n), lambda j, s: (s, 0))],
        out_specs=pl.BlockSpec((tmw, n), lambda j, s: (j, 0)),
        out_shape=_sds((kf, n), F32),
        compiler_params=_params(("parallel", "arbitrary")))(a, dx)
    return g.reshape(N_CHIPS, kf // N_CHIPS, n)


def _rms_fwd(name, x, gain):
    t, d = x.shape
    tm = _pick(t, (512, 256, 128))

    def body(x_ref, g_ref, h_ref):
        xv = x_ref[...]
        h_ref[...] = ((xv * _rms(xv)) * g_ref[...]).astype(BF16)

    return pl.pallas_call(
        body, name=name, grid=(t // tm,),
        in_specs=[pl.BlockSpec((tm, d), lambda i: (i, 0)), pl.BlockSpec((1, d), lambda i: (0, 0))],
        out_specs=pl.BlockSpec((tm, d), lambda i: (i, 0)),
        out_shape=_sds((t, d), BF16),
        compiler_params=_params(("parallel",)))(x, gain)


def _shift_rows(u, k, rows):
    s = u.shape[0]
    if k > 0:
        return jnp.where(rows >= k, pltpu.roll(u, k, 0), 0.0)
    return jnp.where(rows < s + k, pltpu.roll(u, s + k, 0), 0.0)


def _conv_fwd(bcx, cw, nseq, seq):
    t, d3 = bcx.shape
    d = d3 // 3
    cb = MXU_COLS
    nj = d // cb

    def body(b_ref, c_ref, x_ref, cw_ref, z_ref):
        u = b_ref[...].astype(F32) * x_ref[...].astype(F32)
        rows = lax.broadcasted_iota(jnp.int32, u.shape, 0)
        cwv = cw_ref[...]
        y = cwv[2:3] * u + cwv[1:2] * _shift_rows(u, 1, rows) + cwv[0:1] * _shift_rows(u, 2, rows)
        z_ref[...] = (c_ref[...].astype(F32) * y).astype(BF16)

    return pl.pallas_call(
        body, name="conv_fwd", grid=(nseq, nj),
        in_specs=[pl.BlockSpec((seq, cb), lambda b, j: (b, j)),
                  pl.BlockSpec((seq, cb), lambda b, j: (b, nj + j)),
                  pl.BlockSpec((seq, cb), lambda b, j: (b, 2 * nj + j)),
                  pl.BlockSpec((3, cb), lambda b, j: (0, j))],
        out_specs=pl.BlockSpec((seq, cb), lambda b, j: (b, j)),
        out_shape=_sds((t, d), BF16),
        compiler_params=_params(("parallel", "parallel")))(bcx, bcx, bcx, cw)


def _conv_bwd(dz, bcx, cw, nseq, seq):
    t, d3 = bcx.shape
    d = d3 // 3
    cb = MXU_COLS
    nj = d // cb

    def body(dz_ref, b_ref, c_ref, x_ref, cw_ref, o_ref, dcw_ref):
        @pl.when(pl.program_id(1) == 0)
        def _():
            dcw_ref[...] = jnp.zeros_like(dcw_ref)

        b = b_ref[...].astype(F32)
        c = c_ref[...].astype(F32)
        xv = x_ref[...].astype(F32)
        dzv = dz_ref[...].astype(F32)
        u = b * xv
        rows = lax.broadcasted_iota(jnp.int32, u.shape, 0)
        u1 = _shift_rows(u, 1, rows)
        u2 = _shift_rows(u, 2, rows)
        cwv = cw_ref[...]
        y = cwv[2:3] * u + cwv[1:2] * u1 + cwv[0:1] * u2
        dyc = dzv * c
        du = cwv[2:3] * dyc + cwv[1:2] * _shift_rows(dyc, -1, rows) + cwv[0:1] * _shift_rows(dyc, -2, rows)
        o_ref[0] = (du * xv).astype(BF16)
        o_ref[1] = (dzv * y).astype(BF16)
        o_ref[2] = (du * b).astype(BF16)
        s0 = jnp.sum(dyc * u2, axis=0, keepdims=True)
        s1 = jnp.sum(dyc * u1, axis=0, keepdims=True)
        s2 = jnp.sum(dyc * u, axis=0, keepdims=True)
        tap = lax.broadcasted_iota(jnp.int32, (3, cb), 0)
        dcw_ref[...] += jnp.where(tap == 0, s0, jnp.where(tap == 1, s1, s2))

    return pl.pallas_call(
        body, name="conv_bwd", grid=(nj, nseq),
        in_specs=[pl.BlockSpec((seq, cb), lambda j, b: (b, j)),
                  pl.BlockSpec((seq, cb), lambda j, b: (b, j)),
                  pl.BlockSpec((seq, cb), lambda j, b: (b, nj + j)),
                  pl.BlockSpec((seq, cb), lambda j, b: (b, 2 * nj + j)),
                  pl.BlockSpec((3, cb), lambda j, b: (0, j))],
        out_specs=[pl.BlockSpec((3, seq, cb), lambda j, b: (0, b, j)),
                   pl.BlockSpec((3, cb), lambda j, b: (0, j))],
        out_shape=[_sds((3, t, d), BF16), _sds((3, d), F32)],
        compiler_params=_params(("parallel", "arbitrary")))(dz, bcx, bcx, bcx, cw)


def _pair_norm(x, gain_pair, low):
    sq = x * x
    ss_lo = jnp.sum(jnp.where(low, sq, 0.0), axis=-1, keepdims=True)
    ss_hi = jnp.sum(jnp.where(low, 0.0, sq), axis=-1, keepdims=True)
    r = lax.rsqrt(jnp.where(low, ss_lo, ss_hi) * (1.0 / HEAD_DIM) + EPS)
    xhat = x * r
    return xhat * gain_pair, xhat, r


KEYS = 2 * BLOCK
QK_SCALE = 1.0 / (HEAD_DIM ** 0.5)
N_PAIRS = N_Q_HEADS // 2


def _fill_bias(bias_ref):
    rows = lax.broadcasted_iota(jnp.int32, (2 * KEYS, BLOCK), 0)
    qi = lax.broadcasted_iota(jnp.int32, (2 * KEYS, BLOCK), 1)
    odd_head = rows >= KEYS
    kj = jnp.where(odd_head, rows - KEYS, rows)
    for later in range(2):
        dist = later * BLOCK + qi - kj
        mask = jnp.logical_and(dist >= 0, dist < WINDOW)
        distf = dist.astype(F32)
        for j in range(N_PAIRS):
            slope = jnp.where(odd_head, ALIBI_SLOPES[2 * j + 1], ALIBI_SLOPES[2 * j])
            bias_ref[later, j] = jnp.where(mask, -slope * distf, -1e30)


def _kv_pair_rows(kv_tile, parity, low):
    own = jnp.where(low if parity == 0 else jnp.logical_not(low), kv_tile, 0.0)
    other = pltpu.roll(own, HEAD_DIM, 1)
    lo, hi = (own, other) if parity == 0 else (other, own)
    return jnp.concatenate([lo, hi], axis=0).astype(BF16)


def _pair_softmax(s_t, sink_even, sink_odd):
    out = []
    for e, sink in enumerate((sink_even, sink_odd)):
        se = s_t[e * KEYS:(e + 1) * KEYS]
        m = jnp.maximum(jnp.max(se, axis=0, keepdims=True), sink)
        ee = jnp.exp(se - m)
        es = jnp.exp(sink - m)
        inv = 1.0 / (jnp.sum(ee, axis=0, keepdims=True) + es)
        out.append((ee * inv, es * inv))
    return out


def _attn_rows(n):
    q0 = pl.multiple_of(n * BLOCK, BLOCK)
    k0 = pl.multiple_of(jnp.maximum(n - 1, 0) * BLOCK, BLOCK)
    return q0, k0, jnp.minimum(n, 1)


def _attn_fwd(qkv, qg_pair, kg_pair, sinks, nseq, seq):
    t = qkv.shape[0]
    dq = N_Q_HEADS * HEAD_DIM
    dkv = N_KV_HEADS * HEAD_DIM

    def body(sk_ref, qkv_ref, qg_ref, kg_ref, o_ref, bias_ref):
        @pl.when(pl.program_id(0) == 0)
        def _():
            _fill_bias(bias_ref)

        low = lax.broadcasted_iota(jnp.int32, (1, LANES), 1) < HEAD_DIM
        qg = qg_ref[...] * QK_SCALE
        kg = kg_ref[...]

        def blk(n, carry):
            q0, k0, later = _attn_rows(n)
            for kt in range(dkv // LANES):
                kraw = qkv_ref[pl.ds(k0, KEYS), dq + kt * LANES:dq + (kt + 1) * LANES].astype(F32)
                vraw = qkv_ref[pl.ds(k0, KEYS), dq + dkv + kt * LANES:dq + dkv + (kt + 1) * LANES].astype(F32)
                kn, _, _ = _pair_norm(kraw, kg, low)
                for par in range(2):
                    kh = 2 * kt + par
                    k_pair = _kv_pair_rows(kn, par, low)
                    v_pair = _kv_pair_rows(vraw, par, low)
                    for jj in range(2):
                        j = 2 * kh + jj
                        qraw = qkv_ref[pl.ds(q0, BLOCK), j * LANES:(j + 1) * LANES].astype(F32)
                        qn, _, _ = _pair_norm(qraw, qg, low)
                        s_t = _dot(k_pair, qn.astype(BF16), NT) + bias_ref[later, j]
                        (p0, _), (p1, _) = _pair_softmax(s_t, sk_ref[0, 2 * j], sk_ref[0, 2 * j + 1])
                        p_t = jnp.concatenate([p0, p1], axis=0).astype(BF16)
                        o_ref[pl.ds(q0, BLOCK), j * LANES:(j + 1) * LANES] = _dot(p_t, v_pair, TN).astype(BF16)
            return carry

        lax.fori_loop(0, seq // BLOCK, blk, 0)

    return pl.pallas_call(
        body, name="attn_fwd", grid=(nseq,),
        in_specs=[pl.BlockSpec(memory_space=pltpu.SMEM),
                  pl.BlockSpec((seq, dq + 2 * dkv), lambda b: (b, 0)),
                  pl.BlockSpec((1, LANES), lambda b: (0, 0)),
                  pl.BlockSpec((1, LANES), lambda b: (0, 0))],
        out_specs=pl.BlockSpec((seq, dq), lambda b: (b, 0)),
        out_shape=_sds((t, dq), BF16),
        scratch_shapes=[pltpu.VMEM((2, N_PAIRS, 2 * KEYS, BLOCK), F32)],
        compiler_params=_params(("arbitrary",)))(sinks, qkv, qg_pair, kg_pair)


def _attn_bwd(do, qkv, qg_pair, kg_pair, sinks, nseq, seq):
    t = qkv.shape[0]
    dq = N_Q_HEADS * HEAD_DIM
    dkv = N_KV_HEADS * HEAD_DIM

    def body(sk_ref, do_ref, qkv_ref, qg_ref, kg_ref, o_ref, dqg_ref, dkg_ref, dsk_ref, acc_ref, bias_ref):
        @pl.when(pl.program_id(0) == 0)
        def _():
            _fill_bias(bias_ref)
            dqg_ref[...] = jnp.zeros_like(dqg_ref)
            dkg_ref[...] = jnp.zeros_like(dkg_ref)
            dsk_ref[...] = jnp.zeros_like(dsk_ref)

        acc_ref[...] = jnp.zeros_like(acc_ref)
        low = lax.broadcasted_iota(jnp.int32, (1, LANES), 1) < HEAD_DIM
        head_row = lax.broadcasted_iota(jnp.int32, (N_Q_HEADS, LANES), 0)
        qg = qg_ref[...] * QK_SCALE
        kg = kg_ref[...]

        def blk(n, carry):
            dqg_acc, dkg_acc, dsk_acc = carry
            q0, k0, later = _attn_rows(n)
            for kt in range(dkv // LANES):
                kraw = qkv_ref[pl.ds(k0, KEYS), dq + kt * LANES:dq + (kt + 1) * LANES].astype(F32)
                vraw = qkv_ref[pl.ds(k0, KEYS), dq + dkv + kt * LANES:dq + dkv + (kt + 1) * LANES].astype(F32)
                kn, khat, rk = _pair_norm(kraw, kg, low)
                dk_tile = None
                dv_tile = None
                for par in range(2):
                    kh = 2 * kt + par
                    own = low if par == 0 else jnp.logical_not(low)
                    k_pair = _kv_pair_rows(kn, par, low)
                    v_pair = _kv_pair_rows(vraw, par, low)
                    dkn_rows = jnp.zeros((2 * KEYS, LANES), F32)
                    dv_rows = jnp.zeros((2 * KEYS, LANES), F32)
                    for jj in range(2):
                        j = 2 * kh + jj
                        qraw = qkv_ref[pl.ds(q0, BLOCK), j * LANES:(j + 1) * LANES].astype(F32)
                        qn, qhat, rq = _pair_norm(qraw, qg, low)
                        qn_b = qn.astype(BF16)
                        do_b = do_ref[pl.ds(q0, BLOCK), j * LANES:(j + 1) * LANES]
                        s_t = _dot(k_pair, qn_b, NT) + bias_ref[later, j]
                        dp_t = _dot(v_pair, do_b, NT)
                        ds_halves = []
                        probs = _pair_softmax(s_t, sk_ref[0, 2 * j], sk_ref[0, 2 * j + 1])
                        for e, (p, ps) in enumerate(probs):
                            dp = dp_t[e * KEYS:(e + 1) * KEYS]
                            dsum = jnp.sum(p * dp, axis=0, keepdims=True)
                            ds_halves.append(p * (dp - dsum))
                            dsk_acc = dsk_acc - jnp.where(head_row == 2 * j + e, ps * dsum, 0.0)
                        p_t = jnp.concatenate([probs[0][0], probs[1][0]], axis=0).astype(BF16)
                        ds_t = jnp.concatenate(ds_halves, axis=0).astype(BF16)
                        dv_rows = dv_rows + _dot(p_t, do_b, NN)
                        dkn_rows = dkn_rows + _dot(ds_t, qn_b, NN)
                        dqn = _dot(ds_t, k_pair, TN)
                        dqg_acc = dqg_acc + jnp.sum(dqn * qhat, axis=0, keepdims=True)
                        dqhat = dqn * qg
                        prod = dqhat * qhat
                        m_lo = jnp.sum(jnp.where(low, prod, 0.0), axis=-1, keepdims=True)
                        m_hi = jnp.sum(jnp.where(low, 0.0, prod), axis=-1, keepdims=True)
                        mean = jnp.where(low, m_lo, m_hi) * (1.0 / HEAD_DIM)
                        o_ref[pl.ds(q0, BLOCK), j * LANES:(j + 1) * LANES] = (rq * (dqhat - qhat * mean)).astype(BF16)
                    dkn_acc = jnp.where(low, dkn_rows[0:KEYS], dkn_rows[KEYS:2 * KEYS])
                    dv_acc = jnp.where(low, dv_rows[0:KEYS], dv_rows[KEYS:2 * KEYS])
                    dkn = dkn_acc + pltpu.roll(dkn_acc, HEAD_DIM, 1)
                    dvh = dv_acc + pltpu.roll(dv_acc, HEAD_DIM, 1)
                    khat_own = jnp.where(own, khat, 0.0)
                    khat_dup = khat_own + pltpu.roll(khat_own, HEAD_DIM, 1)
                    dkg_acc = dkg_acc + jnp.sum(jnp.where(own, dkn * khat_dup, 0.0), axis=0, keepdims=True)
                    dkhat = dkn * kg
                    mean_k = jnp.sum(dkhat * khat_dup, axis=-1, keepdims=True) * (1.0 / LANES)
                    dk_raw = rk * (dkhat - khat_dup * mean_k)
                    dk_tile = jnp.where(own, dk_raw, 0.0) if dk_tile is None else jnp.where(own, dk_raw, dk_tile)
                    dv_tile = jnp.where(own, dvh, 0.0) if dv_tile is None else jnp.where(own, dvh, dv_tile)
                acc_ref[pl.ds(k0, KEYS), kt * LANES:(kt + 1) * LANES] += dk_tile
                acc_ref[pl.ds(k0, KEYS), dkv + kt * LANES:dkv + (kt + 1) * LANES] += dv_tile
            return dqg_acc, dkg_acc, dsk_acc

        zero = jnp.zeros((1, LANES), F32)
        carry = (zero, zero, jnp.zeros((N_Q_HEADS, LANES), F32))
        dqg_acc, dkg_acc, dsk_acc = lax.fori_loop(0, seq // BLOCK, blk, carry)
        dqg_ref[...] += dqg_acc * QK_SCALE
        dkg_ref[...] += dkg_acc
        dsk_ref[...] += dsk_acc
        o_ref[:, dq:dq + 2 * dkv] = acc_ref[...].astype(BF16)

    small = pl.BlockSpec((1, LANES), lambda b: (0, 0))
    heads = pl.BlockSpec((N_Q_HEADS, LANES), lambda b: (0, 0))
    return pl.pallas_call(
        body, name="attn_bwd", grid=(nseq,),
        in_specs=[pl.BlockSpec(memory_space=pltpu.SMEM),
                  pl.BlockSpec((seq, dq), lambda b: (b, 0)),
                  pl.BlockSpec((seq, dq + 2 * dkv), lambda b: (b, 0)),
                  small, small],
        out_specs=[pl.BlockSpec((seq, dq + 2 * dkv), lambda b: (b, 0)), small, small, heads],
        out_shape=[_sds((t, dq + 2 * dkv), BF16), _sds((1, LANES), F32), _sds((1, LANES), F32),
                   _sds((N_Q_HEADS, LANES), F32)],
        scratch_shapes=[pltpu.VMEM((seq, 2 * dkv), F32), pltpu.VMEM((2, N_PAIRS, 2 * KEYS, BLOCK), F32)],
        compiler_params=_params(("arbitrary",)))(sinks, do, qkv, qg_pair, kg_pair)


def _place():
    x, y, c = lax.axis_index("x"), lax.axis_index("y"), lax.axis_index("c")
    other_chips = [(1 - x, y), (x, 1 - y), (1 - x, 1 - y)]
    return x, y, c, other_chips


def _half_rows(c, rows):
    rh = rows // 2
    return pl.ds(pl.multiple_of(c * rh, BF16_ROWS), rh)


def _cast_own(name, w, place, layer=None):
    nl, r, cdim = w.shape
    first = 0
    if layer is not None:
        nl, first = 1, layer
    rt = _pick(r, (256, 128, 64, 32))

    def body(s_ref, w_ref, o_ref):
        o_ref[...] = w_ref[...].astype(BF16)

    grid_spec = pltpu.PrefetchScalarGridSpec(
        num_scalar_prefetch=1, grid=(nl, r // rt),
        in_specs=[pl.BlockSpec((None, rt, cdim), lambda l, i, s: (first + l, i, 0))],
        out_specs=pl.BlockSpec((None, None, rt, cdim), lambda l, i, s: (l, s[1], i, 0)))
    return pl.pallas_call(
        body, name=name, grid_spec=grid_spec, out_shape=_sds((nl, N_CHIPS, r, cdim), BF16),
        compiler_params=_params(("parallel", "parallel")))(place, w)


def _gather_protocol(outs, shapes, send_sems, recv_sems):
    n = len(outs)
    x, y, c, other_chips = _place()
    me_chip = 2 * x + y
    sibling = (x, y, 1 - c)

    def rows(u, chip, half):
        return outs[u].at[:, chip, _half_rows(half, shapes[u][2]), :]

    def copy(sem, part, to):
        return pltpu.make_async_remote_copy(src_ref=part, dst_ref=part, send_sem=send_sems.at[sem],
                                            recv_sem=recv_sems.at[sem], device_id=to, device_id_type=MESH)

    sends = []
    for u in range(n):
        for k, chip in enumerate(other_chips):
            cp = copy(6 * u + k, rows(u, me_chip, c), (*chip, c))
            cp.start()
            sends.append(cp)
    for u in range(n):
        for k, chip in enumerate(other_chips):
            got = rows(u, 2 * chip[0] + chip[1], c)
            copy(6 * u + k, got, (*chip, c)).wait_recv()
            cp = copy(6 * u + 3 + k, got, sibling)
            cp.start()
            sends.append(cp)
    for u in range(n):
        for k, chip in enumerate(other_chips):
            copy(6 * u + 3 + k, rows(u, 2 * chip[0] + chip[1], 1 - c), sibling).wait_recv()
    for cp in sends:
        cp.wait_send()


def _hbm_ref(a):
    return jax.new_ref(a, memory_space=pltpu.MemorySpace.HBM)


def _hbm_empty(shape, dtype):
    return jax.empty_ref(_sds(shape, dtype), memory_space=pltpu.MemorySpace.HBM)


def _sibling_peer():
    x, y, c, _ = _place()
    return [(x, y, 1 - c)]


def _chip_peers():
    x, y, c, other_chips = _place()
    return [(*chip, c) for chip in other_chips]


def _gather_peers():
    return _chip_peers() + _sibling_peer()


def _on_sequencer(name, collective_id, n_sems, peers, protocol):
    @pl.kernel(mesh=plsc.ScalarSubcoreMesh(axis_name="sequencer", num_cores=1), name=name,
               scratch_types=(pltpu.SemaphoreType.DMA((n_sems,)), pltpu.SemaphoreType.DMA((n_sems,))),
               compiler_params=pltpu.CompilerParams(collective_id=collective_id))
    def launch(send_sems, recv_sems):
        barrier = pltpu.get_barrier_semaphore()
        targets = peers()
        for peer in targets:
            pl.semaphore_signal(barrier, inc=1, device_id=peer, device_id_type=MESH)
        pl.semaphore_wait(barrier, len(targets))
        protocol(send_sems, recv_sems)

    launch()


def _seq_allgather(name, collective_id, bufs):
    shapes = [b.shape for b in bufs]
    refs = [_hbm_ref(b) for b in bufs]
    _on_sequencer(name, collective_id, 6 * len(bufs), _gather_peers,
                  lambda send_sems, recv_sems: _gather_protocol(refs, shapes, send_sems, recv_sems))
    return [r[...] for r in refs]


def _exchange_protocol(gs, outs, shapes, send_sems, recv_sems):
    x, y, c, _ = _place()
    sends = []
    for u in range(len(gs)):
        cp = pltpu.make_async_remote_copy(
            src_ref=gs[u].at[:, _half_rows(1 - c, shapes[u][1]), :], dst_ref=outs[u],
            send_sem=send_sems.at[u], recv_sem=recv_sems.at[u], device_id=(x, y, 1 - c), device_id_type=MESH)
        cp.start()
        sends.append(cp)
    for cp in sends:
        cp.wait_recv()
    for cp in sends:
        cp.wait_send()


def _seq_exchange(name, collective_id, grads):
    shapes = [g.shape for g in grads]
    gs = [_hbm_ref(g) for g in grads]
    outs = [_hbm_empty((s[0], s[1] // 2, s[2]), F32) for s in shapes]
    _on_sequencer(name, collective_id, len(grads), _sibling_peer,
                  lambda send_sems, recv_sems: _exchange_protocol(gs, outs, shapes, send_sems, recv_sems))
    return [o[...] for o in outs]


def _sum_halves(name, g, got, place, after):
    _, r, cdim = g.shape
    rh = r // 2
    rt = _pick(rh, (128, 64, 32, 16))
    nr = rh // rt

    def body(s_ref, g_ref, got_ref, after_ref, pb_ref, pf_ref):
        s = g_ref[...] + got_ref[...]
        pb_ref[...] = s.astype(BF16)

        @pl.when(pl.program_id(1) == s_ref[1])
        def _():
            pf_ref[...] = s

    grid_spec = pltpu.PrefetchScalarGridSpec(
        num_scalar_prefetch=1, grid=(nr, N_CHIPS),
        in_specs=[pl.BlockSpec((None, rt, cdim), lambda i, q, s: (q, s[0] * nr + i, 0)),
                  pl.BlockSpec((None, rt, cdim), lambda i, q, s: (q, i, 0)),
                  pl.BlockSpec(memory_space=pl.ANY)],
        out_specs=[pl.BlockSpec((None, rt, cdim), lambda i, q, s: (q, i, 0)),
                   pl.BlockSpec((rt, cdim), lambda i, q, s: (i, 0))])
    return pl.pallas_call(
        body, name=name, grid_spec=grid_spec,
        out_shape=[_sds((N_CHIPS, rh, cdim), BF16), _sds((rh, cdim), F32)],
        compiler_params=_params(("parallel", "arbitrary")))(place, g, got, after)


def _scatter_protocol(ps, outs, send_sems, recv_sems):
    x, y, c, other_chips = _place()
    sends = []
    for u in range(len(ps)):
        for k, chip in enumerate(other_chips):
            cp = pltpu.make_async_remote_copy(
                src_ref=ps[u].at[2 * chip[0] + chip[1]], dst_ref=outs[u].at[k],
                send_sem=send_sems.at[3 * u + k], recv_sem=recv_sems.at[3 * u + k],
                device_id=(*chip, c), device_id_type=MESH)
            cp.start()
            sends.append(cp)
    for cp in sends:
        cp.wait_recv()
    for cp in sends:
        cp.wait_send()


def _seq_scatter(name, collective_id, partials):
    ps = [_hbm_ref(p) for p in partials]
    outs = [_hbm_empty((3, p.shape[1], p.shape[2]), BF16) for p in partials]
    _on_sequencer(name, collective_id, 3 * len(partials), _chip_peers,
                  lambda send_sems, recv_sems: _scatter_protocol(ps, outs, send_sems, recv_sems))
    return [o[...] for o in outs]


def _sum_partials(name, own, got, place, layer, nl, prev, after):
    rh, cdim = own.shape
    rt = _pick(rh, (128, 64, 32, 16))
    nr = rh // rt

    def body(s_ref, own_ref, got_ref, *rest):
        o_ref = rest[-1]
        o_ref[...] = ((own_ref[...] + got_ref[0].astype(F32)) + got_ref[1].astype(F32)) + got_ref[2].astype(F32)

    in_specs = [pl.BlockSpec((rt, cdim), lambda i, s: (i, 0)), pl.BlockSpec((3, rt, cdim), lambda i, s: (0, i, 0)),
                pl.BlockSpec(memory_space=pl.ANY)]
    args = [place, own, got, after]
    aliases = {}
    if prev is not None:
        in_specs.append(pl.BlockSpec(memory_space=pl.ANY))
        args.append(prev)
        aliases = {4: 0}
    grid_spec = pltpu.PrefetchScalarGridSpec(
        num_scalar_prefetch=1, grid=(nr,), in_specs=in_specs,
        out_specs=pl.BlockSpec((None, rt, cdim), lambda i, s: (layer, s[0] * nr + i, 0)))
    return pl.pallas_call(
        body, name=name, grid_spec=grid_spec, out_shape=_sds((nl, 2 * rh, cdim), F32),
        input_output_aliases=aliases, compiler_params=_params(("parallel",)))(*args)


def _share_protocol(outs, shapes, units, send_sems, recv_sems):
    x, y, c, _ = _place()
    sends = []
    for u, (w, l) in enumerate(units):
        mine = outs[w].at[l, _half_rows(c, shapes[w][1]), :]
        cp = pltpu.make_async_remote_copy(src_ref=mine, dst_ref=mine, send_sem=send_sems.at[u],
                                          recv_sem=recv_sems.at[u], device_id=(x, y, 1 - c), device_id_type=MESH)
        cp.start()
        sends.append(cp)
    for u, (w, l) in enumerate(units):
        theirs = outs[w].at[l, _half_rows(1 - c, shapes[w][1]), :]
        pltpu.make_async_remote_copy(src_ref=theirs, dst_ref=theirs, send_sem=send_sems.at[u],
                                     recv_sem=recv_sems.at[u], device_id=(x, y, 1 - c),
                                     device_id_type=MESH).wait_recv()
    for cp in sends:
        cp.wait_send()


def _seq_share(name, collective_id, bufs):
    shapes = [b.shape for b in bufs]
    units = [(w, l) for w in range(len(bufs)) for l in range(shapes[w][0])]
    refs = [_hbm_ref(b) for b in bufs]
    _on_sequencer(name, collective_id, len(units), _sibling_peer,
                  lambda send_sems, recv_sems: _share_protocol(refs, shapes, units, send_sems, recv_sems))
    return [r[...] for r in refs]


def _gather_blocks(block_ref, all_ref, send_sems, recv_sems):
    x, y, c, _ = _place()
    me = 4 * x + 2 * y + c
    all_ref[me] = block_ref[...]
    sends = []
    for rel in range(1, 8):
        fx, fy, fc = (rel >> 2) & 1, (rel >> 1) & 1, rel & 1
        peer = (x ^ fx, y ^ fy, c ^ fc)
        cp = pltpu.make_async_remote_copy(src_ref=block_ref, dst_ref=all_ref.at[me], send_sem=send_sems.at[rel - 1],
                                          recv_sem=recv_sems.at[rel - 1], device_id=peer, device_id_type=MESH)
        cp.start()
        sends.append(cp)
    for cp in sends:
        cp.wait_recv()
    for cp in sends:
        cp.wait_send()


def _gather_conv_w(cw_block):
    r, d = cw_block.shape

    def body(b_ref, o_ref, all_ref, send_sems, recv_sems):
        _gather_blocks(b_ref, all_ref, send_sems, recv_sems)
        o_ref[...] = (all_ref[0] + all_ref[2]) + (all_ref[4] + all_ref[6])

    vm = pl.BlockSpec(memory_space=pltpu.VMEM)
    return pl.pallas_call(
        body, name="gather_conv_w", in_specs=[vm], out_specs=vm, out_shape=_sds((r, d), F32),
        scratch_shapes=[pltpu.VMEM((8, r, d), F32), pltpu.SemaphoreType.DMA((7,)), pltpu.SemaphoreType.DMA((7,))],
    )(cw_block)


def _adam(w, g, m, v):
    m_new = ADAM_B1 * m + (1.0 - ADAM_B1) * g
    v_new = ADAM_B2 * v + (1.0 - ADAM_B2) * (g * g)
    m_hat = m_new / (1.0 - ADAM_B1 ** ADAM_STEP)
    v_hat = v_new / (1.0 - ADAM_B2 ** ADAM_STEP)
    delta = -ADAM_LR * (m_hat / (jnp.sqrt(v_hat) + ADAM_EPS) + ADAM_WD * w)
    return delta, m_new, v_new


def _small_step(dnm0, dnm1, dnf0, dnf1, dcw, dqg, dkg, dsk, loss, w_blk, m_blk, v_blk):
    d = w_blk.shape[1]

    def body(dnm0_ref, dnm1_ref, dnf0_ref, dnf1_ref, dcw_ref, dqg_ref, dkg_ref, dsk_ref, loss_ref,
             w_ref, m_ref, v_ref, g_ref, dl_ref, mo_ref, vo_ref, blk_ref, all_ref, send_sems, recv_sems):
        blk_ref[...] = jnp.zeros_like(blk_ref)
        blk_ref[0:1, :] = jnp.sum(dnm0_ref[...], axis=0, keepdims=True)
        blk_ref[1:2, :] = jnp.sum(dnm1_ref[...], axis=0, keepdims=True)
        blk_ref[8:9, :] = jnp.sum(dnf0_ref[...], axis=0, keepdims=True)
        blk_ref[9:10, :] = jnp.sum(dnf1_ref[...], axis=0, keepdims=True)
        blk_ref[16:19, :] = dcw_ref[...]
        dqg_v = dqg_ref[...]
        dkg_v = dkg_ref[...]
        blk_ref[24:25, 0:LANES] = dqg_v + pltpu.roll(dqg_v, HEAD_DIM, 1)
        blk_ref[24:25, LANES:2 * LANES] = dkg_v + pltpu.roll(dkg_v, HEAD_DIM, 1)
        for h in range(N_Q_HEADS):
            blk_ref[24:25, 2 * LANES + h:2 * LANES + h + 1] = jnp.sum(dsk_ref[h:h + 1, :], axis=1, keepdims=True)
        blk_ref[24:25, 3 * LANES:4 * LANES] = jnp.broadcast_to(loss_ref[...], (1, LANES))
        _gather_blocks(blk_ref, all_ref, send_sems, recv_sems)
        g = all_ref[0]
        for dev in range(1, 8):
            g = g + all_ref[dev]
        g_ref[...] = g
        delta, m_new, v_new = _adam(w_ref[...], g, m_ref[...], v_ref[...])
        dl_ref[...] = delta
        mo_ref[...] = m_new
        vo_ref[...] = v_new

    vm = pl.BlockSpec(memory_space=pltpu.VMEM)
    blk = _sds((SMALL_ROWS, d), F32)
    return pl.pallas_call(
        body, name="small_step", in_specs=[vm] * 12, out_specs=[vm] * 4, out_shape=[blk] * 4,
        scratch_shapes=[pltpu.VMEM((SMALL_ROWS, d), F32), pltpu.VMEM((8, SMALL_ROWS, d), F32),
                        pltpu.SemaphoreType.DMA((7,)), pltpu.SemaphoreType.DMA((7,))],
    )(dnm0, dnm1, dnf0, dnf1, dcw, dqg, dkg, dsk, loss, w_blk, m_blk, v_blk)


def _adam_step(name, w, g, m, v):
    nl, r, cdim = w.shape
    rt = _pick(r, (128, 64, 32))

    def body(w_ref, g_ref, m_ref, v_ref, d_ref, mo_ref, vo_ref):
        delta, m_new, v_new = _adam(w_ref[...], g_ref[...], m_ref[...], v_ref[...])
        d_ref[...] = delta
        mo_ref[...] = m_new
        vo_ref[...] = v_new

    spec = pl.BlockSpec((None, rt, cdim), lambda l, i: (l, i, 0))
    return pl.pallas_call(
        body, name=name, grid=(nl, r // rt), in_specs=[spec] * 4, out_specs=[spec] * 3,
        out_shape=[_sds(w.shape, F32)] * 3,
        compiler_params=_params(("parallel", "parallel")))(w, g, m, v)


def _pad_rows(a, rows=SUBLANES):
    return jnp.pad(a, ((0, rows - a.shape[0]), (0, 0)))


def _small_block(nm, nf, cw_local, qg, kg, sk, chip):
    d = nm.shape[1]
    cw_rows = lax.dynamic_update_slice(jnp.zeros((SUBLANES, d), F32), cw_local, (0, chip * cw_local.shape[1]))
    misc = jnp.concatenate([qg, qg, kg, kg, jnp.pad(sk, ((0, 0), (0, LANES - sk.shape[1]))),
                            jnp.zeros((1, d - 3 * LANES), F32)], axis=1)
    return jnp.concatenate([_pad_rows(nm), _pad_rows(nf), cw_rows, _pad_rows(misc)], axis=0)


def _unpack_small(blk, chip, cw_cols):
    cw = lax.dynamic_slice(blk[16:19], (0, chip * cw_cols), (3, cw_cols))[None]
    return dict(norm_mixer=blk[0:2], norm_ffn=blk[8:10], conv_w=cw, attn_q_gain=blk[24:25, 0:HEAD_DIM],
                attn_k_gain=blk[24:25, LANES:LANES + HEAD_DIM], attn_sinks=blk[24:25, 2 * LANES:2 * LANES + N_Q_HEADS])


WEIGHT_NAMES = ("conv_w_in", "conv_w", "conv_w_out", "attn_w_qkv", "attn_q_gain", "attn_k_gain", "attn_sinks",
                "attn_w_o", "norm_mixer", "norm_ffn", "ffn_w_gate_up", "ffn_w_down")
BIG = ("conv_w_in", "conv_w_out", "attn_w_qkv", "attn_w_o", "ffn_w_gate_up", "ffn_w_down")


def kernel(x, conv_w_in, conv_w, conv_w_out, attn_w_qkv, attn_q_gain, attn_k_gain, attn_sinks, attn_w_o, norm_mixer, norm_ffn, ffn_w_gate_up, ffn_w_down, loss_target, m_conv_w_in, m_conv_w, m_conv_w_out, m_attn_w_qkv, m_attn_q_gain, m_attn_k_gain, m_attn_sinks, m_attn_w_o, m_norm_mixer, m_norm_ffn, m_ffn_w_gate_up, m_ffn_w_down, v_conv_w_in, v_conv_w, v_conv_w_out, v_attn_w_qkv, v_attn_q_gain, v_attn_k_gain, v_attn_sinks, v_attn_w_o, v_norm_mixer, v_norm_ffn, v_ffn_w_gate_up, v_ffn_w_down):
    w = dict(conv_w_in=conv_w_in, conv_w=conv_w, conv_w_out=conv_w_out, attn_w_qkv=attn_w_qkv,
             attn_q_gain=attn_q_gain, attn_k_gain=attn_k_gain, attn_sinks=attn_sinks, attn_w_o=attn_w_o,
             norm_mixer=norm_mixer, norm_ffn=norm_ffn, ffn_w_gate_up=ffn_w_gate_up, ffn_w_down=ffn_w_down)
    m = dict(conv_w_in=m_conv_w_in, conv_w=m_conv_w, conv_w_out=m_conv_w_out, attn_w_qkv=m_attn_w_qkv,
             attn_q_gain=m_attn_q_gain, attn_k_gain=m_attn_k_gain, attn_sinks=m_attn_sinks, attn_w_o=m_attn_w_o,
             norm_mixer=m_norm_mixer, norm_ffn=m_norm_ffn, ffn_w_gate_up=m_ffn_w_gate_up, ffn_w_down=m_ffn_w_down)
    v = dict(conv_w_in=v_conv_w_in, conv_w=v_conv_w, conv_w_out=v_conv_w_out, attn_w_qkv=v_attn_w_qkv,
             attn_q_gain=v_attn_q_gain, attn_k_gain=v_attn_k_gain, attn_sinks=v_attn_sinks, attn_w_o=v_attn_w_o,
             norm_mixer=v_norm_mixer, norm_ffn=v_norm_ffn, ffn_w_gate_up=v_ffn_w_gate_up, ffn_w_down=v_ffn_w_down)

    nseq, seq, d = x.shape
    t = nseq * seq
    chip = 2 * lax.axis_index("x") + lax.axis_index("y")
    core = lax.axis_index("c")
    place = jnp.stack([core, chip]).astype(jnp.int32)
    x0 = x.reshape(t, d)
    tgt = loss_target.reshape(t, d)

    cw_block = lax.dynamic_update_slice(jnp.zeros((SUBLANES, d), F32), conv_w[0], (0, chip * conv_w.shape[2]))
    cw_full = _gather_conv_w(cw_block)[0:3]
    def cast(k, layer=None):
        return _cast_own(f"cast_{k}" + ("" if layer is None else str(layer)), w[k], place, layer)

    w_in, w_out = _seq_allgather("allgather_conv", 1, [cast("conv_w_in"), cast("conv_w_out")])
    w_gu0, w_dn0 = _seq_allgather("allgather_ffn0", 2, [cast("ffn_w_gate_up", 0), cast("ffn_w_down", 0)])
    w_qkv, w_o, w_gu1, w_dn1 = _seq_allgather(
        "allgather_rest", 3, [cast("attn_w_qkv"), cast("attn_w_o"), cast("ffn_w_gate_up", 1), cast("ffn_w_down", 1)])
    w_out = w_out.reshape(1, d, d)
    w_o = w_o.reshape(1, d, d)
    w_gu = [w_gu0, w_gu1]
    w_dn = [w_dn0.reshape(1, D_FF, d), w_dn1.reshape(1, D_FF, d)]

    qg_pair = jnp.concatenate([attn_q_gain, attn_q_gain], axis=1)
    kg_pair = jnp.concatenate([attn_k_gain, attn_k_gain], axis=1)

    def ffn_bwd(i, dxo, xin, h, g, u, a):
        g_dn = _wgrad_down(f"ffn{i}_down_wgrad", a, dxo, D_FF // 2)
        dg, du = _mm_down_t_swiglu(f"ffn{i}_down_dgrad", dxo, w_dn[i], 0, g, u)
        g_gu = _wgrad_up2(f"ffn{i}_up_wgrad", h, dg, du)
        dxi, dgain = _dgrad_norm_ffn(f"ffn{i}_up_dgrad", dg, du, w_gu[i], 0, xin, norm_ffn[i:i + 1], dxo)
        return dxi, dgain, g_gu, g_dn

    h0 = _rms_fwd("conv_norm", x0, norm_mixer[0:1])
    bcx = _mm_up_joined("conv_in", h0, w_in, 512)
    z = _conv_fwd(bcx, cw_full, nseq, seq)
    x1, h1 = _mm_down_norm("conv_out", z, w_out, 0, x0, norm_ffn[0:1])
    g0, u0, a0 = _mm_up_swiglu("ffn0_up", h1, w_gu[0], 0)
    x2, h2 = _mm_down_norm("ffn0_down", a0, w_dn[0], 0, x1, norm_mixer[1:2])
    qkv = _mm_up_joined("attn_qkv", h2, w_qkv, 1024)
    o = _attn_fwd(qkv, qg_pair, kg_pair, attn_sinks, nseq, seq)
    x3, h3 = _mm_down_norm("attn_out", o, w_o, 0, x2, norm_ffn[1:2])
    g1, u1, a1 = _mm_up_swiglu("ffn1_up", h3, w_gu[1], 0)
    dy, loss_part = _mm_down_loss("ffn1_down", a1, w_dn[1], 0, x3, tgt)

    finished = {k: None for k in BIG}

    def exchange(tag, cid, units):
        return units, _seq_exchange(f"exchange_{tag}", cid, [g for _, _, g in units])

    def scatter(tag, cid, group, after):
        units, got = group
        sums = [_sum_halves(f"sum_halves_{k}{l}", g, r, place, after) for (k, l, g), r in zip(units, got)]
        return units, sums, _seq_scatter(f"scatter_{tag}", cid, [pb for pb, _ in sums])

    def finish(group, after):
        units, sums, arrived = group
        for (k, l, _), (_, pf), r in zip(units, sums, arrived):
            finished[k] = _sum_partials(f"sum_partials_{k}{l}", pf, r, place, l, w[k].shape[0], finished[k], after)

    dx3, dnf1, g_gu1, g_dn1 = ffn_bwd(1, dy, x3, h3, g1, u1, a1)
    ffn1 = exchange("ffn1", 4, [("ffn_w_down", 1, g_dn1), ("ffn_w_gate_up", 1, g_gu1)])
    g_o = _wgrad_down("attn_out_wgrad", o, dx3, d)
    do = _mm_down_t("attn_out_dgrad", dx3, w_o, 0)
    ffn1 = scatter("ffn1", 8, ffn1, do)
    dqkv, dqg, dkg, dsk = _attn_bwd(do, qkv, qg_pair, kg_pair, attn_sinks, nseq, seq)
    g_qkv = _wgrad_joined("attn_qkv_wgrad", h2, dqkv)
    attn = exchange("attn", 5, [("attn_w_o", 0, g_o), ("attn_w_qkv", 0, g_qkv)])
    dx2, dnm1 = _dgrad_norm_qkv("attn_qkv_dgrad", dqkv, w_qkv, x2, norm_mixer[1:2], dx3)
    finish(ffn1, dx2)
    attn = scatter("attn", 9, attn, dx2)
    dx1, dnf0, g_gu0, g_dn0 = ffn_bwd(0, dx2, x1, h1, g0, u0, a0)
    ffn0 = exchange("ffn0", 6, [("ffn_w_down", 0, g_dn0), ("ffn_w_gate_up", 0, g_gu0)])
    g_out = _wgrad_down("conv_out_wgrad", z, dx1, d)
    dz = _mm_down_t("conv_out_dgrad", dx1, w_out, 0)
    finish(attn, dz)
    ffn0 = scatter("ffn0", 10, ffn0, dz)
    dbcx, dcw = _conv_bwd(dz, bcx, cw_full, nseq, seq)
    g_in = _wgrad_conv_in("conv_in_wgrad", h0, dbcx, conv_w_in.shape[2])
    conv = exchange("conv", 7, [("conv_w_out", 0, g_out), ("conv_w_in", 0, g_in)])
    dx0, dnm0 = _dgrad_norm_conv("conv_in_dgrad", dbcx, w_in, x0, norm_mixer[0:1], dx1)
    finish(ffn0, dx0)
    late = ("attn_w_qkv", "attn_w_o", "ffn_w_gate_up", "ffn_w_down")
    grads_late = _seq_share("share_late", 12, [finished[k] for k in late])
    conv = scatter("conv", 11, conv, dx0)

    grad, delta, new_m, new_v = {}, {}, {}, {}

    def adam(k, g):
        grad[k] = g
        delta[k], new_m[k], new_v[k] = _adam_step(f"adam_{k}", w[k], g, m[k], v[k])

    for k, g in zip(late, grads_late):
        adam(k, g)
    finish(conv, delta["ffn_w_gate_up"])
    last = ("conv_w_in", "conv_w_out")
    for k, g in zip(last, _seq_share("share_last", 13, [finished[k] for k in last])):
        adam(k, g)

    def blocks(src):
        return _small_block(src["norm_mixer"], src["norm_ffn"], src["conv_w"][0], src["attn_q_gain"],
                            src["attn_k_gain"], src["attn_sinks"], chip)

    g_blk, d_blk, m_blk, v_blk = _small_step(dnm0, dnm1, dnf0, dnf1, dcw, dqg, dkg, dsk, loss_part,
                                             blocks(w), blocks(m), blocks(v))
    cw_cols = conv_w.shape[2]
    for dst, blk in ((grad, g_blk), (delta, d_blk), (new_m, m_blk), (new_v, v_blk)):
        dst.update(_unpack_small(blk, chip, cw_cols))
    loss = g_blk[24, 3 * LANES]

    return (loss, dx0.reshape(nseq, seq, d), *[grad[k] for k in WEIGHT_NAMES], *[delta[k] for k in WEIGHT_NAMES],
            *[new_m[k] for k in WEIGHT_NAMES], *[new_v[k] for k in WEIGHT_NAMES])
```

```python
import jax
import jax.numpy as jnp
from jax import lax
from jax.experimental import pallas as pl
from jax.experimental.pallas import tpu as pltpu
from jax.experimental.pallas import tpu_sc as plsc

F32 = jnp.float32
BF16 = jnp.bfloat16

D_MODEL = 1024
D_FF = 2816
N_Q_HEADS = 16
N_KV_HEADS = 4
HEAD_DIM = 64
WINDOW = 128
BLOCK = 128
EPS = 1e-6
N_CHIPS = 4
LANES = 128
SUBLANES = 8
BF16_ROWS = 16
MXU_COLS = 256
VMEM_LIMIT = 48 * 1024 * 1024
ADAM_LR, ADAM_B1, ADAM_B2, ADAM_EPS, ADAM_WD, ADAM_STEP = 0.001, 0.9, 0.999, 1e-08, 0.01, 10
ALIBI_SLOPES = tuple(2.0 ** (-8.0 * (h + 1) / N_Q_HEADS) for h in range(N_Q_HEADS))
SMALL_ROWS = 32
MESH = pl.DeviceIdType.MESH

NN = ((1,), (0,))
NT = ((1,), (1,))
TN = ((0,), (0,))


def _dot(a, b, dims):
    return lax.dot_general(a, b, (dims, ((), ())), preferred_element_type=F32)


def _pick(n, cands):
    for c in cands:
        if n % c == 0:
            return c
    raise ValueError((n, cands))


def _row_tile(rows, row_bytes, cap_bytes):
    fits = [r for r in range(BF16_ROWS, rows + 1, BF16_ROWS) if rows % r == 0 and r * row_bytes <= cap_bytes]
    if not fits:
        raise ValueError((rows, row_bytes, cap_bytes))
    return fits[-1]


ELEMENTWISE_BLOCK = 3 << 19


def _resident(block_shape, index_map):
    return pl.BlockSpec(block_shape, index_map, pipeline_mode=pl.Buffered(1))


def _params(sem):
    return pltpu.CompilerParams(dimension_semantics=sem, vmem_limit_bytes=VMEM_LIMIT)


def _sds(shape, dtype):
    return jax.ShapeDtypeStruct(shape, dtype)


def _rms(xv):
    return lax.rsqrt(jnp.mean(xv * xv, axis=-1, keepdims=True) + EPS)


def _sigmoid(g):
    return 1.0 / (1.0 + jnp.exp(-g))


def _mm_up_joined(name, a, w4, tm_pref):
    t, k = a.shape
    _, _, _, nq = w4.shape
    tm = _pick(t, (tm_pref, 256, 128))

    def body(a_ref, w_ref, o_ref, wcat_ref):
        @pl.when(pl.program_id(0) == 0)
        def _():
            for q in range(N_CHIPS):
                wcat_ref[:, q * nq:(q + 1) * nq] = w_ref[q]

        o_ref[...] = _dot(a_ref[...], wcat_ref[...], NN).astype(BF16)

    return pl.pallas_call(
        body, name=name, grid=(t // tm,),
        in_specs=[pl.BlockSpec((tm, k), lambda i: (i, 0)),
                  pl.BlockSpec((None, N_CHIPS, k, nq), lambda i: (0, 0, 0, 0))],
        out_specs=pl.BlockSpec((tm, N_CHIPS * nq), lambda i: (i, 0)),
        out_shape=_sds((t, N_CHIPS * nq), BF16),
        scratch_shapes=[pltpu.VMEM((k, N_CHIPS * nq), BF16)],
        compiler_params=_params(("arbitrary",)))(a, w4)


def _mm_up_swiglu(name, h, w4, layer):
    t, k = h.shape
    _, _, _, nq = w4.shape
    tm = _pick(t, (512, 256, 128))

    def body(h_ref, wg_ref, wu_ref, g_ref, u_ref, a_ref):
        hv = h_ref[...]
        g = _dot(hv, wg_ref[...], NN)
        u = _dot(hv, wu_ref[...], NN)
        g_ref[...] = g.astype(BF16)
        u_ref[...] = u.astype(BF16)
        a_ref[...] = (g * _sigmoid(g) * u).astype(BF16)

    half = N_CHIPS // 2
    out = pl.BlockSpec((tm, nq), lambda j, i: (i, j))
    return pl.pallas_call(
        body, name=name, grid=(half, t // tm),
        in_specs=[pl.BlockSpec((tm, k), lambda j, i: (i, 0)),
                  pl.BlockSpec((None, None, k, nq), lambda j, i: (layer, j, 0, 0)),
                  pl.BlockSpec((None, None, k, nq), lambda j, i: (layer, half + j, 0, 0))],
        out_specs=[out, out, out],
        out_shape=[_sds((t, half * nq), BF16)] * 3,
        compiler_params=_params(("parallel", "parallel")))(h, w4, w4)


def _mm_down_norm(name, a, w, layer, res, gain):
    t, kf = a.shape
    _, _, n = w.shape
    tm = _pick(t, (1024, 512, 256, 128))

    def body(a_ref, w_ref, r_ref, g_ref, o_ref, h_ref):
        xo = r_ref[...] + _dot(a_ref[...], w_ref[...], NN)
        o_ref[...] = xo
        h_ref[...] = ((xo * _rms(xo)) * g_ref[...]).astype(BF16)

    row = pl.BlockSpec((tm, n), lambda i: (i, 0))
    return pl.pallas_call(
        body, name=name, grid=(t // tm,),
        in_specs=[pl.BlockSpec((tm, kf), lambda i: (i, 0)),
                  _resident((None, kf, n), lambda i: (layer, 0, 0)),
                  row, pl.BlockSpec((1, n), lambda i: (0, 0))],
        out_specs=[row, row],
        out_shape=[_sds((t, n), F32), _sds((t, n), BF16)],
        compiler_params=_params(("parallel",)))(a, w, res, gain)


def _mm_down_loss(name, a, w, layer, res, tgt):
    t, kf = a.shape
    _, _, n = w.shape
    tm = _pick(t, (1024, 512, 256, 128))
    steps = t // tm

    def body(a_ref, w_ref, r_ref, t_ref, dy_ref, l_ref, acc_ref):
        i = pl.program_id(0)

        @pl.when(i == 0)
        def _():
            acc_ref[...] = jnp.zeros_like(acc_ref)

        e = (r_ref[...] + _dot(a_ref[...], w_ref[...], NN)) - t_ref[...]
        dy_ref[...] = e * (1.0 / n)
        acc_ref[...] += (e * e).reshape(tm // SUBLANES, SUBLANES, n).sum(axis=0)

        @pl.when(i == steps - 1)
        def _():
            l_ref[...] = jnp.sum(acc_ref[...], keepdims=True) * (0.5 / n)

    row = pl.BlockSpec((tm, n), lambda i: (i, 0))
    return pl.pallas_call(
        body, name=name, grid=(steps,),
        in_specs=[pl.BlockSpec((tm, kf), lambda i: (i, 0)),
                  _resident((None, kf, n), lambda i: (layer, 0, 0)), row, row],
        out_specs=[row, pl.BlockSpec((1, 1), lambda i: (0, 0))],
        out_shape=[_sds((t, n), F32), _sds((1, 1), F32)],
        scratch_shapes=[pltpu.VMEM((SUBLANES, n), F32)],
        compiler_params=_params(("arbitrary",)))(a, w, res, tgt)


def _mm_down_t(name, dx, w, layer):
    t, n = dx.shape
    _, kf, _ = w.shape
    tm = _pick(t, (512, 256, 128))

    def body(a_ref, w_ref, o_ref):
        o_ref[...] = _dot(a_ref[...].astype(BF16), w_ref[...], NT).astype(BF16)

    return pl.pallas_call(
        body, name=name, grid=(t // tm,),
        in_specs=[pl.BlockSpec((tm, n), lambda i: (i, 0)),
                  pl.BlockSpec((None, kf, n), lambda i: (layer, 0, 0))],
        out_specs=pl.BlockSpec((tm, kf), lambda i: (i, 0)),
        out_shape=_sds((t, kf), BF16),
        compiler_params=_params(("parallel",)))(dx, w)


def _mm_down_t_swiglu(name, dx, w, layer, g, u):
    t, n = dx.shape
    f = g.shape[1]
    tn = f // 2
    tm = _pick(t, (512, 256, 128))

    def body(a_ref, w_ref, g_ref, u_ref, dg_ref, du_ref):
        da = _dot(a_ref[...].astype(BF16), w_ref[...], NT)
        gv = g_ref[...].astype(F32)
        sg = _sigmoid(gv)
        dg_ref[...] = (da * u_ref[...].astype(F32) * (sg * (1.0 + gv * (1.0 - sg)))).astype(BF16)
        du_ref[...] = (da * (gv * sg)).astype(BF16)

    tile = pl.BlockSpec((tm, tn), lambda j, i: (i, j))
    return pl.pallas_call(
        body, name=name, grid=(f // tn, t // tm),
        in_specs=[pl.BlockSpec((tm, n), lambda j, i: (i, 0)),
                  pl.BlockSpec((None, tn, n), lambda j, i: (layer, j, 0)), tile, tile],
        out_specs=[tile, tile],
        out_shape=[_sds((t, f), BF16)] * 2,
        compiler_params=_params(("parallel", "parallel")))(dx, w, g, u)


def _dgrad_norm(name, acts, act_blocks, pieces, w4, layer, x, gain, dres):
    t, d = x.shape
    _, _, k, nq = w4.shape
    tm = _pick(t, (512, 256, 128))
    n_act = len(acts)

    def body(*refs):
        act_refs = refs[:n_act]
        w_ref, x_ref, g_ref, dr_ref, dx_ref, dg_ref = refs[n_act:]

        @pl.when(pl.program_id(0) == 0)
        def _():
            dg_ref[...] = jnp.zeros_like(dg_ref)

        dh = None
        for a_tile, w_tile in pieces(act_refs, w_ref):
            term = _dot(a_tile, w_tile, NT)
            dh = term if dh is None else dh + term
        xv = x_ref[...]
        r = _rms(xv)
        xhat = xv * r
        gd = dh * g_ref[...]
        dx_ref[...] = dr_ref[...] + r * (gd - xhat * jnp.mean(gd * xhat, axis=-1, keepdims=True))
        dg_ref[...] += (dh * xhat).reshape(tm // SUBLANES, SUBLANES, d).sum(axis=0)

    row = pl.BlockSpec((tm, d), lambda i: (i, 0))
    return pl.pallas_call(
        body, name=name, grid=(t // tm,),
        in_specs=[*act_blocks(tm),
                  _resident((None, N_CHIPS, k, nq), lambda i: (layer, 0, 0, 0)),
                  row, pl.BlockSpec((1, d), lambda i: (0, 0)), row],
        out_specs=[row, pl.BlockSpec((SUBLANES, d), lambda i: (0, 0))],
        out_shape=[_sds((t, d), F32), _sds((SUBLANES, d), F32)],
        compiler_params=_params(("arbitrary",)))(*acts, w4, x, gain, dres)


def _dgrad_norm_ffn(name, dg, du, w4, layer, x, gain, dres):
    nq = w4.shape[3]
    f = dg.shape[1]

    def blocks(tm):
        return [pl.BlockSpec((tm, f), lambda i: (i, 0))] * 2

    def pieces(act_refs, w_ref):
        dg_ref, du_ref = act_refs
        return [(dg_ref[:, 0:nq], w_ref[0]), (dg_ref[:, nq:2 * nq], w_ref[1]),
                (du_ref[:, 0:nq], w_ref[2]), (du_ref[:, nq:2 * nq], w_ref[3])]

    return _dgrad_norm(name, [dg, du], blocks, pieces, w4, layer, x, gain, dres)


def _dgrad_norm_qkv(name, dqkv, w4, x, gain, dres):
    nq = w4.shape[3]

    def blocks(tm):
        return [pl.BlockSpec((tm, N_CHIPS * nq), lambda i: (i, 0))]

    def pieces(act_refs, w_ref):
        return [(act_refs[0][:, q * nq:(q + 1) * nq], w_ref[q]) for q in range(N_CHIPS)]

    return _dgrad_norm(name, [dqkv], blocks, pieces, w4, 0, x, gain, dres)


def _dgrad_norm_conv(name, d3, w4, x, gain, dres):
    _, _, d = d3.shape
    nq = w4.shape[3]
    per_part, per_q = d // MXU_COLS, nq // MXU_COLS

    def blocks(tm):
        return [pl.BlockSpec((3, tm, d), lambda i: (0, i, 0))]

    def pieces(act_refs, w_ref):
        out = []
        for jb in range(3 * per_part):
            ca, cw = (jb % per_part) * MXU_COLS, (jb % per_q) * MXU_COLS
            out.append((act_refs[0][jb // per_part, :, ca:ca + MXU_COLS], w_ref[jb // per_q, :, cw:cw + MXU_COLS]))
        return out

    return _dgrad_norm(name, [d3], blocks, pieces, w4, 0, x, gain, dres)


def _wgrad_up2(name, h, dg, du):
    t, k = h.shape
    nq = dg.shape[1] // 2
    tk = _pick(t, (1024, 512, 256, 128))
    steps = t // tk
    half = N_CHIPS // 2

    def body(h_ref, dg_ref, du_ref, o_ref):
        q = pl.program_id(0)

        @pl.when(pl.program_id(1) == 0)
        def _():
            o_ref[...] = jnp.zeros_like(o_ref)

        @pl.when(q < half)
        def _():
            o_ref[...] += _dot(h_ref[...], dg_ref[...], TN)

        @pl.when(q >= half)
        def _():
            o_ref[...] += _dot(h_ref[...], du_ref[...], TN)

    return pl.pallas_call(
        body, name=name, grid=(N_CHIPS, steps),
        in_specs=[pl.BlockSpec((tk, k), lambda q, s: (s, 0)),
                  pl.BlockSpec((tk, nq), lambda q, s: (jnp.where(q < half, s, steps - 1), jnp.minimum(q, half - 1))),
                  pl.BlockSpec((tk, nq), lambda q, s: (jnp.where(q >= half, s, 0), jnp.maximum(q - half, 0)))],
        out_specs=pl.BlockSpec((None, k, nq), lambda q, s: (q, 0, 0)),
        out_shape=_sds((N_CHIPS, k, nq), F32),
        compiler_params=_params(("parallel", "arbitrary")))(h, dg, du)


def _wgrad_joined(name, h, dy):
    t, k = h.shape
    nq = dy.shape[1] // N_CHIPS
    tk = _pick(t, (1024, 512, 256, 128))

    def body(h_ref, dy_ref, o_ref):
        @pl.when(pl.program_id(0) == 0)
        def _():
            o_ref[...] = jnp.zeros_like(o_ref)

        res = _dot(h_ref[...], dy_ref[...], TN)
        for q in range(N_CHIPS):
            o_ref[q] += res[:, q * nq:(q + 1) * nq]

    return pl.pallas_call(
        body, name=name, grid=(t // tk,),
        in_specs=[pl.BlockSpec((tk, k), lambda s: (s, 0)), pl.BlockSpec((tk, N_CHIPS * nq), lambda s: (s, 0))],
        out_specs=pl.BlockSpec((N_CHIPS, k, nq), lambda s: (0, 0, 0)),
        out_shape=_sds((N_CHIPS, k, nq), F32),
        compiler_params=_params(("arbitrary",)))(h, dy)


def _wgrad_conv_in(name, h, d3, nq):
    t, k = h.shape
    d = d3.shape[2]
    per_part, per_q = d // MXU_COLS, nq // MXU_COLS
    tk = _pick(t, (512, 256, 128))

    def body(h_ref, d_ref, o_ref):
        @pl.when(pl.program_id(0) == 0)
        def _():
            o_ref[...] = jnp.zeros_like(o_ref)

        hv = h_ref[...]
        for part in range(3):
            res = _dot(hv, d_ref[part], TN)
            for cc in range(per_part):
                jb = part * per_part + cc
                co = (jb % per_q) * MXU_COLS
                o_ref[jb // per_q, :, co:co + MXU_COLS] += res[:, cc * MXU_COLS:(cc + 1) * MXU_COLS]

    return pl.pallas_call(
        body, name=name, grid=(t // tk,),
        in_specs=[pl.BlockSpec((tk, k), lambda s: (s, 0)), pl.BlockSpec((3, tk, d), lambda s: (0, s, 0))],
        out_specs=pl.BlockSpec((N_CHIPS, k, nq), lambda s: (0, 0, 0)),
        out_shape=_sds((N_CHIPS, k, nq), F32),
        compiler_params=_params(("arbitrary",)))(h, d3)


def _wgrad_down(name, a, dx, tmw):
    t, kf = a.shape
    n = dx.shape[1]
    tk = _pick(t, (1024, 512, 256, 128))

    def body(a_ref, b_ref, o_ref):
        @pl.when(pl.program_id(1) == 0)
        def _():
            o_ref[...] = jnp.zeros_like(o_ref)

        o_ref[...] += _dot(a_ref[...], b_ref[...].astype(BF16), TN)

    g = pl.pallas_call(
        body, name=name, grid=(kf // tmw, t // tk),
        in_specs=[pl.BlockSpec((tk, tmw), lambda j, s: (s, j)), pl.BlockSpec((tk, n), lambda j, s: (s, 0))],
        out_specs=pl.BlockSpec((tmw, n), lambda j, s: (j, 0)),
        out_shape=_sds((kf, n), F32),
        compiler_params=_params(("parallel", "arbitrary")))(a, dx)
    return g.reshape(N_CHIPS, kf // N_CHIPS, n)


def _rms_fwd(name, x, gain):
    t, d = x.shape
    tm = _pick(t, (512, 256, 128))

    def body(x_ref, g_ref, h_ref):
        xv = x_ref[...]
        h_ref[...] = ((xv * _rms(xv)) * g_ref[...]).astype(BF16)

    return pl.pallas_call(
        body, name=name, grid=(t // tm,),
        in_specs=[pl.BlockSpec((tm, d), lambda i: (i, 0)), pl.BlockSpec((1, d), lambda i: (0, 0))],
        out_specs=pl.BlockSpec((tm, d), lambda i: (i, 0)),
        out_shape=_sds((t, d), BF16),
        compiler_params=_params(("parallel",)))(x, gain)


def _shift_rows(u, k, rows):
    s = u.shape[0]
    if k > 0:
        return jnp.where(rows >= k, pltpu.roll(u, k, 0), 0.0)
    return jnp.where(rows < s + k, pltpu.roll(u, s + k, 0), 0.0)


def _conv_fwd(bcx, cw, nseq, seq):
    t, d3 = bcx.shape
    d = d3 // 3
    cb = MXU_COLS
    nj = d // cb

    def body(b_ref, c_ref, x_ref, cw_ref, z_ref):
        u = b_ref[...].astype(F32) * x_ref[...].astype(F32)
        rows = lax.broadcasted_iota(jnp.int32, u.shape, 0)
        cwv = cw_ref[...]
        y = cwv[2:3] * u + cwv[1:2] * _shift_rows(u, 1, rows) + cwv[0:1] * _shift_rows(u, 2, rows)
        z_ref[...] = (c_ref[...].astype(F32) * y).astype(BF16)

    return pl.pallas_call(
        body, name="conv_fwd", grid=(nseq, nj),
        in_specs=[pl.BlockSpec((seq, cb), lambda b, j: (b, j)),
                  pl.BlockSpec((seq, cb), lambda b, j: (b, nj + j)),
                  pl.BlockSpec((seq, cb), lambda b, j: (b, 2 * nj + j)),
                  pl.BlockSpec((3, cb), lambda b, j: (0, j))],
        out_specs=pl.BlockSpec((seq, cb), lambda b, j: (b, j)),
        out_shape=_sds((t, d), BF16),
        compiler_params=_params(("parallel", "parallel")))(bcx, bcx, bcx, cw)


def _conv_bwd(dz, bcx, cw, nseq, seq):
    t, d3 = bcx.shape
    d = d3 // 3
    cb = MXU_COLS
    nj = d // cb

    def body(dz_ref, b_ref, c_ref, x_ref, cw_ref, o_ref, dcw_ref):
        @pl.when(pl.program_id(1) == 0)
        def _():
            dcw_ref[...] = jnp.zeros_like(dcw_ref)

        b = b_ref[...].astype(F32)
        c = c_ref[...].astype(F32)
        xv = x_ref[...].astype(F32)
        dzv = dz_ref[...].astype(F32)
        u = b * xv
        rows = lax.broadcasted_iota(jnp.int32, u.shape, 0)
        u1 = _shift_rows(u, 1, rows)
        u2 = _shift_rows(u, 2, rows)
        cwv = cw_ref[...]
        y = cwv[2:3] * u + cwv[1:2] * u1 + cwv[0:1] * u2
        dyc = dzv * c
        du = cwv[2:3] * dyc + cwv[1:2] * _shift_rows(dyc, -1, rows) + cwv[0:1] * _shift_rows(dyc, -2, rows)
        o_ref[0] = (du * xv).astype(BF16)
        o_ref[1] = (dzv * y).astype(BF16)
        o_ref[2] = (du * b).astype(BF16)
        s0 = jnp.sum(dyc * u2, axis=0, keepdims=True)
        s1 = jnp.sum(dyc * u1, axis=0, keepdims=True)
        s2 = jnp.sum(dyc * u, axis=0, keepdims=True)
        tap = lax.broadcasted_iota(jnp.int32, (3, cb), 0)
        dcw_ref[...] += jnp.where(tap == 0, s0, jnp.where(tap == 1, s1, s2))

    return pl.pallas_call(
        body, name="conv_bwd", grid=(nj, nseq),
        in_specs=[pl.BlockSpec((seq, cb), lambda j, b: (b, j)),
                  pl.BlockSpec((seq, cb), lambda j, b: (b, j)),
                  pl.BlockSpec((seq, cb), lambda j, b: (b, nj + j)),
                  pl.BlockSpec((seq, cb), lambda j, b: (b, 2 * nj + j)),
                  pl.BlockSpec((3, cb), lambda j, b: (0, j))],
        out_specs=[pl.BlockSpec((3, seq, cb), lambda j, b: (0, b, j)),
                   pl.BlockSpec((3, cb), lambda j, b: (0, j))],
        out_shape=[_sds((3, t, d), BF16), _sds((3, d), F32)],
        compiler_params=_params(("parallel", "arbitrary")))(dz, bcx, bcx, bcx, cw)


def _pair_norm(x, gain_pair, low):
    sq = x * x
    ss_lo = jnp.sum(jnp.where(low, sq, 0.0), axis=-1, keepdims=True)
    ss_hi = jnp.sum(jnp.where(low, 0.0, sq), axis=-1, keepdims=True)
    r = lax.rsqrt(jnp.where(low, ss_lo, ss_hi) * (1.0 / HEAD_DIM) + EPS)
    xhat = x * r
    return xhat * gain_pair, xhat, r


KEYS = 2 * BLOCK
QK_SCALE = 1.0 / (HEAD_DIM ** 0.5)
N_PAIRS = N_Q_HEADS // 2


def _fill_bias(bias_ref):
    rows = lax.broadcasted_iota(jnp.int32, (2 * KEYS, BLOCK), 0)
    qi = lax.broadcasted_iota(jnp.int32, (2 * KEYS, BLOCK), 1)
    odd_head = rows >= KEYS
    kj = jnp.where(odd_head, rows - KEYS, rows)
    for later in range(2):
        dist = later * BLOCK + qi - kj
        mask = jnp.logical_and(dist >= 0, dist < WINDOW)
        distf = dist.astype(F32)
        for j in range(N_PAIRS):
            slope = jnp.where(odd_head, ALIBI_SLOPES[2 * j + 1], ALIBI_SLOPES[2 * j])
            bias_ref[later, j] = jnp.where(mask, -slope * distf, -1e30)


def _kv_pair_rows(kv_tile, parity, low):
    own = jnp.where(low if parity == 0 else jnp.logical_not(low), kv_tile, 0.0)
    other = pltpu.roll(own, HEAD_DIM, 1)
    lo, hi = (own, other) if parity == 0 else (other, own)
    return jnp.concatenate([lo, hi], axis=0).astype(BF16)


def _pair_softmax(s_t, sink_even, sink_odd):
    out = []
    for e, sink in enumerate((sink_even, sink_odd)):
        se = s_t[e * KEYS:(e + 1) * KEYS]
        m = jnp.maximum(jnp.max(se, axis=0, keepdims=True), sink)
        ee = jnp.exp(se - m)
        es = jnp.exp(sink - m)
        inv = 1.0 / (jnp.sum(ee, axis=0, keepdims=True) + es)
        out.append((ee * inv, es * inv))
    return out


def _attn_rows(n):
    q0 = pl.multiple_of(n * BLOCK, BLOCK)
    k0 = pl.multiple_of(jnp.maximum(n - 1, 0) * BLOCK, BLOCK)
    return q0, k0, jnp.minimum(n, 1)


def _attn_fwd(qkv, qg_pair, kg_pair, sinks, nseq, seq):
    t = qkv.shape[0]
    dq = N_Q_HEADS * HEAD_DIM
    dkv = N_KV_HEADS * HEAD_DIM

    def body(sk_ref, qkv_ref, qg_ref, kg_ref, o_ref, bias_ref):
        @pl.when(pl.program_id(0) == 0)
        def _():
            _fill_bias(bias_ref)

        low = lax.broadcasted_iota(jnp.int32, (1, LANES), 1) < HEAD_DIM
        qg = qg_ref[...] * QK_SCALE
        kg = kg_ref[...]

        def blk(n, carry):
            q0, k0, later = _attn_rows(n)
            for kt in range(dkv // LANES):
                kraw = qkv_ref[pl.ds(k0, KEYS), dq + kt * LANES:dq + (kt + 1) * LANES].astype(F32)
                vraw = qkv_ref[pl.ds(k0, KEYS), dq + dkv + kt * LANES:dq + dkv + (kt + 1) * LANES].astype(F32)
                kn, _, _ = _pair_norm(kraw, kg, low)
                for par in range(2):
                    kh = 2 * kt + par
                    k_pair = _kv_pair_rows(kn, par, low)
                    v_pair = _kv_pair_rows(vraw, par, low)
                    for jj in range(2):
                        j = 2 * kh + jj
                        qraw = qkv_ref[pl.ds(q0, BLOCK), j * LANES:(j + 1) * LANES].astype(F32)
                        qn, _, _ = _pair_norm(qraw, qg, low)
                        s_t = _dot(k_pair, qn.astype(BF16), NT) + bias_ref[later, j]
                        (p0, _), (p1, _) = _pair_softmax(s_t, sk_ref[0, 2 * j], sk_ref[0, 2 * j + 1])
                        p_t = jnp.concatenate([p0, p1], axis=0).astype(BF16)
                        o_ref[pl.ds(q0, BLOCK), j * LANES:(j + 1) * LANES] = _dot(p_t, v_pair, TN).astype(BF16)
            return carry

        lax.fori_loop(0, seq // BLOCK, blk, 0)

    return pl.pallas_call(
        body, name="attn_fwd", grid=(nseq,),
        in_specs=[pl.BlockSpec(memory_space=pltpu.SMEM),
                  pl.BlockSpec((seq, dq + 2 * dkv), lambda b: (b, 0)),
                  pl.BlockSpec((1, LANES), lambda b: (0, 0)),
                  pl.BlockSpec((1, LANES), lambda b: (0, 0))],
        out_specs=pl.BlockSpec((seq, dq), lambda b: (b, 0)),
        out_shape=_sds((t, dq), BF16),
        scratch_shapes=[pltpu.VMEM((2, N_PAIRS, 2 * KEYS, BLOCK), F32)],
        compiler_params=_params(("arbitrary",)))(sinks, qkv, qg_pair, kg_pair)


def _attn_bwd(do, qkv, qg_pair, kg_pair, sinks, nseq, seq):
    t = qkv.shape[0]
    dq = N_Q_HEADS * HEAD_DIM
    dkv = N_KV_HEADS * HEAD_DIM

    def body(sk_ref, do_ref, qkv_ref, qg_ref, kg_ref, o_ref, dqg_ref, dkg_ref, dsk_ref, acc_ref, bias_ref):
        @pl.when(pl.program_id(0) == 0)
        def _():
            _fill_bias(bias_ref)
            dqg_ref[...] = jnp.zeros_like(dqg_ref)
            dkg_ref[...] = jnp.zeros_like(dkg_ref)
            dsk_ref[...] = jnp.zeros_like(dsk_ref)

        acc_ref[...] = jnp.zeros_like(acc_ref)
        low = lax.broadcasted_iota(jnp.int32, (1, LANES), 1) < HEAD_DIM
        head_row = lax.broadcasted_iota(jnp.int32, (N_Q_HEADS, LANES), 0)
        qg = qg_ref[...] * QK_SCALE
        kg = kg_ref[...]

        def blk(n, carry):
            dqg_acc, dkg_acc, dsk_acc = carry
            q0, k0, later = _attn_rows(n)
            for kt in range(dkv // LANES):
                kraw = qkv_ref[pl.ds(k0, KEYS), dq + kt * LANES:dq + (kt + 1) * LANES].astype(F32)
                vraw = qkv_ref[pl.ds(k0, KEYS), dq + dkv + kt * LANES:dq + dkv + (kt + 1) * LANES].astype(F32)
                kn, khat, rk = _pair_norm(kraw, kg, low)
                dk_tile = None
                dv_tile = None
                for par in range(2):
                    kh = 2 * kt + par
                    own = low if par == 0 else jnp.logical_not(low)
                    k_pair = _kv_pair_rows(kn, par, low)
                    v_pair = _kv_pair_rows(vraw, par, low)
                    dkn_rows = jnp.zeros((2 * KEYS, LANES), F32)
                    dv_rows = jnp.zeros((2 * KEYS, LANES), F32)
                    for jj in range(2):
                        j = 2 * kh + jj
                        qraw = qkv_ref[pl.ds(q0, BLOCK), j * LANES:(j + 1) * LANES].astype(F32)
                        qn, qhat, rq = _pair_norm(qraw, qg, low)
                        qn_b = qn.astype(BF16)
                        do_b = do_ref[pl.ds(q0, BLOCK), j * LANES:(j + 1) * LANES]
                        s_t = _dot(k_pair, qn_b, NT) + bias_ref[later, j]
                        dp_t = _dot(v_pair, do_b, NT)
                        ds_halves = []
                        probs = _pair_softmax(s_t, sk_ref[0, 2 * j], sk_ref[0, 2 * j + 1])
                        for e, (p, ps) in enumerate(probs):
                            dp = dp_t[e * KEYS:(e + 1) * KEYS]
                            dsum = jnp.sum(p * dp, axis=0, keepdims=True)
                            ds_halves.append(p * (dp - dsum))
                            dsk_acc = dsk_acc - jnp.where(head_row == 2 * j + e, ps * dsum, 0.0)
                        p_t = jnp.concatenate([probs[0][0], probs[1][0]], axis=0).astype(BF16)
                        ds_t = jnp.concatenate(ds_halves, axis=0).astype(BF16)
                        dv_rows = dv_rows + _dot(p_t, do_b, NN)
                        dkn_rows = dkn_rows + _dot(ds_t, qn_b, NN)
                        dqn = _dot(ds_t, k_pair, TN)
                        dqg_acc = dqg_acc + jnp.sum(dqn * qhat, axis=0, keepdims=True)
                        dqhat = dqn * qg
                        prod = dqhat * qhat
                        m_lo = jnp.sum(jnp.where(low, prod, 0.0), axis=-1, keepdims=True)
                        m_hi = jnp.sum(jnp.where(low, 0.0, prod), axis=-1, keepdims=True)
                        mean = jnp.where(low, m_lo, m_hi) * (1.0 / HEAD_DIM)
                        o_ref[pl.ds(q0, BLOCK), j * LANES:(j + 1) * LANES] = (rq * (dqhat - qhat * mean)).astype(BF16)
                    dkn_acc = jnp.where(low, dkn_rows[0:KEYS], dkn_rows[KEYS:2 * KEYS])
                    dv_acc = jnp.where(low, dv_rows[0:KEYS], dv_rows[KEYS:2 * KEYS])
                    dkn = dkn_acc + pltpu.roll(dkn_acc, HEAD_DIM, 1)
                    dvh = dv_acc + pltpu.roll(dv_acc, HEAD_DIM, 1)
                    khat_own = jnp.where(own, khat, 0.0)
                    khat_dup = khat_own + pltpu.roll(khat_own, HEAD_DIM, 1)
                    dkg_acc = dkg_acc + jnp.sum(jnp.where(own, dkn * khat_dup, 0.0), axis=0, keepdims=True)
                    dkhat = dkn * kg
                    mean_k = jnp.sum(dkhat * khat_dup, axis=-1, keepdims=True) * (1.0 / LANES)
                    dk_raw = rk * (dkhat - khat_dup * mean_k)
                    dk_tile = jnp.where(own, dk_raw, 0.0) if dk_tile is None else jnp.where(own, dk_raw, dk_tile)
                    dv_tile = jnp.where(own, dvh, 0.0) if dv_tile is None else jnp.where(own, dvh, dv_tile)
                acc_ref[pl.ds(k0, KEYS), kt * LANES:(kt + 1) * LANES] += dk_tile
                acc_ref[pl.ds(k0, KEYS), dkv + kt * LANES:dkv + (kt + 1) * LANES] += dv_tile
            return dqg_acc, dkg_acc, dsk_acc

        zero = jnp.zeros((1, LANES), F32)
        carry = (zero, zero, jnp.zeros((N_Q_HEADS, LANES), F32))
        dqg_acc, dkg_acc, dsk_acc = lax.fori_loop(0, seq // BLOCK, blk, carry)
        dqg_ref[...] += dqg_acc * QK_SCALE
        dkg_ref[...] += dkg_acc
        dsk_ref[...] += dsk_acc
        o_ref[:, dq:dq + 2 * dkv] = acc_ref[...].astype(BF16)

    small = pl.BlockSpec((1, LANES), lambda b: (0, 0))
    heads = pl.BlockSpec((N_Q_HEADS, LANES), lambda b: (0, 0))
    return pl.pallas_call(
        body, name="attn_bwd", grid=(nseq,),
        in_specs=[pl.BlockSpec(memory_space=pltpu.SMEM),
                  pl.BlockSpec((seq, dq), lambda b: (b, 0)),
                  pl.BlockSpec((seq, dq + 2 * dkv), lambda b: (b, 0)),
                  small, small],
        out_specs=[pl.BlockSpec((seq, dq + 2 * dkv), lambda b: (b, 0)), small, small, heads],
        out_shape=[_sds((t, dq + 2 * dkv), BF16), _sds((1, LANES), F32), _sds((1, LANES), F32),
                   _sds((N_Q_HEADS, LANES), F32)],
        scratch_shapes=[pltpu.VMEM((seq, 2 * dkv), F32), pltpu.VMEM((2, N_PAIRS, 2 * KEYS, BLOCK), F32)],
        compiler_params=_params(("arbitrary",)))(sinks, do, qkv, qg_pair, kg_pair)


def _place():
    x, y, c = lax.axis_index("x"), lax.axis_index("y"), lax.axis_index("c")
    other_chips = [(1 - x, y), (x, 1 - y), (1 - x, 1 - y)]
    return x, y, c, other_chips


def _half_rows(c, rows):
    rh = rows // 2
    return pl.ds(pl.multiple_of(c * rh, BF16_ROWS), rh)


def _cast_own(name, w, place, layer=None):
    nl, r, cdim = w.shape
    first = 0
    if layer is not None:
        nl, first = 1, layer
    rt = _row_tile(r, 4 * cdim, ELEMENTWISE_BLOCK)

    def body(s_ref, w_ref, o_ref):
        o_ref[...] = w_ref[...].astype(BF16)

    grid_spec = pltpu.PrefetchScalarGridSpec(
        num_scalar_prefetch=1, grid=(nl, r // rt),
        in_specs=[pl.BlockSpec((None, rt, cdim), lambda l, i, s: (first + l, i, 0))],
        out_specs=pl.BlockSpec((None, None, rt, cdim), lambda l, i, s: (l, s[1], i, 0)))
    return pl.pallas_call(
        body, name=name, grid_spec=grid_spec, out_shape=_sds((nl, N_CHIPS, r, cdim), BF16),
        compiler_params=_params(("parallel", "parallel")))(place, w)


def _gather_protocol(outs, shapes, send_sems, recv_sems):
    n = len(outs)
    x, y, c, other_chips = _place()
    me_chip = 2 * x + y
    sibling = (x, y, 1 - c)

    def rows(u, chip, half):
        return outs[u].at[:, chip, _half_rows(half, shapes[u][2]), :]

    def copy(sem, part, to):
        return pltpu.make_async_remote_copy(src_ref=part, dst_ref=part, send_sem=send_sems.at[sem],
                                            recv_sem=recv_sems.at[sem], device_id=to, device_id_type=MESH)

    sends = []
    for u in range(n):
        for k, chip in enumerate(other_chips):
            cp = copy(6 * u + k, rows(u, me_chip, c), (*chip, c))
            cp.start()
            sends.append(cp)
    for u in range(n):
        for k, chip in enumerate(other_chips):
            got = rows(u, 2 * chip[0] + chip[1], c)
            copy(6 * u + k, got, (*chip, c)).wait_recv()
            cp = copy(6 * u + 3 + k, got, sibling)
            cp.start()
            sends.append(cp)
    for u in range(n):
        for k, chip in enumerate(other_chips):
            copy(6 * u + 3 + k, rows(u, 2 * chip[0] + chip[1], 1 - c), sibling).wait_recv()
    for cp in sends:
        cp.wait_send()


def _hbm_ref(a):
    return jax.new_ref(a, memory_space=pltpu.MemorySpace.HBM)


def _hbm_empty(shape, dtype):
    return jax.empty_ref(_sds(shape, dtype), memory_space=pltpu.MemorySpace.HBM)


def _sibling_peer():
    x, y, c, _ = _place()
    return [(x, y, 1 - c)]


def _chip_peers():
    x, y, c, other_chips = _place()
    return [(*chip, c) for chip in other_chips]


def _gather_peers():
    return _chip_peers() + _sibling_peer()


def _on_sequencer(name, collective_id, n_sems, peers, protocol):
    @pl.kernel(mesh=plsc.ScalarSubcoreMesh(axis_name="sequencer", num_cores=1), name=name,
               scratch_types=(pltpu.SemaphoreType.DMA((n_sems,)), pltpu.SemaphoreType.DMA((n_sems,))),
               compiler_params=pltpu.CompilerParams(collective_id=collective_id))
    def launch(send_sems, recv_sems):
        barrier = pltpu.get_barrier_semaphore()
        targets = peers()
        for peer in targets:
            pl.semaphore_signal(barrier, inc=1, device_id=peer, device_id_type=MESH)
        pl.semaphore_wait(barrier, len(targets))
        protocol(send_sems, recv_sems)

    launch()


def _seq_allgather(name, collective_id, bufs):
    shapes = [b.shape for b in bufs]
    refs = [_hbm_ref(b) for b in bufs]
    _on_sequencer(name, collective_id, 6 * len(bufs), _gather_peers,
                  lambda send_sems, recv_sems: _gather_protocol(refs, shapes, send_sems, recv_sems))
    return [r[...] for r in refs]


def _exchange_protocol(gs, outs, shapes, send_sems, recv_sems):
    x, y, c, _ = _place()
    sends = []
    for u in range(len(gs)):
        cp = pltpu.make_async_remote_copy(
            src_ref=gs[u].at[:, _half_rows(1 - c, shapes[u][1]), :], dst_ref=outs[u],
            send_sem=send_sems.at[u], recv_sem=recv_sems.at[u], device_id=(x, y, 1 - c), device_id_type=MESH)
        cp.start()
        sends.append(cp)
    for cp in sends:
        cp.wait_recv()
    for cp in sends:
        cp.wait_send()


def _seq_exchange(name, collective_id, grads):
    shapes = [g.shape for g in grads]
    gs = [_hbm_ref(g) for g in grads]
    outs = [_hbm_empty((s[0], s[1] // 2, s[2]), F32) for s in shapes]
    _on_sequencer(name, collective_id, len(grads), _sibling_peer,
                  lambda send_sems, recv_sems: _exchange_protocol(gs, outs, shapes, send_sems, recv_sems))
    return [o[...] for o in outs]


def _sum_halves(name, g, got, place, after):
    _, r, cdim = g.shape
    rh = r // 2
    rt = _row_tile(rh, 4 * cdim, ELEMENTWISE_BLOCK)
    nr = rh // rt

    def body(s_ref, g_ref, got_ref, after_ref, pb_ref, pf_ref):
        s = g_ref[...] + got_ref[...]
        pb_ref[...] = s.astype(BF16)

        @pl.when(pl.program_id(1) == s_ref[1])
        def _():
            pf_ref[...] = s

    grid_spec = pltpu.PrefetchScalarGridSpec(
        num_scalar_prefetch=1, grid=(nr, N_CHIPS),
        in_specs=[pl.BlockSpec((None, rt, cdim), lambda i, q, s: (q, s[0] * nr + i, 0)),
                  pl.BlockSpec((None, rt, cdim), lambda i, q, s: (q, i, 0)),
                  pl.BlockSpec(memory_space=pl.ANY)],
        out_specs=[pl.BlockSpec((None, rt, cdim), lambda i, q, s: (q, i, 0)),
                   pl.BlockSpec((rt, cdim), lambda i, q, s: (i, 0))])
    return pl.pallas_call(
        body, name=name, grid_spec=grid_spec,
        out_shape=[_sds((N_CHIPS, rh, cdim), BF16), _sds((rh, cdim), F32)],
        compiler_params=_params(("parallel", "arbitrary")))(place, g, got, after)


def _scatter_protocol(ps, outs, send_sems, recv_sems):
    x, y, c, other_chips = _place()
    sends = []
    for u in range(len(ps)):
        for k, chip in enumerate(other_chips):
            cp = pltpu.make_async_remote_copy(
                src_ref=ps[u].at[2 * chip[0] + chip[1]], dst_ref=outs[u].at[k],
                send_sem=send_sems.at[3 * u + k], recv_sem=recv_sems.at[3 * u + k],
                device_id=(*chip, c), device_id_type=MESH)
            cp.start()
            sends.append(cp)
    for cp in sends:
        cp.wait_recv()
    for cp in sends:
        cp.wait_send()


def _seq_scatter(name, collective_id, partials):
    ps = [_hbm_ref(p) for p in partials]
    outs = [_hbm_empty((3, p.shape[1], p.shape[2]), BF16) for p in partials]
    _on_sequencer(name, collective_id, 3 * len(partials), _chip_peers,
                  lambda send_sems, recv_sems: _scatter_protocol(ps, outs, send_sems, recv_sems))
    return [o[...] for o in outs]


def _sum_partials(name, own, got, place, layer, nl, prev, after):
    rh, cdim = own.shape
    rt = _row_tile(rh, 4 * cdim, ELEMENTWISE_BLOCK)
    nr = rh // rt

    def body(s_ref, own_ref, got_ref, *rest):
        o_ref = rest[-1]
        o_ref[...] = ((own_ref[...] + got_ref[0].astype(F32)) + got_ref[1].astype(F32)) + got_ref[2].astype(F32)

    in_specs = [pl.BlockSpec((rt, cdim), lambda i, s: (i, 0)), pl.BlockSpec((3, rt, cdim), lambda i, s: (0, i, 0)),
                pl.BlockSpec(memory_space=pl.ANY)]
    args = [place, own, got, after]
    aliases = {}
    if prev is not None:
        in_specs.append(pl.BlockSpec(memory_space=pl.ANY))
        args.append(prev)
        aliases = {4: 0}
    grid_spec = pltpu.PrefetchScalarGridSpec(
        num_scalar_prefetch=1, grid=(nr,), in_specs=in_specs,
        out_specs=pl.BlockSpec((None, rt, cdim), lambda i, s: (layer, s[0] * nr + i, 0)))
    return pl.pallas_call(
        body, name=name, grid_spec=grid_spec, out_shape=_sds((nl, 2 * rh, cdim), F32),
        input_output_aliases=aliases, compiler_params=_params(("parallel",)))(*args)


def _share_protocol(outs, shapes, units, send_sems, recv_sems):
    x, y, c, _ = _place()
    sends = []
    for u, (w, l) in enumerate(units):
        mine = outs[w].at[l, _half_rows(c, shapes[w][1]), :]
        cp = pltpu.make_async_remote_copy(src_ref=mine, dst_ref=mine, send_sem=send_sems.at[u],
                                          recv_sem=recv_sems.at[u], device_id=(x, y, 1 - c), device_id_type=MESH)
        cp.start()
        sends.append(cp)
    for u, (w, l) in enumerate(units):
        theirs = outs[w].at[l, _half_rows(1 - c, shapes[w][1]), :]
        pltpu.make_async_remote_copy(src_ref=theirs, dst_ref=theirs, send_sem=send_sems.at[u],
                                     recv_sem=recv_sems.at[u], device_id=(x, y, 1 - c),
                                     device_id_type=MESH).wait_recv()
    for cp in sends:
        cp.wait_send()


def _seq_share(name, collective_id, bufs):
    shapes = [b.shape for b in bufs]
    units = [(w, l) for w in range(len(bufs)) for l in range(shapes[w][0])]
    refs = [_hbm_ref(b) for b in bufs]
    _on_sequencer(name, collective_id, len(units), _sibling_peer,
                  lambda send_sems, recv_sems: _share_protocol(refs, shapes, units, send_sems, recv_sems))
    return [r[...] for r in refs]


def _gather_blocks(block_ref, all_ref, send_sems, recv_sems):
    x, y, c, _ = _place()
    me = 4 * x + 2 * y + c
    all_ref[me] = block_ref[...]
    sends = []
    for rel in range(1, 8):
        fx, fy, fc = (rel >> 2) & 1, (rel >> 1) & 1, rel & 1
        peer = (x ^ fx, y ^ fy, c ^ fc)
        cp = pltpu.make_async_remote_copy(src_ref=block_ref, dst_ref=all_ref.at[me], send_sem=send_sems.at[rel - 1],
                                          recv_sem=recv_sems.at[rel - 1], device_id=peer, device_id_type=MESH)
        cp.start()
        sends.append(cp)
    for cp in sends:
        cp.wait_recv()
    for cp in sends:
        cp.wait_send()


def _gather_conv_w(cw_block):
    r, d = cw_block.shape

    def body(b_ref, o_ref, all_ref, send_sems, recv_sems):
        _gather_blocks(b_ref, all_ref, send_sems, recv_sems)
        o_ref[...] = (all_ref[0] + all_ref[2]) + (all_ref[4] + all_ref[6])

    vm = pl.BlockSpec(memory_space=pltpu.VMEM)
    return pl.pallas_call(
        body, name="gather_conv_w", in_specs=[vm], out_specs=vm, out_shape=_sds((r, d), F32),
        scratch_shapes=[pltpu.VMEM((8, r, d), F32), pltpu.SemaphoreType.DMA((7,)), pltpu.SemaphoreType.DMA((7,))],
    )(cw_block)


def _adam(w, g, m, v):
    m_new = ADAM_B1 * m + (1.0 - ADAM_B1) * g
    v_new = ADAM_B2 * v + (1.0 - ADAM_B2) * (g * g)
    m_hat = m_new / (1.0 - ADAM_B1 ** ADAM_STEP)
    v_hat = v_new / (1.0 - ADAM_B2 ** ADAM_STEP)
    delta = -ADAM_LR * (m_hat / (jnp.sqrt(v_hat) + ADAM_EPS) + ADAM_WD * w)
    return delta, m_new, v_new


def _small_step(dnm0, dnm1, dnf0, dnf1, dcw, dqg, dkg, dsk, loss, w_blk, m_blk, v_blk):
    d = w_blk.shape[1]

    def body(dnm0_ref, dnm1_ref, dnf0_ref, dnf1_ref, dcw_ref, dqg_ref, dkg_ref, dsk_ref, loss_ref,
             w_ref, m_ref, v_ref, g_ref, dl_ref, mo_ref, vo_ref, blk_ref, all_ref, send_sems, recv_sems):
        blk_ref[...] = jnp.zeros_like(blk_ref)
        blk_ref[0:1, :] = jnp.sum(dnm0_ref[...], axis=0, keepdims=True)
        blk_ref[1:2, :] = jnp.sum(dnm1_ref[...], axis=0, keepdims=True)
        blk_ref[8:9, :] = jnp.sum(dnf0_ref[...], axis=0, keepdims=True)
        blk_ref[9:10, :] = jnp.sum(dnf1_ref[...], axis=0, keepdims=True)
        blk_ref[16:19, :] = dcw_ref[...]
        dqg_v = dqg_ref[...]
        dkg_v = dkg_ref[...]
        blk_ref[24:25, 0:LANES] = dqg_v + pltpu.roll(dqg_v, HEAD_DIM, 1)
        blk_ref[24:25, LANES:2 * LANES] = dkg_v + pltpu.roll(dkg_v, HEAD_DIM, 1)
        for h in range(N_Q_HEADS):
            blk_ref[24:25, 2 * LANES + h:2 * LANES + h + 1] = jnp.sum(dsk_ref[h:h + 1, :], axis=1, keepdims=True)
        blk_ref[24:25, 3 * LANES:4 * LANES] = jnp.broadcast_to(loss_ref[...], (1, LANES))
        _gather_blocks(blk_ref, all_ref, send_sems, recv_sems)
        g = all_ref[0]
        for dev in range(1, 8):
            g = g + all_ref[dev]
        g_ref[...] = g
        delta, m_new, v_new = _adam(w_ref[...], g, m_ref[...], v_ref[...])
        dl_ref[...] = delta
        mo_ref[...] = m_new
        vo_ref[...] = v_new

    vm = pl.BlockSpec(memory_space=pltpu.VMEM)
    blk = _sds((SMALL_ROWS, d), F32)
    return pl.pallas_call(
        body, name="small_step", in_specs=[vm] * 12, out_specs=[vm] * 4, out_shape=[blk] * 4,
        scratch_shapes=[pltpu.VMEM((SMALL_ROWS, d), F32), pltpu.VMEM((8, SMALL_ROWS, d), F32),
                        pltpu.SemaphoreType.DMA((7,)), pltpu.SemaphoreType.DMA((7,))],
    )(dnm0, dnm1, dnf0, dnf1, dcw, dqg, dkg, dsk, loss, w_blk, m_blk, v_blk)


def _adam_step(name, w, g, m, v):
    nl, r, cdim = w.shape
    rt = _row_tile(r, 4 * cdim, ELEMENTWISE_BLOCK)

    def body(w_ref, g_ref, m_ref, v_ref, d_ref, mo_ref, vo_ref):
        delta, m_new, v_new = _adam(w_ref[...], g_ref[...], m_ref[...], v_ref[...])
        d_ref[...] = delta
        mo_ref[...] = m_new
        vo_ref[...] = v_new

    spec = pl.BlockSpec((None, rt, cdim), lambda l, i: (l, i, 0))
    return pl.pallas_call(
        body, name=name, grid=(nl, r // rt), in_specs=[spec] * 4, out_specs=[spec] * 3,
        out_shape=[_sds(w.shape, F32)] * 3,
        compiler_params=_params(("parallel", "parallel")))(w, g, m, v)


def _pad_rows(a, rows=SUBLANES):
    return jnp.pad(a, ((0, rows - a.shape[0]), (0, 0)))


def _small_block(nm, nf, cw_local, qg, kg, sk, chip):
    d = nm.shape[1]
    cw_rows = lax.dynamic_update_slice(jnp.zeros((SUBLANES, d), F32), cw_local, (0, chip * cw_local.shape[1]))
    misc = jnp.concatenate([qg, qg, kg, kg, jnp.pad(sk, ((0, 0), (0, LANES - sk.shape[1]))),
                            jnp.zeros((1, d - 3 * LANES), F32)], axis=1)
    return jnp.concatenate([_pad_rows(nm), _pad_rows(nf), cw_rows, _pad_rows(misc)], axis=0)


def _unpack_small(blk, chip, cw_cols):
    cw = lax.dynamic_slice(blk[16:19], (0, chip * cw_cols), (3, cw_cols))[None]
    return dict(norm_mixer=blk[0:2], norm_ffn=blk[8:10], conv_w=cw, attn_q_gain=blk[24:25, 0:HEAD_DIM],
                attn_k_gain=blk[24:25, LANES:LANES + HEAD_DIM], attn_sinks=blk[24:25, 2 * LANES:2 * LANES + N_Q_HEADS])


WEIGHT_NAMES = ("conv_w_in", "conv_w", "conv_w_out", "attn_w_qkv", "attn_q_gain", "attn_k_gain", "attn_sinks",
                "attn_w_o", "norm_mixer", "norm_ffn", "ffn_w_gate_up", "ffn_w_down")
BIG = ("conv_w_in", "conv_w_out", "attn_w_qkv", "attn_w_o", "ffn_w_gate_up", "ffn_w_down")


def kernel(x, conv_w_in, conv_w, conv_w_out, attn_w_qkv, attn_q_gain, attn_k_gain, attn_sinks, attn_w_o, norm_mixer, norm_ffn, ffn_w_gate_up, ffn_w_down, loss_target, m_conv_w_in, m_conv_w, m_conv_w_out, m_attn_w_qkv, m_attn_q_gain, m_attn_k_gain, m_attn_sinks, m_attn_w_o, m_norm_mixer, m_norm_ffn, m_ffn_w_gate_up, m_ffn_w_down, v_conv_w_in, v_conv_w, v_conv_w_out, v_attn_w_qkv, v_attn_q_gain, v_attn_k_gain, v_attn_sinks, v_attn_w_o, v_norm_mixer, v_norm_ffn, v_ffn_w_gate_up, v_ffn_w_down):
    w = dict(conv_w_in=conv_w_in, conv_w=conv_w, conv_w_out=conv_w_out, attn_w_qkv=attn_w_qkv,
             attn_q_gain=attn_q_gain, attn_k_gain=attn_k_gain, attn_sinks=attn_sinks, attn_w_o=attn_w_o,
             norm_mixer=norm_mixer, norm_ffn=norm_ffn, ffn_w_gate_up=ffn_w_gate_up, ffn_w_down=ffn_w_down)
    m = dict(conv_w_in=m_conv_w_in, conv_w=m_conv_w, conv_w_out=m_conv_w_out, attn_w_qkv=m_attn_w_qkv,
             attn_q_gain=m_attn_q_gain, attn_k_gain=m_attn_k_gain, attn_sinks=m_attn_sinks, attn_w_o=m_attn_w_o,
             norm_mixer=m_norm_mixer, norm_ffn=m_norm_ffn, ffn_w_gate_up=m_ffn_w_gate_up, ffn_w_down=m_ffn_w_down)
    v = dict(conv_w_in=v_conv_w_in, conv_w=v_conv_w, conv_w_out=v_conv_w_out, attn_w_qkv=v_attn_w_qkv,
             attn_q_gain=v_attn_q_gain, attn_k_gain=v_attn_k_gain, attn_sinks=v_attn_sinks, attn_w_o=v_attn_w_o,
             norm_mixer=v_norm_mixer, norm_ffn=v_norm_ffn, ffn_w_gate_up=v_ffn_w_gate_up, ffn_w_down=v_ffn_w_down)

    nseq, seq, d = x.shape
    t = nseq * seq
    chip = 2 * lax.axis_index("x") + lax.axis_index("y")
    core = lax.axis_index("c")
    place = jnp.stack([core, chip]).astype(jnp.int32)
    x0 = x.reshape(t, d)
    tgt = loss_target.reshape(t, d)

    cw_block = lax.dynamic_update_slice(jnp.zeros((SUBLANES, d), F32), conv_w[0], (0, chip * conv_w.shape[2]))
    cw_full = _gather_conv_w(cw_block)[0:3]
    def cast(k, layer=None):
        return _cast_own(f"cast_{k}" + ("" if layer is None else str(layer)), w[k], place, layer)

    w_in, w_out = _seq_allgather("allgather_conv", 1, [cast("conv_w_in"), cast("conv_w_out")])
    w_gu0, w_dn0 = _seq_allgather("allgather_ffn0", 2, [cast("ffn_w_gate_up", 0), cast("ffn_w_down", 0)])
    w_qkv, w_o, w_gu1, w_dn1 = _seq_allgather(
        "allgather_rest", 3, [cast("attn_w_qkv"), cast("attn_w_o"), cast("ffn_w_gate_up", 1), cast("ffn_w_down", 1)])
    w_out = w_out.reshape(1, d, d)
    w_o = w_o.reshape(1, d, d)
    w_gu = [w_gu0, w_gu1]
    w_dn = [w_dn0.reshape(1, D_FF, d), w_dn1.reshape(1, D_FF, d)]

    qg_pair = jnp.concatenate([attn_q_gain, attn_q_gain], axis=1)
    kg_pair = jnp.concatenate([attn_k_gain, attn_k_gain], axis=1)

    def ffn_bwd(i, dxo, xin, h, g, u, a):
        g_dn = _wgrad_down(f"ffn{i}_down_wgrad", a, dxo, D_FF // 2)
        dg, du = _mm_down_t_swiglu(f"ffn{i}_down_dgrad", dxo, w_dn[i], 0, g, u)
        g_gu = _wgrad_up2(f"ffn{i}_up_wgrad", h, dg, du)
        dxi, dgain = _dgrad_norm_ffn(f"ffn{i}_up_dgrad", dg, du, w_gu[i], 0, xin, norm_ffn[i:i + 1], dxo)
        return dxi, dgain, g_gu, g_dn

    h0 = _rms_fwd("conv_norm", x0, norm_mixer[0:1])
    bcx = _mm_up_joined("conv_in", h0, w_in, 512)
    z = _conv_fwd(bcx, cw_full, nseq, seq)
    x1, h1 = _mm_down_norm("conv_out", z, w_out, 0, x0, norm_ffn[0:1])
    g0, u0, a0 = _mm_up_swiglu("ffn0_up", h1, w_gu[0], 0)
    x2, h2 = _mm_down_norm("ffn0_down", a0, w_dn[0], 0, x1, norm_mixer[1:2])
    qkv = _mm_up_joined("attn_qkv", h2, w_qkv, 1024)
    o = _attn_fwd(qkv, qg_pair, kg_pair, attn_sinks, nseq, seq)
    x3, h3 = _mm_down_norm("attn_out", o, w_o, 0, x2, norm_ffn[1:2])
    g1, u1, a1 = _mm_up_swiglu("ffn1_up", h3, w_gu[1], 0)
    dy, loss_part = _mm_down_loss("ffn1_down", a1, w_dn[1], 0, x3, tgt)

    finished = {k: None for k in BIG}

    def exchange(tag, cid, units):
        return units, _seq_exchange(f"exchange_{tag}", cid, [g for _, _, g in units])

    def scatter(tag, cid, group, after):
        units, got = group
        sums = [_sum_halves(f"sum_halves_{k}{l}", g, r, place, after) for (k, l, g), r in zip(units, got)]
        return units, sums, _seq_scatter(f"scatter_{tag}", cid, [pb for pb, _ in sums])

    def finish(group, after):
        units, sums, arrived = group
        for (k, l, _), (_, pf), r in zip(units, sums, arrived):
            finished[k] = _sum_partials(f"sum_partials_{k}{l}", pf, r, place, l, w[k].shape[0], finished[k], after)

    dx3, dnf1, g_gu1, g_dn1 = ffn_bwd(1, dy, x3, h3, g1, u1, a1)
    ffn1 = exchange("ffn1", 4, [("ffn_w_down", 1, g_dn1), ("ffn_w_gate_up", 1, g_gu1)])
    g_o = _wgrad_down("attn_out_wgrad", o, dx3, d)
    do = _mm_down_t("attn_out_dgrad", dx3, w_o, 0)
    ffn1 = scatter("ffn1", 8, ffn1, do)
    dqkv, dqg, dkg, dsk = _attn_bwd(do, qkv, qg_pair, kg_pair, attn_sinks, nseq, seq)
    g_qkv = _wgrad_joined("attn_qkv_wgrad", h2, dqkv)
    attn = exchange("attn", 5, [("attn_w_o", 0, g_o), ("attn_w_qkv", 0, g_qkv)])
    dx2, dnm1 = _dgrad_norm_qkv("attn_qkv_dgrad", dqkv, w_qkv, x2, norm_mixer[1:2], dx3)
    finish(ffn1, dx2)
    attn = scatter("attn", 9, attn, dx2)
    dx1, dnf0, g_gu0, g_dn0 = ffn_bwd(0, dx2, x1, h1, g0, u0, a0)
    ffn0 = exchange("ffn0", 6, [("ffn_w_down", 0, g_dn0), ("ffn_w_gate_up", 0, g_gu0)])
    g_out = _wgrad_down("conv_out_wgrad", z, dx1, d)
    dz = _mm_down_t("conv_out_dgrad", dx1, w_out, 0)
    finish(attn, dz)
    ffn0 = scatter("ffn0", 10, ffn0, dz)
    dbcx, dcw = _conv_bwd(dz, bcx, cw_full, nseq, seq)
    g_in = _wgrad_conv_in("conv_in_wgrad", h0, dbcx, conv_w_in.shape[2])
    conv = exchange("conv", 7, [("conv_w_out", 0, g_out), ("conv_w_in", 0, g_in)])
    dx0, dnm0 = _dgrad_norm_conv("conv_in_dgrad", dbcx, w_in, x0, norm_mixer[0:1], dx1)
    finish(ffn0, dx0)
    late = ("attn_w_qkv", "attn_w_o", "ffn_w_gate_up", "ffn_w_down")
    grads_late = _seq_share("share_late", 12, [finished[k] for k in late])
    conv = scatter("conv", 11, conv, dx0)

    grad, delta, new_m, new_v = {}, {}, {}, {}

    def adam(k, g):
        grad[k] = g
        delta[k], new_m[k], new_v[k] = _adam_step(f"adam_{k}", w[k], g, m[k], v[k])

    for k, g in zip(late, grads_late):
        adam(k, g)
    finish(conv, delta["ffn_w_gate_up"])
    last = ("conv_w_in", "conv_w_out")
    for k, g in zip(last, _seq_share("share_last", 13, [finished[k] for k in last])):
        adam(k, g)

    def blocks(src):
        return _small_block(src["norm_mixer"], src["norm_ffn"], src["conv_w"][0], src["attn_q_gain"],
                            src["attn_k_gain"], src["attn_sinks"], chip)

    g_blk, d_blk, m_blk, v_blk = _small_step(dnm0, dnm1, dnf0, dnf1, dcw, dqg, dkg, dsk, loss_part,
                                             blocks(w), blocks(m), blocks(v))
    cw_cols = conv_w.shape[2]
    for dst, blk in ((grad, g_blk), (delta, d_blk), (new_m, m_blk), (new_v, v_blk)):
        dst.update(_unpack_small(blk, chip, cw_cols))
    loss = g_blk[24, 3 * LANES]

    return (loss, dx0.reshape(nseq, seq, d), *[grad[k] for k in WEIGHT_NAMES], *[delta[k] for k in WEIGHT_NAMES],
            *[new_m[k] for k in WEIGHT_NAMES], *[new_v[k] for k in WEIGHT_NAMES])
```

```python
import jax
import jax.numpy as jnp
from jax import lax
from jax.experimental import pallas as pl
from jax.experimental.pallas import tpu as pltpu
from jax.experimental.pallas import tpu_sc as plsc

F32 = jnp.float32
BF16 = jnp.bfloat16

D_MODEL = 1024
D_FF = 2816
N_Q_HEADS = 16
N_KV_HEADS = 4
HEAD_DIM = 64
WINDOW = 128
BLOCK = 128
EPS = 1e-6
N_CHIPS = 4
LANES = 128
SUBLANES = 8
BF16_ROWS = 16
MXU_COLS = 256
VMEM_LIMIT = 48 * 1024 * 1024
ADAM_LR, ADAM_B1, ADAM_B2, ADAM_EPS, ADAM_WD, ADAM_STEP = 0.001, 0.9, 0.999, 1e-08, 0.01, 10
ALIBI_SLOPES = tuple(2.0 ** (-8.0 * (h + 1) / N_Q_HEADS) for h in range(N_Q_HEADS))
SMALL_ROWS = 32
MESH = pl.DeviceIdType.MESH

NN = ((1,), (0,))
NT = ((1,), (1,))
TN = ((0,), (0,))


def _dot(a, b, dims):
    return lax.dot_general(a, b, (dims, ((), ())), preferred_element_type=F32)


def _pick(n, cands):
    for c in cands:
        if n % c == 0:
            return c
    raise ValueError((n, cands))


def _row_tile(rows, row_bytes, cap_bytes):
    fits = [r for r in range(BF16_ROWS, rows + 1, BF16_ROWS) if rows % r == 0 and r * row_bytes <= cap_bytes]
    if not fits:
        raise ValueError((rows, row_bytes, cap_bytes))
    return fits[-1]


ELEMENTWISE_BLOCK = 3 << 19


def _resident(block_shape, index_map):
    return pl.BlockSpec(block_shape, index_map, pipeline_mode=pl.Buffered(1))


def _params(sem):
    return pltpu.CompilerParams(dimension_semantics=sem, vmem_limit_bytes=VMEM_LIMIT)


def _sds(shape, dtype):
    return jax.ShapeDtypeStruct(shape, dtype)


def _rms(xv):
    return lax.rsqrt(jnp.mean(xv * xv, axis=-1, keepdims=True) + EPS)


def _sigmoid(g):
    return 1.0 / (1.0 + jnp.exp(-g))


def _mm_up_joined(name, a, w4, tm_pref):
    t, k = a.shape
    _, _, _, nq = w4.shape
    tm = _pick(t, (tm_pref, 256, 128))

    def body(a_ref, w_ref, o_ref, wcat_ref):
        @pl.when(pl.program_id(0) == 0)
        def _():
            for q in range(N_CHIPS):
                wcat_ref[:, q * nq:(q + 1) * nq] = w_ref[q]

        o_ref[...] = _dot(a_ref[...], wcat_ref[...], NN).astype(BF16)

    return pl.pallas_call(
        body, name=name, grid=(t // tm,),
        in_specs=[pl.BlockSpec((tm, k), lambda i: (i, 0)),
                  pl.BlockSpec((None, N_CHIPS, k, nq), lambda i: (0, 0, 0, 0))],
        out_specs=pl.BlockSpec((tm, N_CHIPS * nq), lambda i: (i, 0)),
        out_shape=_sds((t, N_CHIPS * nq), BF16),
        scratch_shapes=[pltpu.VMEM((k, N_CHIPS * nq), BF16)],
        compiler_params=_params(("arbitrary",)))(a, w4)


def _mm_up_swiglu(name, h, w4, layer):
    t, k = h.shape
    _, _, _, nq = w4.shape
    tm = _pick(t, (512, 256, 128))

    def body(h_ref, wg_ref, wu_ref, dag_ref, dau_ref, a_ref):
        hv = h_ref[...]
        g = _dot(hv, wg_ref[...], NN)
        u = _dot(hv, wu_ref[...], NN)
        sg = _sigmoid(g)
        silu = g * sg
        dag_ref[...] = (u * (sg * (1.0 + g * (1.0 - sg)))).astype(BF16)
        dau_ref[...] = silu.astype(BF16)
        a_ref[...] = (silu * u).astype(BF16)

    half = N_CHIPS // 2
    out = pl.BlockSpec((tm, nq), lambda j, i: (i, j))
    return pl.pallas_call(
        body, name=name, grid=(half, t // tm),
        in_specs=[pl.BlockSpec((tm, k), lambda j, i: (i, 0)),
                  pl.BlockSpec((None, None, k, nq), lambda j, i: (layer, j, 0, 0)),
                  pl.BlockSpec((None, None, k, nq), lambda j, i: (layer, half + j, 0, 0))],
        out_specs=[out, out, out],
        out_shape=[_sds((t, half * nq), BF16)] * 3,
        compiler_params=_params(("parallel", "parallel")))(h, w4, w4)


def _mm_down_norm(name, a, w, layer, res, gain):
    t, kf = a.shape
    _, _, n = w.shape
    tm = _pick(t, (1024, 512, 256, 128))

    def body(a_ref, w_ref, r_ref, g_ref, o_ref, h_ref):
        xo = r_ref[...] + _dot(a_ref[...], w_ref[...], NN)
        o_ref[...] = xo
        h_ref[...] = ((xo * _rms(xo)) * g_ref[...]).astype(BF16)

    row = pl.BlockSpec((tm, n), lambda i: (i, 0))
    return pl.pallas_call(
        body, name=name, grid=(t // tm,),
        in_specs=[pl.BlockSpec((tm, kf), lambda i: (i, 0)),
                  _resident((None, kf, n), lambda i: (layer, 0, 0)),
                  row, pl.BlockSpec((1, n), lambda i: (0, 0))],
        out_specs=[row, row],
        out_shape=[_sds((t, n), F32), _sds((t, n), BF16)],
        compiler_params=_params(("parallel",)))(a, w, res, gain)


def _mm_down_loss(name, a, w, layer, res, tgt):
    t, kf = a.shape
    _, _, n = w.shape
    tm = _pick(t, (1024, 512, 256, 128))
    steps = t // tm

    def body(a_ref, w_ref, r_ref, t_ref, dy_ref, l_ref, acc_ref):
        i = pl.program_id(0)

        @pl.when(i == 0)
        def _():
            acc_ref[...] = jnp.zeros_like(acc_ref)

        e = (r_ref[...] + _dot(a_ref[...], w_ref[...], NN)) - t_ref[...]
        dy_ref[...] = e * (1.0 / n)
        acc_ref[...] += (e * e).reshape(tm // SUBLANES, SUBLANES, n).sum(axis=0)

        @pl.when(i == steps - 1)
        def _():
            l_ref[...] = jnp.sum(acc_ref[...], keepdims=True) * (0.5 / n)

    row = pl.BlockSpec((tm, n), lambda i: (i, 0))
    return pl.pallas_call(
        body, name=name, grid=(steps,),
        in_specs=[pl.BlockSpec((tm, kf), lambda i: (i, 0)),
                  _resident((None, kf, n), lambda i: (layer, 0, 0)), row, row],
        out_specs=[row, pl.BlockSpec((1, 1), lambda i: (0, 0))],
        out_shape=[_sds((t, n), F32), _sds((1, 1), F32)],
        scratch_shapes=[pltpu.VMEM((SUBLANES, n), F32)],
        compiler_params=_params(("arbitrary",)))(a, w, res, tgt)


def _mm_down_t(name, dx, w, layer):
    t, n = dx.shape
    _, kf, _ = w.shape
    tm = _pick(t, (512, 256, 128))

    def body(a_ref, w_ref, o_ref):
        o_ref[...] = _dot(a_ref[...].astype(BF16), w_ref[...], NT).astype(BF16)

    return pl.pallas_call(
        body, name=name, grid=(t // tm,),
        in_specs=[pl.BlockSpec((tm, n), lambda i: (i, 0)),
                  pl.BlockSpec((None, kf, n), lambda i: (layer, 0, 0))],
        out_specs=pl.BlockSpec((tm, kf), lambda i: (i, 0)),
        out_shape=_sds((t, kf), BF16),
        compiler_params=_params(("parallel",)))(dx, w)


def _mm_down_t_swiglu(name, dx, w, layer, g, u):
    t, n = dx.shape
    f = g.shape[1]
    tm = _pick(t, (512, 256, 128))

    def body(a_ref, w_ref, dag_ref, dau_ref, dg_ref, du_ref):
        da = _dot(a_ref[...].astype(BF16), w_ref[...], NT)
        dg_ref[...] = (da * dag_ref[...].astype(F32)).astype(BF16)
        du_ref[...] = (da * dau_ref[...].astype(F32)).astype(BF16)

    tile = pl.BlockSpec((tm, f), lambda i: (i, 0))
    return pl.pallas_call(
        body, name=name, grid=(t // tm,),
        in_specs=[pl.BlockSpec((tm, n), lambda i: (i, 0)),
                  _resident((None, f, n), lambda i: (layer, 0, 0)), tile, tile],
        out_specs=[tile, tile],
        out_shape=[_sds((t, f), BF16)] * 2,
        compiler_params=_params(("parallel",)))(dx, w, g, u)


def _dgrad_norm(name, acts, act_blocks, pieces, w4, layer, x, gain, dres):
    t, d = x.shape
    _, _, k, nq = w4.shape
    tm = _pick(t, (512, 256, 128))
    n_act = len(acts)

    def body(*refs):
        act_refs = refs[:n_act]
        w_ref, x_ref, g_ref, dr_ref, dx_ref, dg_ref = refs[n_act:]

        @pl.when(pl.program_id(0) == 0)
        def _():
            dg_ref[...] = jnp.zeros_like(dg_ref)

        dh = None
        for a_tile, w_tile in pieces(act_refs, w_ref):
            term = _dot(a_tile, w_tile, NT)
            dh = term if dh is None else dh + term
        xv = x_ref[...]
        r = _rms(xv)
        xhat = xv * r
        gd = dh * g_ref[...]
        dx_ref[...] = dr_ref[...] + r * (gd - xhat * jnp.mean(gd * xhat, axis=-1, keepdims=True))
        dg_ref[...] += (dh * xhat).reshape(tm // SUBLANES, SUBLANES, d).sum(axis=0)

    row = pl.BlockSpec((tm, d), lambda i: (i, 0))
    return pl.pallas_call(
        body, name=name, grid=(t // tm,),
        in_specs=[*act_blocks(tm),
                  _resident((None, N_CHIPS, k, nq), lambda i: (layer, 0, 0, 0)),
                  row, pl.BlockSpec((1, d), lambda i: (0, 0)), row],
        out_specs=[row, pl.BlockSpec((SUBLANES, d), lambda i: (0, 0))],
        out_shape=[_sds((t, d), F32), _sds((SUBLANES, d), F32)],
        compiler_params=_params(("arbitrary",)))(*acts, w4, x, gain, dres)


def _dgrad_norm_ffn(name, dg, du, w4, layer, x, gain, dres):
    nq = w4.shape[3]
    f = dg.shape[1]

    def blocks(tm):
        return [pl.BlockSpec((tm, f), lambda i: (i, 0))] * 2

    def pieces(act_refs, w_ref):
        dg_ref, du_ref = act_refs
        return [(dg_ref[:, 0:nq], w_ref[0]), (dg_ref[:, nq:2 * nq], w_ref[1]),
                (du_ref[:, 0:nq], w_ref[2]), (du_ref[:, nq:2 * nq], w_ref[3])]

    return _dgrad_norm(name, [dg, du], blocks, pieces, w4, layer, x, gain, dres)


def _dgrad_norm_qkv(name, dqkv, w4, x, gain, dres):
    nq = w4.shape[3]

    def blocks(tm):
        return [pl.BlockSpec((tm, N_CHIPS * nq), lambda i: (i, 0))]

    def pieces(act_refs, w_ref):
        return [(act_refs[0][:, q * nq:(q + 1) * nq], w_ref[q]) for q in range(N_CHIPS)]

    return _dgrad_norm(name, [dqkv], blocks, pieces, w4, 0, x, gain, dres)


def _dgrad_norm_conv(name, d3, w4, x, gain, dres):
    _, _, d = d3.shape
    nq = w4.shape[3]
    per_part, per_q = d // MXU_COLS, nq // MXU_COLS

    def blocks(tm):
        return [pl.BlockSpec((3, tm, d), lambda i: (0, i, 0))]

    def pieces(act_refs, w_ref):
        out = []
        for jb in range(3 * per_part):
            ca, cw = (jb % per_part) * MXU_COLS, (jb % per_q) * MXU_COLS
            out.append((act_refs[0][jb // per_part, :, ca:ca + MXU_COLS], w_ref[jb // per_q, :, cw:cw + MXU_COLS]))
        return out

    return _dgrad_norm(name, [d3], blocks, pieces, w4, 0, x, gain, dres)


def _wgrad_up2(name, h, dg, du):
    t, k = h.shape
    nq = dg.shape[1] // 2
    tk = _pick(t, (1024, 512, 256, 128))
    steps = t // tk
    half = N_CHIPS // 2

    def body(h_ref, dg_ref, du_ref, o_ref):
        q = pl.program_id(0)

        @pl.when(pl.program_id(1) == 0)
        def _():
            o_ref[...] = jnp.zeros_like(o_ref)

        @pl.when(q < half)
        def _():
            o_ref[...] += _dot(h_ref[...], dg_ref[...], TN)

        @pl.when(q >= half)
        def _():
            o_ref[...] += _dot(h_ref[...], du_ref[...], TN)

    return pl.pallas_call(
        body, name=name, grid=(N_CHIPS, steps),
        in_specs=[pl.BlockSpec((tk, k), lambda q, s: (s, 0)),
                  pl.BlockSpec((tk, nq), lambda q, s: (jnp.where(q < half, s, steps - 1), jnp.minimum(q, half - 1))),
                  pl.BlockSpec((tk, nq), lambda q, s: (jnp.where(q >= half, s, 0), jnp.maximum(q - half, 0)))],
        out_specs=pl.BlockSpec((None, k, nq), lambda q, s: (q, 0, 0)),
        out_shape=_sds((N_CHIPS, k, nq), F32),
        compiler_params=_params(("parallel", "arbitrary")))(h, dg, du)


def _wgrad_joined(name, h, dy):
    t, k = h.shape
    nq = dy.shape[1] // N_CHIPS
    tk = _pick(t, (1024, 512, 256, 128))

    def body(h_ref, dy_ref, o_ref):
        @pl.when(pl.program_id(0) == 0)
        def _():
            o_ref[...] = jnp.zeros_like(o_ref)

        res = _dot(h_ref[...], dy_ref[...], TN)
        for q in range(N_CHIPS):
            o_ref[q] += res[:, q * nq:(q + 1) * nq]

    return pl.pallas_call(
        body, name=name, grid=(t // tk,),
        in_specs=[pl.BlockSpec((tk, k), lambda s: (s, 0)), pl.BlockSpec((tk, N_CHIPS * nq), lambda s: (s, 0))],
        out_specs=pl.BlockSpec((N_CHIPS, k, nq), lambda s: (0, 0, 0)),
        out_shape=_sds((N_CHIPS, k, nq), F32),
        compiler_params=_params(("arbitrary",)))(h, dy)


def _wgrad_conv_in(name, h, d3, nq):
    t, k = h.shape
    d = d3.shape[2]
    per_part, per_q = d // MXU_COLS, nq // MXU_COLS
    tk = _pick(t, (512, 256, 128))

    def body(h_ref, d_ref, o_ref):
        @pl.when(pl.program_id(0) == 0)
        def _():
            o_ref[...] = jnp.zeros_like(o_ref)

        hv = h_ref[...]
        for part in range(3):
            res = _dot(hv, d_ref[part], TN)
            for cc in range(per_part):
                jb = part * per_part + cc
                co = (jb % per_q) * MXU_COLS
                o_ref[jb // per_q, :, co:co + MXU_COLS] += res[:, cc * MXU_COLS:(cc + 1) * MXU_COLS]

    return pl.pallas_call(
        body, name=name, grid=(t // tk,),
        in_specs=[pl.BlockSpec((tk, k), lambda s: (s, 0)), pl.BlockSpec((3, tk, d), lambda s: (0, s, 0))],
        out_specs=pl.BlockSpec((N_CHIPS, k, nq), lambda s: (0, 0, 0)),
        out_shape=_sds((N_CHIPS, k, nq), F32),
        compiler_params=_params(("arbitrary",)))(h, d3)


def _wgrad_down(name, a, dx, tmw):
    t, kf = a.shape
    n = dx.shape[1]
    tk = _pick(t, (1024, 512, 256, 128))

    def body(a_ref, b_ref, o_ref):
        @pl.when(pl.program_id(1) == 0)
        def _():
            o_ref[...] = jnp.zeros_like(o_ref)

        o_ref[...] += _dot(a_ref[...], b_ref[...].astype(BF16), TN)

    g = pl.pallas_call(
        body, name=name, grid=(kf // tmw, t // tk),
        in_specs=[pl.BlockSpec((tk, tmw), lambda j, s: (s, j)), pl.BlockSpec((tk, n), lambda j, s: (s, 0))],
        out_specs=pl.BlockSpec((tmw, n), lambda j, s: (j, 0)),
        out_shape=_sds((kf, n), F32),
        compiler_params=_params(("parallel", "arbitrary")))(a, dx)
    return g.reshape(N_CHIPS, kf // N_CHIPS, n)


def _rms_fwd(name, x, gain):
    t, d = x.shape
    tm = _pick(t, (512, 256, 128))

    def body(x_ref, g_ref, h_ref):
        xv = x_ref[...]
        h_ref[...] = ((xv * _rms(xv)) * g_ref[...]).astype(BF16)

    return pl.pallas_call(
        body, name=name, grid=(t // tm,),
        in_specs=[pl.BlockSpec((tm, d), lambda i: (i, 0)), pl.BlockSpec((1, d), lambda i: (0, 0))],
        out_specs=pl.BlockSpec((tm, d), lambda i: (i, 0)),
        out_shape=_sds((t, d), BF16),
        compiler_params=_params(("parallel",)))(x, gain)


def _shift_rows(u, k, rows):
    s = u.shape[0]
    if k > 0:
        return jnp.where(rows >= k, pltpu.roll(u, k, 0), 0.0)
    return jnp.where(rows < s + k, pltpu.roll(u, s + k, 0), 0.0)


def _conv_fwd(bcx, cw, nseq, seq):
    t, d3 = bcx.shape
    d = d3 // 3
    cb = MXU_COLS
    nj = d // cb

    def body(b_ref, c_ref, x_ref, cw_ref, z_ref):
        u = b_ref[...].astype(F32) * x_ref[...].astype(F32)
        rows = lax.broadcasted_iota(jnp.int32, u.shape, 0)
        cwv = cw_ref[...]
        y = cwv[2:3] * u + cwv[1:2] * _shift_rows(u, 1, rows) + cwv[0:1] * _shift_rows(u, 2, rows)
        z_ref[...] = (c_ref[...].astype(F32) * y).astype(BF16)

    return pl.pallas_call(
        body, name="conv_fwd", grid=(nseq, nj),
        in_specs=[pl.BlockSpec((seq, cb), lambda b, j: (b, j)),
                  pl.BlockSpec((seq, cb), lambda b, j: (b, nj + j)),
                  pl.BlockSpec((seq, cb), lambda b, j: (b, 2 * nj + j)),
                  pl.BlockSpec((3, cb), lambda b, j: (0, j))],
        out_specs=pl.BlockSpec((seq, cb), lambda b, j: (b, j)),
        out_shape=_sds((t, d), BF16),
        compiler_params=_params(("parallel", "parallel")))(bcx, bcx, bcx, cw)


def _conv_bwd(dz, bcx, cw, nseq, seq):
    t, d3 = bcx.shape
    d = d3 // 3
    cb = MXU_COLS
    nj = d // cb

    def body(dz_ref, b_ref, c_ref, x_ref, cw_ref, o_ref, dcw_ref):
        @pl.when(pl.program_id(1) == 0)
        def _():
            dcw_ref[...] = jnp.zeros_like(dcw_ref)

        b = b_ref[...].astype(F32)
        c = c_ref[...].astype(F32)
        xv = x_ref[...].astype(F32)
        dzv = dz_ref[...].astype(F32)
        u = b * xv
        rows = lax.broadcasted_iota(jnp.int32, u.shape, 0)
        u1 = _shift_rows(u, 1, rows)
        u2 = _shift_rows(u, 2, rows)
        cwv = cw_ref[...]
        y = cwv[2:3] * u + cwv[1:2] * u1 + cwv[0:1] * u2
        dyc = dzv * c
        du = cwv[2:3] * dyc + cwv[1:2] * _shift_rows(dyc, -1, rows) + cwv[0:1] * _shift_rows(dyc, -2, rows)
        o_ref[0] = (du * xv).astype(BF16)
        o_ref[1] = (dzv * y).astype(BF16)
        o_ref[2] = (du * b).astype(BF16)
        s0 = jnp.sum(dyc * u2, axis=0, keepdims=True)
        s1 = jnp.sum(dyc * u1, axis=0, keepdims=True)
        s2 = jnp.sum(dyc * u, axis=0, keepdims=True)
        tap = lax.broadcasted_iota(jnp.int32, (3, cb), 0)
        dcw_ref[...] += jnp.where(tap == 0, s0, jnp.where(tap == 1, s1, s2))

    return pl.pallas_call(
        body, name="conv_bwd", grid=(nj, nseq),
        in_specs=[pl.BlockSpec((seq, cb), lambda j, b: (b, j)),
                  pl.BlockSpec((seq, cb), lambda j, b: (b, j)),
                  pl.BlockSpec((seq, cb), lambda j, b: (b, nj + j)),
                  pl.BlockSpec((seq, cb), lambda j, b: (b, 2 * nj + j)),
                  pl.BlockSpec((3, cb), lambda j, b: (0, j))],
        out_specs=[pl.BlockSpec((3, seq, cb), lambda j, b: (0, b, j)),
                   pl.BlockSpec((3, cb), lambda j, b: (0, j))],
        out_shape=[_sds((3, t, d), BF16), _sds((3, d), F32)],
        compiler_params=_params(("parallel", "arbitrary")))(dz, bcx, bcx, bcx, cw)


def _pair_norm(x, gain_pair, low):
    sq = x * x
    ss_lo = jnp.sum(jnp.where(low, sq, 0.0), axis=-1, keepdims=True)
    ss_hi = jnp.sum(jnp.where(low, 0.0, sq), axis=-1, keepdims=True)
    r = lax.rsqrt(jnp.where(low, ss_lo, ss_hi) * (1.0 / HEAD_DIM) + EPS)
    xhat = x * r
    return xhat * gain_pair, xhat, r


KEYS = 2 * BLOCK
QK_SCALE = 1.0 / (HEAD_DIM ** 0.5)
N_PAIRS = N_Q_HEADS // 2


def _fill_bias(bias_ref):
    rows = lax.broadcasted_iota(jnp.int32, (2 * KEYS, BLOCK), 0)
    qi = lax.broadcasted_iota(jnp.int32, (2 * KEYS, BLOCK), 1)
    odd_head = rows >= KEYS
    kj = jnp.where(odd_head, rows - KEYS, rows)
    for later in range(2):
        dist = later * BLOCK + qi - kj
        mask = jnp.logical_and(dist >= 0, dist < WINDOW)
        distf = dist.astype(F32)
        for j in range(N_PAIRS):
            slope = jnp.where(odd_head, ALIBI_SLOPES[2 * j + 1], ALIBI_SLOPES[2 * j])
            bias_ref[later, j] = jnp.where(mask, -slope * distf, -1e30)


def _kv_pair_rows(kv_tile, parity, low):
    own = jnp.where(low if parity == 0 else jnp.logical_not(low), kv_tile, 0.0)
    other = pltpu.roll(own, HEAD_DIM, 1)
    lo, hi = (own, other) if parity == 0 else (other, own)
    return jnp.concatenate([lo, hi], axis=0).astype(BF16)


def _pair_softmax(s_t, sink_even, sink_odd):
    out = []
    for e, sink in enumerate((sink_even, sink_odd)):
        se = s_t[e * KEYS:(e + 1) * KEYS]
        m = jnp.maximum(jnp.max(se, axis=0, keepdims=True), sink)
        ee = jnp.exp(se - m)
        es = jnp.exp(sink - m)
        inv = 1.0 / (jnp.sum(ee, axis=0, keepdims=True) + es)
        out.append((ee * inv, es * inv))
    return out


def _attn_rows(n):
    q0 = pl.multiple_of(n * BLOCK, BLOCK)
    k0 = pl.multiple_of(jnp.maximum(n - 1, 0) * BLOCK, BLOCK)
    return q0, k0, jnp.minimum(n, 1)


def _attn_fwd(qkv, qg_pair, kg_pair, sinks, nseq, seq):
    t = qkv.shape[0]
    dq = N_Q_HEADS * HEAD_DIM
    dkv = N_KV_HEADS * HEAD_DIM

    def body(sk_ref, qkv_ref, qg_ref, kg_ref, o_ref, bias_ref):
        @pl.when(pl.program_id(0) == 0)
        def _():
            _fill_bias(bias_ref)

        low = lax.broadcasted_iota(jnp.int32, (1, LANES), 1) < HEAD_DIM
        qg = qg_ref[...] * QK_SCALE
        kg = kg_ref[...]

        def blk(n, carry):
            q0, k0, later = _attn_rows(n)
            for kt in range(dkv // LANES):
                kraw = qkv_ref[pl.ds(k0, KEYS), dq + kt * LANES:dq + (kt + 1) * LANES].astype(F32)
                vraw = qkv_ref[pl.ds(k0, KEYS), dq + dkv + kt * LANES:dq + dkv + (kt + 1) * LANES].astype(F32)
                kn, _, _ = _pair_norm(kraw, kg, low)
                for par in range(2):
                    kh = 2 * kt + par
                    k_pair = _kv_pair_rows(kn, par, low)
                    v_pair = _kv_pair_rows(vraw, par, low)
                    for jj in range(2):
                        j = 2 * kh + jj
                        qraw = qkv_ref[pl.ds(q0, BLOCK), j * LANES:(j + 1) * LANES].astype(F32)
                        qn, _, _ = _pair_norm(qraw, qg, low)
                        s_t = _dot(k_pair, qn.astype(BF16), NT) + bias_ref[later, j]
                        (p0, _), (p1, _) = _pair_softmax(s_t, sk_ref[0, 2 * j], sk_ref[0, 2 * j + 1])
                        p_t = jnp.concatenate([p0, p1], axis=0).astype(BF16)
                        o_ref[pl.ds(q0, BLOCK), j * LANES:(j + 1) * LANES] = _dot(p_t, v_pair, TN).astype(BF16)
            return carry

        lax.fori_loop(0, seq // BLOCK, blk, 0)

    return pl.pallas_call(
        body, name="attn_fwd", grid=(nseq,),
        in_specs=[pl.BlockSpec(memory_space=pltpu.SMEM),
                  pl.BlockSpec((seq, dq + 2 * dkv), lambda b: (b, 0)),
                  pl.BlockSpec((1, LANES), lambda b: (0, 0)),
                  pl.BlockSpec((1, LANES), lambda b: (0, 0))],
        out_specs=pl.BlockSpec((seq, dq), lambda b: (b, 0)),
        out_shape=_sds((t, dq), BF16),
        scratch_shapes=[pltpu.VMEM((2, N_PAIRS, 2 * KEYS, BLOCK), F32)],
        compiler_params=_params(("arbitrary",)))(sinks, qkv, qg_pair, kg_pair)


def _attn_bwd(do, qkv, qg_pair, kg_pair, sinks, nseq, seq):
    t = qkv.shape[0]
    dq = N_Q_HEADS * HEAD_DIM
    dkv = N_KV_HEADS * HEAD_DIM

    def body(sk_ref, do_ref, qkv_ref, qg_ref, kg_ref, o_ref, dqg_ref, dkg_ref, dsk_ref, acc_ref, bias_ref):
        @pl.when(pl.program_id(0) == 0)
        def _():
            _fill_bias(bias_ref)
            dqg_ref[...] = jnp.zeros_like(dqg_ref)
            dkg_ref[...] = jnp.zeros_like(dkg_ref)
            dsk_ref[...] = jnp.zeros_like(dsk_ref)

        acc_ref[...] = jnp.zeros_like(acc_ref)
        low = lax.broadcasted_iota(jnp.int32, (1, LANES), 1) < HEAD_DIM
        head_row = lax.broadcasted_iota(jnp.int32, (N_Q_HEADS, LANES), 0)
        qg = qg_ref[...] * QK_SCALE
        kg = kg_ref[...]

        def blk(n, carry):
            dqg_acc, dkg_acc, dsk_acc = carry
            q0, k0, later = _attn_rows(n)
            for kt in range(dkv // LANES):
                kraw = qkv_ref[pl.ds(k0, KEYS), dq + kt * LANES:dq + (kt + 1) * LANES].astype(F32)
                vraw = qkv_ref[pl.ds(k0, KEYS), dq + dkv + kt * LANES:dq + dkv + (kt + 1) * LANES].astype(F32)
                kn, khat, rk = _pair_norm(kraw, kg, low)
                dk_tile = None
                dv_tile = None
                for par in range(2):
                    kh = 2 * kt + par
                    own = low if par == 0 else jnp.logical_not(low)
                    k_pair = _kv_pair_rows(kn, par, low)
                    v_pair = _kv_pair_rows(vraw, par, low)
                    dkn_rows = jnp.zeros((2 * KEYS, LANES), F32)
                    dv_rows = jnp.zeros((2 * KEYS, LANES), F32)
                    for jj in range(2):
                        j = 2 * kh + jj
                        qraw = qkv_ref[pl.ds(q0, BLOCK), j * LANES:(j + 1) * LANES].astype(F32)
                        qn, qhat, rq = _pair_norm(qraw, qg, low)
                        qn_b = qn.astype(BF16)
                        do_b = do_ref[pl.ds(q0, BLOCK), j * LANES:(j + 1) * LANES]
                        s_t = _dot(k_pair, qn_b, NT) + bias_ref[later, j]
                        dp_t = _dot(v_pair, do_b, NT)
                        ds_halves = []
                        probs = _pair_softmax(s_t, sk_ref[0, 2 * j], sk_ref[0, 2 * j + 1])
                        for e, (p, ps) in enumerate(probs):
                            dp = dp_t[e * KEYS:(e + 1) * KEYS]
                            dsum = jnp.sum(p * dp, axis=0, keepdims=True)
                            ds_halves.append(p * (dp - dsum))
                            dsk_acc = dsk_acc - jnp.where(head_row == 2 * j + e, ps * dsum, 0.0)
                        p_t = jnp.concatenate([probs[0][0], probs[1][0]], axis=0).astype(BF16)
                        ds_t = jnp.concatenate(ds_halves, axis=0).astype(BF16)
                        dv_rows = dv_rows + _dot(p_t, do_b, NN)
                        dkn_rows = dkn_rows + _dot(ds_t, qn_b, NN)
                        dqn = _dot(ds_t, k_pair, TN)
                        dqg_acc = dqg_acc + jnp.sum(dqn * qhat, axis=0, keepdims=True)
                        dqhat = dqn * qg
                        prod = dqhat * qhat
                        m_lo = jnp.sum(jnp.where(low, prod, 0.0), axis=-1, keepdims=True)
                        m_hi = jnp.sum(jnp.where(low, 0.0, prod), axis=-1, keepdims=True)
                        mean = jnp.where(low, m_lo, m_hi) * (1.0 / HEAD_DIM)
                        o_ref[pl.ds(q0, BLOCK), j * LANES:(j + 1) * LANES] = (rq * (dqhat - qhat * mean)).astype(BF16)
                    dkn_acc = jnp.where(low, dkn_rows[0:KEYS], dkn_rows[KEYS:2 * KEYS])
                    dv_acc = jnp.where(low, dv_rows[0:KEYS], dv_rows[KEYS:2 * KEYS])
                    dkn = dkn_acc + pltpu.roll(dkn_acc, HEAD_DIM, 1)
                    dvh = dv_acc + pltpu.roll(dv_acc, HEAD_DIM, 1)
                    khat_own = jnp.where(own, khat, 0.0)
                    khat_dup = khat_own + pltpu.roll(khat_own, HEAD_DIM, 1)
                    dkg_acc = dkg_acc + jnp.sum(jnp.where(own, dkn * khat_dup, 0.0), axis=0, keepdims=True)
                    dkhat = dkn * kg
                    mean_k = jnp.sum(dkhat * khat_dup, axis=-1, keepdims=True) * (1.0 / LANES)
                    dk_raw = rk * (dkhat - khat_dup * mean_k)
                    dk_tile = jnp.where(own, dk_raw, 0.0) if dk_tile is None else jnp.where(own, dk_raw, dk_tile)
                    dv_tile = jnp.where(own, dvh, 0.0) if dv_tile is None else jnp.where(own, dvh, dv_tile)
                acc_ref[pl.ds(k0, KEYS), kt * LANES:(kt + 1) * LANES] += dk_tile
                acc_ref[pl.ds(k0, KEYS), dkv + kt * LANES:dkv + (kt + 1) * LANES] += dv_tile
            return dqg_acc, dkg_acc, dsk_acc

        zero = jnp.zeros((1, LANES), F32)
        carry = (zero, zero, jnp.zeros((N_Q_HEADS, LANES), F32))
        dqg_acc, dkg_acc, dsk_acc = lax.fori_loop(0, seq // BLOCK, blk, carry)
        dqg_ref[...] += dqg_acc * QK_SCALE
        dkg_ref[...] += dkg_acc
        dsk_ref[...] += dsk_acc
        o_ref[:, dq:dq + 2 * dkv] = acc_ref[...].astype(BF16)

    small = pl.BlockSpec((1, LANES), lambda b: (0, 0))
    heads = pl.BlockSpec((N_Q_HEADS, LANES), lambda b: (0, 0))
    return pl.pallas_call(
        body, name="attn_bwd", grid=(nseq,),
        in_specs=[pl.BlockSpec(memory_space=pltpu.SMEM),
                  pl.BlockSpec((seq, dq), lambda b: (b, 0)),
                  pl.BlockSpec((seq, dq + 2 * dkv), lambda b: (b, 0)),
                  small, small],
        out_specs=[pl.BlockSpec((seq, dq + 2 * dkv), lambda b: (b, 0)), small, small, heads],
        out_shape=[_sds((t, dq + 2 * dkv), BF16), _sds((1, LANES), F32), _sds((1, LANES), F32),
                   _sds((N_Q_HEADS, LANES), F32)],
        scratch_shapes=[pltpu.VMEM((seq, 2 * dkv), F32), pltpu.VMEM((2, N_PAIRS, 2 * KEYS, BLOCK), F32)],
        compiler_params=_params(("arbitrary",)))(sinks, do, qkv, qg_pair, kg_pair)


def _place():
    x, y, c = lax.axis_index("x"), lax.axis_index("y"), lax.axis_index("c")
    other_chips = [(1 - x, y), (x, 1 - y), (1 - x, 1 - y)]
    return x, y, c, other_chips


def _half_rows(c, rows):
    rh = rows // 2
    return pl.ds(pl.multiple_of(c * rh, BF16_ROWS), rh)


def _cast_own(name, w, place, layer=None):
    nl, r, cdim = w.shape
    first = 0
    if layer is not None:
        nl, first = 1, layer
    rt = _row_tile(r, 4 * cdim, ELEMENTWISE_BLOCK)

    def body(s_ref, w_ref, o_ref):
        o_ref[...] = w_ref[...].astype(BF16)

    grid_spec = pltpu.PrefetchScalarGridSpec(
        num_scalar_prefetch=1, grid=(nl, r // rt),
        in_specs=[pl.BlockSpec((None, rt, cdim), lambda l, i, s: (first + l, i, 0))],
        out_specs=pl.BlockSpec((None, None, rt, cdim), lambda l, i, s: (l, s[1], i, 0)))
    return pl.pallas_call(
        body, name=name, grid_spec=grid_spec, out_shape=_sds((nl, N_CHIPS, r, cdim), BF16),
        compiler_params=_params(("parallel", "parallel")))(place, w)


def _gather_protocol(outs, shapes, send_sems, recv_sems):
    n = len(outs)
    x, y, c, other_chips = _place()
    me_chip = 2 * x + y
    sibling = (x, y, 1 - c)

    def rows(u, chip, half):
        return outs[u].at[:, chip, _half_rows(half, shapes[u][2]), :]

    def copy(sem, part, to):
        return pltpu.make_async_remote_copy(src_ref=part, dst_ref=part, send_sem=send_sems.at[sem],
                                            recv_sem=recv_sems.at[sem], device_id=to, device_id_type=MESH)

    sends = []
    for u in range(n):
        for k, chip in enumerate(other_chips):
            cp = copy(6 * u + k, rows(u, me_chip, c), (*chip, c))
            cp.start()
            sends.append(cp)
    for u in range(n):
        for k, chip in enumerate(other_chips):
            got = rows(u, 2 * chip[0] + chip[1], c)
            copy(6 * u + k, got, (*chip, c)).wait_recv()
            cp = copy(6 * u + 3 + k, got, sibling)
            cp.start()
            sends.append(cp)
    for u in range(n):
        for k, chip in enumerate(other_chips):
            copy(6 * u + 3 + k, rows(u, 2 * chip[0] + chip[1], 1 - c), sibling).wait_recv()
    for cp in sends:
        cp.wait_send()


def _hbm_ref(a):
    return jax.new_ref(a, memory_space=pltpu.MemorySpace.HBM)


def _hbm_empty(shape, dtype):
    return jax.empty_ref(_sds(shape, dtype), memory_space=pltpu.MemorySpace.HBM)


def _sibling_peer():
    x, y, c, _ = _place()
    return [(x, y, 1 - c)]


def _chip_peers():
    x, y, c, other_chips = _place()
    return [(*chip, c) for chip in other_chips]


def _gather_peers():
    return _chip_peers() + _sibling_peer()


def _on_sequencer(name, collective_id, n_sems, peers, protocol):
    @pl.kernel(mesh=plsc.ScalarSubcoreMesh(axis_name="sequencer", num_cores=1), name=name,
               scratch_types=(pltpu.SemaphoreType.DMA((n_sems,)), pltpu.SemaphoreType.DMA((n_sems,))),
               compiler_params=pltpu.CompilerParams(collective_id=collective_id))
    def launch(send_sems, recv_sems):
        barrier = pltpu.get_barrier_semaphore()
        targets = peers()
        for peer in targets:
            pl.semaphore_signal(barrier, inc=1, device_id=peer, device_id_type=MESH)
        pl.semaphore_wait(barrier, len(targets))
        protocol(send_sems, recv_sems)

    launch()


def _seq_allgather(name, collective_id, bufs):
    shapes = [b.shape for b in bufs]
    refs = [_hbm_ref(b) for b in bufs]
    _on_sequencer(name, collective_id, 6 * len(bufs), _gather_peers,
                  lambda send_sems, recv_sems: _gather_protocol(refs, shapes, send_sems, recv_sems))
    return [r[...] for r in refs]


def _exchange_protocol(gs, outs, shapes, send_sems, recv_sems):
    x, y, c, _ = _place()
    sends = []
    for u in range(len(gs)):
        cp = pltpu.make_async_remote_copy(
            src_ref=gs[u].at[:, _half_rows(1 - c, shapes[u][1]), :], dst_ref=outs[u],
            send_sem=send_sems.at[u], recv_sem=recv_sems.at[u], device_id=(x, y, 1 - c), device_id_type=MESH)
        cp.start()
        sends.append(cp)
    for cp in sends:
        cp.wait_recv()
    for cp in sends:
        cp.wait_send()


def _seq_exchange(name, collective_id, grads):
    shapes = [g.shape for g in grads]
    gs = [_hbm_ref(g) for g in grads]
    outs = [_hbm_empty((s[0], s[1] // 2, s[2]), F32) for s in shapes]
    _on_sequencer(name, collective_id, len(grads), _sibling_peer,
                  lambda send_sems, recv_sems: _exchange_protocol(gs, outs, shapes, send_sems, recv_sems))
    return [o[...] for o in outs]


def _sum_halves(name, g, got, place, after):
    _, r, cdim = g.shape
    rh = r // 2
    rt = _row_tile(rh, 4 * cdim, ELEMENTWISE_BLOCK)
    nr = rh // rt

    def body(s_ref, g_ref, got_ref, after_ref, pb_ref, pf_ref):
        s = g_ref[...] + got_ref[...]
        pb_ref[...] = s.astype(BF16)

        @pl.when(pl.program_id(1) == s_ref[1])
        def _():
            pf_ref[...] = s

    grid_spec = pltpu.PrefetchScalarGridSpec(
        num_scalar_prefetch=1, grid=(nr, N_CHIPS),
        in_specs=[pl.BlockSpec((None, rt, cdim), lambda i, q, s: (q, s[0] * nr + i, 0)),
                  pl.BlockSpec((None, rt, cdim), lambda i, q, s: (q, i, 0)),
                  pl.BlockSpec(memory_space=pl.ANY)],
        out_specs=[pl.BlockSpec((None, rt, cdim), lambda i, q, s: (q, i, 0)),
                   pl.BlockSpec((rt, cdim), lambda i, q, s: (i, 0))])
    return pl.pallas_call(
        body, name=name, grid_spec=grid_spec,
        out_shape=[_sds((N_CHIPS, rh, cdim), BF16), _sds((rh, cdim), F32)],
        compiler_params=_params(("parallel", "arbitrary")))(place, g, got, after)


def _scatter_protocol(ps, outs, send_sems, recv_sems):
    x, y, c, other_chips = _place()
    sends = []
    for u in range(len(ps)):
        for k, chip in enumerate(other_chips):
            cp = pltpu.make_async_remote_copy(
                src_ref=ps[u].at[2 * chip[0] + chip[1]], dst_ref=outs[u].at[k],
                send_sem=send_sems.at[3 * u + k], recv_sem=recv_sems.at[3 * u + k],
                device_id=(*chip, c), device_id_type=MESH)
            cp.start()
            sends.append(cp)
    for cp in sends:
        cp.wait_recv()
    for cp in sends:
        cp.wait_send()


def _seq_scatter(name, collective_id, partials):
    ps = [_hbm_ref(p) for p in partials]
    outs = [_hbm_empty((3, p.shape[1], p.shape[2]), BF16) for p in partials]
    _on_sequencer(name, collective_id, 3 * len(partials), _chip_peers,
                  lambda send_sems, recv_sems: _scatter_protocol(ps, outs, send_sems, recv_sems))
    return [o[...] for o in outs]


def _sum_partials(name, own, got, place, layer, nl, prev, after):
    rh, cdim = own.shape
    rt = _row_tile(rh, 4 * cdim, ELEMENTWISE_BLOCK)
    nr = rh // rt

    def body(s_ref, own_ref, got_ref, *rest):
        o_ref = rest[-1]
        o_ref[...] = ((own_ref[...] + got_ref[0].astype(F32)) + got_ref[1].astype(F32)) + got_ref[2].astype(F32)

    in_specs = [pl.BlockSpec((rt, cdim), lambda i, s: (i, 0)), pl.BlockSpec((3, rt, cdim), lambda i, s: (0, i, 0)),
                pl.BlockSpec(memory_space=pl.ANY)]
    args = [place, own, got, after]
    aliases = {}
    if prev is not None:
        in_specs.append(pl.BlockSpec(memory_space=pl.ANY))
        args.append(prev)
        aliases = {4: 0}
    grid_spec = pltpu.PrefetchScalarGridSpec(
        num_scalar_prefetch=1, grid=(nr,), in_specs=in_specs,
        out_specs=pl.BlockSpec((None, rt, cdim), lambda i, s: (layer, s[0] * nr + i, 0)))
    return pl.pallas_call(
        body, name=name, grid_spec=grid_spec, out_shape=_sds((nl, 2 * rh, cdim), F32),
        input_output_aliases=aliases, compiler_params=_params(("parallel",)))(*args)


def _share_protocol(outs, shapes, units, send_sems, recv_sems):
    x, y, c, _ = _place()
    sends = []
    for u, (w, l) in enumerate(units):
        mine = outs[w].at[l, _half_rows(c, shapes[w][1]), :]
        cp = pltpu.make_async_remote_copy(src_ref=mine, dst_ref=mine, send_sem=send_sems.at[u],
                                          recv_sem=recv_sems.at[u], device_id=(x, y, 1 - c), device_id_type=MESH)
        cp.start()
        sends.append(cp)
    for u, (w, l) in enumerate(units):
        theirs = outs[w].at[l, _half_rows(1 - c, shapes[w][1]), :]
        pltpu.make_async_remote_copy(src_ref=theirs, dst_ref=theirs, send_sem=send_sems.at[u],
                                     recv_sem=recv_sems.at[u], device_id=(x, y, 1 - c),
                                     device_id_type=MESH).wait_recv()
    for cp in sends:
        cp.wait_send()


def _seq_share(name, collective_id, bufs):
    shapes = [b.shape for b in bufs]
    units = [(w, l) for w in range(len(bufs)) for l in range(shapes[w][0])]
    refs = [_hbm_ref(b) for b in bufs]
    _on_sequencer(name, collective_id, len(units), _sibling_peer,
                  lambda send_sems, recv_sems: _share_protocol(refs, shapes, units, send_sems, recv_sems))
    return [r[...] for r in refs]


def _gather_blocks(block_ref, all_ref, send_sems, recv_sems):
    x, y, c, _ = _place()
    me = 4 * x + 2 * y + c
    all_ref[me] = block_ref[...]
    sends = []
    for rel in range(1, 8):
        fx, fy, fc = (rel >> 2) & 1, (rel >> 1) & 1, rel & 1
        peer = (x ^ fx, y ^ fy, c ^ fc)
        cp = pltpu.make_async_remote_copy(src_ref=block_ref, dst_ref=all_ref.at[me], send_sem=send_sems.at[rel - 1],
                                          recv_sem=recv_sems.at[rel - 1], device_id=peer, device_id_type=MESH)
        cp.start()
        sends.append(cp)
    for cp in sends:
        cp.wait_recv()
    for cp in sends:
        cp.wait_send()


def _gather_conv_w(cw_block):
    r, d = cw_block.shape

    def body(b_ref, o_ref, all_ref, send_sems, recv_sems):
        _gather_blocks(b_ref, all_ref, send_sems, recv_sems)
        o_ref[...] = (all_ref[0] + all_ref[2]) + (all_ref[4] + all_ref[6])

    vm = pl.BlockSpec(memory_space=pltpu.VMEM)
    return pl.pallas_call(
        body, name="gather_conv_w", in_specs=[vm], out_specs=vm, out_shape=_sds((r, d), F32),
        scratch_shapes=[pltpu.VMEM((8, r, d), F32), pltpu.SemaphoreType.DMA((7,)), pltpu.SemaphoreType.DMA((7,))],
    )(cw_block)


def _adam(w, g, m, v):
    m_new = ADAM_B1 * m + (1.0 - ADAM_B1) * g
    v_new = ADAM_B2 * v + (1.0 - ADAM_B2) * (g * g)
    m_hat = m_new / (1.0 - ADAM_B1 ** ADAM_STEP)
    v_hat = v_new / (1.0 - ADAM_B2 ** ADAM_STEP)
    delta = -ADAM_LR * (m_hat / (jnp.sqrt(v_hat) + ADAM_EPS) + ADAM_WD * w)
    return delta, m_new, v_new


def _small_step(dnm0, dnm1, dnf0, dnf1, dcw, dqg, dkg, dsk, loss, w_blk, m_blk, v_blk):
    d = w_blk.shape[1]

    def body(dnm0_ref, dnm1_ref, dnf0_ref, dnf1_ref, dcw_ref, dqg_ref, dkg_ref, dsk_ref, loss_ref,
             w_ref, m_ref, v_ref, g_ref, dl_ref, mo_ref, vo_ref, blk_ref, all_ref, send_sems, recv_sems):
        blk_ref[...] = jnp.zeros_like(blk_ref)
        blk_ref[0:1, :] = jnp.sum(dnm0_ref[...], axis=0, keepdims=True)
        blk_ref[1:2, :] = jnp.sum(dnm1_ref[...], axis=0, keepdims=True)
        blk_ref[8:9, :] = jnp.sum(dnf0_ref[...], axis=0, keepdims=True)
        blk_ref[9:10, :] = jnp.sum(dnf1_ref[...], axis=0, keepdims=True)
        blk_ref[16:19, :] = dcw_ref[...]
        dqg_v = dqg_ref[...]
        dkg_v = dkg_ref[...]
        blk_ref[24:25, 0:LANES] = dqg_v + pltpu.roll(dqg_v, HEAD_DIM, 1)
        blk_ref[24:25, LANES:2 * LANES] = dkg_v + pltpu.roll(dkg_v, HEAD_DIM, 1)
        for h in range(N_Q_HEADS):
            blk_ref[24:25, 2 * LANES + h:2 * LANES + h + 1] = jnp.sum(dsk_ref[h:h + 1, :], axis=1, keepdims=True)
        blk_ref[24:25, 3 * LANES:4 * LANES] = jnp.broadcast_to(loss_ref[...], (1, LANES))
        _gather_blocks(blk_ref, all_ref, send_sems, recv_sems)
        g = all_ref[0]
        for dev in range(1, 8):
            g = g + all_ref[dev]
        g_ref[...] = g
        delta, m_new, v_new = _adam(w_ref[...], g, m_ref[...], v_ref[...])
        dl_ref[...] = delta
        mo_ref[...] = m_new
        vo_ref[...] = v_new

    vm = pl.BlockSpec(memory_space=pltpu.VMEM)
    blk = _sds((SMALL_ROWS, d), F32)
    return pl.pallas_call(
        body, name="small_step", in_specs=[vm] * 12, out_specs=[vm] * 4, out_shape=[blk] * 4,
        scratch_shapes=[pltpu.VMEM((SMALL_ROWS, d), F32), pltpu.VMEM((8, SMALL_ROWS, d), F32),
                        pltpu.SemaphoreType.DMA((7,)), pltpu.SemaphoreType.DMA((7,))],
    )(dnm0, dnm1, dnf0, dnf1, dcw, dqg, dkg, dsk, loss, w_blk, m_blk, v_blk)


def _adam_step(name, w, g, m, v):
    nl, r, cdim = w.shape
    rt = _row_tile(r, 4 * cdim, ELEMENTWISE_BLOCK)

    def body(w_ref, g_ref, m_ref, v_ref, d_ref, mo_ref, vo_ref):
        delta, m_new, v_new = _adam(w_ref[...], g_ref[...], m_ref[...], v_ref[...])
        d_ref[...] = delta
        mo_ref[...] = m_new
        vo_ref[...] = v_new

    spec = pl.BlockSpec((None, rt, cdim), lambda l, i: (l, i, 0))
    return pl.pallas_call(
        body, name=name, grid=(nl, r // rt), in_specs=[spec] * 4, out_specs=[spec] * 3,
        out_shape=[_sds(w.shape, F32)] * 3,
        compiler_params=_params(("parallel", "parallel")))(w, g, m, v)


def _pad_rows(a, rows=SUBLANES):
    return jnp.pad(a, ((0, rows - a.shape[0]), (0, 0)))


def _small_block(nm, nf, cw_local, qg, kg, sk, chip):
    d = nm.shape[1]
    cw_rows = lax.dynamic_update_slice(jnp.zeros((SUBLANES, d), F32), cw_local, (0, chip * cw_local.shape[1]))
    misc = jnp.concatenate([qg, qg, kg, kg, jnp.pad(sk, ((0, 0), (0, LANES - sk.shape[1]))),
                            jnp.zeros((1, d - 3 * LANES), F32)], axis=1)
    return jnp.concatenate([_pad_rows(nm), _pad_rows(nf), cw_rows, _pad_rows(misc)], axis=0)


def _unpack_small(blk, chip, cw_cols):
    cw = lax.dynamic_slice(blk[16:19], (0, chip * cw_cols), (3, cw_cols))[None]
    return dict(norm_mixer=blk[0:2], norm_ffn=blk[8:10], conv_w=cw, attn_q_gain=blk[24:25, 0:HEAD_DIM],
                attn_k_gain=blk[24:25, LANES:LANES + HEAD_DIM], attn_sinks=blk[24:25, 2 * LANES:2 * LANES + N_Q_HEADS])


WEIGHT_NAMES = ("conv_w_in", "conv_w", "conv_w_out", "attn_w_qkv", "attn_q_gain", "attn_k_gain", "attn_sinks",
                "attn_w_o", "norm_mixer", "norm_ffn", "ffn_w_gate_up", "ffn_w_down")
BIG = ("conv_w_in", "conv_w_out", "attn_w_qkv", "attn_w_o", "ffn_w_gate_up", "ffn_w_down")


def kernel(x, conv_w_in, conv_w, conv_w_out, attn_w_qkv, attn_q_gain, attn_k_gain, attn_sinks, attn_w_o, norm_mixer, norm_ffn, ffn_w_gate_up, ffn_w_down, loss_target, m_conv_w_in, m_conv_w, m_conv_w_out, m_attn_w_qkv, m_attn_q_gain, m_attn_k_gain, m_attn_sinks, m_attn_w_o, m_norm_mixer, m_norm_ffn, m_ffn_w_gate_up, m_ffn_w_down, v_conv_w_in, v_conv_w, v_conv_w_out, v_attn_w_qkv, v_attn_q_gain, v_attn_k_gain, v_attn_sinks, v_attn_w_o, v_norm_mixer, v_norm_ffn, v_ffn_w_gate_up, v_ffn_w_down):
    w = dict(conv_w_in=conv_w_in, conv_w=conv_w, conv_w_out=conv_w_out, attn_w_qkv=attn_w_qkv,
             attn_q_gain=attn_q_gain, attn_k_gain=attn_k_gain, attn_sinks=attn_sinks, attn_w_o=attn_w_o,
             norm_mixer=norm_mixer, norm_ffn=norm_ffn, ffn_w_gate_up=ffn_w_gate_up, ffn_w_down=ffn_w_down)
    m = dict(conv_w_in=m_conv_w_in, conv_w=m_conv_w, conv_w_out=m_conv_w_out, attn_w_qkv=m_attn_w_qkv,
             attn_q_gain=m_attn_q_gain, attn_k_gain=m_attn_k_gain, attn_sinks=m_attn_sinks, attn_w_o=m_attn_w_o,
             norm_mixer=m_norm_mixer, norm_ffn=m_norm_ffn, ffn_w_gate_up=m_ffn_w_gate_up, ffn_w_down=m_ffn_w_down)
    v = dict(conv_w_in=v_conv_w_in, conv_w=v_conv_w, conv_w_out=v_conv_w_out, attn_w_qkv=v_attn_w_qkv,
             attn_q_gain=v_attn_q_gain, attn_k_gain=v_attn_k_gain, attn_sinks=v_attn_sinks, attn_w_o=v_attn_w_o,
             norm_mixer=v_norm_mixer, norm_ffn=v_norm_ffn, ffn_w_gate_up=v_ffn_w_gate_up, ffn_w_down=v_ffn_w_down)

    nseq, seq, d = x.shape
    t = nseq * seq
    chip = 2 * lax.axis_index("x") + lax.axis_index("y")
    core = lax.axis_index("c")
    place = jnp.stack([core, chip]).astype(jnp.int32)
    x0 = x.reshape(t, d)
    tgt = loss_target.reshape(t, d)

    cw_block = lax.dynamic_update_slice(jnp.zeros((SUBLANES, d), F32), conv_w[0], (0, chip * conv_w.shape[2]))
    cw_full = _gather_conv_w(cw_block)[0:3]
    def cast(k, layer=None):
        return _cast_own(f"cast_{k}" + ("" if layer is None else str(layer)), w[k], place, layer)

    w_in, w_out = _seq_allgather("allgather_conv", 1, [cast("conv_w_in"), cast("conv_w_out")])
    w_gu0, w_dn0 = _seq_allgather("allgather_ffn0", 2, [cast("ffn_w_gate_up", 0), cast("ffn_w_down", 0)])
    w_qkv, w_o, w_gu1, w_dn1 = _seq_allgather(
        "allgather_rest", 3, [cast("attn_w_qkv"), cast("attn_w_o"), cast("ffn_w_gate_up", 1), cast("ffn_w_down", 1)])
    w_out = w_out.reshape(1, d, d)
    w_o = w_o.reshape(1, d, d)
    w_gu = [w_gu0, w_gu1]
    w_dn = [w_dn0.reshape(1, D_FF, d), w_dn1.reshape(1, D_FF, d)]

    qg_pair = jnp.concatenate([attn_q_gain, attn_q_gain], axis=1)
    kg_pair = jnp.concatenate([attn_k_gain, attn_k_gain], axis=1)

    def ffn_bwd(i, dxo, xin, h, g, u, a):
        g_dn = _wgrad_down(f"ffn{i}_down_wgrad", a, dxo, D_FF // 2)
        dg, du = _mm_down_t_swiglu(f"ffn{i}_down_dgrad", dxo, w_dn[i], 0, g, u)
        g_gu = _wgrad_up2(f"ffn{i}_up_wgrad", h, dg, du)
        dxi, dgain = _dgrad_norm_ffn(f"ffn{i}_up_dgrad", dg, du, w_gu[i], 0, xin, norm_ffn[i:i + 1], dxo)
        return dxi, dgain, g_gu, g_dn

    h0 = _rms_fwd("conv_norm", x0, norm_mixer[0:1])
    bcx = _mm_up_joined("conv_in", h0, w_in, 512)
    z = _conv_fwd(bcx, cw_full, nseq, seq)
    x1, h1 = _mm_down_norm("conv_out", z, w_out, 0, x0, norm_ffn[0:1])
    g0, u0, a0 = _mm_up_swiglu("ffn0_up", h1, w_gu[0], 0)
    x2, h2 = _mm_down_norm("ffn0_down", a0, w_dn[0], 0, x1, norm_mixer[1:2])
    qkv = _mm_up_joined("attn_qkv", h2, w_qkv, 1024)
    o = _attn_fwd(qkv, qg_pair, kg_pair, attn_sinks, nseq, seq)
    x3, h3 = _mm_down_norm("attn_out", o, w_o, 0, x2, norm_ffn[1:2])
    g1, u1, a1 = _mm_up_swiglu("ffn1_up", h3, w_gu[1], 0)
    dy, loss_part = _mm_down_loss("ffn1_down", a1, w_dn[1], 0, x3, tgt)

    finished = {k: None for k in BIG}

    def exchange(tag, cid, units):
        return units, _seq_exchange(f"exchange_{tag}", cid, [g for _, _, g in units])

    def scatter(tag, cid, group, after):
        units, got = group
        sums = [_sum_halves(f"sum_halves_{k}{l}", g, r, place, after) for (k, l, g), r in zip(units, got)]
        return units, sums, _seq_scatter(f"scatter_{tag}", cid, [pb for pb, _ in sums])

    def finish(group, after):
        units, sums, arrived = group
        for (k, l, _), (_, pf), r in zip(units, sums, arrived):
            finished[k] = _sum_partials(f"sum_partials_{k}{l}", pf, r, place, l, w[k].shape[0], finished[k], after)

    dx3, dnf1, g_gu1, g_dn1 = ffn_bwd(1, dy, x3, h3, g1, u1, a1)
    ffn1 = exchange("ffn1", 4, [("ffn_w_down", 1, g_dn1), ("ffn_w_gate_up", 1, g_gu1)])
    g_o = _wgrad_down("attn_out_wgrad", o, dx3, d)
    do = _mm_down_t("attn_out_dgrad", dx3, w_o, 0)
    ffn1 = scatter("ffn1", 8, ffn1, do)
    dqkv, dqg, dkg, dsk = _attn_bwd(do, qkv, qg_pair, kg_pair, attn_sinks, nseq, seq)
    g_qkv = _wgrad_joined("attn_qkv_wgrad", h2, dqkv)
    attn = exchange("attn", 5, [("attn_w_o", 0, g_o), ("attn_w_qkv", 0, g_qkv)])
    dx2, dnm1 = _dgrad_norm_qkv("attn_qkv_dgrad", dqkv, w_qkv, x2, norm_mixer[1:2], dx3)
    finish(ffn1, dx2)
    attn = scatter("attn", 9, attn, dx2)
    dx1, dnf0, g_gu0, g_dn0 = ffn_bwd(0, dx2, x1, h1, g0, u0, a0)
    ffn0 = exchange("ffn0", 6, [("ffn_w_down", 0, g_dn0), ("ffn_w_gate_up", 0, g_gu0)])
    g_out = _wgrad_down("conv_out_wgrad", z, dx1, d)
    dz = _mm_down_t("conv_out_dgrad", dx1, w_out, 0)
    finish(attn, dz)
    ffn0 = scatter("ffn0", 10, ffn0, dz)
    dbcx, dcw = _conv_bwd(dz, bcx, cw_full, nseq, seq)
    g_in = _wgrad_conv_in("conv_in_wgrad", h0, dbcx, conv_w_in.shape[2])
    conv = exchange("conv", 7, [("conv_w_out", 0, g_out), ("conv_w_in", 0, g_in)])
    dx0, dnm0 = _dgrad_norm_conv("conv_in_dgrad", dbcx, w_in, x0, norm_mixer[0:1], dx1)
    finish(ffn0, dx0)
    late = ("attn_w_qkv", "attn_w_o", "ffn_w_gate_up", "ffn_w_down")
    grads_late = _seq_share("share_late", 12, [finished[k] for k in late])
    conv = scatter("conv", 11, conv, dx0)

    grad, delta, new_m, new_v = {}, {}, {}, {}

    def adam(k, g):
        grad[k] = g
        delta[k], new_m[k], new_v[k] = _adam_step(f"adam_{k}", w[k], g, m[k], v[k])

    for k, g in zip(late, grads_late):
        adam(k, g)

    def blocks(src):
        return _small_block(src["norm_mixer"], src["norm_ffn"], src["conv_w"][0], src["attn_q_gain"],
                            src["attn_k_gain"], src["attn_sinks"], chip)

    g_blk, d_blk, m_blk, v_blk = _small_step(dnm0, dnm1, dnf0, dnf1, dcw, dqg, dkg, dsk, loss_part,
                                             blocks(w), blocks(m), blocks(v))

    done = sum(new_v[k][0, 0:1, 0:1] for k in late) + v_blk[0:1, 0:1]
    finish(conv, done)
    last = ("conv_w_in", "conv_w_out")
    for k, g in zip(last, _seq_share("share_last", 13, [finished[k] for k in last])):
        adam(k, g)

    cw_cols = conv_w.shape[2]
    for dst, blk in ((grad, g_blk), (delta, d_blk), (new_m, m_blk), (new_v, v_blk)):
        dst.update(_unpack_small(blk, chip, cw_cols))
    loss = g_blk[24, 3 * LANES]

    return (loss, dx0.reshape(nseq, seq, d), *[grad[k] for k in WEIGHT_NAMES], *[delta[k] for k in WEIGHT_NAMES],
            *[new_m[k] for k in WEIGHT_NAMES], *[new_v[k] for k in WEIGHT_NAMES])
```

```python
import jax
import jax.numpy as jnp
from jax import lax
from jax.experimental import pallas as pl
from jax.experimental.pallas import tpu as pltpu
from jax.experimental.pallas import tpu_sc as plsc

F32 = jnp.float32
BF16 = jnp.bfloat16

D_MODEL = 1024
D_FF = 2816
N_Q_HEADS = 16
N_KV_HEADS = 4
HEAD_DIM = 64
WINDOW = 128
BLOCK = 128
EPS = 1e-6
N_CHIPS = 4
LANES = 128
SUBLANES = 8
BF16_ROWS = 16
MXU_COLS = 256
VMEM_LIMIT = 48 * 1024 * 1024
ADAM_LR, ADAM_B1, ADAM_B2, ADAM_EPS, ADAM_WD, ADAM_STEP = 0.001, 0.9, 0.999, 1e-08, 0.01, 10
ALIBI_SLOPES = tuple(2.0 ** (-8.0 * (h + 1) / N_Q_HEADS) for h in range(N_Q_HEADS))
SMALL_ROWS = 32
MESH = pl.DeviceIdType.MESH

NN = ((1,), (0,))
NT = ((1,), (1,))
TN = ((0,), (0,))


def _dot(a, b, dims):
    return lax.dot_general(a, b, (dims, ((), ())), preferred_element_type=F32)


def _pick(n, cands):
    for c in cands:
        if n % c == 0:
            return c
    raise ValueError((n, cands))


def _row_tile(rows, row_bytes, cap_bytes):
    fits = [r for r in range(BF16_ROWS, rows + 1, BF16_ROWS) if rows % r == 0 and r * row_bytes <= cap_bytes]
    if not fits:
        raise ValueError((rows, row_bytes, cap_bytes))
    return fits[-1]


ELEMENTWISE_BLOCK = 3 << 19


def _resident(block_shape, index_map):
    return pl.BlockSpec(block_shape, index_map, pipeline_mode=pl.Buffered(1))


def _params(sem):
    return pltpu.CompilerParams(dimension_semantics=sem, vmem_limit_bytes=VMEM_LIMIT)


def _sds(shape, dtype):
    return jax.ShapeDtypeStruct(shape, dtype)


def _rms(xv):
    return lax.rsqrt(jnp.mean(xv * xv, axis=-1, keepdims=True) + EPS)


def _sigmoid(g):
    return 1.0 / (1.0 + jnp.exp(-g))


def _mm_up_joined(name, a, w4, tm_pref):
    t, k = a.shape
    _, _, _, nq = w4.shape
    tm = _pick(t, (tm_pref, 256, 128))

    def body(a_ref, w_ref, o_ref, wcat_ref):
        @pl.when(pl.program_id(0) == 0)
        def _():
            for q in range(N_CHIPS):
                wcat_ref[:, q * nq:(q + 1) * nq] = w_ref[q]

        o_ref[...] = _dot(a_ref[...], wcat_ref[...], NN).astype(BF16)

    return pl.pallas_call(
        body, name=name, grid=(t // tm,),
        in_specs=[pl.BlockSpec((tm, k), lambda i: (i, 0)),
                  pl.BlockSpec((None, N_CHIPS, k, nq), lambda i: (0, 0, 0, 0))],
        out_specs=pl.BlockSpec((tm, N_CHIPS * nq), lambda i: (i, 0)),
        out_shape=_sds((t, N_CHIPS * nq), BF16),
        scratch_shapes=[pltpu.VMEM((k, N_CHIPS * nq), BF16)],
        compiler_params=_params(("arbitrary",)))(a, w4)


def _mm_norm_up_joined(name, x, gain, w4, tm_pref):
    t, k = x.shape
    _, _, _, nq = w4.shape
    tm = _pick(t, (tm_pref, 256, 128))

    def body(x_ref, g_ref, w_ref, h_ref, o_ref, wcat_ref):
        @pl.when(pl.program_id(0) == 0)
        def _():
            for q in range(N_CHIPS):
                wcat_ref[:, q * nq:(q + 1) * nq] = w_ref[q]

        xv = x_ref[...]
        h = ((xv * _rms(xv)) * g_ref[...]).astype(BF16)
        h_ref[...] = h
        o_ref[...] = _dot(h, wcat_ref[...], NN).astype(BF16)

    return pl.pallas_call(
        body, name=name, grid=(t // tm,),
        in_specs=[pl.BlockSpec((tm, k), lambda i: (i, 0)), pl.BlockSpec((1, k), lambda i: (0, 0)),
                  _resident((None, N_CHIPS, k, nq), lambda i: (0, 0, 0, 0))],
        out_specs=[pl.BlockSpec((tm, k), lambda i: (i, 0)), pl.BlockSpec((tm, N_CHIPS * nq), lambda i: (i, 0))],
        out_shape=[_sds((t, k), BF16), _sds((t, N_CHIPS * nq), BF16)],
        scratch_shapes=[pltpu.VMEM((k, N_CHIPS * nq), BF16)],
        compiler_params=_params(("arbitrary",)))(x, gain, w4)


def _mm_up_swiglu(name, h, w4, layer):
    t, k = h.shape
    _, _, _, nq = w4.shape
    tm = _pick(t, (512, 256, 128))

    def body(h_ref, wg_ref, wu_ref, dag_ref, dau_ref, a_ref):
        hv = h_ref[...]
        g = _dot(hv, wg_ref[...], NN)
        u = _dot(hv, wu_ref[...], NN)
        sg = _sigmoid(g)
        silu = g * sg
        dag_ref[...] = (u * (sg * (1.0 + g * (1.0 - sg)))).astype(BF16)
        dau_ref[...] = silu.astype(BF16)
        a_ref[...] = (silu * u).astype(BF16)

    half = N_CHIPS // 2
    out = pl.BlockSpec((tm, nq), lambda j, i: (i, j))
    return pl.pallas_call(
        body, name=name, grid=(half, t // tm),
        in_specs=[pl.BlockSpec((tm, k), lambda j, i: (i, 0)),
                  pl.BlockSpec((None, None, k, nq), lambda j, i: (layer, j, 0, 0)),
                  pl.BlockSpec((None, None, k, nq), lambda j, i: (layer, half + j, 0, 0))],
        out_specs=[out, out, out],
        out_shape=[_sds((t, half * nq), BF16)] * 3,
        compiler_params=_params(("parallel", "parallel")))(h, w4, w4)


def _mm_down_norm(name, a, w, layer, res, gain):
    t, kf = a.shape
    _, _, n = w.shape
    tm = _pick(t, (1024, 512, 256, 128))

    def body(a_ref, w_ref, r_ref, g_ref, o_ref, h_ref):
        xo = r_ref[...] + _dot(a_ref[...], w_ref[...], NN)
        o_ref[...] = xo
        h_ref[...] = ((xo * _rms(xo)) * g_ref[...]).astype(BF16)

    row = pl.BlockSpec((tm, n), lambda i: (i, 0))
    return pl.pallas_call(
        body, name=name, grid=(t // tm,),
        in_specs=[pl.BlockSpec((tm, kf), lambda i: (i, 0)),
                  _resident((None, kf, n), lambda i: (layer, 0, 0)),
                  row, pl.BlockSpec((1, n), lambda i: (0, 0))],
        out_specs=[row, row],
        out_shape=[_sds((t, n), F32), _sds((t, n), BF16)],
        compiler_params=_params(("parallel",)))(a, w, res, gain)


def _mm_down_loss(name, a, w, layer, res, tgt):
    t, kf = a.shape
    _, _, n = w.shape
    tm = _pick(t, (1024, 512, 256, 128))
    steps = t // tm

    def body(a_ref, w_ref, r_ref, t_ref, dy_ref, l_ref, acc_ref):
        i = pl.program_id(0)

        @pl.when(i == 0)
        def _():
            acc_ref[...] = jnp.zeros_like(acc_ref)

        e = (r_ref[...] + _dot(a_ref[...], w_ref[...], NN)) - t_ref[...]
        dy_ref[...] = e * (1.0 / n)
        acc_ref[...] += (e * e).reshape(tm // SUBLANES, SUBLANES, n).sum(axis=0)

        @pl.when(i == steps - 1)
        def _():
            l_ref[...] = jnp.sum(acc_ref[...], keepdims=True) * (0.5 / n)

    row = pl.BlockSpec((tm, n), lambda i: (i, 0))
    return pl.pallas_call(
        body, name=name, grid=(steps,),
        in_specs=[pl.BlockSpec((tm, kf), lambda i: (i, 0)),
                  _resident((None, kf, n), lambda i: (layer, 0, 0)), row, row],
        out_specs=[row, pl.BlockSpec((1, 1), lambda i: (0, 0))],
        out_shape=[_sds((t, n), F32), _sds((1, 1), F32)],
        scratch_shapes=[pltpu.VMEM((SUBLANES, n), F32)],
        compiler_params=_params(("arbitrary",)))(a, w, res, tgt)


def _mm_down_t(name, dx, w, layer):
    t, n = dx.shape
    _, kf, _ = w.shape
    tm = _pick(t, (512, 256, 128))

    def body(a_ref, w_ref, o_ref):
        o_ref[...] = _dot(a_ref[...].astype(BF16), w_ref[...], NT).astype(BF16)

    return pl.pallas_call(
        body, name=name, grid=(t // tm,),
        in_specs=[pl.BlockSpec((tm, n), lambda i: (i, 0)),
                  pl.BlockSpec((None, kf, n), lambda i: (layer, 0, 0))],
        out_specs=pl.BlockSpec((tm, kf), lambda i: (i, 0)),
        out_shape=_sds((t, kf), BF16),
        compiler_params=_params(("parallel",)))(dx, w)


def _mm_down_t_swiglu(name, dx, w, layer, g, u):
    t, n = dx.shape
    f = g.shape[1]
    tm = _pick(t, (512, 256, 128))

    def body(a_ref, w_ref, dag_ref, dau_ref, dg_ref, du_ref):
        da = _dot(a_ref[...].astype(BF16), w_ref[...], NT)
        dg_ref[...] = (da * dag_ref[...].astype(F32)).astype(BF16)
        du_ref[...] = (da * dau_ref[...].astype(F32)).astype(BF16)

    tile = pl.BlockSpec((tm, f), lambda i: (i, 0))
    return pl.pallas_call(
        body, name=name, grid=(t // tm,),
        in_specs=[pl.BlockSpec((tm, n), lambda i: (i, 0)),
                  _resident((None, f, n), lambda i: (layer, 0, 0)), tile, tile],
        out_specs=[tile, tile],
        out_shape=[_sds((t, f), BF16)] * 2,
        compiler_params=_params(("parallel",)))(dx, w, g, u)


def _dgrad_norm(name, acts, act_blocks, pieces, w4, layer, x, gain, dres):
    t, d = x.shape
    _, _, k, nq = w4.shape
    tm = _pick(t, (512, 256, 128))
    n_act = len(acts)

    def body(*refs):
        act_refs = refs[:n_act]
        w_ref, x_ref, g_ref, dr_ref, dx_ref, dg_ref = refs[n_act:]

        @pl.when(pl.program_id(0) == 0)
        def _():
            dg_ref[...] = jnp.zeros_like(dg_ref)

        dh = None
        for a_tile, w_tile in pieces(act_refs, w_ref):
            term = _dot(a_tile, w_tile, NT)
            dh = term if dh is None else dh + term
        xv = x_ref[...]
        r = _rms(xv)
        xhat = xv * r
        gd = dh * g_ref[...]
        dx_ref[...] = dr_ref[...] + r * (gd - xhat * jnp.mean(gd * xhat, axis=-1, keepdims=True))
        dg_ref[...] += (dh * xhat).reshape(tm // SUBLANES, SUBLANES, d).sum(axis=0)

    row = pl.BlockSpec((tm, d), lambda i: (i, 0))
    return pl.pallas_call(
        body, name=name, grid=(t // tm,),
        in_specs=[*act_blocks(tm),
                  _resident((None, N_CHIPS, k, nq), lambda i: (layer, 0, 0, 0)),
                  row, pl.BlockSpec((1, d), lambda i: (0, 0)), row],
        out_specs=[row, pl.BlockSpec((SUBLANES, d), lambda i: (0, 0))],
        out_shape=[_sds((t, d), F32), _sds((SUBLANES, d), F32)],
        compiler_params=_params(("arbitrary",)))(*acts, w4, x, gain, dres)


def _dgrad_norm_ffn(name, dg, du, w4, layer, x, gain, dres):
    nq = w4.shape[3]
    f = dg.shape[1]

    def blocks(tm):
        return [pl.BlockSpec((tm, f), lambda i: (i, 0))] * 2

    def pieces(act_refs, w_ref):
        dg_ref, du_ref = act_refs
        return [(dg_ref[:, 0:nq], w_ref[0]), (dg_ref[:, nq:2 * nq], w_ref[1]),
                (du_ref[:, 0:nq], w_ref[2]), (du_ref[:, nq:2 * nq], w_ref[3])]

    return _dgrad_norm(name, [dg, du], blocks, pieces, w4, layer, x, gain, dres)


def _dgrad_norm_qkv(name, dqkv, w4, x, gain, dres):
    nq = w4.shape[3]

    def blocks(tm):
        return [pl.BlockSpec((tm, N_CHIPS * nq), lambda i: (i, 0))]

    def pieces(act_refs, w_ref):
        return [(act_refs[0][:, q * nq:(q + 1) * nq], w_ref[q]) for q in range(N_CHIPS)]

    return _dgrad_norm(name, [dqkv], blocks, pieces, w4, 0, x, gain, dres)


def _dgrad_norm_conv(name, d3, w4, x, gain, dres):
    _, _, d = d3.shape
    nq = w4.shape[3]
    per_part, per_q = d // MXU_COLS, nq // MXU_COLS

    def blocks(tm):
        return [pl.BlockSpec((3, tm, d), lambda i: (0, i, 0))]

    def pieces(act_refs, w_ref):
        out = []
        for jb in range(3 * per_part):
            ca, cw = (jb % per_part) * MXU_COLS, (jb % per_q) * MXU_COLS
            out.append((act_refs[0][jb // per_part, :, ca:ca + MXU_COLS], w_ref[jb // per_q, :, cw:cw + MXU_COLS]))
        return out

    return _dgrad_norm(name, [d3], blocks, pieces, w4, 0, x, gain, dres)


def _wgrad_up2(name, h, dg, du):
    t, k = h.shape
    nq = dg.shape[1] // 2
    tk = _pick(t, (1024, 512, 256, 128))
    steps = t // tk
    half = N_CHIPS // 2

    def body(h_ref, dg_ref, du_ref, o_ref):
        q = pl.program_id(0)

        @pl.when(pl.program_id(1) == 0)
        def _():
            o_ref[...] = jnp.zeros_like(o_ref)

        @pl.when(q < half)
        def _():
            o_ref[...] += _dot(h_ref[...], dg_ref[...], TN)

        @pl.when(q >= half)
        def _():
            o_ref[...] += _dot(h_ref[...], du_ref[...], TN)

    return pl.pallas_call(
        body, name=name, grid=(N_CHIPS, steps),
        in_specs=[pl.BlockSpec((tk, k), lambda q, s: (s, 0)),
                  pl.BlockSpec((tk, nq), lambda q, s: (jnp.where(q < half, s, steps - 1), jnp.minimum(q, half - 1))),
                  pl.BlockSpec((tk, nq), lambda q, s: (jnp.where(q >= half, s, 0), jnp.maximum(q - half, 0)))],
        out_specs=pl.BlockSpec((None, k, nq), lambda q, s: (q, 0, 0)),
        out_shape=_sds((N_CHIPS, k, nq), F32),
        compiler_params=_params(("parallel", "arbitrary")))(h, dg, du)


def _wgrad_joined(name, h, dy):
    t, k = h.shape
    nq = dy.shape[1] // N_CHIPS
    tk = _pick(t, (1024, 512, 256, 128))

    def body(h_ref, dy_ref, o_ref):
        @pl.when(pl.program_id(0) == 0)
        def _():
            o_ref[...] = jnp.zeros_like(o_ref)

        res = _dot(h_ref[...], dy_ref[...], TN)
        for q in range(N_CHIPS):
            o_ref[q] += res[:, q * nq:(q + 1) * nq]

    return pl.pallas_call(
        body, name=name, grid=(t // tk,),
        in_specs=[pl.BlockSpec((tk, k), lambda s: (s, 0)), pl.BlockSpec((tk, N_CHIPS * nq), lambda s: (s, 0))],
        out_specs=pl.BlockSpec((N_CHIPS, k, nq), lambda s: (0, 0, 0)),
        out_shape=_sds((N_CHIPS, k, nq), F32),
        compiler_params=_params(("arbitrary",)))(h, dy)


def _wgrad_conv_in(name, h, d3, nq):
    t, k = h.shape
    d = d3.shape[2]
    per_part, per_q = d // MXU_COLS, nq // MXU_COLS
    tk = _pick(t, (512, 256, 128))

    def body(h_ref, d_ref, o_ref):
        @pl.when(pl.program_id(0) == 0)
        def _():
            o_ref[...] = jnp.zeros_like(o_ref)

        hv = h_ref[...]
        for part in range(3):
            res = _dot(hv, d_ref[part], TN)
            for cc in range(per_part):
                jb = part * per_part + cc
                co = (jb % per_q) * MXU_COLS
                o_ref[jb // per_q, :, co:co + MXU_COLS] += res[:, cc * MXU_COLS:(cc + 1) * MXU_COLS]

    return pl.pallas_call(
        body, name=name, grid=(t // tk,),
        in_specs=[pl.BlockSpec((tk, k), lambda s: (s, 0)), pl.BlockSpec((3, tk, d), lambda s: (0, s, 0))],
        out_specs=pl.BlockSpec((N_CHIPS, k, nq), lambda s: (0, 0, 0)),
        out_shape=_sds((N_CHIPS, k, nq), F32),
        compiler_params=_params(("arbitrary",)))(h, d3)


def _wgrad_down(name, a, dx, tmw):
    t, kf = a.shape
    n = dx.shape[1]
    tk = _pick(t, (1024, 512, 256, 128))

    def body(a_ref, b_ref, o_ref):
        @pl.when(pl.program_id(1) == 0)
        def _():
            o_ref[...] = jnp.zeros_like(o_ref)

        o_ref[...] += _dot(a_ref[...], b_ref[...].astype(BF16), TN)

    g = pl.pallas_call(
        body, name=name, grid=(kf // tmw, t // tk),
        in_specs=[pl.BlockSpec((tk, tmw), lambda j, s: (s, j)), pl.BlockSpec((tk, n), lambda j, s: (s, 0))],
        out_specs=pl.BlockSpec((tmw, n), lambda j, s: (j, 0)),
        out_shape=_sds((kf, n), F32),
        compiler_params=_params(("parallel", "arbitrary")))(a, dx)
    return g.reshape(N_CHIPS, kf // N_CHIPS, n)


def _shift_rows(u, k, rows):
    s = u.shape[0]
    if k > 0:
        return jnp.where(rows >= k, pltpu.roll(u, k, 0), 0.0)
    return jnp.where(rows < s + k, pltpu.roll(u, s + k, 0), 0.0)


def _conv_fwd(bcx, cw, nseq, seq):
    t, d3 = bcx.shape
    d = d3 // 3
    cb = MXU_COLS
    nj = d // cb

    def body(b_ref, c_ref, x_ref, cw_ref, z_ref):
        u = b_ref[...].astype(F32) * x_ref[...].astype(F32)
        rows = lax.broadcasted_iota(jnp.int32, u.shape, 0)
        cwv = cw_ref[...]
        y = cwv[2:3] * u + cwv[1:2] * _shift_rows(u, 1, rows) + cwv[0:1] * _shift_rows(u, 2, rows)
        z_ref[...] = (c_ref[...].astype(F32) * y).astype(BF16)

    return pl.pallas_call(
        body, name="conv_fwd", grid=(nseq, nj),
        in_specs=[pl.BlockSpec((seq, cb), lambda b, j: (b, j)),
                  pl.BlockSpec((seq, cb), lambda b, j: (b, nj + j)),
                  pl.BlockSpec((seq, cb), lambda b, j: (b, 2 * nj + j)),
                  pl.BlockSpec((3, cb), lambda b, j: (0, j))],
        out_specs=pl.BlockSpec((seq, cb), lambda b, j: (b, j)),
        out_shape=_sds((t, d), BF16),
        compiler_params=_params(("parallel", "parallel")))(bcx, bcx, bcx, cw)


def _conv_bwd(dz, bcx, cw, nseq, seq):
    t, d3 = bcx.shape
    d = d3 // 3
    cb = MXU_COLS
    nj = d // cb

    def body(dz_ref, b_ref, c_ref, x_ref, cw_ref, o_ref, dcw_ref):
        @pl.when(pl.program_id(1) == 0)
        def _():
            dcw_ref[...] = jnp.zeros_like(dcw_ref)

        b = b_ref[...].astype(F32)
        c = c_ref[...].astype(F32)
        xv = x_ref[...].astype(F32)
        dzv = dz_ref[...].astype(F32)
        u = b * xv
        rows = lax.broadcasted_iota(jnp.int32, u.shape, 0)
        u1 = _shift_rows(u, 1, rows)
        u2 = _shift_rows(u, 2, rows)
        cwv = cw_ref[...]
        y = cwv[2:3] * u + cwv[1:2] * u1 + cwv[0:1] * u2
        dyc = dzv * c
        du = cwv[2:3] * dyc + cwv[1:2] * _shift_rows(dyc, -1, rows) + cwv[0:1] * _shift_rows(dyc, -2, rows)
        o_ref[0] = (du * xv).astype(BF16)
        o_ref[1] = (dzv * y).astype(BF16)
        o_ref[2] = (du * b).astype(BF16)
        s0 = jnp.sum(dyc * u2, axis=0, keepdims=True)
        s1 = jnp.sum(dyc * u1, axis=0, keepdims=True)
        s2 = jnp.sum(dyc * u, axis=0, keepdims=True)
        tap = lax.broadcasted_iota(jnp.int32, (3, cb), 0)
        dcw_ref[...] += jnp.where(tap == 0, s0, jnp.where(tap == 1, s1, s2))

    return pl.pallas_call(
        body, name="conv_bwd", grid=(nj, nseq),
        in_specs=[pl.BlockSpec((seq, cb), lambda j, b: (b, j)),
                  pl.BlockSpec((seq, cb), lambda j, b: (b, j)),
                  pl.BlockSpec((seq, cb), lambda j, b: (b, nj + j)),
                  pl.BlockSpec((seq, cb), lambda j, b: (b, 2 * nj + j)),
                  pl.BlockSpec((3, cb), lambda j, b: (0, j))],
        out_specs=[pl.BlockSpec((3, seq, cb), lambda j, b: (0, b, j)),
                   pl.BlockSpec((3, cb), lambda j, b: (0, j))],
        out_shape=[_sds((3, t, d), BF16), _sds((3, d), F32)],
        compiler_params=_params(("parallel", "arbitrary")))(dz, bcx, bcx, bcx, cw)


def _pair_norm(x, gain_pair, low):
    sq = x * x
    ss_lo = jnp.sum(jnp.where(low, sq, 0.0), axis=-1, keepdims=True)
    ss_hi = jnp.sum(jnp.where(low, 0.0, sq), axis=-1, keepdims=True)
    r = lax.rsqrt(jnp.where(low, ss_lo, ss_hi) * (1.0 / HEAD_DIM) + EPS)
    xhat = x * r
    return xhat * gain_pair, xhat, r


KEYS = 2 * BLOCK
QK_SCALE = 1.0 / (HEAD_DIM ** 0.5)
N_PAIRS = N_Q_HEADS // 2


def _fill_bias(bias_ref):
    rows = lax.broadcasted_iota(jnp.int32, (2 * KEYS, BLOCK), 0)
    qi = lax.broadcasted_iota(jnp.int32, (2 * KEYS, BLOCK), 1)
    odd_head = rows >= KEYS
    kj = jnp.where(odd_head, rows - KEYS, rows)
    for later in range(2):
        dist = later * BLOCK + qi - kj
        mask = jnp.logical_and(dist >= 0, dist < WINDOW)
        distf = dist.astype(F32)
        for j in range(N_PAIRS):
            slope = jnp.where(odd_head, ALIBI_SLOPES[2 * j + 1], ALIBI_SLOPES[2 * j])
            bias_ref[later, j] = jnp.where(mask, -slope * distf, -1e30)


def _kv_pair_rows(kv_tile, parity, low):
    own = jnp.where(low if parity == 0 else jnp.logical_not(low), kv_tile, 0.0)
    other = pltpu.roll(own, HEAD_DIM, 1)
    lo, hi = (own, other) if parity == 0 else (other, own)
    return jnp.concatenate([lo, hi], axis=0).astype(BF16)


def _pair_softmax(s_t, sink_even, sink_odd):
    out = []
    for e, sink in enumerate((sink_even, sink_odd)):
        se = s_t[e * KEYS:(e + 1) * KEYS]
        m = jnp.maximum(jnp.max(se, axis=0, keepdims=True), sink)
        ee = jnp.exp(se - m)
        es = jnp.exp(sink - m)
        inv = 1.0 / (jnp.sum(ee, axis=0, keepdims=True) + es)
        out.append((ee * inv, es * inv))
    return out


def _attn_rows(n):
    q0 = pl.multiple_of(n * BLOCK, BLOCK)
    k0 = pl.multiple_of(jnp.maximum(n - 1, 0) * BLOCK, BLOCK)
    return q0, k0, jnp.minimum(n, 1)


def _attn_fwd(qkv, qg_pair, kg_pair, sinks, nseq, seq):
    t = qkv.shape[0]
    dq = N_Q_HEADS * HEAD_DIM
    dkv = N_KV_HEADS * HEAD_DIM

    def body(sk_ref, qkv_ref, qg_ref, kg_ref, o_ref, bias_ref):
        @pl.when(pl.program_id(0) == 0)
        def _():
            _fill_bias(bias_ref)

        low = lax.broadcasted_iota(jnp.int32, (1, LANES), 1) < HEAD_DIM
        qg = qg_ref[...] * QK_SCALE
        kg = kg_ref[...]

        def blk(n, carry):
            q0, k0, later = _attn_rows(n)
            for kt in range(dkv // LANES):
                kraw = qkv_ref[pl.ds(k0, KEYS), dq + kt * LANES:dq + (kt + 1) * LANES].astype(F32)
                vraw = qkv_ref[pl.ds(k0, KEYS), dq + dkv + kt * LANES:dq + dkv + (kt + 1) * LANES].astype(F32)
                kn, _, _ = _pair_norm(kraw, kg, low)
                for par in range(2):
                    kh = 2 * kt + par
                    k_pair = _kv_pair_rows(kn, par, low)
                    v_pair = _kv_pair_rows(vraw, par, low)
                    for jj in range(2):
                        j = 2 * kh + jj
                        qraw = qkv_ref[pl.ds(q0, BLOCK), j * LANES:(j + 1) * LANES].astype(F32)
                        qn, _, _ = _pair_norm(qraw, qg, low)
                        s_t = _dot(k_pair, qn.astype(BF16), NT) + bias_ref[later, j]
                        (p0, _), (p1, _) = _pair_softmax(s_t, sk_ref[0, 2 * j], sk_ref[0, 2 * j + 1])
                        p_t = jnp.concatenate([p0, p1], axis=0).astype(BF16)
                        o_ref[pl.ds(q0, BLOCK), j * LANES:(j + 1) * LANES] = _dot(p_t, v_pair, TN).astype(BF16)
            return carry

        lax.fori_loop(0, seq // BLOCK, blk, 0)

    return pl.pallas_call(
        body, name="attn_fwd", grid=(nseq,),
        in_specs=[pl.BlockSpec(memory_space=pltpu.SMEM),
                  pl.BlockSpec((seq, dq + 2 * dkv), lambda b: (b, 0)),
                  pl.BlockSpec((1, LANES), lambda b: (0, 0)),
                  pl.BlockSpec((1, LANES), lambda b: (0, 0))],
        out_specs=pl.BlockSpec((seq, dq), lambda b: (b, 0)),
        out_shape=_sds((t, dq), BF16),
        scratch_shapes=[pltpu.VMEM((2, N_PAIRS, 2 * KEYS, BLOCK), F32)],
        compiler_params=_params(("arbitrary",)))(sinks, qkv, qg_pair, kg_pair)


def _attn_bwd(do, qkv, qg_pair, kg_pair, sinks, nseq, seq):
    t = qkv.shape[0]
    dq = N_Q_HEADS * HEAD_DIM
    dkv = N_KV_HEADS * HEAD_DIM

    def body(sk_ref, do_ref, qkv_ref, qg_ref, kg_ref, o_ref, dqg_ref, dkg_ref, dsk_ref, acc_ref, bias_ref):
        @pl.when(pl.program_id(0) == 0)
        def _():
            _fill_bias(bias_ref)
            dqg_ref[...] = jnp.zeros_like(dqg_ref)
            dkg_ref[...] = jnp.zeros_like(dkg_ref)
            dsk_ref[...] = jnp.zeros_like(dsk_ref)

        acc_ref[...] = jnp.zeros_like(acc_ref)
        low = lax.broadcasted_iota(jnp.int32, (1, LANES), 1) < HEAD_DIM
        head_row = lax.broadcasted_iota(jnp.int32, (N_Q_HEADS, LANES), 0)
        qg = qg_ref[...] * QK_SCALE
        kg = kg_ref[...]

        def blk(n, carry):
            dqg_acc, dkg_acc, dsk_acc = carry
            q0, k0, later = _attn_rows(n)
            for kt in range(dkv // LANES):
                kraw = qkv_ref[pl.ds(k0, KEYS), dq + kt * LANES:dq + (kt + 1) * LANES].astype(F32)
                vraw = qkv_ref[pl.ds(k0, KEYS), dq + dkv + kt * LANES:dq + dkv + (kt + 1) * LANES].astype(F32)
                kn, khat, rk = _pair_norm(kraw, kg, low)
                dk_tile = None
                dv_tile = None
                for par in range(2):
                    kh = 2 * kt + par
                    own = low if par == 0 else jnp.logical_not(low)
                    k_pair = _kv_pair_rows(kn, par, low)
                    v_pair = _kv_pair_rows(vraw, par, low)
                    dkn_rows = jnp.zeros((2 * KEYS, LANES), F32)
                    dv_rows = jnp.zeros((2 * KEYS, LANES), F32)
                    for jj in range(2):
                        j = 2 * kh + jj
                        qraw = qkv_ref[pl.ds(q0, BLOCK), j * LANES:(j + 1) * LANES].astype(F32)
                        qn, qhat, rq = _pair_norm(qraw, qg, low)
                        qn_b = qn.astype(BF16)
                        do_b = do_ref[pl.ds(q0, BLOCK), j * LANES:(j + 1) * LANES]
                        s_t = _dot(k_pair, qn_b, NT) + bias_ref[later, j]
                        dp_t = _dot(v_pair, do_b, NT)
                        ds_halves = []
                        probs = _pair_softmax(s_t, sk_ref[0, 2 * j], sk_ref[0, 2 * j + 1])
                        for e, (p, ps) in enumerate(probs):
                            dp = dp_t[e * KEYS:(e + 1) * KEYS]
                            dsum = jnp.sum(p * dp, axis=0, keepdims=True)
                            ds_halves.append(p * (dp - dsum))
                            dsk_acc = dsk_acc - jnp.where(head_row == 2 * j + e, ps * dsum, 0.0)
                        p_t = jnp.concatenate([probs[0][0], probs[1][0]], axis=0).astype(BF16)
                        ds_t = jnp.concatenate(ds_halves, axis=0).astype(BF16)
                        dv_rows = dv_rows + _dot(p_t, do_b, NN)
                        dkn_rows = dkn_rows + _dot(ds_t, qn_b, NN)
                        dqn = _dot(ds_t, k_pair, TN)
                        dqg_acc = dqg_acc + jnp.sum(dqn * qhat, axis=0, keepdims=True)
                        dqhat = dqn * qg
                        prod = dqhat * qhat
                        m_lo = jnp.sum(jnp.where(low, prod, 0.0), axis=-1, keepdims=True)
                        m_hi = jnp.sum(jnp.where(low, 0.0, prod), axis=-1, keepdims=True)
                        mean = jnp.where(low, m_lo, m_hi) * (1.0 / HEAD_DIM)
                        o_ref[pl.ds(q0, BLOCK), j * LANES:(j + 1) * LANES] = (rq * (dqhat - qhat * mean)).astype(BF16)
                    dkn_acc = jnp.where(low, dkn_rows[0:KEYS], dkn_rows[KEYS:2 * KEYS])
                    dv_acc = jnp.where(low, dv_rows[0:KEYS], dv_rows[KEYS:2 * KEYS])
                    dkn = dkn_acc + pltpu.roll(dkn_acc, HEAD_DIM, 1)
                    dvh = dv_acc + pltpu.roll(dv_acc, HEAD_DIM, 1)
                    khat_own = jnp.where(own, khat, 0.0)
                    khat_dup = khat_own + pltpu.roll(khat_own, HEAD_DIM, 1)
                    dkg_acc = dkg_acc + jnp.sum(jnp.where(own, dkn * khat_dup, 0.0), axis=0, keepdims=True)
                    dkhat = dkn * kg
                    mean_k = jnp.sum(dkhat * khat_dup, axis=-1, keepdims=True) * (1.0 / LANES)
                    dk_raw = rk * (dkhat - khat_dup * mean_k)
                    dk_tile = jnp.where(own, dk_raw, 0.0) if dk_tile is None else jnp.where(own, dk_raw, dk_tile)
                    dv_tile = jnp.where(own, dvh, 0.0) if dv_tile is None else jnp.where(own, dvh, dv_tile)
                acc_ref[pl.ds(k0, KEYS), kt * LANES:(kt + 1) * LANES] += dk_tile
                acc_ref[pl.ds(k0, KEYS), dkv + kt * LANES:dkv + (kt + 1) * LANES] += dv_tile
            return dqg_acc, dkg_acc, dsk_acc

        zero = jnp.zeros((1, LANES), F32)
        carry = (zero, zero, jnp.zeros((N_Q_HEADS, LANES), F32))
        dqg_acc, dkg_acc, dsk_acc = lax.fori_loop(0, seq // BLOCK, blk, carry)
        dqg_ref[...] += dqg_acc * QK_SCALE
        dkg_ref[...] += dkg_acc
        dsk_ref[...] += dsk_acc
        o_ref[:, dq:dq + 2 * dkv] = acc_ref[...].astype(BF16)

    small = pl.BlockSpec((1, LANES), lambda b: (0, 0))
    heads = pl.BlockSpec((N_Q_HEADS, LANES), lambda b: (0, 0))
    return pl.pallas_call(
        body, name="attn_bwd", grid=(nseq,),
        in_specs=[pl.BlockSpec(memory_space=pltpu.SMEM),
                  pl.BlockSpec((seq, dq), lambda b: (b, 0)),
                  pl.BlockSpec((seq, dq + 2 * dkv), lambda b: (b, 0)),
                  small, small],
        out_specs=[pl.BlockSpec((seq, dq + 2 * dkv), lambda b: (b, 0)), small, small, heads],
        out_shape=[_sds((t, dq + 2 * dkv), BF16), _sds((1, LANES), F32), _sds((1, LANES), F32),
                   _sds((N_Q_HEADS, LANES), F32)],
        scratch_shapes=[pltpu.VMEM((seq, 2 * dkv), F32), pltpu.VMEM((2, N_PAIRS, 2 * KEYS, BLOCK), F32)],
        compiler_params=_params(("arbitrary",)))(sinks, do, qkv, qg_pair, kg_pair)


def _place():
    x, y, c = lax.axis_index("x"), lax.axis_index("y"), lax.axis_index("c")
    other_chips = [(1 - x, y), (x, 1 - y), (1 - x, 1 - y)]
    return x, y, c, other_chips


def _half_rows(c, rows):
    rh = rows // 2
    return pl.ds(pl.multiple_of(c * rh, BF16_ROWS), rh)


def _cast_own(name, w, place, layer=None):
    nl, r, cdim = w.shape
    first = 0
    if layer is not None:
        nl, first = 1, layer
    rt = _row_tile(r, 4 * cdim, ELEMENTWISE_BLOCK)

    def body(s_ref, w_ref, o_ref):
        o_ref[...] = w_ref[...].astype(BF16)

    grid_spec = pltpu.PrefetchScalarGridSpec(
        num_scalar_prefetch=1, grid=(nl, r // rt),
        in_specs=[pl.BlockSpec((None, rt, cdim), lambda l, i, s: (first + l, i, 0))],
        out_specs=pl.BlockSpec((None, None, rt, cdim), lambda l, i, s: (l, s[1], i, 0)))
    return pl.pallas_call(
        body, name=name, grid_spec=grid_spec, out_shape=_sds((nl, N_CHIPS, r, cdim), BF16),
        compiler_params=_params(("parallel", "parallel")))(place, w)


def _gather_protocol(outs, shapes, send_sems, recv_sems):
    n = len(outs)
    x, y, c, other_chips = _place()
    me_chip = 2 * x + y
    sibling = (x, y, 1 - c)

    def rows(u, chip, half):
        return outs[u].at[:, chip, _half_rows(half, shapes[u][2]), :]

    def copy(sem, part, to):
        return pltpu.make_async_remote_copy(src_ref=part, dst_ref=part, send_sem=send_sems.at[sem],
                                            recv_sem=recv_sems.at[sem], device_id=to, device_id_type=MESH)

    sends = []
    for u in range(n):
        for k, chip in enumerate(other_chips):
            cp = copy(6 * u + k, rows(u, me_chip, c), (*chip, c))
            cp.start()
            sends.append(cp)
    for u in range(n):
        for k, chip in enumerate(other_chips):
            got = rows(u, 2 * chip[0] + chip[1], c)
            copy(6 * u + k, got, (*chip, c)).wait_recv()
            cp = copy(6 * u + 3 + k, got, sibling)
            cp.start()
            sends.append(cp)
    for u in range(n):
        for k, chip in enumerate(other_chips):
            copy(6 * u + 3 + k, rows(u, 2 * chip[0] + chip[1], 1 - c), sibling).wait_recv()
    for cp in sends:
        cp.wait_send()


def _hbm_ref(a):
    return jax.new_ref(a, memory_space=pltpu.MemorySpace.HBM)


def _sibling_peer():
    x, y, c, _ = _place()
    return [(x, y, 1 - c)]


def _chip_peers():
    x, y, c, other_chips = _place()
    return [(*chip, c) for chip in other_chips]


def _gather_peers():
    return _chip_peers() + _sibling_peer()


def _on_sequencer(name, collective_id, n_sems, peers, protocol, operands=(), out_types=()):
    n_in, n_out = len(operands), len(out_types)

    def launch(*refs):
        send_sems, recv_sems = refs[n_in + n_out:]
        barrier = pltpu.get_barrier_semaphore()
        targets = peers()
        for peer in targets:
            pl.semaphore_signal(barrier, inc=1, device_id=peer, device_id_type=MESH)
        pl.semaphore_wait(barrier, len(targets))
        protocol(refs[:n_in], refs[n_in:n_in + n_out], send_sems, recv_sems)

    return pl.kernel(
        launch, out_type=tuple(out_types), mesh=plsc.ScalarSubcoreMesh(axis_name="sequencer", num_cores=1), name=name,
        scratch_types=(pltpu.SemaphoreType.DMA((n_sems,)), pltpu.SemaphoreType.DMA((n_sems,))),
        compiler_params=pltpu.CompilerParams(collective_id=collective_id))(*operands)


def _seq_allgather(name, collective_id, bufs):
    shapes = [b.shape for b in bufs]
    refs = [_hbm_ref(b) for b in bufs]
    _on_sequencer(name, collective_id, 6 * len(bufs), _gather_peers,
                  lambda ins, outs, send_sems, recv_sems: _gather_protocol(refs, shapes, send_sems, recv_sems))
    return [r[...] for r in refs]


def _exchange_protocol(gs, outs, shapes, send_sems, recv_sems):
    x, y, c, _ = _place()
    sends = []
    for u in range(len(gs)):
        cp = pltpu.make_async_remote_copy(
            src_ref=gs[u].at[:, _half_rows(1 - c, shapes[u][1]), :], dst_ref=outs[u],
            send_sem=send_sems.at[u], recv_sem=recv_sems.at[u], device_id=(x, y, 1 - c), device_id_type=MESH)
        cp.start()
        sends.append(cp)
    for cp in sends:
        cp.wait_recv()
    for cp in sends:
        cp.wait_send()


def _seq_exchange(name, collective_id, grads):
    shapes = [g.shape for g in grads]
    return _on_sequencer(
        name, collective_id, len(grads), _sibling_peer,
        lambda gs, outs, send_sems, recv_sems: _exchange_protocol(gs, outs, shapes, send_sems, recv_sems),
        operands=grads, out_types=[_sds((s[0], s[1] // 2, s[2]), F32) for s in shapes])


def _sum_halves(name, g, got, place, after):
    _, r, cdim = g.shape
    rh = r // 2
    rt = _row_tile(rh, 4 * cdim, ELEMENTWISE_BLOCK)
    nr = rh // rt

    def body(s_ref, g_ref, got_ref, after_ref, pb_ref, pf_ref):
        s = g_ref[...] + got_ref[...]
        pb_ref[...] = s.astype(BF16)

        @pl.when(pl.program_id(1) == s_ref[1])
        def _():
            pf_ref[...] = s

    grid_spec = pltpu.PrefetchScalarGridSpec(
        num_scalar_prefetch=1, grid=(nr, N_CHIPS),
        in_specs=[pl.BlockSpec((None, rt, cdim), lambda i, q, s: (q, s[0] * nr + i, 0)),
                  pl.BlockSpec((None, rt, cdim), lambda i, q, s: (q, i, 0)),
                  pl.BlockSpec(memory_space=pl.ANY)],
        out_specs=[pl.BlockSpec((None, rt, cdim), lambda i, q, s: (q, i, 0)),
                   pl.BlockSpec((rt, cdim), lambda i, q, s: (i, 0))])
    return pl.pallas_call(
        body, name=name, grid_spec=grid_spec,
        out_shape=[_sds((N_CHIPS, rh, cdim), BF16), _sds((rh, cdim), F32)],
        compiler_params=_params(("parallel", "arbitrary")))(place, g, got, after)


def _scatter_protocol(ps, outs, send_sems, recv_sems):
    x, y, c, other_chips = _place()
    sends = []
    for u in range(len(ps)):
        for k, chip in enumerate(other_chips):
            cp = pltpu.make_async_remote_copy(
                src_ref=ps[u].at[2 * chip[0] + chip[1]], dst_ref=outs[u].at[k],
                send_sem=send_sems.at[3 * u + k], recv_sem=recv_sems.at[3 * u + k],
                device_id=(*chip, c), device_id_type=MESH)
            cp.start()
            sends.append(cp)
    for cp in sends:
        cp.wait_recv()
    for cp in sends:
        cp.wait_send()


def _seq_scatter(name, collective_id, partials):
    return _on_sequencer(
        name, collective_id, 3 * len(partials), _chip_peers, _scatter_protocol,
        operands=partials, out_types=[_sds((3, p.shape[1], p.shape[2]), BF16) for p in partials])


def _sum_partials(name, own, got, place, layer, nl, prev, after):
    rh, cdim = own.shape
    rt = _row_tile(rh, 4 * cdim, ELEMENTWISE_BLOCK)
    nr = rh // rt

    def body(s_ref, own_ref, got_ref, *rest):
        o_ref = rest[-1]
        o_ref[...] = ((own_ref[...] + got_ref[0].astype(F32)) + got_ref[1].astype(F32)) + got_ref[2].astype(F32)

    in_specs = [pl.BlockSpec((rt, cdim), lambda i, s: (i, 0)), pl.BlockSpec((3, rt, cdim), lambda i, s: (0, i, 0)),
                pl.BlockSpec(memory_space=pl.ANY)]
    args = [place, own, got, after]
    aliases = {}
    if prev is not None:
        in_specs.append(pl.BlockSpec(memory_space=pl.ANY))
        args.append(prev)
        aliases = {4: 0}
    grid_spec = pltpu.PrefetchScalarGridSpec(
        num_scalar_prefetch=1, grid=(nr,), in_specs=in_specs,
        out_specs=pl.BlockSpec((None, rt, cdim), lambda i, s: (layer, s[0] * nr + i, 0)))
    return pl.pallas_call(
        body, name=name, grid_spec=grid_spec, out_shape=_sds((nl, 2 * rh, cdim), F32),
        input_output_aliases=aliases, compiler_params=_params(("parallel",)))(*args)


def _share_protocol(outs, shapes, units, send_sems, recv_sems):
    x, y, c, _ = _place()
    sends = []
    for u, (w, l) in enumerate(units):
        mine = outs[w].at[l, _half_rows(c, shapes[w][1]), :]
        cp = pltpu.make_async_remote_copy(src_ref=mine, dst_ref=mine, send_sem=send_sems.at[u],
                                          recv_sem=recv_sems.at[u], device_id=(x, y, 1 - c), device_id_type=MESH)
        cp.start()
        sends.append(cp)
    for u, (w, l) in enumerate(units):
        theirs = outs[w].at[l, _half_rows(1 - c, shapes[w][1]), :]
        pltpu.make_async_remote_copy(src_ref=theirs, dst_ref=theirs, send_sem=send_sems.at[u],
                                     recv_sem=recv_sems.at[u], device_id=(x, y, 1 - c),
                                     device_id_type=MESH).wait_recv()
    for cp in sends:
        cp.wait_send()


def _seq_share(name, collective_id, bufs):
    shapes = [b.shape for b in bufs]
    units = [(w, l) for w in range(len(bufs)) for l in range(shapes[w][0])]
    refs = [_hbm_ref(b) for b in bufs]
    _on_sequencer(name, collective_id, len(units), _sibling_peer,
                  lambda ins, outs, send_sems, recv_sems: _share_protocol(refs, shapes, units, send_sems, recv_sems))
    return [r[...] for r in refs]


def _gather_blocks(block_ref, all_ref, send_sems, recv_sems):
    x, y, c, _ = _place()
    me = 4 * x + 2 * y + c
    all_ref[me] = block_ref[...]
    sends = []
    for rel in range(1, 8):
        fx, fy, fc = (rel >> 2) & 1, (rel >> 1) & 1, rel & 1
        peer = (x ^ fx, y ^ fy, c ^ fc)
        cp = pltpu.make_async_remote_copy(src_ref=block_ref, dst_ref=all_ref.at[me], send_sem=send_sems.at[rel - 1],
                                          recv_sem=recv_sems.at[rel - 1], device_id=peer, device_id_type=MESH)
        cp.start()
        sends.append(cp)
    for cp in sends:
        cp.wait_recv()
    for cp in sends:
        cp.wait_send()


def _gather_conv_w(cw_block):
    r, d = cw_block.shape

    def body(b_ref, o_ref, all_ref, send_sems, recv_sems):
        _gather_blocks(b_ref, all_ref, send_sems, recv_sems)
        o_ref[...] = (all_ref[0] + all_ref[2]) + (all_ref[4] + all_ref[6])

    vm = pl.BlockSpec(memory_space=pltpu.VMEM)
    return pl.pallas_call(
        body, name="gather_conv_w", in_specs=[vm], out_specs=vm, out_shape=_sds((r, d), F32),
        scratch_shapes=[pltpu.VMEM((8, r, d), F32), pltpu.SemaphoreType.DMA((7,)), pltpu.SemaphoreType.DMA((7,))],
    )(cw_block)


def _adam(w, g, m, v):
    m_new = ADAM_B1 * m + (1.0 - ADAM_B1) * g
    v_new = ADAM_B2 * v + (1.0 - ADAM_B2) * (g * g)
    m_hat = m_new / (1.0 - ADAM_B1 ** ADAM_STEP)
    v_hat = v_new / (1.0 - ADAM_B2 ** ADAM_STEP)
    delta = -ADAM_LR * (m_hat / (jnp.sqrt(v_hat) + ADAM_EPS) + ADAM_WD * w)
    return delta, m_new, v_new


def _small_step(dnm0, dnm1, dnf0, dnf1, dcw, dqg, dkg, dsk, loss, w_blk, m_blk, v_blk):
    d = w_blk.shape[1]

    def body(dnm0_ref, dnm1_ref, dnf0_ref, dnf1_ref, dcw_ref, dqg_ref, dkg_ref, dsk_ref, loss_ref,
             w_ref, m_ref, v_ref, g_ref, dl_ref, mo_ref, vo_ref, blk_ref, all_ref, send_sems, recv_sems):
        blk_ref[...] = jnp.zeros_like(blk_ref)
        blk_ref[0:1, :] = jnp.sum(dnm0_ref[...], axis=0, keepdims=True)
        blk_ref[1:2, :] = jnp.sum(dnm1_ref[...], axis=0, keepdims=True)
        blk_ref[8:9, :] = jnp.sum(dnf0_ref[...], axis=0, keepdims=True)
        blk_ref[9:10, :] = jnp.sum(dnf1_ref[...], axis=0, keepdims=True)
        blk_ref[16:19, :] = dcw_ref[...]
        dqg_v = dqg_ref[...]
        dkg_v = dkg_ref[...]
        blk_ref[24:25, 0:LANES] = dqg_v + pltpu.roll(dqg_v, HEAD_DIM, 1)
        blk_ref[24:25, LANES:2 * LANES] = dkg_v + pltpu.roll(dkg_v, HEAD_DIM, 1)
        for h in range(N_Q_HEADS):
            blk_ref[24:25, 2 * LANES + h:2 * LANES + h + 1] = jnp.sum(dsk_ref[h:h + 1, :], axis=1, keepdims=True)
        blk_ref[24:25, 3 * LANES:4 * LANES] = jnp.broadcast_to(loss_ref[...], (1, LANES))
        _gather_blocks(blk_ref, all_ref, send_sems, recv_sems)
        g = all_ref[0]
        for dev in range(1, 8):
            g = g + all_ref[dev]
        g_ref[...] = g
        delta, m_new, v_new = _adam(w_ref[...], g, m_ref[...], v_ref[...])
        dl_ref[...] = delta
        mo_ref[...] = m_new
        vo_ref[...] = v_new

    vm = pl.BlockSpec(memory_space=pltpu.VMEM)
    blk = _sds((SMALL_ROWS, d), F32)
    return pl.pallas_call(
        body, name="small_step", in_specs=[vm] * 12, out_specs=[vm] * 4, out_shape=[blk] * 4,
        scratch_shapes=[pltpu.VMEM((SMALL_ROWS, d), F32), pltpu.VMEM((8, SMALL_ROWS, d), F32),
                        pltpu.SemaphoreType.DMA((7,)), pltpu.SemaphoreType.DMA((7,))],
    )(dnm0, dnm1, dnf0, dnf1, dcw, dqg, dkg, dsk, loss, w_blk, m_blk, v_blk)


def _adam_step(name, w, g, m, v):
    nl, r, cdim = w.shape
    rt = _row_tile(r, 4 * cdim, ELEMENTWISE_BLOCK)

    def body(w_ref, g_ref, m_ref, v_ref, d_ref, mo_ref, vo_ref):
        delta, m_new, v_new = _adam(w_ref[...], g_ref[...], m_ref[...], v_ref[...])
        d_ref[...] = delta
        mo_ref[...] = m_new
        vo_ref[...] = v_new

    spec = pl.BlockSpec((None, rt, cdim), lambda l, i: (l, i, 0))
    return pl.pallas_call(
        body, name=name, grid=(nl, r // rt), in_specs=[spec] * 4, out_specs=[spec] * 3,
        out_shape=[_sds(w.shape, F32)] * 3,
        compiler_params=_params(("parallel", "parallel")))(w, g, m, v)


def _pad_rows(a, rows=SUBLANES):
    return jnp.pad(a, ((0, rows - a.shape[0]), (0, 0)))


def _small_block(nm, nf, cw_local, qg, kg, sk, chip):
    d = nm.shape[1]
    cw_rows = lax.dynamic_update_slice(jnp.zeros((SUBLANES, d), F32), cw_local, (0, chip * cw_local.shape[1]))
    misc = jnp.concatenate([qg, qg, kg, kg, jnp.pad(sk, ((0, 0), (0, LANES - sk.shape[1]))),
                            jnp.zeros((1, d - 3 * LANES), F32)], axis=1)
    return jnp.concatenate([_pad_rows(nm), _pad_rows(nf), cw_rows, _pad_rows(misc)], axis=0)


def _unpack_small(blk, chip, cw_cols):
    cw = lax.dynamic_slice(blk[16:19], (0, chip * cw_cols), (3, cw_cols))[None]
    return dict(norm_mixer=blk[0:2], norm_ffn=blk[8:10], conv_w=cw, attn_q_gain=blk[24:25, 0:HEAD_DIM],
                attn_k_gain=blk[24:25, LANES:LANES + HEAD_DIM], attn_sinks=blk[24:25, 2 * LANES:2 * LANES + N_Q_HEADS])


WEIGHT_NAMES = ("conv_w_in", "conv_w", "conv_w_out", "attn_w_qkv", "attn_q_gain", "attn_k_gain", "attn_sinks",
                "attn_w_o", "norm_mixer", "norm_ffn", "ffn_w_gate_up", "ffn_w_down")
BIG = ("conv_w_in", "conv_w_out", "attn_w_qkv", "attn_w_o", "ffn_w_gate_up", "ffn_w_down")


def kernel(x, conv_w_in, conv_w, conv_w_out, attn_w_qkv, attn_q_gain, attn_k_gain, attn_sinks, attn_w_o, norm_mixer, norm_ffn, ffn_w_gate_up, ffn_w_down, loss_target, m_conv_w_in, m_conv_w, m_conv_w_out, m_attn_w_qkv, m_attn_q_gain, m_attn_k_gain, m_attn_sinks, m_attn_w_o, m_norm_mixer, m_norm_ffn, m_ffn_w_gate_up, m_ffn_w_down, v_conv_w_in, v_conv_w, v_conv_w_out, v_attn_w_qkv, v_attn_q_gain, v_attn_k_gain, v_attn_sinks, v_attn_w_o, v_norm_mixer, v_norm_ffn, v_ffn_w_gate_up, v_ffn_w_down):
    w = dict(conv_w_in=conv_w_in, conv_w=conv_w, conv_w_out=conv_w_out, attn_w_qkv=attn_w_qkv,
             attn_q_gain=attn_q_gain, attn_k_gain=attn_k_gain, attn_sinks=attn_sinks, attn_w_o=attn_w_o,
             norm_mixer=norm_mixer, norm_ffn=norm_ffn, ffn_w_gate_up=ffn_w_gate_up, ffn_w_down=ffn_w_down)
    m = dict(conv_w_in=m_conv_w_in, conv_w=m_conv_w, conv_w_out=m_conv_w_out, attn_w_qkv=m_attn_w_qkv,
             attn_q_gain=m_attn_q_gain, attn_k_gain=m_attn_k_gain, attn_sinks=m_attn_sinks, attn_w_o=m_attn_w_o,
             norm_mixer=m_norm_mixer, norm_ffn=m_norm_ffn, ffn_w_gate_up=m_ffn_w_gate_up, ffn_w_down=m_ffn_w_down)
    v = dict(conv_w_in=v_conv_w_in, conv_w=v_conv_w, conv_w_out=v_conv_w_out, attn_w_qkv=v_attn_w_qkv,
             attn_q_gain=v_attn_q_gain, attn_k_gain=v_attn_k_gain, attn_sinks=v_attn_sinks, attn_w_o=v_attn_w_o,
             norm_mixer=v_norm_mixer, norm_ffn=v_norm_ffn, ffn_w_gate_up=v_ffn_w_gate_up, ffn_w_down=v_ffn_w_down)

    nseq, seq, d = x.shape
    t = nseq * seq
    chip = 2 * lax.axis_index("x") + lax.axis_index("y")
    core = lax.axis_index("c")
    place = jnp.stack([core, chip]).astype(jnp.int32)
    x0 = x.reshape(t, d)
    tgt = loss_target.reshape(t, d)

    cw_block = lax.dynamic_update_slice(jnp.zeros((SUBLANES, d), F32), conv_w[0], (0, chip * conv_w.shape[2]))
    cw_full = _gather_conv_w(cw_block)[0:3]
    def cast(k, layer=None):
        return _cast_own(f"cast_{k}" + ("" if layer is None else str(layer)), w[k], place, layer)

    w_in, w_out = _seq_allgather("allgather_conv", 1, [cast("conv_w_in"), cast("conv_w_out")])
    w_gu0, w_dn0 = _seq_allgather("allgather_ffn0", 2, [cast("ffn_w_gate_up", 0), cast("ffn_w_down", 0)])
    w_qkv, w_o, w_gu1, w_dn1 = _seq_allgather(
        "allgather_rest", 3, [cast("attn_w_qkv"), cast("attn_w_o"), cast("ffn_w_gate_up", 1), cast("ffn_w_down", 1)])
    w_out = w_out.reshape(1, d, d)
    w_o = w_o.reshape(1, d, d)
    w_gu = [w_gu0, w_gu1]
    w_dn = [w_dn0.reshape(1, D_FF, d), w_dn1.reshape(1, D_FF, d)]

    qg_pair = jnp.concatenate([attn_q_gain, attn_q_gain], axis=1)
    kg_pair = jnp.concatenate([attn_k_gain, attn_k_gain], axis=1)

    def ffn_bwd(i, dxo, xin, h, g, u, a):
        g_dn = _wgrad_down(f"ffn{i}_down_wgrad", a, dxo, D_FF // 2)
        dg, du = _mm_down_t_swiglu(f"ffn{i}_down_dgrad", dxo, w_dn[i], 0, g, u)
        g_gu = _wgrad_up2(f"ffn{i}_up_wgrad", h, dg, du)
        dxi, dgain = _dgrad_norm_ffn(f"ffn{i}_up_dgrad", dg, du, w_gu[i], 0, xin, norm_ffn[i:i + 1], dxo)
        return dxi, dgain, g_gu, g_dn

    h0, bcx = _mm_norm_up_joined("conv_in", x0, norm_mixer[0:1], w_in, 512)
    z = _conv_fwd(bcx, cw_full, nseq, seq)
    x1, h1 = _mm_down_norm("conv_out", z, w_out, 0, x0, norm_ffn[0:1])
    g0, u0, a0 = _mm_up_swiglu("ffn0_up", h1, w_gu[0], 0)
    x2, h2 = _mm_down_norm("ffn0_down", a0, w_dn[0], 0, x1, norm_mixer[1:2])
    qkv = _mm_up_joined("attn_qkv", h2, w_qkv, 1024)
    o = _attn_fwd(qkv, qg_pair, kg_pair, attn_sinks, nseq, seq)
    x3, h3 = _mm_down_norm("attn_out", o, w_o, 0, x2, norm_ffn[1:2])
    g1, u1, a1 = _mm_up_swiglu("ffn1_up", h3, w_gu[1], 0)
    dy, loss_part = _mm_down_loss("ffn1_down", a1, w_dn[1], 0, x3, tgt)

    finished = {k: None for k in BIG}

    def exchange(tag, cid, units):
        return units, _seq_exchange(f"exchange_{tag}", cid, [g for _, _, g in units])

    def scatter(tag, cid, group, after):
        units, got = group
        sums = [_sum_halves(f"sum_halves_{k}{l}", g, r, place, after) for (k, l, g), r in zip(units, got)]
        return units, sums, _seq_scatter(f"scatter_{tag}", cid, [pb for pb, _ in sums])

    def finish(group, after):
        units, sums, arrived = group
        for (k, l, _), (_, pf), r in zip(units, sums, arrived):
            finished[k] = _sum_partials(f"sum_partials_{k}{l}", pf, r, place, l, w[k].shape[0], finished[k], after)

    dx3, dnf1, g_gu1, g_dn1 = ffn_bwd(1, dy, x3, h3, g1, u1, a1)
    ffn1 = exchange("ffn1", 4, [("ffn_w_down", 1, g_dn1), ("ffn_w_gate_up", 1, g_gu1)])
    g_o = _wgrad_down("attn_out_wgrad", o, dx3, d)
    do = _mm_down_t("attn_out_dgrad", dx3, w_o, 0)
    ffn1 = scatter("ffn1", 8, ffn1, do)
    dqkv, dqg, dkg, dsk = _attn_bwd(do, qkv, qg_pair, kg_pair, attn_sinks, nseq, seq)
    g_qkv = _wgrad_joined("attn_qkv_wgrad", h2, dqkv)
    attn = exchange("attn", 5, [("attn_w_o", 0, g_o), ("attn_w_qkv", 0, g_qkv)])
    dx2, dnm1 = _dgrad_norm_qkv("attn_qkv_dgrad", dqkv, w_qkv, x2, norm_mixer[1:2], dx3)
    finish(ffn1, dx2)
    attn = scatter("attn", 9, attn, dx2)
    dx1, dnf0, g_gu0, g_dn0 = ffn_bwd(0, dx2, x1, h1, g0, u0, a0)
    ffn0 = exchange("ffn0", 6, [("ffn_w_down", 0, g_dn0), ("ffn_w_gate_up", 0, g_gu0)])
    g_out = _wgrad_down("conv_out_wgrad", z, dx1, d)
    dz = _mm_down_t("conv_out_dgrad", dx1, w_out, 0)
    finish(attn, dz)
    ffn0 = scatter("ffn0", 10, ffn0, dz)
    dbcx, dcw = _conv_bwd(dz, bcx, cw_full, nseq, seq)
    g_in = _wgrad_conv_in("conv_in_wgrad", h0, dbcx, conv_w_in.shape[2])
    conv = exchange("conv", 7, [("conv_w_out", 0, g_out), ("conv_w_in", 0, g_in)])
    dx0, dnm0 = _dgrad_norm_conv("conv_in_dgrad", dbcx, w_in, x0, norm_mixer[0:1], dx1)
    finish(ffn0, dx0)
    late = ("attn_w_qkv", "attn_w_o", "ffn_w_gate_up", "ffn_w_down")
    grads_late = _seq_share("share_late", 12, [finished[k] for k in late])
    conv = scatter("conv", 11, conv, dx0)

    grad, delta, new_m, new_v = {}, {}, {}, {}

    def adam(k, g):
        grad[k] = g
        delta[k], new_m[k], new_v[k] = _adam_step(f"adam_{k}", w[k], g, m[k], v[k])

    for k, g in zip(late, grads_late):
        adam(k, g)

    def blocks(src):
        return _small_block(src["norm_mixer"], src["norm_ffn"], src["conv_w"][0], src["attn_q_gain"],
                            src["attn_k_gain"], src["attn_sinks"], chip)

    g_blk, d_blk, m_blk, v_blk = _small_step(dnm0, dnm1, dnf0, dnf1, dcw, dqg, dkg, dsk, loss_part,
                                             blocks(w), blocks(m), blocks(v))

    done = sum(new_v[k][0, 0:1, 0:1] for k in late) + v_blk[0:1, 0:1]
    finish(conv, done)
    last = ("conv_w_in", "conv_w_out")
    for k, g in zip(last, _seq_share("share_last", 13, [finished[k] for k in last])):
        adam(k, g)

    cw_cols = conv_w.shape[2]
    for dst, blk in ((grad, g_blk), (delta, d_blk), (new_m, m_blk), (new_v, v_blk)):
        dst.update(_unpack_small(blk, chip, cw_cols))
    loss = g_blk[24, 3 * LANES]

    return (loss, dx0.reshape(nseq, seq, d), *[grad[k] for k in WEIGHT_NAMES], *[delta[k] for k in WEIGHT_NAMES],
            *[new_m[k] for k in WEIGHT_NAMES], *[new_v[k] for k in WEIGHT_NAMES])
```

```python
import jax
import jax.numpy as jnp
from jax import lax
from jax.experimental import pallas as pl
from jax.experimental.pallas import tpu as pltpu
from jax.experimental.pallas import tpu_sc as plsc

F32 = jnp.float32
BF16 = jnp.bfloat16

D_MODEL = 1024
D_FF = 2816
N_Q_HEADS = 16
N_KV_HEADS = 4
HEAD_DIM = 64
WINDOW = 128
BLOCK = 128
EPS = 1e-6
N_CHIPS = 4
LANES = 128
SUBLANES = 8
BF16_ROWS = 16
MXU_COLS = 256
VMEM_LIMIT = 48 * 1024 * 1024
ADAM_LR, ADAM_B1, ADAM_B2, ADAM_EPS, ADAM_WD, ADAM_STEP = 0.001, 0.9, 0.999, 1e-08, 0.01, 10
ALIBI_SLOPES = tuple(2.0 ** (-8.0 * (h + 1) / N_Q_HEADS) for h in range(N_Q_HEADS))
SMALL_ROWS = 32
MESH = pl.DeviceIdType.MESH

NN = ((1,), (0,))
NT = ((1,), (1,))
TN = ((0,), (0,))


def _dot(a, b, dims):
    return lax.dot_general(a, b, (dims, ((), ())), preferred_element_type=F32)


def _pick(n, cands):
    for c in cands:
        if n % c == 0:
            return c
    raise ValueError((n, cands))


def _row_tile(rows, row_bytes, cap_bytes):
    fits = [r for r in range(BF16_ROWS, rows + 1, BF16_ROWS) if rows % r == 0 and r * row_bytes <= cap_bytes]
    if not fits:
        raise ValueError((rows, row_bytes, cap_bytes))
    return fits[-1]


ELEMENTWISE_BLOCK = 3 << 19


def _resident(block_shape, index_map):
    return pl.BlockSpec(block_shape, index_map, pipeline_mode=pl.Buffered(1))


def _params(sem):
    return pltpu.CompilerParams(dimension_semantics=sem, vmem_limit_bytes=VMEM_LIMIT)


def _sds(shape, dtype):
    return jax.ShapeDtypeStruct(shape, dtype)


def _rms(xv):
    return lax.rsqrt(jnp.mean(xv * xv, axis=-1, keepdims=True) + EPS)


def _sigmoid(g):
    return 1.0 / (1.0 + jnp.exp(-g))


def _mm_up_joined(name, a, w4, tm_pref):
    t, k = a.shape
    _, _, _, nq = w4.shape
    tm = _pick(t, (tm_pref, 256, 128))

    def body(a_ref, w_ref, o_ref, wcat_ref):
        @pl.when(pl.program_id(0) == 0)
        def _():
            for q in range(N_CHIPS):
                wcat_ref[:, q * nq:(q + 1) * nq] = w_ref[q]

        o_ref[...] = _dot(a_ref[...], wcat_ref[...], NN).astype(BF16)

    return pl.pallas_call(
        body, name=name, grid=(t // tm,),
        in_specs=[pl.BlockSpec((tm, k), lambda i: (i, 0)),
                  pl.BlockSpec((None, N_CHIPS, k, nq), lambda i: (0, 0, 0, 0))],
        out_specs=pl.BlockSpec((tm, N_CHIPS * nq), lambda i: (i, 0)),
        out_shape=_sds((t, N_CHIPS * nq), BF16),
        scratch_shapes=[pltpu.VMEM((k, N_CHIPS * nq), BF16)],
        compiler_params=_params(("arbitrary",)))(a, w4)


def _mm_norm_up_joined(name, x, gain, w4, tm_pref):
    t, k = x.shape
    _, _, _, nq = w4.shape
    tm = _pick(t, (tm_pref, 256, 128))

    def body(x_ref, g_ref, w_ref, h_ref, o_ref, wcat_ref):
        @pl.when(pl.program_id(0) == 0)
        def _():
            for q in range(N_CHIPS):
                wcat_ref[:, q * nq:(q + 1) * nq] = w_ref[q]

        xv = x_ref[...]
        h = ((xv * _rms(xv)) * g_ref[...]).astype(BF16)
        h_ref[...] = h
        o_ref[...] = _dot(h, wcat_ref[...], NN).astype(BF16)

    return pl.pallas_call(
        body, name=name, grid=(t // tm,),
        in_specs=[pl.BlockSpec((tm, k), lambda i: (i, 0)), pl.BlockSpec((1, k), lambda i: (0, 0)),
                  _resident((None, N_CHIPS, k, nq), lambda i: (0, 0, 0, 0))],
        out_specs=[pl.BlockSpec((tm, k), lambda i: (i, 0)), pl.BlockSpec((tm, N_CHIPS * nq), lambda i: (i, 0))],
        out_shape=[_sds((t, k), BF16), _sds((t, N_CHIPS * nq), BF16)],
        scratch_shapes=[pltpu.VMEM((k, N_CHIPS * nq), BF16)],
        compiler_params=_params(("arbitrary",)))(x, gain, w4)


def _mm_up_swiglu(name, h, w4, layer):
    t, k = h.shape
    _, _, _, nq = w4.shape
    tm = _pick(t, (512, 256, 128))

    def body(h_ref, wg_ref, wu_ref, dag_ref, dau_ref, a_ref):
        hv = h_ref[...]
        g = _dot(hv, wg_ref[...], NN)
        u = _dot(hv, wu_ref[...], NN)
        sg = _sigmoid(g)
        silu = g * sg
        dag_ref[...] = (u * (sg * (1.0 + g * (1.0 - sg)))).astype(BF16)
        dau_ref[...] = silu.astype(BF16)
        a_ref[...] = (silu * u).astype(BF16)

    half = N_CHIPS // 2
    out = pl.BlockSpec((tm, nq), lambda j, i: (i, j))
    return pl.pallas_call(
        body, name=name, grid=(half, t // tm),
        in_specs=[pl.BlockSpec((tm, k), lambda j, i: (i, 0)),
                  pl.BlockSpec((None, None, k, nq), lambda j, i: (layer, j, 0, 0)),
                  pl.BlockSpec((None, None, k, nq), lambda j, i: (layer, half + j, 0, 0))],
        out_specs=[out, out, out],
        out_shape=[_sds((t, half * nq), BF16)] * 3,
        compiler_params=_params(("parallel", "parallel")))(h, w4, w4)


def _mm_down_norm(name, a, w, layer, res, gain):
    t, kf = a.shape
    _, _, n = w.shape
    tm = _pick(t, (1024, 512, 256, 128))

    def body(a_ref, w_ref, r_ref, g_ref, o_ref, h_ref):
        xo = r_ref[...] + _dot(a_ref[...], w_ref[...], NN)
        o_ref[...] = xo
        h_ref[...] = ((xo * _rms(xo)) * g_ref[...]).astype(BF16)

    row = pl.BlockSpec((tm, n), lambda i: (i, 0))
    return pl.pallas_call(
        body, name=name, grid=(t // tm,),
        in_specs=[pl.BlockSpec((tm, kf), lambda i: (i, 0)),
                  _resident((None, kf, n), lambda i: (layer, 0, 0)),
                  row, pl.BlockSpec((1, n), lambda i: (0, 0))],
        out_specs=[row, row],
        out_shape=[_sds((t, n), F32), _sds((t, n), BF16)],
        compiler_params=_params(("parallel",)))(a, w, res, gain)


def _mm_down_loss(name, a, w, layer, res, tgt):
    t, kf = a.shape
    _, _, n = w.shape
    tm = _pick(t, (1024, 512, 256, 128))
    steps = t // tm

    def body(a_ref, w_ref, r_ref, t_ref, dy_ref, l_ref, acc_ref):
        i = pl.program_id(0)

        @pl.when(i == 0)
        def _():
            acc_ref[...] = jnp.zeros_like(acc_ref)

        e = (r_ref[...] + _dot(a_ref[...], w_ref[...], NN)) - t_ref[...]
        dy_ref[...] = e * (1.0 / n)
        acc_ref[...] += (e * e).reshape(tm // SUBLANES, SUBLANES, n).sum(axis=0)

        @pl.when(i == steps - 1)
        def _():
            l_ref[...] = jnp.sum(acc_ref[...], keepdims=True) * (0.5 / n)

    row = pl.BlockSpec((tm, n), lambda i: (i, 0))
    return pl.pallas_call(
        body, name=name, grid=(steps,),
        in_specs=[pl.BlockSpec((tm, kf), lambda i: (i, 0)),
                  _resident((None, kf, n), lambda i: (layer, 0, 0)), row, row],
        out_specs=[row, pl.BlockSpec((1, 1), lambda i: (0, 0))],
        out_shape=[_sds((t, n), F32), _sds((1, 1), F32)],
        scratch_shapes=[pltpu.VMEM((SUBLANES, n), F32)],
        compiler_params=_params(("arbitrary",)))(a, w, res, tgt)


def _mm_down_t(name, dx, w, layer):
    t, n = dx.shape
    _, kf, _ = w.shape
    tm = _pick(t, (512, 256, 128))

    def body(a_ref, w_ref, o_ref):
        o_ref[...] = _dot(a_ref[...].astype(BF16), w_ref[...], NT).astype(BF16)

    return pl.pallas_call(
        body, name=name, grid=(t // tm,),
        in_specs=[pl.BlockSpec((tm, n), lambda i: (i, 0)),
                  pl.BlockSpec((None, kf, n), lambda i: (layer, 0, 0))],
        out_specs=pl.BlockSpec((tm, kf), lambda i: (i, 0)),
        out_shape=_sds((t, kf), BF16),
        compiler_params=_params(("parallel",)))(dx, w)


def _mm_down_t_swiglu(name, dx, w, layer, g, u):
    t, n = dx.shape
    f = g.shape[1]
    tm = _pick(t, (512, 256, 128))

    def body(a_ref, w_ref, dag_ref, dau_ref, dg_ref, du_ref):
        da = _dot(a_ref[...].astype(BF16), w_ref[...], NT)
        dg_ref[...] = (da * dag_ref[...].astype(F32)).astype(BF16)
        du_ref[...] = (da * dau_ref[...].astype(F32)).astype(BF16)

    tile = pl.BlockSpec((tm, f), lambda i: (i, 0))
    return pl.pallas_call(
        body, name=name, grid=(t // tm,),
        in_specs=[pl.BlockSpec((tm, n), lambda i: (i, 0)),
                  _resident((None, f, n), lambda i: (layer, 0, 0)), tile, tile],
        out_specs=[tile, tile],
        out_shape=[_sds((t, f), BF16)] * 2,
        compiler_params=_params(("parallel",)))(dx, w, g, u)


def _dgrad_norm(name, acts, act_blocks, pieces, w4, layer, x, gain, dres):
    t, d = x.shape
    _, _, k, nq = w4.shape
    tm = _pick(t, (512, 256, 128))
    n_act = len(acts)

    def body(*refs):
        act_refs = refs[:n_act]
        w_ref, x_ref, g_ref, dr_ref, dx_ref, dg_ref = refs[n_act:]

        @pl.when(pl.program_id(0) == 0)
        def _():
            dg_ref[...] = jnp.zeros_like(dg_ref)

        dh = None
        for a_tile, w_tile in pieces(act_refs, w_ref):
            term = _dot(a_tile, w_tile, NT)
            dh = term if dh is None else dh + term
        xv = x_ref[...]
        r = _rms(xv)
        xhat = xv * r
        gd = dh * g_ref[...]
        dx_ref[...] = dr_ref[...] + r * (gd - xhat * jnp.mean(gd * xhat, axis=-1, keepdims=True))
        dg_ref[...] += (dh * xhat).reshape(tm // SUBLANES, SUBLANES, d).sum(axis=0)

    row = pl.BlockSpec((tm, d), lambda i: (i, 0))
    return pl.pallas_call(
        body, name=name, grid=(t // tm,),
        in_specs=[*act_blocks(tm),
                  _resident((None, N_CHIPS, k, nq), lambda i: (layer, 0, 0, 0)),
                  row, pl.BlockSpec((1, d), lambda i: (0, 0)), row],
        out_specs=[row, pl.BlockSpec((SUBLANES, d), lambda i: (0, 0))],
        out_shape=[_sds((t, d), F32), _sds((SUBLANES, d), F32)],
        compiler_params=_params(("arbitrary",)))(*acts, w4, x, gain, dres)


def _dgrad_norm_ffn(name, dg, du, w4, layer, x, gain, dres):
    nq = w4.shape[3]
    f = dg.shape[1]

    def blocks(tm):
        return [pl.BlockSpec((tm, f), lambda i: (i, 0))] * 2

    def pieces(act_refs, w_ref):
        dg_ref, du_ref = act_refs
        return [(dg_ref[:, 0:nq], w_ref[0]), (dg_ref[:, nq:2 * nq], w_ref[1]),
                (du_ref[:, 0:nq], w_ref[2]), (du_ref[:, nq:2 * nq], w_ref[3])]

    return _dgrad_norm(name, [dg, du], blocks, pieces, w4, layer, x, gain, dres)


def _dgrad_norm_qkv(name, dqkv, w4, x, gain, dres):
    nq = w4.shape[3]

    def blocks(tm):
        return [pl.BlockSpec((tm, N_CHIPS * nq), lambda i: (i, 0))]

    def pieces(act_refs, w_ref):
        return [(act_refs[0][:, q * nq:(q + 1) * nq], w_ref[q]) for q in range(N_CHIPS)]

    return _dgrad_norm(name, [dqkv], blocks, pieces, w4, 0, x, gain, dres)


def _dgrad_norm_conv(name, d3, w4, x, gain, dres):
    _, _, d = d3.shape
    nq = w4.shape[3]
    per_part, per_q = d // MXU_COLS, nq // MXU_COLS

    def blocks(tm):
        return [pl.BlockSpec((3, tm, d), lambda i: (0, i, 0))]

    def pieces(act_refs, w_ref):
        out = []
        for jb in range(3 * per_part):
            ca, cw = (jb % per_part) * MXU_COLS, (jb % per_q) * MXU_COLS
            out.append((act_refs[0][jb // per_part, :, ca:ca + MXU_COLS], w_ref[jb // per_q, :, cw:cw + MXU_COLS]))
        return out

    return _dgrad_norm(name, [d3], blocks, pieces, w4, 0, x, gain, dres)


def _wgrad_up2(name, h, dg, du):
    t, k = h.shape
    nq = dg.shape[1] // 2
    tk = _pick(t, (1024, 512, 256, 128))
    steps = t // tk
    half = N_CHIPS // 2

    def body(h_ref, dg_ref, du_ref, o_ref):
        q = pl.program_id(0)

        @pl.when(pl.program_id(1) == 0)
        def _():
            o_ref[...] = jnp.zeros_like(o_ref)

        @pl.when(q < half)
        def _():
            o_ref[...] += _dot(h_ref[...], dg_ref[...], TN)

        @pl.when(q >= half)
        def _():
            o_ref[...] += _dot(h_ref[...], du_ref[...], TN)

    return pl.pallas_call(
        body, name=name, grid=(N_CHIPS, steps),
        in_specs=[pl.BlockSpec((tk, k), lambda q, s: (s, 0)),
                  pl.BlockSpec((tk, nq), lambda q, s: (jnp.where(q < half, s, steps - 1), jnp.minimum(q, half - 1))),
                  pl.BlockSpec((tk, nq), lambda q, s: (jnp.where(q >= half, s, 0), jnp.maximum(q - half, 0)))],
        out_specs=pl.BlockSpec((None, k, nq), lambda q, s: (q, 0, 0)),
        out_shape=_sds((N_CHIPS, k, nq), F32),
        compiler_params=_params(("parallel", "arbitrary")))(h, dg, du)


def _wgrad_joined(name, h, dy):
    t, k = h.shape
    nq = dy.shape[1] // N_CHIPS
    tk = _pick(t, (1024, 512, 256, 128))

    def body(h_ref, dy_ref, o_ref):
        @pl.when(pl.program_id(0) == 0)
        def _():
            o_ref[...] = jnp.zeros_like(o_ref)

        res = _dot(h_ref[...], dy_ref[...], TN)
        for q in range(N_CHIPS):
            o_ref[q] += res[:, q * nq:(q + 1) * nq]

    return pl.pallas_call(
        body, name=name, grid=(t // tk,),
        in_specs=[pl.BlockSpec((tk, k), lambda s: (s, 0)), pl.BlockSpec((tk, N_CHIPS * nq), lambda s: (s, 0))],
        out_specs=pl.BlockSpec((N_CHIPS, k, nq), lambda s: (0, 0, 0)),
        out_shape=_sds((N_CHIPS, k, nq), F32),
        compiler_params=_params(("arbitrary",)))(h, dy)


def _wgrad_conv_in(name, h, d3, nq):
    t, k = h.shape
    d = d3.shape[2]
    per_part, per_q = d // MXU_COLS, nq // MXU_COLS
    tk = _pick(t, (512, 256, 128))

    def body(h_ref, d_ref, o_ref):
        @pl.when(pl.program_id(0) == 0)
        def _():
            o_ref[...] = jnp.zeros_like(o_ref)

        hv = h_ref[...]
        for part in range(3):
            res = _dot(hv, d_ref[part], TN)
            for cc in range(per_part):
                jb = part * per_part + cc
                co = (jb % per_q) * MXU_COLS
                o_ref[jb // per_q, :, co:co + MXU_COLS] += res[:, cc * MXU_COLS:(cc + 1) * MXU_COLS]

    return pl.pallas_call(
        body, name=name, grid=(t // tk,),
        in_specs=[pl.BlockSpec((tk, k), lambda s: (s, 0)), pl.BlockSpec((3, tk, d), lambda s: (0, s, 0))],
        out_specs=pl.BlockSpec((N_CHIPS, k, nq), lambda s: (0, 0, 0)),
        out_shape=_sds((N_CHIPS, k, nq), F32),
        compiler_params=_params(("arbitrary",)))(h, d3)


def _wgrad_down(name, a, dx, tmw):
    t, kf = a.shape
    n = dx.shape[1]
    tk = _pick(t, (1024, 512, 256, 128))

    def body(a_ref, b_ref, o_ref):
        @pl.when(pl.program_id(1) == 0)
        def _():
            o_ref[...] = jnp.zeros_like(o_ref)

        o_ref[...] += _dot(a_ref[...], b_ref[...].astype(BF16), TN)

    g = pl.pallas_call(
        body, name=name, grid=(kf // tmw, t // tk),
        in_specs=[pl.BlockSpec((tk, tmw), lambda j, s: (s, j)), pl.BlockSpec((tk, n), lambda j, s: (s, 0))],
        out_specs=pl.BlockSpec((tmw, n), lambda j, s: (j, 0)),
        out_shape=_sds((kf, n), F32),
        compiler_params=_params(("parallel", "arbitrary")))(a, dx)
    return g.reshape(N_CHIPS, kf // N_CHIPS, n)


def _shift_rows(u, k, rows):
    s = u.shape[0]
    if k > 0:
        return jnp.where(rows >= k, pltpu.roll(u, k, 0), 0.0)
    return jnp.where(rows < s + k, pltpu.roll(u, s + k, 0), 0.0)


def _conv_fwd(bcx, cw, nseq, seq):
    t, d3 = bcx.shape
    d = d3 // 3
    cb = MXU_COLS
    nj = d // cb

    def body(b_ref, c_ref, x_ref, cw_ref, z_ref):
        u = b_ref[...].astype(F32) * x_ref[...].astype(F32)
        rows = lax.broadcasted_iota(jnp.int32, u.shape, 0)
        cwv = cw_ref[...]
        y = cwv[2:3] * u + cwv[1:2] * _shift_rows(u, 1, rows) + cwv[0:1] * _shift_rows(u, 2, rows)
        z_ref[...] = (c_ref[...].astype(F32) * y).astype(BF16)

    return pl.pallas_call(
        body, name="conv_fwd", grid=(nseq, nj),
        in_specs=[pl.BlockSpec((seq, cb), lambda b, j: (b, j)),
                  pl.BlockSpec((seq, cb), lambda b, j: (b, nj + j)),
                  pl.BlockSpec((seq, cb), lambda b, j: (b, 2 * nj + j)),
                  pl.BlockSpec((3, cb), lambda b, j: (0, j))],
        out_specs=pl.BlockSpec((seq, cb), lambda b, j: (b, j)),
        out_shape=_sds((t, d), BF16),
        compiler_params=_params(("parallel", "parallel")))(bcx, bcx, bcx, cw)


def _conv_bwd(dz, bcx, cw, nseq, seq):
    t, d3 = bcx.shape
    d = d3 // 3
    cb = MXU_COLS
    nj = d // cb

    def body(dz_ref, b_ref, c_ref, x_ref, cw_ref, o_ref, dcw_ref):
        @pl.when(pl.program_id(1) == 0)
        def _():
            dcw_ref[...] = jnp.zeros_like(dcw_ref)

        b = b_ref[...].astype(F32)
        c = c_ref[...].astype(F32)
        xv = x_ref[...].astype(F32)
        dzv = dz_ref[...].astype(F32)
        u = b * xv
        rows = lax.broadcasted_iota(jnp.int32, u.shape, 0)
        u1 = _shift_rows(u, 1, rows)
        u2 = _shift_rows(u, 2, rows)
        cwv = cw_ref[...]
        y = cwv[2:3] * u + cwv[1:2] * u1 + cwv[0:1] * u2
        dyc = dzv * c
        du = cwv[2:3] * dyc + cwv[1:2] * _shift_rows(dyc, -1, rows) + cwv[0:1] * _shift_rows(dyc, -2, rows)
        o_ref[0] = (du * xv).astype(BF16)
        o_ref[1] = (dzv * y).astype(BF16)
        o_ref[2] = (du * b).astype(BF16)
        s0 = jnp.sum(dyc * u2, axis=0, keepdims=True)
        s1 = jnp.sum(dyc * u1, axis=0, keepdims=True)
        s2 = jnp.sum(dyc * u, axis=0, keepdims=True)
        tap = lax.broadcasted_iota(jnp.int32, (3, cb), 0)
        dcw_ref[...] += jnp.where(tap == 0, s0, jnp.where(tap == 1, s1, s2))

    return pl.pallas_call(
        body, name="conv_bwd", grid=(nj, nseq),
        in_specs=[pl.BlockSpec((seq, cb), lambda j, b: (b, j)),
                  pl.BlockSpec((seq, cb), lambda j, b: (b, j)),
                  pl.BlockSpec((seq, cb), lambda j, b: (b, nj + j)),
                  pl.BlockSpec((seq, cb), lambda j, b: (b, 2 * nj + j)),
                  pl.BlockSpec((3, cb), lambda j, b: (0, j))],
        out_specs=[pl.BlockSpec((3, seq, cb), lambda j, b: (0, b, j)),
                   pl.BlockSpec((3, cb), lambda j, b: (0, j))],
        out_shape=[_sds((3, t, d), BF16), _sds((3, d), F32)],
        compiler_params=_params(("parallel", "arbitrary")))(dz, bcx, bcx, bcx, cw)


def _pair_norm(x, gain_pair, low):
    sq = x * x
    ss_lo = jnp.sum(jnp.where(low, sq, 0.0), axis=-1, keepdims=True)
    ss_hi = jnp.sum(jnp.where(low, 0.0, sq), axis=-1, keepdims=True)
    r = lax.rsqrt(jnp.where(low, ss_lo, ss_hi) * (1.0 / HEAD_DIM) + EPS)
    xhat = x * r
    return xhat * gain_pair, xhat, r


KEYS = 2 * BLOCK
QK_SCALE = 1.0 / (HEAD_DIM ** 0.5)
N_PAIRS = N_Q_HEADS // 2


def _earlier_block(shape=(BLOCK, BLOCK)):
    return lax.broadcasted_iota(jnp.int32, shape, 0) > lax.broadcasted_iota(jnp.int32, shape, 1)


def _fill_bias(bias_ref):
    rows = lax.broadcasted_iota(jnp.int32, (2 * BLOCK, BLOCK), 0)
    qi = lax.broadcasted_iota(jnp.int32, (2 * BLOCK, BLOCK), 1)
    odd_head = rows >= BLOCK
    kj = jnp.where(odd_head, rows - BLOCK, rows)
    earlier = kj > qi
    dist = (jnp.where(earlier, BLOCK, 0) + qi - kj).astype(F32)
    for j in range(N_PAIRS):
        slope = jnp.where(odd_head, ALIBI_SLOPES[2 * j + 1], ALIBI_SLOPES[2 * j])
        bias = -slope * dist
        bias_ref[1, j] = bias
        bias_ref[0, j] = jnp.where(earlier, -1e30, bias)


def _merge_blocks(x_t, earlier):
    return jnp.concatenate([jnp.where(earlier, x_t[e * KEYS:e * KEYS + BLOCK], x_t[e * KEYS + BLOCK:(e + 1) * KEYS])
                            for e in range(2)], axis=0)


def _split_blocks(heads, earlier):
    parts = []
    for x in heads:
        parts += [jnp.where(earlier, x, 0.0), jnp.where(earlier, 0.0, x)]
    return jnp.concatenate(parts, axis=0).astype(BF16)


def _kv_pair_rows(kv_tile, parity, low):
    own = jnp.where(low if parity == 0 else jnp.logical_not(low), kv_tile, 0.0)
    other = pltpu.roll(own, HEAD_DIM, 1)
    lo, hi = (own, other) if parity == 0 else (other, own)
    return jnp.concatenate([lo, hi], axis=0).astype(BF16)


def _pair_softmax(s_t, sink_even, sink_odd):
    out = []
    for e, sink in enumerate((sink_even, sink_odd)):
        se = s_t[e * BLOCK:(e + 1) * BLOCK]
        m = jnp.maximum(jnp.max(se, axis=0, keepdims=True), sink)
        ee = jnp.exp(se - m)
        es = jnp.exp(sink - m)
        inv = 1.0 / (jnp.sum(ee, axis=0, keepdims=True) + es)
        out.append((ee * inv, es * inv))
    return out


def _attn_rows(n):
    q0 = pl.multiple_of(n * BLOCK, BLOCK)
    k0 = pl.multiple_of(jnp.maximum(n - 1, 0) * BLOCK, BLOCK)
    return q0, k0, jnp.minimum(n, 1)


def _key_rows(qkv_ref, k0, q0, col):
    return jnp.concatenate([qkv_ref[pl.ds(k0, BLOCK), col:col + LANES], qkv_ref[pl.ds(q0, BLOCK), col:col + LANES]],
                           axis=0).astype(F32)


def _attn_fwd(qkv, qg_pair, kg_pair, sinks, nseq, seq):
    t = qkv.shape[0]
    dq = N_Q_HEADS * HEAD_DIM
    dkv = N_KV_HEADS * HEAD_DIM

    def body(sk_ref, qkv_ref, qg_ref, kg_ref, o_ref, bias_ref):
        @pl.when(pl.program_id(0) == 0)
        def _():
            _fill_bias(bias_ref)

        low = lax.broadcasted_iota(jnp.int32, (1, LANES), 1) < HEAD_DIM
        earlier = _earlier_block()
        qg = qg_ref[...] * QK_SCALE
        kg = kg_ref[...]

        def blk(n, carry):
            q0, k0, later = _attn_rows(n)
            for kt in range(dkv // LANES):
                kraw = _key_rows(qkv_ref, k0, q0, dq + kt * LANES)
                vraw = _key_rows(qkv_ref, k0, q0, dq + dkv + kt * LANES)
                kn, _, _ = _pair_norm(kraw, kg, low)
                for par in range(2):
                    kh = 2 * kt + par
                    k_pair = _kv_pair_rows(kn, par, low)
                    v_pair = _kv_pair_rows(vraw, par, low)
                    for jj in range(2):
                        j = 2 * kh + jj
                        qraw = qkv_ref[pl.ds(q0, BLOCK), j * LANES:(j + 1) * LANES].astype(F32)
                        qn, _, _ = _pair_norm(qraw, qg, low)
                        s_t = _merge_blocks(_dot(k_pair, qn.astype(BF16), NT), earlier) + bias_ref[later, j]
                        (p0, _), (p1, _) = _pair_softmax(s_t, sk_ref[0, 2 * j], sk_ref[0, 2 * j + 1])
                        p_t = _split_blocks((p0, p1), earlier)
                        o_ref[pl.ds(q0, BLOCK), j * LANES:(j + 1) * LANES] = _dot(p_t, v_pair, TN).astype(BF16)
            return carry

        lax.fori_loop(0, seq // BLOCK, blk, 0)

    return pl.pallas_call(
        body, name="attn_fwd", grid=(nseq,),
        in_specs=[pl.BlockSpec(memory_space=pltpu.SMEM),
                  pl.BlockSpec((seq, dq + 2 * dkv), lambda b: (b, 0)),
                  pl.BlockSpec((1, LANES), lambda b: (0, 0)),
                  pl.BlockSpec((1, LANES), lambda b: (0, 0))],
        out_specs=pl.BlockSpec((seq, dq), lambda b: (b, 0)),
        out_shape=_sds((t, dq), BF16),
        scratch_shapes=[pltpu.VMEM((2, N_PAIRS, 2 * BLOCK, BLOCK), F32)],
        compiler_params=_params(("arbitrary",)))(sinks, qkv, qg_pair, kg_pair)


def _attn_bwd(do, qkv, qg_pair, kg_pair, sinks, nseq, seq):
    t = qkv.shape[0]
    dq = N_Q_HEADS * HEAD_DIM
    dkv = N_KV_HEADS * HEAD_DIM

    def body(sk_ref, do_ref, qkv_ref, qg_ref, kg_ref, o_ref, dqg_ref, dkg_ref, dsk_ref, acc_ref, bias_ref):
        @pl.when(pl.program_id(0) == 0)
        def _():
            _fill_bias(bias_ref)
            dqg_ref[...] = jnp.zeros_like(dqg_ref)
            dkg_ref[...] = jnp.zeros_like(dkg_ref)
            dsk_ref[...] = jnp.zeros_like(dsk_ref)

        acc_ref[...] = jnp.zeros_like(acc_ref)
        low = lax.broadcasted_iota(jnp.int32, (1, LANES), 1) < HEAD_DIM
        earlier = _earlier_block()
        head_row = lax.broadcasted_iota(jnp.int32, (N_Q_HEADS, LANES), 0)
        qg = qg_ref[...] * QK_SCALE
        kg = kg_ref[...]

        def blk(n, carry):
            dqg_acc, dkg_acc, dsk_acc = carry
            q0, k0, later = _attn_rows(n)
            for kt in range(dkv // LANES):
                kraw = _key_rows(qkv_ref, k0, q0, dq + kt * LANES)
                vraw = _key_rows(qkv_ref, k0, q0, dq + dkv + kt * LANES)
                kn, khat, rk = _pair_norm(kraw, kg, low)
                dk_tile = None
                dv_tile = None
                for par in range(2):
                    kh = 2 * kt + par
                    own = low if par == 0 else jnp.logical_not(low)
                    k_pair = _kv_pair_rows(kn, par, low)
                    v_pair = _kv_pair_rows(vraw, par, low)
                    dkn_rows = jnp.zeros((2 * KEYS, LANES), F32)
                    dv_rows = jnp.zeros((2 * KEYS, LANES), F32)
                    for jj in range(2):
                        j = 2 * kh + jj
                        qraw = qkv_ref[pl.ds(q0, BLOCK), j * LANES:(j + 1) * LANES].astype(F32)
                        qn, qhat, rq = _pair_norm(qraw, qg, low)
                        qn_b = qn.astype(BF16)
                        do_b = do_ref[pl.ds(q0, BLOCK), j * LANES:(j + 1) * LANES]
                        s_t = _merge_blocks(_dot(k_pair, qn_b, NT), earlier) + bias_ref[later, j]
                        dp_t = _merge_blocks(_dot(v_pair, do_b, NT), earlier)
                        ds_heads = []
                        probs = _pair_softmax(s_t, sk_ref[0, 2 * j], sk_ref[0, 2 * j + 1])
                        for e, (p, ps) in enumerate(probs):
                            dp = dp_t[e * BLOCK:(e + 1) * BLOCK]
                            dsum = jnp.sum(p * dp, axis=0, keepdims=True)
                            ds_heads.append(p * (dp - dsum))
                            dsk_acc = dsk_acc - jnp.where(head_row == 2 * j + e, ps * dsum, 0.0)
                        p_t = _split_blocks((probs[0][0], probs[1][0]), earlier)
                        ds_t = _split_blocks(ds_heads, earlier)
                        dv_rows = dv_rows + _dot(p_t, do_b, NN)
                        dkn_rows = dkn_rows + _dot(ds_t, qn_b, NN)
                        dqn = _dot(ds_t, k_pair, TN)
                        dqg_acc = dqg_acc + jnp.sum(dqn * qhat, axis=0, keepdims=True)
                        dqhat = dqn * qg
                        prod = dqhat * qhat
                        m_lo = jnp.sum(jnp.where(low, prod, 0.0), axis=-1, keepdims=True)
                        m_hi = jnp.sum(jnp.where(low, 0.0, prod), axis=-1, keepdims=True)
                        mean = jnp.where(low, m_lo, m_hi) * (1.0 / HEAD_DIM)
                        o_ref[pl.ds(q0, BLOCK), j * LANES:(j + 1) * LANES] = (rq * (dqhat - qhat * mean)).astype(BF16)
                    dkn_acc = jnp.where(low, dkn_rows[0:KEYS], dkn_rows[KEYS:2 * KEYS])
                    dv_acc = jnp.where(low, dv_rows[0:KEYS], dv_rows[KEYS:2 * KEYS])
                    dkn = dkn_acc + pltpu.roll(dkn_acc, HEAD_DIM, 1)
                    dvh = dv_acc + pltpu.roll(dv_acc, HEAD_DIM, 1)
                    khat_own = jnp.where(own, khat, 0.0)
                    khat_dup = khat_own + pltpu.roll(khat_own, HEAD_DIM, 1)
                    dkg_acc = dkg_acc + jnp.sum(jnp.where(own, dkn * khat_dup, 0.0), axis=0, keepdims=True)
                    dkhat = dkn * kg
                    mean_k = jnp.sum(dkhat * khat_dup, axis=-1, keepdims=True) * (1.0 / LANES)
                    dk_raw = rk * (dkhat - khat_dup * mean_k)
                    dk_tile = jnp.where(own, dk_raw, 0.0) if dk_tile is None else jnp.where(own, dk_raw, dk_tile)
                    dv_tile = jnp.where(own, dvh, 0.0) if dv_tile is None else jnp.where(own, dvh, dv_tile)
                for r0, part in ((k0, slice(0, BLOCK)), (q0, slice(BLOCK, KEYS))):
                    acc_ref[pl.ds(r0, BLOCK), kt * LANES:(kt + 1) * LANES] += dk_tile[part]
                    acc_ref[pl.ds(r0, BLOCK), dkv + kt * LANES:dkv + (kt + 1) * LANES] += dv_tile[part]
            return dqg_acc, dkg_acc, dsk_acc

        zero = jnp.zeros((1, LANES), F32)
        carry = (zero, zero, jnp.zeros((N_Q_HEADS, LANES), F32))
        dqg_acc, dkg_acc, dsk_acc = lax.fori_loop(0, seq // BLOCK, blk, carry)
        dqg_ref[...] += dqg_acc * QK_SCALE
        dkg_ref[...] += dkg_acc
        dsk_ref[...] += dsk_acc
        o_ref[:, dq:dq + 2 * dkv] = acc_ref[...].astype(BF16)

    small = pl.BlockSpec((1, LANES), lambda b: (0, 0))
    heads = pl.BlockSpec((N_Q_HEADS, LANES), lambda b: (0, 0))
    return pl.pallas_call(
        body, name="attn_bwd", grid=(nseq,),
        in_specs=[pl.BlockSpec(memory_space=pltpu.SMEM),
                  pl.BlockSpec((seq, dq), lambda b: (b, 0)),
                  pl.BlockSpec((seq, dq + 2 * dkv), lambda b: (b, 0)),
                  small, small],
        out_specs=[pl.BlockSpec((seq, dq + 2 * dkv), lambda b: (b, 0)), small, small, heads],
        out_shape=[_sds((t, dq + 2 * dkv), BF16), _sds((1, LANES), F32), _sds((1, LANES), F32),
                   _sds((N_Q_HEADS, LANES), F32)],
        scratch_shapes=[pltpu.VMEM((seq, 2 * dkv), F32), pltpu.VMEM((2, N_PAIRS, 2 * BLOCK, BLOCK), F32)],
        compiler_params=_params(("arbitrary",)))(sinks, do, qkv, qg_pair, kg_pair)


def _place():
    x, y, c = lax.axis_index("x"), lax.axis_index("y"), lax.axis_index("c")
    other_chips = [(1 - x, y), (x, 1 - y), (1 - x, 1 - y)]
    return x, y, c, other_chips


def _half_rows(c, rows):
    rh = rows // 2
    return pl.ds(pl.multiple_of(c * rh, BF16_ROWS), rh)


def _cast_own(name, w, place, layer=None):
    nl, r, cdim = w.shape
    first = 0
    if layer is not None:
        nl, first = 1, layer
    rt = _row_tile(r, 4 * cdim, ELEMENTWISE_BLOCK)

    def body(s_ref, w_ref, o_ref):
        o_ref[...] = w_ref[...].astype(BF16)

    grid_spec = pltpu.PrefetchScalarGridSpec(
        num_scalar_prefetch=1, grid=(nl, r // rt),
        in_specs=[pl.BlockSpec((None, rt, cdim), lambda l, i, s: (first + l, i, 0))],
        out_specs=pl.BlockSpec((None, None, rt, cdim), lambda l, i, s: (l, s[1], i, 0)))
    return pl.pallas_call(
        body, name=name, grid_spec=grid_spec, out_shape=_sds((nl, N_CHIPS, r, cdim), BF16),
        compiler_params=_params(("parallel", "parallel")))(place, w)


def _gather_protocol(outs, shapes, send_sems, recv_sems):
    n = len(outs)
    x, y, c, other_chips = _place()
    me_chip = 2 * x + y
    sibling = (x, y, 1 - c)

    def rows(u, chip, half):
        return outs[u].at[:, chip, _half_rows(half, shapes[u][2]), :]

    def copy(sem, part, to):
        return pltpu.make_async_remote_copy(src_ref=part, dst_ref=part, send_sem=send_sems.at[sem],
                                            recv_sem=recv_sems.at[sem], device_id=to, device_id_type=MESH)

    sends = []
    for u in range(n):
        for k, chip in enumerate(other_chips):
            cp = copy(6 * u + k, rows(u, me_chip, c), (*chip, c))
            cp.start()
            sends.append(cp)
    for u in range(n):
        for k, chip in enumerate(other_chips):
            got = rows(u, 2 * chip[0] + chip[1], c)
            copy(6 * u + k, got, (*chip, c)).wait_recv()
            cp = copy(6 * u + 3 + k, got, sibling)
            cp.start()
            sends.append(cp)
    for u in range(n):
        for k, chip in enumerate(other_chips):
            copy(6 * u + 3 + k, rows(u, 2 * chip[0] + chip[1], 1 - c), sibling).wait_recv()
    for cp in sends:
        cp.wait_send()


def _hbm_ref(a):
    return jax.new_ref(a, memory_space=pltpu.MemorySpace.HBM)


def _sibling_peer():
    x, y, c, _ = _place()
    return [(x, y, 1 - c)]


def _chip_peers():
    x, y, c, other_chips = _place()
    return [(*chip, c) for chip in other_chips]


def _gather_peers():
    return _chip_peers() + _sibling_peer()


def _on_sequencer(name, collective_id, n_sems, peers, protocol, operands=(), out_types=()):
    n_in, n_out = len(operands), len(out_types)

    def launch(*refs):
        send_sems, recv_sems = refs[n_in + n_out:]
        barrier = pltpu.get_barrier_semaphore()
        targets = peers()
        for peer in targets:
            pl.semaphore_signal(barrier, inc=1, device_id=peer, device_id_type=MESH)
        pl.semaphore_wait(barrier, len(targets))
        protocol(refs[:n_in], refs[n_in:n_in + n_out], send_sems, recv_sems)

    return pl.kernel(
        launch, out_type=tuple(out_types), mesh=plsc.ScalarSubcoreMesh(axis_name="sequencer", num_cores=1), name=name,
        scratch_types=(pltpu.SemaphoreType.DMA((n_sems,)), pltpu.SemaphoreType.DMA((n_sems,))),
        compiler_params=pltpu.CompilerParams(collective_id=collective_id))(*operands)


def _seq_allgather(name, collective_id, bufs):
    shapes = [b.shape for b in bufs]
    refs = [_hbm_ref(b) for b in bufs]
    _on_sequencer(name, collective_id, 6 * len(bufs), _gather_peers,
                  lambda ins, outs, send_sems, recv_sems: _gather_protocol(refs, shapes, send_sems, recv_sems))
    return [r[...] for r in refs]


def _exchange_protocol(gs, outs, shapes, send_sems, recv_sems):
    x, y, c, _ = _place()
    sends = []
    for u in range(len(gs)):
        cp = pltpu.make_async_remote_copy(
            src_ref=gs[u].at[:, _half_rows(1 - c, shapes[u][1]), :], dst_ref=outs[u],
            send_sem=send_sems.at[u], recv_sem=recv_sems.at[u], device_id=(x, y, 1 - c), device_id_type=MESH)
        cp.start()
        sends.append(cp)
    for cp in sends:
        cp.wait_recv()
    for cp in sends:
        cp.wait_send()


def _seq_exchange(name, collective_id, grads):
    shapes = [g.shape for g in grads]
    return _on_sequencer(
        name, collective_id, len(grads), _sibling_peer,
        lambda gs, outs, send_sems, recv_sems: _exchange_protocol(gs, outs, shapes, send_sems, recv_sems),
        operands=grads, out_types=[_sds((s[0], s[1] // 2, s[2]), F32) for s in shapes])


def _sum_halves(name, g, got, place, after):
    _, r, cdim = g.shape
    rh = r // 2
    rt = _row_tile(rh, 4 * cdim, ELEMENTWISE_BLOCK)
    nr = rh // rt

    def body(s_ref, g_ref, got_ref, after_ref, pb_ref, pf_ref):
        s = g_ref[...] + got_ref[...]
        pb_ref[...] = s.astype(BF16)

        @pl.when(pl.program_id(1) == s_ref[1])
        def _():
            pf_ref[...] = s

    grid_spec = pltpu.PrefetchScalarGridSpec(
        num_scalar_prefetch=1, grid=(nr, N_CHIPS),
        in_specs=[pl.BlockSpec((None, rt, cdim), lambda i, q, s: (q, s[0] * nr + i, 0)),
                  pl.BlockSpec((None, rt, cdim), lambda i, q, s: (q, i, 0)),
                  pl.BlockSpec(memory_space=pl.ANY)],
        out_specs=[pl.BlockSpec((None, rt, cdim), lambda i, q, s: (q, i, 0)),
                   pl.BlockSpec((rt, cdim), lambda i, q, s: (i, 0))])
    return pl.pallas_call(
        body, name=name, grid_spec=grid_spec,
        out_shape=[_sds((N_CHIPS, rh, cdim), BF16), _sds((rh, cdim), F32)],
        compiler_params=_params(("parallel", "arbitrary")))(place, g, got, after)


def _scatter_protocol(ps, outs, send_sems, recv_sems):
    x, y, c, other_chips = _place()
    sends = []
    for u in range(len(ps)):
        for k, chip in enumerate(other_chips):
            cp = pltpu.make_async_remote_copy(
                src_ref=ps[u].at[2 * chip[0] + chip[1]], dst_ref=outs[u].at[k],
                send_sem=send_sems.at[3 * u + k], recv_sem=recv_sems.at[3 * u + k],
                device_id=(*chip, c), device_id_type=MESH)
            cp.start()
            sends.append(cp)
    for cp in sends:
        cp.wait_recv()
    for cp in sends:
        cp.wait_send()


def _seq_scatter(name, collective_id, partials):
    return _on_sequencer(
        name, collective_id, 3 * len(partials), _chip_peers, _scatter_protocol,
        operands=partials, out_types=[_sds((3, p.shape[1], p.shape[2]), BF16) for p in partials])


def _sum_partials(name, own, got, place, layer, nl, prev, after):
    rh, cdim = own.shape
    rt = _row_tile(rh, 4 * cdim, ELEMENTWISE_BLOCK)
    nr = rh // rt

    def body(s_ref, own_ref, got_ref, *rest):
        o_ref = rest[-1]
        o_ref[...] = ((own_ref[...] + got_ref[0].astype(F32)) + got_ref[1].astype(F32)) + got_ref[2].astype(F32)

    in_specs = [pl.BlockSpec((rt, cdim), lambda i, s: (i, 0)), pl.BlockSpec((3, rt, cdim), lambda i, s: (0, i, 0)),
                pl.BlockSpec(memory_space=pl.ANY)]
    args = [place, own, got, after]
    aliases = {}
    if prev is not None:
        in_specs.append(pl.BlockSpec(memory_space=pl.ANY))
        args.append(prev)
        aliases = {4: 0}
    grid_spec = pltpu.PrefetchScalarGridSpec(
        num_scalar_prefetch=1, grid=(nr,), in_specs=in_specs,
        out_specs=pl.BlockSpec((None, rt, cdim), lambda i, s: (layer, s[0] * nr + i, 0)))
    return pl.pallas_call(
        body, name=name, grid_spec=grid_spec, out_shape=_sds((nl, 2 * rh, cdim), F32),
        input_output_aliases=aliases, compiler_params=_params(("parallel",)))(*args)


def _share_protocol(outs, shapes, units, send_sems, recv_sems):
    x, y, c, _ = _place()
    sends = []
    for u, (w, l) in enumerate(units):
        mine = outs[w].at[l, _half_rows(c, shapes[w][1]), :]
        cp = pltpu.make_async_remote_copy(src_ref=mine, dst_ref=mine, send_sem=send_sems.at[u],
                                          recv_sem=recv_sems.at[u], device_id=(x, y, 1 - c), device_id_type=MESH)
        cp.start()
        sends.append(cp)
    for u, (w, l) in enumerate(units):
        theirs = outs[w].at[l, _half_rows(1 - c, shapes[w][1]), :]
        pltpu.make_async_remote_copy(src_ref=theirs, dst_ref=theirs, send_sem=send_sems.at[u],
                                     recv_sem=recv_sems.at[u], device_id=(x, y, 1 - c),
                                     device_id_type=MESH).wait_recv()
    for cp in sends:
        cp.wait_send()


def _seq_share(name, collective_id, bufs):
    shapes = [b.shape for b in bufs]
    units = [(w, l) for w in range(len(bufs)) for l in range(shapes[w][0])]
    refs = [_hbm_ref(b) for b in bufs]
    _on_sequencer(name, collective_id, len(units), _sibling_peer,
                  lambda ins, outs, send_sems, recv_sems: _share_protocol(refs, shapes, units, send_sems, recv_sems))
    return [r[...] for r in refs]


def _gather_blocks(block_ref, all_ref, send_sems, recv_sems):
    x, y, c, _ = _place()
    me = 4 * x + 2 * y + c
    all_ref[me] = block_ref[...]
    sends = []
    for rel in range(1, 8):
        fx, fy, fc = (rel >> 2) & 1, (rel >> 1) & 1, rel & 1
        peer = (x ^ fx, y ^ fy, c ^ fc)
        cp = pltpu.make_async_remote_copy(src_ref=block_ref, dst_ref=all_ref.at[me], send_sem=send_sems.at[rel - 1],
                                          recv_sem=recv_sems.at[rel - 1], device_id=peer, device_id_type=MESH)
        cp.start()
        sends.append(cp)
    for cp in sends:
        cp.wait_recv()
    for cp in sends:
        cp.wait_send()


def _gather_conv_w(cw_block):
    r, d = cw_block.shape

    def body(b_ref, o_ref, all_ref, send_sems, recv_sems):
        _gather_blocks(b_ref, all_ref, send_sems, recv_sems)
        o_ref[...] = (all_ref[0] + all_ref[2]) + (all_ref[4] + all_ref[6])

    vm = pl.BlockSpec(memory_space=pltpu.VMEM)
    return pl.pallas_call(
        body, name="gather_conv_w", in_specs=[vm], out_specs=vm, out_shape=_sds((r, d), F32),
        scratch_shapes=[pltpu.VMEM((8, r, d), F32), pltpu.SemaphoreType.DMA((7,)), pltpu.SemaphoreType.DMA((7,))],
    )(cw_block)


def _adam(w, g, m, v):
    m_new = ADAM_B1 * m + (1.0 - ADAM_B1) * g
    v_new = ADAM_B2 * v + (1.0 - ADAM_B2) * (g * g)
    m_hat = m_new / (1.0 - ADAM_B1 ** ADAM_STEP)
    v_hat = v_new / (1.0 - ADAM_B2 ** ADAM_STEP)
    delta = -ADAM_LR * (m_hat / (jnp.sqrt(v_hat) + ADAM_EPS) + ADAM_WD * w)
    return delta, m_new, v_new


def _small_step(dnm0, dnm1, dnf0, dnf1, dcw, dqg, dkg, dsk, loss, w_blk, m_blk, v_blk):
    d = w_blk.shape[1]

    def body(dnm0_ref, dnm1_ref, dnf0_ref, dnf1_ref, dcw_ref, dqg_ref, dkg_ref, dsk_ref, loss_ref,
             w_ref, m_ref, v_ref, g_ref, dl_ref, mo_ref, vo_ref, blk_ref, all_ref, send_sems, recv_sems):
        blk_ref[...] = jnp.zeros_like(blk_ref)
        blk_ref[0:1, :] = jnp.sum(dnm0_ref[...], axis=0, keepdims=True)
        blk_ref[1:2, :] = jnp.sum(dnm1_ref[...], axis=0, keepdims=True)
        blk_ref[8:9, :] = jnp.sum(dnf0_ref[...], axis=0, keepdims=True)
        blk_ref[9:10, :] = jnp.sum(dnf1_ref[...], axis=0, keepdims=True)
        blk_ref[16:19, :] = dcw_ref[...]
        dqg_v = dqg_ref[...]
        dkg_v = dkg_ref[...]
        blk_ref[24:25, 0:LANES] = dqg_v + pltpu.roll(dqg_v, HEAD_DIM, 1)
        blk_ref[24:25, LANES:2 * LANES] = dkg_v + pltpu.roll(dkg_v, HEAD_DIM, 1)
        for h in range(N_Q_HEADS):
            blk_ref[24:25, 2 * LANES + h:2 * LANES + h + 1] = jnp.sum(dsk_ref[h:h + 1, :], axis=1, keepdims=True)
        blk_ref[24:25, 3 * LANES:4 * LANES] = jnp.broadcast_to(loss_ref[...], (1, LANES))
        _gather_blocks(blk_ref, all_ref, send_sems, recv_sems)
        g = all_ref[0]
        for dev in range(1, 8):
            g = g + all_ref[dev]
        g_ref[...] = g
        delta, m_new, v_new = _adam(w_ref[...], g, m_ref[...], v_ref[...])
        dl_ref[...] = delta
        mo_ref[...] = m_new
        vo_ref[...] = v_new

    vm = pl.BlockSpec(memory_space=pltpu.VMEM)
    blk = _sds((SMALL_ROWS, d), F32)
    return pl.pallas_call(
        body, name="small_step", in_specs=[vm] * 12, out_specs=[vm] * 4, out_shape=[blk] * 4,
        scratch_shapes=[pltpu.VMEM((SMALL_ROWS, d), F32), pltpu.VMEM((8, SMALL_ROWS, d), F32),
                        pltpu.SemaphoreType.DMA((7,)), pltpu.SemaphoreType.DMA((7,))],
    )(dnm0, dnm1, dnf0, dnf1, dcw, dqg, dkg, dsk, loss, w_blk, m_blk, v_blk)


def _adam_step(name, w, g, m, v):
    nl, r, cdim = w.shape
    rt = _row_tile(r, 4 * cdim, ELEMENTWISE_BLOCK)

    def body(w_ref, g_ref, m_ref, v_ref, d_ref, mo_ref, vo_ref):
        delta, m_new, v_new = _adam(w_ref[...], g_ref[...], m_ref[...], v_ref[...])
        d_ref[...] = delta
        mo_ref[...] = m_new
        vo_ref[...] = v_new

    spec = pl.BlockSpec((None, rt, cdim), lambda l, i: (l, i, 0))
    return pl.pallas_call(
        body, name=name, grid=(nl, r // rt), in_specs=[spec] * 4, out_specs=[spec] * 3,
        out_shape=[_sds(w.shape, F32)] * 3,
        compiler_params=_params(("parallel", "parallel")))(w, g, m, v)


def _pad_rows(a, rows=SUBLANES):
    return jnp.pad(a, ((0, rows - a.shape[0]), (0, 0)))


def _small_block(nm, nf, cw_local, qg, kg, sk, chip):
    d = nm.shape[1]
    cw_rows = lax.dynamic_update_slice(jnp.zeros((SUBLANES, d), F32), cw_local, (0, chip * cw_local.shape[1]))
    misc = jnp.concatenate([qg, qg, kg, kg, jnp.pad(sk, ((0, 0), (0, LANES - sk.shape[1]))),
                            jnp.zeros((1, d - 3 * LANES), F32)], axis=1)
    return jnp.concatenate([_pad_rows(nm), _pad_rows(nf), cw_rows, _pad_rows(misc)], axis=0)


def _unpack_small(blk, chip, cw_cols):
    cw = lax.dynamic_slice(blk[16:19], (0, chip * cw_cols), (3, cw_cols))[None]
    return dict(norm_mixer=blk[0:2], norm_ffn=blk[8:10], conv_w=cw, attn_q_gain=blk[24:25, 0:HEAD_DIM],
                attn_k_gain=blk[24:25, LANES:LANES + HEAD_DIM], attn_sinks=blk[24:25, 2 * LANES:2 * LANES + N_Q_HEADS])


WEIGHT_NAMES = ("conv_w_in", "conv_w", "conv_w_out", "attn_w_qkv", "attn_q_gain", "attn_k_gain", "attn_sinks",
                "attn_w_o", "norm_mixer", "norm_ffn", "ffn_w_gate_up", "ffn_w_down")
BIG = ("conv_w_in", "conv_w_out", "attn_w_qkv", "attn_w_o", "ffn_w_gate_up", "ffn_w_down")


def kernel(x, conv_w_in, conv_w, conv_w_out, attn_w_qkv, attn_q_gain, attn_k_gain, attn_sinks, attn_w_o, norm_mixer, norm_ffn, ffn_w_gate_up, ffn_w_down, loss_target, m_conv_w_in, m_conv_w, m_conv_w_out, m_attn_w_qkv, m_attn_q_gain, m_attn_k_gain, m_attn_sinks, m_attn_w_o, m_norm_mixer, m_norm_ffn, m_ffn_w_gate_up, m_ffn_w_down, v_conv_w_in, v_conv_w, v_conv_w_out, v_attn_w_qkv, v_attn_q_gain, v_attn_k_gain, v_attn_sinks, v_attn_w_o, v_norm_mixer, v_norm_ffn, v_ffn_w_gate_up, v_ffn_w_down):
    w = dict(conv_w_in=conv_w_in, conv_w=conv_w, conv_w_out=conv_w_out, attn_w_qkv=attn_w_qkv,
             attn_q_gain=attn_q_gain, attn_k_gain=attn_k_gain, attn_sinks=attn_sinks, attn_w_o=attn_w_o,
             norm_mixer=norm_mixer, norm_ffn=norm_ffn, ffn_w_gate_up=ffn_w_gate_up, ffn_w_down=ffn_w_down)
    m = dict(conv_w_in=m_conv_w_in, conv_w=m_conv_w, conv_w_out=m_conv_w_out, attn_w_qkv=m_attn_w_qkv,
             attn_q_gain=m_attn_q_gain, attn_k_gain=m_attn_k_gain, attn_sinks=m_attn_sinks, attn_w_o=m_attn_w_o,
             norm_mixer=m_norm_mixer, norm_ffn=m_norm_ffn, ffn_w_gate_up=m_ffn_w_gate_up, ffn_w_down=m_ffn_w_down)
    v = dict(conv_w_in=v_conv_w_in, conv_w=v_conv_w, conv_w_out=v_conv_w_out, attn_w_qkv=v_attn_w_qkv,
             attn_q_gain=v_attn_q_gain, attn_k_gain=v_attn_k_gain, attn_sinks=v_attn_sinks, attn_w_o=v_attn_w_o,
             norm_mixer=v_norm_mixer, norm_ffn=v_norm_ffn, ffn_w_gate_up=v_ffn_w_gate_up, ffn_w_down=v_ffn_w_down)

    nseq, seq, d = x.shape
    t = nseq * seq
    chip = 2 * lax.axis_index("x") + lax.axis_index("y")
    core = lax.axis_index("c")
    place = jnp.stack([core, chip]).astype(jnp.int32)
    x0 = x.reshape(t, d)
    tgt = loss_target.reshape(t, d)

    cw_block = lax.dynamic_update_slice(jnp.zeros((SUBLANES, d), F32), conv_w[0], (0, chip * conv_w.shape[2]))
    cw_full = _gather_conv_w(cw_block)[0:3]
    def cast(k, layer=None):
        return _cast_own(f"cast_{k}" + ("" if layer is None else str(layer)), w[k], place, layer)

    w_in, w_out = _seq_allgather("allgather_conv", 1, [cast("conv_w_in"), cast("conv_w_out")])
    w_gu0, w_dn0 = _seq_allgather("allgather_ffn0", 2, [cast("ffn_w_gate_up", 0), cast("ffn_w_down", 0)])
    w_qkv, w_o, w_gu1, w_dn1 = _seq_allgather(
        "allgather_rest", 3, [cast("attn_w_qkv"), cast("attn_w_o"), cast("ffn_w_gate_up", 1), cast("ffn_w_down", 1)])
    w_out = w_out.reshape(1, d, d)
    w_o = w_o.reshape(1, d, d)
    w_gu = [w_gu0, w_gu1]
    w_dn = [w_dn0.reshape(1, D_FF, d), w_dn1.reshape(1, D_FF, d)]

    qg_pair = jnp.concatenate([attn_q_gain, attn_q_gain], axis=1)
    kg_pair = jnp.concatenate([attn_k_gain, attn_k_gain], axis=1)

    def ffn_bwd(i, dxo, xin, h, g, u, a):
        g_dn = _wgrad_down(f"ffn{i}_down_wgrad", a, dxo, D_FF // 2)
        dg, du = _mm_down_t_swiglu(f"ffn{i}_down_dgrad", dxo, w_dn[i], 0, g, u)
        g_gu = _wgrad_up2(f"ffn{i}_up_wgrad", h, dg, du)
        dxi, dgain = _dgrad_norm_ffn(f"ffn{i}_up_dgrad", dg, du, w_gu[i], 0, xin, norm_ffn[i:i + 1], dxo)
        return dxi, dgain, g_gu, g_dn

    h0, bcx = _mm_norm_up_joined("conv_in", x0, norm_mixer[0:1], w_in, 512)
    z = _conv_fwd(bcx, cw_full, nseq, seq)
    x1, h1 = _mm_down_norm("conv_out", z, w_out, 0, x0, norm_ffn[0:1])
    g0, u0, a0 = _mm_up_swiglu("ffn0_up", h1, w_gu[0], 0)
    x2, h2 = _mm_down_norm("ffn0_down", a0, w_dn[0], 0, x1, norm_mixer[1:2])
    qkv = _mm_up_joined("attn_qkv", h2, w_qkv, 1024)
    o = _attn_fwd(qkv, qg_pair, kg_pair, attn_sinks, nseq, seq)
    x3, h3 = _mm_down_norm("attn_out", o, w_o, 0, x2, norm_ffn[1:2])
    g1, u1, a1 = _mm_up_swiglu("ffn1_up", h3, w_gu[1], 0)
    dy, loss_part = _mm_down_loss("ffn1_down", a1, w_dn[1], 0, x3, tgt)

    finished = {k: None for k in BIG}

    def exchange(tag, cid, units):
        return units, _seq_exchange(f"exchange_{tag}", cid, [g for _, _, g in units])

    def scatter(tag, cid, group, after):
        units, got = group
        sums = [_sum_halves(f"sum_halves_{k}{l}", g, r, place, after) for (k, l, g), r in zip(units, got)]
        return units, sums, _seq_scatter(f"scatter_{tag}", cid, [pb for pb, _ in sums])

    def finish(group, after):
        units, sums, arrived = group
        for (k, l, _), (_, pf), r in zip(units, sums, arrived):
            finished[k] = _sum_partials(f"sum_partials_{k}{l}", pf, r, place, l, w[k].shape[0], finished[k], after)

    dx3, dnf1, g_gu1, g_dn1 = ffn_bwd(1, dy, x3, h3, g1, u1, a1)
    ffn1 = exchange("ffn1", 4, [("ffn_w_down", 1, g_dn1), ("ffn_w_gate_up", 1, g_gu1)])
    g_o = _wgrad_down("attn_out_wgrad", o, dx3, d)
    do = _mm_down_t("attn_out_dgrad", dx3, w_o, 0)
    ffn1 = scatter("ffn1", 8, ffn1, do)
    dqkv, dqg, dkg, dsk = _attn_bwd(do, qkv, qg_pair, kg_pair, attn_sinks, nseq, seq)
    g_qkv = _wgrad_joined("attn_qkv_wgrad", h2, dqkv)
    attn = exchange("attn", 5, [("attn_w_o", 0, g_o), ("attn_w_qkv", 0, g_qkv)])
    dx2, dnm1 = _dgrad_norm_qkv("attn_qkv_dgrad", dqkv, w_qkv, x2, norm_mixer[1:2], dx3)
    finish(ffn1, dx2)
    attn = scatter("attn", 9, attn, dx2)
    dx1, dnf0, g_gu0, g_dn0 = ffn_bwd(0, dx2, x1, h1, g0, u0, a0)
    ffn0 = exchange("ffn0", 6, [("ffn_w_down", 0, g_dn0), ("ffn_w_gate_up", 0, g_gu0)])
    g_out = _wgrad_down("conv_out_wgrad", z, dx1, d)
    dz = _mm_down_t("conv_out_dgrad", dx1, w_out, 0)
    finish(attn, dz)
    ffn0 = scatter("ffn0", 10, ffn0, dz)
    dbcx, dcw = _conv_bwd(dz, bcx, cw_full, nseq, seq)
    g_in = _wgrad_conv_in("conv_in_wgrad", h0, dbcx, conv_w_in.shape[2])
    conv = exchange("conv", 7, [("conv_w_out", 0, g_out), ("conv_w_in", 0, g_in)])
    dx0, dnm0 = _dgrad_norm_conv("conv_in_dgrad", dbcx, w_in, x0, norm_mixer[0:1], dx1)
    finish(ffn0, dx0)
    late = ("attn_w_qkv", "attn_w_o", "ffn_w_gate_up", "ffn_w_down")
    grads_late = _seq_share("share_late", 12, [finished[k] for k in late])
    conv = scatter("conv", 11, conv, dx0)

    grad, delta, new_m, new_v = {}, {}, {}, {}

    def adam(k, g):
        grad[k] = g
        delta[k], new_m[k], new_v[k] = _adam_step(f"adam_{k}", w[k], g, m[k], v[k])

    for k, g in zip(late, grads_late):
        adam(k, g)

    def blocks(src):
        return _small_block(src["norm_mixer"], src["norm_ffn"], src["conv_w"][0], src["attn_q_gain"],
                            src["attn_k_gain"], src["attn_sinks"], chip)

    g_blk, d_blk, m_blk, v_blk = _small_step(dnm0, dnm1, dnf0, dnf1, dcw, dqg, dkg, dsk, loss_part,
                                             blocks(w), blocks(m), blocks(v))

    done = sum(new_v[k][0, 0:1, 0:1] for k in late) + v_blk[0:1, 0:1]
    finish(conv, done)
    last = ("conv_w_in", "conv_w_out")
    for k, g in zip(last, _seq_share("share_last", 13, [finished[k] for k in last])):
        adam(k, g)

    cw_cols = conv_w.shape[2]
    for dst, blk in ((grad, g_blk), (delta, d_blk), (new_m, m_blk), (new_v, v_blk)):
        dst.update(_unpack_small(blk, chip, cw_cols))
    loss = g_blk[24, 3 * LANES]

    return (loss, dx0.reshape(nseq, seq, d), *[grad[k] for k in WEIGHT_NAMES], *[delta[k] for k in WEIGHT_NAMES],
            *[new_m[k] for k in WEIGHT_NAMES], *[new_v[k] for k in WEIGHT_NAMES])
```

```python
import jax
import jax.numpy as jnp
from jax import lax
from jax.experimental import pallas as pl
from jax.experimental.pallas import tpu as pltpu
from jax.experimental.pallas import tpu_sc as plsc

F32 = jnp.float32
BF16 = jnp.bfloat16

D_MODEL = 1024
D_FF = 2816
N_Q_HEADS = 16
N_KV_HEADS = 4
HEAD_DIM = 64
WINDOW = 128
BLOCK = 128
EPS = 1e-6
N_CHIPS = 4
LANES = 128
SUBLANES = 8
BF16_ROWS = 16
MXU_COLS = 256
VMEM_LIMIT = 48 * 1024 * 1024
ADAM_LR, ADAM_B1, ADAM_B2, ADAM_EPS, ADAM_WD, ADAM_STEP = 0.001, 0.9, 0.999, 1e-08, 0.01, 10
ALIBI_SLOPES = tuple(2.0 ** (-8.0 * (h + 1) / N_Q_HEADS) for h in range(N_Q_HEADS))
SMALL_ROWS = 32
MESH = pl.DeviceIdType.MESH

NN = ((1,), (0,))
NT = ((1,), (1,))
TN = ((0,), (0,))


def _dot(a, b, dims):
    return lax.dot_general(a, b, (dims, ((), ())), preferred_element_type=F32)


def _pick(n, cands):
    for c in cands:
        if n % c == 0:
            return c
    raise ValueError((n, cands))


def _row_tile(rows, row_bytes, cap_bytes):
    fits = [r for r in range(BF16_ROWS, rows + 1, BF16_ROWS) if rows % r == 0 and r * row_bytes <= cap_bytes]
    if not fits:
        raise ValueError((rows, row_bytes, cap_bytes))
    return fits[-1]


ELEMENTWISE_BLOCK = 3 << 19


def _resident(block_shape, index_map):
    return pl.BlockSpec(block_shape, index_map, pipeline_mode=pl.Buffered(1))


def _params(sem):
    return pltpu.CompilerParams(dimension_semantics=sem, vmem_limit_bytes=VMEM_LIMIT)


def _sds(shape, dtype):
    return jax.ShapeDtypeStruct(shape, dtype)


def _rms(xv):
    return lax.rsqrt(jnp.mean(xv * xv, axis=-1, keepdims=True) + EPS)


def _sigmoid(g):
    return 1.0 / (1.0 + jnp.exp(-g))


def _mm_up_joined(name, a, w4, tm_pref):
    t, k = a.shape
    _, _, _, nq = w4.shape
    tm = _pick(t, (tm_pref, 256, 128))

    def body(a_ref, w_ref, o_ref, wcat_ref):
        @pl.when(pl.program_id(0) == 0)
        def _():
            for q in range(N_CHIPS):
                wcat_ref[:, q * nq:(q + 1) * nq] = w_ref[q]

        o_ref[...] = _dot(a_ref[...], wcat_ref[...], NN).astype(BF16)

    return pl.pallas_call(
        body, name=name, grid=(t // tm,),
        in_specs=[pl.BlockSpec((tm, k), lambda i: (i, 0)),
                  pl.BlockSpec((None, N_CHIPS, k, nq), lambda i: (0, 0, 0, 0))],
        out_specs=pl.BlockSpec((tm, N_CHIPS * nq), lambda i: (i, 0)),
        out_shape=_sds((t, N_CHIPS * nq), BF16),
        scratch_shapes=[pltpu.VMEM((k, N_CHIPS * nq), BF16)],
        compiler_params=_params(("arbitrary",)))(a, w4)


def _mm_norm_up_joined(name, x, gain, w4, tm_pref):
    t, k = x.shape
    _, _, _, nq = w4.shape
    tm = _pick(t, (tm_pref, 256, 128))

    def body(x_ref, g_ref, w_ref, h_ref, o_ref, wcat_ref):
        @pl.when(pl.program_id(0) == 0)
        def _():
            for q in range(N_CHIPS):
                wcat_ref[:, q * nq:(q + 1) * nq] = w_ref[q]

        xv = x_ref[...]
        h = ((xv * _rms(xv)) * g_ref[...]).astype(BF16)
        h_ref[...] = h
        o_ref[...] = _dot(h, wcat_ref[...], NN).astype(BF16)

    return pl.pallas_call(
        body, name=name, grid=(t // tm,),
        in_specs=[pl.BlockSpec((tm, k), lambda i: (i, 0)), pl.BlockSpec((1, k), lambda i: (0, 0)),
                  _resident((None, N_CHIPS, k, nq), lambda i: (0, 0, 0, 0))],
        out_specs=[pl.BlockSpec((tm, k), lambda i: (i, 0)), pl.BlockSpec((tm, N_CHIPS * nq), lambda i: (i, 0))],
        out_shape=[_sds((t, k), BF16), _sds((t, N_CHIPS * nq), BF16)],
        scratch_shapes=[pltpu.VMEM((k, N_CHIPS * nq), BF16)],
        compiler_params=_params(("arbitrary",)))(x, gain, w4)


def _mm_up_swiglu(name, h, w4, layer):
    t, k = h.shape
    _, _, _, nq = w4.shape
    tm = _pick(t, (512, 256, 128))

    def body(h_ref, wg_ref, wu_ref, dag_ref, dau_ref, a_ref):
        hv = h_ref[...]
        g = _dot(hv, wg_ref[...], NN)
        u = _dot(hv, wu_ref[...], NN)
        sg = _sigmoid(g)
        silu = g * sg
        dag_ref[...] = (u * (sg * (1.0 + g * (1.0 - sg)))).astype(BF16)
        dau_ref[...] = silu.astype(BF16)
        a_ref[...] = (silu * u).astype(BF16)

    half = N_CHIPS // 2
    out = pl.BlockSpec((tm, nq), lambda j, i: (i, j))
    return pl.pallas_call(
        body, name=name, grid=(half, t // tm),
        in_specs=[pl.BlockSpec((tm, k), lambda j, i: (i, 0)),
                  pl.BlockSpec((None, None, k, nq), lambda j, i: (layer, j, 0, 0)),
                  pl.BlockSpec((None, None, k, nq), lambda j, i: (layer, half + j, 0, 0))],
        out_specs=[out, out, out],
        out_shape=[_sds((t, half * nq), BF16)] * 3,
        compiler_params=_params(("parallel", "parallel")))(h, w4, w4)


def _mm_down_norm(name, a, w, layer, res, gain):
    t, kf = a.shape
    _, _, n = w.shape
    tm = _pick(t, (1024, 512, 256, 128))

    def body(a_ref, w_ref, r_ref, g_ref, o_ref, h_ref):
        xo = r_ref[...] + _dot(a_ref[...], w_ref[...], NN)
        o_ref[...] = xo
        h_ref[...] = ((xo * _rms(xo)) * g_ref[...]).astype(BF16)

    row = pl.BlockSpec((tm, n), lambda i: (i, 0))
    return pl.pallas_call(
        body, name=name, grid=(t // tm,),
        in_specs=[pl.BlockSpec((tm, kf), lambda i: (i, 0)),
                  _resident((None, kf, n), lambda i: (layer, 0, 0)),
                  row, pl.BlockSpec((1, n), lambda i: (0, 0))],
        out_specs=[row, row],
        out_shape=[_sds((t, n), F32), _sds((t, n), BF16)],
        compiler_params=_params(("parallel",)))(a, w, res, gain)


def _mm_down_loss(name, a, w, layer, res, tgt):
    t, kf = a.shape
    _, _, n = w.shape
    tm = _pick(t, (1024, 512, 256, 128))
    steps = t // tm

    def body(a_ref, w_ref, r_ref, t_ref, dy_ref, l_ref, acc_ref):
        i = pl.program_id(0)

        @pl.when(i == 0)
        def _():
            acc_ref[...] = jnp.zeros_like(acc_ref)

        e = (r_ref[...] + _dot(a_ref[...], w_ref[...], NN)) - t_ref[...]
        dy_ref[...] = e * (1.0 / n)
        acc_ref[...] += (e * e).reshape(tm // SUBLANES, SUBLANES, n).sum(axis=0)

        @pl.when(i == steps - 1)
        def _():
            l_ref[...] = jnp.sum(acc_ref[...], keepdims=True) * (0.5 / n)

    row = pl.BlockSpec((tm, n), lambda i: (i, 0))
    return pl.pallas_call(
        body, name=name, grid=(steps,),
        in_specs=[pl.BlockSpec((tm, kf), lambda i: (i, 0)),
                  _resident((None, kf, n), lambda i: (layer, 0, 0)), row, row],
        out_specs=[row, pl.BlockSpec((1, 1), lambda i: (0, 0))],
        out_shape=[_sds((t, n), F32), _sds((1, 1), F32)],
        scratch_shapes=[pltpu.VMEM((SUBLANES, n), F32)],
        compiler_params=_params(("arbitrary",)))(a, w, res, tgt)


def _mm_down_t(name, dx, w, layer):
    t, n = dx.shape
    _, kf, _ = w.shape
    tm = _pick(t, (512, 256, 128))

    def body(a_ref, w_ref, o_ref):
        o_ref[...] = _dot(a_ref[...].astype(BF16), w_ref[...], NT).astype(BF16)

    return pl.pallas_call(
        body, name=name, grid=(t // tm,),
        in_specs=[pl.BlockSpec((tm, n), lambda i: (i, 0)),
                  pl.BlockSpec((None, kf, n), lambda i: (layer, 0, 0))],
        out_specs=pl.BlockSpec((tm, kf), lambda i: (i, 0)),
        out_shape=_sds((t, kf), BF16),
        compiler_params=_params(("parallel",)))(dx, w)


def _mm_down_t_swiglu(name, dx, w, layer, g, u):
    t, n = dx.shape
    f = g.shape[1]
    tm = _pick(t, (512, 256, 128))

    def body(a_ref, w_ref, dag_ref, dau_ref, dg_ref, du_ref):
        da = _dot(a_ref[...].astype(BF16), w_ref[...], NT)
        dg_ref[...] = (da * dag_ref[...].astype(F32)).astype(BF16)
        du_ref[...] = (da * dau_ref[...].astype(F32)).astype(BF16)

    tile = pl.BlockSpec((tm, f), lambda i: (i, 0))
    return pl.pallas_call(
        body, name=name, grid=(t // tm,),
        in_specs=[pl.BlockSpec((tm, n), lambda i: (i, 0)),
                  _resident((None, f, n), lambda i: (layer, 0, 0)), tile, tile],
        out_specs=[tile, tile],
        out_shape=[_sds((t, f), BF16)] * 2,
        compiler_params=_params(("parallel",)))(dx, w, g, u)


def _dgrad_norm(name, acts, act_blocks, pieces, w4, layer, x, gain, dres):
    t, d = x.shape
    _, _, k, nq = w4.shape
    tm = _pick(t, (512, 256, 128))
    n_act = len(acts)

    def body(*refs):
        act_refs = refs[:n_act]
        w_ref, x_ref, g_ref, dr_ref, dx_ref, dg_ref = refs[n_act:]

        @pl.when(pl.program_id(0) == 0)
        def _():
            dg_ref[...] = jnp.zeros_like(dg_ref)

        dh = None
        for a_tile, w_tile in pieces(act_refs, w_ref):
            term = _dot(a_tile, w_tile, NT)
            dh = term if dh is None else dh + term
        xv = x_ref[...]
        r = _rms(xv)
        xhat = xv * r
        gd = dh * g_ref[...]
        dx_ref[...] = dr_ref[...] + r * (gd - xhat * jnp.mean(gd * xhat, axis=-1, keepdims=True))
        dg_ref[...] += (dh * xhat).reshape(tm // SUBLANES, SUBLANES, d).sum(axis=0)

    row = pl.BlockSpec((tm, d), lambda i: (i, 0))
    return pl.pallas_call(
        body, name=name, grid=(t // tm,),
        in_specs=[*act_blocks(tm),
                  _resident((None, N_CHIPS, k, nq), lambda i: (layer, 0, 0, 0)),
                  row, pl.BlockSpec((1, d), lambda i: (0, 0)), row],
        out_specs=[row, pl.BlockSpec((SUBLANES, d), lambda i: (0, 0))],
        out_shape=[_sds((t, d), F32), _sds((SUBLANES, d), F32)],
        compiler_params=_params(("arbitrary",)))(*acts, w4, x, gain, dres)


def _dgrad_norm_ffn(name, dg, du, w4, layer, x, gain, dres):
    nq = w4.shape[3]
    f = dg.shape[1]

    def blocks(tm):
        return [pl.BlockSpec((tm, f), lambda i: (i, 0))] * 2

    def pieces(act_refs, w_ref):
        dg_ref, du_ref = act_refs
        return [(dg_ref[:, 0:nq], w_ref[0]), (dg_ref[:, nq:2 * nq], w_ref[1]),
                (du_ref[:, 0:nq], w_ref[2]), (du_ref[:, nq:2 * nq], w_ref[3])]

    return _dgrad_norm(name, [dg, du], blocks, pieces, w4, layer, x, gain, dres)


def _dgrad_norm_qkv(name, dqkv, w4, x, gain, dres):
    nq = w4.shape[3]

    def blocks(tm):
        return [pl.BlockSpec((tm, N_CHIPS * nq), lambda i: (i, 0))]

    def pieces(act_refs, w_ref):
        return [(act_refs[0][:, q * nq:(q + 1) * nq], w_ref[q]) for q in range(N_CHIPS)]

    return _dgrad_norm(name, [dqkv], blocks, pieces, w4, 0, x, gain, dres)


def _dgrad_norm_conv(name, d3, w4, x, gain, dres):
    _, _, d = d3.shape
    nq = w4.shape[3]
    per_part, per_q = d // MXU_COLS, nq // MXU_COLS

    def blocks(tm):
        return [pl.BlockSpec((3, tm, d), lambda i: (0, i, 0))]

    def pieces(act_refs, w_ref):
        out = []
        for jb in range(3 * per_part):
            ca, cw = (jb % per_part) * MXU_COLS, (jb % per_q) * MXU_COLS
            out.append((act_refs[0][jb // per_part, :, ca:ca + MXU_COLS], w_ref[jb // per_q, :, cw:cw + MXU_COLS]))
        return out

    return _dgrad_norm(name, [d3], blocks, pieces, w4, 0, x, gain, dres)


def _wgrad_up2(name, h, dg, du):
    t, k = h.shape
    nq = dg.shape[1] // 2
    tk = _pick(t, (1024, 512, 256, 128))
    steps = t // tk
    half = N_CHIPS // 2

    def body(h_ref, dg_ref, du_ref, o_ref):
        q = pl.program_id(0)

        @pl.when(pl.program_id(1) == 0)
        def _():
            o_ref[...] = jnp.zeros_like(o_ref)

        @pl.when(q < half)
        def _():
            o_ref[...] += _dot(h_ref[...], dg_ref[...], TN)

        @pl.when(q >= half)
        def _():
            o_ref[...] += _dot(h_ref[...], du_ref[...], TN)

    return pl.pallas_call(
        body, name=name, grid=(N_CHIPS, steps),
        in_specs=[pl.BlockSpec((tk, k), lambda q, s: (s, 0)),
                  pl.BlockSpec((tk, nq), lambda q, s: (jnp.where(q < half, s, steps - 1), jnp.minimum(q, half - 1))),
                  pl.BlockSpec((tk, nq), lambda q, s: (jnp.where(q >= half, s, 0), jnp.maximum(q - half, 0)))],
        out_specs=pl.BlockSpec((None, k, nq), lambda q, s: (q, 0, 0)),
        out_shape=_sds((N_CHIPS, k, nq), F32),
        compiler_params=_params(("parallel", "arbitrary")))(h, dg, du)


def _wgrad_joined(name, h, dy):
    t, k = h.shape
    nq = dy.shape[1] // N_CHIPS
    tk = _pick(t, (1024, 512, 256, 128))

    def body(h_ref, dy_ref, o_ref):
        @pl.when(pl.program_id(0) == 0)
        def _():
            o_ref[...] = jnp.zeros_like(o_ref)

        res = _dot(h_ref[...], dy_ref[...], TN)
        for q in range(N_CHIPS):
            o_ref[q] += res[:, q * nq:(q + 1) * nq]

    return pl.pallas_call(
        body, name=name, grid=(t // tk,),
        in_specs=[pl.BlockSpec((tk, k), lambda s: (s, 0)), pl.BlockSpec((tk, N_CHIPS * nq), lambda s: (s, 0))],
        out_specs=pl.BlockSpec((N_CHIPS, k, nq), lambda s: (0, 0, 0)),
        out_shape=_sds((N_CHIPS, k, nq), F32),
        compiler_params=_params(("arbitrary",)))(h, dy)


def _wgrad_conv_in(name, h, d3, nq):
    t, k = h.shape
    d = d3.shape[2]
    per_part, per_q = d // MXU_COLS, nq // MXU_COLS
    tk = _pick(t, (512, 256, 128))

    def body(h_ref, d_ref, o_ref):
        @pl.when(pl.program_id(0) == 0)
        def _():
            o_ref[...] = jnp.zeros_like(o_ref)

        hv = h_ref[...]
        for part in range(3):
            res = _dot(hv, d_ref[part], TN)
            for cc in range(per_part):
                jb = part * per_part + cc
                co = (jb % per_q) * MXU_COLS
                o_ref[jb // per_q, :, co:co + MXU_COLS] += res[:, cc * MXU_COLS:(cc + 1) * MXU_COLS]

    return pl.pallas_call(
        body, name=name, grid=(t // tk,),
        in_specs=[pl.BlockSpec((tk, k), lambda s: (s, 0)), pl.BlockSpec((3, tk, d), lambda s: (0, s, 0))],
        out_specs=pl.BlockSpec((N_CHIPS, k, nq), lambda s: (0, 0, 0)),
        out_shape=_sds((N_CHIPS, k, nq), F32),
        compiler_params=_params(("arbitrary",)))(h, d3)


def _wgrad_down(name, a, dx, tmw):
    t, kf = a.shape
    n = dx.shape[1]
    tk = _pick(t, (1024, 512, 256, 128))

    def body(a_ref, b_ref, o_ref):
        @pl.when(pl.program_id(1) == 0)
        def _():
            o_ref[...] = jnp.zeros_like(o_ref)

        o_ref[...] += _dot(a_ref[...], b_ref[...].astype(BF16), TN)

    g = pl.pallas_call(
        body, name=name, grid=(kf // tmw, t // tk),
        in_specs=[pl.BlockSpec((tk, tmw), lambda j, s: (s, j)), pl.BlockSpec((tk, n), lambda j, s: (s, 0))],
        out_specs=pl.BlockSpec((tmw, n), lambda j, s: (j, 0)),
        out_shape=_sds((kf, n), F32),
        compiler_params=_params(("parallel", "arbitrary")))(a, dx)
    return g.reshape(N_CHIPS, kf // N_CHIPS, n)


def _shift_rows(u, k, rows):
    s = u.shape[0]
    if k > 0:
        r = pltpu.roll(u, k, 0)
        return jnp.concatenate([jnp.where(rows >= k, r[0:SUBLANES], 0.0), r[SUBLANES:]], axis=0)
    r = pltpu.roll(u, s + k, 0)
    return jnp.concatenate([r[:s - SUBLANES], jnp.where(rows < SUBLANES + k, r[s - SUBLANES:], 0.0)], axis=0)


def _conv_taps(cw_ref, got_ref):
    return (cw_ref[...] + got_ref[0]) + (got_ref[1] + got_ref[2])


def _conv_fwd(bcx, cw, cw_got, nseq, seq):
    t, d3 = bcx.shape
    d = d3 // 3
    cb = MXU_COLS
    nj = d // cb

    def body(b_ref, c_ref, x_ref, cw_ref, got_ref, z_ref):
        u = b_ref[...].astype(F32) * x_ref[...].astype(F32)
        rows = lax.broadcasted_iota(jnp.int32, (SUBLANES, cb), 0)
        cwv = _conv_taps(cw_ref, got_ref)
        y = cwv[2:3] * u + cwv[1:2] * _shift_rows(u, 1, rows) + cwv[0:1] * _shift_rows(u, 2, rows)
        z_ref[...] = (c_ref[...].astype(F32) * y).astype(BF16)

    return pl.pallas_call(
        body, name="conv_fwd", grid=(nseq, nj),
        in_specs=[pl.BlockSpec((seq, cb), lambda b, j: (b, j)),
                  pl.BlockSpec((seq, cb), lambda b, j: (b, nj + j)),
                  pl.BlockSpec((seq, cb), lambda b, j: (b, 2 * nj + j)),
                  pl.BlockSpec((SUBLANES, cb), lambda b, j: (0, j)),
                  pl.BlockSpec((3, SUBLANES, cb), lambda b, j: (0, 0, j))],
        out_specs=pl.BlockSpec((seq, cb), lambda b, j: (b, j)),
        out_shape=_sds((t, d), BF16),
        compiler_params=_params(("parallel", "parallel")))(bcx, bcx, bcx, cw, cw_got)


def _conv_bwd(dz, bcx, cw, cw_got, nseq, seq):
    t, d3 = bcx.shape
    d = d3 // 3
    cb = MXU_COLS
    nj = d // cb

    def body(dz_ref, b_ref, c_ref, x_ref, cw_ref, got_ref, o_ref, dcw_ref):
        @pl.when(pl.program_id(1) == 0)
        def _():
            dcw_ref[...] = jnp.zeros_like(dcw_ref)

        b = b_ref[...].astype(F32)
        c = c_ref[...].astype(F32)
        xv = x_ref[...].astype(F32)
        dzv = dz_ref[...].astype(F32)
        u = b * xv
        rows = lax.broadcasted_iota(jnp.int32, (SUBLANES, cb), 0)
        u1 = _shift_rows(u, 1, rows)
        u2 = _shift_rows(u, 2, rows)
        cwv = _conv_taps(cw_ref, got_ref)
        y = cwv[2:3] * u + cwv[1:2] * u1 + cwv[0:1] * u2
        dyc = dzv * c
        du = cwv[2:3] * dyc + cwv[1:2] * _shift_rows(dyc, -1, rows) + cwv[0:1] * _shift_rows(dyc, -2, rows)
        o_ref[0] = (du * xv).astype(BF16)
        o_ref[1] = (dzv * y).astype(BF16)
        o_ref[2] = (du * b).astype(BF16)
        s0 = jnp.sum(dyc * u2, axis=0, keepdims=True)
        s1 = jnp.sum(dyc * u1, axis=0, keepdims=True)
        s2 = jnp.sum(dyc * u, axis=0, keepdims=True)
        tap = lax.broadcasted_iota(jnp.int32, (3, cb), 0)
        dcw_ref[...] += jnp.where(tap == 0, s0, jnp.where(tap == 1, s1, s2))

    return pl.pallas_call(
        body, name="conv_bwd", grid=(nj, nseq),
        in_specs=[pl.BlockSpec((seq, cb), lambda j, b: (b, j)),
                  pl.BlockSpec((seq, cb), lambda j, b: (b, j)),
                  pl.BlockSpec((seq, cb), lambda j, b: (b, nj + j)),
                  pl.BlockSpec((seq, cb), lambda j, b: (b, 2 * nj + j)),
                  pl.BlockSpec((SUBLANES, cb), lambda j, b: (0, j)),
                  pl.BlockSpec((3, SUBLANES, cb), lambda j, b: (0, 0, j))],
        out_specs=[pl.BlockSpec((3, seq, cb), lambda j, b: (0, b, j)),
                   pl.BlockSpec((3, cb), lambda j, b: (0, j))],
        out_shape=[_sds((3, t, d), BF16), _sds((3, d), F32)],
        compiler_params=_params(("parallel", "arbitrary")))(dz, bcx, bcx, bcx, cw, cw_got)


def _pair_norm(x, gain_pair, low):
    sq = x * x
    ss_lo = jnp.sum(jnp.where(low, sq, 0.0), axis=-1, keepdims=True)
    ss_hi = jnp.sum(jnp.where(low, 0.0, sq), axis=-1, keepdims=True)
    r = lax.rsqrt(jnp.where(low, ss_lo, ss_hi) * (1.0 / HEAD_DIM) + EPS)
    xhat = x * r
    return xhat * gain_pair, xhat, r


KEYS = 2 * BLOCK
QK_SCALE = 1.0 / (HEAD_DIM ** 0.5)
N_PAIRS = N_Q_HEADS // 2


def _earlier_block(shape=(BLOCK, BLOCK)):
    return lax.broadcasted_iota(jnp.int32, shape, 0) > lax.broadcasted_iota(jnp.int32, shape, 1)


def _fill_bias(bias_ref):
    rows = lax.broadcasted_iota(jnp.int32, (2 * BLOCK, BLOCK), 0)
    qi = lax.broadcasted_iota(jnp.int32, (2 * BLOCK, BLOCK), 1)
    odd_head = rows >= BLOCK
    kj = jnp.where(odd_head, rows - BLOCK, rows)
    earlier = kj > qi
    dist = (jnp.where(earlier, BLOCK, 0) + qi - kj).astype(F32)
    for j in range(N_PAIRS):
        slope = jnp.where(odd_head, ALIBI_SLOPES[2 * j + 1], ALIBI_SLOPES[2 * j])
        bias = -slope * dist
        bias_ref[1, j] = bias
        bias_ref[0, j] = jnp.where(earlier, -1e30, bias)


def _merge_blocks(x_t, earlier):
    return jnp.concatenate([jnp.where(earlier, x_t[e * KEYS:e * KEYS + BLOCK], x_t[e * KEYS + BLOCK:(e + 1) * KEYS])
                            for e in range(2)], axis=0)


def _split_blocks(heads, earlier):
    parts = []
    for x in heads:
        parts += [jnp.where(earlier, x, 0.0), jnp.where(earlier, 0.0, x)]
    return jnp.concatenate(parts, axis=0).astype(BF16)


def _kv_pair_rows(kv_tile, parity, low):
    own = jnp.where(low if parity == 0 else jnp.logical_not(low), kv_tile, 0.0)
    other = pltpu.roll(own, HEAD_DIM, 1)
    lo, hi = (own, other) if parity == 0 else (other, own)
    return jnp.concatenate([lo, hi], axis=0).astype(BF16)


def _pair_softmax(s_t, sink_even, sink_odd):
    out = []
    for e, sink in enumerate((sink_even, sink_odd)):
        se = s_t[e * BLOCK:(e + 1) * BLOCK]
        m = jnp.maximum(jnp.max(se, axis=0, keepdims=True), sink)
        ee = jnp.exp(se - m)
        es = jnp.exp(sink - m)
        inv = 1.0 / (jnp.sum(ee, axis=0, keepdims=True) + es)
        out.append((ee * inv, es * inv))
    return out


def _attn_rows(n):
    q0 = pl.multiple_of(n * BLOCK, BLOCK)
    k0 = pl.multiple_of(jnp.maximum(n - 1, 0) * BLOCK, BLOCK)
    return q0, k0, jnp.minimum(n, 1)


def _key_rows(qkv_ref, k0, q0, col):
    return jnp.concatenate([qkv_ref[pl.ds(k0, BLOCK), col:col + LANES], qkv_ref[pl.ds(q0, BLOCK), col:col + LANES]],
                           axis=0).astype(F32)


def _attn_fwd(qkv, qg_pair, kg_pair, sinks, nseq, seq):
    t = qkv.shape[0]
    dq = N_Q_HEADS * HEAD_DIM
    dkv = N_KV_HEADS * HEAD_DIM

    def body(sk_ref, qkv_ref, qg_ref, kg_ref, o_ref, bias_ref):
        @pl.when(pl.program_id(0) == 0)
        def _():
            _fill_bias(bias_ref)

        low = lax.broadcasted_iota(jnp.int32, (1, LANES), 1) < HEAD_DIM
        earlier = _earlier_block()
        qg = qg_ref[...] * QK_SCALE
        kg = kg_ref[...]

        def blk(n, carry):
            q0, k0, later = _attn_rows(n)
            for kt in range(dkv // LANES):
                kraw = _key_rows(qkv_ref, k0, q0, dq + kt * LANES)
                vraw = _key_rows(qkv_ref, k0, q0, dq + dkv + kt * LANES)
                kn, _, _ = _pair_norm(kraw, kg, low)
                for par in range(2):
                    kh = 2 * kt + par
                    k_pair = _kv_pair_rows(kn, par, low)
                    v_pair = _kv_pair_rows(vraw, par, low)
                    for jj in range(2):
                        j = 2 * kh + jj
                        qraw = qkv_ref[pl.ds(q0, BLOCK), j * LANES:(j + 1) * LANES].astype(F32)
                        qn, _, _ = _pair_norm(qraw, qg, low)
                        s_t = _merge_blocks(_dot(k_pair, qn.astype(BF16), NT), earlier) + bias_ref[later, j]
                        (p0, _), (p1, _) = _pair_softmax(s_t, sk_ref[0, 2 * j], sk_ref[0, 2 * j + 1])
                        p_t = _split_blocks((p0, p1), earlier)
                        o_ref[pl.ds(q0, BLOCK), j * LANES:(j + 1) * LANES] = _dot(p_t, v_pair, TN).astype(BF16)
            return carry

        lax.fori_loop(0, seq // BLOCK, blk, 0)

    return pl.pallas_call(
        body, name="attn_fwd", grid=(nseq,),
        in_specs=[pl.BlockSpec(memory_space=pltpu.SMEM),
                  pl.BlockSpec((seq, dq + 2 * dkv), lambda b: (b, 0)),
                  pl.BlockSpec((1, LANES), lambda b: (0, 0)),
                  pl.BlockSpec((1, LANES), lambda b: (0, 0))],
        out_specs=pl.BlockSpec((seq, dq), lambda b: (b, 0)),
        out_shape=_sds((t, dq), BF16),
        scratch_shapes=[pltpu.VMEM((2, N_PAIRS, 2 * BLOCK, BLOCK), F32)],
        compiler_params=_params(("arbitrary",)))(sinks, qkv, qg_pair, kg_pair)


def _attn_bwd(do, qkv, qg_pair, kg_pair, sinks, nseq, seq):
    t = qkv.shape[0]
    dq = N_Q_HEADS * HEAD_DIM
    dkv = N_KV_HEADS * HEAD_DIM

    def body(sk_ref, do_ref, qkv_ref, qg_ref, kg_ref, o_ref, dqg_ref, dkg_ref, dsk_ref, acc_ref, bias_ref):
        @pl.when(pl.program_id(0) == 0)
        def _():
            _fill_bias(bias_ref)
            dqg_ref[...] = jnp.zeros_like(dqg_ref)
            dkg_ref[...] = jnp.zeros_like(dkg_ref)
            dsk_ref[...] = jnp.zeros_like(dsk_ref)

        acc_ref[...] = jnp.zeros_like(acc_ref)
        low = lax.broadcasted_iota(jnp.int32, (1, LANES), 1) < HEAD_DIM
        earlier = _earlier_block()
        head_row = lax.broadcasted_iota(jnp.int32, (N_Q_HEADS, LANES), 0)
        qg = qg_ref[...] * QK_SCALE
        kg = kg_ref[...]

        def blk(n, carry):
            dqg_acc, dkg_acc, dsk_acc = carry
            q0, k0, later = _attn_rows(n)
            for kt in range(dkv // LANES):
                kraw = _key_rows(qkv_ref, k0, q0, dq + kt * LANES)
                vraw = _key_rows(qkv_ref, k0, q0, dq + dkv + kt * LANES)
                kn, khat, rk = _pair_norm(kraw, kg, low)
                dk_tile = None
                dv_tile = None
                for par in range(2):
                    kh = 2 * kt + par
                    own = low if par == 0 else jnp.logical_not(low)
                    k_pair = _kv_pair_rows(kn, par, low)
                    v_pair = _kv_pair_rows(vraw, par, low)
                    dkn_rows = jnp.zeros((2 * KEYS, LANES), F32)
                    dv_rows = jnp.zeros((2 * KEYS, LANES), F32)
                    for jj in range(2):
                        j = 2 * kh + jj
                        qraw = qkv_ref[pl.ds(q0, BLOCK), j * LANES:(j + 1) * LANES].astype(F32)
                        qn, qhat, rq = _pair_norm(qraw, qg, low)
                        qn_b = qn.astype(BF16)
                        do_b = do_ref[pl.ds(q0, BLOCK), j * LANES:(j + 1) * LANES]
                        s_t = _merge_blocks(_dot(k_pair, qn_b, NT), earlier) + bias_ref[later, j]
                        dp_t = _merge_blocks(_dot(v_pair, do_b, NT), earlier)
                        ds_heads = []
                        probs = _pair_softmax(s_t, sk_ref[0, 2 * j], sk_ref[0, 2 * j + 1])
                        for e, (p, ps) in enumerate(probs):
                            dp = dp_t[e * BLOCK:(e + 1) * BLOCK]
                            dsum = jnp.sum(p * dp, axis=0, keepdims=True)
                            ds_heads.append(p * (dp - dsum))
                            dsk_acc = dsk_acc - jnp.where(head_row == 2 * j + e, ps * dsum, 0.0)
                        p_t = _split_blocks((probs[0][0], probs[1][0]), earlier)
                        ds_t = _split_blocks(ds_heads, earlier)
                        dv_rows = dv_rows + _dot(p_t, do_b, NN)
                        dkn_rows = dkn_rows + _dot(ds_t, qn_b, NN)
                        dqn = _dot(ds_t, k_pair, TN)
                        dqg_acc = dqg_acc + jnp.sum(dqn * qhat, axis=0, keepdims=True)
                        dqhat = dqn * qg
                        prod = dqhat * qhat
                        m_lo = jnp.sum(jnp.where(low, prod, 0.0), axis=-1, keepdims=True)
                        m_hi = jnp.sum(jnp.where(low, 0.0, prod), axis=-1, keepdims=True)
                        mean = jnp.where(low, m_lo, m_hi) * (1.0 / HEAD_DIM)
                        o_ref[pl.ds(q0, BLOCK), j * LANES:(j + 1) * LANES] = (rq * (dqhat - qhat * mean)).astype(BF16)
                    dkn_acc = jnp.where(low, dkn_rows[0:KEYS], dkn_rows[KEYS:2 * KEYS])
                    dv_acc = jnp.where(low, dv_rows[0:KEYS], dv_rows[KEYS:2 * KEYS])
                    dkn = dkn_acc + pltpu.roll(dkn_acc, HEAD_DIM, 1)
                    dvh = dv_acc + pltpu.roll(dv_acc, HEAD_DIM, 1)
                    khat_own = jnp.where(own, khat, 0.0)
                    khat_dup = khat_own + pltpu.roll(khat_own, HEAD_DIM, 1)
                    dkg_acc = dkg_acc + jnp.sum(jnp.where(own, dkn * khat_dup, 0.0), axis=0, keepdims=True)
                    dkhat = dkn * kg
                    mean_k = jnp.sum(dkhat * khat_dup, axis=-1, keepdims=True) * (1.0 / LANES)
                    dk_raw = rk * (dkhat - khat_dup * mean_k)
                    dk_tile = jnp.where(own, dk_raw, 0.0) if dk_tile is None else jnp.where(own, dk_raw, dk_tile)
                    dv_tile = jnp.where(own, dvh, 0.0) if dv_tile is None else jnp.where(own, dvh, dv_tile)
                for r0, part in ((k0, slice(0, BLOCK)), (q0, slice(BLOCK, KEYS))):
                    acc_ref[pl.ds(r0, BLOCK), kt * LANES:(kt + 1) * LANES] += dk_tile[part]
                    acc_ref[pl.ds(r0, BLOCK), dkv + kt * LANES:dkv + (kt + 1) * LANES] += dv_tile[part]
            return dqg_acc, dkg_acc, dsk_acc

        zero = jnp.zeros((1, LANES), F32)
        carry = (zero, zero, jnp.zeros((N_Q_HEADS, LANES), F32))
        dqg_acc, dkg_acc, dsk_acc = lax.fori_loop(0, seq // BLOCK, blk, carry)
        dqg_ref[...] += dqg_acc * QK_SCALE
        dkg_ref[...] += dkg_acc
        dsk_ref[...] += dsk_acc
        o_ref[:, dq:dq + 2 * dkv] = acc_ref[...].astype(BF16)

    small = pl.BlockSpec((1, LANES), lambda b: (0, 0))
    heads = pl.BlockSpec((N_Q_HEADS, LANES), lambda b: (0, 0))
    return pl.pallas_call(
        body, name="attn_bwd", grid=(nseq,),
        in_specs=[pl.BlockSpec(memory_space=pltpu.SMEM),
                  pl.BlockSpec((seq, dq), lambda b: (b, 0)),
                  pl.BlockSpec((seq, dq + 2 * dkv), lambda b: (b, 0)),
                  small, small],
        out_specs=[pl.BlockSpec((seq, dq + 2 * dkv), lambda b: (b, 0)), small, small, heads],
        out_shape=[_sds((t, dq + 2 * dkv), BF16), _sds((1, LANES), F32), _sds((1, LANES), F32),
                   _sds((N_Q_HEADS, LANES), F32)],
        scratch_shapes=[pltpu.VMEM((seq, 2 * dkv), F32), pltpu.VMEM((2, N_PAIRS, 2 * BLOCK, BLOCK), F32)],
        compiler_params=_params(("arbitrary",)))(sinks, do, qkv, qg_pair, kg_pair)


def _place():
    x, y, c = lax.axis_index("x"), lax.axis_index("y"), lax.axis_index("c")
    other_chips = [(1 - x, y), (x, 1 - y), (1 - x, 1 - y)]
    return x, y, c, other_chips


def _half_rows(c, rows):
    rh = rows // 2
    return pl.ds(pl.multiple_of(c * rh, BF16_ROWS), rh)


def _cast_own(name, w, place, layer=None):
    nl, r, cdim = w.shape
    first = 0
    if layer is not None:
        nl, first = 1, layer
    rt = _row_tile(r, 4 * cdim, ELEMENTWISE_BLOCK)

    def body(s_ref, w_ref, o_ref):
        o_ref[...] = w_ref[...].astype(BF16)

    grid_spec = pltpu.PrefetchScalarGridSpec(
        num_scalar_prefetch=1, grid=(nl, r // rt),
        in_specs=[pl.BlockSpec((None, rt, cdim), lambda l, i, s: (first + l, i, 0))],
        out_specs=pl.BlockSpec((None, None, rt, cdim), lambda l, i, s: (l, s[1], i, 0)))
    return pl.pallas_call(
        body, name=name, grid_spec=grid_spec, out_shape=_sds((nl, N_CHIPS, r, cdim), BF16),
        compiler_params=_params(("parallel", "parallel")))(place, w)


def _gather_protocol(outs, shapes, send_sems, recv_sems):
    n = len(outs)
    x, y, c, other_chips = _place()
    me_chip = 2 * x + y
    sibling = (x, y, 1 - c)

    def rows(u, chip, half):
        return outs[u].at[:, chip, _half_rows(half, shapes[u][2]), :]

    def copy(sem, part, to):
        return pltpu.make_async_remote_copy(src_ref=part, dst_ref=part, send_sem=send_sems.at[sem],
                                            recv_sem=recv_sems.at[sem], device_id=to, device_id_type=MESH)

    sends = []
    for u in range(n):
        for k, chip in enumerate(other_chips):
            cp = copy(6 * u + k, rows(u, me_chip, c), (*chip, c))
            cp.start()
            sends.append(cp)
    for u in range(n):
        for k, chip in enumerate(other_chips):
            got = rows(u, 2 * chip[0] + chip[1], c)
            copy(6 * u + k, got, (*chip, c)).wait_recv()
            cp = copy(6 * u + 3 + k, got, sibling)
            cp.start()
            sends.append(cp)
    for u in range(n):
        for k, chip in enumerate(other_chips):
            copy(6 * u + 3 + k, rows(u, 2 * chip[0] + chip[1], 1 - c), sibling).wait_recv()
    for cp in sends:
        cp.wait_send()


def _hbm_ref(a):
    return jax.new_ref(a, memory_space=pltpu.MemorySpace.HBM)


def _sibling_peer():
    x, y, c, _ = _place()
    return [(x, y, 1 - c)]


def _chip_peers():
    x, y, c, other_chips = _place()
    return [(*chip, c) for chip in other_chips]


def _gather_peers():
    return _chip_peers() + _sibling_peer()


def _on_sequencer(name, collective_id, n_sems, peers, protocol, operands=(), out_types=()):
    n_in, n_out = len(operands), len(out_types)

    def launch(*refs):
        send_sems, recv_sems = refs[n_in + n_out:]
        barrier = pltpu.get_barrier_semaphore()
        targets = peers()
        for peer in targets:
            pl.semaphore_signal(barrier, inc=1, device_id=peer, device_id_type=MESH)
        pl.semaphore_wait(barrier, len(targets))
        protocol(refs[:n_in], refs[n_in:n_in + n_out], send_sems, recv_sems)

    return pl.kernel(
        launch, out_type=tuple(out_types), mesh=plsc.ScalarSubcoreMesh(axis_name="sequencer", num_cores=1), name=name,
        scratch_types=(pltpu.SemaphoreType.DMA((n_sems,)), pltpu.SemaphoreType.DMA((n_sems,))),
        compiler_params=pltpu.CompilerParams(collective_id=collective_id))(*operands)


def _seq_allgather(name, collective_id, bufs):
    shapes = [b.shape for b in bufs]
    refs = [_hbm_ref(b) for b in bufs]
    _on_sequencer(name, collective_id, 6 * len(bufs), _gather_peers,
                  lambda ins, outs, send_sems, recv_sems: _gather_protocol(refs, shapes, send_sems, recv_sems))
    return [r[...] for r in refs]


def _taps_protocol(block_ref, got_ref, send_sems, recv_sems, first_sem):
    x, y, c, other_chips = _place()
    copies = []
    for k, chip in enumerate(other_chips):
        cp = pltpu.make_async_remote_copy(src_ref=block_ref, dst_ref=got_ref.at[k], send_sem=send_sems.at[first_sem + k],
                                          recv_sem=recv_sems.at[first_sem + k], device_id=(*chip, c), device_id_type=MESH)
        cp.start()
        copies.append(cp)
    return copies


def _seq_allgather_conv(collective_id, bufs, cw_block):
    shapes = [b.shape for b in bufs]
    refs = [_hbm_ref(b) for b in bufs]

    def protocol(ins, outs, send_sems, recv_sems):
        taps = _taps_protocol(ins[0], outs[0], send_sems, recv_sems, 6 * len(bufs))
        _gather_protocol(refs, shapes, send_sems, recv_sems)
        for cp in taps:
            cp.wait_recv()
        for cp in taps:
            cp.wait_send()

    (got,) = _on_sequencer("allgather_conv", collective_id, 6 * len(bufs) + 3, _gather_peers, protocol,
                           operands=(cw_block,), out_types=(_sds((3, *cw_block.shape), F32),))
    return [r[...] for r in refs], got


def _exchange_protocol(gs, outs, shapes, send_sems, recv_sems):
    x, y, c, _ = _place()
    sends = []
    for u in range(len(gs)):
        cp = pltpu.make_async_remote_copy(
            src_ref=gs[u].at[:, _half_rows(1 - c, shapes[u][1]), :], dst_ref=outs[u],
            send_sem=send_sems.at[u], recv_sem=recv_sems.at[u], device_id=(x, y, 1 - c), device_id_type=MESH)
        cp.start()
        sends.append(cp)
    for cp in sends:
        cp.wait_recv()
    for cp in sends:
        cp.wait_send()


def _seq_exchange(name, collective_id, grads):
    shapes = [g.shape for g in grads]
    return _on_sequencer(
        name, collective_id, len(grads), _sibling_peer,
        lambda gs, outs, send_sems, recv_sems: _exchange_protocol(gs, outs, shapes, send_sems, recv_sems),
        operands=grads, out_types=[_sds((s[0], s[1] // 2, s[2]), F32) for s in shapes])


def _sum_halves(name, g, got, place, after):
    _, r, cdim = g.shape
    rh = r // 2
    rt = _row_tile(rh, 4 * cdim, ELEMENTWISE_BLOCK)
    nr = rh // rt

    def body(s_ref, g_ref, got_ref, after_ref, pb_ref, pf_ref):
        s = g_ref[...] + got_ref[...]
        pb_ref[...] = s.astype(BF16)

        @pl.when(pl.program_id(1) == s_ref[1])
        def _():
            pf_ref[...] = s

    grid_spec = pltpu.PrefetchScalarGridSpec(
        num_scalar_prefetch=1, grid=(nr, N_CHIPS),
        in_specs=[pl.BlockSpec((None, rt, cdim), lambda i, q, s: (q, s[0] * nr + i, 0)),
                  pl.BlockSpec((None, rt, cdim), lambda i, q, s: (q, i, 0)),
                  pl.BlockSpec(memory_space=pl.ANY)],
        out_specs=[pl.BlockSpec((None, rt, cdim), lambda i, q, s: (q, i, 0)),
                   pl.BlockSpec((rt, cdim), lambda i, q, s: (i, 0))])
    return pl.pallas_call(
        body, name=name, grid_spec=grid_spec,
        out_shape=[_sds((N_CHIPS, rh, cdim), BF16), _sds((rh, cdim), F32)],
        compiler_params=_params(("parallel", "arbitrary")))(place, g, got, after)


def _scatter_protocol(ps, outs, send_sems, recv_sems):
    x, y, c, other_chips = _place()
    sends = []
    for u in range(len(ps)):
        for k, chip in enumerate(other_chips):
            cp = pltpu.make_async_remote_copy(
                src_ref=ps[u].at[2 * chip[0] + chip[1]], dst_ref=outs[u].at[k],
                send_sem=send_sems.at[3 * u + k], recv_sem=recv_sems.at[3 * u + k],
                device_id=(*chip, c), device_id_type=MESH)
            cp.start()
            sends.append(cp)
    for cp in sends:
        cp.wait_recv()
    for cp in sends:
        cp.wait_send()


def _seq_scatter(name, collective_id, partials):
    return _on_sequencer(
        name, collective_id, 3 * len(partials), _chip_peers, _scatter_protocol,
        operands=partials, out_types=[_sds((3, p.shape[1], p.shape[2]), BF16) for p in partials])


def _sum_partials(name, own, got, place, layer, nl, prev, after):
    rh, cdim = own.shape
    rt = _row_tile(rh, 4 * cdim, ELEMENTWISE_BLOCK)
    nr = rh // rt

    def body(s_ref, own_ref, got_ref, *rest):
        o_ref = rest[-1]
        o_ref[...] = ((own_ref[...] + got_ref[0].astype(F32)) + got_ref[1].astype(F32)) + got_ref[2].astype(F32)

    in_specs = [pl.BlockSpec((rt, cdim), lambda i, s: (i, 0)), pl.BlockSpec((3, rt, cdim), lambda i, s: (0, i, 0)),
                pl.BlockSpec(memory_space=pl.ANY)]
    args = [place, own, got, after]
    aliases = {}
    if prev is not None:
        in_specs.append(pl.BlockSpec(memory_space=pl.ANY))
        args.append(prev)
        aliases = {4: 0}
    grid_spec = pltpu.PrefetchScalarGridSpec(
        num_scalar_prefetch=1, grid=(nr,), in_specs=in_specs,
        out_specs=pl.BlockSpec((None, rt, cdim), lambda i, s: (layer, s[0] * nr + i, 0)))
    return pl.pallas_call(
        body, name=name, grid_spec=grid_spec, out_shape=_sds((nl, 2 * rh, cdim), F32),
        input_output_aliases=aliases, compiler_params=_params(("parallel",)))(*args)


def _share_protocol(outs, shapes, units, send_sems, recv_sems):
    x, y, c, _ = _place()
    sends = []
    for u, (w, l) in enumerate(units):
        mine = outs[w].at[l, _half_rows(c, shapes[w][1]), :]
        cp = pltpu.make_async_remote_copy(src_ref=mine, dst_ref=mine, send_sem=send_sems.at[u],
                                          recv_sem=recv_sems.at[u], device_id=(x, y, 1 - c), device_id_type=MESH)
        cp.start()
        sends.append(cp)
    for u, (w, l) in enumerate(units):
        theirs = outs[w].at[l, _half_rows(1 - c, shapes[w][1]), :]
        pltpu.make_async_remote_copy(src_ref=theirs, dst_ref=theirs, send_sem=send_sems.at[u],
                                     recv_sem=recv_sems.at[u], device_id=(x, y, 1 - c),
                                     device_id_type=MESH).wait_recv()
    for cp in sends:
        cp.wait_send()


def _seq_share(name, collective_id, bufs):
    shapes = [b.shape for b in bufs]
    units = [(w, l) for w in range(len(bufs)) for l in range(shapes[w][0])]
    refs = [_hbm_ref(b) for b in bufs]
    _on_sequencer(name, collective_id, len(units), _sibling_peer,
                  lambda ins, outs, send_sems, recv_sems: _share_protocol(refs, shapes, units, send_sems, recv_sems))
    return [r[...] for r in refs]


def _gather_blocks(block_ref, all_ref, send_sems, recv_sems):
    x, y, c, _ = _place()
    me = 4 * x + 2 * y + c
    all_ref[me] = block_ref[...]
    sends = []
    for rel in range(1, 8):
        fx, fy, fc = (rel >> 2) & 1, (rel >> 1) & 1, rel & 1
        peer = (x ^ fx, y ^ fy, c ^ fc)
        cp = pltpu.make_async_remote_copy(src_ref=block_ref, dst_ref=all_ref.at[me], send_sem=send_sems.at[rel - 1],
                                          recv_sem=recv_sems.at[rel - 1], device_id=peer, device_id_type=MESH)
        cp.start()
        sends.append(cp)
    for cp in sends:
        cp.wait_recv()
    for cp in sends:
        cp.wait_send()


def _adam(w, g, m, v):
    m_new = ADAM_B1 * m + (1.0 - ADAM_B1) * g
    v_new = ADAM_B2 * v + (1.0 - ADAM_B2) * (g * g)
    m_hat = m_new / (1.0 - ADAM_B1 ** ADAM_STEP)
    v_hat = v_new / (1.0 - ADAM_B2 ** ADAM_STEP)
    delta = -ADAM_LR * (m_hat / (jnp.sqrt(v_hat) + ADAM_EPS) + ADAM_WD * w)
    return delta, m_new, v_new


def _small_step(dnm0, dnm1, dnf0, dnf1, dcw, dqg, dkg, dsk, loss, w_blk, m_blk, v_blk, cw_cols):
    d = w_blk.shape[1]

    def body(dnm0_ref, dnm1_ref, dnf0_ref, dnf1_ref, dcw_ref, dqg_ref, dkg_ref, dsk_ref, loss_ref,
             w_ref, m_ref, v_ref, *rest):
        out_refs, (blk_ref, all_ref, send_sems, recv_sems) = rest[:-4], rest[-4:]
        blk_ref[...] = jnp.zeros_like(blk_ref)
        blk_ref[0:1, :] = jnp.sum(dnm0_ref[...], axis=0, keepdims=True)
        blk_ref[1:2, :] = jnp.sum(dnm1_ref[...], axis=0, keepdims=True)
        blk_ref[8:9, :] = jnp.sum(dnf0_ref[...], axis=0, keepdims=True)
        blk_ref[9:10, :] = jnp.sum(dnf1_ref[...], axis=0, keepdims=True)
        blk_ref[16:19, :] = dcw_ref[...]
        dqg_v = dqg_ref[...]
        dkg_v = dkg_ref[...]
        blk_ref[24:25, 0:LANES] = dqg_v + pltpu.roll(dqg_v, HEAD_DIM, 1)
        blk_ref[24:25, LANES:2 * LANES] = dkg_v + pltpu.roll(dkg_v, HEAD_DIM, 1)
        for h in range(N_Q_HEADS):
            blk_ref[24:25, 2 * LANES + h:2 * LANES + h + 1] = jnp.sum(dsk_ref[h:h + 1, :], axis=1, keepdims=True)
        blk_ref[24:25, 3 * LANES:4 * LANES] = jnp.broadcast_to(loss_ref[...], (1, LANES))
        _gather_blocks(blk_ref, all_ref, send_sems, recv_sems)
        g = all_ref[0]
        for dev in range(1, 8):
            g = g + all_ref[dev]
        out_refs[0][...] = g[24:25, 3 * LANES:3 * LANES + 1]
        chip = 2 * lax.axis_index("x") + lax.axis_index("y")
        for i, blk in enumerate((g, *_adam(w_ref[...], g, m_ref[...], v_ref[...]))):
            nm_ref, nf_ref, cw_ref, qg_ref, kg_ref, sk_ref = out_refs[1 + 6 * i:7 + 6 * i]
            nm_ref[...] = blk[0:2]
            nf_ref[...] = blk[8:10]
            qg_ref[...] = blk[24:25, 0:HEAD_DIM]
            kg_ref[...] = blk[24:25, LANES:LANES + HEAD_DIM]
            sk_ref[...] = blk[24:25, 2 * LANES:2 * LANES + N_Q_HEADS]
            for q in range(N_CHIPS):
                @pl.when(chip == q)
                def _(blk=blk, cw_ref=cw_ref, q=q):
                    cw_ref[0] = blk[16:19, q * cw_cols:(q + 1) * cw_cols]

    vm = pl.BlockSpec(memory_space=pltpu.VMEM)
    group = [_sds((2, d), F32), _sds((2, d), F32), _sds((1, 3, cw_cols), F32), _sds((1, HEAD_DIM), F32),
             _sds((1, HEAD_DIM), F32), _sds((1, N_Q_HEADS), F32)]
    outs = pl.pallas_call(
        body, name="small_step", in_specs=[vm] * 12, out_specs=[vm] * 25, out_shape=[_sds((1, 1), F32)] + group * 4,
        scratch_shapes=[pltpu.VMEM((SMALL_ROWS, d), F32), pltpu.VMEM((8, SMALL_ROWS, d), F32),
                        pltpu.SemaphoreType.DMA((7,)), pltpu.SemaphoreType.DMA((7,))],
    )(dnm0, dnm1, dnf0, dnf1, dcw, dqg, dkg, dsk, loss, w_blk, m_blk, v_blk)
    names = ("norm_mixer", "norm_ffn", "conv_w", "attn_q_gain", "attn_k_gain", "attn_sinks")
    return outs[0], [dict(zip(names, outs[1 + 6 * i:7 + 6 * i])) for i in range(4)]


def _adam_step(name, w, g, m, v):
    nl, r, cdim = w.shape
    rt = _row_tile(r, 4 * cdim, ELEMENTWISE_BLOCK)

    def body(w_ref, g_ref, m_ref, v_ref, d_ref, mo_ref, vo_ref):
        delta, m_new, v_new = _adam(w_ref[...], g_ref[...], m_ref[...], v_ref[...])
        d_ref[...] = delta
        mo_ref[...] = m_new
        vo_ref[...] = v_new

    spec = pl.BlockSpec((None, rt, cdim), lambda l, i: (l, i, 0))
    return pl.pallas_call(
        body, name=name, grid=(nl, r // rt), in_specs=[spec] * 4, out_specs=[spec] * 3,
        out_shape=[_sds(w.shape, F32)] * 3,
        compiler_params=_params(("parallel", "parallel")))(w, g, m, v)


def _pad_rows(a, rows=SUBLANES):
    return jnp.pad(a, ((0, rows - a.shape[0]), (0, 0)))


def _small_block(nm, nf, cw_local, qg, kg, sk, chip):
    d = nm.shape[1]
    cw_rows = lax.dynamic_update_slice(jnp.zeros((SUBLANES, d), F32), cw_local, (0, chip * cw_local.shape[1]))
    misc = jnp.concatenate([qg, qg, kg, kg, jnp.pad(sk, ((0, 0), (0, LANES - sk.shape[1]))),
                            jnp.zeros((1, d - 3 * LANES), F32)], axis=1)
    return jnp.concatenate([_pad_rows(nm), _pad_rows(nf), cw_rows, _pad_rows(misc)], axis=0)


WEIGHT_NAMES = ("conv_w_in", "conv_w", "conv_w_out", "attn_w_qkv", "attn_q_gain", "attn_k_gain", "attn_sinks",
                "attn_w_o", "norm_mixer", "norm_ffn", "ffn_w_gate_up", "ffn_w_down")
BIG = ("conv_w_in", "conv_w_out", "attn_w_qkv", "attn_w_o", "ffn_w_gate_up", "ffn_w_down")


def kernel(x, conv_w_in, conv_w, conv_w_out, attn_w_qkv, attn_q_gain, attn_k_gain, attn_sinks, attn_w_o, norm_mixer, norm_ffn, ffn_w_gate_up, ffn_w_down, loss_target, m_conv_w_in, m_conv_w, m_conv_w_out, m_attn_w_qkv, m_attn_q_gain, m_attn_k_gain, m_attn_sinks, m_attn_w_o, m_norm_mixer, m_norm_ffn, m_ffn_w_gate_up, m_ffn_w_down, v_conv_w_in, v_conv_w, v_conv_w_out, v_attn_w_qkv, v_attn_q_gain, v_attn_k_gain, v_attn_sinks, v_attn_w_o, v_norm_mixer, v_norm_ffn, v_ffn_w_gate_up, v_ffn_w_down):
    w = dict(conv_w_in=conv_w_in, conv_w=conv_w, conv_w_out=conv_w_out, attn_w_qkv=attn_w_qkv,
             attn_q_gain=attn_q_gain, attn_k_gain=attn_k_gain, attn_sinks=attn_sinks, attn_w_o=attn_w_o,
             norm_mixer=norm_mixer, norm_ffn=norm_ffn, ffn_w_gate_up=ffn_w_gate_up, ffn_w_down=ffn_w_down)
    m = dict(conv_w_in=m_conv_w_in, conv_w=m_conv_w, conv_w_out=m_conv_w_out, attn_w_qkv=m_attn_w_qkv,
             attn_q_gain=m_attn_q_gain, attn_k_gain=m_attn_k_gain, attn_sinks=m_attn_sinks, attn_w_o=m_attn_w_o,
             norm_mixer=m_norm_mixer, norm_ffn=m_norm_ffn, ffn_w_gate_up=m_ffn_w_gate_up, ffn_w_down=m_ffn_w_down)
    v = dict(conv_w_in=v_conv_w_in, conv_w=v_conv_w, conv_w_out=v_conv_w_out, attn_w_qkv=v_attn_w_qkv,
             attn_q_gain=v_attn_q_gain, attn_k_gain=v_attn_k_gain, attn_sinks=v_attn_sinks, attn_w_o=v_attn_w_o,
             norm_mixer=v_norm_mixer, norm_ffn=v_norm_ffn, ffn_w_gate_up=v_ffn_w_gate_up, ffn_w_down=v_ffn_w_down)

    nseq, seq, d = x.shape
    t = nseq * seq
    chip = 2 * lax.axis_index("x") + lax.axis_index("y")
    core = lax.axis_index("c")
    place = jnp.stack([core, chip]).astype(jnp.int32)
    x0 = x.reshape(t, d)
    tgt = loss_target.reshape(t, d)

    cw_block = lax.dynamic_update_slice(jnp.zeros((SUBLANES, d), F32), conv_w[0], (0, chip * conv_w.shape[2]))
    def cast(k, layer=None):
        return _cast_own(f"cast_{k}" + ("" if layer is None else str(layer)), w[k], place, layer)

    (w_in, w_out), cw_got = _seq_allgather_conv(1, [cast("conv_w_in"), cast("conv_w_out")], cw_block)
    w_gu0, w_dn0 = _seq_allgather("allgather_ffn0", 2, [cast("ffn_w_gate_up", 0), cast("ffn_w_down", 0)])
    w_qkv, w_o, w_gu1, w_dn1 = _seq_allgather(
        "allgather_rest", 3, [cast("attn_w_qkv"), cast("attn_w_o"), cast("ffn_w_gate_up", 1), cast("ffn_w_down", 1)])
    w_out = w_out.reshape(1, d, d)
    w_o = w_o.reshape(1, d, d)
    w_gu = [w_gu0, w_gu1]
    w_dn = [w_dn0.reshape(1, D_FF, d), w_dn1.reshape(1, D_FF, d)]

    qg_pair = jnp.concatenate([attn_q_gain, attn_q_gain], axis=1)
    kg_pair = jnp.concatenate([attn_k_gain, attn_k_gain], axis=1)

    def ffn_bwd(i, dxo, xin, h, g, u, a):
        g_dn = _wgrad_down(f"ffn{i}_down_wgrad", a, dxo, D_FF // 2)
        dg, du = _mm_down_t_swiglu(f"ffn{i}_down_dgrad", dxo, w_dn[i], 0, g, u)
        g_gu = _wgrad_up2(f"ffn{i}_up_wgrad", h, dg, du)
        dxi, dgain = _dgrad_norm_ffn(f"ffn{i}_up_dgrad", dg, du, w_gu[i], 0, xin, norm_ffn[i:i + 1], dxo)
        return dxi, dgain, g_gu, g_dn

    h0, bcx = _mm_norm_up_joined("conv_in", x0, norm_mixer[0:1], w_in, 512)
    z = _conv_fwd(bcx, cw_block, cw_got, nseq, seq)
    x1, h1 = _mm_down_norm("conv_out", z, w_out, 0, x0, norm_ffn[0:1])
    g0, u0, a0 = _mm_up_swiglu("ffn0_up", h1, w_gu[0], 0)
    x2, h2 = _mm_down_norm("ffn0_down", a0, w_dn[0], 0, x1, norm_mixer[1:2])
    qkv = _mm_up_joined("attn_qkv", h2, w_qkv, 1024)
    o = _attn_fwd(qkv, qg_pair, kg_pair, attn_sinks, nseq, seq)
    x3, h3 = _mm_down_norm("attn_out", o, w_o, 0, x2, norm_ffn[1:2])
    g1, u1, a1 = _mm_up_swiglu("ffn1_up", h3, w_gu[1], 0)
    dy, loss_part = _mm_down_loss("ffn1_down", a1, w_dn[1], 0, x3, tgt)

    finished = {k: None for k in BIG}

    def exchange(tag, cid, units):
        return units, _seq_exchange(f"exchange_{tag}", cid, [g for _, _, g in units])

    def scatter(tag, cid, group, after):
        units, got = group
        sums = [_sum_halves(f"sum_halves_{k}{l}", g, r, place, after) for (k, l, g), r in zip(units, got)]
        return units, sums, _seq_scatter(f"scatter_{tag}", cid, [pb for pb, _ in sums])

    def finish(group, after):
        units, sums, arrived = group
        for (k, l, _), (_, pf), r in zip(units, sums, arrived):
            finished[k] = _sum_partials(f"sum_partials_{k}{l}", pf, r, place, l, w[k].shape[0], finished[k], after)

    dx3, dnf1, g_gu1, g_dn1 = ffn_bwd(1, dy, x3, h3, g1, u1, a1)
    ffn1 = exchange("ffn1", 4, [("ffn_w_down", 1, g_dn1), ("ffn_w_gate_up", 1, g_gu1)])
    g_o = _wgrad_down("attn_out_wgrad", o, dx3, d)
    do = _mm_down_t("attn_out_dgrad", dx3, w_o, 0)
    ffn1 = scatter("ffn1", 8, ffn1, do)
    dqkv, dqg, dkg, dsk = _attn_bwd(do, qkv, qg_pair, kg_pair, attn_sinks, nseq, seq)
    g_qkv = _wgrad_joined("attn_qkv_wgrad", h2, dqkv)
    attn = exchange("attn", 5, [("attn_w_o", 0, g_o), ("attn_w_qkv", 0, g_qkv)])
    dx2, dnm1 = _dgrad_norm_qkv("attn_qkv_dgrad", dqkv, w_qkv, x2, norm_mixer[1:2], dx3)
    finish(ffn1, dx2)
    attn = scatter("attn", 9, attn, dx2)
    dx1, dnf0, g_gu0, g_dn0 = ffn_bwd(0, dx2, x1, h1, g0, u0, a0)
    ffn0 = exchange("ffn0", 6, [("ffn_w_down", 0, g_dn0), ("ffn_w_gate_up", 0, g_gu0)])
    g_out = _wgrad_down("conv_out_wgrad", z, dx1, d)
    dz = _mm_down_t("conv_out_dgrad", dx1, w_out, 0)
    finish(attn, dz)
    ffn0 = scatter("ffn0", 10, ffn0, dz)
    dbcx, dcw = _conv_bwd(dz, bcx, cw_block, cw_got, nseq, seq)
    g_in = _wgrad_conv_in("conv_in_wgrad", h0, dbcx, conv_w_in.shape[2])
    conv = exchange("conv", 7, [("conv_w_out", 0, g_out), ("conv_w_in", 0, g_in)])
    dx0, dnm0 = _dgrad_norm_conv("conv_in_dgrad", dbcx, w_in, x0, norm_mixer[0:1], dx1)
    finish(ffn0, dx0)
    late = ("attn_w_qkv", "attn_w_o", "ffn_w_gate_up", "ffn_w_down")
    grads_late = _seq_share("share_late", 12, [finished[k] for k in late])
    conv = scatter("conv", 11, conv, dx0)

    grad, delta, new_m, new_v = {}, {}, {}, {}

    def adam(k, g):
        grad[k] = g
        delta[k], new_m[k], new_v[k] = _adam_step(f"adam_{k}", w[k], g, m[k], v[k])

    for k, g in zip(late, grads_late):
        adam(k, g)

    def blocks(src):
        return _small_block(src["norm_mixer"], src["norm_ffn"], src["conv_w"][0], src["attn_q_gain"],
                            src["attn_k_gain"], src["attn_sinks"], chip)

    loss, small = _small_step(dnm0, dnm1, dnf0, dnf1, dcw, dqg, dkg, dsk, loss_part,
                              blocks(w), blocks(m), blocks(v), conv_w.shape[2])
    for dst, part in zip((grad, delta, new_m, new_v), small):
        dst.update(part)

    done = sum(new_v[k][0, 0:1, 0:1] for k in late) + loss
    finish(conv, done)
    last = ("conv_w_in", "conv_w_out")
    for k, g in zip(last, _seq_share("share_last", 13, [finished[k] for k in last])):
        adam(k, g)

    return (loss.reshape(()), dx0.reshape(nseq, seq, d), *[grad[k] for k in WEIGHT_NAMES], *[delta[k] for k in WEIGHT_NAMES],
            *[new_m[k] for k in WEIGHT_NAMES], *[new_v[k] for k in WEIGHT_NAMES])
```

```python
import jax
import jax.numpy as jnp
from jax import lax
from jax.experimental import pallas as pl
from jax.experimental.pallas import tpu as pltpu
from jax.experimental.pallas import tpu_sc as plsc

F32 = jnp.float32
BF16 = jnp.bfloat16

D_MODEL = 1024
D_FF = 2816
N_Q_HEADS = 16
N_KV_HEADS = 4
HEAD_DIM = 64
WINDOW = 128
BLOCK = 128
EPS = 1e-6
N_CHIPS = 4
LANES = 128
SUBLANES = 8
BF16_ROWS = 16
MXU_COLS = 256
VMEM_LIMIT = 48 * 1024 * 1024
ADAM_LR, ADAM_B1, ADAM_B2, ADAM_EPS, ADAM_WD, ADAM_STEP = 0.001, 0.9, 0.999, 1e-08, 0.01, 10
ALIBI_SLOPES = tuple(2.0 ** (-8.0 * (h + 1) / N_Q_HEADS) for h in range(N_Q_HEADS))
SMALL_ROWS = 32
MESH = pl.DeviceIdType.MESH

NN = ((1,), (0,))
NT = ((1,), (1,))
TN = ((0,), (0,))


def _dot(a, b, dims):
    return lax.dot_general(a, b, (dims, ((), ())), preferred_element_type=F32)


def _pick(n, cands):
    for c in cands:
        if n % c == 0:
            return c
    raise ValueError((n, cands))


def _row_tile(rows, row_bytes, cap_bytes):
    fits = [r for r in range(BF16_ROWS, rows + 1, BF16_ROWS) if rows % r == 0 and r * row_bytes <= cap_bytes]
    if not fits:
        raise ValueError((rows, row_bytes, cap_bytes))
    return fits[-1]


ELEMENTWISE_BLOCK = 3 << 19


def _resident(block_shape, index_map):
    return pl.BlockSpec(block_shape, index_map, pipeline_mode=pl.Buffered(1))


def _params(sem):
    return pltpu.CompilerParams(dimension_semantics=sem, vmem_limit_bytes=VMEM_LIMIT)


def _sds(shape, dtype):
    return jax.ShapeDtypeStruct(shape, dtype)


def _rms(xv):
    return lax.rsqrt(jnp.mean(xv * xv, axis=-1, keepdims=True) + EPS)


def _sigmoid(g):
    return 1.0 / (1.0 + jnp.exp(-g))


def _mm_up_joined(name, a, w4, tm_pref):
    t, k = a.shape
    _, _, _, nq = w4.shape
    tm = _pick(t, (tm_pref, 256, 128))

    def body(a_ref, w_ref, o_ref, wcat_ref):
        @pl.when(pl.program_id(0) == 0)
        def _():
            for q in range(N_CHIPS):
                wcat_ref[:, q * nq:(q + 1) * nq] = w_ref[q]

        o_ref[...] = _dot(a_ref[...], wcat_ref[...], NN).astype(BF16)

    return pl.pallas_call(
        body, name=name, grid=(t // tm,),
        in_specs=[pl.BlockSpec((tm, k), lambda i: (i, 0)),
                  pl.BlockSpec((None, N_CHIPS, k, nq), lambda i: (0, 0, 0, 0))],
        out_specs=pl.BlockSpec((tm, N_CHIPS * nq), lambda i: (i, 0)),
        out_shape=_sds((t, N_CHIPS * nq), BF16),
        scratch_shapes=[pltpu.VMEM((k, N_CHIPS * nq), BF16)],
        compiler_params=_params(("arbitrary",)))(a, w4)


def _mm_norm_up_joined(name, x, gain, w4, tm_pref):
    t, k = x.shape
    _, _, _, nq = w4.shape
    tm = _pick(t, (tm_pref, 256, 128))

    def body(x_ref, g_ref, w_ref, h_ref, o_ref, wcat_ref):
        @pl.when(pl.program_id(0) == 0)
        def _():
            for q in range(N_CHIPS):
                wcat_ref[:, q * nq:(q + 1) * nq] = w_ref[q]

        xv = x_ref[...]
        h = ((xv * _rms(xv)) * g_ref[...]).astype(BF16)
        h_ref[...] = h
        o_ref[...] = _dot(h, wcat_ref[...], NN).astype(BF16)

    return pl.pallas_call(
        body, name=name, grid=(t // tm,),
        in_specs=[pl.BlockSpec((tm, k), lambda i: (i, 0)), pl.BlockSpec((1, k), lambda i: (0, 0)),
                  _resident((None, N_CHIPS, k, nq), lambda i: (0, 0, 0, 0))],
        out_specs=[pl.BlockSpec((tm, k), lambda i: (i, 0)), pl.BlockSpec((tm, N_CHIPS * nq), lambda i: (i, 0))],
        out_shape=[_sds((t, k), BF16), _sds((t, N_CHIPS * nq), BF16)],
        scratch_shapes=[pltpu.VMEM((k, N_CHIPS * nq), BF16)],
        compiler_params=_params(("arbitrary",)))(x, gain, w4)


def _mm_up_swiglu(name, h, w4, layer):
    t, k = h.shape
    _, _, _, nq = w4.shape
    tm = _pick(t, (512, 256, 128))

    def body(h_ref, wg_ref, wu_ref, dag_ref, dau_ref, a_ref):
        hv = h_ref[...]
        g = _dot(hv, wg_ref[...], NN)
        u = _dot(hv, wu_ref[...], NN)
        sg = _sigmoid(g)
        silu = g * sg
        dag_ref[...] = (u * (sg * (1.0 + g * (1.0 - sg)))).astype(BF16)
        dau_ref[...] = silu.astype(BF16)
        a_ref[...] = (silu * u).astype(BF16)

    half = N_CHIPS // 2
    out = pl.BlockSpec((tm, nq), lambda j, i: (i, j))
    return pl.pallas_call(
        body, name=name, grid=(half, t // tm),
        in_specs=[pl.BlockSpec((tm, k), lambda j, i: (i, 0)),
                  pl.BlockSpec((None, None, k, nq), lambda j, i: (layer, j, 0, 0)),
                  pl.BlockSpec((None, None, k, nq), lambda j, i: (layer, half + j, 0, 0))],
        out_specs=[out, out, out],
        out_shape=[_sds((t, half * nq), BF16)] * 3,
        compiler_params=_params(("parallel", "parallel")))(h, w4, w4)


def _mm_down_norm(name, a, w, layer, res, gain):
    t, kf = a.shape
    _, _, n = w.shape
    tm = _pick(t, (1024, 512, 256, 128))

    def body(a_ref, w_ref, r_ref, g_ref, o_ref, h_ref):
        xo = r_ref[...] + _dot(a_ref[...], w_ref[...], NN)
        o_ref[...] = xo
        h_ref[...] = ((xo * _rms(xo)) * g_ref[...]).astype(BF16)

    row = pl.BlockSpec((tm, n), lambda i: (i, 0))
    return pl.pallas_call(
        body, name=name, grid=(t // tm,),
        in_specs=[pl.BlockSpec((tm, kf), lambda i: (i, 0)),
                  _resident((None, kf, n), lambda i: (layer, 0, 0)),
                  row, pl.BlockSpec((1, n), lambda i: (0, 0))],
        out_specs=[row, row],
        out_shape=[_sds((t, n), F32), _sds((t, n), BF16)],
        compiler_params=_params(("parallel",)))(a, w, res, gain)


def _mm_down_loss(name, a, w, layer, res, tgt):
    t, kf = a.shape
    _, _, n = w.shape
    tm = _pick(t, (1024, 512, 256, 128))
    steps = t // tm

    def body(a_ref, w_ref, r_ref, t_ref, dy_ref, l_ref, acc_ref):
        i = pl.program_id(0)

        @pl.when(i == 0)
        def _():
            acc_ref[...] = jnp.zeros_like(acc_ref)

        e = (r_ref[...] + _dot(a_ref[...], w_ref[...], NN)) - t_ref[...]
        dy_ref[...] = e * (1.0 / n)
        acc_ref[...] += (e * e).reshape(tm // SUBLANES, SUBLANES, n).sum(axis=0)

        @pl.when(i == steps - 1)
        def _():
            l_ref[...] = jnp.sum(acc_ref[...], keepdims=True) * (0.5 / n)

    row = pl.BlockSpec((tm, n), lambda i: (i, 0))
    return pl.pallas_call(
        body, name=name, grid=(steps,),
        in_specs=[pl.BlockSpec((tm, kf), lambda i: (i, 0)),
                  _resident((None, kf, n), lambda i: (layer, 0, 0)), row, row],
        out_specs=[row, pl.BlockSpec((1, 1), lambda i: (0, 0))],
        out_shape=[_sds((t, n), F32), _sds((1, 1), F32)],
        scratch_shapes=[pltpu.VMEM((SUBLANES, n), F32)],
        compiler_params=_params(("arbitrary",)))(a, w, res, tgt)


def _mm_down_t(name, dx, w, layer):
    t, n = dx.shape
    _, kf, _ = w.shape
    tm = _pick(t, (512, 256, 128))

    def body(a_ref, w_ref, o_ref):
        o_ref[...] = _dot(a_ref[...].astype(BF16), w_ref[...], NT).astype(BF16)

    return pl.pallas_call(
        body, name=name, grid=(t // tm,),
        in_specs=[pl.BlockSpec((tm, n), lambda i: (i, 0)),
                  pl.BlockSpec((None, kf, n), lambda i: (layer, 0, 0))],
        out_specs=pl.BlockSpec((tm, kf), lambda i: (i, 0)),
        out_shape=_sds((t, kf), BF16),
        compiler_params=_params(("parallel",)))(dx, w)


def _mm_down_t_swiglu(name, dx, w, layer, g, u):
    t, n = dx.shape
    f = g.shape[1]
    tm = _pick(t, (512, 256, 128))

    def body(a_ref, w_ref, dag_ref, dau_ref, dg_ref, du_ref):
        da = _dot(a_ref[...].astype(BF16), w_ref[...], NT)
        dg_ref[...] = (da * dag_ref[...].astype(F32)).astype(BF16)
        du_ref[...] = (da * dau_ref[...].astype(F32)).astype(BF16)

    tile = pl.BlockSpec((tm, f), lambda i: (i, 0))
    return pl.pallas_call(
        body, name=name, grid=(t // tm,),
        in_specs=[pl.BlockSpec((tm, n), lambda i: (i, 0)),
                  _resident((None, f, n), lambda i: (layer, 0, 0)), tile, tile],
        out_specs=[tile, tile],
        out_shape=[_sds((t, f), BF16)] * 2,
        compiler_params=_params(("parallel",)))(dx, w, g, u)


def _dgrad_norm(name, acts, act_blocks, pieces, w4, layer, x, gain, dres):
    t, d = x.shape
    _, _, k, nq = w4.shape
    tm = _pick(t, (512, 256, 128))
    n_act = len(acts)

    def body(*refs):
        act_refs = refs[:n_act]
        w_ref, x_ref, g_ref, dr_ref, dx_ref, dg_ref = refs[n_act:]

        @pl.when(pl.program_id(0) == 0)
        def _():
            dg_ref[...] = jnp.zeros_like(dg_ref)

        dh = None
        for a_tile, w_tile in pieces(act_refs, w_ref):
            term = _dot(a_tile, w_tile, NT)
            dh = term if dh is None else dh + term
        xv = x_ref[...]
        r = _rms(xv)
        xhat = xv * r
        gd = dh * g_ref[...]
        dx_ref[...] = dr_ref[...] + r * (gd - xhat * jnp.mean(gd * xhat, axis=-1, keepdims=True))
        dg_ref[...] += (dh * xhat).reshape(tm // SUBLANES, SUBLANES, d).sum(axis=0)

    row = pl.BlockSpec((tm, d), lambda i: (i, 0))
    return pl.pallas_call(
        body, name=name, grid=(t // tm,),
        in_specs=[*act_blocks(tm),
                  _resident((None, N_CHIPS, k, nq), lambda i: (layer, 0, 0, 0)),
                  row, pl.BlockSpec((1, d), lambda i: (0, 0)), row],
        out_specs=[row, pl.BlockSpec((SUBLANES, d), lambda i: (0, 0))],
        out_shape=[_sds((t, d), F32), _sds((SUBLANES, d), F32)],
        compiler_params=_params(("arbitrary",)))(*acts, w4, x, gain, dres)


def _dgrad_norm_ffn(name, dg, du, w4, layer, x, gain, dres):
    nq = w4.shape[3]
    f = dg.shape[1]

    def blocks(tm):
        return [pl.BlockSpec((tm, f), lambda i: (i, 0))] * 2

    def pieces(act_refs, w_ref):
        dg_ref, du_ref = act_refs
        return [(dg_ref[:, 0:nq], w_ref[0]), (dg_ref[:, nq:2 * nq], w_ref[1]),
                (du_ref[:, 0:nq], w_ref[2]), (du_ref[:, nq:2 * nq], w_ref[3])]

    return _dgrad_norm(name, [dg, du], blocks, pieces, w4, layer, x, gain, dres)


def _dgrad_norm_qkv(name, dqkv, w4, x, gain, dres):
    nq = w4.shape[3]

    def blocks(tm):
        return [pl.BlockSpec((tm, N_CHIPS * nq), lambda i: (i, 0))]

    def pieces(act_refs, w_ref):
        return [(act_refs[0][:, q * nq:(q + 1) * nq], w_ref[q]) for q in range(N_CHIPS)]

    return _dgrad_norm(name, [dqkv], blocks, pieces, w4, 0, x, gain, dres)


def _dgrad_norm_conv(name, d3, w4, x, gain, dres):
    _, _, d = d3.shape
    nq = w4.shape[3]
    per_part, per_q = d // MXU_COLS, nq // MXU_COLS

    def blocks(tm):
        return [pl.BlockSpec((3, tm, d), lambda i: (0, i, 0))]

    def pieces(act_refs, w_ref):
        out = []
        for jb in range(3 * per_part):
            ca, cw = (jb % per_part) * MXU_COLS, (jb % per_q) * MXU_COLS
            out.append((act_refs[0][jb // per_part, :, ca:ca + MXU_COLS], w_ref[jb // per_q, :, cw:cw + MXU_COLS]))
        return out

    return _dgrad_norm(name, [d3], blocks, pieces, w4, 0, x, gain, dres)


def _wgrad_up2(name, h, dg, du):
    t, k = h.shape
    nq = dg.shape[1] // 2
    tk = _pick(t, (1024, 512, 256, 128))
    steps = t // tk
    half = N_CHIPS // 2

    def body(h_ref, dg_ref, du_ref, o_ref):
        q = pl.program_id(0)

        @pl.when(pl.program_id(1) == 0)
        def _():
            o_ref[...] = jnp.zeros_like(o_ref)

        @pl.when(q < half)
        def _():
            o_ref[...] += _dot(h_ref[...], dg_ref[...], TN)

        @pl.when(q >= half)
        def _():
            o_ref[...] += _dot(h_ref[...], du_ref[...], TN)

    return pl.pallas_call(
        body, name=name, grid=(N_CHIPS, steps),
        in_specs=[pl.BlockSpec((tk, k), lambda q, s: (s, 0)),
                  pl.BlockSpec((tk, nq), lambda q, s: (jnp.where(q < half, s, steps - 1), jnp.minimum(q, half - 1))),
                  pl.BlockSpec((tk, nq), lambda q, s: (jnp.where(q >= half, s, 0), jnp.maximum(q - half, 0)))],
        out_specs=pl.BlockSpec((None, k, nq), lambda q, s: (q, 0, 0)),
        out_shape=_sds((N_CHIPS, k, nq), F32),
        compiler_params=_params(("parallel", "arbitrary")))(h, dg, du)


def _wgrad_joined(name, h, dy):
    t, k = h.shape
    nq = dy.shape[1] // N_CHIPS
    tk = _pick(t, (1024, 512, 256, 128))

    def body(h_ref, dy_ref, o_ref):
        @pl.when(pl.program_id(0) == 0)
        def _():
            o_ref[...] = jnp.zeros_like(o_ref)

        res = _dot(h_ref[...], dy_ref[...], TN)
        for q in range(N_CHIPS):
            o_ref[q] += res[:, q * nq:(q + 1) * nq]

    return pl.pallas_call(
        body, name=name, grid=(t // tk,),
        in_specs=[pl.BlockSpec((tk, k), lambda s: (s, 0)), pl.BlockSpec((tk, N_CHIPS * nq), lambda s: (s, 0))],
        out_specs=pl.BlockSpec((N_CHIPS, k, nq), lambda s: (0, 0, 0)),
        out_shape=_sds((N_CHIPS, k, nq), F32),
        compiler_params=_params(("arbitrary",)))(h, dy)


def _wgrad_conv_in(name, h, d3, nq):
    t, k = h.shape
    d = d3.shape[2]
    per_part, per_q = d // MXU_COLS, nq // MXU_COLS
    tk = _pick(t, (512, 256, 128))

    def body(h_ref, d_ref, o_ref):
        @pl.when(pl.program_id(0) == 0)
        def _():
            o_ref[...] = jnp.zeros_like(o_ref)

        hv = h_ref[...]
        for part in range(3):
            res = _dot(hv, d_ref[part], TN)
            for cc in range(per_part):
                jb = part * per_part + cc
                co = (jb % per_q) * MXU_COLS
                o_ref[jb // per_q, :, co:co + MXU_COLS] += res[:, cc * MXU_COLS:(cc + 1) * MXU_COLS]

    return pl.pallas_call(
        body, name=name, grid=(t // tk,),
        in_specs=[pl.BlockSpec((tk, k), lambda s: (s, 0)), pl.BlockSpec((3, tk, d), lambda s: (0, s, 0))],
        out_specs=pl.BlockSpec((N_CHIPS, k, nq), lambda s: (0, 0, 0)),
        out_shape=_sds((N_CHIPS, k, nq), F32),
        compiler_params=_params(("arbitrary",)))(h, d3)


def _wgrad_down(name, a, dx, tmw):
    t, kf = a.shape
    n = dx.shape[1]
    tk = _pick(t, (1024, 512, 256, 128))

    def body(a_ref, b_ref, o_ref):
        @pl.when(pl.program_id(1) == 0)
        def _():
            o_ref[...] = jnp.zeros_like(o_ref)

        o_ref[...] += _dot(a_ref[...], b_ref[...].astype(BF16), TN)

    g = pl.pallas_call(
        body, name=name, grid=(kf // tmw, t // tk),
        in_specs=[pl.BlockSpec((tk, tmw), lambda j, s: (s, j)), pl.BlockSpec((tk, n), lambda j, s: (s, 0))],
        out_specs=pl.BlockSpec((tmw, n), lambda j, s: (j, 0)),
        out_shape=_sds((kf, n), F32),
        compiler_params=_params(("parallel", "arbitrary")))(a, dx)
    return g.reshape(N_CHIPS, kf // N_CHIPS, n)


def _shift_rows(u, k, rows):
    s = u.shape[0]
    if k > 0:
        r = pltpu.roll(u, k, 0)
        return jnp.concatenate([jnp.where(rows >= k, r[0:SUBLANES], 0.0), r[SUBLANES:]], axis=0)
    r = pltpu.roll(u, s + k, 0)
    return jnp.concatenate([r[:s - SUBLANES], jnp.where(rows < SUBLANES + k, r[s - SUBLANES:], 0.0)], axis=0)


def _conv_taps(cw_ref, got_ref):
    return (cw_ref[...] + got_ref[0]) + (got_ref[1] + got_ref[2])


def _conv_fwd(bcx, cw, cw_got, nseq, seq):
    t, d3 = bcx.shape
    d = d3 // 3
    cb = 2 * MXU_COLS
    nj = d // cb

    def body(b_ref, c_ref, x_ref, cw_ref, got_ref, z_ref):
        u = b_ref[...].astype(F32) * x_ref[...].astype(F32)
        rows = lax.broadcasted_iota(jnp.int32, (SUBLANES, cb), 0)
        cwv = _conv_taps(cw_ref, got_ref)
        y = cwv[2:3] * u + cwv[1:2] * _shift_rows(u, 1, rows) + cwv[0:1] * _shift_rows(u, 2, rows)
        z_ref[...] = (c_ref[...].astype(F32) * y).astype(BF16)

    return pl.pallas_call(
        body, name="conv_fwd", grid=(nseq, nj),
        in_specs=[pl.BlockSpec((seq, cb), lambda b, j: (b, j)),
                  pl.BlockSpec((seq, cb), lambda b, j: (b, nj + j)),
                  pl.BlockSpec((seq, cb), lambda b, j: (b, 2 * nj + j)),
                  pl.BlockSpec((SUBLANES, cb), lambda b, j: (0, j)),
                  pl.BlockSpec((3, SUBLANES, cb), lambda b, j: (0, 0, j))],
        out_specs=pl.BlockSpec((seq, cb), lambda b, j: (b, j)),
        out_shape=_sds((t, d), BF16),
        compiler_params=_params(("parallel", "parallel")))(bcx, bcx, bcx, cw, cw_got)


def _conv_bwd(dz, bcx, cw, cw_got, nseq, seq):
    t, d3 = bcx.shape
    d = d3 // 3
    cb = MXU_COLS
    nj = d // cb

    def body(dz_ref, b_ref, c_ref, x_ref, cw_ref, got_ref, o_ref, dcw_ref):
        @pl.when(pl.program_id(1) == 0)
        def _():
            dcw_ref[...] = jnp.zeros_like(dcw_ref)

        b = b_ref[...].astype(F32)
        c = c_ref[...].astype(F32)
        xv = x_ref[...].astype(F32)
        dzv = dz_ref[...].astype(F32)
        u = b * xv
        rows = lax.broadcasted_iota(jnp.int32, (SUBLANES, cb), 0)
        u1 = _shift_rows(u, 1, rows)
        u2 = _shift_rows(u, 2, rows)
        cwv = _conv_taps(cw_ref, got_ref)
        y = cwv[2:3] * u + cwv[1:2] * u1 + cwv[0:1] * u2
        dyc = dzv * c
        du = cwv[2:3] * dyc + cwv[1:2] * _shift_rows(dyc, -1, rows) + cwv[0:1] * _shift_rows(dyc, -2, rows)
        o_ref[0] = (du * xv).astype(BF16)
        o_ref[1] = (dzv * y).astype(BF16)
        o_ref[2] = (du * b).astype(BF16)
        s0 = jnp.sum(dyc * u2, axis=0, keepdims=True)
        s1 = jnp.sum(dyc * u1, axis=0, keepdims=True)
        s2 = jnp.sum(dyc * u, axis=0, keepdims=True)
        tap = lax.broadcasted_iota(jnp.int32, (3, cb), 0)
        dcw_ref[...] += jnp.where(tap == 0, s0, jnp.where(tap == 1, s1, s2))

    return pl.pallas_call(
        body, name="conv_bwd", grid=(nj, nseq),
        in_specs=[pl.BlockSpec((seq, cb), lambda j, b: (b, j)),
                  pl.BlockSpec((seq, cb), lambda j, b: (b, j)),
                  pl.BlockSpec((seq, cb), lambda j, b: (b, nj + j)),
                  pl.BlockSpec((seq, cb), lambda j, b: (b, 2 * nj + j)),
                  pl.BlockSpec((SUBLANES, cb), lambda j, b: (0, j)),
                  pl.BlockSpec((3, SUBLANES, cb), lambda j, b: (0, 0, j))],
        out_specs=[pl.BlockSpec((3, seq, cb), lambda j, b: (0, b, j)),
                   pl.BlockSpec((3, cb), lambda j, b: (0, j))],
        out_shape=[_sds((3, t, d), BF16), _sds((3, d), F32)],
        compiler_params=_params(("parallel", "arbitrary")))(dz, bcx, bcx, bcx, cw, cw_got)


def _pair_norm(x, gain_pair, low):
    sq = x * x
    ss_lo = jnp.sum(jnp.where(low, sq, 0.0), axis=-1, keepdims=True)
    ss_hi = jnp.sum(jnp.where(low, 0.0, sq), axis=-1, keepdims=True)
    r = lax.rsqrt(jnp.where(low, ss_lo, ss_hi) * (1.0 / HEAD_DIM) + EPS)
    xhat = x * r
    return xhat * gain_pair, xhat, r


KEYS = 2 * BLOCK
QK_SCALE = 1.0 / (HEAD_DIM ** 0.5)
N_PAIRS = N_Q_HEADS // 2


def _earlier_block(shape=(BLOCK, BLOCK)):
    return lax.broadcasted_iota(jnp.int32, shape, 0) > lax.broadcasted_iota(jnp.int32, shape, 1)


def _fill_bias(bias_ref):
    rows = lax.broadcasted_iota(jnp.int32, (2 * BLOCK, BLOCK), 0)
    qi = lax.broadcasted_iota(jnp.int32, (2 * BLOCK, BLOCK), 1)
    odd_head = rows >= BLOCK
    kj = jnp.where(odd_head, rows - BLOCK, rows)
    earlier = kj > qi
    dist = (jnp.where(earlier, BLOCK, 0) + qi - kj).astype(F32)
    for j in range(N_PAIRS):
        slope = jnp.where(odd_head, ALIBI_SLOPES[2 * j + 1], ALIBI_SLOPES[2 * j])
        bias = -slope * dist
        bias_ref[1, j] = bias
        bias_ref[0, j] = jnp.where(earlier, -1e30, bias)


def _merge_blocks(x_t, earlier):
    return jnp.concatenate([jnp.where(earlier, x_t[e * KEYS:e * KEYS + BLOCK], x_t[e * KEYS + BLOCK:(e + 1) * KEYS])
                            for e in range(2)], axis=0)


def _split_blocks(heads, earlier):
    parts = []
    for x in heads:
        parts += [jnp.where(earlier, x, 0.0), jnp.where(earlier, 0.0, x)]
    return jnp.concatenate(parts, axis=0).astype(BF16)


def _kv_pair_rows(kv_tile, parity, low):
    own = jnp.where(low if parity == 0 else jnp.logical_not(low), kv_tile, 0.0)
    other = pltpu.roll(own, HEAD_DIM, 1)
    lo, hi = (own, other) if parity == 0 else (other, own)
    return jnp.concatenate([lo, hi], axis=0).astype(BF16)


def _pair_softmax(s_t, sink_even, sink_odd):
    out = []
    for e, sink in enumerate((sink_even, sink_odd)):
        se = s_t[e * BLOCK:(e + 1) * BLOCK]
        m = jnp.maximum(jnp.max(se, axis=0, keepdims=True), sink)
        ee = jnp.exp(se - m)
        es = jnp.exp(sink - m)
        inv = 1.0 / (jnp.sum(ee, axis=0, keepdims=True) + es)
        out.append((ee * inv, es * inv))
    return out


def _attn_rows(n):
    q0 = pl.multiple_of(n * BLOCK, BLOCK)
    k0 = pl.multiple_of(jnp.maximum(n - 1, 0) * BLOCK, BLOCK)
    return q0, k0, jnp.minimum(n, 1)


def _key_rows(qkv_ref, k0, q0, col):
    return jnp.concatenate([qkv_ref[pl.ds(k0, BLOCK), col:col + LANES], qkv_ref[pl.ds(q0, BLOCK), col:col + LANES]],
                           axis=0).astype(F32)


def _attn_fwd(qkv, qg_pair, kg_pair, sinks, nseq, seq):
    t = qkv.shape[0]
    dq = N_Q_HEADS * HEAD_DIM
    dkv = N_KV_HEADS * HEAD_DIM

    def body(sk_ref, qkv_ref, qg_ref, kg_ref, o_ref, bias_ref):
        @pl.when(pl.program_id(0) == 0)
        def _():
            _fill_bias(bias_ref)

        low = lax.broadcasted_iota(jnp.int32, (1, LANES), 1) < HEAD_DIM
        earlier = _earlier_block()
        qg = qg_ref[...] * QK_SCALE
        kg = kg_ref[...]

        def blk(n, carry):
            q0, k0, later = _attn_rows(n)
            for kt in range(dkv // LANES):
                kraw = _key_rows(qkv_ref, k0, q0, dq + kt * LANES)
                vraw = _key_rows(qkv_ref, k0, q0, dq + dkv + kt * LANES)
                kn, _, _ = _pair_norm(kraw, kg, low)
                for par in range(2):
                    kh = 2 * kt + par
                    k_pair = _kv_pair_rows(kn, par, low)
                    v_pair = _kv_pair_rows(vraw, par, low)
                    for jj in range(2):
                        j = 2 * kh + jj
                        qraw = qkv_ref[pl.ds(q0, BLOCK), j * LANES:(j + 1) * LANES].astype(F32)
                        qn, _, _ = _pair_norm(qraw, qg, low)
                        s_t = _merge_blocks(_dot(k_pair, qn.astype(BF16), NT), earlier) + bias_ref[later, j]
                        (p0, _), (p1, _) = _pair_softmax(s_t, sk_ref[0, 2 * j], sk_ref[0, 2 * j + 1])
                        p_t = _split_blocks((p0, p1), earlier)
                        o_ref[pl.ds(q0, BLOCK), j * LANES:(j + 1) * LANES] = _dot(p_t, v_pair, TN).astype(BF16)
            return carry

        lax.fori_loop(0, seq // BLOCK, blk, 0)

    return pl.pallas_call(
        body, name="attn_fwd", grid=(nseq,),
        in_specs=[pl.BlockSpec(memory_space=pltpu.SMEM),
                  pl.BlockSpec((seq, dq + 2 * dkv), lambda b: (b, 0)),
                  pl.BlockSpec((1, LANES), lambda b: (0, 0)),
                  pl.BlockSpec((1, LANES), lambda b: (0, 0))],
        out_specs=pl.BlockSpec((seq, dq), lambda b: (b, 0)),
        out_shape=_sds((t, dq), BF16),
        scratch_shapes=[pltpu.VMEM((2, N_PAIRS, 2 * BLOCK, BLOCK), F32)],
        compiler_params=_params(("arbitrary",)))(sinks, qkv, qg_pair, kg_pair)


def _attn_bwd(do, qkv, qg_pair, kg_pair, sinks, nseq, seq):
    t = qkv.shape[0]
    dq = N_Q_HEADS * HEAD_DIM
    dkv = N_KV_HEADS * HEAD_DIM

    def body(sk_ref, do_ref, qkv_ref, qg_ref, kg_ref, o_ref, dqg_ref, dkg_ref, dsk_ref, acc_ref, bias_ref):
        @pl.when(pl.program_id(0) == 0)
        def _():
            _fill_bias(bias_ref)
            dqg_ref[...] = jnp.zeros_like(dqg_ref)
            dkg_ref[...] = jnp.zeros_like(dkg_ref)
            dsk_ref[...] = jnp.zeros_like(dsk_ref)

        acc_ref[...] = jnp.zeros_like(acc_ref)
        low = lax.broadcasted_iota(jnp.int32, (1, LANES), 1) < HEAD_DIM
        earlier = _earlier_block()
        head_row = lax.broadcasted_iota(jnp.int32, (N_Q_HEADS, LANES), 0)
        qg = qg_ref[...] * QK_SCALE
        kg = kg_ref[...]

        def blk(n, carry):
            dqg_acc, dkg_acc, dsk_acc = carry
            q0, k0, later = _attn_rows(n)
            for kt in range(dkv // LANES):
                kraw = _key_rows(qkv_ref, k0, q0, dq + kt * LANES)
                vraw = _key_rows(qkv_ref, k0, q0, dq + dkv + kt * LANES)
                kn, khat, rk = _pair_norm(kraw, kg, low)
                dk_tile = None
                dv_tile = None
                for par in range(2):
                    kh = 2 * kt + par
                    own = low if par == 0 else jnp.logical_not(low)
                    k_pair = _kv_pair_rows(kn, par, low)
                    v_pair = _kv_pair_rows(vraw, par, low)
                    dkn_rows = jnp.zeros((2 * KEYS, LANES), F32)
                    dv_rows = jnp.zeros((2 * KEYS, LANES), F32)
                    for jj in range(2):
                        j = 2 * kh + jj
                        qraw = qkv_ref[pl.ds(q0, BLOCK), j * LANES:(j + 1) * LANES].astype(F32)
                        qn, qhat, rq = _pair_norm(qraw, qg, low)
                        qn_b = qn.astype(BF16)
                        do_b = do_ref[pl.ds(q0, BLOCK), j * LANES:(j + 1) * LANES]
                        s_t = _merge_blocks(_dot(k_pair, qn_b, NT), earlier) + bias_ref[later, j]
                        dp_t = _merge_blocks(_dot(v_pair, do_b, NT), earlier)
                        ds_heads = []
                        probs = _pair_softmax(s_t, sk_ref[0, 2 * j], sk_ref[0, 2 * j + 1])
                        for e, (p, ps) in enumerate(probs):
                            dp = dp_t[e * BLOCK:(e + 1) * BLOCK]
                            dsum = jnp.sum(p * dp, axis=0, keepdims=True)
                            ds_heads.append(p * (dp - dsum))
                            dsk_acc = dsk_acc - jnp.where(head_row == 2 * j + e, ps * dsum, 0.0)
                        p_t = _split_blocks((probs[0][0], probs[1][0]), earlier)
                        ds_t = _split_blocks(ds_heads, earlier)
                        dv_rows = dv_rows + _dot(p_t, do_b, NN)
                        dkn_rows = dkn_rows + _dot(ds_t, qn_b, NN)
                        dqn = _dot(ds_t, k_pair, TN)
                        dqg_acc = dqg_acc + jnp.sum(dqn * qhat, axis=0, keepdims=True)
                        dqhat = dqn * qg
                        prod = dqhat * qhat
                        m_lo = jnp.sum(jnp.where(low, prod, 0.0), axis=-1, keepdims=True)
                        m_hi = jnp.sum(jnp.where(low, 0.0, prod), axis=-1, keepdims=True)
                        mean = jnp.where(low, m_lo, m_hi) * (1.0 / HEAD_DIM)
                        o_ref[pl.ds(q0, BLOCK), j * LANES:(j + 1) * LANES] = (rq * (dqhat - qhat * mean)).astype(BF16)
                    dkn_acc = jnp.where(low, dkn_rows[0:KEYS], dkn_rows[KEYS:2 * KEYS])
                    dv_acc = jnp.where(low, dv_rows[0:KEYS], dv_rows[KEYS:2 * KEYS])
                    dkn = dkn_acc + pltpu.roll(dkn_acc, HEAD_DIM, 1)
                    dvh = dv_acc + pltpu.roll(dv_acc, HEAD_DIM, 1)
                    khat_own = jnp.where(own, khat, 0.0)
                    khat_dup = khat_own + pltpu.roll(khat_own, HEAD_DIM, 1)
                    dkg_acc = dkg_acc + jnp.sum(jnp.where(own, dkn * khat_dup, 0.0), axis=0, keepdims=True)
                    dkhat = dkn * kg
                    mean_k = jnp.sum(dkhat * khat_dup, axis=-1, keepdims=True) * (1.0 / LANES)
                    dk_raw = rk * (dkhat - khat_dup * mean_k)
                    dk_tile = jnp.where(own, dk_raw, 0.0) if dk_tile is None else jnp.where(own, dk_raw, dk_tile)
                    dv_tile = jnp.where(own, dvh, 0.0) if dv_tile is None else jnp.where(own, dvh, dv_tile)
                for r0, part in ((k0, slice(0, BLOCK)), (q0, slice(BLOCK, KEYS))):
                    acc_ref[pl.ds(r0, BLOCK), kt * LANES:(kt + 1) * LANES] += dk_tile[part]
                    acc_ref[pl.ds(r0, BLOCK), dkv + kt * LANES:dkv + (kt + 1) * LANES] += dv_tile[part]
            return dqg_acc, dkg_acc, dsk_acc

        zero = jnp.zeros((1, LANES), F32)
        carry = (zero, zero, jnp.zeros((N_Q_HEADS, LANES), F32))
        dqg_acc, dkg_acc, dsk_acc = lax.fori_loop(0, seq // BLOCK, blk, carry)
        dqg_ref[...] += dqg_acc * QK_SCALE
        dkg_ref[...] += dkg_acc
        dsk_ref[...] += dsk_acc
        o_ref[:, dq:dq + 2 * dkv] = acc_ref[...].astype(BF16)

    small = pl.BlockSpec((1, LANES), lambda b: (0, 0))
    heads = pl.BlockSpec((N_Q_HEADS, LANES), lambda b: (0, 0))
    return pl.pallas_call(
        body, name="attn_bwd", grid=(nseq,),
        in_specs=[pl.BlockSpec(memory_space=pltpu.SMEM),
                  pl.BlockSpec((seq, dq), lambda b: (b, 0)),
                  pl.BlockSpec((seq, dq + 2 * dkv), lambda b: (b, 0)),
                  small, small],
        out_specs=[pl.BlockSpec((seq, dq + 2 * dkv), lambda b: (b, 0)), small, small, heads],
        out_shape=[_sds((t, dq + 2 * dkv), BF16), _sds((1, LANES), F32), _sds((1, LANES), F32),
                   _sds((N_Q_HEADS, LANES), F32)],
        scratch_shapes=[pltpu.VMEM((seq, 2 * dkv), F32), pltpu.VMEM((2, N_PAIRS, 2 * BLOCK, BLOCK), F32)],
        compiler_params=_params(("arbitrary",)))(sinks, do, qkv, qg_pair, kg_pair)


def _place():
    x, y, c = lax.axis_index("x"), lax.axis_index("y"), lax.axis_index("c")
    other_chips = [(1 - x, y), (x, 1 - y), (1 - x, 1 - y)]
    return x, y, c, other_chips


def _half_rows(c, rows):
    rh = rows // 2
    return pl.ds(pl.multiple_of(c * rh, BF16_ROWS), rh)


def _cast_own(name, w, place, layer=None):
    nl, r, cdim = w.shape
    first = 0
    if layer is not None:
        nl, first = 1, layer
    rt = _row_tile(r, 4 * cdim, ELEMENTWISE_BLOCK)

    def body(s_ref, w_ref, o_ref):
        o_ref[...] = w_ref[...].astype(BF16)

    grid_spec = pltpu.PrefetchScalarGridSpec(
        num_scalar_prefetch=1, grid=(nl, r // rt),
        in_specs=[pl.BlockSpec((None, rt, cdim), lambda l, i, s: (first + l, i, 0))],
        out_specs=pl.BlockSpec((None, None, rt, cdim), lambda l, i, s: (l, s[1], i, 0)))
    return pl.pallas_call(
        body, name=name, grid_spec=grid_spec, out_shape=_sds((nl, N_CHIPS, r, cdim), BF16),
        compiler_params=_params(("parallel", "parallel")))(place, w)


def _gather_protocol(outs, shapes, send_sems, recv_sems):
    n = len(outs)
    x, y, c, other_chips = _place()
    me_chip = 2 * x + y
    sibling = (x, y, 1 - c)

    def rows(u, chip, half):
        return outs[u].at[:, chip, _half_rows(half, shapes[u][2]), :]

    def copy(sem, part, to):
        return pltpu.make_async_remote_copy(src_ref=part, dst_ref=part, send_sem=send_sems.at[sem],
                                            recv_sem=recv_sems.at[sem], device_id=to, device_id_type=MESH)

    sends = []
    for u in range(n):
        for k, chip in enumerate(other_chips):
            cp = copy(6 * u + k, rows(u, me_chip, c), (*chip, c))
            cp.start()
            sends.append(cp)
    for u in range(n):
        for k, chip in enumerate(other_chips):
            got = rows(u, 2 * chip[0] + chip[1], c)
            copy(6 * u + k, got, (*chip, c)).wait_recv()
            cp = copy(6 * u + 3 + k, got, sibling)
            cp.start()
            sends.append(cp)
    for u in range(n):
        for k, chip in enumerate(other_chips):
            copy(6 * u + 3 + k, rows(u, 2 * chip[0] + chip[1], 1 - c), sibling).wait_recv()
    for cp in sends:
        cp.wait_send()


def _hbm_ref(a):
    return jax.new_ref(a, memory_space=pltpu.MemorySpace.HBM)


def _sibling_peer():
    x, y, c, _ = _place()
    return [(x, y, 1 - c)]


def _chip_peers():
    x, y, c, other_chips = _place()
    return [(*chip, c) for chip in other_chips]


def _gather_peers():
    return _chip_peers() + _sibling_peer()


def _on_sequencer(name, collective_id, n_sems, peers, protocol, operands=(), out_types=()):
    n_in, n_out = len(operands), len(out_types)

    def launch(*refs):
        send_sems, recv_sems = refs[n_in + n_out:]
        barrier = pltpu.get_barrier_semaphore()
        targets = peers()
        for peer in targets:
            pl.semaphore_signal(barrier, inc=1, device_id=peer, device_id_type=MESH)
        pl.semaphore_wait(barrier, len(targets))
        protocol(refs[:n_in], refs[n_in:n_in + n_out], send_sems, recv_sems)

    return pl.kernel(
        launch, out_type=tuple(out_types), mesh=plsc.ScalarSubcoreMesh(axis_name="sequencer", num_cores=1), name=name,
        scratch_types=(pltpu.SemaphoreType.DMA((n_sems,)), pltpu.SemaphoreType.DMA((n_sems,))),
        compiler_params=pltpu.CompilerParams(collective_id=collective_id))(*operands)


def _seq_allgather(name, collective_id, bufs):
    shapes = [b.shape for b in bufs]
    refs = [_hbm_ref(b) for b in bufs]
    _on_sequencer(name, collective_id, 6 * len(bufs), _gather_peers,
                  lambda ins, outs, send_sems, recv_sems: _gather_protocol(refs, shapes, send_sems, recv_sems))
    return [r[...] for r in refs]


def _taps_protocol(block_ref, got_ref, send_sems, recv_sems, first_sem):
    x, y, c, other_chips = _place()
    copies = []
    for k, chip in enumerate(other_chips):
        cp = pltpu.make_async_remote_copy(src_ref=block_ref, dst_ref=got_ref.at[k], send_sem=send_sems.at[first_sem + k],
                                          recv_sem=recv_sems.at[first_sem + k], device_id=(*chip, c), device_id_type=MESH)
        cp.start()
        copies.append(cp)
    return copies


def _seq_allgather_conv(collective_id, bufs, cw_block):
    shapes = [b.shape for b in bufs]
    refs = [_hbm_ref(b) for b in bufs]

    def protocol(ins, outs, send_sems, recv_sems):
        taps = _taps_protocol(ins[0], outs[0], send_sems, recv_sems, 6 * len(bufs))
        _gather_protocol(refs, shapes, send_sems, recv_sems)
        for cp in taps:
            cp.wait_recv()
        for cp in taps:
            cp.wait_send()

    (got,) = _on_sequencer("allgather_conv", collective_id, 6 * len(bufs) + 3, _gather_peers, protocol,
                           operands=(cw_block,), out_types=(_sds((3, *cw_block.shape), F32),))
    return [r[...] for r in refs], got


def _exchange_protocol(gs, outs, shapes, send_sems, recv_sems):
    x, y, c, _ = _place()
    sends = []
    for u in range(len(gs)):
        cp = pltpu.make_async_remote_copy(
            src_ref=gs[u].at[:, _half_rows(1 - c, shapes[u][1]), :], dst_ref=outs[u],
            send_sem=send_sems.at[u], recv_sem=recv_sems.at[u], device_id=(x, y, 1 - c), device_id_type=MESH)
        cp.start()
        sends.append(cp)
    for cp in sends:
        cp.wait_recv()
    for cp in sends:
        cp.wait_send()


def _seq_exchange(name, collective_id, grads):
    shapes = [g.shape for g in grads]
    return _on_sequencer(
        name, collective_id, len(grads), _sibling_peer,
        lambda gs, outs, send_sems, recv_sems: _exchange_protocol(gs, outs, shapes, send_sems, recv_sems),
        operands=grads, out_types=[_sds((s[0], s[1] // 2, s[2]), F32) for s in shapes])


def _sum_halves(name, g, got, place, after):
    _, r, cdim = g.shape
    rh = r // 2
    rt = _row_tile(rh, 4 * N_CHIPS * cdim, 2 * ELEMENTWISE_BLOCK)
    nr = rh // rt

    def body(s_ref, g_ref, got_ref, after_ref, pb_ref, pf_ref):
        pb_ref[...] = (g_ref[...] + got_ref[...]).astype(BF16)
        mine = s_ref[1]
        pf_ref[...] = g_ref[mine] + got_ref[mine]

    quarters = (N_CHIPS, rt, cdim)
    grid_spec = pltpu.PrefetchScalarGridSpec(
        num_scalar_prefetch=1, grid=(nr,),
        in_specs=[pl.BlockSpec(quarters, lambda i, s: (0, s[0] * nr + i, 0)),
                  pl.BlockSpec(quarters, lambda i, s: (0, i, 0)),
                  pl.BlockSpec(memory_space=pl.ANY)],
        out_specs=[pl.BlockSpec(quarters, lambda i, s: (0, i, 0)),
                   pl.BlockSpec((rt, cdim), lambda i, s: (i, 0))])
    return pl.pallas_call(
        body, name=name, grid_spec=grid_spec,
        out_shape=[_sds((N_CHIPS, rh, cdim), BF16), _sds((rh, cdim), F32)],
        compiler_params=_params(("parallel",)))(place, g, got, after)


def _scatter_protocol(ps, outs, send_sems, recv_sems):
    x, y, c, other_chips = _place()
    sends = []
    for u in range(len(ps)):
        for k, chip in enumerate(other_chips):
            cp = pltpu.make_async_remote_copy(
                src_ref=ps[u].at[2 * chip[0] + chip[1]], dst_ref=outs[u].at[k],
                send_sem=send_sems.at[3 * u + k], recv_sem=recv_sems.at[3 * u + k],
                device_id=(*chip, c), device_id_type=MESH)
            cp.start()
            sends.append(cp)
    for cp in sends:
        cp.wait_recv()
    for cp in sends:
        cp.wait_send()


def _seq_scatter(name, collective_id, partials):
    return _on_sequencer(
        name, collective_id, 3 * len(partials), _chip_peers, _scatter_protocol,
        operands=partials, out_types=[_sds((3, p.shape[1], p.shape[2]), BF16) for p in partials])


def _sum_partials(name, own, got, place, layer, nl, prev, after):
    rh, cdim = own.shape
    rt = _row_tile(rh, 4 * cdim, ELEMENTWISE_BLOCK)
    nr = rh // rt

    def body(s_ref, own_ref, got_ref, *rest):
        o_ref = rest[-1]
        o_ref[...] = ((own_ref[...] + got_ref[0].astype(F32)) + got_ref[1].astype(F32)) + got_ref[2].astype(F32)

    in_specs = [pl.BlockSpec((rt, cdim), lambda i, s: (i, 0)), pl.BlockSpec((3, rt, cdim), lambda i, s: (0, i, 0)),
                pl.BlockSpec(memory_space=pl.ANY)]
    args = [place, own, got, after]
    aliases = {}
    if prev is not None:
        in_specs.append(pl.BlockSpec(memory_space=pl.ANY))
        args.append(prev)
        aliases = {4: 0}
    grid_spec = pltpu.PrefetchScalarGridSpec(
        num_scalar_prefetch=1, grid=(nr,), in_specs=in_specs,
        out_specs=pl.BlockSpec((None, rt, cdim), lambda i, s: (layer, s[0] * nr + i, 0)))
    return pl.pallas_call(
        body, name=name, grid_spec=grid_spec, out_shape=_sds((nl, 2 * rh, cdim), F32),
        input_output_aliases=aliases, compiler_params=_params(("parallel",)))(*args)


def _share_protocol(outs, shapes, units, send_sems, recv_sems):
    x, y, c, _ = _place()
    sends = []
    for u, (w, l) in enumerate(units):
        mine = outs[w].at[l, _half_rows(c, shapes[w][1]), :]
        cp = pltpu.make_async_remote_copy(src_ref=mine, dst_ref=mine, send_sem=send_sems.at[u],
                                          recv_sem=recv_sems.at[u], device_id=(x, y, 1 - c), device_id_type=MESH)
        cp.start()
        sends.append(cp)
    for u, (w, l) in enumerate(units):
        theirs = outs[w].at[l, _half_rows(1 - c, shapes[w][1]), :]
        pltpu.make_async_remote_copy(src_ref=theirs, dst_ref=theirs, send_sem=send_sems.at[u],
                                     recv_sem=recv_sems.at[u], device_id=(x, y, 1 - c),
                                     device_id_type=MESH).wait_recv()
    for cp in sends:
        cp.wait_send()


def _seq_share(name, collective_id, bufs):
    shapes = [b.shape for b in bufs]
    units = [(w, l) for w in range(len(bufs)) for l in range(shapes[w][0])]
    refs = [_hbm_ref(b) for b in bufs]
    _on_sequencer(name, collective_id, len(units), _sibling_peer,
                  lambda ins, outs, send_sems, recv_sems: _share_protocol(refs, shapes, units, send_sems, recv_sems))
    return [r[...] for r in refs]


def _gather_blocks(block_ref, all_ref, send_sems, recv_sems):
    x, y, c, _ = _place()
    me = 4 * x + 2 * y + c
    all_ref[me] = block_ref[...]
    sends = []
    for rel in range(1, 8):
        fx, fy, fc = (rel >> 2) & 1, (rel >> 1) & 1, rel & 1
        peer = (x ^ fx, y ^ fy, c ^ fc)
        cp = pltpu.make_async_remote_copy(src_ref=block_ref, dst_ref=all_ref.at[me], send_sem=send_sems.at[rel - 1],
                                          recv_sem=recv_sems.at[rel - 1], device_id=peer, device_id_type=MESH)
        cp.start()
        sends.append(cp)
    for cp in sends:
        cp.wait_recv()
    for cp in sends:
        cp.wait_send()


def _adam(w, g, m, v):
    m_new = ADAM_B1 * m + (1.0 - ADAM_B1) * g
    v_new = ADAM_B2 * v + (1.0 - ADAM_B2) * (g * g)
    m_hat = m_new / (1.0 - ADAM_B1 ** ADAM_STEP)
    v_hat = v_new / (1.0 - ADAM_B2 ** ADAM_STEP)
    delta = -ADAM_LR * (m_hat / (jnp.sqrt(v_hat) + ADAM_EPS) + ADAM_WD * w)
    return delta, m_new, v_new


def _small_step(dnm0, dnm1, dnf0, dnf1, dcw, dqg, dkg, dsk, loss, w_blk, m_blk, v_blk, cw_cols):
    d = w_blk.shape[1]
    vm = pl.BlockSpec(memory_space=pltpu.VMEM)

    def reduce_body(dnm0_ref, dnm1_ref, dnf0_ref, dnf1_ref, dcw_ref, dqg_ref, dkg_ref, dsk_ref, loss_ref,
                    g_ref, blk_ref, all_ref, send_sems, recv_sems):
        blk_ref[...] = jnp.zeros_like(blk_ref)
        blk_ref[0:1, :] = jnp.sum(dnm0_ref[...], axis=0, keepdims=True)
        blk_ref[1:2, :] = jnp.sum(dnm1_ref[...], axis=0, keepdims=True)
        blk_ref[8:9, :] = jnp.sum(dnf0_ref[...], axis=0, keepdims=True)
        blk_ref[9:10, :] = jnp.sum(dnf1_ref[...], axis=0, keepdims=True)
        blk_ref[16:19, :] = dcw_ref[...]
        dqg_v = dqg_ref[...]
        dkg_v = dkg_ref[...]
        blk_ref[24:25, 0:LANES] = dqg_v + pltpu.roll(dqg_v, HEAD_DIM, 1)
        blk_ref[24:25, LANES:2 * LANES] = dkg_v + pltpu.roll(dkg_v, HEAD_DIM, 1)
        for h in range(N_Q_HEADS):
            blk_ref[24:25, 2 * LANES + h:2 * LANES + h + 1] = jnp.sum(dsk_ref[h:h + 1, :], axis=1, keepdims=True)
        blk_ref[24:25, 3 * LANES:4 * LANES] = jnp.broadcast_to(loss_ref[...], (1, LANES))
        _gather_blocks(blk_ref, all_ref, send_sems, recv_sems)
        g = all_ref[0]
        for dev in range(1, 8):
            g = g + all_ref[dev]
        g_ref[...] = g

    g_blk = pl.pallas_call(
        reduce_body, name="small_allreduce", in_specs=[vm] * 9, out_specs=vm, out_shape=_sds((SMALL_ROWS, d), F32),
        scratch_shapes=[pltpu.VMEM((SMALL_ROWS, d), F32), pltpu.VMEM((8, SMALL_ROWS, d), F32),
                        pltpu.SemaphoreType.DMA((7,)), pltpu.SemaphoreType.DMA((7,))],
    )(dnm0, dnm1, dnf0, dnf1, dcw, dqg, dkg, dsk, loss)

    def body(g_ref, w_ref, m_ref, v_ref, *out_refs):
        g = g_ref[...]
        out_refs[0][...] = g[24:25, 3 * LANES:3 * LANES + 1]
        chip = 2 * lax.axis_index("x") + lax.axis_index("y")
        for i, blk in enumerate((g, *_adam(w_ref[...], g, m_ref[...], v_ref[...]))):
            nm_ref, nf_ref, cw_ref, qg_ref, kg_ref, sk_ref = out_refs[1 + 6 * i:7 + 6 * i]
            nm_ref[...] = blk[0:2]
            nf_ref[...] = blk[8:10]
            qg_ref[...] = blk[24:25, 0:HEAD_DIM]
            kg_ref[...] = blk[24:25, LANES:LANES + HEAD_DIM]
            sk_ref[...] = blk[24:25, 2 * LANES:2 * LANES + N_Q_HEADS]
            for q in range(N_CHIPS):
                @pl.when(chip == q)
                def _(blk=blk, cw_ref=cw_ref, q=q):
                    cw_ref[0] = blk[16:19, q * cw_cols:(q + 1) * cw_cols]

    group = [_sds((2, d), F32), _sds((2, d), F32), _sds((1, 3, cw_cols), F32), _sds((1, HEAD_DIM), F32),
             _sds((1, HEAD_DIM), F32), _sds((1, N_Q_HEADS), F32)]
    outs = pl.pallas_call(
        body, name="small_adam", in_specs=[vm] * 4, out_specs=[vm] * 25, out_shape=[_sds((1, 1), F32)] + group * 4,
    )(g_blk, w_blk, m_blk, v_blk)
    names = ("norm_mixer", "norm_ffn", "conv_w", "attn_q_gain", "attn_k_gain", "attn_sinks")
    return outs[0], [dict(zip(names, outs[1 + 6 * i:7 + 6 * i])) for i in range(4)]


def _adam_step(name, w, g, m, v):
    nl, r, cdim = w.shape
    rt = _row_tile(r, 4 * cdim, ELEMENTWISE_BLOCK)

    def body(w_ref, g_ref, m_ref, v_ref, go_ref, d_ref, mo_ref, vo_ref):
        gv = g_ref[...]
        go_ref[...] = gv
        delta, m_new, v_new = _adam(w_ref[...], gv, m_ref[...], v_ref[...])
        d_ref[...] = delta
        mo_ref[...] = m_new
        vo_ref[...] = v_new

    spec = pl.BlockSpec((None, rt, cdim), lambda l, i: (l, i, 0))
    return pl.pallas_call(
        body, name=name, grid=(nl, r // rt), in_specs=[spec] * 4, out_specs=[spec] * 4,
        out_shape=[_sds(w.shape, F32)] * 4,
        compiler_params=_params(("parallel", "parallel")))(w, g, m, v)


def _pad_rows(a, rows=SUBLANES):
    return jnp.pad(a, ((0, rows - a.shape[0]), (0, 0)))


def _small_block(nm, nf, cw_local, qg, kg, sk, chip):
    d = nm.shape[1]
    cw_rows = lax.dynamic_update_slice(jnp.zeros((SUBLANES, d), F32), cw_local, (0, chip * cw_local.shape[1]))
    misc = jnp.concatenate([qg, qg, kg, kg, jnp.pad(sk, ((0, 0), (0, LANES - sk.shape[1]))),
                            jnp.zeros((1, d - 3 * LANES), F32)], axis=1)
    return jnp.concatenate([_pad_rows(nm), _pad_rows(nf), cw_rows, _pad_rows(misc)], axis=0)


WEIGHT_NAMES = ("conv_w_in", "conv_w", "conv_w_out", "attn_w_qkv", "attn_q_gain", "attn_k_gain", "attn_sinks",
                "attn_w_o", "norm_mixer", "norm_ffn", "ffn_w_gate_up", "ffn_w_down")
BIG = ("conv_w_in", "conv_w_out", "attn_w_qkv", "attn_w_o", "ffn_w_gate_up", "ffn_w_down")


def kernel(x, conv_w_in, conv_w, conv_w_out, attn_w_qkv, attn_q_gain, attn_k_gain, attn_sinks, attn_w_o, norm_mixer, norm_ffn, ffn_w_gate_up, ffn_w_down, loss_target, m_conv_w_in, m_conv_w, m_conv_w_out, m_attn_w_qkv, m_attn_q_gain, m_attn_k_gain, m_attn_sinks, m_attn_w_o, m_norm_mixer, m_norm_ffn, m_ffn_w_gate_up, m_ffn_w_down, v_conv_w_in, v_conv_w, v_conv_w_out, v_attn_w_qkv, v_attn_q_gain, v_attn_k_gain, v_attn_sinks, v_attn_w_o, v_norm_mixer, v_norm_ffn, v_ffn_w_gate_up, v_ffn_w_down):
    w = dict(conv_w_in=conv_w_in, conv_w=conv_w, conv_w_out=conv_w_out, attn_w_qkv=attn_w_qkv,
             attn_q_gain=attn_q_gain, attn_k_gain=attn_k_gain, attn_sinks=attn_sinks, attn_w_o=attn_w_o,
             norm_mixer=norm_mixer, norm_ffn=norm_ffn, ffn_w_gate_up=ffn_w_gate_up, ffn_w_down=ffn_w_down)
    m = dict(conv_w_in=m_conv_w_in, conv_w=m_conv_w, conv_w_out=m_conv_w_out, attn_w_qkv=m_attn_w_qkv,
             attn_q_gain=m_attn_q_gain, attn_k_gain=m_attn_k_gain, attn_sinks=m_attn_sinks, attn_w_o=m_attn_w_o,
             norm_mixer=m_norm_mixer, norm_ffn=m_norm_ffn, ffn_w_gate_up=m_ffn_w_gate_up, ffn_w_down=m_ffn_w_down)
    v = dict(conv_w_in=v_conv_w_in, conv_w=v_conv_w, conv_w_out=v_conv_w_out, attn_w_qkv=v_attn_w_qkv,
             attn_q_gain=v_attn_q_gain, attn_k_gain=v_attn_k_gain, attn_sinks=v_attn_sinks, attn_w_o=v_attn_w_o,
             norm_mixer=v_norm_mixer, norm_ffn=v_norm_ffn, ffn_w_gate_up=v_ffn_w_gate_up, ffn_w_down=v_ffn_w_down)

    nseq, seq, d = x.shape
    t = nseq * seq
    chip = 2 * lax.axis_index("x") + lax.axis_index("y")
    core = lax.axis_index("c")
    place = jnp.stack([core, chip]).astype(jnp.int32)
    x0 = x.reshape(t, d)
    tgt = loss_target.reshape(t, d)

    cw_block = lax.dynamic_update_slice(jnp.zeros((SUBLANES, d), F32), conv_w[0], (0, chip * conv_w.shape[2]))
    def cast(k, layer=None):
        return _cast_own(f"cast_{k}" + ("" if layer is None else str(layer)), w[k], place, layer)

    (w_in, w_out), cw_got = _seq_allgather_conv(1, [cast("conv_w_in"), cast("conv_w_out")], cw_block)
    w_gu0, w_dn0 = _seq_allgather("allgather_ffn0", 2, [cast("ffn_w_gate_up", 0), cast("ffn_w_down", 0)])
    w_qkv, w_o, w_gu1, w_dn1 = _seq_allgather(
        "allgather_rest", 3, [cast("attn_w_qkv"), cast("attn_w_o"), cast("ffn_w_gate_up", 1), cast("ffn_w_down", 1)])
    w_out = w_out.reshape(1, d, d)
    w_o = w_o.reshape(1, d, d)
    w_gu = [w_gu0, w_gu1]
    w_dn = [w_dn0.reshape(1, D_FF, d), w_dn1.reshape(1, D_FF, d)]

    qg_pair = jnp.concatenate([attn_q_gain, attn_q_gain], axis=1)
    kg_pair = jnp.concatenate([attn_k_gain, attn_k_gain], axis=1)

    def ffn_bwd(i, dxo, xin, h, g, u, a):
        g_dn = _wgrad_down(f"ffn{i}_down_wgrad", a, dxo, D_FF // 2)
        dg, du = _mm_down_t_swiglu(f"ffn{i}_down_dgrad", dxo, w_dn[i], 0, g, u)
        g_gu = _wgrad_up2(f"ffn{i}_up_wgrad", h, dg, du)
        dxi, dgain = _dgrad_norm_ffn(f"ffn{i}_up_dgrad", dg, du, w_gu[i], 0, xin, norm_ffn[i:i + 1], dxo)
        return dxi, dgain, g_gu, g_dn

    h0, bcx = _mm_norm_up_joined("conv_in", x0, norm_mixer[0:1], w_in, 512)
    z = _conv_fwd(bcx, cw_block, cw_got, nseq, seq)
    x1, h1 = _mm_down_norm("conv_out", z, w_out, 0, x0, norm_ffn[0:1])
    g0, u0, a0 = _mm_up_swiglu("ffn0_up", h1, w_gu[0], 0)
    x2, h2 = _mm_down_norm("ffn0_down", a0, w_dn[0], 0, x1, norm_mixer[1:2])
    qkv = _mm_up_joined("attn_qkv", h2, w_qkv, 1024)
    o = _attn_fwd(qkv, qg_pair, kg_pair, attn_sinks, nseq, seq)
    x3, h3 = _mm_down_norm("attn_out", o, w_o, 0, x2, norm_ffn[1:2])
    g1, u1, a1 = _mm_up_swiglu("ffn1_up", h3, w_gu[1], 0)
    dy, loss_part = _mm_down_loss("ffn1_down", a1, w_dn[1], 0, x3, tgt)

    finished = {k: None for k in BIG}

    def exchange(tag, cid, units):
        return units, _seq_exchange(f"exchange_{tag}", cid, [g for _, _, g in units])

    def scatter(tag, cid, group, after):
        units, got = group
        sums = [_sum_halves(f"sum_halves_{k}{l}", g, r, place, after) for (k, l, g), r in zip(units, got)]
        return units, sums, _seq_scatter(f"scatter_{tag}", cid, [pb for pb, _ in sums])

    def finish(group, after):
        units, sums, arrived = group
        for (k, l, _), (_, pf), r in zip(units, sums, arrived):
            finished[k] = _sum_partials(f"sum_partials_{k}{l}", pf, r, place, l, w[k].shape[0], finished[k], after)

    dx3, dnf1, g_gu1, g_dn1 = ffn_bwd(1, dy, x3, h3, g1, u1, a1)
    ffn1 = exchange("ffn1", 4, [("ffn_w_down", 1, g_dn1), ("ffn_w_gate_up", 1, g_gu1)])
    g_o = _wgrad_down("attn_out_wgrad", o, dx3, d)
    do = _mm_down_t("attn_out_dgrad", dx3, w_o, 0)
    ffn1 = scatter("ffn1", 8, ffn1, do)
    dqkv, dqg, dkg, dsk = _attn_bwd(do, qkv, qg_pair, kg_pair, attn_sinks, nseq, seq)
    g_qkv = _wgrad_joined("attn_qkv_wgrad", h2, dqkv)
    attn = exchange("attn", 5, [("attn_w_o", 0, g_o), ("attn_w_qkv", 0, g_qkv)])
    dx2, dnm1 = _dgrad_norm_qkv("attn_qkv_dgrad", dqkv, w_qkv, x2, norm_mixer[1:2], dx3)
    finish(ffn1, dx2)
    attn = scatter("attn", 9, attn, dx2)
    dx1, dnf0, g_gu0, g_dn0 = ffn_bwd(0, dx2, x1, h1, g0, u0, a0)
    ffn0 = exchange("ffn0", 6, [("ffn_w_down", 0, g_dn0), ("ffn_w_gate_up", 0, g_gu0)])
    g_out = _wgrad_down("conv_out_wgrad", z, dx1, d)
    dz = _mm_down_t("conv_out_dgrad", dx1, w_out, 0)
    finish(attn, dz)
    ffn0 = scatter("ffn0", 10, ffn0, dz)
    dbcx, dcw = _conv_bwd(dz, bcx, cw_block, cw_got, nseq, seq)
    g_in = _wgrad_conv_in("conv_in_wgrad", h0, dbcx, conv_w_in.shape[2])
    conv = exchange("conv", 7, [("conv_w_out", 0, g_out), ("conv_w_in", 0, g_in)])
    dx0, dnm0 = _dgrad_norm_conv("conv_in_dgrad", dbcx, w_in, x0, norm_mixer[0:1], dx1)
    finish(ffn0, dx0)
    late = ("attn_w_qkv", "attn_w_o", "ffn_w_gate_up", "ffn_w_down")
    grads_late = _seq_share("share_late", 12, [finished[k] for k in late])
    conv = scatter("conv", 11, conv, dx0)

    grad, delta, new_m, new_v = {}, {}, {}, {}

    def adam(k, g):
        grad[k], delta[k], new_m[k], new_v[k] = _adam_step(f"adam_{k}", w[k], g, m[k], v[k])

    for k, g in zip(late, grads_late):
        adam(k, g)

    def blocks(src):
        return _small_block(src["norm_mixer"], src["norm_ffn"], src["conv_w"][0], src["attn_q_gain"],
                            src["attn_k_gain"], src["attn_sinks"], chip)

    loss, small = _small_step(dnm0, dnm1, dnf0, dnf1, dcw, dqg, dkg, dsk, loss_part,
                              blocks(w), blocks(m), blocks(v), conv_w.shape[2])
    for dst, part in zip((grad, delta, new_m, new_v), small):
        dst.update(part)

    done = sum(new_v[k][0, 0:1, 0:1] for k in late) + loss
    finish(conv, done)
    last = ("conv_w_in", "conv_w_out")
    for k, g in zip(last, _seq_share("share_last", 13, [finished[k] for k in last])):
        adam(k, g)

    return (loss.reshape(()), dx0.reshape(nseq, seq, d), *[grad[k] for k in WEIGHT_NAMES], *[delta[k] for k in WEIGHT_NAMES],
            *[new_m[k] for k in WEIGHT_NAMES], *[new_v[k] for k in WEIGHT_NAMES])
```

```python
import jax
import jax.numpy as jnp
from jax import lax
from jax.experimental import pallas as pl
from jax.experimental.pallas import tpu as pltpu
from jax.experimental.pallas import tpu_sc as plsc

F32 = jnp.float32
BF16 = jnp.bfloat16

D_MODEL = 1024
D_FF = 2816
N_Q_HEADS = 16
N_KV_HEADS = 4
HEAD_DIM = 64
WINDOW = 128
BLOCK = 128
EPS = 1e-6
N_CHIPS = 4
LANES = 128
SUBLANES = 8
BF16_ROWS = 16
MXU_COLS = 256
VMEM_LIMIT = 48 * 1024 * 1024
ADAM_LR, ADAM_B1, ADAM_B2, ADAM_EPS, ADAM_WD, ADAM_STEP = 0.001, 0.9, 0.999, 1e-08, 0.01, 10
ALIBI_SLOPES = tuple(2.0 ** (-8.0 * (h + 1) / N_Q_HEADS) for h in range(N_Q_HEADS))
SMALL_ROWS = 32
MESH = pl.DeviceIdType.MESH

NN = ((1,), (0,))
NT = ((1,), (1,))
TN = ((0,), (0,))


def _dot(a, b, dims):
    return lax.dot_general(a, b, (dims, ((), ())), preferred_element_type=F32)


def _pick(n, cands):
    for c in cands:
        if n % c == 0:
            return c
    raise ValueError((n, cands))


def _row_tile(rows, row_bytes, cap_bytes):
    fits = [r for r in range(BF16_ROWS, rows + 1, BF16_ROWS) if rows % r == 0 and r * row_bytes <= cap_bytes]
    if not fits:
        raise ValueError((rows, row_bytes, cap_bytes))
    return fits[-1]


ELEMENTWISE_BLOCK = 3 << 19


def _resident(block_shape, index_map):
    return pl.BlockSpec(block_shape, index_map, pipeline_mode=pl.Buffered(1))


def _params(sem):
    return pltpu.CompilerParams(dimension_semantics=sem, vmem_limit_bytes=VMEM_LIMIT)


def _sds(shape, dtype):
    return jax.ShapeDtypeStruct(shape, dtype)


def _rms(xv):
    return lax.rsqrt(jnp.mean(xv * xv, axis=-1, keepdims=True) + EPS)


def _sigmoid(g):
    return 1.0 / (1.0 + jnp.exp(-g))


def _mm_up_joined(name, a, w4, tm_pref):
    t, k = a.shape
    _, _, _, nq = w4.shape
    tm = _pick(t, (tm_pref, 256, 128))

    def body(a_ref, w_ref, o_ref, wcat_ref):
        @pl.when(pl.program_id(0) == 0)
        def _():
            for q in range(N_CHIPS):
                wcat_ref[:, q * nq:(q + 1) * nq] = w_ref[q]

        o_ref[...] = _dot(a_ref[...], wcat_ref[...], NN).astype(BF16)

    return pl.pallas_call(
        body, name=name, grid=(t // tm,),
        in_specs=[pl.BlockSpec((tm, k), lambda i: (i, 0)),
                  pl.BlockSpec((None, N_CHIPS, k, nq), lambda i: (0, 0, 0, 0))],
        out_specs=pl.BlockSpec((tm, N_CHIPS * nq), lambda i: (i, 0)),
        out_shape=_sds((t, N_CHIPS * nq), BF16),
        scratch_shapes=[pltpu.VMEM((k, N_CHIPS * nq), BF16)],
        compiler_params=_params(("arbitrary",)))(a, w4)


def _mm_norm_up_joined(name, x, gain, w4, tm_pref):
    t, k = x.shape
    _, _, _, nq = w4.shape
    tm = _pick(t, (tm_pref, 256, 128))

    def body(x_ref, g_ref, w_ref, h_ref, o_ref, wcat_ref):
        @pl.when(pl.program_id(0) == 0)
        def _():
            for q in range(N_CHIPS):
                wcat_ref[:, q * nq:(q + 1) * nq] = w_ref[q]

        xv = x_ref[...]
        h = ((xv * _rms(xv)) * g_ref[...]).astype(BF16)
        h_ref[...] = h
        o_ref[...] = _dot(h, wcat_ref[...], NN).astype(BF16)

    return pl.pallas_call(
        body, name=name, grid=(t // tm,),
        in_specs=[pl.BlockSpec((tm, k), lambda i: (i, 0)), pl.BlockSpec((1, k), lambda i: (0, 0)),
                  _resident((None, N_CHIPS, k, nq), lambda i: (0, 0, 0, 0))],
        out_specs=[pl.BlockSpec((tm, k), lambda i: (i, 0)), pl.BlockSpec((tm, N_CHIPS * nq), lambda i: (i, 0))],
        out_shape=[_sds((t, k), BF16), _sds((t, N_CHIPS * nq), BF16)],
        scratch_shapes=[pltpu.VMEM((k, N_CHIPS * nq), BF16)],
        compiler_params=_params(("arbitrary",)))(x, gain, w4)


def _mm_up_swiglu(name, h, w4, layer):
    t, k = h.shape
    _, _, _, nq = w4.shape
    tm = _pick(t, (512, 256, 128))

    def body(h_ref, wg_ref, wu_ref, dag_ref, dau_ref, a_ref):
        hv = h_ref[...]
        g = _dot(hv, wg_ref[...], NN)
        u = _dot(hv, wu_ref[...], NN)
        sg = _sigmoid(g)
        silu = g * sg
        dag_ref[...] = (u * (sg * (1.0 + g * (1.0 - sg)))).astype(BF16)
        dau_ref[...] = silu.astype(BF16)
        a_ref[...] = (silu * u).astype(BF16)

    half = N_CHIPS // 2
    out = pl.BlockSpec((tm, nq), lambda j, i: (i, j))
    return pl.pallas_call(
        body, name=name, grid=(half, t // tm),
        in_specs=[pl.BlockSpec((tm, k), lambda j, i: (i, 0)),
                  pl.BlockSpec((None, None, k, nq), lambda j, i: (layer, j, 0, 0)),
                  pl.BlockSpec((None, None, k, nq), lambda j, i: (layer, half + j, 0, 0))],
        out_specs=[out, out, out],
        out_shape=[_sds((t, half * nq), BF16)] * 3,
        compiler_params=_params(("parallel", "parallel")))(h, w4, w4)


def _mm_down_norm(name, a, w, layer, res, gain):
    t, kf = a.shape
    _, _, n = w.shape
    tm = _pick(t, (1024, 512, 256, 128))

    def body(a_ref, w_ref, r_ref, g_ref, o_ref, h_ref):
        xo = r_ref[...] + _dot(a_ref[...], w_ref[...], NN)
        o_ref[...] = xo
        h_ref[...] = ((xo * _rms(xo)) * g_ref[...]).astype(BF16)

    row = pl.BlockSpec((tm, n), lambda i: (i, 0))
    return pl.pallas_call(
        body, name=name, grid=(t // tm,),
        in_specs=[pl.BlockSpec((tm, kf), lambda i: (i, 0)),
                  _resident((None, kf, n), lambda i: (layer, 0, 0)),
                  row, pl.BlockSpec((1, n), lambda i: (0, 0))],
        out_specs=[row, row],
        out_shape=[_sds((t, n), F32), _sds((t, n), BF16)],
        compiler_params=_params(("parallel",)))(a, w, res, gain)


def _mm_down_loss(name, a, w, layer, res, tgt):
    t, kf = a.shape
    _, _, n = w.shape
    tm = _pick(t, (1024, 512, 256, 128))
    steps = t // tm

    def body(a_ref, w_ref, r_ref, t_ref, dy_ref, l_ref, acc_ref):
        i = pl.program_id(0)

        @pl.when(i == 0)
        def _():
            acc_ref[...] = jnp.zeros_like(acc_ref)

        e = (r_ref[...] + _dot(a_ref[...], w_ref[...], NN)) - t_ref[...]
        dy_ref[...] = e * (1.0 / n)
        acc_ref[...] += (e * e).reshape(tm // SUBLANES, SUBLANES, n).sum(axis=0)

        @pl.when(i == steps - 1)
        def _():
            l_ref[...] = jnp.sum(acc_ref[...], keepdims=True) * (0.5 / n)

    row = pl.BlockSpec((tm, n), lambda i: (i, 0))
    return pl.pallas_call(
        body, name=name, grid=(steps,),
        in_specs=[pl.BlockSpec((tm, kf), lambda i: (i, 0)),
                  _resident((None, kf, n), lambda i: (layer, 0, 0)), row, row],
        out_specs=[row, pl.BlockSpec((1, 1), lambda i: (0, 0))],
        out_shape=[_sds((t, n), F32), _sds((1, 1), F32)],
        scratch_shapes=[pltpu.VMEM((SUBLANES, n), F32)],
        compiler_params=_params(("arbitrary",)))(a, w, res, tgt)


def _mm_down_t(name, dx, w, layer):
    t, n = dx.shape
    _, kf, _ = w.shape
    tm = _pick(t, (1024, 512, 256, 128))

    def body(a_ref, w_ref, o_ref):
        o_ref[...] = _dot(a_ref[...].astype(BF16), w_ref[...], NT).astype(BF16)

    return pl.pallas_call(
        body, name=name, grid=(t // tm,),
        in_specs=[pl.BlockSpec((tm, n), lambda i: (i, 0)),
                  _resident((None, kf, n), lambda i: (layer, 0, 0))],
        out_specs=pl.BlockSpec((tm, kf), lambda i: (i, 0)),
        out_shape=_sds((t, kf), BF16),
        compiler_params=_params(("parallel",)))(dx, w)


def _mm_down_t_swiglu(name, dx, w, layer, g, u):
    t, n = dx.shape
    f = g.shape[1]
    tm = _pick(t, (512, 256, 128))

    def body(a_ref, w_ref, dag_ref, dau_ref, dg_ref, du_ref):
        da = _dot(a_ref[...].astype(BF16), w_ref[...], NT)
        dg_ref[...] = (da * dag_ref[...].astype(F32)).astype(BF16)
        du_ref[...] = (da * dau_ref[...].astype(F32)).astype(BF16)

    tile = pl.BlockSpec((tm, f), lambda i: (i, 0))
    return pl.pallas_call(
        body, name=name, grid=(t // tm,),
        in_specs=[pl.BlockSpec((tm, n), lambda i: (i, 0)),
                  _resident((None, f, n), lambda i: (layer, 0, 0)), tile, tile],
        out_specs=[tile, tile],
        out_shape=[_sds((t, f), BF16)] * 2,
        compiler_params=_params(("parallel",)))(dx, w, g, u)


def _dgrad_norm(name, acts, act_blocks, pieces, w4, layer, x, gain, dres):
    t, d = x.shape
    _, _, k, nq = w4.shape
    tm = _pick(t, (512, 256, 128))
    n_act = len(acts)

    def body(*refs):
        act_refs = refs[:n_act]
        w_ref, x_ref, g_ref, dr_ref, dx_ref, dg_ref = refs[n_act:]

        @pl.when(pl.program_id(0) == 0)
        def _():
            dg_ref[...] = jnp.zeros_like(dg_ref)

        dh = None
        for a_tile, w_tile in pieces(act_refs, w_ref):
            term = _dot(a_tile, w_tile, NT)
            dh = term if dh is None else dh + term
        xv = x_ref[...]
        r = _rms(xv)
        xhat = xv * r
        gd = dh * g_ref[...]
        dx_ref[...] = dr_ref[...] + r * (gd - xhat * jnp.mean(gd * xhat, axis=-1, keepdims=True))
        dg_ref[...] += (dh * xhat).reshape(tm // SUBLANES, SUBLANES, d).sum(axis=0)

    row = pl.BlockSpec((tm, d), lambda i: (i, 0))
    return pl.pallas_call(
        body, name=name, grid=(t // tm,),
        in_specs=[*act_blocks(tm),
                  _resident((None, N_CHIPS, k, nq), lambda i: (layer, 0, 0, 0)),
                  row, pl.BlockSpec((1, d), lambda i: (0, 0)), row],
        out_specs=[row, pl.BlockSpec((SUBLANES, d), lambda i: (0, 0))],
        out_shape=[_sds((t, d), F32), _sds((SUBLANES, d), F32)],
        compiler_params=_params(("arbitrary",)))(*acts, w4, x, gain, dres)


def _dgrad_norm_ffn(name, dg, du, w4, layer, x, gain, dres):
    nq = w4.shape[3]
    f = dg.shape[1]

    def blocks(tm):
        return [pl.BlockSpec((tm, f), lambda i: (i, 0))] * 2

    def pieces(act_refs, w_ref):
        dg_ref, du_ref = act_refs
        return [(dg_ref[:, 0:nq], w_ref[0]), (dg_ref[:, nq:2 * nq], w_ref[1]),
                (du_ref[:, 0:nq], w_ref[2]), (du_ref[:, nq:2 * nq], w_ref[3])]

    return _dgrad_norm(name, [dg, du], blocks, pieces, w4, layer, x, gain, dres)


def _dgrad_norm_qkv(name, dqkv, w4, x, gain, dres):
    nq = w4.shape[3]

    def blocks(tm):
        return [pl.BlockSpec((tm, N_CHIPS * nq), lambda i: (i, 0))]

    def pieces(act_refs, w_ref):
        return [(act_refs[0][:, q * nq:(q + 1) * nq], w_ref[q]) for q in range(N_CHIPS)]

    return _dgrad_norm(name, [dqkv], blocks, pieces, w4, 0, x, gain, dres)


def _dgrad_norm_conv(name, d3, w4, x, gain, dres):
    _, _, d = d3.shape
    nq = w4.shape[3]
    per_part, per_q = d // MXU_COLS, nq // MXU_COLS

    def blocks(tm):
        return [pl.BlockSpec((3, tm, d), lambda i: (0, i, 0))]

    def pieces(act_refs, w_ref):
        out = []
        for jb in range(3 * per_part):
            ca, cw = (jb % per_part) * MXU_COLS, (jb % per_q) * MXU_COLS
            out.append((act_refs[0][jb // per_part, :, ca:ca + MXU_COLS], w_ref[jb // per_q, :, cw:cw + MXU_COLS]))
        return out

    return _dgrad_norm(name, [d3], blocks, pieces, w4, 0, x, gain, dres)


def _wgrad_up2(name, h, dg, du):
    t, k = h.shape
    nq = dg.shape[1] // 2
    tk = _pick(t, (1024, 512, 256, 128))
    steps = t // tk
    half = N_CHIPS // 2

    def body(h_ref, dg_ref, du_ref, o_ref):
        q = pl.program_id(0)

        @pl.when(pl.program_id(1) == 0)
        def _():
            o_ref[...] = jnp.zeros_like(o_ref)

        @pl.when(q < half)
        def _():
            o_ref[...] += _dot(h_ref[...], dg_ref[...], TN)

        @pl.when(q >= half)
        def _():
            o_ref[...] += _dot(h_ref[...], du_ref[...], TN)

    return pl.pallas_call(
        body, name=name, grid=(N_CHIPS, steps),
        in_specs=[pl.BlockSpec((tk, k), lambda q, s: (s, 0)),
                  pl.BlockSpec((tk, nq), lambda q, s: (jnp.where(q < half, s, steps - 1), jnp.minimum(q, half - 1))),
                  pl.BlockSpec((tk, nq), lambda q, s: (jnp.where(q >= half, s, 0), jnp.maximum(q - half, 0)))],
        out_specs=pl.BlockSpec((None, k, nq), lambda q, s: (q, 0, 0)),
        out_shape=_sds((N_CHIPS, k, nq), F32),
        compiler_params=_params(("parallel", "arbitrary")))(h, dg, du)


def _wgrad_joined(name, h, dy):
    t, k = h.shape
    nq = dy.shape[1] // N_CHIPS
    tk = _pick(t, (1024, 512, 256, 128))

    def body(h_ref, dy_ref, o_ref):
        @pl.when(pl.program_id(0) == 0)
        def _():
            o_ref[...] = jnp.zeros_like(o_ref)

        res = _dot(h_ref[...], dy_ref[...], TN)
        for q in range(N_CHIPS):
            o_ref[q] += res[:, q * nq:(q + 1) * nq]

    return pl.pallas_call(
        body, name=name, grid=(t // tk,),
        in_specs=[pl.BlockSpec((tk, k), lambda s: (s, 0)), pl.BlockSpec((tk, N_CHIPS * nq), lambda s: (s, 0))],
        out_specs=pl.BlockSpec((N_CHIPS, k, nq), lambda s: (0, 0, 0)),
        out_shape=_sds((N_CHIPS, k, nq), F32),
        compiler_params=_params(("arbitrary",)))(h, dy)


def _wgrad_conv_in(name, h, d3, nq):
    t, k = h.shape
    d = d3.shape[2]
    per_part, per_q = d // MXU_COLS, nq // MXU_COLS
    tk = _pick(t, (512, 256, 128))

    def body(h_ref, d_ref, o_ref):
        @pl.when(pl.program_id(0) == 0)
        def _():
            o_ref[...] = jnp.zeros_like(o_ref)

        hv = h_ref[...]
        for part in range(3):
            res = _dot(hv, d_ref[part], TN)
            for cc in range(per_part):
                jb = part * per_part + cc
                co = (jb % per_q) * MXU_COLS
                o_ref[jb // per_q, :, co:co + MXU_COLS] += res[:, cc * MXU_COLS:(cc + 1) * MXU_COLS]

    return pl.pallas_call(
        body, name=name, grid=(t // tk,),
        in_specs=[pl.BlockSpec((tk, k), lambda s: (s, 0)), pl.BlockSpec((3, tk, d), lambda s: (0, s, 0))],
        out_specs=pl.BlockSpec((N_CHIPS, k, nq), lambda s: (0, 0, 0)),
        out_shape=_sds((N_CHIPS, k, nq), F32),
        compiler_params=_params(("arbitrary",)))(h, d3)


def _wgrad_down(name, a, dx, tmw):
    t, kf = a.shape
    n = dx.shape[1]
    tk = _pick(t, (1024, 512, 256, 128))

    def body(a_ref, b_ref, o_ref):
        @pl.when(pl.program_id(1) == 0)
        def _():
            o_ref[...] = jnp.zeros_like(o_ref)

        o_ref[...] += _dot(a_ref[...], b_ref[...].astype(BF16), TN)

    g = pl.pallas_call(
        body, name=name, grid=(kf // tmw, t // tk),
        in_specs=[pl.BlockSpec((tk, tmw), lambda j, s: (s, j)), pl.BlockSpec((tk, n), lambda j, s: (s, 0))],
        out_specs=pl.BlockSpec((tmw, n), lambda j, s: (j, 0)),
        out_shape=_sds((kf, n), F32),
        compiler_params=_params(("parallel", "arbitrary")))(a, dx)
    return g.reshape(N_CHIPS, kf // N_CHIPS, n)


def _shift_rows(u, k, rows):
    s = u.shape[0]
    if k > 0:
        r = pltpu.roll(u, k, 0)
        return jnp.concatenate([jnp.where(rows >= k, r[0:SUBLANES], 0.0), r[SUBLANES:]], axis=0)
    r = pltpu.roll(u, s + k, 0)
    return jnp.concatenate([r[:s - SUBLANES], jnp.where(rows < SUBLANES + k, r[s - SUBLANES:], 0.0)], axis=0)


def _conv_taps(cw_ref, got_ref):
    return (cw_ref[...] + got_ref[0]) + (got_ref[1] + got_ref[2])


def _conv_fwd(bcx, cw, cw_got, nseq, seq):
    t, d3 = bcx.shape
    d = d3 // 3
    cb = 2 * MXU_COLS
    nj = d // cb

    def body(b_ref, c_ref, x_ref, cw_ref, got_ref, z_ref):
        u = b_ref[...].astype(F32) * x_ref[...].astype(F32)
        rows = lax.broadcasted_iota(jnp.int32, (SUBLANES, cb), 0)
        cwv = _conv_taps(cw_ref, got_ref)
        y = cwv[2:3] * u + cwv[1:2] * _shift_rows(u, 1, rows) + cwv[0:1] * _shift_rows(u, 2, rows)
        z_ref[...] = (c_ref[...].astype(F32) * y).astype(BF16)

    return pl.pallas_call(
        body, name="conv_fwd", grid=(nseq, nj),
        in_specs=[pl.BlockSpec((seq, cb), lambda b, j: (b, j)),
                  pl.BlockSpec((seq, cb), lambda b, j: (b, nj + j)),
                  pl.BlockSpec((seq, cb), lambda b, j: (b, 2 * nj + j)),
                  pl.BlockSpec((SUBLANES, cb), lambda b, j: (0, j)),
                  pl.BlockSpec((3, SUBLANES, cb), lambda b, j: (0, 0, j))],
        out_specs=pl.BlockSpec((seq, cb), lambda b, j: (b, j)),
        out_shape=_sds((t, d), BF16),
        compiler_params=_params(("parallel", "parallel")))(bcx, bcx, bcx, cw, cw_got)


def _conv_bwd(dz, bcx, cw, cw_got, nseq, seq):
    t, d3 = bcx.shape
    d = d3 // 3
    cb = MXU_COLS
    nj = d // cb

    def body(dz_ref, b_ref, c_ref, x_ref, cw_ref, got_ref, o_ref, dcw_ref):
        @pl.when(pl.program_id(1) == 0)
        def _():
            dcw_ref[...] = jnp.zeros_like(dcw_ref)

        b = b_ref[...].astype(F32)
        c = c_ref[...].astype(F32)
        xv = x_ref[...].astype(F32)
        dzv = dz_ref[...].astype(F32)
        u = b * xv
        rows = lax.broadcasted_iota(jnp.int32, (SUBLANES, cb), 0)
        u1 = _shift_rows(u, 1, rows)
        u2 = _shift_rows(u, 2, rows)
        cwv = _conv_taps(cw_ref, got_ref)
        y = cwv[2:3] * u + cwv[1:2] * u1 + cwv[0:1] * u2
        dyc = dzv * c
        du = cwv[2:3] * dyc + cwv[1:2] * _shift_rows(dyc, -1, rows) + cwv[0:1] * _shift_rows(dyc, -2, rows)
        o_ref[0] = (du * xv).astype(BF16)
        o_ref[1] = (dzv * y).astype(BF16)
        o_ref[2] = (du * b).astype(BF16)
        s0 = jnp.sum(dyc * u2, axis=0, keepdims=True)
        s1 = jnp.sum(dyc * u1, axis=0, keepdims=True)
        s2 = jnp.sum(dyc * u, axis=0, keepdims=True)
        tap = lax.broadcasted_iota(jnp.int32, (3, cb), 0)
        dcw_ref[...] += jnp.where(tap == 0, s0, jnp.where(tap == 1, s1, s2))

    return pl.pallas_call(
        body, name="conv_bwd", grid=(nj, nseq),
        in_specs=[pl.BlockSpec((seq, cb), lambda j, b: (b, j)),
                  pl.BlockSpec((seq, cb), lambda j, b: (b, j)),
                  pl.BlockSpec((seq, cb), lambda j, b: (b, nj + j)),
                  pl.BlockSpec((seq, cb), lambda j, b: (b, 2 * nj + j)),
                  pl.BlockSpec((SUBLANES, cb), lambda j, b: (0, j)),
                  pl.BlockSpec((3, SUBLANES, cb), lambda j, b: (0, 0, j))],
        out_specs=[pl.BlockSpec((3, seq, cb), lambda j, b: (0, b, j)),
                   pl.BlockSpec((3, cb), lambda j, b: (0, j))],
        out_shape=[_sds((3, t, d), BF16), _sds((3, d), F32)],
        compiler_params=_params(("parallel", "arbitrary")))(dz, bcx, bcx, bcx, cw, cw_got)


def _pair_norm(x, gain_pair, low):
    sq = x * x
    ss_lo = jnp.sum(jnp.where(low, sq, 0.0), axis=-1, keepdims=True)
    ss_hi = jnp.sum(jnp.where(low, 0.0, sq), axis=-1, keepdims=True)
    r = lax.rsqrt(jnp.where(low, ss_lo, ss_hi) * (1.0 / HEAD_DIM) + EPS)
    xhat = x * r
    return xhat * gain_pair, xhat, r


KEYS = 2 * BLOCK
QK_SCALE = 1.0 / (HEAD_DIM ** 0.5)
N_PAIRS = N_Q_HEADS // 2


def _earlier_block(shape=(BLOCK, BLOCK)):
    return lax.broadcasted_iota(jnp.int32, shape, 0) > lax.broadcasted_iota(jnp.int32, shape, 1)


def _fill_bias(bias_ref):
    rows = lax.broadcasted_iota(jnp.int32, (2 * BLOCK, BLOCK), 0)
    qi = lax.broadcasted_iota(jnp.int32, (2 * BLOCK, BLOCK), 1)
    odd_head = rows >= BLOCK
    kj = jnp.where(odd_head, rows - BLOCK, rows)
    earlier = kj > qi
    dist = (jnp.where(earlier, BLOCK, 0) + qi - kj).astype(F32)
    for j in range(N_PAIRS):
        slope = jnp.where(odd_head, ALIBI_SLOPES[2 * j + 1], ALIBI_SLOPES[2 * j])
        bias = -slope * dist
        bias_ref[1, j] = bias
        bias_ref[0, j] = jnp.where(earlier, -1e30, bias)


def _merge_blocks(x_t, earlier):
    return jnp.concatenate([jnp.where(earlier, x_t[e * KEYS:e * KEYS + BLOCK], x_t[e * KEYS + BLOCK:(e + 1) * KEYS])
                            for e in range(2)], axis=0)


def _split_blocks(heads, earlier):
    parts = []
    for x in heads:
        parts += [jnp.where(earlier, x, 0.0), jnp.where(earlier, 0.0, x)]
    return jnp.concatenate(parts, axis=0).astype(BF16)


def _kv_pair_rows(kv_tile, parity, low):
    own = jnp.where(low if parity == 0 else jnp.logical_not(low), kv_tile, 0.0)
    other = pltpu.roll(own, HEAD_DIM, 1)
    lo, hi = (own, other) if parity == 0 else (other, own)
    return jnp.concatenate([lo, hi], axis=0).astype(BF16)


def _pair_softmax(s_t, sink_even, sink_odd):
    out = []
    for e, sink in enumerate((sink_even, sink_odd)):
        se = s_t[e * BLOCK:(e + 1) * BLOCK]
        m = jnp.maximum(jnp.max(se, axis=0, keepdims=True), sink)
        ee = jnp.exp(se - m)
        es = jnp.exp(sink - m)
        inv = 1.0 / (jnp.sum(ee, axis=0, keepdims=True) + es)
        out.append((ee * inv, es * inv))
    return out


def _attn_rows(n):
    q0 = pl.multiple_of(n * BLOCK, BLOCK)
    k0 = pl.multiple_of(jnp.maximum(n - 1, 0) * BLOCK, BLOCK)
    return q0, k0, jnp.minimum(n, 1)


def _key_rows(qkv_ref, k0, q0, col):
    return jnp.concatenate([qkv_ref[pl.ds(k0, BLOCK), col:col + LANES], qkv_ref[pl.ds(q0, BLOCK), col:col + LANES]],
                           axis=0).astype(F32)


def _attn_fwd(qkv, qg_pair, kg_pair, sinks, nseq, seq):
    t = qkv.shape[0]
    dq = N_Q_HEADS * HEAD_DIM
    dkv = N_KV_HEADS * HEAD_DIM

    def body(sk_ref, qkv_ref, qg_ref, kg_ref, o_ref, bias_ref):
        @pl.when(pl.program_id(0) == 0)
        def _():
            _fill_bias(bias_ref)

        low = lax.broadcasted_iota(jnp.int32, (1, LANES), 1) < HEAD_DIM
        earlier = _earlier_block()
        qg = qg_ref[...] * QK_SCALE
        kg = kg_ref[...]

        def blk(n, carry):
            q0, k0, later = _attn_rows(n)
            for kt in range(dkv // LANES):
                kraw = _key_rows(qkv_ref, k0, q0, dq + kt * LANES)
                vraw = _key_rows(qkv_ref, k0, q0, dq + dkv + kt * LANES)
                kn, _, _ = _pair_norm(kraw, kg, low)
                for par in range(2):
                    kh = 2 * kt + par
                    k_pair = _kv_pair_rows(kn, par, low)
                    v_pair = _kv_pair_rows(vraw, par, low)
                    for jj in range(2):
                        j = 2 * kh + jj
                        qraw = qkv_ref[pl.ds(q0, BLOCK), j * LANES:(j + 1) * LANES].astype(F32)
                        qn, _, _ = _pair_norm(qraw, qg, low)
                        s_t = _merge_blocks(_dot(k_pair, qn.astype(BF16), NT), earlier) + bias_ref[later, j]
                        (p0, _), (p1, _) = _pair_softmax(s_t, sk_ref[0, 2 * j], sk_ref[0, 2 * j + 1])
                        p_t = _split_blocks((p0, p1), earlier)
                        o_ref[pl.ds(q0, BLOCK), j * LANES:(j + 1) * LANES] = _dot(p_t, v_pair, TN).astype(BF16)
            return carry

        lax.fori_loop(0, seq // BLOCK, blk, 0)

    return pl.pallas_call(
        body, name="attn_fwd", grid=(nseq,),
        in_specs=[pl.BlockSpec(memory_space=pltpu.SMEM),
                  pl.BlockSpec((seq, dq + 2 * dkv), lambda b: (b, 0)),
                  pl.BlockSpec((1, LANES), lambda b: (0, 0)),
                  pl.BlockSpec((1, LANES), lambda b: (0, 0))],
        out_specs=pl.BlockSpec((seq, dq), lambda b: (b, 0)),
        out_shape=_sds((t, dq), BF16),
        scratch_shapes=[pltpu.VMEM((2, N_PAIRS, 2 * BLOCK, BLOCK), F32)],
        compiler_params=_params(("arbitrary",)))(sinks, qkv, qg_pair, kg_pair)


def _attn_bwd(do, qkv, qg_pair, kg_pair, sinks, nseq, seq):
    t = qkv.shape[0]
    dq = N_Q_HEADS * HEAD_DIM
    dkv = N_KV_HEADS * HEAD_DIM

    def body(sk_ref, do_ref, qkv_ref, qg_ref, kg_ref, o_ref, dqg_ref, dkg_ref, dsk_ref, acc_ref, bias_ref):
        @pl.when(pl.program_id(0) == 0)
        def _():
            _fill_bias(bias_ref)
            dqg_ref[...] = jnp.zeros_like(dqg_ref)
            dkg_ref[...] = jnp.zeros_like(dkg_ref)
            dsk_ref[...] = jnp.zeros_like(dsk_ref)

        acc_ref[...] = jnp.zeros_like(acc_ref)
        low = lax.broadcasted_iota(jnp.int32, (1, LANES), 1) < HEAD_DIM
        earlier = _earlier_block()
        head_row = lax.broadcasted_iota(jnp.int32, (N_Q_HEADS, LANES), 0)
        qg = qg_ref[...] * QK_SCALE
        kg = kg_ref[...]

        def blk(n, carry):
            dqg_acc, dkg_acc, dsk_acc = carry
            q0, k0, later = _attn_rows(n)
            for kt in range(dkv // LANES):
                kraw = _key_rows(qkv_ref, k0, q0, dq + kt * LANES)
                vraw = _key_rows(qkv_ref, k0, q0, dq + dkv + kt * LANES)
                kn, khat, rk = _pair_norm(kraw, kg, low)
                dk_tile = None
                dv_tile = None
                for par in range(2):
                    kh = 2 * kt + par
                    own = low if par == 0 else jnp.logical_not(low)
                    k_pair = _kv_pair_rows(kn, par, low)
                    v_pair = _kv_pair_rows(vraw, par, low)
                    dkn_rows = jnp.zeros((2 * KEYS, LANES), F32)
                    dv_rows = jnp.zeros((2 * KEYS, LANES), F32)
                    for jj in range(2):
                        j = 2 * kh + jj
                        qraw = qkv_ref[pl.ds(q0, BLOCK), j * LANES:(j + 1) * LANES].astype(F32)
                        qn, qhat, rq = _pair_norm(qraw, qg, low)
                        qn_b = qn.astype(BF16)
                        do_b = do_ref[pl.ds(q0, BLOCK), j * LANES:(j + 1) * LANES]
                        s_t = _merge_blocks(_dot(k_pair, qn_b, NT), earlier) + bias_ref[later, j]
                        dp_t = _merge_blocks(_dot(v_pair, do_b, NT), earlier)
                        ds_heads = []
                        probs = _pair_softmax(s_t, sk_ref[0, 2 * j], sk_ref[0, 2 * j + 1])
                        for e, (p, ps) in enumerate(probs):
                            dp = dp_t[e * BLOCK:(e + 1) * BLOCK]
                            dsum = jnp.sum(p * dp, axis=0, keepdims=True)
                            ds_heads.append(p * (dp - dsum))
                            dsk_acc = dsk_acc - jnp.where(head_row == 2 * j + e, ps * dsum, 0.0)
                        p_t = _split_blocks((probs[0][0], probs[1][0]), earlier)
                        ds_t = _split_blocks(ds_heads, earlier)
                        dv_rows = dv_rows + _dot(p_t, do_b, NN)
                        dkn_rows = dkn_rows + _dot(ds_t, qn_b, NN)
                        dqn = _dot(ds_t, k_pair, TN)
                        dqg_acc = dqg_acc + jnp.sum(dqn * qhat, axis=0, keepdims=True)
                        dqhat = dqn * qg
                        prod = dqhat * qhat
                        m_lo = jnp.sum(jnp.where(low, prod, 0.0), axis=-1, keepdims=True)
                        m_hi = jnp.sum(jnp.where(low, 0.0, prod), axis=-1, keepdims=True)
                        mean = jnp.where(low, m_lo, m_hi) * (1.0 / HEAD_DIM)
                        o_ref[pl.ds(q0, BLOCK), j * LANES:(j + 1) * LANES] = (rq * (dqhat - qhat * mean)).astype(BF16)
                    dkn_acc = jnp.where(low, dkn_rows[0:KEYS], dkn_rows[KEYS:2 * KEYS])
                    dv_acc = jnp.where(low, dv_rows[0:KEYS], dv_rows[KEYS:2 * KEYS])
                    dkn = dkn_acc + pltpu.roll(dkn_acc, HEAD_DIM, 1)
                    dvh = dv_acc + pltpu.roll(dv_acc, HEAD_DIM, 1)
                    khat_own = jnp.where(own, khat, 0.0)
                    khat_dup = khat_own + pltpu.roll(khat_own, HEAD_DIM, 1)
                    dkg_acc = dkg_acc + jnp.sum(jnp.where(own, dkn * khat_dup, 0.0), axis=0, keepdims=True)
                    dkhat = dkn * kg
                    mean_k = jnp.sum(dkhat * khat_dup, axis=-1, keepdims=True) * (1.0 / LANES)
                    dk_raw = rk * (dkhat - khat_dup * mean_k)
                    dk_tile = jnp.where(own, dk_raw, 0.0) if dk_tile is None else jnp.where(own, dk_raw, dk_tile)
                    dv_tile = jnp.where(own, dvh, 0.0) if dv_tile is None else jnp.where(own, dvh, dv_tile)
                for r0, part in ((k0, slice(0, BLOCK)), (q0, slice(BLOCK, KEYS))):
                    acc_ref[pl.ds(r0, BLOCK), kt * LANES:(kt + 1) * LANES] += dk_tile[part]
                    acc_ref[pl.ds(r0, BLOCK), dkv + kt * LANES:dkv + (kt + 1) * LANES] += dv_tile[part]
            return dqg_acc, dkg_acc, dsk_acc

        zero = jnp.zeros((1, LANES), F32)
        carry = (zero, zero, jnp.zeros((N_Q_HEADS, LANES), F32))
        dqg_acc, dkg_acc, dsk_acc = lax.fori_loop(0, seq // BLOCK, blk, carry)
        dqg_ref[...] += dqg_acc * QK_SCALE
        dkg_ref[...] += dkg_acc
        dsk_ref[...] += dsk_acc
        o_ref[:, dq:dq + 2 * dkv] = acc_ref[...].astype(BF16)

    small = pl.BlockSpec((1, LANES), lambda b: (0, 0))
    heads = pl.BlockSpec((N_Q_HEADS, LANES), lambda b: (0, 0))
    return pl.pallas_call(
        body, name="attn_bwd", grid=(nseq,),
        in_specs=[pl.BlockSpec(memory_space=pltpu.SMEM),
                  pl.BlockSpec((seq, dq), lambda b: (b, 0)),
                  pl.BlockSpec((seq, dq + 2 * dkv), lambda b: (b, 0)),
                  small, small],
        out_specs=[pl.BlockSpec((seq, dq + 2 * dkv), lambda b: (b, 0)), small, small, heads],
        out_shape=[_sds((t, dq + 2 * dkv), BF16), _sds((1, LANES), F32), _sds((1, LANES), F32),
                   _sds((N_Q_HEADS, LANES), F32)],
        scratch_shapes=[pltpu.VMEM((seq, 2 * dkv), F32), pltpu.VMEM((2, N_PAIRS, 2 * BLOCK, BLOCK), F32)],
        compiler_params=_params(("arbitrary",)))(sinks, do, qkv, qg_pair, kg_pair)


def _place():
    x, y, c = lax.axis_index("x"), lax.axis_index("y"), lax.axis_index("c")
    other_chips = [(1 - x, y), (x, 1 - y), (1 - x, 1 - y)]
    return x, y, c, other_chips


def _half_rows(c, rows):
    rh = rows // 2
    return pl.ds(pl.multiple_of(c * rh, BF16_ROWS), rh)


def _cast_own(name, w, place, layer=None):
    nl, r, cdim = w.shape
    first = 0
    if layer is not None:
        nl, first = 1, layer
    rt = _row_tile(r, 4 * cdim, ELEMENTWISE_BLOCK)

    def body(s_ref, w_ref, o_ref):
        o_ref[...] = w_ref[...].astype(BF16)

    grid_spec = pltpu.PrefetchScalarGridSpec(
        num_scalar_prefetch=1, grid=(nl, r // rt),
        in_specs=[pl.BlockSpec((None, rt, cdim), lambda l, i, s: (first + l, i, 0))],
        out_specs=pl.BlockSpec((None, None, rt, cdim), lambda l, i, s: (l, s[1], i, 0)))
    return pl.pallas_call(
        body, name=name, grid_spec=grid_spec, out_shape=_sds((nl, N_CHIPS, r, cdim), BF16),
        compiler_params=_params(("parallel", "parallel")))(place, w)


def _gather_protocol(outs, shapes, send_sems, recv_sems):
    n = len(outs)
    x, y, c, other_chips = _place()
    me_chip = 2 * x + y
    sibling = (x, y, 1 - c)

    def rows(u, chip, half):
        return outs[u].at[:, chip, _half_rows(half, shapes[u][2]), :]

    def copy(sem, part, to):
        return pltpu.make_async_remote_copy(src_ref=part, dst_ref=part, send_sem=send_sems.at[sem],
                                            recv_sem=recv_sems.at[sem], device_id=to, device_id_type=MESH)

    sends = []
    for u in range(n):
        for k, chip in enumerate(other_chips):
            cp = copy(6 * u + k, rows(u, me_chip, c), (*chip, c))
            cp.start()
            sends.append(cp)
    for u in range(n):
        for k, chip in enumerate(other_chips):
            got = rows(u, 2 * chip[0] + chip[1], c)
            copy(6 * u + k, got, (*chip, c)).wait_recv()
            cp = copy(6 * u + 3 + k, got, sibling)
            cp.start()
            sends.append(cp)
    for u in range(n):
        for k, chip in enumerate(other_chips):
            copy(6 * u + 3 + k, rows(u, 2 * chip[0] + chip[1], 1 - c), sibling).wait_recv()
    for cp in sends:
        cp.wait_send()


def _hbm_ref(a):
    return jax.new_ref(a, memory_space=pltpu.MemorySpace.HBM)


def _sibling_peer():
    x, y, c, _ = _place()
    return [(x, y, 1 - c)]


def _chip_peers():
    x, y, c, other_chips = _place()
    return [(*chip, c) for chip in other_chips]


def _gather_peers():
    return _chip_peers() + _sibling_peer()


def _on_sequencer(name, collective_id, n_sems, peers, protocol, operands=(), out_types=()):
    n_in, n_out = len(operands), len(out_types)

    def launch(*refs):
        send_sems, recv_sems = refs[n_in + n_out:]
        barrier = pltpu.get_barrier_semaphore()
        targets = peers()
        for peer in targets:
            pl.semaphore_signal(barrier, inc=1, device_id=peer, device_id_type=MESH)
        pl.semaphore_wait(barrier, len(targets))
        protocol(refs[:n_in], refs[n_in:n_in + n_out], send_sems, recv_sems)

    return pl.kernel(
        launch, out_type=tuple(out_types), mesh=plsc.ScalarSubcoreMesh(axis_name="sequencer", num_cores=1), name=name,
        scratch_types=(pltpu.SemaphoreType.DMA((n_sems,)), pltpu.SemaphoreType.DMA((n_sems,))),
        compiler_params=pltpu.CompilerParams(collective_id=collective_id))(*operands)


def _seq_allgather(name, collective_id, bufs):
    shapes = [b.shape for b in bufs]
    refs = [_hbm_ref(b) for b in bufs]
    _on_sequencer(name, collective_id, 6 * len(bufs), _gather_peers,
                  lambda ins, outs, send_sems, recv_sems: _gather_protocol(refs, shapes, send_sems, recv_sems))
    return [r[...] for r in refs]


def _taps_protocol(block_ref, got_ref, send_sems, recv_sems, first_sem):
    x, y, c, other_chips = _place()
    copies = []
    for k, chip in enumerate(other_chips):
        cp = pltpu.make_async_remote_copy(src_ref=block_ref, dst_ref=got_ref.at[k], send_sem=send_sems.at[first_sem + k],
                                          recv_sem=recv_sems.at[first_sem + k], device_id=(*chip, c), device_id_type=MESH)
        cp.start()
        copies.append(cp)
    return copies


def _seq_allgather_conv(collective_id, bufs, cw_block):
    shapes = [b.shape for b in bufs]
    refs = [_hbm_ref(b) for b in bufs]

    def protocol(ins, outs, send_sems, recv_sems):
        taps = _taps_protocol(ins[0], outs[0], send_sems, recv_sems, 6 * len(bufs))
        _gather_protocol(refs, shapes, send_sems, recv_sems)
        for cp in taps:
            cp.wait_recv()
        for cp in taps:
            cp.wait_send()

    (got,) = _on_sequencer("allgather_conv", collective_id, 6 * len(bufs) + 3, _gather_peers, protocol,
                           operands=(cw_block,), out_types=(_sds((3, *cw_block.shape), F32),))
    return [r[...] for r in refs], got


def _exchange_protocol(gs, outs, shapes, send_sems, recv_sems):
    x, y, c, _ = _place()
    sends = []
    for u in range(len(gs)):
        cp = pltpu.make_async_remote_copy(
            src_ref=gs[u].at[:, _half_rows(1 - c, shapes[u][1]), :], dst_ref=outs[u],
            send_sem=send_sems.at[u], recv_sem=recv_sems.at[u], device_id=(x, y, 1 - c), device_id_type=MESH)
        cp.start()
        sends.append(cp)
    for cp in sends:
        cp.wait_recv()
    for cp in sends:
        cp.wait_send()


def _seq_exchange(name, collective_id, grads):
    shapes = [g.shape for g in grads]
    return _on_sequencer(
        name, collective_id, len(grads), _sibling_peer,
        lambda gs, outs, send_sems, recv_sems: _exchange_protocol(gs, outs, shapes, send_sems, recv_sems),
        operands=grads, out_types=[_sds((s[0], s[1] // 2, s[2]), F32) for s in shapes])


def _sum_halves(name, g, got, place, after):
    _, r, cdim = g.shape
    rh = r // 2
    rt = _row_tile(rh, 4 * N_CHIPS * cdim, 2 * ELEMENTWISE_BLOCK)
    nr = rh // rt

    def body(s_ref, g_ref, got_ref, after_ref, pb_ref, pf_ref):
        pb_ref[...] = (g_ref[...] + got_ref[...]).astype(BF16)
        mine = s_ref[1]
        pf_ref[...] = g_ref[mine] + got_ref[mine]

    quarters = (N_CHIPS, rt, cdim)
    grid_spec = pltpu.PrefetchScalarGridSpec(
        num_scalar_prefetch=1, grid=(nr,),
        in_specs=[pl.BlockSpec(quarters, lambda i, s: (0, s[0] * nr + i, 0)),
                  pl.BlockSpec(quarters, lambda i, s: (0, i, 0)),
                  pl.BlockSpec(memory_space=pl.ANY)],
        out_specs=[pl.BlockSpec(quarters, lambda i, s: (0, i, 0)),
                   pl.BlockSpec((rt, cdim), lambda i, s: (i, 0))])
    return pl.pallas_call(
        body, name=name, grid_spec=grid_spec,
        out_shape=[_sds((N_CHIPS, rh, cdim), BF16), _sds((rh, cdim), F32)],
        compiler_params=_params(("parallel",)))(place, g, got, after)


def _scatter_protocol(ps, outs, send_sems, recv_sems):
    x, y, c, other_chips = _place()
    sends = []
    for u in range(len(ps)):
        for k, chip in enumerate(other_chips):
            cp = pltpu.make_async_remote_copy(
                src_ref=ps[u].at[2 * chip[0] + chip[1]], dst_ref=outs[u].at[k],
                send_sem=send_sems.at[3 * u + k], recv_sem=recv_sems.at[3 * u + k],
                device_id=(*chip, c), device_id_type=MESH)
            cp.start()
            sends.append(cp)
    for cp in sends:
        cp.wait_recv()
    for cp in sends:
        cp.wait_send()


def _seq_scatter(name, collective_id, partials):
    return _on_sequencer(
        name, collective_id, 3 * len(partials), _chip_peers, _scatter_protocol,
        operands=partials, out_types=[_sds((3, p.shape[1], p.shape[2]), BF16) for p in partials])


def _sum_partials(name, own, got, place, layer, nl, prev, after):
    rh, cdim = own.shape
    rt = _row_tile(rh, 4 * cdim, ELEMENTWISE_BLOCK)
    nr = rh // rt

    def body(s_ref, own_ref, got_ref, *rest):
        o_ref = rest[-1]
        o_ref[...] = ((own_ref[...] + got_ref[0].astype(F32)) + got_ref[1].astype(F32)) + got_ref[2].astype(F32)

    in_specs = [pl.BlockSpec((rt, cdim), lambda i, s: (i, 0)), pl.BlockSpec((3, rt, cdim), lambda i, s: (0, i, 0)),
                pl.BlockSpec(memory_space=pl.ANY)]
    args = [place, own, got, after]
    aliases = {}
    if prev is not None:
        in_specs.append(pl.BlockSpec(memory_space=pl.ANY))
        args.append(prev)
        aliases = {4: 0}
    grid_spec = pltpu.PrefetchScalarGridSpec(
        num_scalar_prefetch=1, grid=(nr,), in_specs=in_specs,
        out_specs=pl.BlockSpec((None, rt, cdim), lambda i, s: (layer, s[0] * nr + i, 0)))
    return pl.pallas_call(
        body, name=name, grid_spec=grid_spec, out_shape=_sds((nl, 2 * rh, cdim), F32),
        input_output_aliases=aliases, compiler_params=_params(("parallel",)))(*args)


def _share_protocol(outs, shapes, units, send_sems, recv_sems):
    x, y, c, _ = _place()
    sends = []
    for u, (w, l) in enumerate(units):
        mine = outs[w].at[l, _half_rows(c, shapes[w][1]), :]
        cp = pltpu.make_async_remote_copy(src_ref=mine, dst_ref=mine, send_sem=send_sems.at[u],
                                          recv_sem=recv_sems.at[u], device_id=(x, y, 1 - c), device_id_type=MESH)
        cp.start()
        sends.append(cp)
    for u, (w, l) in enumerate(units):
        theirs = outs[w].at[l, _half_rows(1 - c, shapes[w][1]), :]
        pltpu.make_async_remote_copy(src_ref=theirs, dst_ref=theirs, send_sem=send_sems.at[u],
                                     recv_sem=recv_sems.at[u], device_id=(x, y, 1 - c),
                                     device_id_type=MESH).wait_recv()
    for cp in sends:
        cp.wait_send()


def _seq_share(name, collective_id, bufs):
    shapes = [b.shape for b in bufs]
    units = [(w, l) for w in range(len(bufs)) for l in range(shapes[w][0])]
    refs = [_hbm_ref(b) for b in bufs]
    _on_sequencer(name, collective_id, len(units), _sibling_peer,
                  lambda ins, outs, send_sems, recv_sems: _share_protocol(refs, shapes, units, send_sems, recv_sems))
    return [r[...] for r in refs]


def _gather_blocks(block_ref, all_ref, send_sems, recv_sems):
    x, y, c, _ = _place()
    me = 4 * x + 2 * y + c
    all_ref[me] = block_ref[...]
    sends = []
    for rel in range(1, 8):
        fx, fy, fc = (rel >> 2) & 1, (rel >> 1) & 1, rel & 1
        peer = (x ^ fx, y ^ fy, c ^ fc)
        cp = pltpu.make_async_remote_copy(src_ref=block_ref, dst_ref=all_ref.at[me], send_sem=send_sems.at[rel - 1],
                                          recv_sem=recv_sems.at[rel - 1], device_id=peer, device_id_type=MESH)
        cp.start()
        sends.append(cp)
    for cp in sends:
        cp.wait_recv()
    for cp in sends:
        cp.wait_send()


def _adam(w, g, m, v):
    m_new = ADAM_B1 * m + (1.0 - ADAM_B1) * g
    v_new = ADAM_B2 * v + (1.0 - ADAM_B2) * (g * g)
    m_hat = m_new / (1.0 - ADAM_B1 ** ADAM_STEP)
    v_hat = v_new / (1.0 - ADAM_B2 ** ADAM_STEP)
    delta = -ADAM_LR * (m_hat / (jnp.sqrt(v_hat) + ADAM_EPS) + ADAM_WD * w)
    return delta, m_new, v_new


def _small_step(dnm0, dnm1, dnf0, dnf1, dcw, dqg, dkg, dsk, loss, w_blk, m_blk, v_blk, cw_cols):
    d = w_blk.shape[1]
    vm = pl.BlockSpec(memory_space=pltpu.VMEM)

    def reduce_body(dnm0_ref, dnm1_ref, dnf0_ref, dnf1_ref, dcw_ref, dqg_ref, dkg_ref, dsk_ref, loss_ref,
                    g_ref, blk_ref, all_ref, send_sems, recv_sems):
        blk_ref[...] = jnp.zeros_like(blk_ref)
        blk_ref[0:1, :] = jnp.sum(dnm0_ref[...], axis=0, keepdims=True)
        blk_ref[1:2, :] = jnp.sum(dnm1_ref[...], axis=0, keepdims=True)
        blk_ref[8:9, :] = jnp.sum(dnf0_ref[...], axis=0, keepdims=True)
        blk_ref[9:10, :] = jnp.sum(dnf1_ref[...], axis=0, keepdims=True)
        blk_ref[16:19, :] = dcw_ref[...]
        dqg_v = dqg_ref[...]
        dkg_v = dkg_ref[...]
        blk_ref[24:25, 0:LANES] = dqg_v + pltpu.roll(dqg_v, HEAD_DIM, 1)
        blk_ref[24:25, LANES:2 * LANES] = dkg_v + pltpu.roll(dkg_v, HEAD_DIM, 1)
        for h in range(N_Q_HEADS):
            blk_ref[24:25, 2 * LANES + h:2 * LANES + h + 1] = jnp.sum(dsk_ref[h:h + 1, :], axis=1, keepdims=True)
        blk_ref[24:25, 3 * LANES:4 * LANES] = jnp.broadcast_to(loss_ref[...], (1, LANES))
        _gather_blocks(blk_ref, all_ref, send_sems, recv_sems)
        g = all_ref[0]
        for dev in range(1, 8):
            g = g + all_ref[dev]
        g_ref[...] = g

    g_blk = pl.pallas_call(
        reduce_body, name="small_allreduce", in_specs=[vm] * 9, out_specs=vm, out_shape=_sds((SMALL_ROWS, d), F32),
        scratch_shapes=[pltpu.VMEM((SMALL_ROWS, d), F32), pltpu.VMEM((8, SMALL_ROWS, d), F32),
                        pltpu.SemaphoreType.DMA((7,)), pltpu.SemaphoreType.DMA((7,))],
    )(dnm0, dnm1, dnf0, dnf1, dcw, dqg, dkg, dsk, loss)

    def body(g_ref, w_ref, m_ref, v_ref, *out_refs):
        g = g_ref[...]
        out_refs[0][...] = g[24:25, 3 * LANES:3 * LANES + 1]
        chip = 2 * lax.axis_index("x") + lax.axis_index("y")
        for i, blk in enumerate((g, *_adam(w_ref[...], g, m_ref[...], v_ref[...]))):
            nm_ref, nf_ref, cw_ref, qg_ref, kg_ref, sk_ref = out_refs[1 + 6 * i:7 + 6 * i]
            nm_ref[...] = blk[0:2]
            nf_ref[...] = blk[8:10]
            qg_ref[...] = blk[24:25, 0:HEAD_DIM]
            kg_ref[...] = blk[24:25, LANES:LANES + HEAD_DIM]
            sk_ref[...] = blk[24:25, 2 * LANES:2 * LANES + N_Q_HEADS]
            for q in range(N_CHIPS):
                @pl.when(chip == q)
                def _(blk=blk, cw_ref=cw_ref, q=q):
                    cw_ref[0] = blk[16:19, q * cw_cols:(q + 1) * cw_cols]

    group = [_sds((2, d), F32), _sds((2, d), F32), _sds((1, 3, cw_cols), F32), _sds((1, HEAD_DIM), F32),
             _sds((1, HEAD_DIM), F32), _sds((1, N_Q_HEADS), F32)]
    outs = pl.pallas_call(
        body, name="small_adam", in_specs=[vm] * 4, out_specs=[vm] * 25, out_shape=[_sds((1, 1), F32)] + group * 4,
    )(g_blk, w_blk, m_blk, v_blk)
    names = ("norm_mixer", "norm_ffn", "conv_w", "attn_q_gain", "attn_k_gain", "attn_sinks")
    return outs[0], [dict(zip(names, outs[1 + 6 * i:7 + 6 * i])) for i in range(4)]


def _adam_step(name, w, g, m, v):
    nl, r, cdim = w.shape
    rt = _row_tile(r, 4 * cdim, ELEMENTWISE_BLOCK)

    def body(w_ref, g_ref, m_ref, v_ref, go_ref, d_ref, mo_ref, vo_ref):
        gv = g_ref[...]
        go_ref[...] = gv
        delta, m_new, v_new = _adam(w_ref[...], gv, m_ref[...], v_ref[...])
        d_ref[...] = delta
        mo_ref[...] = m_new
        vo_ref[...] = v_new

    spec = pl.BlockSpec((None, rt, cdim), lambda l, i: (l, i, 0))
    return pl.pallas_call(
        body, name=name, grid=(nl, r // rt), in_specs=[spec] * 4, out_specs=[spec] * 4,
        out_shape=[_sds(w.shape, F32)] * 4,
        compiler_params=_params(("parallel", "parallel")))(w, g, m, v)


def _pad_rows(a, rows=SUBLANES):
    return jnp.pad(a, ((0, rows - a.shape[0]), (0, 0)))


def _small_block(nm, nf, cw_local, qg, kg, sk, chip):
    d = nm.shape[1]
    cw_rows = lax.dynamic_update_slice(jnp.zeros((SUBLANES, d), F32), cw_local, (0, chip * cw_local.shape[1]))
    misc = jnp.concatenate([qg, qg, kg, kg, jnp.pad(sk, ((0, 0), (0, LANES - sk.shape[1]))),
                            jnp.zeros((1, d - 3 * LANES), F32)], axis=1)
    return jnp.concatenate([_pad_rows(nm), _pad_rows(nf), cw_rows, _pad_rows(misc)], axis=0)


WEIGHT_NAMES = ("conv_w_in", "conv_w", "conv_w_out", "attn_w_qkv", "attn_q_gain", "attn_k_gain", "attn_sinks",
                "attn_w_o", "norm_mixer", "norm_ffn", "ffn_w_gate_up", "ffn_w_down")
BIG = ("conv_w_in", "conv_w_out", "attn_w_qkv", "attn_w_o", "ffn_w_gate_up", "ffn_w_down")


def kernel(x, conv_w_in, conv_w, conv_w_out, attn_w_qkv, attn_q_gain, attn_k_gain, attn_sinks, attn_w_o, norm_mixer, norm_ffn, ffn_w_gate_up, ffn_w_down, loss_target, m_conv_w_in, m_conv_w, m_conv_w_out, m_attn_w_qkv, m_attn_q_gain, m_attn_k_gain, m_attn_sinks, m_attn_w_o, m_norm_mixer, m_norm_ffn, m_ffn_w_gate_up, m_ffn_w_down, v_conv_w_in, v_conv_w, v_conv_w_out, v_attn_w_qkv, v_attn_q_gain, v_attn_k_gain, v_attn_sinks, v_attn_w_o, v_norm_mixer, v_norm_ffn, v_ffn_w_gate_up, v_ffn_w_down):
    w = dict(conv_w_in=conv_w_in, conv_w=conv_w, conv_w_out=conv_w_out, attn_w_qkv=attn_w_qkv,
             attn_q_gain=attn_q_gain, attn_k_gain=attn_k_gain, attn_sinks=attn_sinks, attn_w_o=attn_w_o,
             norm_mixer=norm_mixer, norm_ffn=norm_ffn, ffn_w_gate_up=ffn_w_gate_up, ffn_w_down=ffn_w_down)
    m = dict(conv_w_in=m_conv_w_in, conv_w=m_conv_w, conv_w_out=m_conv_w_out, attn_w_qkv=m_attn_w_qkv,
             attn_q_gain=m_attn_q_gain, attn_k_gain=m_attn_k_gain, attn_sinks=m_attn_sinks, attn_w_o=m_attn_w_o,
             norm_mixer=m_norm_mixer, norm_ffn=m_norm_ffn, ffn_w_gate_up=m_ffn_w_gate_up, ffn_w_down=m_ffn_w_down)
    v = dict(conv_w_in=v_conv_w_in, conv_w=v_conv_w, conv_w_out=v_conv_w_out, attn_w_qkv=v_attn_w_qkv,
             attn_q_gain=v_attn_q_gain, attn_k_gain=v_attn_k_gain, attn_sinks=v_attn_sinks, attn_w_o=v_attn_w_o,
             norm_mixer=v_norm_mixer, norm_ffn=v_norm_ffn, ffn_w_gate_up=v_ffn_w_gate_up, ffn_w_down=v_ffn_w_down)

    nseq, seq, d = x.shape
    t = nseq * seq
    chip = 2 * lax.axis_index("x") + lax.axis_index("y")
    core = lax.axis_index("c")
    place = jnp.stack([core, chip]).astype(jnp.int32)
    x0 = x.reshape(t, d)
    tgt = loss_target.reshape(t, d)

    cw_block = lax.dynamic_update_slice(jnp.zeros((SUBLANES, d), F32), conv_w[0], (0, chip * conv_w.shape[2]))
    def cast(k, layer=None):
        return _cast_own(f"cast_{k}" + ("" if layer is None else str(layer)), w[k], place, layer)

    (w_in,), cw_got = _seq_allgather_conv(1, [cast("conv_w_in")], cw_block)
    w_out, w_gu0, w_dn0 = _seq_allgather(
        "allgather_ffn0", 2, [cast("conv_w_out"), cast("ffn_w_gate_up", 0), cast("ffn_w_down", 0)])
    w_qkv, w_o, w_gu1, w_dn1 = _seq_allgather(
        "allgather_rest", 3, [cast("attn_w_qkv"), cast("attn_w_o"), cast("ffn_w_gate_up", 1), cast("ffn_w_down", 1)])
    w_out = w_out.reshape(1, d, d)
    w_o = w_o.reshape(1, d, d)
    w_gu = [w_gu0, w_gu1]
    w_dn = [w_dn0.reshape(1, D_FF, d), w_dn1.reshape(1, D_FF, d)]

    qg_pair = jnp.concatenate([attn_q_gain, attn_q_gain], axis=1)
    kg_pair = jnp.concatenate([attn_k_gain, attn_k_gain], axis=1)

    def ffn_bwd(i, dxo, xin, h, g, u, a):
        g_dn = _wgrad_down(f"ffn{i}_down_wgrad", a, dxo, D_FF // 2)
        dg, du = _mm_down_t_swiglu(f"ffn{i}_down_dgrad", dxo, w_dn[i], 0, g, u)
        g_gu = _wgrad_up2(f"ffn{i}_up_wgrad", h, dg, du)
        dxi, dgain = _dgrad_norm_ffn(f"ffn{i}_up_dgrad", dg, du, w_gu[i], 0, xin, norm_ffn[i:i + 1], dxo)
        return dxi, dgain, g_gu, g_dn

    h0, bcx = _mm_norm_up_joined("conv_in", x0, norm_mixer[0:1], w_in, 512)
    z = _conv_fwd(bcx, cw_block, cw_got, nseq, seq)
    x1, h1 = _mm_down_norm("conv_out", z, w_out, 0, x0, norm_ffn[0:1])
    g0, u0, a0 = _mm_up_swiglu("ffn0_up", h1, w_gu[0], 0)
    x2, h2 = _mm_down_norm("ffn0_down", a0, w_dn[0], 0, x1, norm_mixer[1:2])
    qkv = _mm_up_joined("attn_qkv", h2, w_qkv, 1024)
    o = _attn_fwd(qkv, qg_pair, kg_pair, attn_sinks, nseq, seq)
    x3, h3 = _mm_down_norm("attn_out", o, w_o, 0, x2, norm_ffn[1:2])
    g1, u1, a1 = _mm_up_swiglu("ffn1_up", h3, w_gu[1], 0)
    dy, loss_part = _mm_down_loss("ffn1_down", a1, w_dn[1], 0, x3, tgt)

    finished = {k: None for k in BIG}

    def exchange(tag, cid, units):
        return units, _seq_exchange(f"exchange_{tag}", cid, [g for _, _, g in units])

    def scatter(tag, cid, group, after):
        units, got = group
        sums = [_sum_halves(f"sum_halves_{k}{l}", g, r, place, after) for (k, l, g), r in zip(units, got)]
        return units, sums, _seq_scatter(f"scatter_{tag}", cid, [pb for pb, _ in sums])

    def finish(group, after):
        units, sums, arrived = group
        for (k, l, _), (_, pf), r in zip(units, sums, arrived):
            finished[k] = _sum_partials(f"sum_partials_{k}{l}", pf, r, place, l, w[k].shape[0], finished[k], after)

    dx3, dnf1, g_gu1, g_dn1 = ffn_bwd(1, dy, x3, h3, g1, u1, a1)
    ffn1 = exchange("ffn1", 4, [("ffn_w_down", 1, g_dn1), ("ffn_w_gate_up", 1, g_gu1)])
    g_o = _wgrad_down("attn_out_wgrad", o, dx3, d)
    do = _mm_down_t("attn_out_dgrad", dx3, w_o, 0)
    ffn1 = scatter("ffn1", 8, ffn1, do)
    dqkv, dqg, dkg, dsk = _attn_bwd(do, qkv, qg_pair, kg_pair, attn_sinks, nseq, seq)
    g_qkv = _wgrad_joined("attn_qkv_wgrad", h2, dqkv)
    attn = exchange("attn", 5, [("attn_w_o", 0, g_o), ("attn_w_qkv", 0, g_qkv)])
    dx2, dnm1 = _dgrad_norm_qkv("attn_qkv_dgrad", dqkv, w_qkv, x2, norm_mixer[1:2], dx3)
    finish(ffn1, dx2)
    attn = scatter("attn", 9, attn, dx2)
    dx1, dnf0, g_gu0, g_dn0 = ffn_bwd(0, dx2, x1, h1, g0, u0, a0)
    ffn0 = exchange("ffn0", 6, [("ffn_w_down", 0, g_dn0), ("ffn_w_gate_up", 0, g_gu0)])
    g_out = _wgrad_down("conv_out_wgrad", z, dx1, d)
    dz = _mm_down_t("conv_out_dgrad", dx1, w_out, 0)
    finish(attn, dz)
    ffn0 = scatter("ffn0", 10, ffn0, dz)
    dbcx, dcw = _conv_bwd(dz, bcx, cw_block, cw_got, nseq, seq)
    g_in = _wgrad_conv_in("conv_in_wgrad", h0, dbcx, conv_w_in.shape[2])
    conv = exchange("conv", 7, [("conv_w_out", 0, g_out), ("conv_w_in", 0, g_in)])
    dx0, dnm0 = _dgrad_norm_conv("conv_in_dgrad", dbcx, w_in, x0, norm_mixer[0:1], dx1)
    finish(ffn0, dx0)
    late = ("attn_w_qkv", "attn_w_o", "ffn_w_gate_up", "ffn_w_down")
    grads_late = _seq_share("share_late", 12, [finished[k] for k in late])
    conv = scatter("conv", 11, conv, dx0)

    grad, delta, new_m, new_v = {}, {}, {}, {}

    def adam(k, g):
        grad[k], delta[k], new_m[k], new_v[k] = _adam_step(f"adam_{k}", w[k], g, m[k], v[k])

    for k, g in zip(late, grads_late):
        adam(k, g)

    def blocks(src):
        return _small_block(src["norm_mixer"], src["norm_ffn"], src["conv_w"][0], src["attn_q_gain"],
                            src["attn_k_gain"], src["attn_sinks"], chip)

    loss, small = _small_step(dnm0, dnm1, dnf0, dnf1, dcw, dqg, dkg, dsk, loss_part,
                              blocks(w), blocks(m), blocks(v), conv_w.shape[2])
    for dst, part in zip((grad, delta, new_m, new_v), small):
        dst.update(part)

    done = sum(new_v[k][0, 0:1, 0:1] for k in late) + loss
    finish(conv, done)
    last = ("conv_w_in", "conv_w_out")
    for k, g in zip(last, _seq_share("share_last", 13, [finished[k] for k in last])):
        adam(k, g)

    return (loss.reshape(()), dx0.reshape(nseq, seq, d), *[grad[k] for k in WEIGHT_NAMES], *[delta[k] for k in WEIGHT_NAMES],
            *[new_m[k] for k in WEIGHT_NAMES], *[new_v[k] for k in WEIGHT_NAMES])
```

```python
import jax
import jax.numpy as jnp
from jax import lax
from jax.experimental import pallas as pl
from jax.experimental.pallas import tpu as pltpu
from jax.experimental.pallas import tpu_sc as plsc

F32 = jnp.float32
BF16 = jnp.bfloat16

D_FF = 2816
N_Q_HEADS = 16
N_KV_HEADS = 4
HEAD_DIM = 64
WINDOW = 128
BLOCK = 128
EPS = 1e-6
N_CHIPS = 4
LANES = 128
SUBLANES = 8
BF16_ROWS = 16
MXU_COLS = 256
VMEM_LIMIT = 48 * 1024 * 1024
ADAM_LR, ADAM_B1, ADAM_B2, ADAM_EPS, ADAM_WD, ADAM_STEP = 0.001, 0.9, 0.999, 1e-08, 0.01, 10
ALIBI_SLOPES = tuple(2.0 ** (-8.0 * (h + 1) / N_Q_HEADS) for h in range(N_Q_HEADS))
SMALL_ROWS = 32
ROW_NORM_MIXER, ROW_NORM_FFN, ROW_CONV_W, ROW_MISC = 0, 8, 16, 24
TILE_Q_GAIN, TILE_K_GAIN, TILE_SINKS, TILE_LOSS = 0, 1, 2, 3
MESH = pl.DeviceIdType.MESH

NN = ((1,), (0,))
NT = ((1,), (1,))
TN = ((0,), (0,))


def _dot(a, b, dims):
    return lax.dot_general(a, b, (dims, ((), ())), preferred_element_type=F32)


def _pick(n, cands):
    for c in cands:
        if n % c == 0:
            return c
    raise ValueError((n, cands))


def _row_tile(rows, row_bytes, cap_bytes):
    fits = [r for r in range(BF16_ROWS, rows + 1, BF16_ROWS) if rows % r == 0 and r * row_bytes <= cap_bytes]
    if not fits:
        raise ValueError((rows, row_bytes, cap_bytes))
    return fits[-1]


ELEMENTWISE_BLOCK = 3 << 19


def _resident(block_shape, index_map):
    return pl.BlockSpec(block_shape, index_map, pipeline_mode=pl.Buffered(1))


def _params(sem):
    return pltpu.CompilerParams(dimension_semantics=sem, vmem_limit_bytes=VMEM_LIMIT)


def _sds(shape, dtype):
    return jax.ShapeDtypeStruct(shape, dtype)


def _rms(xv):
    return lax.rsqrt(jnp.mean(xv * xv, axis=-1, keepdims=True) + EPS)


def _sigmoid(g):
    return 1.0 / (1.0 + jnp.exp(-g))


def _mm_up_joined(name, a, w4, tm_pref):
    t, k = a.shape
    _, _, _, nq = w4.shape
    tm = _pick(t, (tm_pref, 256, 128))

    def body(a_ref, w_ref, o_ref, wcat_ref):
        @pl.when(pl.program_id(0) == 0)
        def _():
            for q in range(N_CHIPS):
                wcat_ref[:, q * nq:(q + 1) * nq] = w_ref[q]

        o_ref[...] = _dot(a_ref[...], wcat_ref[...], NN).astype(BF16)

    return pl.pallas_call(
        body, name=name, grid=(t // tm,),
        in_specs=[pl.BlockSpec((tm, k), lambda i: (i, 0)),
                  pl.BlockSpec((None, N_CHIPS, k, nq), lambda i: (0, 0, 0, 0))],
        out_specs=pl.BlockSpec((tm, N_CHIPS * nq), lambda i: (i, 0)),
        out_shape=_sds((t, N_CHIPS * nq), BF16),
        scratch_shapes=[pltpu.VMEM((k, N_CHIPS * nq), BF16)],
        compiler_params=_params(("arbitrary",)))(a, w4)


def _mm_norm_up_joined(name, x, gain, w4, tm_pref):
    t, k = x.shape
    _, _, _, nq = w4.shape
    tm = _pick(t, (tm_pref, 256, 128))

    def body(x_ref, g_ref, w_ref, h_ref, o_ref, wcat_ref):
        @pl.when(pl.program_id(0) == 0)
        def _():
            for q in range(N_CHIPS):
                wcat_ref[:, q * nq:(q + 1) * nq] = w_ref[q]

        xv = x_ref[...]
        h = ((xv * _rms(xv)) * g_ref[...]).astype(BF16)
        h_ref[...] = h
        o_ref[...] = _dot(h, wcat_ref[...], NN).astype(BF16)

    return pl.pallas_call(
        body, name=name, grid=(t // tm,),
        in_specs=[pl.BlockSpec((tm, k), lambda i: (i, 0)), pl.BlockSpec((1, k), lambda i: (0, 0)),
                  _resident((None, N_CHIPS, k, nq), lambda i: (0, 0, 0, 0))],
        out_specs=[pl.BlockSpec((tm, k), lambda i: (i, 0)), pl.BlockSpec((tm, N_CHIPS * nq), lambda i: (i, 0))],
        out_shape=[_sds((t, k), BF16), _sds((t, N_CHIPS * nq), BF16)],
        scratch_shapes=[pltpu.VMEM((k, N_CHIPS * nq), BF16)],
        compiler_params=_params(("arbitrary",)))(x, gain, w4)


def _mm_up_swiglu(name, h, w4, layer):
    t, k = h.shape
    _, _, _, nq = w4.shape
    tm = _pick(t, (512, 256, 128))

    def body(h_ref, wg_ref, wu_ref, dag_ref, dau_ref, a_ref):
        hv = h_ref[...]
        g = _dot(hv, wg_ref[...], NN)
        u = _dot(hv, wu_ref[...], NN)
        sg = _sigmoid(g)
        silu = g * sg
        dag_ref[...] = (u * (sg * (1.0 + g * (1.0 - sg)))).astype(BF16)
        dau_ref[...] = silu.astype(BF16)
        a_ref[...] = (silu * u).astype(BF16)

    half = N_CHIPS // 2
    out = pl.BlockSpec((tm, nq), lambda j, i: (i, j))
    return pl.pallas_call(
        body, name=name, grid=(half, t // tm),
        in_specs=[pl.BlockSpec((tm, k), lambda j, i: (i, 0)),
                  pl.BlockSpec((None, None, k, nq), lambda j, i: (layer, j, 0, 0)),
                  pl.BlockSpec((None, None, k, nq), lambda j, i: (layer, half + j, 0, 0))],
        out_specs=[out, out, out],
        out_shape=[_sds((t, half * nq), BF16)] * 3,
        compiler_params=_params(("parallel", "parallel")))(h, w4, w4)


def _mm_down_norm(name, a, w, layer, res, gain):
    t, kf = a.shape
    _, _, n = w.shape
    tm = _pick(t, (1024, 512, 256, 128))

    def body(a_ref, w_ref, r_ref, g_ref, o_ref, h_ref):
        xo = r_ref[...] + _dot(a_ref[...], w_ref[...], NN)
        o_ref[...] = xo
        h_ref[...] = ((xo * _rms(xo)) * g_ref[...]).astype(BF16)

    row = pl.BlockSpec((tm, n), lambda i: (i, 0))
    return pl.pallas_call(
        body, name=name, grid=(t // tm,),
        in_specs=[pl.BlockSpec((tm, kf), lambda i: (i, 0)),
                  _resident((None, kf, n), lambda i: (layer, 0, 0)),
                  row, pl.BlockSpec((1, n), lambda i: (0, 0))],
        out_specs=[row, row],
        out_shape=[_sds((t, n), F32), _sds((t, n), BF16)],
        compiler_params=_params(("parallel",)))(a, w, res, gain)


def _mm_down_loss(name, a, w, layer, res, tgt):
    t, kf = a.shape
    _, _, n = w.shape
    tm = _pick(t, (1024, 512, 256, 128))
    steps = t // tm

    def body(a_ref, w_ref, r_ref, t_ref, dy_ref, l_ref, acc_ref):
        i = pl.program_id(0)

        @pl.when(i == 0)
        def _():
            acc_ref[...] = jnp.zeros_like(acc_ref)

        e = (r_ref[...] + _dot(a_ref[...], w_ref[...], NN)) - t_ref[...]
        dy_ref[...] = e * (1.0 / n)
        acc_ref[...] += (e * e).reshape(tm // SUBLANES, SUBLANES, n).sum(axis=0)

        @pl.when(i == steps - 1)
        def _():
            l_ref[...] = jnp.sum(acc_ref[...], keepdims=True) * (0.5 / n)

    row = pl.BlockSpec((tm, n), lambda i: (i, 0))
    return pl.pallas_call(
        body, name=name, grid=(steps,),
        in_specs=[pl.BlockSpec((tm, kf), lambda i: (i, 0)),
                  _resident((None, kf, n), lambda i: (layer, 0, 0)), row, row],
        out_specs=[row, pl.BlockSpec((1, 1), lambda i: (0, 0))],
        out_shape=[_sds((t, n), F32), _sds((1, 1), F32)],
        scratch_shapes=[pltpu.VMEM((SUBLANES, n), F32)],
        compiler_params=_params(("arbitrary",)))(a, w, res, tgt)


def _mm_down_t(name, dx, w, layer):
    t, n = dx.shape
    _, kf, _ = w.shape
    tm = _pick(t, (1024, 512, 256, 128))

    def body(a_ref, w_ref, o_ref):
        o_ref[...] = _dot(a_ref[...].astype(BF16), w_ref[...], NT).astype(BF16)

    return pl.pallas_call(
        body, name=name, grid=(t // tm,),
        in_specs=[pl.BlockSpec((tm, n), lambda i: (i, 0)),
                  _resident((None, kf, n), lambda i: (layer, 0, 0))],
        out_specs=pl.BlockSpec((tm, kf), lambda i: (i, 0)),
        out_shape=_sds((t, kf), BF16),
        compiler_params=_params(("parallel",)))(dx, w)


def _mm_down_t_swiglu(name, dx, w, layer, g, u):
    t, n = dx.shape
    f = g.shape[1]
    tm = _pick(t, (512, 256, 128))

    def body(a_ref, w_ref, dag_ref, dau_ref, dg_ref, du_ref):
        da = _dot(a_ref[...].astype(BF16), w_ref[...], NT)
        dg_ref[...] = (da * dag_ref[...].astype(F32)).astype(BF16)
        du_ref[...] = (da * dau_ref[...].astype(F32)).astype(BF16)

    tile = pl.BlockSpec((tm, f), lambda i: (i, 0))
    return pl.pallas_call(
        body, name=name, grid=(t // tm,),
        in_specs=[pl.BlockSpec((tm, n), lambda i: (i, 0)),
                  _resident((None, f, n), lambda i: (layer, 0, 0)), tile, tile],
        out_specs=[tile, tile],
        out_shape=[_sds((t, f), BF16)] * 2,
        compiler_params=_params(("parallel",)))(dx, w, g, u)


def _dgrad_norm(name, acts, act_blocks, pieces, w4, layer, x, gain, dres, joined=False):
    t, d = x.shape
    _, _, k, nq = w4.shape
    tm = _pick(t, (512, 256, 128))
    n_act = len(acts)

    def body(*refs):
        act_refs = refs[:n_act]
        w_ref, x_ref, g_ref, dr_ref, dx_ref, dg_ref = refs[n_act:n_act + 6]

        @pl.when(pl.program_id(0) == 0)
        def _():
            dg_ref[...] = jnp.zeros_like(dg_ref)
            if joined:
                for q in range(N_CHIPS):
                    refs[-1][:, q * nq:(q + 1) * nq] = w_ref[q]

        dh = None
        for a_tile, w_tile in pieces(act_refs, refs[-1] if joined else w_ref):
            term = _dot(a_tile, w_tile, NT)
            dh = term if dh is None else dh + term
        xv = x_ref[...]
        r = _rms(xv)
        xhat = xv * r
        gd = dh * g_ref[...]
        dx_ref[...] = dr_ref[...] + r * (gd - xhat * jnp.mean(gd * xhat, axis=-1, keepdims=True))
        dg_ref[...] += (dh * xhat).reshape(tm // SUBLANES, SUBLANES, d).sum(axis=0)

    row = pl.BlockSpec((tm, d), lambda i: (i, 0))
    return pl.pallas_call(
        body, name=name, grid=(t // tm,),
        in_specs=[*act_blocks(tm),
                  _resident((None, N_CHIPS, k, nq), lambda i: (layer, 0, 0, 0)),
                  row, pl.BlockSpec((1, d), lambda i: (0, 0)), row],
        out_specs=[row, pl.BlockSpec((SUBLANES, d), lambda i: (0, 0))],
        out_shape=[_sds((t, d), F32), _sds((SUBLANES, d), F32)],
        scratch_shapes=[pltpu.VMEM((k, N_CHIPS * nq), BF16)] if joined else [],
        compiler_params=_params(("arbitrary",)))(*acts, w4, x, gain, dres)


def _dgrad_norm_ffn(name, dg, du, w4, layer, x, gain, dres):
    nq = w4.shape[3]
    f = dg.shape[1]

    def blocks(tm):
        return [pl.BlockSpec((tm, f), lambda i: (i, 0))] * 2

    def pieces(act_refs, w_ref):
        dg_ref, du_ref = act_refs
        return [(dg_ref[:, 0:nq], w_ref[0]), (dg_ref[:, nq:2 * nq], w_ref[1]),
                (du_ref[:, 0:nq], w_ref[2]), (du_ref[:, nq:2 * nq], w_ref[3])]

    return _dgrad_norm(name, [dg, du], blocks, pieces, w4, layer, x, gain, dres)


def _dgrad_norm_qkv(name, dqkv, w4, x, gain, dres):
    nq = w4.shape[3]

    def blocks(tm):
        return [pl.BlockSpec((tm, N_CHIPS * nq), lambda i: (i, 0))]

    def pieces(act_refs, w_ref):
        return [(act_refs[0][...], w_ref[...])]

    return _dgrad_norm(name, [dqkv], blocks, pieces, w4, 0, x, gain, dres, joined=True)


def _dgrad_norm_conv(name, d3, w4, x, gain, dres):
    _, _, d = d3.shape

    def blocks(tm):
        return [pl.BlockSpec((3, tm, d), lambda i: (0, i, 0))]

    def pieces(act_refs, w_ref):
        return [(jnp.concatenate([act_refs[0][part] for part in range(3)], axis=1), w_ref[...])]

    return _dgrad_norm(name, [d3], blocks, pieces, w4, 0, x, gain, dres, joined=True)


def _wgrad_up2(name, h, dg, du):
    t, k = h.shape
    nq = dg.shape[1] // 2
    tk = _pick(t, (1024, 512, 256, 128))
    steps = t // tk
    half = N_CHIPS // 2

    def body(h_ref, dg_ref, du_ref, o_ref):
        q = pl.program_id(0)

        @pl.when(pl.program_id(1) == 0)
        def _():
            o_ref[...] = jnp.zeros_like(o_ref)

        @pl.when(q < half)
        def _():
            o_ref[...] += _dot(h_ref[...], dg_ref[...], TN)

        @pl.when(q >= half)
        def _():
            o_ref[...] += _dot(h_ref[...], du_ref[...], TN)

    return pl.pallas_call(
        body, name=name, grid=(N_CHIPS, steps),
        in_specs=[pl.BlockSpec((tk, k), lambda q, s: (s, 0)),
                  pl.BlockSpec((tk, nq), lambda q, s: (jnp.where(q < half, s, steps - 1), jnp.minimum(q, half - 1))),
                  pl.BlockSpec((tk, nq), lambda q, s: (jnp.where(q >= half, s, 0), jnp.maximum(q - half, 0)))],
        out_specs=pl.BlockSpec((None, k, nq), lambda q, s: (q, 0, 0)),
        out_shape=_sds((N_CHIPS, k, nq), F32),
        compiler_params=_params(("parallel", "arbitrary")))(h, dg, du)


def _wgrad_joined(name, h, dy):
    t, k = h.shape
    nq = dy.shape[1] // N_CHIPS
    tk = _pick(t, (1024, 512, 256, 128))

    def body(h_ref, dy_ref, o_ref):
        @pl.when(pl.program_id(0) == 0)
        def _():
            o_ref[...] = jnp.zeros_like(o_ref)

        res = _dot(h_ref[...], dy_ref[...], TN)
        for q in range(N_CHIPS):
            o_ref[q] += res[:, q * nq:(q + 1) * nq]

    return pl.pallas_call(
        body, name=name, grid=(t // tk,),
        in_specs=[pl.BlockSpec((tk, k), lambda s: (s, 0)), pl.BlockSpec((tk, N_CHIPS * nq), lambda s: (s, 0))],
        out_specs=pl.BlockSpec((N_CHIPS, k, nq), lambda s: (0, 0, 0)),
        out_shape=_sds((N_CHIPS, k, nq), F32),
        compiler_params=_params(("arbitrary",)))(h, dy)


def _wgrad_conv_in(name, h, d3, nq):
    t, k = h.shape
    d = d3.shape[2]
    per_part, per_q = d // MXU_COLS, nq // MXU_COLS
    tk = _pick(t, (512, 256, 128))

    def body(h_ref, d_ref, o_ref):
        @pl.when(pl.program_id(0) == 0)
        def _():
            o_ref[...] = jnp.zeros_like(o_ref)

        hv = h_ref[...]
        for part in range(3):
            res = _dot(hv, d_ref[part], TN)
            for cc in range(per_part):
                jb = part * per_part + cc
                co = (jb % per_q) * MXU_COLS
                o_ref[jb // per_q, :, co:co + MXU_COLS] += res[:, cc * MXU_COLS:(cc + 1) * MXU_COLS]

    return pl.pallas_call(
        body, name=name, grid=(t // tk,),
        in_specs=[pl.BlockSpec((tk, k), lambda s: (s, 0)), pl.BlockSpec((3, tk, d), lambda s: (0, s, 0))],
        out_specs=pl.BlockSpec((N_CHIPS, k, nq), lambda s: (0, 0, 0)),
        out_shape=_sds((N_CHIPS, k, nq), F32),
        compiler_params=_params(("arbitrary",)))(h, d3)


def _wgrad_down(name, a, dx, tmw):
    t, kf = a.shape
    n = dx.shape[1]
    tk = _pick(t, (1024, 512, 256, 128))

    def body(a_ref, b_ref, o_ref):
        @pl.when(pl.program_id(1) == 0)
        def _():
            o_ref[...] = jnp.zeros_like(o_ref)

        o_ref[...] += _dot(a_ref[...], b_ref[...].astype(BF16), TN)

    g = pl.pallas_call(
        body, name=name, grid=(kf // tmw, t // tk),
        in_specs=[pl.BlockSpec((tk, tmw), lambda j, s: (s, j)), pl.BlockSpec((tk, n), lambda j, s: (s, 0))],
        out_specs=pl.BlockSpec((tmw, n), lambda j, s: (j, 0)),
        out_shape=_sds((kf, n), F32),
        compiler_params=_params(("parallel", "arbitrary")))(a, dx)
    return g.reshape(N_CHIPS, kf // N_CHIPS, n)


def _shift_rows(u, k, rows):
    s = u.shape[0]
    if k > 0:
        r = pltpu.roll(u, k, 0)
        return jnp.concatenate([jnp.where(rows >= k, r[0:SUBLANES], 0.0), r[SUBLANES:]], axis=0)
    r = pltpu.roll(u, s + k, 0)
    return jnp.concatenate([r[:s - SUBLANES], jnp.where(rows < SUBLANES + k, r[s - SUBLANES:], 0.0)], axis=0)


def _conv_taps(cw_ref, got_ref):
    return (cw_ref[...] + got_ref[0]) + (got_ref[1] + got_ref[2])


def _conv_fwd(bcx, cw, cw_got, nseq, seq):
    t, d3 = bcx.shape
    d = d3 // 3
    cb = 2 * MXU_COLS
    nj = d // cb

    def body(b_ref, c_ref, x_ref, cw_ref, got_ref, z_ref):
        u = b_ref[...].astype(F32) * x_ref[...].astype(F32)
        rows = lax.broadcasted_iota(jnp.int32, (SUBLANES, cb), 0)
        cwv = _conv_taps(cw_ref, got_ref)
        y = cwv[2:3] * u + cwv[1:2] * _shift_rows(u, 1, rows) + cwv[0:1] * _shift_rows(u, 2, rows)
        z_ref[...] = (c_ref[...].astype(F32) * y).astype(BF16)

    return pl.pallas_call(
        body, name="conv_fwd", grid=(nseq, nj),
        in_specs=[pl.BlockSpec((seq, cb), lambda b, j: (b, j)),
                  pl.BlockSpec((seq, cb), lambda b, j: (b, nj + j)),
                  pl.BlockSpec((seq, cb), lambda b, j: (b, 2 * nj + j)),
                  pl.BlockSpec((SUBLANES, cb), lambda b, j: (0, j)),
                  pl.BlockSpec((3, SUBLANES, cb), lambda b, j: (0, 0, j))],
        out_specs=pl.BlockSpec((seq, cb), lambda b, j: (b, j)),
        out_shape=_sds((t, d), BF16),
        compiler_params=_params(("parallel", "parallel")))(bcx, bcx, bcx, cw, cw_got)


def _conv_bwd(dz, bcx, cw, cw_got, nseq, seq):
    t, d3 = bcx.shape
    d = d3 // 3
    cb = MXU_COLS
    nj = d // cb

    def body(dz_ref, b_ref, c_ref, x_ref, cw_ref, got_ref, o_ref, dcw_ref):
        @pl.when(pl.program_id(1) == 0)
        def _():
            dcw_ref[...] = jnp.zeros_like(dcw_ref)

        b = b_ref[...].astype(F32)
        c = c_ref[...].astype(F32)
        xv = x_ref[...].astype(F32)
        dzv = dz_ref[...].astype(F32)
        u = b * xv
        rows = lax.broadcasted_iota(jnp.int32, (SUBLANES, cb), 0)
        u1 = _shift_rows(u, 1, rows)
        u2 = _shift_rows(u, 2, rows)
        cwv = _conv_taps(cw_ref, got_ref)
        y = cwv[2:3] * u + cwv[1:2] * u1 + cwv[0:1] * u2
        dyc = dzv * c
        du = cwv[2:3] * dyc + cwv[1:2] * _shift_rows(dyc, -1, rows) + cwv[0:1] * _shift_rows(dyc, -2, rows)
        o_ref[0] = (du * xv).astype(BF16)
        o_ref[1] = (dzv * y).astype(BF16)
        o_ref[2] = (du * b).astype(BF16)
        s0 = jnp.sum(dyc * u2, axis=0, keepdims=True)
        s1 = jnp.sum(dyc * u1, axis=0, keepdims=True)
        s2 = jnp.sum(dyc * u, axis=0, keepdims=True)
        tap = lax.broadcasted_iota(jnp.int32, (3, cb), 0)
        dcw_ref[...] += jnp.where(tap == 0, s0, jnp.where(tap == 1, s1, s2))

    return pl.pallas_call(
        body, name="conv_bwd", grid=(nj, nseq),
        in_specs=[pl.BlockSpec((seq, cb), lambda j, b: (b, j)),
                  pl.BlockSpec((seq, cb), lambda j, b: (b, j)),
                  pl.BlockSpec((seq, cb), lambda j, b: (b, nj + j)),
                  pl.BlockSpec((seq, cb), lambda j, b: (b, 2 * nj + j)),
                  pl.BlockSpec((SUBLANES, cb), lambda j, b: (0, j)),
                  pl.BlockSpec((3, SUBLANES, cb), lambda j, b: (0, 0, j))],
        out_specs=[pl.BlockSpec((3, seq, cb), lambda j, b: (0, b, j)),
                   pl.BlockSpec((3, cb), lambda j, b: (0, j))],
        out_shape=[_sds((3, t, d), BF16), _sds((3, d), F32)],
        compiler_params=_params(("parallel", "arbitrary")))(dz, bcx, bcx, bcx, cw, cw_got)


def _pair_norm(x, gain_pair, low):
    sq = x * x
    ss_lo = jnp.sum(jnp.where(low, sq, 0.0), axis=-1, keepdims=True)
    ss_hi = jnp.sum(jnp.where(low, 0.0, sq), axis=-1, keepdims=True)
    r = lax.rsqrt(jnp.where(low, ss_lo, ss_hi) * (1.0 / HEAD_DIM) + EPS)
    xhat = x * r
    return xhat * gain_pair, xhat, r


KEYS = 2 * BLOCK
QK_SCALE = 1.0 / (HEAD_DIM ** 0.5)
N_PAIRS = N_Q_HEADS // 2


def _earlier_block(shape=(BLOCK, BLOCK)):
    return lax.broadcasted_iota(jnp.int32, shape, 0) > lax.broadcasted_iota(jnp.int32, shape, 1)


def _fill_bias(bias_ref):
    rows = lax.broadcasted_iota(jnp.int32, (2 * BLOCK, BLOCK), 0)
    qi = lax.broadcasted_iota(jnp.int32, (2 * BLOCK, BLOCK), 1)
    odd_head = rows >= BLOCK
    kj = jnp.where(odd_head, rows - BLOCK, rows)
    earlier = kj > qi
    dist = (jnp.where(earlier, BLOCK, 0) + qi - kj).astype(F32)
    for j in range(N_PAIRS):
        slope = jnp.where(odd_head, ALIBI_SLOPES[2 * j + 1], ALIBI_SLOPES[2 * j])
        bias = -slope * dist
        bias_ref[1, j] = bias
        bias_ref[0, j] = jnp.where(earlier, -1e30, bias)


def _merge_blocks(x_t, earlier):
    return jnp.concatenate([jnp.where(earlier, x_t[e * KEYS:e * KEYS + BLOCK], x_t[e * KEYS + BLOCK:(e + 1) * KEYS])
                            for e in range(2)], axis=0)


def _split_blocks(heads, earlier):
    parts = []
    for x in heads:
        parts += [jnp.where(earlier, x, 0.0), jnp.where(earlier, 0.0, x)]
    return jnp.concatenate(parts, axis=0).astype(BF16)


def _kv_pair_rows(kv_tile, parity, low):
    own = jnp.where(low if parity == 0 else jnp.logical_not(low), kv_tile, 0.0)
    other = pltpu.roll(own, HEAD_DIM, 1)
    lo, hi = (own, other) if parity == 0 else (other, own)
    return jnp.concatenate([lo, hi], axis=0).astype(BF16)


def _pair_softmax(s_t, sink_even, sink_odd):
    out = []
    for e, sink in enumerate((sink_even, sink_odd)):
        se = s_t[e * BLOCK:(e + 1) * BLOCK]
        m = jnp.maximum(jnp.max(se, axis=0, keepdims=True), sink)
        ee = jnp.exp(se - m)
        es = jnp.exp(sink - m)
        inv = 1.0 / (jnp.sum(ee, axis=0, keepdims=True) + es)
        out.append((ee * inv, es * inv))
    return out


def _attn_rows(n):
    q0 = pl.multiple_of(n * BLOCK, BLOCK)
    k0 = pl.multiple_of(jnp.maximum(n - 1, 0) * BLOCK, BLOCK)
    return q0, k0, jnp.minimum(n, 1)


def _key_rows(qkv_ref, k0, q0, col):
    return jnp.concatenate([qkv_ref[pl.ds(k0, BLOCK), col:col + LANES], qkv_ref[pl.ds(q0, BLOCK), col:col + LANES]],
                           axis=0).astype(F32)


def _attn_fwd(qkv, qg_pair, kg_pair, sinks, nseq, seq):
    t = qkv.shape[0]
    dq = N_Q_HEADS * HEAD_DIM
    dkv = N_KV_HEADS * HEAD_DIM

    def body(sk_ref, qkv_ref, qg_ref, kg_ref, o_ref, bias_ref):
        @pl.when(pl.program_id(0) == 0)
        def _():
            _fill_bias(bias_ref)

        low = lax.broadcasted_iota(jnp.int32, (1, LANES), 1) < HEAD_DIM
        earlier = _earlier_block()
        qg = qg_ref[...] * QK_SCALE
        kg = kg_ref[...]

        def blk(n, carry):
            q0, k0, later = _attn_rows(n)
            for kt in range(dkv // LANES):
                kraw = _key_rows(qkv_ref, k0, q0, dq + kt * LANES)
                vraw = _key_rows(qkv_ref, k0, q0, dq + dkv + kt * LANES)
                kn, _, _ = _pair_norm(kraw, kg, low)
                for par in range(2):
                    kh = 2 * kt + par
                    k_pair = _kv_pair_rows(kn, par, low)
                    v_pair = _kv_pair_rows(vraw, par, low)
                    for jj in range(2):
                        j = 2 * kh + jj
                        qraw = qkv_ref[pl.ds(q0, BLOCK), j * LANES:(j + 1) * LANES].astype(F32)
                        qn, _, _ = _pair_norm(qraw, qg, low)
                        s_t = _merge_blocks(_dot(k_pair, qn.astype(BF16), NT), earlier) + bias_ref[later, j]
                        (p0, _), (p1, _) = _pair_softmax(s_t, sk_ref[0, 2 * j], sk_ref[0, 2 * j + 1])
                        p_t = _split_blocks((p0, p1), earlier)
                        o_ref[pl.ds(q0, BLOCK), j * LANES:(j + 1) * LANES] = _dot(p_t, v_pair, TN).astype(BF16)
            return carry

        lax.fori_loop(0, seq // BLOCK, blk, 0)

    return pl.pallas_call(
        body, name="attn_fwd", grid=(nseq,),
        in_specs=[pl.BlockSpec(memory_space=pltpu.SMEM),
                  pl.BlockSpec((seq, dq + 2 * dkv), lambda b: (b, 0)),
                  pl.BlockSpec((1, LANES), lambda b: (0, 0)),
                  pl.BlockSpec((1, LANES), lambda b: (0, 0))],
        out_specs=pl.BlockSpec((seq, dq), lambda b: (b, 0)),
        out_shape=_sds((t, dq), BF16),
        scratch_shapes=[pltpu.VMEM((2, N_PAIRS, 2 * BLOCK, BLOCK), F32)],
        compiler_params=_params(("arbitrary",)))(sinks, qkv, qg_pair, kg_pair)


def _attn_bwd(do, qkv, qg_pair, kg_pair, sinks, nseq, seq):
    t = qkv.shape[0]
    dq = N_Q_HEADS * HEAD_DIM
    dkv = N_KV_HEADS * HEAD_DIM

    def body(sk_ref, do_ref, qkv_ref, qg_ref, kg_ref, o_ref, dqg_ref, dkg_ref, dsk_ref, acc_ref, bias_ref):
        @pl.when(pl.program_id(0) == 0)
        def _():
            _fill_bias(bias_ref)
            dqg_ref[...] = jnp.zeros_like(dqg_ref)
            dkg_ref[...] = jnp.zeros_like(dkg_ref)
            dsk_ref[...] = jnp.zeros_like(dsk_ref)

        acc_ref[...] = jnp.zeros_like(acc_ref)
        low = lax.broadcasted_iota(jnp.int32, (1, LANES), 1) < HEAD_DIM
        earlier = _earlier_block()
        head_row = lax.broadcasted_iota(jnp.int32, (N_Q_HEADS, LANES), 0)
        qg = qg_ref[...] * QK_SCALE
        kg = kg_ref[...]

        def blk(n, carry):
            dqg_acc, dkg_acc, dsk_acc = carry
            q0, k0, later = _attn_rows(n)
            for kt in range(dkv // LANES):
                kraw = _key_rows(qkv_ref, k0, q0, dq + kt * LANES)
                vraw = _key_rows(qkv_ref, k0, q0, dq + dkv + kt * LANES)
                kn, khat, rk = _pair_norm(kraw, kg, low)
                dk_tile = None
                dv_tile = None
                for par in range(2):
                    kh = 2 * kt + par
                    own = low if par == 0 else jnp.logical_not(low)
                    k_pair = _kv_pair_rows(kn, par, low)
                    v_pair = _kv_pair_rows(vraw, par, low)
                    dkn_rows = jnp.zeros((2 * KEYS, LANES), F32)
                    dv_rows = jnp.zeros((2 * KEYS, LANES), F32)
                    for jj in range(2):
                        j = 2 * kh + jj
                        qraw = qkv_ref[pl.ds(q0, BLOCK), j * LANES:(j + 1) * LANES].astype(F32)
                        qn, qhat, rq = _pair_norm(qraw, qg, low)
                        qn_b = qn.astype(BF16)
                        do_b = do_ref[pl.ds(q0, BLOCK), j * LANES:(j + 1) * LANES]
                        s_t = _merge_blocks(_dot(k_pair, qn_b, NT), earlier) + bias_ref[later, j]
                        dp_t = _merge_blocks(_dot(v_pair, do_b, NT), earlier)
                        ds_heads = []
                        probs = _pair_softmax(s_t, sk_ref[0, 2 * j], sk_ref[0, 2 * j + 1])
                        for e, (p, ps) in enumerate(probs):
                            dp = dp_t[e * BLOCK:(e + 1) * BLOCK]
                            dsum = jnp.sum(p * dp, axis=0, keepdims=True)
                            ds_heads.append(p * (dp - dsum))
                            dsk_acc = dsk_acc - jnp.where(head_row == 2 * j + e, ps * dsum, 0.0)
                        p_t = _split_blocks((probs[0][0], probs[1][0]), earlier)
                        ds_t = _split_blocks(ds_heads, earlier)
                        dv_rows = dv_rows + _dot(p_t, do_b, NN)
                        dkn_rows = dkn_rows + _dot(ds_t, qn_b, NN)
                        dqn = _dot(ds_t, k_pair, TN)
                        dqg_acc = dqg_acc + jnp.sum(dqn * qhat, axis=0, keepdims=True)
                        dqhat = dqn * qg
                        prod = dqhat * qhat
                        m_lo = jnp.sum(jnp.where(low, prod, 0.0), axis=-1, keepdims=True)
                        m_hi = jnp.sum(jnp.where(low, 0.0, prod), axis=-1, keepdims=True)
                        mean = jnp.where(low, m_lo, m_hi) * (1.0 / HEAD_DIM)
                        o_ref[pl.ds(q0, BLOCK), j * LANES:(j + 1) * LANES] = (rq * (dqhat - qhat * mean)).astype(BF16)
                    dkn_acc = jnp.where(low, dkn_rows[0:KEYS], dkn_rows[KEYS:2 * KEYS])
                    dv_acc = jnp.where(low, dv_rows[0:KEYS], dv_rows[KEYS:2 * KEYS])
                    dkn = dkn_acc + pltpu.roll(dkn_acc, HEAD_DIM, 1)
                    dvh = dv_acc + pltpu.roll(dv_acc, HEAD_DIM, 1)
                    khat_own = jnp.where(own, khat, 0.0)
                    khat_dup = khat_own + pltpu.roll(khat_own, HEAD_DIM, 1)
                    dkg_acc = dkg_acc + jnp.sum(jnp.where(own, dkn * khat_dup, 0.0), axis=0, keepdims=True)
                    dkhat = dkn * kg
                    mean_k = jnp.sum(dkhat * khat_dup, axis=-1, keepdims=True) * (1.0 / LANES)
                    dk_raw = rk * (dkhat - khat_dup * mean_k)
                    dk_tile = jnp.where(own, dk_raw, 0.0) if dk_tile is None else jnp.where(own, dk_raw, dk_tile)
                    dv_tile = jnp.where(own, dvh, 0.0) if dv_tile is None else jnp.where(own, dvh, dv_tile)
                for r0, part in ((k0, slice(0, BLOCK)), (q0, slice(BLOCK, KEYS))):
                    acc_ref[pl.ds(r0, BLOCK), kt * LANES:(kt + 1) * LANES] += dk_tile[part]
                    acc_ref[pl.ds(r0, BLOCK), dkv + kt * LANES:dkv + (kt + 1) * LANES] += dv_tile[part]
            return dqg_acc, dkg_acc, dsk_acc

        zero = jnp.zeros((1, LANES), F32)
        carry = (zero, zero, jnp.zeros((N_Q_HEADS, LANES), F32))
        dqg_acc, dkg_acc, dsk_acc = lax.fori_loop(0, seq // BLOCK, blk, carry)
        dqg_ref[...] += dqg_acc * QK_SCALE
        dkg_ref[...] += dkg_acc
        dsk_ref[...] += dsk_acc
        o_ref[:, dq:dq + 2 * dkv] = acc_ref[...].astype(BF16)

    small = pl.BlockSpec((1, LANES), lambda b: (0, 0))
    heads = pl.BlockSpec((N_Q_HEADS, LANES), lambda b: (0, 0))
    return pl.pallas_call(
        body, name="attn_bwd", grid=(nseq,),
        in_specs=[pl.BlockSpec(memory_space=pltpu.SMEM),
                  pl.BlockSpec((seq, dq), lambda b: (b, 0)),
                  pl.BlockSpec((seq, dq + 2 * dkv), lambda b: (b, 0)),
                  small, small],
        out_specs=[pl.BlockSpec((seq, dq + 2 * dkv), lambda b: (b, 0)), small, small, heads],
        out_shape=[_sds((t, dq + 2 * dkv), BF16), _sds((1, LANES), F32), _sds((1, LANES), F32),
                   _sds((N_Q_HEADS, LANES), F32)],
        scratch_shapes=[pltpu.VMEM((seq, 2 * dkv), F32), pltpu.VMEM((2, N_PAIRS, 2 * BLOCK, BLOCK), F32)],
        compiler_params=_params(("arbitrary",)))(sinks, do, qkv, qg_pair, kg_pair)


def _place():
    x, y, c = lax.axis_index("x"), lax.axis_index("y"), lax.axis_index("c")
    other_chips = [(1 - x, y), (x, 1 - y), (1 - x, 1 - y)]
    return x, y, c, other_chips


def _half_rows(c, rows):
    rh = rows // 2
    return pl.ds(pl.multiple_of(c * rh, BF16_ROWS), rh)


def _cast_own(name, w, place, layer=None):
    nl, r, cdim = w.shape
    first = 0
    if layer is not None:
        nl, first = 1, layer
    rt = _row_tile(r, 4 * cdim, ELEMENTWISE_BLOCK)

    def body(s_ref, w_ref, o_ref):
        o_ref[...] = w_ref[...].astype(BF16)

    grid_spec = pltpu.PrefetchScalarGridSpec(
        num_scalar_prefetch=1, grid=(nl, r // rt),
        in_specs=[pl.BlockSpec((None, rt, cdim), lambda l, i, s: (first + l, i, 0))],
        out_specs=pl.BlockSpec((None, None, rt, cdim), lambda l, i, s: (l, s[1], i, 0)))
    return pl.pallas_call(
        body, name=name, grid_spec=grid_spec, out_shape=_sds((nl, N_CHIPS, r, cdim), BF16),
        compiler_params=_params(("parallel", "parallel")))(place, w)


def _gather_protocol(outs, shapes, send_sems, recv_sems):
    n = len(outs)
    x, y, c, other_chips = _place()
    me_chip = 2 * x + y
    sibling = (x, y, 1 - c)

    def rows(u, chip, half):
        return outs[u].at[:, chip, _half_rows(half, shapes[u][2]), :]

    def copy(sem, part, to):
        return pltpu.make_async_remote_copy(src_ref=part, dst_ref=part, send_sem=send_sems.at[sem],
                                            recv_sem=recv_sems.at[sem], device_id=to, device_id_type=MESH)

    sends = []
    for u in range(n):
        for k, chip in enumerate(other_chips):
            cp = copy(6 * u + k, rows(u, me_chip, c), (*chip, c))
            cp.start()
            sends.append(cp)
    for u in range(n):
        for k, chip in enumerate(other_chips):
            got = rows(u, 2 * chip[0] + chip[1], c)
            copy(6 * u + k, got, (*chip, c)).wait_recv()
            cp = copy(6 * u + 3 + k, got, sibling)
            cp.start()
            sends.append(cp)
    for u in range(n):
        for k, chip in enumerate(other_chips):
            copy(6 * u + 3 + k, rows(u, 2 * chip[0] + chip[1], 1 - c), sibling).wait_recv()
    for cp in sends:
        cp.wait_send()


def _hbm_ref(a):
    return jax.new_ref(a, memory_space=pltpu.MemorySpace.HBM)


def _sibling_peer():
    x, y, c, _ = _place()
    return [(x, y, 1 - c)]


def _chip_peers():
    x, y, c, other_chips = _place()
    return [(*chip, c) for chip in other_chips]


def _gather_peers():
    return _chip_peers() + _sibling_peer()


def _on_sequencer(name, collective_id, n_sems, peers, protocol, operands=(), out_types=()):
    n_in, n_out = len(operands), len(out_types)

    def launch(*refs):
        send_sems, recv_sems = refs[n_in + n_out:]
        barrier = pltpu.get_barrier_semaphore()
        targets = peers()
        for peer in targets:
            pl.semaphore_signal(barrier, inc=1, device_id=peer, device_id_type=MESH)
        pl.semaphore_wait(barrier, len(targets))
        protocol(refs[:n_in], refs[n_in:n_in + n_out], send_sems, recv_sems)

    return pl.kernel(
        launch, out_type=tuple(out_types), mesh=plsc.ScalarSubcoreMesh(axis_name="sequencer", num_cores=1), name=name,
        scratch_types=(pltpu.SemaphoreType.DMA((n_sems,)), pltpu.SemaphoreType.DMA((n_sems,))),
        compiler_params=pltpu.CompilerParams(collective_id=collective_id))(*operands)


def _seq_allgather(name, collective_id, bufs):
    shapes = [b.shape for b in bufs]
    refs = [_hbm_ref(b) for b in bufs]
    _on_sequencer(name, collective_id, 6 * len(bufs), _gather_peers,
                  lambda ins, outs, send_sems, recv_sems: _gather_protocol(refs, shapes, send_sems, recv_sems))
    return [r[...] for r in refs]


def _taps_protocol(block_ref, got_ref, send_sems, recv_sems, first_sem):
    x, y, c, other_chips = _place()
    copies = []
    for k, chip in enumerate(other_chips):
        cp = pltpu.make_async_remote_copy(src_ref=block_ref, dst_ref=got_ref.at[k], send_sem=send_sems.at[first_sem + k],
                                          recv_sem=recv_sems.at[first_sem + k], device_id=(*chip, c), device_id_type=MESH)
        cp.start()
        copies.append(cp)
    return copies


def _seq_allgather_conv(collective_id, bufs, cw_block):
    shapes = [b.shape for b in bufs]
    refs = [_hbm_ref(b) for b in bufs]

    def protocol(ins, outs, send_sems, recv_sems):
        taps = _taps_protocol(ins[0], outs[0], send_sems, recv_sems, 6 * len(bufs))
        _gather_protocol(refs, shapes, send_sems, recv_sems)
        for cp in taps:
            cp.wait_recv()
        for cp in taps:
            cp.wait_send()

    (got,) = _on_sequencer("allgather_conv", collective_id, 6 * len(bufs) + 3, _gather_peers, protocol,
                           operands=(cw_block,), out_types=(_sds((3, *cw_block.shape), F32),))
    return [r[...] for r in refs], got


def _exchange_protocol(gs, outs, shapes, send_sems, recv_sems):
    x, y, c, _ = _place()
    sends = []
    for u in range(len(gs)):
        cp = pltpu.make_async_remote_copy(
            src_ref=gs[u].at[:, _half_rows(1 - c, shapes[u][1]), :], dst_ref=outs[u],
            send_sem=send_sems.at[u], recv_sem=recv_sems.at[u], device_id=(x, y, 1 - c), device_id_type=MESH)
        cp.start()
        sends.append(cp)
    for cp in sends:
        cp.wait_recv()
    for cp in sends:
        cp.wait_send()


def _seq_exchange(name, collective_id, grads):
    shapes = [g.shape for g in grads]
    return _on_sequencer(
        name, collective_id, len(grads), _sibling_peer,
        lambda gs, outs, send_sems, recv_sems: _exchange_protocol(gs, outs, shapes, send_sems, recv_sems),
        operands=grads, out_types=[_sds((s[0], s[1] // 2, s[2]), F32) for s in shapes])


def _sum_halves(name, g, got, place, after):
    _, r, cdim = g.shape
    rh = r // 2
    rt = _row_tile(rh, 4 * N_CHIPS * cdim, 2 * ELEMENTWISE_BLOCK)
    nr = rh // rt

    def body(s_ref, g_ref, got_ref, after_ref, pb_ref, pf_ref):
        pb_ref[...] = (g_ref[...] + got_ref[...]).astype(BF16)
        mine = s_ref[1]
        pf_ref[...] = g_ref[mine] + got_ref[mine]

    quarters = (N_CHIPS, rt, cdim)
    grid_spec = pltpu.PrefetchScalarGridSpec(
        num_scalar_prefetch=1, grid=(nr,),
        in_specs=[pl.BlockSpec(quarters, lambda i, s: (0, s[0] * nr + i, 0)),
                  pl.BlockSpec(quarters, lambda i, s: (0, i, 0)),
                  pl.BlockSpec(memory_space=pl.ANY)],
        out_specs=[pl.BlockSpec(quarters, lambda i, s: (0, i, 0)),
                   pl.BlockSpec((rt, cdim), lambda i, s: (i, 0))])
    return pl.pallas_call(
        body, name=name, grid_spec=grid_spec,
        out_shape=[_sds((N_CHIPS, rh, cdim), BF16), _sds((rh, cdim), F32)],
        compiler_params=_params(("parallel",)))(place, g, got, after)


def _scatter_protocol(ps, outs, send_sems, recv_sems):
    x, y, c, other_chips = _place()
    sends = []
    for u in range(len(ps)):
        for k, chip in enumerate(other_chips):
            cp = pltpu.make_async_remote_copy(
                src_ref=ps[u].at[2 * chip[0] + chip[1]], dst_ref=outs[u].at[k],
                send_sem=send_sems.at[3 * u + k], recv_sem=recv_sems.at[3 * u + k],
                device_id=(*chip, c), device_id_type=MESH)
            cp.start()
            sends.append(cp)
    for cp in sends:
        cp.wait_recv()
    for cp in sends:
        cp.wait_send()


def _seq_scatter(name, collective_id, partials):
    return _on_sequencer(
        name, collective_id, 3 * len(partials), _chip_peers, _scatter_protocol,
        operands=partials, out_types=[_sds((3, p.shape[1], p.shape[2]), BF16) for p in partials])


def _sum_partials(name, own, got, place, layer, nl, prev, after):
    rh, cdim = own.shape
    rt = _row_tile(rh, 4 * cdim, ELEMENTWISE_BLOCK)
    nr = rh // rt

    def body(s_ref, own_ref, got_ref, *rest):
        o_ref = rest[-1]
        o_ref[...] = ((own_ref[...] + got_ref[0].astype(F32)) + got_ref[1].astype(F32)) + got_ref[2].astype(F32)

    in_specs = [pl.BlockSpec((rt, cdim), lambda i, s: (i, 0)), pl.BlockSpec((3, rt, cdim), lambda i, s: (0, i, 0)),
                pl.BlockSpec(memory_space=pl.ANY)]
    args = [place, own, got, after]
    aliases = {}
    if prev is not None:
        in_specs.append(pl.BlockSpec(memory_space=pl.ANY))
        args.append(prev)
        aliases = {4: 0}
    grid_spec = pltpu.PrefetchScalarGridSpec(
        num_scalar_prefetch=1, grid=(nr,), in_specs=in_specs,
        out_specs=pl.BlockSpec((None, rt, cdim), lambda i, s: (layer, s[0] * nr + i, 0)))
    return pl.pallas_call(
        body, name=name, grid_spec=grid_spec, out_shape=_sds((nl, 2 * rh, cdim), F32),
        input_output_aliases=aliases, compiler_params=_params(("parallel",)))(*args)


def _share_protocol(outs, shapes, units, send_sems, recv_sems):
    x, y, c, _ = _place()
    sends = []
    for u, (w, l) in enumerate(units):
        mine = outs[w].at[l, _half_rows(c, shapes[w][1]), :]
        cp = pltpu.make_async_remote_copy(src_ref=mine, dst_ref=mine, send_sem=send_sems.at[u],
                                          recv_sem=recv_sems.at[u], device_id=(x, y, 1 - c), device_id_type=MESH)
        cp.start()
        sends.append(cp)
    for u, (w, l) in enumerate(units):
        theirs = outs[w].at[l, _half_rows(1 - c, shapes[w][1]), :]
        pltpu.make_async_remote_copy(src_ref=theirs, dst_ref=theirs, send_sem=send_sems.at[u],
                                     recv_sem=recv_sems.at[u], device_id=(x, y, 1 - c),
                                     device_id_type=MESH).wait_recv()
    for cp in sends:
        cp.wait_send()


def _seq_share(name, collective_id, bufs):
    shapes = [b.shape for b in bufs]
    units = [(w, l) for w in range(len(bufs)) for l in range(shapes[w][0])]
    refs = [_hbm_ref(b) for b in bufs]
    _on_sequencer(name, collective_id, len(units), _sibling_peer,
                  lambda ins, outs, send_sems, recv_sems: _share_protocol(refs, shapes, units, send_sems, recv_sems))
    return [r[...] for r in refs]


def _gather_blocks(block_ref, all_ref, send_sems, recv_sems):
    x, y, c, _ = _place()
    me = 4 * x + 2 * y + c
    all_ref[me] = block_ref[...]
    sends = []
    for rel in range(1, 8):
        fx, fy, fc = (rel >> 2) & 1, (rel >> 1) & 1, rel & 1
        peer = (x ^ fx, y ^ fy, c ^ fc)
        cp = pltpu.make_async_remote_copy(src_ref=block_ref, dst_ref=all_ref.at[me], send_sem=send_sems.at[rel - 1],
                                          recv_sem=recv_sems.at[rel - 1], device_id=peer, device_id_type=MESH)
        cp.start()
        sends.append(cp)
    for cp in sends:
        cp.wait_recv()
    for cp in sends:
        cp.wait_send()


def _adam(w, g, m, v):
    m_new = ADAM_B1 * m + (1.0 - ADAM_B1) * g
    v_new = ADAM_B2 * v + (1.0 - ADAM_B2) * (g * g)
    m_hat = m_new / (1.0 - ADAM_B1 ** ADAM_STEP)
    v_hat = v_new / (1.0 - ADAM_B2 ** ADAM_STEP)
    delta = -ADAM_LR * (m_hat / (jnp.sqrt(v_hat) + ADAM_EPS) + ADAM_WD * w)
    return delta, m_new, v_new


def _small_step(dnm0, dnm1, dnf0, dnf1, dcw, dqg, dkg, dsk, loss, w_blk, m_blk, v_blk, cw_cols):
    d = w_blk.shape[1]
    vm = pl.BlockSpec(memory_space=pltpu.VMEM)

    def reduce_body(dnm0_ref, dnm1_ref, dnf0_ref, dnf1_ref, dcw_ref, dqg_ref, dkg_ref, dsk_ref, loss_ref,
                    g_ref, blk_ref, all_ref, send_sems, recv_sems):
        blk_ref[...] = jnp.zeros_like(blk_ref)
        for row, part_ref in ((ROW_NORM_MIXER, dnm0_ref), (ROW_NORM_MIXER + 1, dnm1_ref),
                              (ROW_NORM_FFN, dnf0_ref), (ROW_NORM_FFN + 1, dnf1_ref)):
            blk_ref[row:row + 1, :] = jnp.sum(part_ref[...], axis=0, keepdims=True)
        blk_ref[ROW_CONV_W:ROW_CONV_W + 3, :] = dcw_ref[...]
        misc = slice(ROW_MISC, ROW_MISC + 1)
        for tile, gain_ref in ((TILE_Q_GAIN, dqg_ref), (TILE_K_GAIN, dkg_ref)):
            pair = gain_ref[...]
            blk_ref[misc, tile * LANES:(tile + 1) * LANES] = pair + pltpu.roll(pair, HEAD_DIM, 1)
        for h in range(N_Q_HEADS):
            lane = TILE_SINKS * LANES + h
            blk_ref[misc, lane:lane + 1] = jnp.sum(dsk_ref[h:h + 1, :], axis=1, keepdims=True)
        blk_ref[misc, TILE_LOSS * LANES:(TILE_LOSS + 1) * LANES] = jnp.broadcast_to(loss_ref[...], (1, LANES))
        _gather_blocks(blk_ref, all_ref, send_sems, recv_sems)
        g = all_ref[0]
        for dev in range(1, 8):
            g = g + all_ref[dev]
        g_ref[...] = g

    g_blk = pl.pallas_call(
        reduce_body, name="small_allreduce", in_specs=[vm] * 9, out_specs=vm, out_shape=_sds((SMALL_ROWS, d), F32),
        scratch_shapes=[pltpu.VMEM((SMALL_ROWS, d), F32), pltpu.VMEM((8, SMALL_ROWS, d), F32),
                        pltpu.SemaphoreType.DMA((7,)), pltpu.SemaphoreType.DMA((7,))],
    )(dnm0, dnm1, dnf0, dnf1, dcw, dqg, dkg, dsk, loss)

    def body(g_ref, w_ref, m_ref, v_ref, *out_refs):
        g = g_ref[...]
        misc = slice(ROW_MISC, ROW_MISC + 1)
        out_refs[0][...] = g[misc, TILE_LOSS * LANES:TILE_LOSS * LANES + 1]
        chip = 2 * lax.axis_index("x") + lax.axis_index("y")
        for i, blk in enumerate((g, *_adam(w_ref[...], g, m_ref[...], v_ref[...]))):
            nm_ref, nf_ref, cw_ref, qg_ref, kg_ref, sk_ref = out_refs[1 + 6 * i:7 + 6 * i]
            nm_ref[...] = blk[ROW_NORM_MIXER:ROW_NORM_MIXER + 2]
            nf_ref[...] = blk[ROW_NORM_FFN:ROW_NORM_FFN + 2]
            qg_ref[...] = blk[misc, TILE_Q_GAIN * LANES:TILE_Q_GAIN * LANES + HEAD_DIM]
            kg_ref[...] = blk[misc, TILE_K_GAIN * LANES:TILE_K_GAIN * LANES + HEAD_DIM]
            sk_ref[...] = blk[misc, TILE_SINKS * LANES:TILE_SINKS * LANES + N_Q_HEADS]
            for q in range(N_CHIPS):
                @pl.when(chip == q)
                def _(blk=blk, cw_ref=cw_ref, q=q):
                    cw_ref[0] = blk[ROW_CONV_W:ROW_CONV_W + 3, q * cw_cols:(q + 1) * cw_cols]

    group = [_sds((2, d), F32), _sds((2, d), F32), _sds((1, 3, cw_cols), F32), _sds((1, HEAD_DIM), F32),
             _sds((1, HEAD_DIM), F32), _sds((1, N_Q_HEADS), F32)]
    outs = pl.pallas_call(
        body, name="small_adam", in_specs=[vm] * 4, out_specs=[vm] * 25, out_shape=[_sds((1, 1), F32)] + group * 4,
    )(g_blk, w_blk, m_blk, v_blk)
    names = ("norm_mixer", "norm_ffn", "conv_w", "attn_q_gain", "attn_k_gain", "attn_sinks")
    return outs[0], [dict(zip(names, outs[1 + 6 * i:7 + 6 * i])) for i in range(4)]


def _adam_step(name, w, g, m, v):
    nl, r, cdim = w.shape
    rt = _row_tile(r, 4 * cdim, ELEMENTWISE_BLOCK)

    def body(w_ref, g_ref, m_ref, v_ref, go_ref, d_ref, mo_ref, vo_ref):
        gv = g_ref[...]
        go_ref[...] = gv
        delta, m_new, v_new = _adam(w_ref[...], gv, m_ref[...], v_ref[...])
        d_ref[...] = delta
        mo_ref[...] = m_new
        vo_ref[...] = v_new

    spec = pl.BlockSpec((None, rt, cdim), lambda l, i: (l, i, 0))
    return pl.pallas_call(
        body, name=name, grid=(nl, r // rt), in_specs=[spec] * 4, out_specs=[spec] * 4,
        out_shape=[_sds(w.shape, F32)] * 4,
        compiler_params=_params(("parallel", "parallel")))(w, g, m, v)


def _pad_rows(a, rows=SUBLANES):
    return jnp.pad(a, ((0, rows - a.shape[0]), (0, 0)))


def _small_block(nm, nf, cw_local, qg, kg, sk, chip):
    d = nm.shape[1]
    cw_rows = lax.dynamic_update_slice(jnp.zeros((SUBLANES, d), F32), cw_local, (0, chip * cw_local.shape[1]))
    misc = jnp.concatenate([qg, qg, kg, kg, jnp.pad(sk, ((0, 0), (0, LANES - sk.shape[1]))),
                            jnp.zeros((1, d - 3 * LANES), F32)], axis=1)
    return jnp.concatenate([_pad_rows(nm), _pad_rows(nf), cw_rows, _pad_rows(misc)], axis=0)


WEIGHT_NAMES = ("conv_w_in", "conv_w", "conv_w_out", "attn_w_qkv", "attn_q_gain", "attn_k_gain", "attn_sinks",
                "attn_w_o", "norm_mixer", "norm_ffn", "ffn_w_gate_up", "ffn_w_down")
BIG = ("conv_w_in", "conv_w_out", "attn_w_qkv", "attn_w_o", "ffn_w_gate_up", "ffn_w_down")


def kernel(x, conv_w_in, conv_w, conv_w_out, attn_w_qkv, attn_q_gain, attn_k_gain, attn_sinks, attn_w_o, norm_mixer, norm_ffn, ffn_w_gate_up, ffn_w_down, loss_target, m_conv_w_in, m_conv_w, m_conv_w_out, m_attn_w_qkv, m_attn_q_gain, m_attn_k_gain, m_attn_sinks, m_attn_w_o, m_norm_mixer, m_norm_ffn, m_ffn_w_gate_up, m_ffn_w_down, v_conv_w_in, v_conv_w, v_conv_w_out, v_attn_w_qkv, v_attn_q_gain, v_attn_k_gain, v_attn_sinks, v_attn_w_o, v_norm_mixer, v_norm_ffn, v_ffn_w_gate_up, v_ffn_w_down):
    w = dict(conv_w_in=conv_w_in, conv_w=conv_w, conv_w_out=conv_w_out, attn_w_qkv=attn_w_qkv,
             attn_q_gain=attn_q_gain, attn_k_gain=attn_k_gain, attn_sinks=attn_sinks, attn_w_o=attn_w_o,
             norm_mixer=norm_mixer, norm_ffn=norm_ffn, ffn_w_gate_up=ffn_w_gate_up, ffn_w_down=ffn_w_down)
    m = dict(conv_w_in=m_conv_w_in, conv_w=m_conv_w, conv_w_out=m_conv_w_out, attn_w_qkv=m_attn_w_qkv,
             attn_q_gain=m_attn_q_gain, attn_k_gain=m_attn_k_gain, attn_sinks=m_attn_sinks, attn_w_o=m_attn_w_o,
             norm_mixer=m_norm_mixer, norm_ffn=m_norm_ffn, ffn_w_gate_up=m_ffn_w_gate_up, ffn_w_down=m_ffn_w_down)
    v = dict(conv_w_in=v_conv_w_in, conv_w=v_conv_w, conv_w_out=v_conv_w_out, attn_w_qkv=v_attn_w_qkv,
             attn_q_gain=v_attn_q_gain, attn_k_gain=v_attn_k_gain, attn_sinks=v_attn_sinks, attn_w_o=v_attn_w_o,
             norm_mixer=v_norm_mixer, norm_ffn=v_norm_ffn, ffn_w_gate_up=v_ffn_w_gate_up, ffn_w_down=v_ffn_w_down)

    nseq, seq, d = x.shape
    t = nseq * seq
    chip = 2 * lax.axis_index("x") + lax.axis_index("y")
    core = lax.axis_index("c")
    place = jnp.stack([core, chip]).astype(jnp.int32)
    x0 = x.reshape(t, d)
    tgt = loss_target.reshape(t, d)

    cw_block = lax.dynamic_update_slice(jnp.zeros((SUBLANES, d), F32), conv_w[0], (0, chip * conv_w.shape[2]))
    def cast(k, layer=None):
        return _cast_own(f"cast_{k}" + ("" if layer is None else str(layer)), w[k], place, layer)

    (w_in,), cw_got = _seq_allgather_conv(1, [cast("conv_w_in")], cw_block)
    w_out, w_gu0, w_dn0 = _seq_allgather(
        "allgather_ffn0", 2, [cast("conv_w_out"), cast("ffn_w_gate_up", 0), cast("ffn_w_down", 0)])
    w_qkv, w_o, w_gu1, w_dn1 = _seq_allgather(
        "allgather_rest", 3, [cast("attn_w_qkv"), cast("attn_w_o"), cast("ffn_w_gate_up", 1), cast("ffn_w_down", 1)])
    w_out = w_out.reshape(1, d, d)
    w_o = w_o.reshape(1, d, d)
    w_gu = [w_gu0, w_gu1]
    w_dn = [w_dn0.reshape(1, D_FF, d), w_dn1.reshape(1, D_FF, d)]

    qg_pair = jnp.concatenate([attn_q_gain, attn_q_gain], axis=1)
    kg_pair = jnp.concatenate([attn_k_gain, attn_k_gain], axis=1)

    def ffn_bwd(i, dxo, xin, h, g, u, a):
        g_dn = _wgrad_down(f"ffn{i}_down_wgrad", a, dxo, D_FF // 2)
        dg, du = _mm_down_t_swiglu(f"ffn{i}_down_dgrad", dxo, w_dn[i], 0, g, u)
        g_gu = _wgrad_up2(f"ffn{i}_up_wgrad", h, dg, du)
        dxi, dgain = _dgrad_norm_ffn(f"ffn{i}_up_dgrad", dg, du, w_gu[i], 0, xin, norm_ffn[i:i + 1], dxo)
        return dxi, dgain, g_gu, g_dn

    h0, bcx = _mm_norm_up_joined("conv_in", x0, norm_mixer[0:1], w_in, 512)
    z = _conv_fwd(bcx, cw_block, cw_got, nseq, seq)
    x1, h1 = _mm_down_norm("conv_out", z, w_out, 0, x0, norm_ffn[0:1])
    g0, u0, a0 = _mm_up_swiglu("ffn0_up", h1, w_gu[0], 0)
    x2, h2 = _mm_down_norm("ffn0_down", a0, w_dn[0], 0, x1, norm_mixer[1:2])
    qkv = _mm_up_joined("attn_qkv", h2, w_qkv, 1024)
    o = _attn_fwd(qkv, qg_pair, kg_pair, attn_sinks, nseq, seq)
    x3, h3 = _mm_down_norm("attn_out", o, w_o, 0, x2, norm_ffn[1:2])
    g1, u1, a1 = _mm_up_swiglu("ffn1_up", h3, w_gu[1], 0)
    dy, loss_part = _mm_down_loss("ffn1_down", a1, w_dn[1], 0, x3, tgt)

    finished = {k: None for k in BIG}

    def exchange(tag, cid, units):
        return units, _seq_exchange(f"exchange_{tag}", cid, [g for _, _, g in units])

    def scatter(tag, cid, group, after):
        units, got = group
        sums = [_sum_halves(f"sum_halves_{k}{l}", g, r, place, after) for (k, l, g), r in zip(units, got)]
        return units, sums, _seq_scatter(f"scatter_{tag}", cid, [pb for pb, _ in sums])

    def finish(group, after):
        units, sums, arrived = group
        for (k, l, _), (_, pf), r in zip(units, sums, arrived):
            finished[k] = _sum_partials(f"sum_partials_{k}{l}", pf, r, place, l, w[k].shape[0], finished[k], after)

    dx3, dnf1, g_gu1, g_dn1 = ffn_bwd(1, dy, x3, h3, g1, u1, a1)
    ffn1 = exchange("ffn1", 4, [("ffn_w_down", 1, g_dn1), ("ffn_w_gate_up", 1, g_gu1)])
    g_o = _wgrad_down("attn_out_wgrad", o, dx3, d)
    do = _mm_down_t("attn_out_dgrad", dx3, w_o, 0)
    ffn1 = scatter("ffn1", 8, ffn1, do)
    dqkv, dqg, dkg, dsk = _attn_bwd(do, qkv, qg_pair, kg_pair, attn_sinks, nseq, seq)
    g_qkv = _wgrad_joined("attn_qkv_wgrad", h2, dqkv)
    attn = exchange("attn", 5, [("attn_w_o", 0, g_o), ("attn_w_qkv", 0, g_qkv)])
    dx2, dnm1 = _dgrad_norm_qkv("attn_qkv_dgrad", dqkv, w_qkv, x2, norm_mixer[1:2], dx3)
    finish(ffn1, dx2)
    attn = scatter("attn", 9, attn, dx2)
    dx1, dnf0, g_gu0, g_dn0 = ffn_bwd(0, dx2, x1, h1, g0, u0, a0)
    ffn0 = exchange("ffn0", 6, [("ffn_w_down", 0, g_dn0), ("ffn_w_gate_up", 0, g_gu0)])
    g_out = _wgrad_down("conv_out_wgrad", z, dx1, d)
    dz = _mm_down_t("conv_out_dgrad", dx1, w_out, 0)
    finish(attn, dz)
    ffn0 = scatter("ffn0", 10, ffn0, dz)
    dbcx, dcw = _conv_bwd(dz, bcx, cw_block, cw_got, nseq, seq)
    g_in = _wgrad_conv_in("conv_in_wgrad", h0, dbcx, conv_w_in.shape[2])
    conv = exchange("conv", 7, [("conv_w_out", 0, g_out), ("conv_w_in", 0, g_in)])
    dx0, dnm0 = _dgrad_norm_conv("conv_in_dgrad", dbcx, w_in, x0, norm_mixer[0:1], dx1)
    finish(ffn0, dx0)
    late = ("attn_w_qkv", "attn_w_o", "ffn_w_gate_up", "ffn_w_down")
    grads_late = _seq_share("share_late", 12, [finished[k] for k in late])
    conv = scatter("conv", 11, conv, dx0)

    grad, delta, new_m, new_v = {}, {}, {}, {}

    def adam(k, g):
        grad[k], delta[k], new_m[k], new_v[k] = _adam_step(f"adam_{k}", w[k], g, m[k], v[k])

    for k, g in zip(late, grads_late):
        adam(k, g)

    def blocks(src):
        return _small_block(src["norm_mixer"], src["norm_ffn"], src["conv_w"][0], src["attn_q_gain"],
                            src["attn_k_gain"], src["attn_sinks"], chip)

    loss, small = _small_step(dnm0, dnm1, dnf0, dnf1, dcw, dqg, dkg, dsk, loss_part,
                              blocks(w), blocks(m), blocks(v), conv_w.shape[2])
    for dst, part in zip((grad, delta, new_m, new_v), small):
        dst.update(part)

    done = sum(new_v[k][0, 0:1, 0:1] for k in late) + loss
    finish(conv, done)
    last = ("conv_w_in", "conv_w_out")
    for k, g in zip(last, _seq_share("share_last", 13, [finished[k] for k in last])):
        adam(k, g)

    return (loss.reshape(()), dx0.reshape(nseq, seq, d), *[grad[k] for k in WEIGHT_NAMES], *[delta[k] for k in WEIGHT_NAMES],
            *[new_m[k] for k in WEIGHT_NAMES], *[new_v[k] for k in WEIGHT_NAMES])
```

```python
import jax
import jax.numpy as jnp
from jax import lax
from jax.experimental import pallas as pl
from jax.experimental.pallas import tpu as pltpu
from jax.experimental.pallas import tpu_sc as plsc

F32 = jnp.float32
BF16 = jnp.bfloat16

D_FF = 2816
N_Q_HEADS = 16
N_KV_HEADS = 4
HEAD_DIM = 64
WINDOW = 128
BLOCK = 128
EPS = 1e-6
N_CHIPS = 4
LANES = 128
SUBLANES = 8
BF16_ROWS = 16
MXU_COLS = 256
VMEM_LIMIT = 48 * 1024 * 1024
ADAM_LR, ADAM_B1, ADAM_B2, ADAM_EPS, ADAM_WD, ADAM_STEP = 0.001, 0.9, 0.999, 1e-08, 0.01, 10
ALIBI_SLOPES = tuple(2.0 ** (-8.0 * (h + 1) / N_Q_HEADS) for h in range(N_Q_HEADS))
SMALL_ROWS = 32
ROW_NORM_MIXER, ROW_NORM_FFN, ROW_CONV_W, ROW_MISC = 0, 8, 16, 24
TILE_Q_GAIN, TILE_K_GAIN, TILE_SINKS, TILE_LOSS = 0, 1, 2, 3
MESH = pl.DeviceIdType.MESH

NN = ((1,), (0,))
NT = ((1,), (1,))
TN = ((0,), (0,))


def _dot(a, b, dims):
    return lax.dot_general(a, b, (dims, ((), ())), preferred_element_type=F32)


def _pick(n, cands):
    for c in cands:
        if n % c == 0:
            return c
    raise ValueError((n, cands))


def _row_tile(rows, row_bytes, cap_bytes):
    fits = [r for r in range(BF16_ROWS, rows + 1, BF16_ROWS) if rows % r == 0 and r * row_bytes <= cap_bytes]
    if not fits:
        raise ValueError((rows, row_bytes, cap_bytes))
    return fits[-1]


ELEMENTWISE_BLOCK = 3 << 19


def _resident(block_shape, index_map):
    return pl.BlockSpec(block_shape, index_map, pipeline_mode=pl.Buffered(1))


def _params(sem):
    return pltpu.CompilerParams(dimension_semantics=sem, vmem_limit_bytes=VMEM_LIMIT)


def _sds(shape, dtype):
    return jax.ShapeDtypeStruct(shape, dtype)


def _rms(xv):
    return lax.rsqrt(jnp.mean(xv * xv, axis=-1, keepdims=True) + EPS)


def _sigmoid(g):
    return 1.0 / (1.0 + jnp.exp(-g))


def _mm_up_joined(name, a, w4, tm_pref):
    t, k = a.shape
    _, _, _, nq = w4.shape
    tm = _pick(t, (tm_pref, 256, 128))

    def body(a_ref, w_ref, o_ref, wcat_ref):
        @pl.when(pl.program_id(0) == 0)
        def _():
            for q in range(N_CHIPS):
                wcat_ref[:, q * nq:(q + 1) * nq] = w_ref[q]

        o_ref[...] = _dot(a_ref[...], wcat_ref[...], NN).astype(BF16)

    return pl.pallas_call(
        body, name=name, grid=(t // tm,),
        in_specs=[pl.BlockSpec((tm, k), lambda i: (i, 0)),
                  pl.BlockSpec((None, N_CHIPS, k, nq), lambda i: (0, 0, 0, 0))],
        out_specs=pl.BlockSpec((tm, N_CHIPS * nq), lambda i: (i, 0)),
        out_shape=_sds((t, N_CHIPS * nq), BF16),
        scratch_shapes=[pltpu.VMEM((k, N_CHIPS * nq), BF16)],
        compiler_params=_params(("arbitrary",)))(a, w4)


def _mm_norm_up_joined(name, x, gain, w4, tm_pref):
    t, k = x.shape
    _, _, _, nq = w4.shape
    tm = _pick(t, (tm_pref, 256, 128))

    def body(x_ref, g_ref, w_ref, h_ref, o_ref, wcat_ref):
        @pl.when(pl.program_id(0) == 0)
        def _():
            for q in range(N_CHIPS):
                wcat_ref[:, q * nq:(q + 1) * nq] = w_ref[q]

        xv = x_ref[...]
        h = ((xv * _rms(xv)) * g_ref[...]).astype(BF16)
        h_ref[...] = h
        o_ref[...] = _dot(h, wcat_ref[...], NN).astype(BF16)

    return pl.pallas_call(
        body, name=name, grid=(t // tm,),
        in_specs=[pl.BlockSpec((tm, k), lambda i: (i, 0)), pl.BlockSpec((1, k), lambda i: (0, 0)),
                  _resident((None, N_CHIPS, k, nq), lambda i: (0, 0, 0, 0))],
        out_specs=[pl.BlockSpec((tm, k), lambda i: (i, 0)), pl.BlockSpec((tm, N_CHIPS * nq), lambda i: (i, 0))],
        out_shape=[_sds((t, k), BF16), _sds((t, N_CHIPS * nq), BF16)],
        scratch_shapes=[pltpu.VMEM((k, N_CHIPS * nq), BF16)],
        compiler_params=_params(("arbitrary",)))(x, gain, w4)


def _mm_up_swiglu(name, h, w4, layer):
    t, k = h.shape
    _, _, _, nq = w4.shape
    tm = _pick(t, (512, 256, 128))

    def body(h_ref, wg_ref, wu_ref, dag_ref, dau_ref, a_ref):
        hv = h_ref[...]
        g = _dot(hv, wg_ref[...], NN)
        u = _dot(hv, wu_ref[...], NN)
        sg = _sigmoid(g)
        silu = g * sg
        dag_ref[...] = (u * (sg * (1.0 + g * (1.0 - sg)))).astype(BF16)
        dau_ref[...] = silu.astype(BF16)
        a_ref[...] = (silu * u).astype(BF16)

    half = N_CHIPS // 2
    out = pl.BlockSpec((tm, nq), lambda j, i: (i, j))
    return pl.pallas_call(
        body, name=name, grid=(half, t // tm),
        in_specs=[pl.BlockSpec((tm, k), lambda j, i: (i, 0)),
                  pl.BlockSpec((None, None, k, nq), lambda j, i: (layer, j, 0, 0)),
                  pl.BlockSpec((None, None, k, nq), lambda j, i: (layer, half + j, 0, 0))],
        out_specs=[out, out, out],
        out_shape=[_sds((t, half * nq), BF16)] * 3,
        compiler_params=_params(("parallel", "parallel")))(h, w4, w4)


def _mm_down_norm(name, a, w, layer, res, gain):
    t, kf = a.shape
    _, _, n = w.shape
    tm = _pick(t, (1024, 512, 256, 128))

    def body(a_ref, w_ref, r_ref, g_ref, o_ref, h_ref):
        xo = r_ref[...] + _dot(a_ref[...], w_ref[...], NN)
        o_ref[...] = xo
        h_ref[...] = ((xo * _rms(xo)) * g_ref[...]).astype(BF16)

    row = pl.BlockSpec((tm, n), lambda i: (i, 0))
    return pl.pallas_call(
        body, name=name, grid=(t // tm,),
        in_specs=[pl.BlockSpec((tm, kf), lambda i: (i, 0)),
                  _resident((None, kf, n), lambda i: (layer, 0, 0)),
                  row, pl.BlockSpec((1, n), lambda i: (0, 0))],
        out_specs=[row, row],
        out_shape=[_sds((t, n), F32), _sds((t, n), BF16)],
        compiler_params=_params(("parallel",)))(a, w, res, gain)


def _mm_down_loss(name, a, w, layer, res, tgt):
    t, kf = a.shape
    _, _, n = w.shape
    tm = _pick(t, (1024, 512, 256, 128))
    steps = t // tm

    def body(a_ref, w_ref, r_ref, t_ref, dy_ref, l_ref, acc_ref):
        i = pl.program_id(0)

        @pl.when(i == 0)
        def _():
            acc_ref[...] = jnp.zeros_like(acc_ref)

        e = (r_ref[...] + _dot(a_ref[...], w_ref[...], NN)) - t_ref[...]
        dy_ref[...] = e * (1.0 / n)
        acc_ref[...] += (e * e).reshape(tm // SUBLANES, SUBLANES, n).sum(axis=0)

        @pl.when(i == steps - 1)
        def _():
            l_ref[...] = jnp.sum(acc_ref[...], keepdims=True) * (0.5 / n)

    row = pl.BlockSpec((tm, n), lambda i: (i, 0))
    return pl.pallas_call(
        body, name=name, grid=(steps,),
        in_specs=[pl.BlockSpec((tm, kf), lambda i: (i, 0)),
                  _resident((None, kf, n), lambda i: (layer, 0, 0)), row, row],
        out_specs=[row, pl.BlockSpec((1, 1), lambda i: (0, 0))],
        out_shape=[_sds((t, n), F32), _sds((1, 1), F32)],
        scratch_shapes=[pltpu.VMEM((SUBLANES, n), F32)],
        compiler_params=_params(("arbitrary",)))(a, w, res, tgt)


def _mm_down_t(name, dx, w, layer):
    t, n = dx.shape
    _, kf, _ = w.shape
    tm = _pick(t, (1024, 512, 256, 128))

    def body(a_ref, w_ref, o_ref):
        o_ref[...] = _dot(a_ref[...].astype(BF16), w_ref[...], NT).astype(BF16)

    return pl.pallas_call(
        body, name=name, grid=(t // tm,),
        in_specs=[pl.BlockSpec((tm, n), lambda i: (i, 0)),
                  _resident((None, kf, n), lambda i: (layer, 0, 0))],
        out_specs=pl.BlockSpec((tm, kf), lambda i: (i, 0)),
        out_shape=_sds((t, kf), BF16),
        compiler_params=_params(("parallel",)))(dx, w)


def _mm_down_t_swiglu(name, dx, w, layer, g, u):
    t, n = dx.shape
    f = g.shape[1]
    tm = _pick(t, (512, 256, 128))

    def body(a_ref, w_ref, dag_ref, dau_ref, dg_ref, du_ref):
        da = _dot(a_ref[...].astype(BF16), w_ref[...], NT)
        dg_ref[...] = (da * dag_ref[...].astype(F32)).astype(BF16)
        du_ref[...] = (da * dau_ref[...].astype(F32)).astype(BF16)

    tile = pl.BlockSpec((tm, f), lambda i: (i, 0))
    return pl.pallas_call(
        body, name=name, grid=(t // tm,),
        in_specs=[pl.BlockSpec((tm, n), lambda i: (i, 0)),
                  _resident((None, f, n), lambda i: (layer, 0, 0)), tile, tile],
        out_specs=[tile, tile],
        out_shape=[_sds((t, f), BF16)] * 2,
        compiler_params=_params(("parallel",)))(dx, w, g, u)


def _dgrad_norm(name, acts, act_blocks, pieces, w4, layer, x, gain, dres):
    t, d = x.shape
    _, _, k, nq = w4.shape
    tm = _pick(t, (512, 256, 128))
    n_act = len(acts)

    def body(*refs):
        act_refs = refs[:n_act]
        w_ref, x_ref, g_ref, dr_ref, dx_ref, dg_ref = refs[n_act:]

        @pl.when(pl.program_id(0) == 0)
        def _():
            dg_ref[...] = jnp.zeros_like(dg_ref)

        dh = None
        for a_tile, w_tile in pieces(act_refs, w_ref):
            term = _dot(a_tile, w_tile, NT)
            dh = term if dh is None else dh + term
        xv = x_ref[...]
        r = _rms(xv)
        xhat = xv * r
        gd = dh * g_ref[...]
        dx_ref[...] = dr_ref[...] + r * (gd - xhat * jnp.mean(gd * xhat, axis=-1, keepdims=True))
        dg_ref[...] += (dh * xhat).reshape(tm // SUBLANES, SUBLANES, d).sum(axis=0)

    row = pl.BlockSpec((tm, d), lambda i: (i, 0))
    return pl.pallas_call(
        body, name=name, grid=(t // tm,),
        in_specs=[*act_blocks(tm),
                  _resident((None, N_CHIPS, k, nq), lambda i: (layer, 0, 0, 0)),
                  row, pl.BlockSpec((1, d), lambda i: (0, 0)), row],
        out_specs=[row, pl.BlockSpec((SUBLANES, d), lambda i: (0, 0))],
        out_shape=[_sds((t, d), F32), _sds((SUBLANES, d), F32)],
        compiler_params=_params(("arbitrary",)))(*acts, w4, x, gain, dres)


def _dgrad_norm_ffn(name, dg, du, w4, layer, x, gain, dres):
    nq = w4.shape[3]
    f = dg.shape[1]

    def blocks(tm):
        return [pl.BlockSpec((tm, f), lambda i: (i, 0))] * 2

    def pieces(act_refs, w_ref):
        dg_ref, du_ref = act_refs
        return [(dg_ref[:, 0:nq], w_ref[0]), (dg_ref[:, nq:2 * nq], w_ref[1]),
                (du_ref[:, 0:nq], w_ref[2]), (du_ref[:, nq:2 * nq], w_ref[3])]

    return _dgrad_norm(name, [dg, du], blocks, pieces, w4, layer, x, gain, dres)


def _dgrad_norm_qkv(name, dqkv, w4, x, gain, dres):
    nq = w4.shape[3]

    def blocks(tm):
        return [pl.BlockSpec((tm, N_CHIPS * nq), lambda i: (i, 0))]

    def pieces(act_refs, w_ref):
        return [(act_refs[0][:, q * nq:(q + 1) * nq], w_ref[q]) for q in range(N_CHIPS)]

    return _dgrad_norm(name, [dqkv], blocks, pieces, w4, 0, x, gain, dres)


def _dgrad_norm_conv(name, d3, w4, x, gain, dres):
    _, _, d = d3.shape
    nq = w4.shape[3]
    per_part, per_q = d // MXU_COLS, nq // MXU_COLS

    def blocks(tm):
        return [pl.BlockSpec((3, tm, d), lambda i: (0, i, 0))]

    def pieces(act_refs, w_ref):
        out = []
        for jb in range(3 * per_part):
            ca, cw = (jb % per_part) * MXU_COLS, (jb % per_q) * MXU_COLS
            out.append((act_refs[0][jb // per_part, :, ca:ca + MXU_COLS], w_ref[jb // per_q, :, cw:cw + MXU_COLS]))
        return out

    return _dgrad_norm(name, [d3], blocks, pieces, w4, 0, x, gain, dres)


def _wgrad_up2(name, h, dg, du):
    t, k = h.shape
    nq = dg.shape[1] // 2
    tk = _pick(t, (1024, 512, 256, 128))
    steps = t // tk
    half = N_CHIPS // 2

    def body(h_ref, dg_ref, du_ref, o_ref):
        q = pl.program_id(0)

        @pl.when(pl.program_id(1) == 0)
        def _():
            o_ref[...] = jnp.zeros_like(o_ref)

        @pl.when(q < half)
        def _():
            o_ref[...] += _dot(h_ref[...], dg_ref[...], TN)

        @pl.when(q >= half)
        def _():
            o_ref[...] += _dot(h_ref[...], du_ref[...], TN)

    return pl.pallas_call(
        body, name=name, grid=(N_CHIPS, steps),
        in_specs=[pl.BlockSpec((tk, k), lambda q, s: (s, 0)),
                  pl.BlockSpec((tk, nq), lambda q, s: (jnp.where(q < half, s, steps - 1), jnp.minimum(q, half - 1))),
                  pl.BlockSpec((tk, nq), lambda q, s: (jnp.where(q >= half, s, 0), jnp.maximum(q - half, 0)))],
        out_specs=pl.BlockSpec((None, k, nq), lambda q, s: (q, 0, 0)),
        out_shape=_sds((N_CHIPS, k, nq), F32),
        compiler_params=_params(("parallel", "arbitrary")))(h, dg, du)


def _wgrad_joined(name, h, dy):
    t, k = h.shape
    nq = dy.shape[1] // N_CHIPS
    tk = _pick(t, (1024, 512, 256, 128))

    def body(h_ref, dy_ref, o_ref):
        @pl.when(pl.program_id(0) == 0)
        def _():
            o_ref[...] = jnp.zeros_like(o_ref)

        res = _dot(h_ref[...], dy_ref[...], TN)
        for q in range(N_CHIPS):
            o_ref[q] += res[:, q * nq:(q + 1) * nq]

    return pl.pallas_call(
        body, name=name, grid=(t // tk,),
        in_specs=[pl.BlockSpec((tk, k), lambda s: (s, 0)), pl.BlockSpec((tk, N_CHIPS * nq), lambda s: (s, 0))],
        out_specs=pl.BlockSpec((N_CHIPS, k, nq), lambda s: (0, 0, 0)),
        out_shape=_sds((N_CHIPS, k, nq), F32),
        compiler_params=_params(("arbitrary",)))(h, dy)


def _wgrad_conv_in(name, h, d3, nq):
    t, k = h.shape
    d = d3.shape[2]
    per_part, per_q = d // MXU_COLS, nq // MXU_COLS
    tk = _pick(t, (512, 256, 128))

    def body(h_ref, d_ref, o_ref):
        @pl.when(pl.program_id(0) == 0)
        def _():
            o_ref[...] = jnp.zeros_like(o_ref)

        hv = h_ref[...]
        for part in range(3):
            res = _dot(hv, d_ref[part], TN)
            for cc in range(per_part):
                jb = part * per_part + cc
                co = (jb % per_q) * MXU_COLS
                o_ref[jb // per_q, :, co:co + MXU_COLS] += res[:, cc * MXU_COLS:(cc + 1) * MXU_COLS]

    return pl.pallas_call(
        body, name=name, grid=(t // tk,),
        in_specs=[pl.BlockSpec((tk, k), lambda s: (s, 0)), pl.BlockSpec((3, tk, d), lambda s: (0, s, 0))],
        out_specs=pl.BlockSpec((N_CHIPS, k, nq), lambda s: (0, 0, 0)),
        out_shape=_sds((N_CHIPS, k, nq), F32),
        compiler_params=_params(("arbitrary",)))(h, d3)


def _wgrad_down(name, a, dx, tmw):
    t, kf = a.shape
    n = dx.shape[1]
    tk = _pick(t, (1024, 512, 256, 128))

    def body(a_ref, b_ref, o_ref):
        @pl.when(pl.program_id(1) == 0)
        def _():
            o_ref[...] = jnp.zeros_like(o_ref)

        o_ref[...] += _dot(a_ref[...], b_ref[...].astype(BF16), TN)

    g = pl.pallas_call(
        body, name=name, grid=(kf // tmw, t // tk),
        in_specs=[pl.BlockSpec((tk, tmw), lambda j, s: (s, j)), pl.BlockSpec((tk, n), lambda j, s: (s, 0))],
        out_specs=pl.BlockSpec((tmw, n), lambda j, s: (j, 0)),
        out_shape=_sds((kf, n), F32),
        compiler_params=_params(("parallel", "arbitrary")))(a, dx)
    return g.reshape(N_CHIPS, kf // N_CHIPS, n)


def _shift_rows(u, k, rows):
    s = u.shape[0]
    if k > 0:
        r = pltpu.roll(u, k, 0)
        return jnp.concatenate([jnp.where(rows >= k, r[0:SUBLANES], 0.0), r[SUBLANES:]], axis=0)
    r = pltpu.roll(u, s + k, 0)
    return jnp.concatenate([r[:s - SUBLANES], jnp.where(rows < SUBLANES + k, r[s - SUBLANES:], 0.0)], axis=0)


def _conv_taps(cw_ref, got_ref):
    return (cw_ref[...] + got_ref[0]) + (got_ref[1] + got_ref[2])


def _conv_fwd(bcx, cw, cw_got, nseq, seq):
    t, d3 = bcx.shape
    d = d3 // 3
    cb = 2 * MXU_COLS
    nj = d // cb

    def body(b_ref, c_ref, x_ref, cw_ref, got_ref, z_ref):
        u = b_ref[...].astype(F32) * x_ref[...].astype(F32)
        rows = lax.broadcasted_iota(jnp.int32, (SUBLANES, cb), 0)
        cwv = _conv_taps(cw_ref, got_ref)
        y = cwv[2:3] * u + cwv[1:2] * _shift_rows(u, 1, rows) + cwv[0:1] * _shift_rows(u, 2, rows)
        z_ref[...] = (c_ref[...].astype(F32) * y).astype(BF16)

    return pl.pallas_call(
        body, name="conv_fwd", grid=(nseq, nj),
        in_specs=[pl.BlockSpec((seq, cb), lambda b, j: (b, j)),
                  pl.BlockSpec((seq, cb), lambda b, j: (b, nj + j)),
                  pl.BlockSpec((seq, cb), lambda b, j: (b, 2 * nj + j)),
                  pl.BlockSpec((SUBLANES, cb), lambda b, j: (0, j)),
                  pl.BlockSpec((3, SUBLANES, cb), lambda b, j: (0, 0, j))],
        out_specs=pl.BlockSpec((seq, cb), lambda b, j: (b, j)),
        out_shape=_sds((t, d), BF16),
        compiler_params=_params(("parallel", "parallel")))(bcx, bcx, bcx, cw, cw_got)


def _conv_bwd(dz, bcx, cw, cw_got, nseq, seq):
    t, d3 = bcx.shape
    d = d3 // 3
    cb = MXU_COLS
    nj = d // cb

    def body(dz_ref, b_ref, c_ref, x_ref, cw_ref, got_ref, o_ref, dcw_ref):
        @pl.when(pl.program_id(1) == 0)
        def _():
            dcw_ref[...] = jnp.zeros_like(dcw_ref)

        b = b_ref[...].astype(F32)
        c = c_ref[...].astype(F32)
        xv = x_ref[...].astype(F32)
        dzv = dz_ref[...].astype(F32)
        u = b * xv
        rows = lax.broadcasted_iota(jnp.int32, (SUBLANES, cb), 0)
        u1 = _shift_rows(u, 1, rows)
        u2 = _shift_rows(u, 2, rows)
        cwv = _conv_taps(cw_ref, got_ref)
        y = cwv[2:3] * u + cwv[1:2] * u1 + cwv[0:1] * u2
        dyc = dzv * c
        du = cwv[2:3] * dyc + cwv[1:2] * _shift_rows(dyc, -1, rows) + cwv[0:1] * _shift_rows(dyc, -2, rows)
        o_ref[0] = (du * xv).astype(BF16)
        o_ref[1] = (dzv * y).astype(BF16)
        o_ref[2] = (du * b).astype(BF16)
        s0 = jnp.sum(dyc * u2, axis=0, keepdims=True)
        s1 = jnp.sum(dyc * u1, axis=0, keepdims=True)
        s2 = jnp.sum(dyc * u, axis=0, keepdims=True)
        tap = lax.broadcasted_iota(jnp.int32, (3, cb), 0)
        dcw_ref[...] += jnp.where(tap == 0, s0, jnp.where(tap == 1, s1, s2))

    return pl.pallas_call(
        body, name="conv_bwd", grid=(nj, nseq),
        in_specs=[pl.BlockSpec((seq, cb), lambda j, b: (b, j)),
                  pl.BlockSpec((seq, cb), lambda j, b: (b, j)),
                  pl.BlockSpec((seq, cb), lambda j, b: (b, nj + j)),
                  pl.BlockSpec((seq, cb), lambda j, b: (b, 2 * nj + j)),
                  pl.BlockSpec((SUBLANES, cb), lambda j, b: (0, j)),
                  pl.BlockSpec((3, SUBLANES, cb), lambda j, b: (0, 0, j))],
        out_specs=[pl.BlockSpec((3, seq, cb), lambda j, b: (0, b, j)),
                   pl.BlockSpec((3, cb), lambda j, b: (0, j))],
        out_shape=[_sds((3, t, d), BF16), _sds((3, d), F32)],
        compiler_params=_params(("parallel", "arbitrary")))(dz, bcx, bcx, bcx, cw, cw_got)


def _pair_norm(x, gain_pair, low):
    sq = x * x
    ss_lo = jnp.sum(jnp.where(low, sq, 0.0), axis=-1, keepdims=True)
    ss_hi = jnp.sum(jnp.where(low, 0.0, sq), axis=-1, keepdims=True)
    r = lax.rsqrt(jnp.where(low, ss_lo, ss_hi) * (1.0 / HEAD_DIM) + EPS)
    xhat = x * r
    return xhat * gain_pair, xhat, r


KEYS = 2 * BLOCK
QK_SCALE = 1.0 / (HEAD_DIM ** 0.5)
N_PAIRS = N_Q_HEADS // 2


def _earlier_block(shape=(BLOCK, BLOCK)):
    return lax.broadcasted_iota(jnp.int32, shape, 0) > lax.broadcasted_iota(jnp.int32, shape, 1)


def _fill_bias(bias_ref):
    rows = lax.broadcasted_iota(jnp.int32, (2 * BLOCK, BLOCK), 0)
    qi = lax.broadcasted_iota(jnp.int32, (2 * BLOCK, BLOCK), 1)
    odd_head = rows >= BLOCK
    kj = jnp.where(odd_head, rows - BLOCK, rows)
    earlier = kj > qi
    dist = (jnp.where(earlier, BLOCK, 0) + qi - kj).astype(F32)
    for j in range(N_PAIRS):
        slope = jnp.where(odd_head, ALIBI_SLOPES[2 * j + 1], ALIBI_SLOPES[2 * j])
        bias = -slope * dist
        bias_ref[1, j] = bias
        bias_ref[0, j] = jnp.where(earlier, -1e30, bias)


def _merge_blocks(x_t, earlier):
    return jnp.concatenate([jnp.where(earlier, x_t[e * KEYS:e * KEYS + BLOCK], x_t[e * KEYS + BLOCK:(e + 1) * KEYS])
                            for e in range(2)], axis=0)


def _split_blocks(heads, earlier):
    parts = []
    for x in heads:
        parts += [jnp.where(earlier, x, 0.0), jnp.where(earlier, 0.0, x)]
    return jnp.concatenate(parts, axis=0).astype(BF16)


def _kv_pair_rows(kv_tile, parity, low):
    own = jnp.where(low if parity == 0 else jnp.logical_not(low), kv_tile, 0.0)
    other = pltpu.roll(own, HEAD_DIM, 1)
    lo, hi = (own, other) if parity == 0 else (other, own)
    return jnp.concatenate([lo, hi], axis=0).astype(BF16)


def _pair_softmax(s_t, sink_even, sink_odd):
    out = []
    for e, sink in enumerate((sink_even, sink_odd)):
        se = s_t[e * BLOCK:(e + 1) * BLOCK]
        m = jnp.maximum(jnp.max(se, axis=0, keepdims=True), sink)
        ee = jnp.exp(se - m)
        es = jnp.exp(sink - m)
        inv = 1.0 / (jnp.sum(ee, axis=0, keepdims=True) + es)
        out.append((ee * inv, es * inv))
    return out


def _attn_rows(n):
    q0 = pl.multiple_of(n * BLOCK, BLOCK)
    k0 = pl.multiple_of(jnp.maximum(n - 1, 0) * BLOCK, BLOCK)
    return q0, k0, jnp.minimum(n, 1)


def _key_rows(qkv_ref, k0, q0, col):
    return jnp.concatenate([qkv_ref[pl.ds(k0, BLOCK), col:col + LANES], qkv_ref[pl.ds(q0, BLOCK), col:col + LANES]],
                           axis=0).astype(F32)


def _attn_fwd(qkv, qg_pair, kg_pair, sinks, nseq, seq):
    t = qkv.shape[0]
    dq = N_Q_HEADS * HEAD_DIM
    dkv = N_KV_HEADS * HEAD_DIM

    def body(sk_ref, qkv_ref, qg_ref, kg_ref, o_ref, bias_ref):
        @pl.when(pl.program_id(0) == 0)
        def _():
            _fill_bias(bias_ref)

        low = lax.broadcasted_iota(jnp.int32, (1, LANES), 1) < HEAD_DIM
        earlier = _earlier_block()
        qg = qg_ref[...] * QK_SCALE
        kg = kg_ref[...]

        def blk(n, carry):
            q0, k0, later = _attn_rows(n)
            for kt in range(dkv // LANES):
                kraw = _key_rows(qkv_ref, k0, q0, dq + kt * LANES)
                vraw = _key_rows(qkv_ref, k0, q0, dq + dkv + kt * LANES)
                kn, _, _ = _pair_norm(kraw, kg, low)
                for par in range(2):
                    kh = 2 * kt + par
                    k_pair = _kv_pair_rows(kn, par, low)
                    v_pair = _kv_pair_rows(vraw, par, low)
                    for jj in range(2):
                        j = 2 * kh + jj
                        qraw = qkv_ref[pl.ds(q0, BLOCK), j * LANES:(j + 1) * LANES].astype(F32)
                        qn, _, _ = _pair_norm(qraw, qg, low)
                        s_t = _merge_blocks(_dot(k_pair, qn.astype(BF16), NT), earlier) + bias_ref[later, j]
                        (p0, _), (p1, _) = _pair_softmax(s_t, sk_ref[0, 2 * j], sk_ref[0, 2 * j + 1])
                        p_t = _split_blocks((p0, p1), earlier)
                        o_ref[pl.ds(q0, BLOCK), j * LANES:(j + 1) * LANES] = _dot(p_t, v_pair, TN).astype(BF16)
            return carry

        lax.fori_loop(0, seq // BLOCK, blk, 0)

    return pl.pallas_call(
        body, name="attn_fwd", grid=(nseq,),
        in_specs=[pl.BlockSpec(memory_space=pltpu.SMEM),
                  pl.BlockSpec((seq, dq + 2 * dkv), lambda b: (b, 0)),
                  pl.BlockSpec((1, LANES), lambda b: (0, 0)),
                  pl.BlockSpec((1, LANES), lambda b: (0, 0))],
        out_specs=pl.BlockSpec((seq, dq), lambda b: (b, 0)),
        out_shape=_sds((t, dq), BF16),
        scratch_shapes=[pltpu.VMEM((2, N_PAIRS, 2 * BLOCK, BLOCK), F32)],
        compiler_params=_params(("arbitrary",)))(sinks, qkv, qg_pair, kg_pair)


def _attn_bwd(do, qkv, qg_pair, kg_pair, sinks, nseq, seq):
    t = qkv.shape[0]
    dq = N_Q_HEADS * HEAD_DIM
    dkv = N_KV_HEADS * HEAD_DIM

    def body(sk_ref, do_ref, qkv_ref, qg_ref, kg_ref, o_ref, dqg_ref, dkg_ref, dsk_ref, acc_ref, bias_ref):
        @pl.when(pl.program_id(0) == 0)
        def _():
            _fill_bias(bias_ref)
            dqg_ref[...] = jnp.zeros_like(dqg_ref)
            dkg_ref[...] = jnp.zeros_like(dkg_ref)
            dsk_ref[...] = jnp.zeros_like(dsk_ref)

        acc_ref[...] = jnp.zeros_like(acc_ref)
        low = lax.broadcasted_iota(jnp.int32, (1, LANES), 1) < HEAD_DIM
        earlier = _earlier_block()
        head_row = lax.broadcasted_iota(jnp.int32, (N_Q_HEADS, LANES), 0)
        qg = qg_ref[...] * QK_SCALE
        kg = kg_ref[...]

        def blk(n, carry):
            dqg_acc, dkg_acc, dsk_acc = carry
            q0, k0, later = _attn_rows(n)
            for kt in range(dkv // LANES):
                kraw = _key_rows(qkv_ref, k0, q0, dq + kt * LANES)
                vraw = _key_rows(qkv_ref, k0, q0, dq + dkv + kt * LANES)
                kn, khat, rk = _pair_norm(kraw, kg, low)
                dk_tile = None
                dv_tile = None
                for par in range(2):
                    kh = 2 * kt + par
                    own = low if par == 0 else jnp.logical_not(low)
                    k_pair = _kv_pair_rows(kn, par, low)
                    v_pair = _kv_pair_rows(vraw, par, low)
                    dkn_rows = jnp.zeros((2 * KEYS, LANES), F32)
                    dv_rows = jnp.zeros((2 * KEYS, LANES), F32)
                    for jj in range(2):
                        j = 2 * kh + jj
                        qraw = qkv_ref[pl.ds(q0, BLOCK), j * LANES:(j + 1) * LANES].astype(F32)
                        qn, qhat, rq = _pair_norm(qraw, qg, low)
                        qn_b = qn.astype(BF16)
                        do_b = do_ref[pl.ds(q0, BLOCK), j * LANES:(j + 1) * LANES]
                        s_t = _merge_blocks(_dot(k_pair, qn_b, NT), earlier) + bias_ref[later, j]
                        dp_t = _merge_blocks(_dot(v_pair, do_b, NT), earlier)
                        ds_heads = []
                        probs = _pair_softmax(s_t, sk_ref[0, 2 * j], sk_ref[0, 2 * j + 1])
                        for e, (p, ps) in enumerate(probs):
                            dp = dp_t[e * BLOCK:(e + 1) * BLOCK]
                            dsum = jnp.sum(p * dp, axis=0, keepdims=True)
                            ds_heads.append(p * (dp - dsum))
                            dsk_acc = dsk_acc - jnp.where(head_row == 2 * j + e, ps * dsum, 0.0)
                        p_t = _split_blocks((probs[0][0], probs[1][0]), earlier)
                        ds_t = _split_blocks(ds_heads, earlier)
                        dv_rows = dv_rows + _dot(p_t, do_b, NN)
                        dkn_rows = dkn_rows + _dot(ds_t, qn_b, NN)
                        dqn = _dot(ds_t, k_pair, TN)
                        dqg_acc = dqg_acc + jnp.sum(dqn * qhat, axis=0, keepdims=True)
                        dqhat = dqn * qg
                        prod = dqhat * qhat
                        m_lo = jnp.sum(jnp.where(low, prod, 0.0), axis=-1, keepdims=True)
                        m_hi = jnp.sum(jnp.where(low, 0.0, prod), axis=-1, keepdims=True)
                        mean = jnp.where(low, m_lo, m_hi) * (1.0 / HEAD_DIM)
                        o_ref[pl.ds(q0, BLOCK), j * LANES:(j + 1) * LANES] = (rq * (dqhat - qhat * mean)).astype(BF16)
                    dkn_acc = jnp.where(low, dkn_rows[0:KEYS], dkn_rows[KEYS:2 * KEYS])
                    dv_acc = jnp.where(low, dv_rows[0:KEYS], dv_rows[KEYS:2 * KEYS])
                    dkn = dkn_acc + pltpu.roll(dkn_acc, HEAD_DIM, 1)
                    dvh = dv_acc + pltpu.roll(dv_acc, HEAD_DIM, 1)
                    khat_own = jnp.where(own, khat, 0.0)
                    khat_dup = khat_own + pltpu.roll(khat_own, HEAD_DIM, 1)
                    dkg_acc = dkg_acc + jnp.sum(jnp.where(own, dkn * khat_dup, 0.0), axis=0, keepdims=True)
                    dkhat = dkn * kg
                    mean_k = jnp.sum(dkhat * khat_dup, axis=-1, keepdims=True) * (1.0 / LANES)
                    dk_raw = rk * (dkhat - khat_dup * mean_k)
                    dk_tile = jnp.where(own, dk_raw, 0.0) if dk_tile is None else jnp.where(own, dk_raw, dk_tile)
                    dv_tile = jnp.where(own, dvh, 0.0) if dv_tile is None else jnp.where(own, dvh, dv_tile)
                for r0, part in ((k0, slice(0, BLOCK)), (q0, slice(BLOCK, KEYS))):
                    acc_ref[pl.ds(r0, BLOCK), kt * LANES:(kt + 1) * LANES] += dk_tile[part]
                    acc_ref[pl.ds(r0, BLOCK), dkv + kt * LANES:dkv + (kt + 1) * LANES] += dv_tile[part]
            return dqg_acc, dkg_acc, dsk_acc

        zero = jnp.zeros((1, LANES), F32)
        carry = (zero, zero, jnp.zeros((N_Q_HEADS, LANES), F32))
        dqg_acc, dkg_acc, dsk_acc = lax.fori_loop(0, seq // BLOCK, blk, carry)
        dqg_ref[...] += dqg_acc * QK_SCALE
        dkg_ref[...] += dkg_acc
        dsk_ref[...] += dsk_acc
        o_ref[:, dq:dq + 2 * dkv] = acc_ref[...].astype(BF16)

    small = pl.BlockSpec((1, LANES), lambda b: (0, 0))
    heads = pl.BlockSpec((N_Q_HEADS, LANES), lambda b: (0, 0))
    return pl.pallas_call(
        body, name="attn_bwd", grid=(nseq,),
        in_specs=[pl.BlockSpec(memory_space=pltpu.SMEM),
                  pl.BlockSpec((seq, dq), lambda b: (b, 0)),
                  pl.BlockSpec((seq, dq + 2 * dkv), lambda b: (b, 0)),
                  small, small],
        out_specs=[pl.BlockSpec((seq, dq + 2 * dkv), lambda b: (b, 0)), small, small, heads],
        out_shape=[_sds((t, dq + 2 * dkv), BF16), _sds((1, LANES), F32), _sds((1, LANES), F32),
                   _sds((N_Q_HEADS, LANES), F32)],
        scratch_shapes=[pltpu.VMEM((seq, 2 * dkv), F32), pltpu.VMEM((2, N_PAIRS, 2 * BLOCK, BLOCK), F32)],
        compiler_params=_params(("arbitrary",)))(sinks, do, qkv, qg_pair, kg_pair)


def _place():
    x, y, c = lax.axis_index("x"), lax.axis_index("y"), lax.axis_index("c")
    other_chips = [(1 - x, y), (x, 1 - y), (1 - x, 1 - y)]
    return x, y, c, other_chips


def _half_rows(c, rows):
    rh = rows // 2
    return pl.ds(pl.multiple_of(c * rh, BF16_ROWS), rh)


def _cast_own(name, w, place, layer=None):
    nl, r, cdim = w.shape
    first = 0
    if layer is not None:
        nl, first = 1, layer
    rt = _row_tile(r, 4 * cdim, ELEMENTWISE_BLOCK)

    def body(s_ref, w_ref, o_ref):
        o_ref[...] = w_ref[...].astype(BF16)

    grid_spec = pltpu.PrefetchScalarGridSpec(
        num_scalar_prefetch=1, grid=(nl, r // rt),
        in_specs=[pl.BlockSpec((None, rt, cdim), lambda l, i, s: (first + l, i, 0))],
        out_specs=pl.BlockSpec((None, None, rt, cdim), lambda l, i, s: (l, s[1], i, 0)))
    return pl.pallas_call(
        body, name=name, grid_spec=grid_spec, out_shape=_sds((nl, N_CHIPS, r, cdim), BF16),
        compiler_params=_params(("parallel", "parallel")))(place, w)


def _gather_protocol(outs, shapes, send_sems, recv_sems):
    n = len(outs)
    x, y, c, other_chips = _place()
    me_chip = 2 * x + y
    sibling = (x, y, 1 - c)

    def rows(u, chip, half):
        return outs[u].at[:, chip, _half_rows(half, shapes[u][2]), :]

    def copy(sem, part, to):
        return pltpu.make_async_remote_copy(src_ref=part, dst_ref=part, send_sem=send_sems.at[sem],
                                            recv_sem=recv_sems.at[sem], device_id=to, device_id_type=MESH)

    sends = []
    for u in range(n):
        for k, chip in enumerate(other_chips):
            cp = copy(6 * u + k, rows(u, me_chip, c), (*chip, c))
            cp.start()
            sends.append(cp)
    for u in range(n):
        for k, chip in enumerate(other_chips):
            got = rows(u, 2 * chip[0] + chip[1], c)
            copy(6 * u + k, got, (*chip, c)).wait_recv()
            cp = copy(6 * u + 3 + k, got, sibling)
            cp.start()
            sends.append(cp)
    for u in range(n):
        for k, chip in enumerate(other_chips):
            copy(6 * u + 3 + k, rows(u, 2 * chip[0] + chip[1], 1 - c), sibling).wait_recv()
    for cp in sends:
        cp.wait_send()


def _hbm_ref(a):
    return jax.new_ref(a, memory_space=pltpu.MemorySpace.HBM)


def _sibling_peer():
    x, y, c, _ = _place()
    return [(x, y, 1 - c)]


def _chip_peers():
    x, y, c, other_chips = _place()
    return [(*chip, c) for chip in other_chips]


def _gather_peers():
    return _chip_peers() + _sibling_peer()


def _on_sequencer(name, collective_id, n_sems, peers, protocol, operands=(), out_types=()):
    n_in, n_out = len(operands), len(out_types)

    def launch(*refs):
        send_sems, recv_sems = refs[n_in + n_out:]
        barrier = pltpu.get_barrier_semaphore()
        targets = peers()
        for peer in targets:
            pl.semaphore_signal(barrier, inc=1, device_id=peer, device_id_type=MESH)
        pl.semaphore_wait(barrier, len(targets))
        protocol(refs[:n_in], refs[n_in:n_in + n_out], send_sems, recv_sems)

    return pl.kernel(
        launch, out_type=tuple(out_types), mesh=plsc.ScalarSubcoreMesh(axis_name="sequencer", num_cores=1), name=name,
        scratch_types=(pltpu.SemaphoreType.DMA((n_sems,)), pltpu.SemaphoreType.DMA((n_sems,))),
        compiler_params=pltpu.CompilerParams(collective_id=collective_id))(*operands)


def _seq_allgather(name, collective_id, bufs):
    shapes = [b.shape for b in bufs]
    refs = [_hbm_ref(b) for b in bufs]
    _on_sequencer(name, collective_id, 6 * len(bufs), _gather_peers,
                  lambda ins, outs, send_sems, recv_sems: _gather_protocol(refs, shapes, send_sems, recv_sems))
    return [r[...] for r in refs]


def _taps_protocol(block_ref, got_ref, send_sems, recv_sems, first_sem):
    x, y, c, other_chips = _place()
    copies = []
    for k, chip in enumerate(other_chips):
        cp = pltpu.make_async_remote_copy(src_ref=block_ref, dst_ref=got_ref.at[k], send_sem=send_sems.at[first_sem + k],
                                          recv_sem=recv_sems.at[first_sem + k], device_id=(*chip, c), device_id_type=MESH)
        cp.start()
        copies.append(cp)
    return copies


def _seq_allgather_conv(collective_id, bufs, cw_block):
    shapes = [b.shape for b in bufs]
    refs = [_hbm_ref(b) for b in bufs]

    def protocol(ins, outs, send_sems, recv_sems):
        taps = _taps_protocol(ins[0], outs[0], send_sems, recv_sems, 6 * len(bufs))
        _gather_protocol(refs, shapes, send_sems, recv_sems)
        for cp in taps:
            cp.wait_recv()
        for cp in taps:
            cp.wait_send()

    (got,) = _on_sequencer("allgather_conv", collective_id, 6 * len(bufs) + 3, _gather_peers, protocol,
                           operands=(cw_block,), out_types=(_sds((3, *cw_block.shape), F32),))
    return [r[...] for r in refs], got


def _exchange_protocol(gs, outs, shapes, send_sems, recv_sems):
    x, y, c, _ = _place()
    sends = []
    for u in range(len(gs)):
        cp = pltpu.make_async_remote_copy(
            src_ref=gs[u].at[:, _half_rows(1 - c, shapes[u][1]), :], dst_ref=outs[u],
            send_sem=send_sems.at[u], recv_sem=recv_sems.at[u], device_id=(x, y, 1 - c), device_id_type=MESH)
        cp.start()
        sends.append(cp)
    for cp in sends:
        cp.wait_recv()
    for cp in sends:
        cp.wait_send()


def _seq_exchange(name, collective_id, grads):
    shapes = [g.shape for g in grads]
    return _on_sequencer(
        name, collective_id, len(grads), _sibling_peer,
        lambda gs, outs, send_sems, recv_sems: _exchange_protocol(gs, outs, shapes, send_sems, recv_sems),
        operands=grads, out_types=[_sds((s[0], s[1] // 2, s[2]), F32) for s in shapes])


def _sum_halves(name, g, got, place, after):
    _, r, cdim = g.shape
    rh = r // 2
    rt = _row_tile(rh, 4 * N_CHIPS * cdim, 2 * ELEMENTWISE_BLOCK)
    nr = rh // rt

    def body(s_ref, g_ref, got_ref, after_ref, pb_ref, pf_ref):
        pb_ref[...] = (g_ref[...] + got_ref[...]).astype(BF16)
        mine = s_ref[1]
        pf_ref[...] = g_ref[mine] + got_ref[mine]

    quarters = (N_CHIPS, rt, cdim)
    grid_spec = pltpu.PrefetchScalarGridSpec(
        num_scalar_prefetch=1, grid=(nr,),
        in_specs=[pl.BlockSpec(quarters, lambda i, s: (0, s[0] * nr + i, 0)),
                  pl.BlockSpec(quarters, lambda i, s: (0, i, 0)),
                  pl.BlockSpec(memory_space=pl.ANY)],
        out_specs=[pl.BlockSpec(quarters, lambda i, s: (0, i, 0)),
                   pl.BlockSpec((rt, cdim), lambda i, s: (i, 0))])
    return pl.pallas_call(
        body, name=name, grid_spec=grid_spec,
        out_shape=[_sds((N_CHIPS, rh, cdim), BF16), _sds((rh, cdim), F32)],
        compiler_params=_params(("parallel",)))(place, g, got, after)


def _scatter_protocol(ps, outs, send_sems, recv_sems):
    x, y, c, other_chips = _place()
    sends = []
    for u in range(len(ps)):
        for k, chip in enumerate(other_chips):
            cp = pltpu.make_async_remote_copy(
                src_ref=ps[u].at[2 * chip[0] + chip[1]], dst_ref=outs[u].at[k],
                send_sem=send_sems.at[3 * u + k], recv_sem=recv_sems.at[3 * u + k],
                device_id=(*chip, c), device_id_type=MESH)
            cp.start()
            sends.append(cp)
    for cp in sends:
        cp.wait_recv()
    for cp in sends:
        cp.wait_send()


def _seq_scatter(name, collective_id, partials):
    return _on_sequencer(
        name, collective_id, 3 * len(partials), _chip_peers, _scatter_protocol,
        operands=partials, out_types=[_sds((3, p.shape[1], p.shape[2]), BF16) for p in partials])


def _sum_partials(name, own, got, place, layer, nl, prev, after):
    rh, cdim = own.shape
    rt = _row_tile(rh, 4 * cdim, ELEMENTWISE_BLOCK)
    nr = rh // rt

    def body(s_ref, own_ref, got_ref, *rest):
        o_ref = rest[-1]
        o_ref[...] = ((own_ref[...] + got_ref[0].astype(F32)) + got_ref[1].astype(F32)) + got_ref[2].astype(F32)

    in_specs = [pl.BlockSpec((rt, cdim), lambda i, s: (i, 0)), pl.BlockSpec((3, rt, cdim), lambda i, s: (0, i, 0)),
                pl.BlockSpec(memory_space=pl.ANY)]
    args = [place, own, got, after]
    aliases = {}
    if prev is not None:
        in_specs.append(pl.BlockSpec(memory_space=pl.ANY))
        args.append(prev)
        aliases = {4: 0}
    grid_spec = pltpu.PrefetchScalarGridSpec(
        num_scalar_prefetch=1, grid=(nr,), in_specs=in_specs,
        out_specs=pl.BlockSpec((None, rt, cdim), lambda i, s: (layer, s[0] * nr + i, 0)))
    return pl.pallas_call(
        body, name=name, grid_spec=grid_spec, out_shape=_sds((nl, 2 * rh, cdim), F32),
        input_output_aliases=aliases, compiler_params=_params(("parallel",)))(*args)


def _share_protocol(outs, shapes, units, send_sems, recv_sems):
    x, y, c, _ = _place()
    sends = []
    for u, (w, l) in enumerate(units):
        mine = outs[w].at[l, _half_rows(c, shapes[w][1]), :]
        cp = pltpu.make_async_remote_copy(src_ref=mine, dst_ref=mine, send_sem=send_sems.at[u],
                                          recv_sem=recv_sems.at[u], device_id=(x, y, 1 - c), device_id_type=MESH)
        cp.start()
        sends.append(cp)
    for u, (w, l) in enumerate(units):
        theirs = outs[w].at[l, _half_rows(1 - c, shapes[w][1]), :]
        pltpu.make_async_remote_copy(src_ref=theirs, dst_ref=theirs, send_sem=send_sems.at[u],
                                     recv_sem=recv_sems.at[u], device_id=(x, y, 1 - c),
                                     device_id_type=MESH).wait_recv()
    for cp in sends:
        cp.wait_send()


def _seq_share(name, collective_id, bufs):
    shapes = [b.shape for b in bufs]
    units = [(w, l) for w in range(len(bufs)) for l in range(shapes[w][0])]
    refs = [_hbm_ref(b) for b in bufs]
    _on_sequencer(name, collective_id, len(units), _sibling_peer,
                  lambda ins, outs, send_sems, recv_sems: _share_protocol(refs, shapes, units, send_sems, recv_sems))
    return [r[...] for r in refs]


def _gather_blocks(block_ref, all_ref, send_sems, recv_sems):
    x, y, c, _ = _place()
    me = 4 * x + 2 * y + c
    all_ref[me] = block_ref[...]
    sends = []
    for rel in range(1, 8):
        fx, fy, fc = (rel >> 2) & 1, (rel >> 1) & 1, rel & 1
        peer = (x ^ fx, y ^ fy, c ^ fc)
        cp = pltpu.make_async_remote_copy(src_ref=block_ref, dst_ref=all_ref.at[me], send_sem=send_sems.at[rel - 1],
                                          recv_sem=recv_sems.at[rel - 1], device_id=peer, device_id_type=MESH)
        cp.start()
        sends.append(cp)
    for cp in sends:
        cp.wait_recv()
    for cp in sends:
        cp.wait_send()


def _adam(w, g, m, v):
    m_new = ADAM_B1 * m + (1.0 - ADAM_B1) * g
    v_new = ADAM_B2 * v + (1.0 - ADAM_B2) * (g * g)
    m_hat = m_new / (1.0 - ADAM_B1 ** ADAM_STEP)
    v_hat = v_new / (1.0 - ADAM_B2 ** ADAM_STEP)
    delta = -ADAM_LR * (m_hat / (jnp.sqrt(v_hat) + ADAM_EPS) + ADAM_WD * w)
    return delta, m_new, v_new


def _small_step(dnm0, dnm1, dnf0, dnf1, dcw, dqg, dkg, dsk, loss, w_blk, m_blk, v_blk, cw_cols):
    d = w_blk.shape[1]
    vm = pl.BlockSpec(memory_space=pltpu.VMEM)

    def reduce_body(dnm0_ref, dnm1_ref, dnf0_ref, dnf1_ref, dcw_ref, dqg_ref, dkg_ref, dsk_ref, loss_ref,
                    g_ref, blk_ref, all_ref, send_sems, recv_sems):
        blk_ref[...] = jnp.zeros_like(blk_ref)
        for row, part_ref in ((ROW_NORM_MIXER, dnm0_ref), (ROW_NORM_MIXER + 1, dnm1_ref),
                              (ROW_NORM_FFN, dnf0_ref), (ROW_NORM_FFN + 1, dnf1_ref)):
            blk_ref[row:row + 1, :] = jnp.sum(part_ref[...], axis=0, keepdims=True)
        blk_ref[ROW_CONV_W:ROW_CONV_W + 3, :] = dcw_ref[...]
        misc = slice(ROW_MISC, ROW_MISC + 1)
        for tile, gain_ref in ((TILE_Q_GAIN, dqg_ref), (TILE_K_GAIN, dkg_ref)):
            pair = gain_ref[...]
            blk_ref[misc, tile * LANES:(tile + 1) * LANES] = pair + pltpu.roll(pair, HEAD_DIM, 1)
        for h in range(N_Q_HEADS):
            lane = TILE_SINKS * LANES + h
            blk_ref[misc, lane:lane + 1] = jnp.sum(dsk_ref[h:h + 1, :], axis=1, keepdims=True)
        blk_ref[misc, TILE_LOSS * LANES:(TILE_LOSS + 1) * LANES] = jnp.broadcast_to(loss_ref[...], (1, LANES))
        _gather_blocks(blk_ref, all_ref, send_sems, recv_sems)
        g = all_ref[0]
        for dev in range(1, 8):
            g = g + all_ref[dev]
        g_ref[...] = g

    g_blk = pl.pallas_call(
        reduce_body, name="small_allreduce", in_specs=[vm] * 9, out_specs=vm, out_shape=_sds((SMALL_ROWS, d), F32),
        scratch_shapes=[pltpu.VMEM((SMALL_ROWS, d), F32), pltpu.VMEM((8, SMALL_ROWS, d), F32),
                        pltpu.SemaphoreType.DMA((7,)), pltpu.SemaphoreType.DMA((7,))],
    )(dnm0, dnm1, dnf0, dnf1, dcw, dqg, dkg, dsk, loss)

    def body(g_ref, w_ref, m_ref, v_ref, *out_refs):
        g = g_ref[...]
        misc = slice(ROW_MISC, ROW_MISC + 1)
        out_refs[0][...] = g[misc, TILE_LOSS * LANES:TILE_LOSS * LANES + 1]
        chip = 2 * lax.axis_index("x") + lax.axis_index("y")
        for i, blk in enumerate((g, *_adam(w_ref[...], g, m_ref[...], v_ref[...]))):
            nm_ref, nf_ref, cw_ref, qg_ref, kg_ref, sk_ref = out_refs[1 + 6 * i:7 + 6 * i]
            nm_ref[...] = blk[ROW_NORM_MIXER:ROW_NORM_MIXER + 2]
            nf_ref[...] = blk[ROW_NORM_FFN:ROW_NORM_FFN + 2]
            qg_ref[...] = blk[misc, TILE_Q_GAIN * LANES:TILE_Q_GAIN * LANES + HEAD_DIM]
            kg_ref[...] = blk[misc, TILE_K_GAIN * LANES:TILE_K_GAIN * LANES + HEAD_DIM]
            sk_ref[...] = blk[misc, TILE_SINKS * LANES:TILE_SINKS * LANES + N_Q_HEADS]
            for q in range(N_CHIPS):
                @pl.when(chip == q)
                def _(blk=blk, cw_ref=cw_ref, q=q):
                    cw_ref[0] = blk[ROW_CONV_W:ROW_CONV_W + 3, q * cw_cols:(q + 1) * cw_cols]

    group = [_sds((2, d), F32), _sds((2, d), F32), _sds((1, 3, cw_cols), F32), _sds((1, HEAD_DIM), F32),
             _sds((1, HEAD_DIM), F32), _sds((1, N_Q_HEADS), F32)]
    outs = pl.pallas_call(
        body, name="small_adam", in_specs=[vm] * 4, out_specs=[vm] * 25, out_shape=[_sds((1, 1), F32)] + group * 4,
    )(g_blk, w_blk, m_blk, v_blk)
    names = ("norm_mixer", "norm_ffn", "conv_w", "attn_q_gain", "attn_k_gain", "attn_sinks")
    return outs[0], [dict(zip(names, outs[1 + 6 * i:7 + 6 * i])) for i in range(4)]


def _adam_step(name, w, g, m, v):
    nl, r, cdim = w.shape
    rt = _row_tile(r, 4 * cdim, ELEMENTWISE_BLOCK)

    def body(w_ref, g_ref, m_ref, v_ref, go_ref, d_ref, mo_ref, vo_ref):
        gv = g_ref[...]
        go_ref[...] = gv
        delta, m_new, v_new = _adam(w_ref[...], gv, m_ref[...], v_ref[...])
        d_ref[...] = delta
        mo_ref[...] = m_new
        vo_ref[...] = v_new

    spec = pl.BlockSpec((None, rt, cdim), lambda l, i: (l, i, 0))
    return pl.pallas_call(
        body, name=name, grid=(nl, r // rt), in_specs=[spec] * 4, out_specs=[spec] * 4,
        out_shape=[_sds(w.shape, F32)] * 4,
        compiler_params=_params(("parallel", "parallel")))(w, g, m, v)


def _pad_rows(a, rows=SUBLANES):
    return jnp.pad(a, ((0, rows - a.shape[0]), (0, 0)))


def _small_block(nm, nf, cw_local, qg, kg, sk, chip):
    d = nm.shape[1]
    cw_rows = lax.dynamic_update_slice(jnp.zeros((SUBLANES, d), F32), cw_local, (0, chip * cw_local.shape[1]))
    misc = jnp.concatenate([qg, qg, kg, kg, jnp.pad(sk, ((0, 0), (0, LANES - sk.shape[1]))),
                            jnp.zeros((1, d - 3 * LANES), F32)], axis=1)
    return jnp.concatenate([_pad_rows(nm), _pad_rows(nf), cw_rows, _pad_rows(misc)], axis=0)


WEIGHT_NAMES = ("conv_w_in", "conv_w", "conv_w_out", "attn_w_qkv", "attn_q_gain", "attn_k_gain", "attn_sinks",
                "attn_w_o", "norm_mixer", "norm_ffn", "ffn_w_gate_up", "ffn_w_down")
BIG = ("conv_w_in", "conv_w_out", "attn_w_qkv", "attn_w_o", "ffn_w_gate_up", "ffn_w_down")


def kernel(x, conv_w_in, conv_w, conv_w_out, attn_w_qkv, attn_q_gain, attn_k_gain, attn_sinks, attn_w_o, norm_mixer, norm_ffn, ffn_w_gate_up, ffn_w_down, loss_target, m_conv_w_in, m_conv_w, m_conv_w_out, m_attn_w_qkv, m_attn_q_gain, m_attn_k_gain, m_attn_sinks, m_attn_w_o, m_norm_mixer, m_norm_ffn, m_ffn_w_gate_up, m_ffn_w_down, v_conv_w_in, v_conv_w, v_conv_w_out, v_attn_w_qkv, v_attn_q_gain, v_attn_k_gain, v_attn_sinks, v_attn_w_o, v_norm_mixer, v_norm_ffn, v_ffn_w_gate_up, v_ffn_w_down):
    w = dict(conv_w_in=conv_w_in, conv_w=conv_w, conv_w_out=conv_w_out, attn_w_qkv=attn_w_qkv,
             attn_q_gain=attn_q_gain, attn_k_gain=attn_k_gain, attn_sinks=attn_sinks, attn_w_o=attn_w_o,
             norm_mixer=norm_mixer, norm_ffn=norm_ffn, ffn_w_gate_up=ffn_w_gate_up, ffn_w_down=ffn_w_down)
    m = dict(conv_w_in=m_conv_w_in, conv_w=m_conv_w, conv_w_out=m_conv_w_out, attn_w_qkv=m_attn_w_qkv,
             attn_q_gain=m_attn_q_gain, attn_k_gain=m_attn_k_gain, attn_sinks=m_attn_sinks, attn_w_o=m_attn_w_o,
             norm_mixer=m_norm_mixer, norm_ffn=m_norm_ffn, ffn_w_gate_up=m_ffn_w_gate_up, ffn_w_down=m_ffn_w_down)
    v = dict(conv_w_in=v_conv_w_in, conv_w=v_conv_w, conv_w_out=v_conv_w_out, attn_w_qkv=v_attn_w_qkv,
             attn_q_gain=v_attn_q_gain, attn_k_gain=v_attn_k_gain, attn_sinks=v_attn_sinks, attn_w_o=v_attn_w_o,
             norm_mixer=v_norm_mixer, norm_ffn=v_norm_ffn, ffn_w_gate_up=v_ffn_w_gate_up, ffn_w_down=v_ffn_w_down)

    nseq, seq, d = x.shape
    t = nseq * seq
    chip = 2 * lax.axis_index("x") + lax.axis_index("y")
    core = lax.axis_index("c")
    place = jnp.stack([core, chip]).astype(jnp.int32)
    x0 = x.reshape(t, d)
    tgt = loss_target.reshape(t, d)

    cw_block = lax.dynamic_update_slice(jnp.zeros((SUBLANES, d), F32), conv_w[0], (0, chip * conv_w.shape[2]))
    def cast(k, layer=None):
        return _cast_own(f"cast_{k}" + ("" if layer is None else str(layer)), w[k], place, layer)

    (w_in,), cw_got = _seq_allgather_conv(1, [cast("conv_w_in")], cw_block)
    w_out, w_gu0, w_dn0 = _seq_allgather(
        "allgather_ffn0", 2, [cast("conv_w_out"), cast("ffn_w_gate_up", 0), cast("ffn_w_down", 0)])
    w_qkv, w_o, w_gu1, w_dn1 = _seq_allgather(
        "allgather_rest", 3, [cast("attn_w_qkv"), cast("attn_w_o"), cast("ffn_w_gate_up", 1), cast("ffn_w_down", 1)])
    w_out = w_out.reshape(1, d, d)
    w_o = w_o.reshape(1, d, d)
    w_gu = [w_gu0, w_gu1]
    w_dn = [w_dn0.reshape(1, D_FF, d), w_dn1.reshape(1, D_FF, d)]

    qg_pair = jnp.concatenate([attn_q_gain, attn_q_gain], axis=1)
    kg_pair = jnp.concatenate([attn_k_gain, attn_k_gain], axis=1)

    def ffn_bwd(i, dxo, xin, h, g, u, a):
        g_dn = _wgrad_down(f"ffn{i}_down_wgrad", a, dxo, D_FF // 2)
        dg, du = _mm_down_t_swiglu(f"ffn{i}_down_dgrad", dxo, w_dn[i], 0, g, u)
        g_gu = _wgrad_up2(f"ffn{i}_up_wgrad", h, dg, du)
        dxi, dgain = _dgrad_norm_ffn(f"ffn{i}_up_dgrad", dg, du, w_gu[i], 0, xin, norm_ffn[i:i + 1], dxo)
        return dxi, dgain, g_gu, g_dn

    h0, bcx = _mm_norm_up_joined("conv_in", x0, norm_mixer[0:1], w_in, 512)
    z = _conv_fwd(bcx, cw_block, cw_got, nseq, seq)
    x1, h1 = _mm_down_norm("conv_out", z, w_out, 0, x0, norm_ffn[0:1])
    g0, u0, a0 = _mm_up_swiglu("ffn0_up", h1, w_gu[0], 0)
    x2, h2 = _mm_down_norm("ffn0_down", a0, w_dn[0], 0, x1, norm_mixer[1:2])
    qkv = _mm_up_joined("attn_qkv", h2, w_qkv, 1024)
    o = _attn_fwd(qkv, qg_pair, kg_pair, attn_sinks, nseq, seq)
    x3, h3 = _mm_down_norm("attn_out", o, w_o, 0, x2, norm_ffn[1:2])
    g1, u1, a1 = _mm_up_swiglu("ffn1_up", h3, w_gu[1], 0)
    dy, loss_part = _mm_down_loss("ffn1_down", a1, w_dn[1], 0, x3, tgt)

    finished = {k: None for k in BIG}

    def exchange(tag, cid, units):
        return units, _seq_exchange(f"exchange_{tag}", cid, [g for _, _, g in units])

    def scatter(tag, cid, group, after):
        units, got = group
        sums = [_sum_halves(f"sum_halves_{k}{l}", g, r, place, after) for (k, l, g), r in zip(units, got)]
        return units, sums, _seq_scatter(f"scatter_{tag}", cid, [pb for pb, _ in sums])

    def finish(group, after):
        units, sums, arrived = group
        for (k, l, _), (_, pf), r in zip(units, sums, arrived):
            finished[k] = _sum_partials(f"sum_partials_{k}{l}", pf, r, place, l, w[k].shape[0], finished[k], after)

    dx3, dnf1, g_gu1, g_dn1 = ffn_bwd(1, dy, x3, h3, g1, u1, a1)
    ffn1 = exchange("ffn1", 4, [("ffn_w_down", 1, g_dn1), ("ffn_w_gate_up", 1, g_gu1)])
    g_o = _wgrad_down("attn_out_wgrad", o, dx3, d)
    do = _mm_down_t("attn_out_dgrad", dx3, w_o, 0)
    ffn1 = scatter("ffn1", 8, ffn1, do)
    dqkv, dqg, dkg, dsk = _attn_bwd(do, qkv, qg_pair, kg_pair, attn_sinks, nseq, seq)
    g_qkv = _wgrad_joined("attn_qkv_wgrad", h2, dqkv)
    attn = exchange("attn", 5, [("attn_w_o", 0, g_o), ("attn_w_qkv", 0, g_qkv)])
    dx2, dnm1 = _dgrad_norm_qkv("attn_qkv_dgrad", dqkv, w_qkv, x2, norm_mixer[1:2], dx3)
    finish(ffn1, dx2)
    attn = scatter("attn", 9, attn, dx2)
    dx1, dnf0, g_gu0, g_dn0 = ffn_bwd(0, dx2, x1, h1, g0, u0, a0)
    ffn0 = exchange("ffn0", 6, [("ffn_w_down", 0, g_dn0), ("ffn_w_gate_up", 0, g_gu0)])
    g_out = _wgrad_down("conv_out_wgrad", z, dx1, d)
    dz = _mm_down_t("conv_out_dgrad", dx1, w_out, 0)
    finish(attn, dz)
    ffn0 = scatter("ffn0", 10, ffn0, dz)
    dbcx, dcw = _conv_bwd(dz, bcx, cw_block, cw_got, nseq, seq)
    g_in = _wgrad_conv_in("conv_in_wgrad", h0, dbcx, conv_w_in.shape[2])
    conv = exchange("conv", 7, [("conv_w_out", 0, g_out), ("conv_w_in", 0, g_in)])
    dx0, dnm0 = _dgrad_norm_conv("conv_in_dgrad", dbcx, w_in, x0, norm_mixer[0:1], dx1)
    finish(ffn0, dx0)
    late = ("attn_w_qkv", "attn_w_o", "ffn_w_gate_up", "ffn_w_down")
    grads_late = _seq_share("share_late", 12, [finished[k] for k in late])
    conv = scatter("conv", 11, conv, dx0)

    grad, delta, new_m, new_v = {}, {}, {}, {}

    def adam(k, g):
        grad[k], delta[k], new_m[k], new_v[k] = _adam_step(f"adam_{k}", w[k], g, m[k], v[k])

    for k, g in zip(late, grads_late):
        adam(k, g)

    def blocks(src):
        return _small_block(src["norm_mixer"], src["norm_ffn"], src["conv_w"][0], src["attn_q_gain"],
                            src["attn_k_gain"], src["attn_sinks"], chip)

    loss, small = _small_step(dnm0, dnm1, dnf0, dnf1, dcw, dqg, dkg, dsk, loss_part,
                              blocks(w), blocks(m), blocks(v), conv_w.shape[2])
    for dst, part in zip((grad, delta, new_m, new_v), small):
        dst.update(part)

    done = sum(new_v[k][0, 0:1, 0:1] for k in late) + loss
    finish(conv, done)
    last = ("conv_w_in", "conv_w_out")
    for k, g in zip(last, _seq_share("share_last", 13, [finished[k] for k in last])):
        adam(k, g)

    return (loss.reshape(()), dx0.reshape(nseq, seq, d), *[grad[k] for k in WEIGHT_NAMES], *[delta[k] for k in WEIGHT_NAMES],
            *[new_m[k] for k in WEIGHT_NAMES], *[new_v[k] for k in WEIGHT_NAMES])
```

```python
import jax
import jax.numpy as jnp
from jax import lax
from jax.experimental import pallas as pl
from jax.experimental.pallas import tpu as pltpu
from jax.experimental.pallas import tpu_sc as plsc

F32 = jnp.float32
BF16 = jnp.bfloat16

D_FF = 2816
N_Q_HEADS = 16
N_KV_HEADS = 4
HEAD_DIM = 64
WINDOW = 128
BLOCK = 128
EPS = 1e-6
N_CHIPS = 4
LANES = 128
SUBLANES = 8
BF16_ROWS = 16
MXU_COLS = 256
VMEM_LIMIT = 48 * 1024 * 1024
ADAM_LR, ADAM_B1, ADAM_B2, ADAM_EPS, ADAM_WD, ADAM_STEP = 0.001, 0.9, 0.999, 1e-08, 0.01, 10
ALIBI_SLOPES = tuple(2.0 ** (-8.0 * (h + 1) / N_Q_HEADS) for h in range(N_Q_HEADS))
SMALL_ROWS = 32
ROW_NORM_MIXER, ROW_NORM_FFN, ROW_CONV_W, ROW_MISC = 0, 8, 16, 24
SENT_NORM_MIXER, SENT_NORM_FFN, SENT_CONV_W, SENT_MISC = 0, 2, 4, 7
TILE_Q_GAIN, TILE_K_GAIN, TILE_SINKS, TILE_LOSS = 0, 1, 2, 3
MESH = pl.DeviceIdType.MESH

NN = ((1,), (0,))
NT = ((1,), (1,))
TN = ((0,), (0,))


def _dot(a, b, dims):
    return lax.dot_general(a, b, (dims, ((), ())), preferred_element_type=F32)


def _pick(n, cands):
    for c in cands:
        if n % c == 0:
            return c
    raise ValueError((n, cands))


def _row_tile(rows, row_bytes, cap_bytes):
    fits = [r for r in range(BF16_ROWS, rows + 1, BF16_ROWS) if rows % r == 0 and r * row_bytes <= cap_bytes]
    if not fits:
        raise ValueError((rows, row_bytes, cap_bytes))
    return fits[-1]


ELEMENTWISE_BLOCK = 3 << 19


def _resident(block_shape, index_map):
    return pl.BlockSpec(block_shape, index_map, pipeline_mode=pl.Buffered(1))


def _params(sem):
    return pltpu.CompilerParams(dimension_semantics=sem, vmem_limit_bytes=VMEM_LIMIT)


def _sds(shape, dtype):
    return jax.ShapeDtypeStruct(shape, dtype)


def _rms(xv):
    return lax.rsqrt(jnp.mean(xv * xv, axis=-1, keepdims=True) + EPS)


def _sigmoid(g):
    return 1.0 / (1.0 + jnp.exp(-g))


def _mm_up_joined(name, a, w4, tm_pref):
    t, k = a.shape
    _, _, _, nq = w4.shape
    tm = _pick(t, (tm_pref, 256, 128))

    def body(a_ref, w_ref, o_ref, wcat_ref):
        @pl.when(pl.program_id(0) == 0)
        def _():
            for q in range(N_CHIPS):
                wcat_ref[:, q * nq:(q + 1) * nq] = w_ref[q]

        o_ref[...] = _dot(a_ref[...], wcat_ref[...], NN).astype(BF16)

    return pl.pallas_call(
        body, name=name, grid=(t // tm,),
        in_specs=[pl.BlockSpec((tm, k), lambda i: (i, 0)),
                  pl.BlockSpec((None, N_CHIPS, k, nq), lambda i: (0, 0, 0, 0))],
        out_specs=pl.BlockSpec((tm, N_CHIPS * nq), lambda i: (i, 0)),
        out_shape=_sds((t, N_CHIPS * nq), BF16),
        scratch_shapes=[pltpu.VMEM((k, N_CHIPS * nq), BF16)],
        compiler_params=_params(("arbitrary",)))(a, w4)


def _mm_norm_up_joined(name, x, gain, w4, tm_pref):
    t, k = x.shape
    _, _, _, nq = w4.shape
    tm = _pick(t, (tm_pref, 256, 128))

    def body(x_ref, g_ref, w_ref, h_ref, o_ref, wcat_ref):
        @pl.when(pl.program_id(0) == 0)
        def _():
            for q in range(N_CHIPS):
                wcat_ref[:, q * nq:(q + 1) * nq] = w_ref[q]

        xv = x_ref[...]
        h = ((xv * _rms(xv)) * g_ref[...]).astype(BF16)
        h_ref[...] = h
        o_ref[...] = _dot(h, wcat_ref[...], NN).astype(BF16)

    return pl.pallas_call(
        body, name=name, grid=(t // tm,),
        in_specs=[pl.BlockSpec((tm, k), lambda i: (i, 0)), pl.BlockSpec((1, k), lambda i: (0, 0)),
                  _resident((None, N_CHIPS, k, nq), lambda i: (0, 0, 0, 0))],
        out_specs=[pl.BlockSpec((tm, k), lambda i: (i, 0)), pl.BlockSpec((tm, N_CHIPS * nq), lambda i: (i, 0))],
        out_shape=[_sds((t, k), BF16), _sds((t, N_CHIPS * nq), BF16)],
        scratch_shapes=[pltpu.VMEM((k, N_CHIPS * nq), BF16)],
        compiler_params=_params(("arbitrary",)))(x, gain, w4)


def _mm_up_swiglu(name, h, w4, layer):
    t, k = h.shape
    _, _, _, nq = w4.shape
    tm = _pick(t, (512, 256, 128))

    def body(h_ref, wg_ref, wu_ref, dag_ref, dau_ref, a_ref):
        hv = h_ref[...]
        g = _dot(hv, wg_ref[...], NN)
        u = _dot(hv, wu_ref[...], NN)
        sg = _sigmoid(g)
        silu = g * sg
        dag_ref[...] = (u * (sg * (1.0 + g * (1.0 - sg)))).astype(BF16)
        dau_ref[...] = silu.astype(BF16)
        a_ref[...] = (silu * u).astype(BF16)

    half = N_CHIPS // 2
    out = pl.BlockSpec((tm, nq), lambda j, i: (i, j))
    return pl.pallas_call(
        body, name=name, grid=(half, t // tm),
        in_specs=[pl.BlockSpec((tm, k), lambda j, i: (i, 0)),
                  pl.BlockSpec((None, None, k, nq), lambda j, i: (layer, j, 0, 0)),
                  pl.BlockSpec((None, None, k, nq), lambda j, i: (layer, half + j, 0, 0))],
        out_specs=[out, out, out],
        out_shape=[_sds((t, half * nq), BF16)] * 3,
        compiler_params=_params(("parallel", "parallel")))(h, w4, w4)


def _mm_down_norm(name, a, w, layer, res, gain):
    t, kf = a.shape
    _, _, n = w.shape
    tm = _pick(t, (1024, 512, 256, 128))

    def body(a_ref, w_ref, r_ref, g_ref, o_ref, h_ref):
        xo = r_ref[...] + _dot(a_ref[...], w_ref[...], NN)
        o_ref[...] = xo
        h_ref[...] = ((xo * _rms(xo)) * g_ref[...]).astype(BF16)

    row = pl.BlockSpec((tm, n), lambda i: (i, 0))
    return pl.pallas_call(
        body, name=name, grid=(t // tm,),
        in_specs=[pl.BlockSpec((tm, kf), lambda i: (i, 0)),
                  _resident((None, kf, n), lambda i: (layer, 0, 0)),
                  row, pl.BlockSpec((1, n), lambda i: (0, 0))],
        out_specs=[row, row],
        out_shape=[_sds((t, n), F32), _sds((t, n), BF16)],
        compiler_params=_params(("parallel",)))(a, w, res, gain)


def _mm_down_loss(name, a, w, layer, res, tgt):
    t, kf = a.shape
    _, _, n = w.shape
    tm = _pick(t, (1024, 512, 256, 128))
    steps = t // tm

    def body(a_ref, w_ref, r_ref, t_ref, dy_ref, l_ref, acc_ref):
        i = pl.program_id(0)

        @pl.when(i == 0)
        def _():
            acc_ref[...] = jnp.zeros_like(acc_ref)

        e = (r_ref[...] + _dot(a_ref[...], w_ref[...], NN)) - t_ref[...]
        dy_ref[...] = e * (1.0 / n)
        acc_ref[...] += (e * e).reshape(tm // SUBLANES, SUBLANES, n).sum(axis=0)

        @pl.when(i == steps - 1)
        def _():
            l_ref[...] = jnp.sum(acc_ref[...], keepdims=True) * (0.5 / n)

    row = pl.BlockSpec((tm, n), lambda i: (i, 0))
    return pl.pallas_call(
        body, name=name, grid=(steps,),
        in_specs=[pl.BlockSpec((tm, kf), lambda i: (i, 0)),
                  _resident((None, kf, n), lambda i: (layer, 0, 0)), row, row],
        out_specs=[row, pl.BlockSpec((1, 1), lambda i: (0, 0))],
        out_shape=[_sds((t, n), F32), _sds((1, 1), F32)],
        scratch_shapes=[pltpu.VMEM((SUBLANES, n), F32)],
        compiler_params=_params(("arbitrary",)))(a, w, res, tgt)


def _mm_down_t(name, dx, w, layer):
    t, n = dx.shape
    _, kf, _ = w.shape
    tm = _pick(t, (1024, 512, 256, 128))

    def body(a_ref, w_ref, o_ref):
        o_ref[...] = _dot(a_ref[...].astype(BF16), w_ref[...], NT).astype(BF16)

    return pl.pallas_call(
        body, name=name, grid=(t // tm,),
        in_specs=[pl.BlockSpec((tm, n), lambda i: (i, 0)),
                  _resident((None, kf, n), lambda i: (layer, 0, 0))],
        out_specs=pl.BlockSpec((tm, kf), lambda i: (i, 0)),
        out_shape=_sds((t, kf), BF16),
        compiler_params=_params(("parallel",)))(dx, w)


def _mm_down_t_swiglu(name, dx, w, layer, g, u):
    t, n = dx.shape
    f = g.shape[1]
    tm = _pick(t, (512, 256, 128))

    def body(a_ref, w_ref, dag_ref, dau_ref, dg_ref, du_ref):
        da = _dot(a_ref[...].astype(BF16), w_ref[...], NT)
        dg_ref[...] = (da * dag_ref[...].astype(F32)).astype(BF16)
        du_ref[...] = (da * dau_ref[...].astype(F32)).astype(BF16)

    tile = pl.BlockSpec((tm, f), lambda i: (i, 0))
    return pl.pallas_call(
        body, name=name, grid=(t // tm,),
        in_specs=[pl.BlockSpec((tm, n), lambda i: (i, 0)),
                  _resident((None, f, n), lambda i: (layer, 0, 0)), tile, tile],
        out_specs=[tile, tile],
        out_shape=[_sds((t, f), BF16)] * 2,
        compiler_params=_params(("parallel",)))(dx, w, g, u)


def _dgrad_norm(name, acts, act_blocks, pieces, w4, layer, x, gain, dres):
    t, d = x.shape
    _, _, k, nq = w4.shape
    tm = _pick(t, (512, 256, 128))
    n_act = len(acts)

    def body(*refs):
        act_refs = refs[:n_act]
        w_ref, x_ref, g_ref, dr_ref, dx_ref, dg_ref = refs[n_act:]

        @pl.when(pl.program_id(0) == 0)
        def _():
            dg_ref[...] = jnp.zeros_like(dg_ref)

        dh = None
        for a_tile, w_tile in pieces(act_refs, w_ref):
            term = _dot(a_tile, w_tile, NT)
            dh = term if dh is None else dh + term
        xv = x_ref[...]
        r = _rms(xv)
        xhat = xv * r
        gd = dh * g_ref[...]
        dx_ref[...] = dr_ref[...] + r * (gd - xhat * jnp.mean(gd * xhat, axis=-1, keepdims=True))
        dg_ref[...] += (dh * xhat).reshape(tm // SUBLANES, SUBLANES, d).sum(axis=0)

    row = pl.BlockSpec((tm, d), lambda i: (i, 0))
    return pl.pallas_call(
        body, name=name, grid=(t // tm,),
        in_specs=[*act_blocks(tm),
                  _resident((None, N_CHIPS, k, nq), lambda i: (layer, 0, 0, 0)),
                  row, pl.BlockSpec((1, d), lambda i: (0, 0)), row],
        out_specs=[row, pl.BlockSpec((SUBLANES, d), lambda i: (0, 0))],
        out_shape=[_sds((t, d), F32), _sds((SUBLANES, d), F32)],
        compiler_params=_params(("arbitrary",)))(*acts, w4, x, gain, dres)


def _dgrad_norm_ffn(name, dg, du, w4, layer, x, gain, dres):
    nq = w4.shape[3]
    f = dg.shape[1]

    def blocks(tm):
        return [pl.BlockSpec((tm, f), lambda i: (i, 0))] * 2

    def pieces(act_refs, w_ref):
        dg_ref, du_ref = act_refs
        return [(dg_ref[:, 0:nq], w_ref[0]), (dg_ref[:, nq:2 * nq], w_ref[1]),
                (du_ref[:, 0:nq], w_ref[2]), (du_ref[:, nq:2 * nq], w_ref[3])]

    return _dgrad_norm(name, [dg, du], blocks, pieces, w4, layer, x, gain, dres)


def _dgrad_norm_qkv(name, dqkv, w4, x, gain, dres):
    nq = w4.shape[3]

    def blocks(tm):
        return [pl.BlockSpec((tm, N_CHIPS * nq), lambda i: (i, 0))]

    def pieces(act_refs, w_ref):
        return [(act_refs[0][:, q * nq:(q + 1) * nq], w_ref[q]) for q in range(N_CHIPS)]

    return _dgrad_norm(name, [dqkv], blocks, pieces, w4, 0, x, gain, dres)


def _dgrad_norm_conv(name, d3, w4, x, gain, dres):
    _, _, d = d3.shape
    nq = w4.shape[3]
    per_part, per_q = d // MXU_COLS, nq // MXU_COLS

    def blocks(tm):
        return [pl.BlockSpec((3, tm, d), lambda i: (0, i, 0))]

    def pieces(act_refs, w_ref):
        out = []
        for jb in range(3 * per_part):
            ca, cw = (jb % per_part) * MXU_COLS, (jb % per_q) * MXU_COLS
            out.append((act_refs[0][jb // per_part, :, ca:ca + MXU_COLS], w_ref[jb // per_q, :, cw:cw + MXU_COLS]))
        return out

    return _dgrad_norm(name, [d3], blocks, pieces, w4, 0, x, gain, dres)


def _wgrad_up2(name, h, dg, du):
    t, k = h.shape
    nq = dg.shape[1] // 2
    tk = _pick(t, (1024, 512, 256, 128))
    steps = t // tk
    half = N_CHIPS // 2

    def body(h_ref, dg_ref, du_ref, o_ref):
        q = pl.program_id(0)

        @pl.when(pl.program_id(1) == 0)
        def _():
            o_ref[...] = jnp.zeros_like(o_ref)

        @pl.when(q < half)
        def _():
            o_ref[...] += _dot(h_ref[...], dg_ref[...], TN)

        @pl.when(q >= half)
        def _():
            o_ref[...] += _dot(h_ref[...], du_ref[...], TN)

    return pl.pallas_call(
        body, name=name, grid=(N_CHIPS, steps),
        in_specs=[pl.BlockSpec((tk, k), lambda q, s: (s, 0)),
                  pl.BlockSpec((tk, nq), lambda q, s: (jnp.where(q < half, s, steps - 1), jnp.minimum(q, half - 1))),
                  pl.BlockSpec((tk, nq), lambda q, s: (jnp.where(q >= half, s, 0), jnp.maximum(q - half, 0)))],
        out_specs=pl.BlockSpec((None, k, nq), lambda q, s: (q, 0, 0)),
        out_shape=_sds((N_CHIPS, k, nq), F32),
        compiler_params=_params(("parallel", "arbitrary")))(h, dg, du)


def _wgrad_joined(name, h, dy):
    t, k = h.shape
    nq = dy.shape[1] // N_CHIPS
    tk = _pick(t, (1024, 512, 256, 128))

    def body(h_ref, dy_ref, o_ref):
        @pl.when(pl.program_id(0) == 0)
        def _():
            o_ref[...] = jnp.zeros_like(o_ref)

        res = _dot(h_ref[...], dy_ref[...], TN)
        for q in range(N_CHIPS):
            o_ref[q] += res[:, q * nq:(q + 1) * nq]

    return pl.pallas_call(
        body, name=name, grid=(t // tk,),
        in_specs=[pl.BlockSpec((tk, k), lambda s: (s, 0)), pl.BlockSpec((tk, N_CHIPS * nq), lambda s: (s, 0))],
        out_specs=pl.BlockSpec((N_CHIPS, k, nq), lambda s: (0, 0, 0)),
        out_shape=_sds((N_CHIPS, k, nq), F32),
        compiler_params=_params(("arbitrary",)))(h, dy)


def _wgrad_conv_in(name, h, d3, nq):
    t, k = h.shape
    d = d3.shape[2]
    per_part, per_q = d // MXU_COLS, nq // MXU_COLS
    tk = _pick(t, (512, 256, 128))

    def body(h_ref, d_ref, o_ref):
        @pl.when(pl.program_id(0) == 0)
        def _():
            o_ref[...] = jnp.zeros_like(o_ref)

        hv = h_ref[...]
        for part in range(3):
            res = _dot(hv, d_ref[part], TN)
            for cc in range(per_part):
                jb = part * per_part + cc
                co = (jb % per_q) * MXU_COLS
                o_ref[jb // per_q, :, co:co + MXU_COLS] += res[:, cc * MXU_COLS:(cc + 1) * MXU_COLS]

    return pl.pallas_call(
        body, name=name, grid=(t // tk,),
        in_specs=[pl.BlockSpec((tk, k), lambda s: (s, 0)), pl.BlockSpec((3, tk, d), lambda s: (0, s, 0))],
        out_specs=pl.BlockSpec((N_CHIPS, k, nq), lambda s: (0, 0, 0)),
        out_shape=_sds((N_CHIPS, k, nq), F32),
        compiler_params=_params(("arbitrary",)))(h, d3)


def _wgrad_down(name, a, dx, tmw):
    t, kf = a.shape
    n = dx.shape[1]
    tk = _pick(t, (1024, 512, 256, 128))

    def body(a_ref, b_ref, o_ref):
        @pl.when(pl.program_id(1) == 0)
        def _():
            o_ref[...] = jnp.zeros_like(o_ref)

        o_ref[...] += _dot(a_ref[...], b_ref[...].astype(BF16), TN)

    g = pl.pallas_call(
        body, name=name, grid=(kf // tmw, t // tk),
        in_specs=[pl.BlockSpec((tk, tmw), lambda j, s: (s, j)), pl.BlockSpec((tk, n), lambda j, s: (s, 0))],
        out_specs=pl.BlockSpec((tmw, n), lambda j, s: (j, 0)),
        out_shape=_sds((kf, n), F32),
        compiler_params=_params(("parallel", "arbitrary")))(a, dx)
    return g.reshape(N_CHIPS, kf // N_CHIPS, n)


def _shift_rows(u, k, rows):
    s = u.shape[0]
    if k > 0:
        r = pltpu.roll(u, k, 0)
        return jnp.concatenate([jnp.where(rows >= k, r[0:SUBLANES], 0.0), r[SUBLANES:]], axis=0)
    r = pltpu.roll(u, s + k, 0)
    return jnp.concatenate([r[:s - SUBLANES], jnp.where(rows < SUBLANES + k, r[s - SUBLANES:], 0.0)], axis=0)


def _conv_taps(cw_ref, got_ref):
    return (cw_ref[...] + got_ref[0]) + (got_ref[1] + got_ref[2])


def _conv_fwd(bcx, cw, cw_got, nseq, seq):
    t, d3 = bcx.shape
    d = d3 // 3
    cb = 2 * MXU_COLS
    nj = d // cb

    def body(b_ref, c_ref, x_ref, cw_ref, got_ref, z_ref):
        u = b_ref[...].astype(F32) * x_ref[...].astype(F32)
        rows = lax.broadcasted_iota(jnp.int32, (SUBLANES, cb), 0)
        cwv = _conv_taps(cw_ref, got_ref)
        y = cwv[2:3] * u + cwv[1:2] * _shift_rows(u, 1, rows) + cwv[0:1] * _shift_rows(u, 2, rows)
        z_ref[...] = (c_ref[...].astype(F32) * y).astype(BF16)

    return pl.pallas_call(
        body, name="conv_fwd", grid=(nseq, nj),
        in_specs=[pl.BlockSpec((seq, cb), lambda b, j: (b, j)),
                  pl.BlockSpec((seq, cb), lambda b, j: (b, nj + j)),
                  pl.BlockSpec((seq, cb), lambda b, j: (b, 2 * nj + j)),
                  pl.BlockSpec((SUBLANES, cb), lambda b, j: (0, j)),
                  pl.BlockSpec((3, SUBLANES, cb), lambda b, j: (0, 0, j))],
        out_specs=pl.BlockSpec((seq, cb), lambda b, j: (b, j)),
        out_shape=_sds((t, d), BF16),
        compiler_params=_params(("parallel", "parallel")))(bcx, bcx, bcx, cw, cw_got)


def _conv_bwd(dz, bcx, cw, cw_got, nseq, seq):
    t, d3 = bcx.shape
    d = d3 // 3
    cb = MXU_COLS
    nj = d // cb

    def body(dz_ref, b_ref, c_ref, x_ref, cw_ref, got_ref, o_ref, dcw_ref):
        @pl.when(pl.program_id(1) == 0)
        def _():
            dcw_ref[...] = jnp.zeros_like(dcw_ref)

        b = b_ref[...].astype(F32)
        c = c_ref[...].astype(F32)
        xv = x_ref[...].astype(F32)
        dzv = dz_ref[...].astype(F32)
        u = b * xv
        rows = lax.broadcasted_iota(jnp.int32, (SUBLANES, cb), 0)
        u1 = _shift_rows(u, 1, rows)
        u2 = _shift_rows(u, 2, rows)
        cwv = _conv_taps(cw_ref, got_ref)
        y = cwv[2:3] * u + cwv[1:2] * u1 + cwv[0:1] * u2
        dyc = dzv * c
        du = cwv[2:3] * dyc + cwv[1:2] * _shift_rows(dyc, -1, rows) + cwv[0:1] * _shift_rows(dyc, -2, rows)
        o_ref[0] = (du * xv).astype(BF16)
        o_ref[1] = (dzv * y).astype(BF16)
        o_ref[2] = (du * b).astype(BF16)
        s0 = jnp.sum(dyc * u2, axis=0, keepdims=True)
        s1 = jnp.sum(dyc * u1, axis=0, keepdims=True)
        s2 = jnp.sum(dyc * u, axis=0, keepdims=True)
        tap = lax.broadcasted_iota(jnp.int32, (3, cb), 0)
        dcw_ref[...] += jnp.where(tap == 0, s0, jnp.where(tap == 1, s1, s2))

    return pl.pallas_call(
        body, name="conv_bwd", grid=(nj, nseq),
        in_specs=[pl.BlockSpec((seq, cb), lambda j, b: (b, j)),
                  pl.BlockSpec((seq, cb), lambda j, b: (b, j)),
                  pl.BlockSpec((seq, cb), lambda j, b: (b, nj + j)),
                  pl.BlockSpec((seq, cb), lambda j, b: (b, 2 * nj + j)),
                  pl.BlockSpec((SUBLANES, cb), lambda j, b: (0, j)),
                  pl.BlockSpec((3, SUBLANES, cb), lambda j, b: (0, 0, j))],
        out_specs=[pl.BlockSpec((3, seq, cb), lambda j, b: (0, b, j)),
                   pl.BlockSpec((3, cb), lambda j, b: (0, j))],
        out_shape=[_sds((3, t, d), BF16), _sds((3, d), F32)],
        compiler_params=_params(("parallel", "arbitrary")))(dz, bcx, bcx, bcx, cw, cw_got)


def _pair_norm(x, gain_pair, low):
    sq = x * x
    ss_lo = jnp.sum(jnp.where(low, sq, 0.0), axis=-1, keepdims=True)
    ss_hi = jnp.sum(jnp.where(low, 0.0, sq), axis=-1, keepdims=True)
    r = lax.rsqrt(jnp.where(low, ss_lo, ss_hi) * (1.0 / HEAD_DIM) + EPS)
    xhat = x * r
    return xhat * gain_pair, xhat, r


KEYS = 2 * BLOCK
QK_SCALE = 1.0 / (HEAD_DIM ** 0.5)
N_PAIRS = N_Q_HEADS // 2


def _earlier_block(shape=(BLOCK, BLOCK)):
    return lax.broadcasted_iota(jnp.int32, shape, 0) > lax.broadcasted_iota(jnp.int32, shape, 1)


def _fill_bias(bias_ref):
    rows = lax.broadcasted_iota(jnp.int32, (2 * BLOCK, BLOCK), 0)
    qi = lax.broadcasted_iota(jnp.int32, (2 * BLOCK, BLOCK), 1)
    odd_head = rows >= BLOCK
    kj = jnp.where(odd_head, rows - BLOCK, rows)
    earlier = kj > qi
    dist = (jnp.where(earlier, BLOCK, 0) + qi - kj).astype(F32)
    for j in range(N_PAIRS):
        slope = jnp.where(odd_head, ALIBI_SLOPES[2 * j + 1], ALIBI_SLOPES[2 * j])
        bias = -slope * dist
        bias_ref[1, j] = bias
        bias_ref[0, j] = jnp.where(earlier, -1e30, bias)


def _merge_blocks(x_t, earlier):
    return jnp.concatenate([jnp.where(earlier, x_t[e * KEYS:e * KEYS + BLOCK], x_t[e * KEYS + BLOCK:(e + 1) * KEYS])
                            for e in range(2)], axis=0)


def _split_blocks(heads, earlier):
    parts = []
    for x in heads:
        parts += [jnp.where(earlier, x, 0.0), jnp.where(earlier, 0.0, x)]
    return jnp.concatenate(parts, axis=0).astype(BF16)


def _kv_pair_rows(kv_tile, parity, low):
    own = jnp.where(low if parity == 0 else jnp.logical_not(low), kv_tile, 0.0)
    other = pltpu.roll(own, HEAD_DIM, 1)
    lo, hi = (own, other) if parity == 0 else (other, own)
    return jnp.concatenate([lo, hi], axis=0).astype(BF16)


def _pair_softmax(s_t, sink_even, sink_odd):
    out = []
    for e, sink in enumerate((sink_even, sink_odd)):
        se = s_t[e * BLOCK:(e + 1) * BLOCK]
        m = jnp.maximum(jnp.max(se, axis=0, keepdims=True), sink)
        ee = jnp.exp(se - m)
        es = jnp.exp(sink - m)
        inv = 1.0 / (jnp.sum(ee, axis=0, keepdims=True) + es)
        out.append((ee * inv, es * inv))
    return out


def _attn_rows(n):
    q0 = pl.multiple_of(n * BLOCK, BLOCK)
    k0 = pl.multiple_of(jnp.maximum(n - 1, 0) * BLOCK, BLOCK)
    return q0, k0, jnp.minimum(n, 1)


def _key_rows(qkv_ref, k0, q0, col):
    return jnp.concatenate([qkv_ref[pl.ds(k0, BLOCK), col:col + LANES], qkv_ref[pl.ds(q0, BLOCK), col:col + LANES]],
                           axis=0).astype(F32)


def _attn_fwd(qkv, qg_pair, kg_pair, sinks, nseq, seq):
    t = qkv.shape[0]
    dq = N_Q_HEADS * HEAD_DIM
    dkv = N_KV_HEADS * HEAD_DIM

    def body(sk_ref, qkv_ref, qg_ref, kg_ref, o_ref, bias_ref):
        @pl.when(pl.program_id(0) == 0)
        def _():
            _fill_bias(bias_ref)

        low = lax.broadcasted_iota(jnp.int32, (1, LANES), 1) < HEAD_DIM
        earlier = _earlier_block()
        qg = qg_ref[...] * QK_SCALE
        kg = kg_ref[...]

        def blk(n, carry):
            q0, k0, later = _attn_rows(n)
            for kt in range(dkv // LANES):
                kraw = _key_rows(qkv_ref, k0, q0, dq + kt * LANES)
                vraw = _key_rows(qkv_ref, k0, q0, dq + dkv + kt * LANES)
                kn, _, _ = _pair_norm(kraw, kg, low)
                for par in range(2):
                    kh = 2 * kt + par
                    k_pair = _kv_pair_rows(kn, par, low)
                    v_pair = _kv_pair_rows(vraw, par, low)
                    for jj in range(2):
                        j = 2 * kh + jj
                        qraw = qkv_ref[pl.ds(q0, BLOCK), j * LANES:(j + 1) * LANES].astype(F32)
                        qn, _, _ = _pair_norm(qraw, qg, low)
                        s_t = _merge_blocks(_dot(k_pair, qn.astype(BF16), NT), earlier) + bias_ref[later, j]
                        (p0, _), (p1, _) = _pair_softmax(s_t, sk_ref[0, 2 * j], sk_ref[0, 2 * j + 1])
                        p_t = _split_blocks((p0, p1), earlier)
                        o_ref[pl.ds(q0, BLOCK), j * LANES:(j + 1) * LANES] = _dot(p_t, v_pair, TN).astype(BF16)
            return carry

        lax.fori_loop(0, seq // BLOCK, blk, 0)

    return pl.pallas_call(
        body, name="attn_fwd", grid=(nseq,),
        in_specs=[pl.BlockSpec(memory_space=pltpu.SMEM),
                  pl.BlockSpec((seq, dq + 2 * dkv), lambda b: (b, 0)),
                  pl.BlockSpec((1, LANES), lambda b: (0, 0)),
                  pl.BlockSpec((1, LANES), lambda b: (0, 0))],
        out_specs=pl.BlockSpec((seq, dq), lambda b: (b, 0)),
        out_shape=_sds((t, dq), BF16),
        scratch_shapes=[pltpu.VMEM((2, N_PAIRS, 2 * BLOCK, BLOCK), F32)],
        compiler_params=_params(("arbitrary",)))(sinks, qkv, qg_pair, kg_pair)


def _attn_bwd(do, qkv, qg_pair, kg_pair, sinks, nseq, seq):
    t = qkv.shape[0]
    dq = N_Q_HEADS * HEAD_DIM
    dkv = N_KV_HEADS * HEAD_DIM

    def body(sk_ref, do_ref, qkv_ref, qg_ref, kg_ref, o_ref, dqg_ref, dkg_ref, dsk_ref, acc_ref, bias_ref):
        @pl.when(pl.program_id(0) == 0)
        def _():
            _fill_bias(bias_ref)
            dqg_ref[...] = jnp.zeros_like(dqg_ref)
            dkg_ref[...] = jnp.zeros_like(dkg_ref)
            dsk_ref[...] = jnp.zeros_like(dsk_ref)

        acc_ref[...] = jnp.zeros_like(acc_ref)
        low = lax.broadcasted_iota(jnp.int32, (1, LANES), 1) < HEAD_DIM
        earlier = _earlier_block()
        head_row = lax.broadcasted_iota(jnp.int32, (N_Q_HEADS, LANES), 0)
        qg = qg_ref[...] * QK_SCALE
        kg = kg_ref[...]

        def blk(n, carry):
            dqg_acc, dkg_acc, dsk_acc = carry
            q0, k0, later = _attn_rows(n)
            for kt in range(dkv // LANES):
                kraw = _key_rows(qkv_ref, k0, q0, dq + kt * LANES)
                vraw = _key_rows(qkv_ref, k0, q0, dq + dkv + kt * LANES)
                kn, khat, rk = _pair_norm(kraw, kg, low)
                dk_tile = None
                dv_tile = None
                for par in range(2):
                    kh = 2 * kt + par
                    own = low if par == 0 else jnp.logical_not(low)
                    k_pair = _kv_pair_rows(kn, par, low)
                    v_pair = _kv_pair_rows(vraw, par, low)
                    dkn_rows = jnp.zeros((2 * KEYS, LANES), F32)
                    dv_rows = jnp.zeros((2 * KEYS, LANES), F32)
                    for jj in range(2):
                        j = 2 * kh + jj
                        qraw = qkv_ref[pl.ds(q0, BLOCK), j * LANES:(j + 1) * LANES].astype(F32)
                        qn, qhat, rq = _pair_norm(qraw, qg, low)
                        qn_b = qn.astype(BF16)
                        do_b = do_ref[pl.ds(q0, BLOCK), j * LANES:(j + 1) * LANES]
                        s_t = _merge_blocks(_dot(k_pair, qn_b, NT), earlier) + bias_ref[later, j]
                        dp_t = _merge_blocks(_dot(v_pair, do_b, NT), earlier)
                        ds_heads = []
                        probs = _pair_softmax(s_t, sk_ref[0, 2 * j], sk_ref[0, 2 * j + 1])
                        for e, (p, ps) in enumerate(probs):
                            dp = dp_t[e * BLOCK:(e + 1) * BLOCK]
                            dsum = jnp.sum(p * dp, axis=0, keepdims=True)
                            ds_heads.append(p * (dp - dsum))
                            dsk_acc = dsk_acc - jnp.where(head_row == 2 * j + e, ps * dsum, 0.0)
                        p_t = _split_blocks((probs[0][0], probs[1][0]), earlier)
                        ds_t = _split_blocks(ds_heads, earlier)
                        dv_rows = dv_rows + _dot(p_t, do_b, NN)
                        dkn_rows = dkn_rows + _dot(ds_t, qn_b, NN)
                        dqn = _dot(ds_t, k_pair, TN)
                        dqg_acc = dqg_acc + jnp.sum(dqn * qhat, axis=0, keepdims=True)
                        dqhat = dqn * qg
                        prod = dqhat * qhat
                        m_lo = jnp.sum(jnp.where(low, prod, 0.0), axis=-1, keepdims=True)
                        m_hi = jnp.sum(jnp.where(low, 0.0, prod), axis=-1, keepdims=True)
                        mean = jnp.where(low, m_lo, m_hi) * (1.0 / HEAD_DIM)
                        o_ref[pl.ds(q0, BLOCK), j * LANES:(j + 1) * LANES] = (rq * (dqhat - qhat * mean)).astype(BF16)
                    dkn_acc = jnp.where(low, dkn_rows[0:KEYS], dkn_rows[KEYS:2 * KEYS])
                    dv_acc = jnp.where(low, dv_rows[0:KEYS], dv_rows[KEYS:2 * KEYS])
                    dkn = dkn_acc + pltpu.roll(dkn_acc, HEAD_DIM, 1)
                    dvh = dv_acc + pltpu.roll(dv_acc, HEAD_DIM, 1)
                    khat_own = jnp.where(own, khat, 0.0)
                    khat_dup = khat_own + pltpu.roll(khat_own, HEAD_DIM, 1)
                    dkg_acc = dkg_acc + jnp.sum(jnp.where(own, dkn * khat_dup, 0.0), axis=0, keepdims=True)
                    dkhat = dkn * kg
                    mean_k = jnp.sum(dkhat * khat_dup, axis=-1, keepdims=True) * (1.0 / LANES)
                    dk_raw = rk * (dkhat - khat_dup * mean_k)
                    dk_tile = jnp.where(own, dk_raw, 0.0) if dk_tile is None else jnp.where(own, dk_raw, dk_tile)
                    dv_tile = jnp.where(own, dvh, 0.0) if dv_tile is None else jnp.where(own, dvh, dv_tile)
                for r0, part in ((k0, slice(0, BLOCK)), (q0, slice(BLOCK, KEYS))):
                    acc_ref[pl.ds(r0, BLOCK), kt * LANES:(kt + 1) * LANES] += dk_tile[part]
                    acc_ref[pl.ds(r0, BLOCK), dkv + kt * LANES:dkv + (kt + 1) * LANES] += dv_tile[part]
            return dqg_acc, dkg_acc, dsk_acc

        zero = jnp.zeros((1, LANES), F32)
        carry = (zero, zero, jnp.zeros((N_Q_HEADS, LANES), F32))
        dqg_acc, dkg_acc, dsk_acc = lax.fori_loop(0, seq // BLOCK, blk, carry)
        dqg_ref[...] += dqg_acc * QK_SCALE
        dkg_ref[...] += dkg_acc
        dsk_ref[...] += dsk_acc
        o_ref[:, dq:dq + 2 * dkv] = acc_ref[...].astype(BF16)

    small = pl.BlockSpec((1, LANES), lambda b: (0, 0))
    heads = pl.BlockSpec((N_Q_HEADS, LANES), lambda b: (0, 0))
    return pl.pallas_call(
        body, name="attn_bwd", grid=(nseq,),
        in_specs=[pl.BlockSpec(memory_space=pltpu.SMEM),
                  pl.BlockSpec((seq, dq), lambda b: (b, 0)),
                  pl.BlockSpec((seq, dq + 2 * dkv), lambda b: (b, 0)),
                  small, small],
        out_specs=[pl.BlockSpec((seq, dq + 2 * dkv), lambda b: (b, 0)), small, small, heads],
        out_shape=[_sds((t, dq + 2 * dkv), BF16), _sds((1, LANES), F32), _sds((1, LANES), F32),
                   _sds((N_Q_HEADS, LANES), F32)],
        scratch_shapes=[pltpu.VMEM((seq, 2 * dkv), F32), pltpu.VMEM((2, N_PAIRS, 2 * BLOCK, BLOCK), F32)],
        compiler_params=_params(("arbitrary",)))(sinks, do, qkv, qg_pair, kg_pair)


def _place():
    x, y, c = lax.axis_index("x"), lax.axis_index("y"), lax.axis_index("c")
    other_chips = [(1 - x, y), (x, 1 - y), (1 - x, 1 - y)]
    return x, y, c, other_chips


def _half_rows(c, rows):
    rh = rows // 2
    return pl.ds(pl.multiple_of(c * rh, BF16_ROWS), rh)


def _cast_own(name, w, place, layer=None):
    nl, r, cdim = w.shape
    first = 0
    if layer is not None:
        nl, first = 1, layer
    rt = _row_tile(r, 4 * cdim, ELEMENTWISE_BLOCK)

    def body(s_ref, w_ref, o_ref):
        o_ref[...] = w_ref[...].astype(BF16)

    grid_spec = pltpu.PrefetchScalarGridSpec(
        num_scalar_prefetch=1, grid=(nl, r // rt),
        in_specs=[pl.BlockSpec((None, rt, cdim), lambda l, i, s: (first + l, i, 0))],
        out_specs=pl.BlockSpec((None, None, rt, cdim), lambda l, i, s: (l, s[1], i, 0)))
    return pl.pallas_call(
        body, name=name, grid_spec=grid_spec, out_shape=_sds((nl, N_CHIPS, r, cdim), BF16),
        compiler_params=_params(("parallel", "parallel")))(place, w)


def _gather_protocol(outs, shapes, send_sems, recv_sems):
    n = len(outs)
    x, y, c, other_chips = _place()
    me_chip = 2 * x + y
    sibling = (x, y, 1 - c)

    def rows(u, chip, half):
        return outs[u].at[:, chip, _half_rows(half, shapes[u][2]), :]

    def copy(sem, part, to):
        return pltpu.make_async_remote_copy(src_ref=part, dst_ref=part, send_sem=send_sems.at[sem],
                                            recv_sem=recv_sems.at[sem], device_id=to, device_id_type=MESH)

    sends = []
    for u in range(n):
        for k, chip in enumerate(other_chips):
            cp = copy(6 * u + k, rows(u, me_chip, c), (*chip, c))
            cp.start()
            sends.append(cp)
    for u in range(n):
        for k, chip in enumerate(other_chips):
            got = rows(u, 2 * chip[0] + chip[1], c)
            copy(6 * u + k, got, (*chip, c)).wait_recv()
            cp = copy(6 * u + 3 + k, got, sibling)
            cp.start()
            sends.append(cp)
    for u in range(n):
        for k, chip in enumerate(other_chips):
            copy(6 * u + 3 + k, rows(u, 2 * chip[0] + chip[1], 1 - c), sibling).wait_recv()
    for cp in sends:
        cp.wait_send()


def _hbm_ref(a):
    return jax.new_ref(a, memory_space=pltpu.MemorySpace.HBM)


def _sibling_peer():
    x, y, c, _ = _place()
    return [(x, y, 1 - c)]


def _chip_peers():
    x, y, c, other_chips = _place()
    return [(*chip, c) for chip in other_chips]


def _gather_peers():
    return _chip_peers() + _sibling_peer()


def _on_sequencer(name, collective_id, n_sems, peers, protocol, operands=(), out_types=()):
    n_in, n_out = len(operands), len(out_types)

    def launch(*refs):
        send_sems, recv_sems = refs[n_in + n_out:]
        barrier = pltpu.get_barrier_semaphore()
        targets = peers()
        for peer in targets:
            pl.semaphore_signal(barrier, inc=1, device_id=peer, device_id_type=MESH)
        pl.semaphore_wait(barrier, len(targets))
        protocol(refs[:n_in], refs[n_in:n_in + n_out], send_sems, recv_sems)

    return pl.kernel(
        launch, out_type=tuple(out_types), mesh=plsc.ScalarSubcoreMesh(axis_name="sequencer", num_cores=1), name=name,
        scratch_types=(pltpu.SemaphoreType.DMA((n_sems,)), pltpu.SemaphoreType.DMA((n_sems,))),
        compiler_params=pltpu.CompilerParams(collective_id=collective_id))(*operands)


def _seq_allgather(name, collective_id, bufs):
    shapes = [b.shape for b in bufs]
    refs = [_hbm_ref(b) for b in bufs]
    _on_sequencer(name, collective_id, 6 * len(bufs), _gather_peers,
                  lambda ins, outs, send_sems, recv_sems: _gather_protocol(refs, shapes, send_sems, recv_sems))
    return [r[...] for r in refs]


def _taps_protocol(block_ref, got_ref, send_sems, recv_sems, first_sem):
    x, y, c, other_chips = _place()
    copies = []
    for k, chip in enumerate(other_chips):
        cp = pltpu.make_async_remote_copy(src_ref=block_ref, dst_ref=got_ref.at[k], send_sem=send_sems.at[first_sem + k],
                                          recv_sem=recv_sems.at[first_sem + k], device_id=(*chip, c), device_id_type=MESH)
        cp.start()
        copies.append(cp)
    return copies


def _seq_allgather_conv(collective_id, bufs, cw_block):
    shapes = [b.shape for b in bufs]
    refs = [_hbm_ref(b) for b in bufs]

    def protocol(ins, outs, send_sems, recv_sems):
        taps = _taps_protocol(ins[0], outs[0], send_sems, recv_sems, 6 * len(bufs))
        _gather_protocol(refs, shapes, send_sems, recv_sems)
        for cp in taps:
            cp.wait_recv()
        for cp in taps:
            cp.wait_send()

    (got,) = _on_sequencer("allgather_conv", collective_id, 6 * len(bufs) + 3, _gather_peers, protocol,
                           operands=(cw_block,), out_types=(_sds((3, *cw_block.shape), F32),))
    return [r[...] for r in refs], got


def _exchange_protocol(gs, outs, shapes, send_sems, recv_sems):
    x, y, c, _ = _place()
    sends = []
    for u in range(len(gs)):
        cp = pltpu.make_async_remote_copy(
            src_ref=gs[u].at[:, _half_rows(1 - c, shapes[u][1]), :], dst_ref=outs[u],
            send_sem=send_sems.at[u], recv_sem=recv_sems.at[u], device_id=(x, y, 1 - c), device_id_type=MESH)
        cp.start()
        sends.append(cp)
    for cp in sends:
        cp.wait_recv()
    for cp in sends:
        cp.wait_send()


def _seq_exchange(name, collective_id, grads):
    shapes = [g.shape for g in grads]
    return _on_sequencer(
        name, collective_id, len(grads), _sibling_peer,
        lambda gs, outs, send_sems, recv_sems: _exchange_protocol(gs, outs, shapes, send_sems, recv_sems),
        operands=grads, out_types=[_sds((s[0], s[1] // 2, s[2]), F32) for s in shapes])


def _sum_halves(name, g, got, place, after):
    _, r, cdim = g.shape
    rh = r // 2
    rt = _row_tile(rh, 4 * N_CHIPS * cdim, 2 * ELEMENTWISE_BLOCK)
    nr = rh // rt

    def body(s_ref, g_ref, got_ref, after_ref, pb_ref, pf_ref):
        pb_ref[...] = (g_ref[...] + got_ref[...]).astype(BF16)
        mine = s_ref[1]
        pf_ref[...] = g_ref[mine] + got_ref[mine]

    quarters = (N_CHIPS, rt, cdim)
    grid_spec = pltpu.PrefetchScalarGridSpec(
        num_scalar_prefetch=1, grid=(nr,),
        in_specs=[pl.BlockSpec(quarters, lambda i, s: (0, s[0] * nr + i, 0)),
                  pl.BlockSpec(quarters, lambda i, s: (0, i, 0)),
                  pl.BlockSpec(memory_space=pl.ANY)],
        out_specs=[pl.BlockSpec(quarters, lambda i, s: (0, i, 0)),
                   pl.BlockSpec((rt, cdim), lambda i, s: (i, 0))])
    return pl.pallas_call(
        body, name=name, grid_spec=grid_spec,
        out_shape=[_sds((N_CHIPS, rh, cdim), BF16), _sds((rh, cdim), F32)],
        compiler_params=_params(("parallel",)))(place, g, got, after)


def _scatter_protocol(ps, outs, send_sems, recv_sems):
    x, y, c, other_chips = _place()
    sends = []
    for u in range(len(ps)):
        for k, chip in enumerate(other_chips):
            cp = pltpu.make_async_remote_copy(
                src_ref=ps[u].at[2 * chip[0] + chip[1]], dst_ref=outs[u].at[k],
                send_sem=send_sems.at[3 * u + k], recv_sem=recv_sems.at[3 * u + k],
                device_id=(*chip, c), device_id_type=MESH)
            cp.start()
            sends.append(cp)
    for cp in sends:
        cp.wait_recv()
    for cp in sends:
        cp.wait_send()


def _seq_scatter(name, collective_id, partials):
    return _on_sequencer(
        name, collective_id, 3 * len(partials), _chip_peers, _scatter_protocol,
        operands=partials, out_types=[_sds((3, p.shape[1], p.shape[2]), BF16) for p in partials])


def _sum_partials(name, own, got, place, layer, nl, prev, after):
    rh, cdim = own.shape
    rt = _row_tile(rh, 4 * cdim, ELEMENTWISE_BLOCK)
    nr = rh // rt

    def body(s_ref, own_ref, got_ref, *rest):
        o_ref = rest[-1]
        o_ref[...] = ((own_ref[...] + got_ref[0].astype(F32)) + got_ref[1].astype(F32)) + got_ref[2].astype(F32)

    in_specs = [pl.BlockSpec((rt, cdim), lambda i, s: (i, 0)), pl.BlockSpec((3, rt, cdim), lambda i, s: (0, i, 0)),
                pl.BlockSpec(memory_space=pl.ANY)]
    args = [place, own, got, after]
    aliases = {}
    if prev is not None:
        in_specs.append(pl.BlockSpec(memory_space=pl.ANY))
        args.append(prev)
        aliases = {4: 0}
    grid_spec = pltpu.PrefetchScalarGridSpec(
        num_scalar_prefetch=1, grid=(nr,), in_specs=in_specs,
        out_specs=pl.BlockSpec((None, rt, cdim), lambda i, s: (layer, s[0] * nr + i, 0)))
    return pl.pallas_call(
        body, name=name, grid_spec=grid_spec, out_shape=_sds((nl, 2 * rh, cdim), F32),
        input_output_aliases=aliases, compiler_params=_params(("parallel",)))(*args)


def _share_protocol(outs, shapes, units, send_sems, recv_sems):
    x, y, c, _ = _place()
    sends = []
    for u, (w, l) in enumerate(units):
        mine = outs[w].at[l, _half_rows(c, shapes[w][1]), :]
        cp = pltpu.make_async_remote_copy(src_ref=mine, dst_ref=mine, send_sem=send_sems.at[u],
                                          recv_sem=recv_sems.at[u], device_id=(x, y, 1 - c), device_id_type=MESH)
        cp.start()
        sends.append(cp)
    for u, (w, l) in enumerate(units):
        theirs = outs[w].at[l, _half_rows(1 - c, shapes[w][1]), :]
        pltpu.make_async_remote_copy(src_ref=theirs, dst_ref=theirs, send_sem=send_sems.at[u],
                                     recv_sem=recv_sems.at[u], device_id=(x, y, 1 - c),
                                     device_id_type=MESH).wait_recv()
    for cp in sends:
        cp.wait_send()


def _seq_share(name, collective_id, bufs):
    shapes = [b.shape for b in bufs]
    units = [(w, l) for w in range(len(bufs)) for l in range(shapes[w][0])]
    refs = [_hbm_ref(b) for b in bufs]
    _on_sequencer(name, collective_id, len(units), _sibling_peer,
                  lambda ins, outs, send_sems, recv_sems: _share_protocol(refs, shapes, units, send_sems, recv_sems))
    return [r[...] for r in refs]


def _gather_blocks(block_ref, all_ref, send_sems, recv_sems):
    x, y, c, _ = _place()
    me = 4 * x + 2 * y + c
    all_ref[me] = block_ref[...]
    sends = []
    for rel in range(1, 8):
        fx, fy, fc = (rel >> 2) & 1, (rel >> 1) & 1, rel & 1
        peer = (x ^ fx, y ^ fy, c ^ fc)
        cp = pltpu.make_async_remote_copy(src_ref=block_ref, dst_ref=all_ref.at[me], send_sem=send_sems.at[rel - 1],
                                          recv_sem=recv_sems.at[rel - 1], device_id=peer, device_id_type=MESH)
        cp.start()
        sends.append(cp)
    for cp in sends:
        cp.wait_recv()
    for cp in sends:
        cp.wait_send()


def _adam(w, g, m, v):
    m_new = ADAM_B1 * m + (1.0 - ADAM_B1) * g
    v_new = ADAM_B2 * v + (1.0 - ADAM_B2) * (g * g)
    m_hat = m_new / (1.0 - ADAM_B1 ** ADAM_STEP)
    v_hat = v_new / (1.0 - ADAM_B2 ** ADAM_STEP)
    delta = -ADAM_LR * (m_hat / (jnp.sqrt(v_hat) + ADAM_EPS) + ADAM_WD * w)
    return delta, m_new, v_new


def _small_step(dnm0, dnm1, dnf0, dnf1, dcw, dqg, dkg, dsk, loss, w_blk, m_blk, v_blk, cw_cols):
    d = w_blk.shape[1]
    vm = pl.BlockSpec(memory_space=pltpu.VMEM)

    def reduce_body(dnm0_ref, dnm1_ref, dnf0_ref, dnf1_ref, dcw_ref, dqg_ref, dkg_ref, dsk_ref, loss_ref,
                    g_ref, blk_ref, all_ref, send_sems, recv_sems):
        blk_ref[...] = jnp.zeros_like(blk_ref)
        for row, part_ref in ((SENT_NORM_MIXER, dnm0_ref), (SENT_NORM_MIXER + 1, dnm1_ref),
                              (SENT_NORM_FFN, dnf0_ref), (SENT_NORM_FFN + 1, dnf1_ref)):
            blk_ref[row:row + 1, :] = jnp.sum(part_ref[...], axis=0, keepdims=True)
        blk_ref[SENT_CONV_W:SENT_CONV_W + 3, :] = dcw_ref[...]
        misc = slice(SENT_MISC, SENT_MISC + 1)
        for tile, gain_ref in ((TILE_Q_GAIN, dqg_ref), (TILE_K_GAIN, dkg_ref)):
            pair = gain_ref[...]
            blk_ref[misc, tile * LANES:(tile + 1) * LANES] = pair + pltpu.roll(pair, HEAD_DIM, 1)
        for h in range(N_Q_HEADS):
            lane = TILE_SINKS * LANES + h
            blk_ref[misc, lane:lane + 1] = jnp.sum(dsk_ref[h:h + 1, :], axis=1, keepdims=True)
        blk_ref[misc, TILE_LOSS * LANES:(TILE_LOSS + 1) * LANES] = jnp.broadcast_to(loss_ref[...], (1, LANES))
        _gather_blocks(blk_ref, all_ref, send_sems, recv_sems)
        g = all_ref[0]
        for dev in range(1, 8):
            g = g + all_ref[dev]
        g_ref[...] = jnp.zeros_like(g_ref)
        for sent, row, n in ((SENT_NORM_MIXER, ROW_NORM_MIXER, 2), (SENT_NORM_FFN, ROW_NORM_FFN, 2),
                             (SENT_CONV_W, ROW_CONV_W, 3), (SENT_MISC, ROW_MISC, 1)):
            g_ref[row:row + n, :] = g[sent:sent + n]

    g_blk = pl.pallas_call(
        reduce_body, name="small_allreduce", in_specs=[vm] * 9, out_specs=vm, out_shape=_sds((SMALL_ROWS, d), F32),
        scratch_shapes=[pltpu.VMEM((SUBLANES, d), F32), pltpu.VMEM((8, SUBLANES, d), F32),
                        pltpu.SemaphoreType.DMA((7,)), pltpu.SemaphoreType.DMA((7,))],
    )(dnm0, dnm1, dnf0, dnf1, dcw, dqg, dkg, dsk, loss)

    def body(g_ref, w_ref, m_ref, v_ref, *out_refs):
        g = g_ref[...]
        misc = slice(ROW_MISC, ROW_MISC + 1)
        out_refs[0][...] = g[misc, TILE_LOSS * LANES:TILE_LOSS * LANES + 1]
        chip = 2 * lax.axis_index("x") + lax.axis_index("y")
        for i, blk in enumerate((g, *_adam(w_ref[...], g, m_ref[...], v_ref[...]))):
            nm_ref, nf_ref, cw_ref, qg_ref, kg_ref, sk_ref = out_refs[1 + 6 * i:7 + 6 * i]
            nm_ref[...] = blk[ROW_NORM_MIXER:ROW_NORM_MIXER + 2]
            nf_ref[...] = blk[ROW_NORM_FFN:ROW_NORM_FFN + 2]
            qg_ref[...] = blk[misc, TILE_Q_GAIN * LANES:TILE_Q_GAIN * LANES + HEAD_DIM]
            kg_ref[...] = blk[misc, TILE_K_GAIN * LANES:TILE_K_GAIN * LANES + HEAD_DIM]
            sk_ref[...] = blk[misc, TILE_SINKS * LANES:TILE_SINKS * LANES + N_Q_HEADS]
            for q in range(N_CHIPS):
                @pl.when(chip == q)
                def _(blk=blk, cw_ref=cw_ref, q=q):
                    cw_ref[0] = blk[ROW_CONV_W:ROW_CONV_W + 3, q * cw_cols:(q + 1) * cw_cols]

    group = [_sds((2, d), F32), _sds((2, d), F32), _sds((1, 3, cw_cols), F32), _sds((1, HEAD_DIM), F32),
             _sds((1, HEAD_DIM), F32), _sds((1, N_Q_HEADS), F32)]
    outs = pl.pallas_call(
        body, name="small_adam", in_specs=[vm] * 4, out_specs=[vm] * 25, out_shape=[_sds((1, 1), F32)] + group * 4,
    )(g_blk, w_blk, m_blk, v_blk)
    names = ("norm_mixer", "norm_ffn", "conv_w", "attn_q_gain", "attn_k_gain", "attn_sinks")
    return outs[0], [dict(zip(names, outs[1 + 6 * i:7 + 6 * i])) for i in range(4)]


def _adam_step(name, w, g, m, v):
    nl, r, cdim = w.shape
    rt = _row_tile(r, 4 * cdim, ELEMENTWISE_BLOCK)

    def body(w_ref, g_ref, m_ref, v_ref, go_ref, d_ref, mo_ref, vo_ref):
        gv = g_ref[...]
        go_ref[...] = gv
        delta, m_new, v_new = _adam(w_ref[...], gv, m_ref[...], v_ref[...])
        d_ref[...] = delta
        mo_ref[...] = m_new
        vo_ref[...] = v_new

    spec = pl.BlockSpec((None, rt, cdim), lambda l, i: (l, i, 0))
    return pl.pallas_call(
        body, name=name, grid=(nl, r // rt), in_specs=[spec] * 4, out_specs=[spec] * 4,
        out_shape=[_sds(w.shape, F32)] * 4,
        compiler_params=_params(("parallel", "parallel")))(w, g, m, v)


def _pad_rows(a, rows=SUBLANES):
    return jnp.pad(a, ((0, rows - a.shape[0]), (0, 0)))


def _small_block(nm, nf, cw_local, qg, kg, sk, chip):
    d = nm.shape[1]
    cw_rows = lax.dynamic_update_slice(jnp.zeros((SUBLANES, d), F32), cw_local, (0, chip * cw_local.shape[1]))
    misc = jnp.concatenate([qg, qg, kg, kg, jnp.pad(sk, ((0, 0), (0, LANES - sk.shape[1]))),
                            jnp.zeros((1, d - 3 * LANES), F32)], axis=1)
    return jnp.concatenate([_pad_rows(nm), _pad_rows(nf), cw_rows, _pad_rows(misc)], axis=0)


WEIGHT_NAMES = ("conv_w_in", "conv_w", "conv_w_out", "attn_w_qkv", "attn_q_gain", "attn_k_gain", "attn_sinks",
                "attn_w_o", "norm_mixer", "norm_ffn", "ffn_w_gate_up", "ffn_w_down")
BIG = ("conv_w_in", "conv_w_out", "attn_w_qkv", "attn_w_o", "ffn_w_gate_up", "ffn_w_down")


def kernel(x, conv_w_in, conv_w, conv_w_out, attn_w_qkv, attn_q_gain, attn_k_gain, attn_sinks, attn_w_o, norm_mixer, norm_ffn, ffn_w_gate_up, ffn_w_down, loss_target, m_conv_w_in, m_conv_w, m_conv_w_out, m_attn_w_qkv, m_attn_q_gain, m_attn_k_gain, m_attn_sinks, m_attn_w_o, m_norm_mixer, m_norm_ffn, m_ffn_w_gate_up, m_ffn_w_down, v_conv_w_in, v_conv_w, v_conv_w_out, v_attn_w_qkv, v_attn_q_gain, v_attn_k_gain, v_attn_sinks, v_attn_w_o, v_norm_mixer, v_norm_ffn, v_ffn_w_gate_up, v_ffn_w_down):
    w = dict(conv_w_in=conv_w_in, conv_w=conv_w, conv_w_out=conv_w_out, attn_w_qkv=attn_w_qkv,
             attn_q_gain=attn_q_gain, attn_k_gain=attn_k_gain, attn_sinks=attn_sinks, attn_w_o=attn_w_o,
             norm_mixer=norm_mixer, norm_ffn=norm_ffn, ffn_w_gate_up=ffn_w_gate_up, ffn_w_down=ffn_w_down)
    m = dict(conv_w_in=m_conv_w_in, conv_w=m_conv_w, conv_w_out=m_conv_w_out, attn_w_qkv=m_attn_w_qkv,
             attn_q_gain=m_attn_q_gain, attn_k_gain=m_attn_k_gain, attn_sinks=m_attn_sinks, attn_w_o=m_attn_w_o,
             norm_mixer=m_norm_mixer, norm_ffn=m_norm_ffn, ffn_w_gate_up=m_ffn_w_gate_up, ffn_w_down=m_ffn_w_down)
    v = dict(conv_w_in=v_conv_w_in, conv_w=v_conv_w, conv_w_out=v_conv_w_out, attn_w_qkv=v_attn_w_qkv,
             attn_q_gain=v_attn_q_gain, attn_k_gain=v_attn_k_gain, attn_sinks=v_attn_sinks, attn_w_o=v_attn_w_o,
             norm_mixer=v_norm_mixer, norm_ffn=v_norm_ffn, ffn_w_gate_up=v_ffn_w_gate_up, ffn_w_down=v_ffn_w_down)

    nseq, seq, d = x.shape
    t = nseq * seq
    chip = 2 * lax.axis_index("x") + lax.axis_index("y")
    core = lax.axis_index("c")
    place = jnp.stack([core, chip]).astype(jnp.int32)
    x0 = x.reshape(t, d)
    tgt = loss_target.reshape(t, d)

    cw_block = lax.dynamic_update_slice(jnp.zeros((SUBLANES, d), F32), conv_w[0], (0, chip * conv_w.shape[2]))
    def cast(k, layer=None):
        return _cast_own(f"cast_{k}" + ("" if layer is None else str(layer)), w[k], place, layer)

    (w_in,), cw_got = _seq_allgather_conv(1, [cast("conv_w_in")], cw_block)
    w_out, w_gu0, w_dn0 = _seq_allgather(
        "allgather_ffn0", 2, [cast("conv_w_out"), cast("ffn_w_gate_up", 0), cast("ffn_w_down", 0)])
    w_qkv, w_o, w_gu1, w_dn1 = _seq_allgather(
        "allgather_rest", 3, [cast("attn_w_qkv"), cast("attn_w_o"), cast("ffn_w_gate_up", 1), cast("ffn_w_down", 1)])
    w_out = w_out.reshape(1, d, d)
    w_o = w_o.reshape(1, d, d)
    w_gu = [w_gu0, w_gu1]
    w_dn = [w_dn0.reshape(1, D_FF, d), w_dn1.reshape(1, D_FF, d)]

    qg_pair = jnp.concatenate([attn_q_gain, attn_q_gain], axis=1)
    kg_pair = jnp.concatenate([attn_k_gain, attn_k_gain], axis=1)

    def ffn_bwd(i, dxo, xin, h, g, u, a):
        g_dn = _wgrad_down(f"ffn{i}_down_wgrad", a, dxo, D_FF // 2)
        dg, du = _mm_down_t_swiglu(f"ffn{i}_down_dgrad", dxo, w_dn[i], 0, g, u)
        g_gu = _wgrad_up2(f"ffn{i}_up_wgrad", h, dg, du)
        dxi, dgain = _dgrad_norm_ffn(f"ffn{i}_up_dgrad", dg, du, w_gu[i], 0, xin, norm_ffn[i:i + 1], dxo)
        return dxi, dgain, g_gu, g_dn

    h0, bcx = _mm_norm_up_joined("conv_in", x0, norm_mixer[0:1], w_in, 512)
    z = _conv_fwd(bcx, cw_block, cw_got, nseq, seq)
    x1, h1 = _mm_down_norm("conv_out", z, w_out, 0, x0, norm_ffn[0:1])
    g0, u0, a0 = _mm_up_swiglu("ffn0_up", h1, w_gu[0], 0)
    x2, h2 = _mm_down_norm("ffn0_down", a0, w_dn[0], 0, x1, norm_mixer[1:2])
    qkv = _mm_up_joined("attn_qkv", h2, w_qkv, 1024)
    o = _attn_fwd(qkv, qg_pair, kg_pair, attn_sinks, nseq, seq)
    x3, h3 = _mm_down_norm("attn_out", o, w_o, 0, x2, norm_ffn[1:2])
    g1, u1, a1 = _mm_up_swiglu("ffn1_up", h3, w_gu[1], 0)
    dy, loss_part = _mm_down_loss("ffn1_down", a1, w_dn[1], 0, x3, tgt)

    finished = {k: None for k in BIG}

    def exchange(tag, cid, units):
        return units, _seq_exchange(f"exchange_{tag}", cid, [g for _, _, g in units])

    def scatter(tag, cid, group, after):
        units, got = group
        sums = [_sum_halves(f"sum_halves_{k}{l}", g, r, place, after) for (k, l, g), r in zip(units, got)]
        return units, sums, _seq_scatter(f"scatter_{tag}", cid, [pb for pb, _ in sums])

    def finish(group, after):
        units, sums, arrived = group
        for (k, l, _), (_, pf), r in zip(units, sums, arrived):
            finished[k] = _sum_partials(f"sum_partials_{k}{l}", pf, r, place, l, w[k].shape[0], finished[k], after)

    dx3, dnf1, g_gu1, g_dn1 = ffn_bwd(1, dy, x3, h3, g1, u1, a1)
    ffn1 = exchange("ffn1", 4, [("ffn_w_down", 1, g_dn1), ("ffn_w_gate_up", 1, g_gu1)])
    g_o = _wgrad_down("attn_out_wgrad", o, dx3, d)
    do = _mm_down_t("attn_out_dgrad", dx3, w_o, 0)
    ffn1 = scatter("ffn1", 8, ffn1, do)
    dqkv, dqg, dkg, dsk = _attn_bwd(do, qkv, qg_pair, kg_pair, attn_sinks, nseq, seq)
    g_qkv = _wgrad_joined("attn_qkv_wgrad", h2, dqkv)
    attn = exchange("attn", 5, [("attn_w_o", 0, g_o), ("attn_w_qkv", 0, g_qkv)])
    dx2, dnm1 = _dgrad_norm_qkv("attn_qkv_dgrad", dqkv, w_qkv, x2, norm_mixer[1:2], dx3)
    finish(ffn1, dx2)
    attn = scatter("attn", 9, attn, dx2)
    dx1, dnf0, g_gu0, g_dn0 = ffn_bwd(0, dx2, x1, h1, g0, u0, a0)
    ffn0 = exchange("ffn0", 6, [("ffn_w_down", 0, g_dn0), ("ffn_w_gate_up", 0, g_gu0)])
    g_out = _wgrad_down("conv_out_wgrad", z, dx1, d)
    dz = _mm_down_t("conv_out_dgrad", dx1, w_out, 0)
    finish(attn, dz)
    ffn0 = scatter("ffn0", 10, ffn0, dz)
    dbcx, dcw = _conv_bwd(dz, bcx, cw_block, cw_got, nseq, seq)
    g_in = _wgrad_conv_in("conv_in_wgrad", h0, dbcx, conv_w_in.shape[2])
    conv = exchange("conv", 7, [("conv_w_out", 0, g_out), ("conv_w_in", 0, g_in)])
    dx0, dnm0 = _dgrad_norm_conv("conv_in_dgrad", dbcx, w_in, x0, norm_mixer[0:1], dx1)
    finish(ffn0, dx0)
    late = ("attn_w_qkv", "attn_w_o", "ffn_w_gate_up", "ffn_w_down")
    grads_late = _seq_share("share_late", 12, [finished[k] for k in late])
    conv = scatter("conv", 11, conv, dx0)

    grad, delta, new_m, new_v = {}, {}, {}, {}

    def adam(k, g):
        grad[k], delta[k], new_m[k], new_v[k] = _adam_step(f"adam_{k}", w[k], g, m[k], v[k])

    for k, g in zip(late, grads_late):
        adam(k, g)

    def blocks(src):
        return _small_block(src["norm_mixer"], src["norm_ffn"], src["conv_w"][0], src["attn_q_gain"],
                            src["attn_k_gain"], src["attn_sinks"], chip)

    loss, small = _small_step(dnm0, dnm1, dnf0, dnf1, dcw, dqg, dkg, dsk, loss_part,
                              blocks(w), blocks(m), blocks(v), conv_w.shape[2])
    for dst, part in zip((grad, delta, new_m, new_v), small):
        dst.update(part)

    done = sum(new_v[k][0, 0:1, 0:1] for k in late) + loss
    finish(conv, done)
    last = ("conv_w_in", "conv_w_out")
    for k, g in zip(last, _seq_share("share_last", 13, [finished[k] for k in last])):
        adam(k, g)

    return (loss.reshape(()), dx0.reshape(nseq, seq, d), *[grad[k] for k in WEIGHT_NAMES], *[delta[k] for k in WEIGHT_NAMES],
            *[new_m[k] for k in WEIGHT_NAMES], *[new_v[k] for k in WEIGHT_NAMES])
```

```python
import jax
import jax.numpy as jnp
from jax import lax
from jax.experimental import pallas as pl
from jax.experimental.pallas import tpu as pltpu
from jax.experimental.pallas import tpu_sc as plsc

F32 = jnp.float32
BF16 = jnp.bfloat16

D_FF = 2816
N_Q_HEADS = 16
N_KV_HEADS = 4
HEAD_DIM = 64
WINDOW = 128
BLOCK = 128
EPS = 1e-6
N_CHIPS = 4
LANES = 128
SUBLANES = 8
BF16_ROWS = 16
MXU_COLS = 256
VMEM_LIMIT = 48 * 1024 * 1024
ADAM_LR, ADAM_B1, ADAM_B2, ADAM_EPS, ADAM_WD, ADAM_STEP = 0.001, 0.9, 0.999, 1e-08, 0.01, 10
ALIBI_SLOPES = tuple(2.0 ** (-8.0 * (h + 1) / N_Q_HEADS) for h in range(N_Q_HEADS))
SMALL_ROWS = 32
ROW_NORM_MIXER, ROW_NORM_FFN, ROW_CONV_W, ROW_MISC = 0, 8, 16, 24
SENT_NORM_MIXER, SENT_NORM_FFN, SENT_CONV_W, SENT_MISC = 0, 2, 4, 7
TILE_Q_GAIN, TILE_K_GAIN, TILE_SINKS, TILE_LOSS = 0, 1, 2, 3
MESH = pl.DeviceIdType.MESH

NN = ((1,), (0,))
NT = ((1,), (1,))
TN = ((0,), (0,))


def _dot(a, b, dims):
    return lax.dot_general(a, b, (dims, ((), ())), preferred_element_type=F32)


def _pick(n, cands):
    for c in cands:
        if n % c == 0:
            return c
    raise ValueError((n, cands))


def _row_tile(rows, row_bytes, cap_bytes):
    fits = [r for r in range(BF16_ROWS, rows + 1, BF16_ROWS) if rows % r == 0 and r * row_bytes <= cap_bytes]
    if not fits:
        raise ValueError((rows, row_bytes, cap_bytes))
    return fits[-1]


ELEMENTWISE_BLOCK = 3 << 19


def _resident(block_shape, index_map):
    return pl.BlockSpec(block_shape, index_map, pipeline_mode=pl.Buffered(1))


def _params(sem):
    return pltpu.CompilerParams(dimension_semantics=sem, vmem_limit_bytes=VMEM_LIMIT)


def _sds(shape, dtype):
    return jax.ShapeDtypeStruct(shape, dtype)


def _rms(xv):
    return lax.rsqrt(jnp.mean(xv * xv, axis=-1, keepdims=True) + EPS)


def _sigmoid(g):
    return 1.0 / (1.0 + jnp.exp(-g))


def _mm_up_joined(name, a, w4, tm_pref):
    t, k = a.shape
    _, _, _, nq = w4.shape
    tm = _pick(t, (tm_pref, 256, 128))

    def body(a_ref, w_ref, o_ref, wcat_ref):
        @pl.when(pl.program_id(0) == 0)
        def _():
            for q in range(N_CHIPS):
                wcat_ref[:, q * nq:(q + 1) * nq] = w_ref[q]

        o_ref[...] = _dot(a_ref[...], wcat_ref[...], NN).astype(BF16)

    return pl.pallas_call(
        body, name=name, grid=(t // tm,),
        in_specs=[pl.BlockSpec((tm, k), lambda i: (i, 0)),
                  pl.BlockSpec((None, N_CHIPS, k, nq), lambda i: (0, 0, 0, 0))],
        out_specs=pl.BlockSpec((tm, N_CHIPS * nq), lambda i: (i, 0)),
        out_shape=_sds((t, N_CHIPS * nq), BF16),
        scratch_shapes=[pltpu.VMEM((k, N_CHIPS * nq), BF16)],
        compiler_params=_params(("arbitrary",)))(a, w4)


def _mm_norm_up_joined(name, x, gain, w4, tm_pref):
    t, k = x.shape
    _, _, _, nq = w4.shape
    tm = _pick(t, (tm_pref, 256, 128))

    def body(x_ref, g_ref, w_ref, h_ref, o_ref, wcat_ref):
        @pl.when(pl.program_id(0) == 0)
        def _():
            for q in range(N_CHIPS):
                wcat_ref[:, q * nq:(q + 1) * nq] = w_ref[q]

        xv = x_ref[...]
        h = ((xv * _rms(xv)) * g_ref[...]).astype(BF16)
        h_ref[...] = h
        o_ref[...] = _dot(h, wcat_ref[...], NN).astype(BF16)

    return pl.pallas_call(
        body, name=name, grid=(t // tm,),
        in_specs=[pl.BlockSpec((tm, k), lambda i: (i, 0)), pl.BlockSpec((1, k), lambda i: (0, 0)),
                  _resident((None, N_CHIPS, k, nq), lambda i: (0, 0, 0, 0))],
        out_specs=[pl.BlockSpec((tm, k), lambda i: (i, 0)), pl.BlockSpec((tm, N_CHIPS * nq), lambda i: (i, 0))],
        out_shape=[_sds((t, k), BF16), _sds((t, N_CHIPS * nq), BF16)],
        scratch_shapes=[pltpu.VMEM((k, N_CHIPS * nq), BF16)],
        compiler_params=_params(("arbitrary",)))(x, gain, w4)


def _mm_up_swiglu(name, h, w4, layer):
    t, k = h.shape
    _, _, _, nq = w4.shape
    tm = _pick(t, (512, 256, 128))

    def body(h_ref, wg_ref, wu_ref, dag_ref, dau_ref, a_ref):
        hv = h_ref[...]
        g = _dot(hv, wg_ref[...], NN)
        u = _dot(hv, wu_ref[...], NN)
        sg = _sigmoid(g)
        silu = g * sg
        a = silu * u
        dag_ref[...] = (a + sg * (u - a)).astype(BF16)
        dau_ref[...] = silu.astype(BF16)
        a_ref[...] = a.astype(BF16)

    half = N_CHIPS // 2
    out = pl.BlockSpec((tm, nq), lambda j, i: (i, j))
    return pl.pallas_call(
        body, name=name, grid=(half, t // tm),
        in_specs=[pl.BlockSpec((tm, k), lambda j, i: (i, 0)),
                  pl.BlockSpec((None, None, k, nq), lambda j, i: (layer, j, 0, 0)),
                  pl.BlockSpec((None, None, k, nq), lambda j, i: (layer, half + j, 0, 0))],
        out_specs=[out, out, out],
        out_shape=[_sds((t, half * nq), BF16)] * 3,
        compiler_params=_params(("parallel", "parallel")))(h, w4, w4)


def _mm_down_norm(name, a, w, layer, res, gain):
    t, kf = a.shape
    _, _, n = w.shape
    tm = _pick(t, (1024, 512, 256, 128))

    def body(a_ref, w_ref, r_ref, g_ref, o_ref, h_ref):
        xo = r_ref[...] + _dot(a_ref[...], w_ref[...], NN)
        o_ref[...] = xo
        h_ref[...] = ((xo * _rms(xo)) * g_ref[...]).astype(BF16)

    row = pl.BlockSpec((tm, n), lambda i: (i, 0))
    return pl.pallas_call(
        body, name=name, grid=(t // tm,),
        in_specs=[pl.BlockSpec((tm, kf), lambda i: (i, 0)),
                  _resident((None, kf, n), lambda i: (layer, 0, 0)),
                  row, pl.BlockSpec((1, n), lambda i: (0, 0))],
        out_specs=[row, row],
        out_shape=[_sds((t, n), F32), _sds((t, n), BF16)],
        compiler_params=_params(("parallel",)))(a, w, res, gain)


def _mm_down_loss(name, a, w, layer, res, tgt):
    t, kf = a.shape
    _, _, n = w.shape
    tm = _pick(t, (1024, 512, 256, 128))
    steps = t // tm

    def body(a_ref, w_ref, r_ref, t_ref, dy_ref, l_ref, acc_ref):
        i = pl.program_id(0)

        @pl.when(i == 0)
        def _():
            acc_ref[...] = jnp.zeros_like(acc_ref)

        e = (r_ref[...] + _dot(a_ref[...], w_ref[...], NN)) - t_ref[...]
        dy_ref[...] = e * (1.0 / n)
        acc_ref[...] += (e * e).reshape(tm // SUBLANES, SUBLANES, n).sum(axis=0)

        @pl.when(i == steps - 1)
        def _():
            l_ref[...] = jnp.sum(acc_ref[...], keepdims=True) * (0.5 / n)

    row = pl.BlockSpec((tm, n), lambda i: (i, 0))
    return pl.pallas_call(
        body, name=name, grid=(steps,),
        in_specs=[pl.BlockSpec((tm, kf), lambda i: (i, 0)),
                  _resident((None, kf, n), lambda i: (layer, 0, 0)), row, row],
        out_specs=[row, pl.BlockSpec((1, 1), lambda i: (0, 0))],
        out_shape=[_sds((t, n), F32), _sds((1, 1), F32)],
        scratch_shapes=[pltpu.VMEM((SUBLANES, n), F32)],
        compiler_params=_params(("arbitrary",)))(a, w, res, tgt)


def _mm_down_t(name, dx, w, layer):
    t, n = dx.shape
    _, kf, _ = w.shape
    tm = _pick(t, (1024, 512, 256, 128))

    def body(a_ref, w_ref, o_ref):
        o_ref[...] = _dot(a_ref[...].astype(BF16), w_ref[...], NT).astype(BF16)

    return pl.pallas_call(
        body, name=name, grid=(t // tm,),
        in_specs=[pl.BlockSpec((tm, n), lambda i: (i, 0)),
                  _resident((None, kf, n), lambda i: (layer, 0, 0))],
        out_specs=pl.BlockSpec((tm, kf), lambda i: (i, 0)),
        out_shape=_sds((t, kf), BF16),
        compiler_params=_params(("parallel",)))(dx, w)


def _mm_down_t_swiglu(name, dx, w, layer, g, u):
    t, n = dx.shape
    f = g.shape[1]
    tm = _pick(t, (512, 256, 128))

    def body(a_ref, w_ref, dag_ref, dau_ref, dg_ref, du_ref):
        da = _dot(a_ref[...].astype(BF16), w_ref[...], NT)
        dg_ref[...] = (da * dag_ref[...].astype(F32)).astype(BF16)
        du_ref[...] = (da * dau_ref[...].astype(F32)).astype(BF16)

    tile = pl.BlockSpec((tm, f), lambda i: (i, 0))
    return pl.pallas_call(
        body, name=name, grid=(t // tm,),
        in_specs=[pl.BlockSpec((tm, n), lambda i: (i, 0)),
                  _resident((None, f, n), lambda i: (layer, 0, 0)), tile, tile],
        out_specs=[tile, tile],
        out_shape=[_sds((t, f), BF16)] * 2,
        compiler_params=_params(("parallel",)))(dx, w, g, u)


def _dgrad_norm(name, acts, act_blocks, pieces, w4, layer, x, gain, dres):
    t, d = x.shape
    _, _, k, nq = w4.shape
    tm = _pick(t, (512, 256, 128))
    n_act = len(acts)

    def body(*refs):
        act_refs = refs[:n_act]
        w_ref, x_ref, g_ref, dr_ref, dx_ref, dg_ref = refs[n_act:]

        @pl.when(pl.program_id(0) == 0)
        def _():
            dg_ref[...] = jnp.zeros_like(dg_ref)

        dh = None
        for a_tile, w_tile in pieces(act_refs, w_ref):
            term = _dot(a_tile, w_tile, NT)
            dh = term if dh is None else dh + term
        xv = x_ref[...]
        r = _rms(xv)
        xhat = xv * r
        gd = dh * g_ref[...]
        dx_ref[...] = dr_ref[...] + r * (gd - xhat * jnp.mean(gd * xhat, axis=-1, keepdims=True))
        dg_ref[...] += (dh * xhat).reshape(tm // SUBLANES, SUBLANES, d).sum(axis=0)

    row = pl.BlockSpec((tm, d), lambda i: (i, 0))
    return pl.pallas_call(
        body, name=name, grid=(t // tm,),
        in_specs=[*act_blocks(tm),
                  _resident((None, N_CHIPS, k, nq), lambda i: (layer, 0, 0, 0)),
                  row, pl.BlockSpec((1, d), lambda i: (0, 0)), row],
        out_specs=[row, pl.BlockSpec((SUBLANES, d), lambda i: (0, 0))],
        out_shape=[_sds((t, d), F32), _sds((SUBLANES, d), F32)],
        compiler_params=_params(("arbitrary",)))(*acts, w4, x, gain, dres)


def _dgrad_norm_ffn(name, dg, du, w4, layer, x, gain, dres):
    nq = w4.shape[3]
    f = dg.shape[1]

    def blocks(tm):
        return [pl.BlockSpec((tm, f), lambda i: (i, 0))] * 2

    def pieces(act_refs, w_ref):
        dg_ref, du_ref = act_refs
        return [(dg_ref[:, 0:nq], w_ref[0]), (dg_ref[:, nq:2 * nq], w_ref[1]),
                (du_ref[:, 0:nq], w_ref[2]), (du_ref[:, nq:2 * nq], w_ref[3])]

    return _dgrad_norm(name, [dg, du], blocks, pieces, w4, layer, x, gain, dres)


def _dgrad_norm_qkv(name, dqkv, w4, x, gain, dres):
    nq = w4.shape[3]

    def blocks(tm):
        return [pl.BlockSpec((tm, N_CHIPS * nq), lambda i: (i, 0))]

    def pieces(act_refs, w_ref):
        return [(act_refs[0][:, q * nq:(q + 1) * nq], w_ref[q]) for q in range(N_CHIPS)]

    return _dgrad_norm(name, [dqkv], blocks, pieces, w4, 0, x, gain, dres)


def _dgrad_norm_conv(name, d3, w4, x, gain, dres):
    _, _, d = d3.shape
    nq = w4.shape[3]
    per_part, per_q = d // MXU_COLS, nq // MXU_COLS

    def blocks(tm):
        return [pl.BlockSpec((3, tm, d), lambda i: (0, i, 0))]

    def pieces(act_refs, w_ref):
        out = []
        for jb in range(3 * per_part):
            ca, cw = (jb % per_part) * MXU_COLS, (jb % per_q) * MXU_COLS
            out.append((act_refs[0][jb // per_part, :, ca:ca + MXU_COLS], w_ref[jb // per_q, :, cw:cw + MXU_COLS]))
        return out

    return _dgrad_norm(name, [d3], blocks, pieces, w4, 0, x, gain, dres)


def _wgrad_up2(name, h, dg, du):
    t, k = h.shape
    nq = dg.shape[1] // 2
    tk = _pick(t, (1024, 512, 256, 128))
    steps = t // tk
    half = N_CHIPS // 2

    def body(h_ref, dg_ref, du_ref, o_ref):
        q = pl.program_id(0)

        @pl.when(pl.program_id(1) == 0)
        def _():
            o_ref[...] = jnp.zeros_like(o_ref)

        @pl.when(q < half)
        def _():
            o_ref[...] += _dot(h_ref[...], dg_ref[...], TN)

        @pl.when(q >= half)
        def _():
            o_ref[...] += _dot(h_ref[...], du_ref[...], TN)

    return pl.pallas_call(
        body, name=name, grid=(N_CHIPS, steps),
        in_specs=[pl.BlockSpec((tk, k), lambda q, s: (s, 0)),
                  pl.BlockSpec((tk, nq), lambda q, s: (jnp.where(q < half, s, steps - 1), jnp.minimum(q, half - 1))),
                  pl.BlockSpec((tk, nq), lambda q, s: (jnp.where(q >= half, s, 0), jnp.maximum(q - half, 0)))],
        out_specs=pl.BlockSpec((None, k, nq), lambda q, s: (q, 0, 0)),
        out_shape=_sds((N_CHIPS, k, nq), F32),
        compiler_params=_params(("parallel", "arbitrary")))(h, dg, du)


def _wgrad_joined(name, h, dy):
    t, k = h.shape
    nq = dy.shape[1] // N_CHIPS
    tk = _pick(t, (1024, 512, 256, 128))

    def body(h_ref, dy_ref, o_ref):
        @pl.when(pl.program_id(0) == 0)
        def _():
            o_ref[...] = jnp.zeros_like(o_ref)

        res = _dot(h_ref[...], dy_ref[...], TN)
        for q in range(N_CHIPS):
            o_ref[q] += res[:, q * nq:(q + 1) * nq]

    return pl.pallas_call(
        body, name=name, grid=(t // tk,),
        in_specs=[pl.BlockSpec((tk, k), lambda s: (s, 0)), pl.BlockSpec((tk, N_CHIPS * nq), lambda s: (s, 0))],
        out_specs=pl.BlockSpec((N_CHIPS, k, nq), lambda s: (0, 0, 0)),
        out_shape=_sds((N_CHIPS, k, nq), F32),
        compiler_params=_params(("arbitrary",)))(h, dy)


def _wgrad_conv_in(name, h, d3, nq):
    t, k = h.shape
    d = d3.shape[2]
    per_part, per_q = d // MXU_COLS, nq // MXU_COLS
    tk = _pick(t, (512, 256, 128))

    def body(h_ref, d_ref, o_ref):
        @pl.when(pl.program_id(0) == 0)
        def _():
            o_ref[...] = jnp.zeros_like(o_ref)

        hv = h_ref[...]
        for part in range(3):
            res = _dot(hv, d_ref[part], TN)
            for cc in range(per_part):
                jb = part * per_part + cc
                co = (jb % per_q) * MXU_COLS
                o_ref[jb // per_q, :, co:co + MXU_COLS] += res[:, cc * MXU_COLS:(cc + 1) * MXU_COLS]

    return pl.pallas_call(
        body, name=name, grid=(t // tk,),
        in_specs=[pl.BlockSpec((tk, k), lambda s: (s, 0)), pl.BlockSpec((3, tk, d), lambda s: (0, s, 0))],
        out_specs=pl.BlockSpec((N_CHIPS, k, nq), lambda s: (0, 0, 0)),
        out_shape=_sds((N_CHIPS, k, nq), F32),
        compiler_params=_params(("arbitrary",)))(h, d3)


def _wgrad_down(name, a, dx, tmw):
    t, kf = a.shape
    n = dx.shape[1]
    tk = _pick(t, (1024, 512, 256, 128))

    def body(a_ref, b_ref, o_ref):
        @pl.when(pl.program_id(1) == 0)
        def _():
            o_ref[...] = jnp.zeros_like(o_ref)

        o_ref[...] += _dot(a_ref[...], b_ref[...].astype(BF16), TN)

    g = pl.pallas_call(
        body, name=name, grid=(kf // tmw, t // tk),
        in_specs=[pl.BlockSpec((tk, tmw), lambda j, s: (s, j)), pl.BlockSpec((tk, n), lambda j, s: (s, 0))],
        out_specs=pl.BlockSpec((tmw, n), lambda j, s: (j, 0)),
        out_shape=_sds((kf, n), F32),
        compiler_params=_params(("parallel", "arbitrary")))(a, dx)
    return g.reshape(N_CHIPS, kf // N_CHIPS, n)


def _shift_rows(u, k, rows):
    s = u.shape[0]
    if k > 0:
        r = pltpu.roll(u, k, 0)
        return jnp.concatenate([jnp.where(rows >= k, r[0:SUBLANES], 0.0), r[SUBLANES:]], axis=0)
    r = pltpu.roll(u, s + k, 0)
    return jnp.concatenate([r[:s - SUBLANES], jnp.where(rows < SUBLANES + k, r[s - SUBLANES:], 0.0)], axis=0)


def _conv_taps(cw_ref, got_ref):
    return (cw_ref[...] + got_ref[0]) + (got_ref[1] + got_ref[2])


def _conv_fwd(bcx, cw, cw_got, nseq, seq):
    t, d3 = bcx.shape
    d = d3 // 3
    cb = 2 * MXU_COLS
    nj = d // cb

    def body(b_ref, c_ref, x_ref, cw_ref, got_ref, z_ref):
        u = b_ref[...].astype(F32) * x_ref[...].astype(F32)
        rows = lax.broadcasted_iota(jnp.int32, (SUBLANES, cb), 0)
        cwv = _conv_taps(cw_ref, got_ref)
        y = cwv[2:3] * u + cwv[1:2] * _shift_rows(u, 1, rows) + cwv[0:1] * _shift_rows(u, 2, rows)
        z_ref[...] = (c_ref[...].astype(F32) * y).astype(BF16)

    return pl.pallas_call(
        body, name="conv_fwd", grid=(nseq, nj),
        in_specs=[pl.BlockSpec((seq, cb), lambda b, j: (b, j)),
                  pl.BlockSpec((seq, cb), lambda b, j: (b, nj + j)),
                  pl.BlockSpec((seq, cb), lambda b, j: (b, 2 * nj + j)),
                  pl.BlockSpec((SUBLANES, cb), lambda b, j: (0, j)),
                  pl.BlockSpec((3, SUBLANES, cb), lambda b, j: (0, 0, j))],
        out_specs=pl.BlockSpec((seq, cb), lambda b, j: (b, j)),
        out_shape=_sds((t, d), BF16),
        compiler_params=_params(("parallel", "parallel")))(bcx, bcx, bcx, cw, cw_got)


def _conv_bwd(dz, bcx, cw, cw_got, nseq, seq):
    t, d3 = bcx.shape
    d = d3 // 3
    cb = MXU_COLS
    nj = d // cb

    def body(dz_ref, b_ref, c_ref, x_ref, cw_ref, got_ref, o_ref, dcw_ref):
        @pl.when(pl.program_id(1) == 0)
        def _():
            dcw_ref[...] = jnp.zeros_like(dcw_ref)

        b = b_ref[...].astype(F32)
        c = c_ref[...].astype(F32)
        xv = x_ref[...].astype(F32)
        dzv = dz_ref[...].astype(F32)
        u = b * xv
        rows = lax.broadcasted_iota(jnp.int32, (SUBLANES, cb), 0)
        u1 = _shift_rows(u, 1, rows)
        u2 = _shift_rows(u, 2, rows)
        cwv = _conv_taps(cw_ref, got_ref)
        y = cwv[2:3] * u + cwv[1:2] * u1 + cwv[0:1] * u2
        dyc = dzv * c
        du = cwv[2:3] * dyc + cwv[1:2] * _shift_rows(dyc, -1, rows) + cwv[0:1] * _shift_rows(dyc, -2, rows)
        o_ref[0] = (du * xv).astype(BF16)
        o_ref[1] = (dzv * y).astype(BF16)
        o_ref[2] = (du * b).astype(BF16)
        s0 = jnp.sum(dyc * u2, axis=0, keepdims=True)
        s1 = jnp.sum(dyc * u1, axis=0, keepdims=True)
        s2 = jnp.sum(dyc * u, axis=0, keepdims=True)
        tap = lax.broadcasted_iota(jnp.int32, (3, cb), 0)
        dcw_ref[...] += jnp.where(tap == 0, s0, jnp.where(tap == 1, s1, s2))

    return pl.pallas_call(
        body, name="conv_bwd", grid=(nj, nseq),
        in_specs=[pl.BlockSpec((seq, cb), lambda j, b: (b, j)),
                  pl.BlockSpec((seq, cb), lambda j, b: (b, j)),
                  pl.BlockSpec((seq, cb), lambda j, b: (b, nj + j)),
                  pl.BlockSpec((seq, cb), lambda j, b: (b, 2 * nj + j)),
                  pl.BlockSpec((SUBLANES, cb), lambda j, b: (0, j)),
                  pl.BlockSpec((3, SUBLANES, cb), lambda j, b: (0, 0, j))],
        out_specs=[pl.BlockSpec((3, seq, cb), lambda j, b: (0, b, j)),
                   pl.BlockSpec((3, cb), lambda j, b: (0, j))],
        out_shape=[_sds((3, t, d), BF16), _sds((3, d), F32)],
        compiler_params=_params(("parallel", "arbitrary")))(dz, bcx, bcx, bcx, cw, cw_got)


def _pair_norm(x, gain_pair, low):
    sq = x * x
    ss_lo = jnp.sum(jnp.where(low, sq, 0.0), axis=-1, keepdims=True)
    ss_hi = jnp.sum(jnp.where(low, 0.0, sq), axis=-1, keepdims=True)
    r = lax.rsqrt(jnp.where(low, ss_lo, ss_hi) * (1.0 / HEAD_DIM) + EPS)
    xhat = x * r
    return xhat * gain_pair, xhat, r


KEYS = 2 * BLOCK
QK_SCALE = 1.0 / (HEAD_DIM ** 0.5)
N_PAIRS = N_Q_HEADS // 2


def _earlier_block(shape=(BLOCK, BLOCK)):
    return lax.broadcasted_iota(jnp.int32, shape, 0) > lax.broadcasted_iota(jnp.int32, shape, 1)


def _fill_bias(bias_ref):
    rows = lax.broadcasted_iota(jnp.int32, (2 * BLOCK, BLOCK), 0)
    qi = lax.broadcasted_iota(jnp.int32, (2 * BLOCK, BLOCK), 1)
    odd_head = rows >= BLOCK
    kj = jnp.where(odd_head, rows - BLOCK, rows)
    earlier = kj > qi
    dist = (jnp.where(earlier, BLOCK, 0) + qi - kj).astype(F32)
    for j in range(N_PAIRS):
        slope = jnp.where(odd_head, ALIBI_SLOPES[2 * j + 1], ALIBI_SLOPES[2 * j])
        bias = -slope * dist
        bias_ref[1, j] = bias
        bias_ref[0, j] = jnp.where(earlier, -1e30, bias)


def _merge_blocks(x_t, earlier):
    return jnp.concatenate([jnp.where(earlier, x_t[e * KEYS:e * KEYS + BLOCK], x_t[e * KEYS + BLOCK:(e + 1) * KEYS])
                            for e in range(2)], axis=0)


def _split_blocks(heads, earlier):
    parts = []
    for x in heads:
        parts += [jnp.where(earlier, x, 0.0), jnp.where(earlier, 0.0, x)]
    return jnp.concatenate(parts, axis=0).astype(BF16)


def _kv_pair_rows(kv_tile, parity, low):
    own = jnp.where(low if parity == 0 else jnp.logical_not(low), kv_tile, 0.0)
    other = pltpu.roll(own, HEAD_DIM, 1)
    lo, hi = (own, other) if parity == 0 else (other, own)
    return jnp.concatenate([lo, hi], axis=0).astype(BF16)


def _pair_softmax(s_t, sink_even, sink_odd):
    out = []
    for e, sink in enumerate((sink_even, sink_odd)):
        se = s_t[e * BLOCK:(e + 1) * BLOCK]
        m = jnp.maximum(jnp.max(se, axis=0, keepdims=True), sink)
        ee = jnp.exp(se - m)
        es = jnp.exp(sink - m)
        inv = 1.0 / (jnp.sum(ee, axis=0, keepdims=True) + es)
        out.append((ee * inv, es * inv))
    return out


def _attn_rows(n):
    q0 = pl.multiple_of(n * BLOCK, BLOCK)
    k0 = pl.multiple_of(jnp.maximum(n - 1, 0) * BLOCK, BLOCK)
    return q0, k0, jnp.minimum(n, 1)


def _key_rows(qkv_ref, k0, q0, col):
    return jnp.concatenate([qkv_ref[pl.ds(k0, BLOCK), col:col + LANES], qkv_ref[pl.ds(q0, BLOCK), col:col + LANES]],
                           axis=0).astype(F32)


def _attn_fwd(qkv, qg_pair, kg_pair, sinks, nseq, seq):
    t = qkv.shape[0]
    dq = N_Q_HEADS * HEAD_DIM
    dkv = N_KV_HEADS * HEAD_DIM

    def body(sk_ref, qkv_ref, qg_ref, kg_ref, o_ref, bias_ref):
        @pl.when(pl.program_id(0) == 0)
        def _():
            _fill_bias(bias_ref)

        low = lax.broadcasted_iota(jnp.int32, (1, LANES), 1) < HEAD_DIM
        earlier = _earlier_block()
        qg = qg_ref[...] * QK_SCALE
        kg = kg_ref[...]

        def blk(n, carry):
            q0, k0, later = _attn_rows(n)
            for kt in range(dkv // LANES):
                kraw = _key_rows(qkv_ref, k0, q0, dq + kt * LANES)
                vraw = _key_rows(qkv_ref, k0, q0, dq + dkv + kt * LANES)
                kn, _, _ = _pair_norm(kraw, kg, low)
                for par in range(2):
                    kh = 2 * kt + par
                    k_pair = _kv_pair_rows(kn, par, low)
                    v_pair = _kv_pair_rows(vraw, par, low)
                    for jj in range(2):
                        j = 2 * kh + jj
                        qraw = qkv_ref[pl.ds(q0, BLOCK), j * LANES:(j + 1) * LANES].astype(F32)
                        qn, _, _ = _pair_norm(qraw, qg, low)
                        s_t = _merge_blocks(_dot(k_pair, qn.astype(BF16), NT), earlier) + bias_ref[later, j]
                        (p0, _), (p1, _) = _pair_softmax(s_t, sk_ref[0, 2 * j], sk_ref[0, 2 * j + 1])
                        p_t = _split_blocks((p0, p1), earlier)
                        o_ref[pl.ds(q0, BLOCK), j * LANES:(j + 1) * LANES] = _dot(p_t, v_pair, TN).astype(BF16)
            return carry

        lax.fori_loop(0, seq // BLOCK, blk, 0)

    return pl.pallas_call(
        body, name="attn_fwd", grid=(nseq,),
        in_specs=[pl.BlockSpec(memory_space=pltpu.SMEM),
                  pl.BlockSpec((seq, dq + 2 * dkv), lambda b: (b, 0)),
                  pl.BlockSpec((1, LANES), lambda b: (0, 0)),
                  pl.BlockSpec((1, LANES), lambda b: (0, 0))],
        out_specs=pl.BlockSpec((seq, dq), lambda b: (b, 0)),
        out_shape=_sds((t, dq), BF16),
        scratch_shapes=[pltpu.VMEM((2, N_PAIRS, 2 * BLOCK, BLOCK), F32)],
        compiler_params=_params(("arbitrary",)))(sinks, qkv, qg_pair, kg_pair)


def _attn_bwd(do, qkv, qg_pair, kg_pair, sinks, nseq, seq):
    t = qkv.shape[0]
    dq = N_Q_HEADS * HEAD_DIM
    dkv = N_KV_HEADS * HEAD_DIM

    def body(sk_ref, do_ref, qkv_ref, qg_ref, kg_ref, o_ref, dqg_ref, dkg_ref, dsk_ref, acc_ref, bias_ref):
        @pl.when(pl.program_id(0) == 0)
        def _():
            _fill_bias(bias_ref)
            dqg_ref[...] = jnp.zeros_like(dqg_ref)
            dkg_ref[...] = jnp.zeros_like(dkg_ref)
            dsk_ref[...] = jnp.zeros_like(dsk_ref)

        acc_ref[...] = jnp.zeros_like(acc_ref)
        low = lax.broadcasted_iota(jnp.int32, (1, LANES), 1) < HEAD_DIM
        earlier = _earlier_block()
        head_row = lax.broadcasted_iota(jnp.int32, (N_Q_HEADS, LANES), 0)
        qg = qg_ref[...] * QK_SCALE
        kg = kg_ref[...]

        def blk(n, carry):
            dqg_acc, dkg_acc, dsk_acc = carry
            q0, k0, later = _attn_rows(n)
            for kt in range(dkv // LANES):
                kraw = _key_rows(qkv_ref, k0, q0, dq + kt * LANES)
                vraw = _key_rows(qkv_ref, k0, q0, dq + dkv + kt * LANES)
                kn, khat, rk = _pair_norm(kraw, kg, low)
                dk_tile = None
                dv_tile = None
                for par in range(2):
                    kh = 2 * kt + par
                    own = low if par == 0 else jnp.logical_not(low)
                    k_pair = _kv_pair_rows(kn, par, low)
                    v_pair = _kv_pair_rows(vraw, par, low)
                    dkn_rows = jnp.zeros((2 * KEYS, LANES), F32)
                    dv_rows = jnp.zeros((2 * KEYS, LANES), F32)
                    for jj in range(2):
                        j = 2 * kh + jj
                        qraw = qkv_ref[pl.ds(q0, BLOCK), j * LANES:(j + 1) * LANES].astype(F32)
                        qn, qhat, rq = _pair_norm(qraw, qg, low)
                        qn_b = qn.astype(BF16)
                        do_b = do_ref[pl.ds(q0, BLOCK), j * LANES:(j + 1) * LANES]
                        s_t = _merge_blocks(_dot(k_pair, qn_b, NT), earlier) + bias_ref[later, j]
                        dp_t = _merge_blocks(_dot(v_pair, do_b, NT), earlier)
                        ds_heads = []
                        probs = _pair_softmax(s_t, sk_ref[0, 2 * j], sk_ref[0, 2 * j + 1])
                        for e, (p, ps) in enumerate(probs):
                            dp = dp_t[e * BLOCK:(e + 1) * BLOCK]
                            dsum = jnp.sum(p * dp, axis=0, keepdims=True)
                            ds_heads.append(p * (dp - dsum))
                            dsk_acc = dsk_acc - jnp.where(head_row == 2 * j + e, ps * dsum, 0.0)
                        p_t = _split_blocks((probs[0][0], probs[1][0]), earlier)
                        ds_t = _split_blocks(ds_heads, earlier)
                        dv_rows = dv_rows + _dot(p_t, do_b, NN)
                        dkn_rows = dkn_rows + _dot(ds_t, qn_b, NN)
                        dqn = _dot(ds_t, k_pair, TN)
                        dqg_acc = dqg_acc + jnp.sum(dqn * qhat, axis=0, keepdims=True)
                        dqhat = dqn * qg
                        prod = dqhat * qhat
                        m_lo = jnp.sum(jnp.where(low, prod, 0.0), axis=-1, keepdims=True)
                        m_hi = jnp.sum(jnp.where(low, 0.0, prod), axis=-1, keepdims=True)
                        mean = jnp.where(low, m_lo, m_hi) * (1.0 / HEAD_DIM)
                        o_ref[pl.ds(q0, BLOCK), j * LANES:(j + 1) * LANES] = (rq * (dqhat - qhat * mean)).astype(BF16)
                    dkn_acc = jnp.where(low, dkn_rows[0:KEYS], dkn_rows[KEYS:2 * KEYS])
                    dv_acc = jnp.where(low, dv_rows[0:KEYS], dv_rows[KEYS:2 * KEYS])
                    dkn = dkn_acc + pltpu.roll(dkn_acc, HEAD_DIM, 1)
                    dvh = dv_acc + pltpu.roll(dv_acc, HEAD_DIM, 1)
                    khat_own = jnp.where(own, khat, 0.0)
                    khat_dup = khat_own + pltpu.roll(khat_own, HEAD_DIM, 1)
                    dkg_acc = dkg_acc + jnp.sum(jnp.where(own, dkn * khat_dup, 0.0), axis=0, keepdims=True)
                    dkhat = dkn * kg
                    mean_k = jnp.sum(dkhat * khat_dup, axis=-1, keepdims=True) * (1.0 / LANES)
                    dk_raw = rk * (dkhat - khat_dup * mean_k)
                    dk_tile = jnp.where(own, dk_raw, 0.0) if dk_tile is None else jnp.where(own, dk_raw, dk_tile)
                    dv_tile = jnp.where(own, dvh, 0.0) if dv_tile is None else jnp.where(own, dvh, dv_tile)
                for r0, part in ((k0, slice(0, BLOCK)), (q0, slice(BLOCK, KEYS))):
                    acc_ref[pl.ds(r0, BLOCK), kt * LANES:(kt + 1) * LANES] += dk_tile[part]
                    acc_ref[pl.ds(r0, BLOCK), dkv + kt * LANES:dkv + (kt + 1) * LANES] += dv_tile[part]
            return dqg_acc, dkg_acc, dsk_acc

        zero = jnp.zeros((1, LANES), F32)
        carry = (zero, zero, jnp.zeros((N_Q_HEADS, LANES), F32))
        dqg_acc, dkg_acc, dsk_acc = lax.fori_loop(0, seq // BLOCK, blk, carry)
        dqg_ref[...] += dqg_acc * QK_SCALE
        dkg_ref[...] += dkg_acc
        dsk_ref[...] += dsk_acc
        o_ref[:, dq:dq + 2 * dkv] = acc_ref[...].astype(BF16)

    small = pl.BlockSpec((1, LANES), lambda b: (0, 0))
    heads = pl.BlockSpec((N_Q_HEADS, LANES), lambda b: (0, 0))
    return pl.pallas_call(
        body, name="attn_bwd", grid=(nseq,),
        in_specs=[pl.BlockSpec(memory_space=pltpu.SMEM),
                  pl.BlockSpec((seq, dq), lambda b: (b, 0)),
                  pl.BlockSpec((seq, dq + 2 * dkv), lambda b: (b, 0)),
                  small, small],
        out_specs=[pl.BlockSpec((seq, dq + 2 * dkv), lambda b: (b, 0)), small, small, heads],
        out_shape=[_sds((t, dq + 2 * dkv), BF16), _sds((1, LANES), F32), _sds((1, LANES), F32),
                   _sds((N_Q_HEADS, LANES), F32)],
        scratch_shapes=[pltpu.VMEM((seq, 2 * dkv), F32), pltpu.VMEM((2, N_PAIRS, 2 * BLOCK, BLOCK), F32)],
        compiler_params=_params(("arbitrary",)))(sinks, do, qkv, qg_pair, kg_pair)


def _place():
    x, y, c = lax.axis_index("x"), lax.axis_index("y"), lax.axis_index("c")
    other_chips = [(1 - x, y), (x, 1 - y), (1 - x, 1 - y)]
    return x, y, c, other_chips


def _half_rows(c, rows):
    rh = rows // 2
    return pl.ds(pl.multiple_of(c * rh, BF16_ROWS), rh)


def _cast_own(name, w, place, layer=None):
    nl, r, cdim = w.shape
    first = 0
    if layer is not None:
        nl, first = 1, layer
    rt = _row_tile(r, 4 * cdim, ELEMENTWISE_BLOCK)

    def body(s_ref, w_ref, o_ref):
        o_ref[...] = w_ref[...].astype(BF16)

    grid_spec = pltpu.PrefetchScalarGridSpec(
        num_scalar_prefetch=1, grid=(nl, r // rt),
        in_specs=[pl.BlockSpec((None, rt, cdim), lambda l, i, s: (first + l, i, 0))],
        out_specs=pl.BlockSpec((None, None, rt, cdim), lambda l, i, s: (l, s[1], i, 0)))
    return pl.pallas_call(
        body, name=name, grid_spec=grid_spec, out_shape=_sds((nl, N_CHIPS, r, cdim), BF16),
        compiler_params=_params(("parallel", "parallel")))(place, w)


def _gather_protocol(outs, shapes, send_sems, recv_sems):
    n = len(outs)
    x, y, c, other_chips = _place()
    me_chip = 2 * x + y
    sibling = (x, y, 1 - c)

    def rows(u, chip, half):
        return outs[u].at[:, chip, _half_rows(half, shapes[u][2]), :]

    def copy(sem, part, to):
        return pltpu.make_async_remote_copy(src_ref=part, dst_ref=part, send_sem=send_sems.at[sem],
                                            recv_sem=recv_sems.at[sem], device_id=to, device_id_type=MESH)

    sends = []
    for u in range(n):
        for k, chip in enumerate(other_chips):
            cp = copy(6 * u + k, rows(u, me_chip, c), (*chip, c))
            cp.start()
            sends.append(cp)
    for u in range(n):
        for k, chip in enumerate(other_chips):
            got = rows(u, 2 * chip[0] + chip[1], c)
            copy(6 * u + k, got, (*chip, c)).wait_recv()
            cp = copy(6 * u + 3 + k, got, sibling)
            cp.start()
            sends.append(cp)
    for u in range(n):
        for k, chip in enumerate(other_chips):
            copy(6 * u + 3 + k, rows(u, 2 * chip[0] + chip[1], 1 - c), sibling).wait_recv()
    for cp in sends:
        cp.wait_send()


def _hbm_ref(a):
    return jax.new_ref(a, memory_space=pltpu.MemorySpace.HBM)


def _sibling_peer():
    x, y, c, _ = _place()
    return [(x, y, 1 - c)]


def _chip_peers():
    x, y, c, other_chips = _place()
    return [(*chip, c) for chip in other_chips]


def _gather_peers():
    return _chip_peers() + _sibling_peer()


def _on_sequencer(name, collective_id, n_sems, peers, protocol, operands=(), out_types=()):
    n_in, n_out = len(operands), len(out_types)

    def launch(*refs):
        send_sems, recv_sems = refs[n_in + n_out:]
        barrier = pltpu.get_barrier_semaphore()
        targets = peers()
        for peer in targets:
            pl.semaphore_signal(barrier, inc=1, device_id=peer, device_id_type=MESH)
        pl.semaphore_wait(barrier, len(targets))
        protocol(refs[:n_in], refs[n_in:n_in + n_out], send_sems, recv_sems)

    return pl.kernel(
        launch, out_type=tuple(out_types), mesh=plsc.ScalarSubcoreMesh(axis_name="sequencer", num_cores=1), name=name,
        scratch_types=(pltpu.SemaphoreType.DMA((n_sems,)), pltpu.SemaphoreType.DMA((n_sems,))),
        compiler_params=pltpu.CompilerParams(collective_id=collective_id))(*operands)


def _seq_allgather(name, collective_id, bufs):
    shapes = [b.shape for b in bufs]
    refs = [_hbm_ref(b) for b in bufs]
    _on_sequencer(name, collective_id, 6 * len(bufs), _gather_peers,
                  lambda ins, outs, send_sems, recv_sems: _gather_protocol(refs, shapes, send_sems, recv_sems))
    return [r[...] for r in refs]


def _taps_protocol(block_ref, got_ref, send_sems, recv_sems, first_sem):
    x, y, c, other_chips = _place()
    copies = []
    for k, chip in enumerate(other_chips):
        cp = pltpu.make_async_remote_copy(src_ref=block_ref, dst_ref=got_ref.at[k], send_sem=send_sems.at[first_sem + k],
                                          recv_sem=recv_sems.at[first_sem + k], device_id=(*chip, c), device_id_type=MESH)
        cp.start()
        copies.append(cp)
    return copies


def _seq_allgather_conv(collective_id, bufs, cw_block):
    shapes = [b.shape for b in bufs]
    refs = [_hbm_ref(b) for b in bufs]

    def protocol(ins, outs, send_sems, recv_sems):
        taps = _taps_protocol(ins[0], outs[0], send_sems, recv_sems, 6 * len(bufs))
        _gather_protocol(refs, shapes, send_sems, recv_sems)
        for cp in taps:
            cp.wait_recv()
        for cp in taps:
            cp.wait_send()

    (got,) = _on_sequencer("allgather_conv", collective_id, 6 * len(bufs) + 3, _gather_peers, protocol,
                           operands=(cw_block,), out_types=(_sds((3, *cw_block.shape), F32),))
    return [r[...] for r in refs], got


def _exchange_protocol(gs, outs, shapes, send_sems, recv_sems):
    x, y, c, _ = _place()
    sends = []
    for u in range(len(gs)):
        cp = pltpu.make_async_remote_copy(
            src_ref=gs[u].at[:, _half_rows(1 - c, shapes[u][1]), :], dst_ref=outs[u],
            send_sem=send_sems.at[u], recv_sem=recv_sems.at[u], device_id=(x, y, 1 - c), device_id_type=MESH)
        cp.start()
        sends.append(cp)
    for cp in sends:
        cp.wait_recv()
    for cp in sends:
        cp.wait_send()


def _seq_exchange(name, collective_id, grads):
    shapes = [g.shape for g in grads]
    return _on_sequencer(
        name, collective_id, len(grads), _sibling_peer,
        lambda gs, outs, send_sems, recv_sems: _exchange_protocol(gs, outs, shapes, send_sems, recv_sems),
        operands=grads, out_types=[_sds((s[0], s[1] // 2, s[2]), F32) for s in shapes])


def _sum_halves(name, g, got, place, after):
    _, r, cdim = g.shape
    rh = r // 2
    rt = _row_tile(rh, 4 * N_CHIPS * cdim, 2 * ELEMENTWISE_BLOCK)
    nr = rh // rt

    def body(s_ref, g_ref, got_ref, after_ref, pb_ref, pf_ref):
        pb_ref[...] = (g_ref[...] + got_ref[...]).astype(BF16)
        mine = s_ref[1]
        pf_ref[...] = g_ref[mine] + got_ref[mine]

    quarters = (N_CHIPS, rt, cdim)
    grid_spec = pltpu.PrefetchScalarGridSpec(
        num_scalar_prefetch=1, grid=(nr,),
        in_specs=[pl.BlockSpec(quarters, lambda i, s: (0, s[0] * nr + i, 0)),
                  pl.BlockSpec(quarters, lambda i, s: (0, i, 0)),
                  pl.BlockSpec(memory_space=pl.ANY)],
        out_specs=[pl.BlockSpec(quarters, lambda i, s: (0, i, 0)),
                   pl.BlockSpec((rt, cdim), lambda i, s: (i, 0))])
    return pl.pallas_call(
        body, name=name, grid_spec=grid_spec,
        out_shape=[_sds((N_CHIPS, rh, cdim), BF16), _sds((rh, cdim), F32)],
        compiler_params=_params(("parallel",)))(place, g, got, after)


def _scatter_protocol(ps, outs, send_sems, recv_sems):
    x, y, c, other_chips = _place()
    sends = []
    for u in range(len(ps)):
        for k, chip in enumerate(other_chips):
            cp = pltpu.make_async_remote_copy(
                src_ref=ps[u].at[2 * chip[0] + chip[1]], dst_ref=outs[u].at[k],
                send_sem=send_sems.at[3 * u + k], recv_sem=recv_sems.at[3 * u + k],
                device_id=(*chip, c), device_id_type=MESH)
            cp.start()
            sends.append(cp)
    for cp in sends:
        cp.wait_recv()
    for cp in sends:
        cp.wait_send()


def _seq_scatter(name, collective_id, partials):
    return _on_sequencer(
        name, collective_id, 3 * len(partials), _chip_peers, _scatter_protocol,
        operands=partials, out_types=[_sds((3, p.shape[1], p.shape[2]), BF16) for p in partials])


def _sum_partials(name, own, got, place, layer, nl, prev, after):
    rh, cdim = own.shape
    rt = _row_tile(rh, 4 * cdim, ELEMENTWISE_BLOCK)
    nr = rh // rt

    def body(s_ref, own_ref, got_ref, *rest):
        o_ref = rest[-1]
        o_ref[...] = ((own_ref[...] + got_ref[0].astype(F32)) + got_ref[1].astype(F32)) + got_ref[2].astype(F32)

    in_specs = [pl.BlockSpec((rt, cdim), lambda i, s: (i, 0)), pl.BlockSpec((3, rt, cdim), lambda i, s: (0, i, 0)),
                pl.BlockSpec(memory_space=pl.ANY)]
    args = [place, own, got, after]
    aliases = {}
    if prev is not None:
        in_specs.append(pl.BlockSpec(memory_space=pl.ANY))
        args.append(prev)
        aliases = {4: 0}
    grid_spec = pltpu.PrefetchScalarGridSpec(
        num_scalar_prefetch=1, grid=(nr,), in_specs=in_specs,
        out_specs=pl.BlockSpec((None, rt, cdim), lambda i, s: (layer, s[0] * nr + i, 0)))
    return pl.pallas_call(
        body, name=name, grid_spec=grid_spec, out_shape=_sds((nl, 2 * rh, cdim), F32),
        input_output_aliases=aliases, compiler_params=_params(("parallel",)))(*args)


def _share_protocol(outs, shapes, units, send_sems, recv_sems):
    x, y, c, _ = _place()
    sends = []
    for u, (w, l) in enumerate(units):
        mine = outs[w].at[l, _half_rows(c, shapes[w][1]), :]
        cp = pltpu.make_async_remote_copy(src_ref=mine, dst_ref=mine, send_sem=send_sems.at[u],
                                          recv_sem=recv_sems.at[u], device_id=(x, y, 1 - c), device_id_type=MESH)
        cp.start()
        sends.append(cp)
    for u, (w, l) in enumerate(units):
        theirs = outs[w].at[l, _half_rows(1 - c, shapes[w][1]), :]
        pltpu.make_async_remote_copy(src_ref=theirs, dst_ref=theirs, send_sem=send_sems.at[u],
                                     recv_sem=recv_sems.at[u], device_id=(x, y, 1 - c),
                                     device_id_type=MESH).wait_recv()
    for cp in sends:
        cp.wait_send()


def _seq_share(name, collective_id, bufs):
    shapes = [b.shape for b in bufs]
    units = [(w, l) for w in range(len(bufs)) for l in range(shapes[w][0])]
    refs = [_hbm_ref(b) for b in bufs]
    _on_sequencer(name, collective_id, len(units), _sibling_peer,
                  lambda ins, outs, send_sems, recv_sems: _share_protocol(refs, shapes, units, send_sems, recv_sems))
    return [r[...] for r in refs]


def _gather_blocks(block_ref, all_ref, send_sems, recv_sems):
    x, y, c, _ = _place()
    me = 4 * x + 2 * y + c
    all_ref[me] = block_ref[...]
    sends = []
    for rel in range(1, 8):
        fx, fy, fc = (rel >> 2) & 1, (rel >> 1) & 1, rel & 1
        peer = (x ^ fx, y ^ fy, c ^ fc)
        cp = pltpu.make_async_remote_copy(src_ref=block_ref, dst_ref=all_ref.at[me], send_sem=send_sems.at[rel - 1],
                                          recv_sem=recv_sems.at[rel - 1], device_id=peer, device_id_type=MESH)
        cp.start()
        sends.append(cp)
    for cp in sends:
        cp.wait_recv()
    for cp in sends:
        cp.wait_send()


def _adam(w, g, m, v):
    m_new = ADAM_B1 * m + (1.0 - ADAM_B1) * g
    v_new = ADAM_B2 * v + (1.0 - ADAM_B2) * (g * g)
    m_hat = m_new / (1.0 - ADAM_B1 ** ADAM_STEP)
    v_hat = v_new / (1.0 - ADAM_B2 ** ADAM_STEP)
    delta = -ADAM_LR * (m_hat / (jnp.sqrt(v_hat) + ADAM_EPS) + ADAM_WD * w)
    return delta, m_new, v_new


def _small_step(dnm0, dnm1, dnf0, dnf1, dcw, dqg, dkg, dsk, loss, w_blk, m_blk, v_blk, cw_cols):
    d = w_blk.shape[1]
    vm = pl.BlockSpec(memory_space=pltpu.VMEM)

    def reduce_body(dnm0_ref, dnm1_ref, dnf0_ref, dnf1_ref, dcw_ref, dqg_ref, dkg_ref, dsk_ref, loss_ref,
                    g_ref, blk_ref, all_ref, send_sems, recv_sems):
        blk_ref[...] = jnp.zeros_like(blk_ref)
        for row, part_ref in ((SENT_NORM_MIXER, dnm0_ref), (SENT_NORM_MIXER + 1, dnm1_ref),
                              (SENT_NORM_FFN, dnf0_ref), (SENT_NORM_FFN + 1, dnf1_ref)):
            blk_ref[row:row + 1, :] = jnp.sum(part_ref[...], axis=0, keepdims=True)
        blk_ref[SENT_CONV_W:SENT_CONV_W + 3, :] = dcw_ref[...]
        misc = slice(SENT_MISC, SENT_MISC + 1)
        for tile, gain_ref in ((TILE_Q_GAIN, dqg_ref), (TILE_K_GAIN, dkg_ref)):
            pair = gain_ref[...]
            blk_ref[misc, tile * LANES:(tile + 1) * LANES] = pair + pltpu.roll(pair, HEAD_DIM, 1)
        for h in range(N_Q_HEADS):
            lane = TILE_SINKS * LANES + h
            blk_ref[misc, lane:lane + 1] = jnp.sum(dsk_ref[h:h + 1, :], axis=1, keepdims=True)
        blk_ref[misc, TILE_LOSS * LANES:(TILE_LOSS + 1) * LANES] = jnp.broadcast_to(loss_ref[...], (1, LANES))
        _gather_blocks(blk_ref, all_ref, send_sems, recv_sems)
        g = all_ref[0]
        for dev in range(1, 8):
            g = g + all_ref[dev]
        g_ref[...] = jnp.zeros_like(g_ref)
        for sent, row, n in ((SENT_NORM_MIXER, ROW_NORM_MIXER, 2), (SENT_NORM_FFN, ROW_NORM_FFN, 2),
                             (SENT_CONV_W, ROW_CONV_W, 3), (SENT_MISC, ROW_MISC, 1)):
            g_ref[row:row + n, :] = g[sent:sent + n]

    g_blk = pl.pallas_call(
        reduce_body, name="small_allreduce", in_specs=[vm] * 9, out_specs=vm, out_shape=_sds((SMALL_ROWS, d), F32),
        scratch_shapes=[pltpu.VMEM((SUBLANES, d), F32), pltpu.VMEM((8, SUBLANES, d), F32),
                        pltpu.SemaphoreType.DMA((7,)), pltpu.SemaphoreType.DMA((7,))],
    )(dnm0, dnm1, dnf0, dnf1, dcw, dqg, dkg, dsk, loss)

    def body(g_ref, w_ref, m_ref, v_ref, *out_refs):
        g = g_ref[...]
        misc = slice(ROW_MISC, ROW_MISC + 1)
        out_refs[0][...] = g[misc, TILE_LOSS * LANES:TILE_LOSS * LANES + 1]
        chip = 2 * lax.axis_index("x") + lax.axis_index("y")
        for i, blk in enumerate((g, *_adam(w_ref[...], g, m_ref[...], v_ref[...]))):
            nm_ref, nf_ref, cw_ref, qg_ref, kg_ref, sk_ref = out_refs[1 + 6 * i:7 + 6 * i]
            nm_ref[...] = blk[ROW_NORM_MIXER:ROW_NORM_MIXER + 2]
            nf_ref[...] = blk[ROW_NORM_FFN:ROW_NORM_FFN + 2]
            qg_ref[...] = blk[misc, TILE_Q_GAIN * LANES:TILE_Q_GAIN * LANES + HEAD_DIM]
            kg_ref[...] = blk[misc, TILE_K_GAIN * LANES:TILE_K_GAIN * LANES + HEAD_DIM]
            sk_ref[...] = blk[misc, TILE_SINKS * LANES:TILE_SINKS * LANES + N_Q_HEADS]
            for q in range(N_CHIPS):
                @pl.when(chip == q)
                def _(blk=blk, cw_ref=cw_ref, q=q):
                    cw_ref[0] = blk[ROW_CONV_W:ROW_CONV_W + 3, q * cw_cols:(q + 1) * cw_cols]

    group = [_sds((2, d), F32), _sds((2, d), F32), _sds((1, 3, cw_cols), F32), _sds((1, HEAD_DIM), F32),
             _sds((1, HEAD_DIM), F32), _sds((1, N_Q_HEADS), F32)]
    outs = pl.pallas_call(
        body, name="small_adam", in_specs=[vm] * 4, out_specs=[vm] * 25, out_shape=[_sds((1, 1), F32)] + group * 4,
    )(g_blk, w_blk, m_blk, v_blk)
    names = ("norm_mixer", "norm_ffn", "conv_w", "attn_q_gain", "attn_k_gain", "attn_sinks")
    return outs[0], [dict(zip(names, outs[1 + 6 * i:7 + 6 * i])) for i in range(4)]


def _adam_step(name, w, g, m, v):
    nl, r, cdim = w.shape
    rt = _row_tile(r, 4 * cdim, ELEMENTWISE_BLOCK)

    def body(w_ref, g_ref, m_ref, v_ref, go_ref, d_ref, mo_ref, vo_ref):
        gv = g_ref[...]
        go_ref[...] = gv
        delta, m_new, v_new = _adam(w_ref[...], gv, m_ref[...], v_ref[...])
        d_ref[...] = delta
        mo_ref[...] = m_new
        vo_ref[...] = v_new

    spec = pl.BlockSpec((None, rt, cdim), lambda l, i: (l, i, 0))
    return pl.pallas_call(
        body, name=name, grid=(nl, r // rt), in_specs=[spec] * 4, out_specs=[spec] * 4,
        out_shape=[_sds(w.shape, F32)] * 4,
        compiler_params=_params(("parallel", "parallel")))(w, g, m, v)


def _pad_rows(a, rows=SUBLANES):
    return jnp.pad(a, ((0, rows - a.shape[0]), (0, 0)))


def _small_block(nm, nf, cw_local, qg, kg, sk, chip):
    d = nm.shape[1]
    cw_rows = lax.dynamic_update_slice(jnp.zeros((SUBLANES, d), F32), cw_local, (0, chip * cw_local.shape[1]))
    misc = jnp.concatenate([qg, qg, kg, kg, jnp.pad(sk, ((0, 0), (0, LANES - sk.shape[1]))),
                            jnp.zeros((1, d - 3 * LANES), F32)], axis=1)
    return jnp.concatenate([_pad_rows(nm), _pad_rows(nf), cw_rows, _pad_rows(misc)], axis=0)


WEIGHT_NAMES = ("conv_w_in", "conv_w", "conv_w_out", "attn_w_qkv", "attn_q_gain", "attn_k_gain", "attn_sinks",
                "attn_w_o", "norm_mixer", "norm_ffn", "ffn_w_gate_up", "ffn_w_down")
BIG = ("conv_w_in", "conv_w_out", "attn_w_qkv", "attn_w_o", "ffn_w_gate_up", "ffn_w_down")


def kernel(x, conv_w_in, conv_w, conv_w_out, attn_w_qkv, attn_q_gain, attn_k_gain, attn_sinks, attn_w_o, norm_mixer, norm_ffn, ffn_w_gate_up, ffn_w_down, loss_target, m_conv_w_in, m_conv_w, m_conv_w_out, m_attn_w_qkv, m_attn_q_gain, m_attn_k_gain, m_attn_sinks, m_attn_w_o, m_norm_mixer, m_norm_ffn, m_ffn_w_gate_up, m_ffn_w_down, v_conv_w_in, v_conv_w, v_conv_w_out, v_attn_w_qkv, v_attn_q_gain, v_attn_k_gain, v_attn_sinks, v_attn_w_o, v_norm_mixer, v_norm_ffn, v_ffn_w_gate_up, v_ffn_w_down):
    w = dict(conv_w_in=conv_w_in, conv_w=conv_w, conv_w_out=conv_w_out, attn_w_qkv=attn_w_qkv,
             attn_q_gain=attn_q_gain, attn_k_gain=attn_k_gain, attn_sinks=attn_sinks, attn_w_o=attn_w_o,
             norm_mixer=norm_mixer, norm_ffn=norm_ffn, ffn_w_gate_up=ffn_w_gate_up, ffn_w_down=ffn_w_down)
    m = dict(conv_w_in=m_conv_w_in, conv_w=m_conv_w, conv_w_out=m_conv_w_out, attn_w_qkv=m_attn_w_qkv,
             attn_q_gain=m_attn_q_gain, attn_k_gain=m_attn_k_gain, attn_sinks=m_attn_sinks, attn_w_o=m_attn_w_o,
             norm_mixer=m_norm_mixer, norm_ffn=m_norm_ffn, ffn_w_gate_up=m_ffn_w_gate_up, ffn_w_down=m_ffn_w_down)
    v = dict(conv_w_in=v_conv_w_in, conv_w=v_conv_w, conv_w_out=v_conv_w_out, attn_w_qkv=v_attn_w_qkv,
             attn_q_gain=v_attn_q_gain, attn_k_gain=v_attn_k_gain, attn_sinks=v_attn_sinks, attn_w_o=v_attn_w_o,
             norm_mixer=v_norm_mixer, norm_ffn=v_norm_ffn, ffn_w_gate_up=v_ffn_w_gate_up, ffn_w_down=v_ffn_w_down)

    nseq, seq, d = x.shape
    t = nseq * seq
    chip = 2 * lax.axis_index("x") + lax.axis_index("y")
    core = lax.axis_index("c")
    place = jnp.stack([core, chip]).astype(jnp.int32)
    x0 = x.reshape(t, d)
    tgt = loss_target.reshape(t, d)

    cw_block = lax.dynamic_update_slice(jnp.zeros((SUBLANES, d), F32), conv_w[0], (0, chip * conv_w.shape[2]))
    def cast(k, layer=None):
        return _cast_own(f"cast_{k}" + ("" if layer is None else str(layer)), w[k], place, layer)

    (w_in,), cw_got = _seq_allgather_conv(1, [cast("conv_w_in")], cw_block)
    w_out, w_gu0, w_dn0 = _seq_allgather(
        "allgather_ffn0", 2, [cast("conv_w_out"), cast("ffn_w_gate_up", 0), cast("ffn_w_down", 0)])
    w_qkv, w_o, w_gu1, w_dn1 = _seq_allgather(
        "allgather_rest", 3, [cast("attn_w_qkv"), cast("attn_w_o"), cast("ffn_w_gate_up", 1), cast("ffn_w_down", 1)])
    w_out = w_out.reshape(1, d, d)
    w_o = w_o.reshape(1, d, d)
    w_gu = [w_gu0, w_gu1]
    w_dn = [w_dn0.reshape(1, D_FF, d), w_dn1.reshape(1, D_FF, d)]

    qg_pair = jnp.concatenate([attn_q_gain, attn_q_gain], axis=1)
    kg_pair = jnp.concatenate([attn_k_gain, attn_k_gain], axis=1)

    def ffn_bwd(i, dxo, xin, h, g, u, a):
        g_dn = _wgrad_down(f"ffn{i}_down_wgrad", a, dxo, D_FF // 2)
        dg, du = _mm_down_t_swiglu(f"ffn{i}_down_dgrad", dxo, w_dn[i], 0, g, u)
        g_gu = _wgrad_up2(f"ffn{i}_up_wgrad", h, dg, du)
        dxi, dgain = _dgrad_norm_ffn(f"ffn{i}_up_dgrad", dg, du, w_gu[i], 0, xin, norm_ffn[i:i + 1], dxo)
        return dxi, dgain, g_gu, g_dn

    h0, bcx = _mm_norm_up_joined("conv_in", x0, norm_mixer[0:1], w_in, 512)
    z = _conv_fwd(bcx, cw_block, cw_got, nseq, seq)
    x1, h1 = _mm_down_norm("conv_out", z, w_out, 0, x0, norm_ffn[0:1])
    g0, u0, a0 = _mm_up_swiglu("ffn0_up", h1, w_gu[0], 0)
    x2, h2 = _mm_down_norm("ffn0_down", a0, w_dn[0], 0, x1, norm_mixer[1:2])
    qkv = _mm_up_joined("attn_qkv", h2, w_qkv, 1024)
    o = _attn_fwd(qkv, qg_pair, kg_pair, attn_sinks, nseq, seq)
    x3, h3 = _mm_down_norm("attn_out", o, w_o, 0, x2, norm_ffn[1:2])
    g1, u1, a1 = _mm_up_swiglu("ffn1_up", h3, w_gu[1], 0)
    dy, loss_part = _mm_down_loss("ffn1_down", a1, w_dn[1], 0, x3, tgt)

    finished = {k: None for k in BIG}

    def exchange(tag, cid, units):
        return units, _seq_exchange(f"exchange_{tag}", cid, [g for _, _, g in units])

    def scatter(tag, cid, group, after):
        units, got = group
        sums = [_sum_halves(f"sum_halves_{k}{l}", g, r, place, after) for (k, l, g), r in zip(units, got)]
        return units, sums, _seq_scatter(f"scatter_{tag}", cid, [pb for pb, _ in sums])

    def finish(group, after):
        units, sums, arrived = group
        for (k, l, _), (_, pf), r in zip(units, sums, arrived):
            finished[k] = _sum_partials(f"sum_partials_{k}{l}", pf, r, place, l, w[k].shape[0], finished[k], after)

    dx3, dnf1, g_gu1, g_dn1 = ffn_bwd(1, dy, x3, h3, g1, u1, a1)
    ffn1 = exchange("ffn1", 4, [("ffn_w_down", 1, g_dn1), ("ffn_w_gate_up", 1, g_gu1)])
    g_o = _wgrad_down("attn_out_wgrad", o, dx3, d)
    do = _mm_down_t("attn_out_dgrad", dx3, w_o, 0)
    ffn1 = scatter("ffn1", 8, ffn1, do)
    dqkv, dqg, dkg, dsk = _attn_bwd(do, qkv, qg_pair, kg_pair, attn_sinks, nseq, seq)
    g_qkv = _wgrad_joined("attn_qkv_wgrad", h2, dqkv)
    attn = exchange("attn", 5, [("attn_w_o", 0, g_o), ("attn_w_qkv", 0, g_qkv)])
    dx2, dnm1 = _dgrad_norm_qkv("attn_qkv_dgrad", dqkv, w_qkv, x2, norm_mixer[1:2], dx3)
    finish(ffn1, dx2)
    attn = scatter("attn", 9, attn, dx2)
    dx1, dnf0, g_gu0, g_dn0 = ffn_bwd(0, dx2, x1, h1, g0, u0, a0)
    ffn0 = exchange("ffn0", 6, [("ffn_w_down", 0, g_dn0), ("ffn_w_gate_up", 0, g_gu0)])
    g_out = _wgrad_down("conv_out_wgrad", z, dx1, d)
    dz = _mm_down_t("conv_out_dgrad", dx1, w_out, 0)
    finish(attn, dz)
    ffn0 = scatter("ffn0", 10, ffn0, dz)
    dbcx, dcw = _conv_bwd(dz, bcx, cw_block, cw_got, nseq, seq)
    g_in = _wgrad_conv_in("conv_in_wgrad", h0, dbcx, conv_w_in.shape[2])
    conv = exchange("conv", 7, [("conv_w_out", 0, g_out), ("conv_w_in", 0, g_in)])
    dx0, dnm0 = _dgrad_norm_conv("conv_in_dgrad", dbcx, w_in, x0, norm_mixer[0:1], dx1)
    finish(ffn0, dx0)
    late = ("attn_w_qkv", "attn_w_o", "ffn_w_gate_up", "ffn_w_down")
    grads_late = _seq_share("share_late", 12, [finished[k] for k in late])
    conv = scatter("conv", 11, conv, dx0)

    grad, delta, new_m, new_v = {}, {}, {}, {}

    def adam(k, g):
        grad[k], delta[k], new_m[k], new_v[k] = _adam_step(f"adam_{k}", w[k], g, m[k], v[k])

    for k, g in zip(late, grads_late):
        adam(k, g)

    def blocks(src):
        return _small_block(src["norm_mixer"], src["norm_ffn"], src["conv_w"][0], src["attn_q_gain"],
                            src["attn_k_gain"], src["attn_sinks"], chip)

    loss, small = _small_step(dnm0, dnm1, dnf0, dnf1, dcw, dqg, dkg, dsk, loss_part,
                              blocks(w), blocks(m), blocks(v), conv_w.shape[2])
    for dst, part in zip((grad, delta, new_m, new_v), small):
        dst.update(part)

    done = sum(new_v[k][0, 0:1, 0:1] for k in late) + loss
    finish(conv, done)
    last = ("conv_w_in", "conv_w_out")
    for k, g in zip(last, _seq_share("share_last", 13, [finished[k] for k in last])):
        adam(k, g)

    return (loss.reshape(()), dx0.reshape(nseq, seq, d), *[grad[k] for k in WEIGHT_NAMES], *[delta[k] for k in WEIGHT_NAMES],
            *[new_m[k] for k in WEIGHT_NAMES], *[new_v[k] for k in WEIGHT_NAMES])
```

```python
import jax
import jax.numpy as jnp
from jax import lax
from jax.experimental import pallas as pl
from jax.experimental.pallas import tpu as pltpu
from jax.experimental.pallas import tpu_sc as plsc

F32 = jnp.float32
BF16 = jnp.bfloat16

D_FF = 2816
N_Q_HEADS = 16
N_KV_HEADS = 4
HEAD_DIM = 64
WINDOW = 128
BLOCK = 128
EPS = 1e-6
N_CHIPS = 4
LANES = 128
SUBLANES = 8
BF16_ROWS = 16
MXU_COLS = 256
VMEM_LIMIT = 48 * 1024 * 1024
ADAM_LR, ADAM_B1, ADAM_B2, ADAM_EPS, ADAM_WD, ADAM_STEP = 0.001, 0.9, 0.999, 1e-08, 0.01, 10
ALIBI_SLOPES = tuple(2.0 ** (-8.0 * (h + 1) / N_Q_HEADS) for h in range(N_Q_HEADS))
SMALL_ROWS = 32
ROW_NORM_MIXER, ROW_NORM_FFN, ROW_CONV_W, ROW_MISC = 0, 8, 16, 24
SENT_NORM_MIXER, SENT_NORM_FFN, SENT_CONV_W, SENT_MISC = 0, 2, 4, 7
TILE_Q_GAIN, TILE_K_GAIN, TILE_SINKS, TILE_LOSS = 0, 1, 2, 3
MESH = pl.DeviceIdType.MESH

NN = ((1,), (0,))
NT = ((1,), (1,))
TN = ((0,), (0,))


def _dot(a, b, dims):
    return lax.dot_general(a, b, (dims, ((), ())), preferred_element_type=F32)


def _pick(n, cands):
    for c in cands:
        if n % c == 0:
            return c
    raise ValueError((n, cands))


def _row_tile(rows, row_bytes, cap_bytes):
    fits = [r for r in range(BF16_ROWS, rows + 1, BF16_ROWS) if rows % r == 0 and r * row_bytes <= cap_bytes]
    if not fits:
        raise ValueError((rows, row_bytes, cap_bytes))
    return fits[-1]


ELEMENTWISE_BLOCK = 3 << 19


def _resident(block_shape, index_map):
    return pl.BlockSpec(block_shape, index_map, pipeline_mode=pl.Buffered(1))


def _params(sem):
    return pltpu.CompilerParams(dimension_semantics=sem, vmem_limit_bytes=VMEM_LIMIT)


def _sds(shape, dtype):
    return jax.ShapeDtypeStruct(shape, dtype)


def _rms(xv):
    return lax.rsqrt(jnp.mean(xv * xv, axis=-1, keepdims=True) + EPS)


def _sigmoid(g):
    return 1.0 / (1.0 + jnp.exp(-g))


def _mm_up_joined(name, a, w4, tm_pref):
    t, k = a.shape
    _, _, _, nq = w4.shape
    tm = _pick(t, (tm_pref, 256, 128))

    def body(a_ref, w_ref, o_ref, wcat_ref):
        @pl.when(pl.program_id(0) == 0)
        def _():
            for q in range(N_CHIPS):
                wcat_ref[:, q * nq:(q + 1) * nq] = w_ref[q]

        o_ref[...] = _dot(a_ref[...], wcat_ref[...], NN).astype(BF16)

    return pl.pallas_call(
        body, name=name, grid=(t // tm,),
        in_specs=[pl.BlockSpec((tm, k), lambda i: (i, 0)),
                  pl.BlockSpec((None, N_CHIPS, k, nq), lambda i: (0, 0, 0, 0))],
        out_specs=pl.BlockSpec((tm, N_CHIPS * nq), lambda i: (i, 0)),
        out_shape=_sds((t, N_CHIPS * nq), BF16),
        scratch_shapes=[pltpu.VMEM((k, N_CHIPS * nq), BF16)],
        compiler_params=_params(("arbitrary",)))(a, w4)


def _mm_norm_up_joined(name, x, gain, w4, tm_pref):
    t, k = x.shape
    _, _, _, nq = w4.shape
    tm = _pick(t, (tm_pref, 256, 128))

    def body(x_ref, g_ref, w_ref, h_ref, o_ref, wcat_ref):
        @pl.when(pl.program_id(0) == 0)
        def _():
            for q in range(N_CHIPS):
                wcat_ref[:, q * nq:(q + 1) * nq] = w_ref[q]

        xv = x_ref[...]
        h = ((xv * _rms(xv)) * g_ref[...]).astype(BF16)
        h_ref[...] = h
        o_ref[...] = _dot(h, wcat_ref[...], NN).astype(BF16)

    return pl.pallas_call(
        body, name=name, grid=(t // tm,),
        in_specs=[pl.BlockSpec((tm, k), lambda i: (i, 0)), pl.BlockSpec((1, k), lambda i: (0, 0)),
                  _resident((None, N_CHIPS, k, nq), lambda i: (0, 0, 0, 0))],
        out_specs=[pl.BlockSpec((tm, k), lambda i: (i, 0)), pl.BlockSpec((tm, N_CHIPS * nq), lambda i: (i, 0))],
        out_shape=[_sds((t, k), BF16), _sds((t, N_CHIPS * nq), BF16)],
        scratch_shapes=[pltpu.VMEM((k, N_CHIPS * nq), BF16)],
        compiler_params=_params(("arbitrary",)))(x, gain, w4)


def _mm_up_swiglu(name, h, w4, layer):
    t, k = h.shape
    _, _, _, nq = w4.shape
    tm = _pick(t, (512, 256, 128))

    def body(h_ref, wg_ref, wu_ref, dag_ref, dau_ref, a_ref):
        hv = h_ref[...]
        g = _dot(hv, wg_ref[...], NN)
        u = _dot(hv, wu_ref[...], NN)
        sg = _sigmoid(g)
        silu = g * sg
        a = silu * u
        dag_ref[...] = (a + sg * (u - a)).astype(BF16)
        dau_ref[...] = silu.astype(BF16)
        a_ref[...] = a.astype(BF16)

    half = N_CHIPS // 2
    out = pl.BlockSpec((tm, nq), lambda j, i: (i, j))
    return pl.pallas_call(
        body, name=name, grid=(half, t // tm),
        in_specs=[pl.BlockSpec((tm, k), lambda j, i: (i, 0)),
                  pl.BlockSpec((None, None, k, nq), lambda j, i: (layer, j, 0, 0)),
                  pl.BlockSpec((None, None, k, nq), lambda j, i: (layer, half + j, 0, 0))],
        out_specs=[out, out, out],
        out_shape=[_sds((t, half * nq), BF16)] * 3,
        compiler_params=_params(("parallel", "parallel")))(h, w4, w4)


def _mm_down_norm(name, a, w, layer, res, gain):
    t, kf = a.shape
    _, _, n = w.shape
    tm = _pick(t, (1024, 512, 256, 128))

    def body(a_ref, w_ref, r_ref, g_ref, o_ref, h_ref):
        xo = r_ref[...] + _dot(a_ref[...], w_ref[...], NN)
        o_ref[...] = xo
        h_ref[...] = ((xo * _rms(xo)) * g_ref[...]).astype(BF16)

    row = pl.BlockSpec((tm, n), lambda i: (i, 0))
    return pl.pallas_call(
        body, name=name, grid=(t // tm,),
        in_specs=[pl.BlockSpec((tm, kf), lambda i: (i, 0)),
                  _resident((None, kf, n), lambda i: (layer, 0, 0)),
                  row, pl.BlockSpec((1, n), lambda i: (0, 0))],
        out_specs=[row, row],
        out_shape=[_sds((t, n), F32), _sds((t, n), BF16)],
        compiler_params=_params(("parallel",)))(a, w, res, gain)


def _mm_down_loss(name, a, w, layer, res, tgt):
    t, kf = a.shape
    _, _, n = w.shape
    tm = _pick(t, (1024, 512, 256, 128))
    steps = t // tm

    def body(a_ref, w_ref, r_ref, t_ref, dy_ref, l_ref, acc_ref):
        i = pl.program_id(0)

        @pl.when(i == 0)
        def _():
            acc_ref[...] = jnp.zeros_like(acc_ref)

        e = (r_ref[...] + _dot(a_ref[...], w_ref[...], NN)) - t_ref[...]
        dy_ref[...] = e * (1.0 / n)
        acc_ref[...] += (e * e).reshape(tm // SUBLANES, SUBLANES, n).sum(axis=0)

        @pl.when(i == steps - 1)
        def _():
            l_ref[...] = jnp.sum(acc_ref[...], keepdims=True) * (0.5 / n)

    row = pl.BlockSpec((tm, n), lambda i: (i, 0))
    return pl.pallas_call(
        body, name=name, grid=(steps,),
        in_specs=[pl.BlockSpec((tm, kf), lambda i: (i, 0)),
                  _resident((None, kf, n), lambda i: (layer, 0, 0)), row, row],
        out_specs=[row, pl.BlockSpec((1, 1), lambda i: (0, 0))],
        out_shape=[_sds((t, n), F32), _sds((1, 1), F32)],
        scratch_shapes=[pltpu.VMEM((SUBLANES, n), F32)],
        compiler_params=_params(("arbitrary",)))(a, w, res, tgt)


def _mm_down_t(name, dx, w, layer):
    t, n = dx.shape
    _, kf, _ = w.shape
    tm = _pick(t, (1024, 512, 256, 128))

    def body(a_ref, w_ref, o_ref):
        o_ref[...] = _dot(a_ref[...].astype(BF16), w_ref[...], NT).astype(BF16)

    return pl.pallas_call(
        body, name=name, grid=(t // tm,),
        in_specs=[pl.BlockSpec((tm, n), lambda i: (i, 0)),
                  _resident((None, kf, n), lambda i: (layer, 0, 0))],
        out_specs=pl.BlockSpec((tm, kf), lambda i: (i, 0)),
        out_shape=_sds((t, kf), BF16),
        compiler_params=_params(("parallel",)))(dx, w)


def _mm_down_t_swiglu(name, dx, w, layer, g, u):
    t, n = dx.shape
    f = g.shape[1]
    tm = _pick(t, (512, 256, 128))

    def body(a_ref, w_ref, dag_ref, dau_ref, dg_ref, du_ref):
        da = _dot(a_ref[...].astype(BF16), w_ref[...], NT)
        dg_ref[...] = (da * dag_ref[...].astype(F32)).astype(BF16)
        du_ref[...] = (da * dau_ref[...].astype(F32)).astype(BF16)

    tile = pl.BlockSpec((tm, f), lambda i: (i, 0))
    return pl.pallas_call(
        body, name=name, grid=(t // tm,),
        in_specs=[pl.BlockSpec((tm, n), lambda i: (i, 0)),
                  _resident((None, f, n), lambda i: (layer, 0, 0)), tile, tile],
        out_specs=[tile, tile],
        out_shape=[_sds((t, f), BF16)] * 2,
        compiler_params=_params(("parallel",)))(dx, w, g, u)


def _dgrad_norm(name, acts, act_blocks, pieces, w4, layer, x, gain, dres):
    t, d = x.shape
    _, _, k, nq = w4.shape
    tm = _pick(t, (512, 256, 128))
    n_act = len(acts)

    def body(*refs):
        act_refs = refs[:n_act]
        w_ref, x_ref, g_ref, dr_ref, dx_ref, dg_ref = refs[n_act:]

        @pl.when(pl.program_id(0) == 0)
        def _():
            dg_ref[...] = jnp.zeros_like(dg_ref)

        dh = None
        for a_tile, w_tile in pieces(act_refs, w_ref):
            term = _dot(a_tile, w_tile, NT)
            dh = term if dh is None else dh + term
        xv = x_ref[...]
        r = _rms(xv)
        xhat = xv * r
        gd = dh * g_ref[...]
        dx_ref[...] = dr_ref[...] + r * (gd - xhat * jnp.mean(gd * xhat, axis=-1, keepdims=True))
        dg_ref[...] += (dh * xhat).reshape(tm // SUBLANES, SUBLANES, d).sum(axis=0)

    row = pl.BlockSpec((tm, d), lambda i: (i, 0))
    return pl.pallas_call(
        body, name=name, grid=(t // tm,),
        in_specs=[*act_blocks(tm),
                  _resident((None, N_CHIPS, k, nq), lambda i: (layer, 0, 0, 0)),
                  row, pl.BlockSpec((1, d), lambda i: (0, 0)), row],
        out_specs=[row, pl.BlockSpec((SUBLANES, d), lambda i: (0, 0))],
        out_shape=[_sds((t, d), F32), _sds((SUBLANES, d), F32)],
        compiler_params=_params(("arbitrary",)))(*acts, w4, x, gain, dres)


def _dgrad_norm_ffn(name, dg, du, w4, layer, x, gain, dres):
    nq = w4.shape[3]
    f = dg.shape[1]

    def blocks(tm):
        return [pl.BlockSpec((tm, f), lambda i: (i, 0))] * 2

    def pieces(act_refs, w_ref):
        dg_ref, du_ref = act_refs
        return [(dg_ref[:, 0:nq], w_ref[0]), (dg_ref[:, nq:2 * nq], w_ref[1]),
                (du_ref[:, 0:nq], w_ref[2]), (du_ref[:, nq:2 * nq], w_ref[3])]

    return _dgrad_norm(name, [dg, du], blocks, pieces, w4, layer, x, gain, dres)


def _dgrad_norm_qkv(name, dqkv, w4, x, gain, dres):
    nq = w4.shape[3]

    def blocks(tm):
        return [pl.BlockSpec((tm, N_CHIPS * nq), lambda i: (i, 0))]

    def pieces(act_refs, w_ref):
        return [(act_refs[0][:, q * nq:(q + 1) * nq], w_ref[q]) for q in range(N_CHIPS)]

    return _dgrad_norm(name, [dqkv], blocks, pieces, w4, 0, x, gain, dres)


def _dgrad_norm_conv(name, d3, w4, x, gain, dres):
    _, _, d = d3.shape
    nq = w4.shape[3]
    per_part, per_q = d // MXU_COLS, nq // MXU_COLS

    def blocks(tm):
        return [pl.BlockSpec((3, tm, d), lambda i: (0, i, 0))]

    def pieces(act_refs, w_ref):
        out = []
        for jb in range(3 * per_part):
            ca, cw = (jb % per_part) * MXU_COLS, (jb % per_q) * MXU_COLS
            out.append((act_refs[0][jb // per_part, :, ca:ca + MXU_COLS], w_ref[jb // per_q, :, cw:cw + MXU_COLS]))
        return out

    return _dgrad_norm(name, [d3], blocks, pieces, w4, 0, x, gain, dres)


def _wgrad_up2(name, h, dg, du):
    t, k = h.shape
    nq = dg.shape[1] // 2
    tk = _pick(t, (2048, 1024, 512, 256, 128))
    steps = t // tk
    half = N_CHIPS // 2

    def body(h_ref, dg_ref, du_ref, o_ref):
        q = pl.program_id(0)

        @pl.when(pl.program_id(1) == 0)
        def _():
            o_ref[...] = jnp.zeros_like(o_ref)

        @pl.when(q < half)
        def _():
            o_ref[...] += _dot(h_ref[...], dg_ref[...], TN)

        @pl.when(q >= half)
        def _():
            o_ref[...] += _dot(h_ref[...], du_ref[...], TN)

    return pl.pallas_call(
        body, name=name, grid=(N_CHIPS, steps),
        in_specs=[pl.BlockSpec((tk, k), lambda q, s: (s, 0)),
                  pl.BlockSpec((tk, nq), lambda q, s: (jnp.where(q < half, s, steps - 1), jnp.minimum(q, half - 1))),
                  pl.BlockSpec((tk, nq), lambda q, s: (jnp.where(q >= half, s, 0), jnp.maximum(q - half, 0)))],
        out_specs=pl.BlockSpec((None, k, nq), lambda q, s: (q, 0, 0)),
        out_shape=_sds((N_CHIPS, k, nq), F32),
        compiler_params=_params(("parallel", "arbitrary")))(h, dg, du)


def _wgrad_joined(name, h, dy):
    t, k = h.shape
    nq = dy.shape[1] // N_CHIPS
    tk = _pick(t, (2048, 1024, 512, 256, 128))

    def body(h_ref, dy_ref, o_ref):
        @pl.when(pl.program_id(0) == 0)
        def _():
            o_ref[...] = jnp.zeros_like(o_ref)

        res = _dot(h_ref[...], dy_ref[...], TN)
        for q in range(N_CHIPS):
            o_ref[q] += res[:, q * nq:(q + 1) * nq]

    return pl.pallas_call(
        body, name=name, grid=(t // tk,),
        in_specs=[pl.BlockSpec((tk, k), lambda s: (s, 0)), pl.BlockSpec((tk, N_CHIPS * nq), lambda s: (s, 0))],
        out_specs=pl.BlockSpec((N_CHIPS, k, nq), lambda s: (0, 0, 0)),
        out_shape=_sds((N_CHIPS, k, nq), F32),
        compiler_params=_params(("arbitrary",)))(h, dy)


def _wgrad_conv_in(name, h, d3, nq):
    t, k = h.shape
    d = d3.shape[2]
    per_part, per_q = d // MXU_COLS, nq // MXU_COLS
    tk = _pick(t, (512, 256, 128))

    def body(h_ref, d_ref, o_ref):
        @pl.when(pl.program_id(0) == 0)
        def _():
            o_ref[...] = jnp.zeros_like(o_ref)

        hv = h_ref[...]
        for part in range(3):
            res = _dot(hv, d_ref[part], TN)
            for cc in range(per_part):
                jb = part * per_part + cc
                co = (jb % per_q) * MXU_COLS
                o_ref[jb // per_q, :, co:co + MXU_COLS] += res[:, cc * MXU_COLS:(cc + 1) * MXU_COLS]

    return pl.pallas_call(
        body, name=name, grid=(t // tk,),
        in_specs=[pl.BlockSpec((tk, k), lambda s: (s, 0)), pl.BlockSpec((3, tk, d), lambda s: (0, s, 0))],
        out_specs=pl.BlockSpec((N_CHIPS, k, nq), lambda s: (0, 0, 0)),
        out_shape=_sds((N_CHIPS, k, nq), F32),
        compiler_params=_params(("arbitrary",)))(h, d3)


def _wgrad_down(name, a, dx, tmw):
    t, kf = a.shape
    n = dx.shape[1]
    tk = _pick(t, (2048, 1024, 512, 256, 128))

    def body(a_ref, b_ref, o_ref):
        @pl.when(pl.program_id(1) == 0)
        def _():
            o_ref[...] = jnp.zeros_like(o_ref)

        o_ref[...] += _dot(a_ref[...], b_ref[...].astype(BF16), TN)

    g = pl.pallas_call(
        body, name=name, grid=(kf // tmw, t // tk),
        in_specs=[pl.BlockSpec((tk, tmw), lambda j, s: (s, j)), pl.BlockSpec((tk, n), lambda j, s: (s, 0))],
        out_specs=pl.BlockSpec((tmw, n), lambda j, s: (j, 0)),
        out_shape=_sds((kf, n), F32),
        compiler_params=_params(("parallel", "arbitrary")))(a, dx)
    return g.reshape(N_CHIPS, kf // N_CHIPS, n)


def _shift_rows(u, k, rows):
    s = u.shape[0]
    if k > 0:
        r = pltpu.roll(u, k, 0)
        return jnp.concatenate([jnp.where(rows >= k, r[0:SUBLANES], 0.0), r[SUBLANES:]], axis=0)
    r = pltpu.roll(u, s + k, 0)
    return jnp.concatenate([r[:s - SUBLANES], jnp.where(rows < SUBLANES + k, r[s - SUBLANES:], 0.0)], axis=0)


def _conv_taps(cw_ref, got_ref):
    return (cw_ref[...] + got_ref[0]) + (got_ref[1] + got_ref[2])


def _conv_fwd(bcx, cw, cw_got, nseq, seq):
    t, d3 = bcx.shape
    d = d3 // 3
    cb = 2 * MXU_COLS
    nj = d // cb

    def body(b_ref, c_ref, x_ref, cw_ref, got_ref, z_ref):
        u = b_ref[...].astype(F32) * x_ref[...].astype(F32)
        rows = lax.broadcasted_iota(jnp.int32, (SUBLANES, cb), 0)
        cwv = _conv_taps(cw_ref, got_ref)
        y = cwv[2:3] * u + cwv[1:2] * _shift_rows(u, 1, rows) + cwv[0:1] * _shift_rows(u, 2, rows)
        z_ref[...] = (c_ref[...].astype(F32) * y).astype(BF16)

    return pl.pallas_call(
        body, name="conv_fwd", grid=(nseq, nj),
        in_specs=[pl.BlockSpec((seq, cb), lambda b, j: (b, j)),
                  pl.BlockSpec((seq, cb), lambda b, j: (b, nj + j)),
                  pl.BlockSpec((seq, cb), lambda b, j: (b, 2 * nj + j)),
                  pl.BlockSpec((SUBLANES, cb), lambda b, j: (0, j)),
                  pl.BlockSpec((3, SUBLANES, cb), lambda b, j: (0, 0, j))],
        out_specs=pl.BlockSpec((seq, cb), lambda b, j: (b, j)),
        out_shape=_sds((t, d), BF16),
        compiler_params=_params(("parallel", "parallel")))(bcx, bcx, bcx, cw, cw_got)


def _conv_bwd(dz, bcx, cw, cw_got, nseq, seq):
    t, d3 = bcx.shape
    d = d3 // 3
    cb = MXU_COLS
    nj = d // cb

    def body(dz_ref, b_ref, c_ref, x_ref, cw_ref, got_ref, o_ref, dcw_ref):
        @pl.when(pl.program_id(1) == 0)
        def _():
            dcw_ref[...] = jnp.zeros_like(dcw_ref)

        b = b_ref[...].astype(F32)
        c = c_ref[...].astype(F32)
        xv = x_ref[...].astype(F32)
        dzv = dz_ref[...].astype(F32)
        u = b * xv
        rows = lax.broadcasted_iota(jnp.int32, (SUBLANES, cb), 0)
        u1 = _shift_rows(u, 1, rows)
        u2 = _shift_rows(u, 2, rows)
        cwv = _conv_taps(cw_ref, got_ref)
        y = cwv[2:3] * u + cwv[1:2] * u1 + cwv[0:1] * u2
        dyc = dzv * c
        du = cwv[2:3] * dyc + cwv[1:2] * _shift_rows(dyc, -1, rows) + cwv[0:1] * _shift_rows(dyc, -2, rows)
        o_ref[0] = (du * xv).astype(BF16)
        o_ref[1] = (dzv * y).astype(BF16)
        o_ref[2] = (du * b).astype(BF16)
        s0 = jnp.sum(dyc * u2, axis=0, keepdims=True)
        s1 = jnp.sum(dyc * u1, axis=0, keepdims=True)
        s2 = jnp.sum(dyc * u, axis=0, keepdims=True)
        tap = lax.broadcasted_iota(jnp.int32, (3, cb), 0)
        dcw_ref[...] += jnp.where(tap == 0, s0, jnp.where(tap == 1, s1, s2))

    return pl.pallas_call(
        body, name="conv_bwd", grid=(nj, nseq),
        in_specs=[pl.BlockSpec((seq, cb), lambda j, b: (b, j)),
                  pl.BlockSpec((seq, cb), lambda j, b: (b, j)),
                  pl.BlockSpec((seq, cb), lambda j, b: (b, nj + j)),
                  pl.BlockSpec((seq, cb), lambda j, b: (b, 2 * nj + j)),
                  pl.BlockSpec((SUBLANES, cb), lambda j, b: (0, j)),
                  pl.BlockSpec((3, SUBLANES, cb), lambda j, b: (0, 0, j))],
        out_specs=[pl.BlockSpec((3, seq, cb), lambda j, b: (0, b, j)),
                   pl.BlockSpec((3, cb), lambda j, b: (0, j))],
        out_shape=[_sds((3, t, d), BF16), _sds((3, d), F32)],
        compiler_params=_params(("parallel", "arbitrary")))(dz, bcx, bcx, bcx, cw, cw_got)


def _pair_norm(x, gain_pair, low):
    sq = x * x
    ss_lo = jnp.sum(jnp.where(low, sq, 0.0), axis=-1, keepdims=True)
    ss_hi = jnp.sum(jnp.where(low, 0.0, sq), axis=-1, keepdims=True)
    r = lax.rsqrt(jnp.where(low, ss_lo, ss_hi) * (1.0 / HEAD_DIM) + EPS)
    xhat = x * r
    return xhat * gain_pair, xhat, r


KEYS = 2 * BLOCK
QK_SCALE = 1.0 / (HEAD_DIM ** 0.5)
N_PAIRS = N_Q_HEADS // 2


def _earlier_block(shape=(BLOCK, BLOCK)):
    return lax.broadcasted_iota(jnp.int32, shape, 0) > lax.broadcasted_iota(jnp.int32, shape, 1)


def _fill_bias(bias_ref):
    rows = lax.broadcasted_iota(jnp.int32, (2 * BLOCK, BLOCK), 0)
    qi = lax.broadcasted_iota(jnp.int32, (2 * BLOCK, BLOCK), 1)
    odd_head = rows >= BLOCK
    kj = jnp.where(odd_head, rows - BLOCK, rows)
    earlier = kj > qi
    dist = (jnp.where(earlier, BLOCK, 0) + qi - kj).astype(F32)
    for j in range(N_PAIRS):
        slope = jnp.where(odd_head, ALIBI_SLOPES[2 * j + 1], ALIBI_SLOPES[2 * j])
        bias = -slope * dist
        bias_ref[1, j] = bias
        bias_ref[0, j] = jnp.where(earlier, -1e30, bias)


def _merge_blocks(x_t, earlier):
    return jnp.concatenate([jnp.where(earlier, x_t[e * KEYS:e * KEYS + BLOCK], x_t[e * KEYS + BLOCK:(e + 1) * KEYS])
                            for e in range(2)], axis=0)


def _split_blocks(heads, earlier):
    parts = []
    for x in heads:
        parts += [jnp.where(earlier, x, 0.0), jnp.where(earlier, 0.0, x)]
    return jnp.concatenate(parts, axis=0).astype(BF16)


def _kv_pair_rows(kv_tile, parity, low):
    own = jnp.where(low if parity == 0 else jnp.logical_not(low), kv_tile, 0.0)
    other = pltpu.roll(own, HEAD_DIM, 1)
    lo, hi = (own, other) if parity == 0 else (other, own)
    return jnp.concatenate([lo, hi], axis=0).astype(BF16)


def _pair_softmax(s_t, sink_even, sink_odd):
    out = []
    for e, sink in enumerate((sink_even, sink_odd)):
        se = s_t[e * BLOCK:(e + 1) * BLOCK]
        m = jnp.maximum(jnp.max(se, axis=0, keepdims=True), sink)
        ee = jnp.exp(se - m)
        es = jnp.exp(sink - m)
        inv = 1.0 / (jnp.sum(ee, axis=0, keepdims=True) + es)
        out.append((ee * inv, es * inv))
    return out


def _attn_rows(n):
    q0 = pl.multiple_of(n * BLOCK, BLOCK)
    k0 = pl.multiple_of(jnp.maximum(n - 1, 0) * BLOCK, BLOCK)
    return q0, k0, jnp.minimum(n, 1)


def _key_rows(qkv_ref, k0, q0, col):
    return jnp.concatenate([qkv_ref[pl.ds(k0, BLOCK), col:col + LANES], qkv_ref[pl.ds(q0, BLOCK), col:col + LANES]],
                           axis=0).astype(F32)


def _attn_fwd(qkv, qg_pair, kg_pair, sinks, nseq, seq):
    t = qkv.shape[0]
    dq = N_Q_HEADS * HEAD_DIM
    dkv = N_KV_HEADS * HEAD_DIM

    def body(sk_ref, qkv_ref, qg_ref, kg_ref, o_ref, bias_ref):
        @pl.when(pl.program_id(0) == 0)
        def _():
            _fill_bias(bias_ref)

        low = lax.broadcasted_iota(jnp.int32, (1, LANES), 1) < HEAD_DIM
        earlier = _earlier_block()
        qg = qg_ref[...] * QK_SCALE
        kg = kg_ref[...]

        def blk(n, carry):
            q0, k0, later = _attn_rows(n)
            for kt in range(dkv // LANES):
                kraw = _key_rows(qkv_ref, k0, q0, dq + kt * LANES)
                vraw = _key_rows(qkv_ref, k0, q0, dq + dkv + kt * LANES)
                kn, _, _ = _pair_norm(kraw, kg, low)
                for par in range(2):
                    kh = 2 * kt + par
                    k_pair = _kv_pair_rows(kn, par, low)
                    v_pair = _kv_pair_rows(vraw, par, low)
                    for jj in range(2):
                        j = 2 * kh + jj
                        qraw = qkv_ref[pl.ds(q0, BLOCK), j * LANES:(j + 1) * LANES].astype(F32)
                        qn, _, _ = _pair_norm(qraw, qg, low)
                        s_t = _merge_blocks(_dot(k_pair, qn.astype(BF16), NT), earlier) + bias_ref[later, j]
                        (p0, _), (p1, _) = _pair_softmax(s_t, sk_ref[0, 2 * j], sk_ref[0, 2 * j + 1])
                        p_t = _split_blocks((p0, p1), earlier)
                        o_ref[pl.ds(q0, BLOCK), j * LANES:(j + 1) * LANES] = _dot(p_t, v_pair, TN).astype(BF16)
            return carry

        lax.fori_loop(0, seq // BLOCK, blk, 0)

    return pl.pallas_call(
        body, name="attn_fwd", grid=(nseq,),
        in_specs=[pl.BlockSpec(memory_space=pltpu.SMEM),
                  pl.BlockSpec((seq, dq + 2 * dkv), lambda b: (b, 0)),
                  pl.BlockSpec((1, LANES), lambda b: (0, 0)),
                  pl.BlockSpec((1, LANES), lambda b: (0, 0))],
        out_specs=pl.BlockSpec((seq, dq), lambda b: (b, 0)),
        out_shape=_sds((t, dq), BF16),
        scratch_shapes=[pltpu.VMEM((2, N_PAIRS, 2 * BLOCK, BLOCK), F32)],
        compiler_params=_params(("arbitrary",)))(sinks, qkv, qg_pair, kg_pair)


def _attn_bwd(do, qkv, qg_pair, kg_pair, sinks, nseq, seq):
    t = qkv.shape[0]
    dq = N_Q_HEADS * HEAD_DIM
    dkv = N_KV_HEADS * HEAD_DIM

    def body(sk_ref, do_ref, qkv_ref, qg_ref, kg_ref, o_ref, dqg_ref, dkg_ref, dsk_ref, acc_ref, bias_ref):
        @pl.when(pl.program_id(0) == 0)
        def _():
            _fill_bias(bias_ref)
            dqg_ref[...] = jnp.zeros_like(dqg_ref)
            dkg_ref[...] = jnp.zeros_like(dkg_ref)
            dsk_ref[...] = jnp.zeros_like(dsk_ref)

        acc_ref[...] = jnp.zeros_like(acc_ref)
        low = lax.broadcasted_iota(jnp.int32, (1, LANES), 1) < HEAD_DIM
        earlier = _earlier_block()
        head_row = lax.broadcasted_iota(jnp.int32, (N_Q_HEADS, LANES), 0)
        qg = qg_ref[...] * QK_SCALE
        kg = kg_ref[...]

        def blk(n, carry):
            dqg_acc, dkg_acc, dsk_acc = carry
            q0, k0, later = _attn_rows(n)
            for kt in range(dkv // LANES):
                kraw = _key_rows(qkv_ref, k0, q0, dq + kt * LANES)
                vraw = _key_rows(qkv_ref, k0, q0, dq + dkv + kt * LANES)
                kn, khat, rk = _pair_norm(kraw, kg, low)
                dk_tile = None
                dv_tile = None
                for par in range(2):
                    kh = 2 * kt + par
                    own = low if par == 0 else jnp.logical_not(low)
                    k_pair = _kv_pair_rows(kn, par, low)
                    v_pair = _kv_pair_rows(vraw, par, low)
                    dkn_rows = jnp.zeros((2 * KEYS, LANES), F32)
                    dv_rows = jnp.zeros((2 * KEYS, LANES), F32)
                    for jj in range(2):
                        j = 2 * kh + jj
                        qraw = qkv_ref[pl.ds(q0, BLOCK), j * LANES:(j + 1) * LANES].astype(F32)
                        qn, qhat, rq = _pair_norm(qraw, qg, low)
                        qn_b = qn.astype(BF16)
                        do_b = do_ref[pl.ds(q0, BLOCK), j * LANES:(j + 1) * LANES]
                        s_t = _merge_blocks(_dot(k_pair, qn_b, NT), earlier) + bias_ref[later, j]
                        dp_t = _merge_blocks(_dot(v_pair, do_b, NT), earlier)
                        ds_heads = []
                        probs = _pair_softmax(s_t, sk_ref[0, 2 * j], sk_ref[0, 2 * j + 1])
                        for e, (p, ps) in enumerate(probs):
                            dp = dp_t[e * BLOCK:(e + 1) * BLOCK]
                            dsum = jnp.sum(p * dp, axis=0, keepdims=True)
                            ds_heads.append(p * (dp - dsum))
                            dsk_acc = dsk_acc - jnp.where(head_row == 2 * j + e, ps * dsum, 0.0)
                        p_t = _split_blocks((probs[0][0], probs[1][0]), earlier)
                        ds_t = _split_blocks(ds_heads, earlier)
                        dv_rows = dv_rows + _dot(p_t, do_b, NN)
                        dkn_rows = dkn_rows + _dot(ds_t, qn_b, NN)
                        dqn = _dot(ds_t, k_pair, TN)
                        dqg_acc = dqg_acc + jnp.sum(dqn * qhat, axis=0, keepdims=True)
                        dqhat = dqn * qg
                        prod = dqhat * qhat
                        m_lo = jnp.sum(jnp.where(low, prod, 0.0), axis=-1, keepdims=True)
                        m_hi = jnp.sum(jnp.where(low, 0.0, prod), axis=-1, keepdims=True)
                        mean = jnp.where(low, m_lo, m_hi) * (1.0 / HEAD_DIM)
                        o_ref[pl.ds(q0, BLOCK), j * LANES:(j + 1) * LANES] = (rq * (dqhat - qhat * mean)).astype(BF16)
                    dkn_acc = jnp.where(low, dkn_rows[0:KEYS], dkn_rows[KEYS:2 * KEYS])
                    dv_acc = jnp.where(low, dv_rows[0:KEYS], dv_rows[KEYS:2 * KEYS])
                    dkn = dkn_acc + pltpu.roll(dkn_acc, HEAD_DIM, 1)
                    dvh = dv_acc + pltpu.roll(dv_acc, HEAD_DIM, 1)
                    khat_own = jnp.where(own, khat, 0.0)
                    khat_dup = khat_own + pltpu.roll(khat_own, HEAD_DIM, 1)
                    dkg_acc = dkg_acc + jnp.sum(jnp.where(own, dkn * khat_dup, 0.0), axis=0, keepdims=True)
                    dkhat = dkn * kg
                    mean_k = jnp.sum(dkhat * khat_dup, axis=-1, keepdims=True) * (1.0 / LANES)
                    dk_raw = rk * (dkhat - khat_dup * mean_k)
                    dk_tile = jnp.where(own, dk_raw, 0.0) if dk_tile is None else jnp.where(own, dk_raw, dk_tile)
                    dv_tile = jnp.where(own, dvh, 0.0) if dv_tile is None else jnp.where(own, dvh, dv_tile)
                for r0, part in ((k0, slice(0, BLOCK)), (q0, slice(BLOCK, KEYS))):
                    acc_ref[pl.ds(r0, BLOCK), kt * LANES:(kt + 1) * LANES] += dk_tile[part]
                    acc_ref[pl.ds(r0, BLOCK), dkv + kt * LANES:dkv + (kt + 1) * LANES] += dv_tile[part]
            return dqg_acc, dkg_acc, dsk_acc

        zero = jnp.zeros((1, LANES), F32)
        carry = (zero, zero, jnp.zeros((N_Q_HEADS, LANES), F32))
        dqg_acc, dkg_acc, dsk_acc = lax.fori_loop(0, seq // BLOCK, blk, carry)
        dqg_ref[...] += dqg_acc * QK_SCALE
        dkg_ref[...] += dkg_acc
        dsk_ref[...] += dsk_acc
        o_ref[:, dq:dq + 2 * dkv] = acc_ref[...].astype(BF16)

    small = pl.BlockSpec((1, LANES), lambda b: (0, 0))
    heads = pl.BlockSpec((N_Q_HEADS, LANES), lambda b: (0, 0))
    return pl.pallas_call(
        body, name="attn_bwd", grid=(nseq,),
        in_specs=[pl.BlockSpec(memory_space=pltpu.SMEM),
                  pl.BlockSpec((seq, dq), lambda b: (b, 0)),
                  pl.BlockSpec((seq, dq + 2 * dkv), lambda b: (b, 0)),
                  small, small],
        out_specs=[pl.BlockSpec((seq, dq + 2 * dkv), lambda b: (b, 0)), small, small, heads],
        out_shape=[_sds((t, dq + 2 * dkv), BF16), _sds((1, LANES), F32), _sds((1, LANES), F32),
                   _sds((N_Q_HEADS, LANES), F32)],
        scratch_shapes=[pltpu.VMEM((seq, 2 * dkv), F32), pltpu.VMEM((2, N_PAIRS, 2 * BLOCK, BLOCK), F32)],
        compiler_params=_params(("arbitrary",)))(sinks, do, qkv, qg_pair, kg_pair)


def _place():
    x, y, c = lax.axis_index("x"), lax.axis_index("y"), lax.axis_index("c")
    other_chips = [(1 - x, y), (x, 1 - y), (1 - x, 1 - y)]
    return x, y, c, other_chips


def _half_rows(c, rows):
    rh = rows // 2
    return pl.ds(pl.multiple_of(c * rh, BF16_ROWS), rh)


def _cast_own(name, w, place, layer=None):
    nl, r, cdim = w.shape
    first = 0
    if layer is not None:
        nl, first = 1, layer
    rt = _row_tile(r, 4 * cdim, ELEMENTWISE_BLOCK)

    def body(s_ref, w_ref, o_ref):
        o_ref[...] = w_ref[...].astype(BF16)

    grid_spec = pltpu.PrefetchScalarGridSpec(
        num_scalar_prefetch=1, grid=(nl, r // rt),
        in_specs=[pl.BlockSpec((None, rt, cdim), lambda l, i, s: (first + l, i, 0))],
        out_specs=pl.BlockSpec((None, None, rt, cdim), lambda l, i, s: (l, s[1], i, 0)))
    return pl.pallas_call(
        body, name=name, grid_spec=grid_spec, out_shape=_sds((nl, N_CHIPS, r, cdim), BF16),
        compiler_params=_params(("parallel", "parallel")))(place, w)


def _gather_protocol(outs, shapes, send_sems, recv_sems):
    n = len(outs)
    x, y, c, other_chips = _place()
    me_chip = 2 * x + y
    sibling = (x, y, 1 - c)

    def rows(u, chip, half):
        return outs[u].at[:, chip, _half_rows(half, shapes[u][2]), :]

    def copy(sem, part, to):
        return pltpu.make_async_remote_copy(src_ref=part, dst_ref=part, send_sem=send_sems.at[sem],
                                            recv_sem=recv_sems.at[sem], device_id=to, device_id_type=MESH)

    sends = []
    for u in range(n):
        for k, chip in enumerate(other_chips):
            cp = copy(6 * u + k, rows(u, me_chip, c), (*chip, c))
            cp.start()
            sends.append(cp)
    for u in range(n):
        for k, chip in enumerate(other_chips):
            got = rows(u, 2 * chip[0] + chip[1], c)
            copy(6 * u + k, got, (*chip, c)).wait_recv()
            cp = copy(6 * u + 3 + k, got, sibling)
            cp.start()
            sends.append(cp)
    for u in range(n):
        for k, chip in enumerate(other_chips):
            copy(6 * u + 3 + k, rows(u, 2 * chip[0] + chip[1], 1 - c), sibling).wait_recv()
    for cp in sends:
        cp.wait_send()


def _hbm_ref(a):
    return jax.new_ref(a, memory_space=pltpu.MemorySpace.HBM)


def _sibling_peer():
    x, y, c, _ = _place()
    return [(x, y, 1 - c)]


def _chip_peers():
    x, y, c, other_chips = _place()
    return [(*chip, c) for chip in other_chips]


def _gather_peers():
    return _chip_peers() + _sibling_peer()


def _on_sequencer(name, collective_id, n_sems, peers, protocol, operands=(), out_types=()):
    n_in, n_out = len(operands), len(out_types)

    def launch(*refs):
        send_sems, recv_sems = refs[n_in + n_out:]
        barrier = pltpu.get_barrier_semaphore()
        targets = peers()
        for peer in targets:
            pl.semaphore_signal(barrier, inc=1, device_id=peer, device_id_type=MESH)
        pl.semaphore_wait(barrier, len(targets))
        protocol(refs[:n_in], refs[n_in:n_in + n_out], send_sems, recv_sems)

    return pl.kernel(
        launch, out_type=tuple(out_types), mesh=plsc.ScalarSubcoreMesh(axis_name="sequencer", num_cores=1), name=name,
        scratch_types=(pltpu.SemaphoreType.DMA((n_sems,)), pltpu.SemaphoreType.DMA((n_sems,))),
        compiler_params=pltpu.CompilerParams(collective_id=collective_id))(*operands)


def _seq_allgather(name, collective_id, bufs):
    shapes = [b.shape for b in bufs]
    refs = [_hbm_ref(b) for b in bufs]
    _on_sequencer(name, collective_id, 6 * len(bufs), _gather_peers,
                  lambda ins, outs, send_sems, recv_sems: _gather_protocol(refs, shapes, send_sems, recv_sems))
    return [r[...] for r in refs]


def _taps_protocol(block_ref, got_ref, send_sems, recv_sems, first_sem):
    x, y, c, other_chips = _place()
    copies = []
    for k, chip in enumerate(other_chips):
        cp = pltpu.make_async_remote_copy(src_ref=block_ref, dst_ref=got_ref.at[k], send_sem=send_sems.at[first_sem + k],
                                          recv_sem=recv_sems.at[first_sem + k], device_id=(*chip, c), device_id_type=MESH)
        cp.start()
        copies.append(cp)
    return copies


def _seq_allgather_conv(collective_id, bufs, cw_block):
    shapes = [b.shape for b in bufs]
    refs = [_hbm_ref(b) for b in bufs]

    def protocol(ins, outs, send_sems, recv_sems):
        taps = _taps_protocol(ins[0], outs[0], send_sems, recv_sems, 6 * len(bufs))
        _gather_protocol(refs, shapes, send_sems, recv_sems)
        for cp in taps:
            cp.wait_recv()
        for cp in taps:
            cp.wait_send()

    (got,) = _on_sequencer("allgather_conv", collective_id, 6 * len(bufs) + 3, _gather_peers, protocol,
                           operands=(cw_block,), out_types=(_sds((3, *cw_block.shape), F32),))
    return [r[...] for r in refs], got


def _exchange_protocol(gs, outs, shapes, send_sems, recv_sems):
    x, y, c, _ = _place()
    sends = []
    for u in range(len(gs)):
        cp = pltpu.make_async_remote_copy(
            src_ref=gs[u].at[:, _half_rows(1 - c, shapes[u][1]), :], dst_ref=outs[u],
            send_sem=send_sems.at[u], recv_sem=recv_sems.at[u], device_id=(x, y, 1 - c), device_id_type=MESH)
        cp.start()
        sends.append(cp)
    for cp in sends:
        cp.wait_recv()
    for cp in sends:
        cp.wait_send()


def _seq_exchange(name, collective_id, grads):
    shapes = [g.shape for g in grads]
    return _on_sequencer(
        name, collective_id, len(grads), _sibling_peer,
        lambda gs, outs, send_sems, recv_sems: _exchange_protocol(gs, outs, shapes, send_sems, recv_sems),
        operands=grads, out_types=[_sds((s[0], s[1] // 2, s[2]), F32) for s in shapes])


def _sum_halves(name, g, got, place, after):
    _, r, cdim = g.shape
    rh = r // 2
    rt = _row_tile(rh, 4 * N_CHIPS * cdim, 2 * ELEMENTWISE_BLOCK)
    nr = rh // rt

    def body(s_ref, g_ref, got_ref, after_ref, pb_ref, pf_ref):
        pb_ref[...] = (g_ref[...] + got_ref[...]).astype(BF16)
        mine = s_ref[1]
        pf_ref[...] = g_ref[mine] + got_ref[mine]

    quarters = (N_CHIPS, rt, cdim)
    grid_spec = pltpu.PrefetchScalarGridSpec(
        num_scalar_prefetch=1, grid=(nr,),
        in_specs=[pl.BlockSpec(quarters, lambda i, s: (0, s[0] * nr + i, 0)),
                  pl.BlockSpec(quarters, lambda i, s: (0, i, 0)),
                  pl.BlockSpec(memory_space=pl.ANY)],
        out_specs=[pl.BlockSpec(quarters, lambda i, s: (0, i, 0)),
                   pl.BlockSpec((rt, cdim), lambda i, s: (i, 0))])
    return pl.pallas_call(
        body, name=name, grid_spec=grid_spec,
        out_shape=[_sds((N_CHIPS, rh, cdim), BF16), _sds((rh, cdim), F32)],
        compiler_params=_params(("parallel",)))(place, g, got, after)


def _scatter_protocol(ps, outs, send_sems, recv_sems):
    x, y, c, other_chips = _place()
    sends = []
    for u in range(len(ps)):
        for k, chip in enumerate(other_chips):
            cp = pltpu.make_async_remote_copy(
                src_ref=ps[u].at[2 * chip[0] + chip[1]], dst_ref=outs[u].at[k],
                send_sem=send_sems.at[3 * u + k], recv_sem=recv_sems.at[3 * u + k],
                device_id=(*chip, c), device_id_type=MESH)
            cp.start()
            sends.append(cp)
    for cp in sends:
        cp.wait_recv()
    for cp in sends:
        cp.wait_send()


def _seq_scatter(name, collective_id, partials):
    return _on_sequencer(
        name, collective_id, 3 * len(partials), _chip_peers, _scatter_protocol,
        operands=partials, out_types=[_sds((3, p.shape[1], p.shape[2]), BF16) for p in partials])


def _sum_partials(name, own, got, place, layer, nl, prev, after):
    rh, cdim = own.shape
    rt = _row_tile(rh, 4 * cdim, ELEMENTWISE_BLOCK)
    nr = rh // rt

    def body(s_ref, own_ref, got_ref, *rest):
        o_ref = rest[-1]
        o_ref[...] = ((own_ref[...] + got_ref[0].astype(F32)) + got_ref[1].astype(F32)) + got_ref[2].astype(F32)

    in_specs = [pl.BlockSpec((rt, cdim), lambda i, s: (i, 0)), pl.BlockSpec((3, rt, cdim), lambda i, s: (0, i, 0)),
                pl.BlockSpec(memory_space=pl.ANY)]
    args = [place, own, got, after]
    aliases = {}
    if prev is not None:
        in_specs.append(pl.BlockSpec(memory_space=pl.ANY))
        args.append(prev)
        aliases = {4: 0}
    grid_spec = pltpu.PrefetchScalarGridSpec(
        num_scalar_prefetch=1, grid=(nr,), in_specs=in_specs,
        out_specs=pl.BlockSpec((None, rt, cdim), lambda i, s: (layer, s[0] * nr + i, 0)))
    return pl.pallas_call(
        body, name=name, grid_spec=grid_spec, out_shape=_sds((nl, 2 * rh, cdim), F32),
        input_output_aliases=aliases, compiler_params=_params(("parallel",)))(*args)


def _share_protocol(outs, shapes, units, send_sems, recv_sems):
    x, y, c, _ = _place()
    sends = []
    for u, (w, l) in enumerate(units):
        mine = outs[w].at[l, _half_rows(c, shapes[w][1]), :]
        cp = pltpu.make_async_remote_copy(src_ref=mine, dst_ref=mine, send_sem=send_sems.at[u],
                                          recv_sem=recv_sems.at[u], device_id=(x, y, 1 - c), device_id_type=MESH)
        cp.start()
        sends.append(cp)
    for u, (w, l) in enumerate(units):
        theirs = outs[w].at[l, _half_rows(1 - c, shapes[w][1]), :]
        pltpu.make_async_remote_copy(src_ref=theirs, dst_ref=theirs, send_sem=send_sems.at[u],
                                     recv_sem=recv_sems.at[u], device_id=(x, y, 1 - c),
                                     device_id_type=MESH).wait_recv()
    for cp in sends:
        cp.wait_send()


def _seq_share(name, collective_id, bufs):
    shapes = [b.shape for b in bufs]
    units = [(w, l) for w in range(len(bufs)) for l in range(shapes[w][0])]
    refs = [_hbm_ref(b) for b in bufs]
    _on_sequencer(name, collective_id, len(units), _sibling_peer,
                  lambda ins, outs, send_sems, recv_sems: _share_protocol(refs, shapes, units, send_sems, recv_sems))
    return [r[...] for r in refs]


def _gather_blocks(block_ref, all_ref, send_sems, recv_sems):
    x, y, c, _ = _place()
    me = 4 * x + 2 * y + c
    all_ref[me] = block_ref[...]
    sends = []
    for rel in range(1, 8):
        fx, fy, fc = (rel >> 2) & 1, (rel >> 1) & 1, rel & 1
        peer = (x ^ fx, y ^ fy, c ^ fc)
        cp = pltpu.make_async_remote_copy(src_ref=block_ref, dst_ref=all_ref.at[me], send_sem=send_sems.at[rel - 1],
                                          recv_sem=recv_sems.at[rel - 1], device_id=peer, device_id_type=MESH)
        cp.start()
        sends.append(cp)
    for cp in sends:
        cp.wait_recv()
    for cp in sends:
        cp.wait_send()


def _adam(w, g, m, v):
    m_new = ADAM_B1 * m + (1.0 - ADAM_B1) * g
    v_new = ADAM_B2 * v + (1.0 - ADAM_B2) * (g * g)
    m_hat = m_new / (1.0 - ADAM_B1 ** ADAM_STEP)
    v_hat = v_new / (1.0 - ADAM_B2 ** ADAM_STEP)
    delta = -ADAM_LR * (m_hat / (jnp.sqrt(v_hat) + ADAM_EPS) + ADAM_WD * w)
    return delta, m_new, v_new


def _small_step(dnm0, dnm1, dnf0, dnf1, dcw, dqg, dkg, dsk, loss, w_blk, m_blk, v_blk, cw_cols):
    d = w_blk.shape[1]
    vm = pl.BlockSpec(memory_space=pltpu.VMEM)

    def reduce_body(dnm0_ref, dnm1_ref, dnf0_ref, dnf1_ref, dcw_ref, dqg_ref, dkg_ref, dsk_ref, loss_ref,
                    g_ref, blk_ref, all_ref, send_sems, recv_sems):
        blk_ref[...] = jnp.zeros_like(blk_ref)
        for row, part_ref in ((SENT_NORM_MIXER, dnm0_ref), (SENT_NORM_MIXER + 1, dnm1_ref),
                              (SENT_NORM_FFN, dnf0_ref), (SENT_NORM_FFN + 1, dnf1_ref)):
            blk_ref[row:row + 1, :] = jnp.sum(part_ref[...], axis=0, keepdims=True)
        blk_ref[SENT_CONV_W:SENT_CONV_W + 3, :] = dcw_ref[...]
        misc = slice(SENT_MISC, SENT_MISC + 1)
        for tile, gain_ref in ((TILE_Q_GAIN, dqg_ref), (TILE_K_GAIN, dkg_ref)):
            pair = gain_ref[...]
            blk_ref[misc, tile * LANES:(tile + 1) * LANES] = pair + pltpu.roll(pair, HEAD_DIM, 1)
        for h in range(N_Q_HEADS):
            lane = TILE_SINKS * LANES + h
            blk_ref[misc, lane:lane + 1] = jnp.sum(dsk_ref[h:h + 1, :], axis=1, keepdims=True)
        blk_ref[misc, TILE_LOSS * LANES:(TILE_LOSS + 1) * LANES] = jnp.broadcast_to(loss_ref[...], (1, LANES))
        _gather_blocks(blk_ref, all_ref, send_sems, recv_sems)
        g = all_ref[0]
        for dev in range(1, 8):
            g = g + all_ref[dev]
        g_ref[...] = jnp.zeros_like(g_ref)
        for sent, row, n in ((SENT_NORM_MIXER, ROW_NORM_MIXER, 2), (SENT_NORM_FFN, ROW_NORM_FFN, 2),
                             (SENT_CONV_W, ROW_CONV_W, 3), (SENT_MISC, ROW_MISC, 1)):
            g_ref[row:row + n, :] = g[sent:sent + n]

    g_blk = pl.pallas_call(
        reduce_body, name="small_allreduce", in_specs=[vm] * 9, out_specs=vm, out_shape=_sds((SMALL_ROWS, d), F32),
        scratch_shapes=[pltpu.VMEM((SUBLANES, d), F32), pltpu.VMEM((8, SUBLANES, d), F32),
                        pltpu.SemaphoreType.DMA((7,)), pltpu.SemaphoreType.DMA((7,))],
    )(dnm0, dnm1, dnf0, dnf1, dcw, dqg, dkg, dsk, loss)

    def body(g_ref, w_ref, m_ref, v_ref, *out_refs):
        g = g_ref[...]
        misc = slice(ROW_MISC, ROW_MISC + 1)
        out_refs[0][...] = g[misc, TILE_LOSS * LANES:TILE_LOSS * LANES + 1]
        chip = 2 * lax.axis_index("x") + lax.axis_index("y")
        for i, blk in enumerate((g, *_adam(w_ref[...], g, m_ref[...], v_ref[...]))):
            nm_ref, nf_ref, cw_ref, qg_ref, kg_ref, sk_ref = out_refs[1 + 6 * i:7 + 6 * i]
            nm_ref[...] = blk[ROW_NORM_MIXER:ROW_NORM_MIXER + 2]
            nf_ref[...] = blk[ROW_NORM_FFN:ROW_NORM_FFN + 2]
            qg_ref[...] = blk[misc, TILE_Q_GAIN * LANES:TILE_Q_GAIN * LANES + HEAD_DIM]
            kg_ref[...] = blk[misc, TILE_K_GAIN * LANES:TILE_K_GAIN * LANES + HEAD_DIM]
            sk_ref[...] = blk[misc, TILE_SINKS * LANES:TILE_SINKS * LANES + N_Q_HEADS]
            for q in range(N_CHIPS):
                @pl.when(chip == q)
                def _(blk=blk, cw_ref=cw_ref, q=q):
                    cw_ref[0] = blk[ROW_CONV_W:ROW_CONV_W + 3, q * cw_cols:(q + 1) * cw_cols]

    group = [_sds((2, d), F32), _sds((2, d), F32), _sds((1, 3, cw_cols), F32), _sds((1, HEAD_DIM), F32),
             _sds((1, HEAD_DIM), F32), _sds((1, N_Q_HEADS), F32)]
    outs = pl.pallas_call(
        body, name="small_adam", in_specs=[vm] * 4, out_specs=[vm] * 25, out_shape=[_sds((1, 1), F32)] + group * 4,
    )(g_blk, w_blk, m_blk, v_blk)
    names = ("norm_mixer", "norm_ffn", "conv_w", "attn_q_gain", "attn_k_gain", "attn_sinks")
    return outs[0], [dict(zip(names, outs[1 + 6 * i:7 + 6 * i])) for i in range(4)]


def _adam_step(name, w, g, m, v):
    nl, r, cdim = w.shape
    rt = _row_tile(r, 4 * cdim, ELEMENTWISE_BLOCK)

    def body(w_ref, g_ref, m_ref, v_ref, go_ref, d_ref, mo_ref, vo_ref):
        gv = g_ref[...]
        go_ref[...] = gv
        delta, m_new, v_new = _adam(w_ref[...], gv, m_ref[...], v_ref[...])
        d_ref[...] = delta
        mo_ref[...] = m_new
        vo_ref[...] = v_new

    spec = pl.BlockSpec((None, rt, cdim), lambda l, i: (l, i, 0))
    return pl.pallas_call(
        body, name=name, grid=(nl, r // rt), in_specs=[spec] * 4, out_specs=[spec] * 4,
        out_shape=[_sds(w.shape, F32)] * 4,
        compiler_params=_params(("parallel", "parallel")))(w, g, m, v)


def _pad_rows(a, rows=SUBLANES):
    return jnp.pad(a, ((0, rows - a.shape[0]), (0, 0)))


def _small_block(nm, nf, cw_local, qg, kg, sk, chip):
    d = nm.shape[1]
    cw_rows = lax.dynamic_update_slice(jnp.zeros((SUBLANES, d), F32), cw_local, (0, chip * cw_local.shape[1]))
    misc = jnp.concatenate([qg, qg, kg, kg, jnp.pad(sk, ((0, 0), (0, LANES - sk.shape[1]))),
                            jnp.zeros((1, d - 3 * LANES), F32)], axis=1)
    return jnp.concatenate([_pad_rows(nm), _pad_rows(nf), cw_rows, _pad_rows(misc)], axis=0)


WEIGHT_NAMES = ("conv_w_in", "conv_w", "conv_w_out", "attn_w_qkv", "attn_q_gain", "attn_k_gain", "attn_sinks",
                "attn_w_o", "norm_mixer", "norm_ffn", "ffn_w_gate_up", "ffn_w_down")
BIG = ("conv_w_in", "conv_w_out", "attn_w_qkv", "attn_w_o", "ffn_w_gate_up", "ffn_w_down")


def kernel(x, conv_w_in, conv_w, conv_w_out, attn_w_qkv, attn_q_gain, attn_k_gain, attn_sinks, attn_w_o, norm_mixer, norm_ffn, ffn_w_gate_up, ffn_w_down, loss_target, m_conv_w_in, m_conv_w, m_conv_w_out, m_attn_w_qkv, m_attn_q_gain, m_attn_k_gain, m_attn_sinks, m_attn_w_o, m_norm_mixer, m_norm_ffn, m_ffn_w_gate_up, m_ffn_w_down, v_conv_w_in, v_conv_w, v_conv_w_out, v_attn_w_qkv, v_attn_q_gain, v_attn_k_gain, v_attn_sinks, v_attn_w_o, v_norm_mixer, v_norm_ffn, v_ffn_w_gate_up, v_ffn_w_down):
    w = dict(conv_w_in=conv_w_in, conv_w=conv_w, conv_w_out=conv_w_out, attn_w_qkv=attn_w_qkv,
             attn_q_gain=attn_q_gain, attn_k_gain=attn_k_gain, attn_sinks=attn_sinks, attn_w_o=attn_w_o,
             norm_mixer=norm_mixer, norm_ffn=norm_ffn, ffn_w_gate_up=ffn_w_gate_up, ffn_w_down=ffn_w_down)
    m = dict(conv_w_in=m_conv_w_in, conv_w=m_conv_w, conv_w_out=m_conv_w_out, attn_w_qkv=m_attn_w_qkv,
             attn_q_gain=m_attn_q_gain, attn_k_gain=m_attn_k_gain, attn_sinks=m_attn_sinks, attn_w_o=m_attn_w_o,
             norm_mixer=m_norm_mixer, norm_ffn=m_norm_ffn, ffn_w_gate_up=m_ffn_w_gate_up, ffn_w_down=m_ffn_w_down)
    v = dict(conv_w_in=v_conv_w_in, conv_w=v_conv_w, conv_w_out=v_conv_w_out, attn_w_qkv=v_attn_w_qkv,
             attn_q_gain=v_attn_q_gain, attn_k_gain=v_attn_k_gain, attn_sinks=v_attn_sinks, attn_w_o=v_attn_w_o,
             norm_mixer=v_norm_mixer, norm_ffn=v_norm_ffn, ffn_w_gate_up=v_ffn_w_gate_up, ffn_w_down=v_ffn_w_down)

    nseq, seq, d = x.shape
    t = nseq * seq
    chip = 2 * lax.axis_index("x") + lax.axis_index("y")
    core = lax.axis_index("c")
    place = jnp.stack([core, chip]).astype(jnp.int32)
    x0 = x.reshape(t, d)
    tgt = loss_target.reshape(t, d)

    cw_block = lax.dynamic_update_slice(jnp.zeros((SUBLANES, d), F32), conv_w[0], (0, chip * conv_w.shape[2]))
    def cast(k, layer=None):
        return _cast_own(f"cast_{k}" + ("" if layer is None else str(layer)), w[k], place, layer)

    (w_in,), cw_got = _seq_allgather_conv(1, [cast("conv_w_in")], cw_block)
    w_out, w_gu0, w_dn0 = _seq_allgather(
        "allgather_ffn0", 2, [cast("conv_w_out"), cast("ffn_w_gate_up", 0), cast("ffn_w_down", 0)])
    w_qkv, w_o, w_gu1, w_dn1 = _seq_allgather(
        "allgather_rest", 3, [cast("attn_w_qkv"), cast("attn_w_o"), cast("ffn_w_gate_up", 1), cast("ffn_w_down", 1)])
    w_out = w_out.reshape(1, d, d)
    w_o = w_o.reshape(1, d, d)
    w_gu = [w_gu0, w_gu1]
    w_dn = [w_dn0.reshape(1, D_FF, d), w_dn1.reshape(1, D_FF, d)]

    qg_pair = jnp.concatenate([attn_q_gain, attn_q_gain], axis=1)
    kg_pair = jnp.concatenate([attn_k_gain, attn_k_gain], axis=1)

    def ffn_bwd(i, dxo, xin, h, g, u, a):
        g_dn = _wgrad_down(f"ffn{i}_down_wgrad", a, dxo, D_FF // 2)
        dg, du = _mm_down_t_swiglu(f"ffn{i}_down_dgrad", dxo, w_dn[i], 0, g, u)
        g_gu = _wgrad_up2(f"ffn{i}_up_wgrad", h, dg, du)
        dxi, dgain = _dgrad_norm_ffn(f"ffn{i}_up_dgrad", dg, du, w_gu[i], 0, xin, norm_ffn[i:i + 1], dxo)
        return dxi, dgain, g_gu, g_dn

    h0, bcx = _mm_norm_up_joined("conv_in", x0, norm_mixer[0:1], w_in, 512)
    z = _conv_fwd(bcx, cw_block, cw_got, nseq, seq)
    x1, h1 = _mm_down_norm("conv_out", z, w_out, 0, x0, norm_ffn[0:1])
    g0, u0, a0 = _mm_up_swiglu("ffn0_up", h1, w_gu[0], 0)
    x2, h2 = _mm_down_norm("ffn0_down", a0, w_dn[0], 0, x1, norm_mixer[1:2])
    qkv = _mm_up_joined("attn_qkv", h2, w_qkv, 1024)
    o = _attn_fwd(qkv, qg_pair, kg_pair, attn_sinks, nseq, seq)
    x3, h3 = _mm_down_norm("attn_out", o, w_o, 0, x2, norm_ffn[1:2])
    g1, u1, a1 = _mm_up_swiglu("ffn1_up", h3, w_gu[1], 0)
    dy, loss_part = _mm_down_loss("ffn1_down", a1, w_dn[1], 0, x3, tgt)

    finished = {k: None for k in BIG}

    def exchange(tag, cid, units):
        return units, _seq_exchange(f"exchange_{tag}", cid, [g for _, _, g in units])

    def scatter(tag, cid, group, after):
        units, got = group
        sums = [_sum_halves(f"sum_halves_{k}{l}", g, r, place, after) for (k, l, g), r in zip(units, got)]
        return units, sums, _seq_scatter(f"scatter_{tag}", cid, [pb for pb, _ in sums])

    def finish(group, after):
        units, sums, arrived = group
        for (k, l, _), (_, pf), r in zip(units, sums, arrived):
            finished[k] = _sum_partials(f"sum_partials_{k}{l}", pf, r, place, l, w[k].shape[0], finished[k], after)

    dx3, dnf1, g_gu1, g_dn1 = ffn_bwd(1, dy, x3, h3, g1, u1, a1)
    ffn1 = exchange("ffn1", 4, [("ffn_w_down", 1, g_dn1), ("ffn_w_gate_up", 1, g_gu1)])
    g_o = _wgrad_down("attn_out_wgrad", o, dx3, d)
    do = _mm_down_t("attn_out_dgrad", dx3, w_o, 0)
    ffn1 = scatter("ffn1", 8, ffn1, do)
    dqkv, dqg, dkg, dsk = _attn_bwd(do, qkv, qg_pair, kg_pair, attn_sinks, nseq, seq)
    g_qkv = _wgrad_joined("attn_qkv_wgrad", h2, dqkv)
    attn = exchange("attn", 5, [("attn_w_o", 0, g_o), ("attn_w_qkv", 0, g_qkv)])
    dx2, dnm1 = _dgrad_norm_qkv("attn_qkv_dgrad", dqkv, w_qkv, x2, norm_mixer[1:2], dx3)
    finish(ffn1, dx2)
    attn = scatter("attn", 9, attn, dx2)
    dx1, dnf0, g_gu0, g_dn0 = ffn_bwd(0, dx2, x1, h1, g0, u0, a0)
    ffn0 = exchange("ffn0", 6, [("ffn_w_down", 0, g_dn0), ("ffn_w_gate_up", 0, g_gu0)])
    g_out = _wgrad_down("conv_out_wgrad", z, dx1, d)
    dz = _mm_down_t("conv_out_dgrad", dx1, w_out, 0)
    finish(attn, dz)
    ffn0 = scatter("ffn0", 10, ffn0, dz)
    dbcx, dcw = _conv_bwd(dz, bcx, cw_block, cw_got, nseq, seq)
    g_in = _wgrad_conv_in("conv_in_wgrad", h0, dbcx, conv_w_in.shape[2])
    conv = exchange("conv", 7, [("conv_w_out", 0, g_out), ("conv_w_in", 0, g_in)])
    dx0, dnm0 = _dgrad_norm_conv("conv_in_dgrad", dbcx, w_in, x0, norm_mixer[0:1], dx1)
    finish(ffn0, dx0)
    late = ("attn_w_qkv", "attn_w_o", "ffn_w_gate_up", "ffn_w_down")
    grads_late = _seq_share("share_late", 12, [finished[k] for k in late])
    conv = scatter("conv", 11, conv, dx0)

    grad, delta, new_m, new_v = {}, {}, {}, {}

    def adam(k, g):
        grad[k], delta[k], new_m[k], new_v[k] = _adam_step(f"adam_{k}", w[k], g, m[k], v[k])

    for k, g in zip(late, grads_late):
        adam(k, g)

    def blocks(src):
        return _small_block(src["norm_mixer"], src["norm_ffn"], src["conv_w"][0], src["attn_q_gain"],
                            src["attn_k_gain"], src["attn_sinks"], chip)

    loss, small = _small_step(dnm0, dnm1, dnf0, dnf1, dcw, dqg, dkg, dsk, loss_part,
                              blocks(w), blocks(m), blocks(v), conv_w.shape[2])
    for dst, part in zip((grad, delta, new_m, new_v), small):
        dst.update(part)

    done = sum(new_v[k][0, 0:1, 0:1] for k in late) + loss
    finish(conv, done)
    last = ("conv_w_in", "conv_w_out")
    for k, g in zip(last, _seq_share("share_last", 13, [finished[k] for k in last])):
        adam(k, g)

    return (loss.reshape(()), dx0.reshape(nseq, seq, d), *[grad[k] for k in WEIGHT_NAMES], *[delta[k] for k in WEIGHT_NAMES],
            *[new_m[k] for k in WEIGHT_NAMES], *[new_v[k] for k in WEIGHT_NAMES])
```

```python
import jax
import jax.numpy as jnp
from jax import lax
from jax.experimental import pallas as pl
from jax.experimental.pallas import tpu as pltpu
from jax.experimental.pallas import tpu_sc as plsc

F32 = jnp.float32
BF16 = jnp.bfloat16

D_FF = 2816
N_Q_HEADS = 16
N_KV_HEADS = 4
HEAD_DIM = 64
WINDOW = 128
BLOCK = 128
EPS = 1e-6
N_CHIPS = 4
LANES = 128
SUBLANES = 8
BF16_ROWS = 16
MXU_COLS = 256
VMEM_LIMIT = 48 * 1024 * 1024
ADAM_LR, ADAM_B1, ADAM_B2, ADAM_EPS, ADAM_WD, ADAM_STEP = 0.001, 0.9, 0.999, 1e-08, 0.01, 10
ALIBI_SLOPES = tuple(2.0 ** (-8.0 * (h + 1) / N_Q_HEADS) for h in range(N_Q_HEADS))
SMALL_ROWS = 32
ROW_NORM_MIXER, ROW_NORM_FFN, ROW_CONV_W, ROW_MISC = 0, 8, 16, 24
SENT_NORM_MIXER, SENT_NORM_FFN, SENT_CONV_W, SENT_MISC = 0, 2, 4, 7
TILE_Q_GAIN, TILE_K_GAIN, TILE_SINKS, TILE_LOSS = 0, 1, 2, 3
MESH = pl.DeviceIdType.MESH

NN = ((1,), (0,))
NT = ((1,), (1,))
TN = ((0,), (0,))


def _dot(a, b, dims):
    return lax.dot_general(a, b, (dims, ((), ())), preferred_element_type=F32)


def _pick(n, cands):
    for c in cands:
        if n % c == 0:
            return c
    raise ValueError((n, cands))


def _row_tile(rows, row_bytes, cap_bytes):
    fits = [r for r in range(BF16_ROWS, rows + 1, BF16_ROWS) if rows % r == 0 and r * row_bytes <= cap_bytes]
    if not fits:
        raise ValueError((rows, row_bytes, cap_bytes))
    return fits[-1]


ELEMENTWISE_BLOCK = 3 << 19


def _resident(block_shape, index_map):
    return pl.BlockSpec(block_shape, index_map, pipeline_mode=pl.Buffered(1))


def _params(sem):
    return pltpu.CompilerParams(dimension_semantics=sem, vmem_limit_bytes=VMEM_LIMIT)


def _sds(shape, dtype):
    return jax.ShapeDtypeStruct(shape, dtype)


def _rms(xv):
    return lax.rsqrt(jnp.mean(xv * xv, axis=-1, keepdims=True) + EPS)


def _sigmoid(g):
    return 1.0 / (1.0 + jnp.exp(-g))


def _mm_up_joined(name, a, w4, tm_pref):
    t, k = a.shape
    _, _, _, nq = w4.shape
    tm = _pick(t, (tm_pref, 256, 128))

    def body(a_ref, w_ref, o_ref, wcat_ref):
        @pl.when(pl.program_id(0) == 0)
        def _():
            for q in range(N_CHIPS):
                wcat_ref[:, q * nq:(q + 1) * nq] = w_ref[q]

        o_ref[...] = _dot(a_ref[...], wcat_ref[...], NN).astype(BF16)

    return pl.pallas_call(
        body, name=name, grid=(t // tm,),
        in_specs=[pl.BlockSpec((tm, k), lambda i: (i, 0)),
                  pl.BlockSpec((None, N_CHIPS, k, nq), lambda i: (0, 0, 0, 0))],
        out_specs=pl.BlockSpec((tm, N_CHIPS * nq), lambda i: (i, 0)),
        out_shape=_sds((t, N_CHIPS * nq), BF16),
        scratch_shapes=[pltpu.VMEM((k, N_CHIPS * nq), BF16)],
        compiler_params=_params(("arbitrary",)))(a, w4)


def _mm_norm_up_joined(name, x, gain, w4, tm_pref):
    t, k = x.shape
    _, _, _, nq = w4.shape
    tm = _pick(t, (tm_pref, 256, 128))

    def body(x_ref, g_ref, w_ref, h_ref, o_ref, wcat_ref):
        @pl.when(pl.program_id(0) == 0)
        def _():
            for q in range(N_CHIPS):
                wcat_ref[:, q * nq:(q + 1) * nq] = w_ref[q]

        xv = x_ref[...]
        h = ((xv * _rms(xv)) * g_ref[...]).astype(BF16)
        h_ref[...] = h
        o_ref[...] = _dot(h, wcat_ref[...], NN).astype(BF16)

    return pl.pallas_call(
        body, name=name, grid=(t // tm,),
        in_specs=[pl.BlockSpec((tm, k), lambda i: (i, 0)), pl.BlockSpec((1, k), lambda i: (0, 0)),
                  _resident((None, N_CHIPS, k, nq), lambda i: (0, 0, 0, 0))],
        out_specs=[pl.BlockSpec((tm, k), lambda i: (i, 0)), pl.BlockSpec((tm, N_CHIPS * nq), lambda i: (i, 0))],
        out_shape=[_sds((t, k), BF16), _sds((t, N_CHIPS * nq), BF16)],
        scratch_shapes=[pltpu.VMEM((k, N_CHIPS * nq), BF16)],
        compiler_params=_params(("arbitrary",)))(x, gain, w4)


def _mm_up_swiglu(name, h, w4, layer):
    t, k = h.shape
    _, _, _, nq = w4.shape
    tm = _pick(t, (512, 256, 128))

    def body(h_ref, wg_ref, wu_ref, dag_ref, dau_ref, a_ref):
        hv = h_ref[...]
        g = _dot(hv, wg_ref[...], NN)
        u = _dot(hv, wu_ref[...], NN)
        sg = _sigmoid(g)
        silu = g * sg
        a = silu * u
        dag_ref[...] = (a + sg * (u - a)).astype(BF16)
        dau_ref[...] = silu.astype(BF16)
        a_ref[...] = a.astype(BF16)

    half = N_CHIPS // 2
    out = pl.BlockSpec((tm, nq), lambda j, i: (i, j))
    return pl.pallas_call(
        body, name=name, grid=(half, t // tm),
        in_specs=[pl.BlockSpec((tm, k), lambda j, i: (i, 0)),
                  pl.BlockSpec((None, None, k, nq), lambda j, i: (layer, j, 0, 0)),
                  pl.BlockSpec((None, None, k, nq), lambda j, i: (layer, half + j, 0, 0))],
        out_specs=[out, out, out],
        out_shape=[_sds((t, half * nq), BF16)] * 3,
        compiler_params=_params(("parallel", "parallel")))(h, w4, w4)


def _mm_down_norm(name, a, w, layer, res, gain):
    t, kf = a.shape
    _, _, n = w.shape
    tm = _pick(t, (1024, 512, 256, 128))

    def body(a_ref, w_ref, r_ref, g_ref, o_ref, h_ref):
        xo = r_ref[...] + _dot(a_ref[...], w_ref[...], NN)
        o_ref[...] = xo
        h_ref[...] = ((xo * _rms(xo)) * g_ref[...]).astype(BF16)

    row = pl.BlockSpec((tm, n), lambda i: (i, 0))
    return pl.pallas_call(
        body, name=name, grid=(t // tm,),
        in_specs=[pl.BlockSpec((tm, kf), lambda i: (i, 0)),
                  _resident((None, kf, n), lambda i: (layer, 0, 0)),
                  row, pl.BlockSpec((1, n), lambda i: (0, 0))],
        out_specs=[row, row],
        out_shape=[_sds((t, n), F32), _sds((t, n), BF16)],
        compiler_params=_params(("parallel",)))(a, w, res, gain)


def _mm_down_loss(name, a, w, layer, res, tgt):
    t, kf = a.shape
    _, _, n = w.shape
    tm = _pick(t, (1024, 512, 256, 128))
    steps = t // tm

    def body(a_ref, w_ref, r_ref, t_ref, dy_ref, l_ref, acc_ref):
        i = pl.program_id(0)

        @pl.when(i == 0)
        def _():
            acc_ref[...] = jnp.zeros_like(acc_ref)

        e = (r_ref[...] + _dot(a_ref[...], w_ref[...], NN)) - t_ref[...]
        dy_ref[...] = e * (1.0 / n)
        acc_ref[...] += (e * e).reshape(tm // SUBLANES, SUBLANES, n).sum(axis=0)

        @pl.when(i == steps - 1)
        def _():
            l_ref[...] = jnp.sum(acc_ref[...], keepdims=True) * (0.5 / n)

    row = pl.BlockSpec((tm, n), lambda i: (i, 0))
    return pl.pallas_call(
        body, name=name, grid=(steps,),
        in_specs=[pl.BlockSpec((tm, kf), lambda i: (i, 0)),
                  _resident((None, kf, n), lambda i: (layer, 0, 0)), row, row],
        out_specs=[row, pl.BlockSpec((1, 1), lambda i: (0, 0))],
        out_shape=[_sds((t, n), F32), _sds((1, 1), F32)],
        scratch_shapes=[pltpu.VMEM((SUBLANES, n), F32)],
        compiler_params=_params(("arbitrary",)))(a, w, res, tgt)


def _mm_down_t(name, dx, w, layer):
    t, n = dx.shape
    _, kf, _ = w.shape
    tm = _pick(t, (1024, 512, 256, 128))

    def body(a_ref, w_ref, o_ref):
        o_ref[...] = _dot(a_ref[...].astype(BF16), w_ref[...], NT).astype(BF16)

    return pl.pallas_call(
        body, name=name, grid=(t // tm,),
        in_specs=[pl.BlockSpec((tm, n), lambda i: (i, 0)),
                  _resident((None, kf, n), lambda i: (layer, 0, 0))],
        out_specs=pl.BlockSpec((tm, kf), lambda i: (i, 0)),
        out_shape=_sds((t, kf), BF16),
        compiler_params=_params(("parallel",)))(dx, w)


def _mm_down_t_swiglu(name, dx, w, layer, g, u):
    t, n = dx.shape
    f = g.shape[1]
    tm = _pick(t, (512, 256, 128))

    def body(a_ref, w_ref, dag_ref, dau_ref, dg_ref, du_ref):
        da = _dot(a_ref[...].astype(BF16), w_ref[...], NT)
        dg_ref[...] = (da * dag_ref[...].astype(F32)).astype(BF16)
        du_ref[...] = (da * dau_ref[...].astype(F32)).astype(BF16)

    tile = pl.BlockSpec((tm, f), lambda i: (i, 0))
    return pl.pallas_call(
        body, name=name, grid=(t // tm,),
        in_specs=[pl.BlockSpec((tm, n), lambda i: (i, 0)),
                  _resident((None, f, n), lambda i: (layer, 0, 0)), tile, tile],
        out_specs=[tile, tile],
        out_shape=[_sds((t, f), BF16)] * 2,
        compiler_params=_params(("parallel",)))(dx, w, g, u)


def _dgrad_norm(name, acts, act_blocks, pieces, w4, layer, x, gain, dres):
    t, d = x.shape
    _, _, k, nq = w4.shape
    tm = _pick(t, (512, 256, 128))
    n_act = len(acts)

    def body(*refs):
        act_refs = refs[:n_act]
        w_ref, x_ref, g_ref, dr_ref, dx_ref, dg_ref = refs[n_act:]

        @pl.when(pl.program_id(0) == 0)
        def _():
            dg_ref[...] = jnp.zeros_like(dg_ref)

        dh = None
        for a_tile, w_tile in pieces(act_refs, w_ref):
            term = _dot(a_tile, w_tile, NT)
            dh = term if dh is None else dh + term
        xv = x_ref[...]
        r = _rms(xv)
        xhat = xv * r
        gd = dh * g_ref[...]
        dx_ref[...] = dr_ref[...] + r * (gd - xhat * jnp.mean(gd * xhat, axis=-1, keepdims=True))
        dg_ref[...] += (dh * xhat).reshape(tm // SUBLANES, SUBLANES, d).sum(axis=0)

    row = pl.BlockSpec((tm, d), lambda i: (i, 0))
    return pl.pallas_call(
        body, name=name, grid=(t // tm,),
        in_specs=[*act_blocks(tm),
                  _resident((None, N_CHIPS, k, nq), lambda i: (layer, 0, 0, 0)),
                  row, pl.BlockSpec((1, d), lambda i: (0, 0)), row],
        out_specs=[row, pl.BlockSpec((SUBLANES, d), lambda i: (0, 0))],
        out_shape=[_sds((t, d), F32), _sds((SUBLANES, d), F32)],
        compiler_params=_params(("arbitrary",)))(*acts, w4, x, gain, dres)


def _dgrad_norm_ffn(name, dg, du, w4, layer, x, gain, dres):
    nq = w4.shape[3]
    f = dg.shape[1]

    def blocks(tm):
        return [pl.BlockSpec((tm, f), lambda i: (i, 0))] * 2

    def pieces(act_refs, w_ref):
        dg_ref, du_ref = act_refs
        return [(dg_ref[:, 0:nq], w_ref[0]), (dg_ref[:, nq:2 * nq], w_ref[1]),
                (du_ref[:, 0:nq], w_ref[2]), (du_ref[:, nq:2 * nq], w_ref[3])]

    return _dgrad_norm(name, [dg, du], blocks, pieces, w4, layer, x, gain, dres)


def _dgrad_norm_qkv(name, dqkv, w4, x, gain, dres):
    nq = w4.shape[3]

    def blocks(tm):
        return [pl.BlockSpec((tm, N_CHIPS * nq), lambda i: (i, 0))]

    def pieces(act_refs, w_ref):
        return [(act_refs[0][:, q * nq:(q + 1) * nq], w_ref[q]) for q in range(N_CHIPS)]

    return _dgrad_norm(name, [dqkv], blocks, pieces, w4, 0, x, gain, dres)


def _dgrad_norm_conv(name, d3, w4, x, gain, dres):
    _, _, d = d3.shape
    nq = w4.shape[3]
    per_part, per_q = d // MXU_COLS, nq // MXU_COLS

    def blocks(tm):
        return [pl.BlockSpec((3, tm, d), lambda i: (0, i, 0))]

    def pieces(act_refs, w_ref):
        out = []
        for jb in range(3 * per_part):
            ca, cw = (jb % per_part) * MXU_COLS, (jb % per_q) * MXU_COLS
            out.append((act_refs[0][jb // per_part, :, ca:ca + MXU_COLS], w_ref[jb // per_q, :, cw:cw + MXU_COLS]))
        return out

    return _dgrad_norm(name, [d3], blocks, pieces, w4, 0, x, gain, dres)


def _wgrad_up2(name, h, dg, du):
    t, k = h.shape
    nq = dg.shape[1] // 2
    tk = _pick(t, (2048, 1024, 512, 256, 128))
    steps = t // tk
    half = N_CHIPS // 2

    def body(h_ref, dg_ref, du_ref, o_ref):
        q = pl.program_id(0)

        @pl.when(pl.program_id(1) == 0)
        def _():
            o_ref[...] = jnp.zeros_like(o_ref)

        @pl.when(q < half)
        def _():
            o_ref[...] += _dot(h_ref[...], dg_ref[...], TN)

        @pl.when(q >= half)
        def _():
            o_ref[...] += _dot(h_ref[...], du_ref[...], TN)

    return pl.pallas_call(
        body, name=name, grid=(N_CHIPS, steps),
        in_specs=[pl.BlockSpec((tk, k), lambda q, s: (s, 0)),
                  pl.BlockSpec((tk, nq), lambda q, s: (jnp.where(q < half, s, steps - 1), jnp.minimum(q, half - 1))),
                  pl.BlockSpec((tk, nq), lambda q, s: (jnp.where(q >= half, s, 0), jnp.maximum(q - half, 0)))],
        out_specs=pl.BlockSpec((None, k, nq), lambda q, s: (q, 0, 0)),
        out_shape=_sds((N_CHIPS, k, nq), F32),
        compiler_params=_params(("parallel", "arbitrary")))(h, dg, du)


def _wgrad_joined(name, h, dy):
    t, k = h.shape
    nq = dy.shape[1] // N_CHIPS
    tk = _pick(t, (2048, 1024, 512, 256, 128))

    def body(h_ref, dy_ref, o_ref):
        @pl.when(pl.program_id(0) == 0)
        def _():
            o_ref[...] = jnp.zeros_like(o_ref)

        res = _dot(h_ref[...], dy_ref[...], TN)
        for q in range(N_CHIPS):
            o_ref[q] += res[:, q * nq:(q + 1) * nq]

    return pl.pallas_call(
        body, name=name, grid=(t // tk,),
        in_specs=[pl.BlockSpec((tk, k), lambda s: (s, 0)), pl.BlockSpec((tk, N_CHIPS * nq), lambda s: (s, 0))],
        out_specs=pl.BlockSpec((N_CHIPS, k, nq), lambda s: (0, 0, 0)),
        out_shape=_sds((N_CHIPS, k, nq), F32),
        compiler_params=_params(("arbitrary",)))(h, dy)


def _wgrad_conv_in(name, h, d3, nq):
    t, k = h.shape
    d = d3.shape[2]
    per_part, per_q = d // MXU_COLS, nq // MXU_COLS
    tk = _pick(t, (512, 256, 128))

    def body(h_ref, d_ref, o_ref):
        @pl.when(pl.program_id(0) == 0)
        def _():
            o_ref[...] = jnp.zeros_like(o_ref)

        hv = h_ref[...]
        for part in range(3):
            res = _dot(hv, d_ref[part], TN)
            for cc in range(per_part):
                jb = part * per_part + cc
                co = (jb % per_q) * MXU_COLS
                o_ref[jb // per_q, :, co:co + MXU_COLS] += res[:, cc * MXU_COLS:(cc + 1) * MXU_COLS]

    return pl.pallas_call(
        body, name=name, grid=(t // tk,),
        in_specs=[pl.BlockSpec((tk, k), lambda s: (s, 0)), pl.BlockSpec((3, tk, d), lambda s: (0, s, 0))],
        out_specs=pl.BlockSpec((N_CHIPS, k, nq), lambda s: (0, 0, 0)),
        out_shape=_sds((N_CHIPS, k, nq), F32),
        compiler_params=_params(("arbitrary",)))(h, d3)


def _wgrad_down(name, a, dx, tmw):
    t, kf = a.shape
    n = dx.shape[1]
    tk = _pick(t, (2048, 1024, 512, 256, 128))

    def body(a_ref, b_ref, o_ref):
        @pl.when(pl.program_id(1) == 0)
        def _():
            o_ref[...] = jnp.zeros_like(o_ref)

        o_ref[...] += _dot(a_ref[...], b_ref[...].astype(BF16), TN)

    g = pl.pallas_call(
        body, name=name, grid=(kf // tmw, t // tk),
        in_specs=[pl.BlockSpec((tk, tmw), lambda j, s: (s, j)), pl.BlockSpec((tk, n), lambda j, s: (s, 0))],
        out_specs=pl.BlockSpec((tmw, n), lambda j, s: (j, 0)),
        out_shape=_sds((kf, n), F32),
        compiler_params=_params(("parallel", "arbitrary")))(a, dx)
    return g.reshape(N_CHIPS, kf // N_CHIPS, n)


def _shift_rows(u, k, rows):
    s = u.shape[0]
    if k > 0:
        r = pltpu.roll(u, k, 0)
        return jnp.concatenate([jnp.where(rows >= k, r[0:SUBLANES], 0.0), r[SUBLANES:]], axis=0)
    r = pltpu.roll(u, s + k, 0)
    return jnp.concatenate([r[:s - SUBLANES], jnp.where(rows < SUBLANES + k, r[s - SUBLANES:], 0.0)], axis=0)


def _conv_taps(cw_ref, got_ref):
    return (cw_ref[...] + got_ref[0]) + (got_ref[1] + got_ref[2])


def _conv_fwd(bcx, cw, cw_got, nseq, seq):
    t, d3 = bcx.shape
    d = d3 // 3
    cb = 2 * MXU_COLS
    nj = d // cb

    def body(b_ref, c_ref, x_ref, cw_ref, got_ref, z_ref):
        u = b_ref[...].astype(F32) * x_ref[...].astype(F32)
        rows = lax.broadcasted_iota(jnp.int32, (SUBLANES, cb), 0)
        cwv = _conv_taps(cw_ref, got_ref)
        y = cwv[2:3] * u + cwv[1:2] * _shift_rows(u, 1, rows) + cwv[0:1] * _shift_rows(u, 2, rows)
        z_ref[...] = (c_ref[...].astype(F32) * y).astype(BF16)

    return pl.pallas_call(
        body, name="conv_fwd", grid=(nseq, nj),
        in_specs=[pl.BlockSpec((seq, cb), lambda b, j: (b, j)),
                  pl.BlockSpec((seq, cb), lambda b, j: (b, nj + j)),
                  pl.BlockSpec((seq, cb), lambda b, j: (b, 2 * nj + j)),
                  pl.BlockSpec((SUBLANES, cb), lambda b, j: (0, j)),
                  pl.BlockSpec((3, SUBLANES, cb), lambda b, j: (0, 0, j))],
        out_specs=pl.BlockSpec((seq, cb), lambda b, j: (b, j)),
        out_shape=_sds((t, d), BF16),
        compiler_params=_params(("parallel", "parallel")))(bcx, bcx, bcx, cw, cw_got)


def _conv_bwd(dz, bcx, cw, cw_got, nseq, seq):
    t, d3 = bcx.shape
    d = d3 // 3
    cb = MXU_COLS
    nj = d // cb

    def body(dz_ref, b_ref, c_ref, x_ref, cw_ref, got_ref, o_ref, dcw_ref):
        @pl.when(pl.program_id(1) == 0)
        def _():
            dcw_ref[...] = jnp.zeros_like(dcw_ref)

        b = b_ref[...].astype(F32)
        c = c_ref[...].astype(F32)
        xv = x_ref[...].astype(F32)
        dzv = dz_ref[...].astype(F32)
        u = b * xv
        rows = lax.broadcasted_iota(jnp.int32, (SUBLANES, cb), 0)
        u1 = _shift_rows(u, 1, rows)
        u2 = _shift_rows(u, 2, rows)
        cwv = _conv_taps(cw_ref, got_ref)
        y = cwv[2:3] * u + cwv[1:2] * u1 + cwv[0:1] * u2
        dyc = dzv * c
        du = cwv[2:3] * dyc + cwv[1:2] * _shift_rows(dyc, -1, rows) + cwv[0:1] * _shift_rows(dyc, -2, rows)
        o_ref[0] = (du * xv).astype(BF16)
        o_ref[1] = (dzv * y).astype(BF16)
        o_ref[2] = (du * b).astype(BF16)
        s0 = jnp.sum(dyc * u2, axis=0, keepdims=True)
        s1 = jnp.sum(dyc * u1, axis=0, keepdims=True)
        s2 = jnp.sum(dyc * u, axis=0, keepdims=True)
        tap = lax.broadcasted_iota(jnp.int32, (3, cb), 0)
        dcw_ref[...] += jnp.where(tap == 0, s0, jnp.where(tap == 1, s1, s2))

    return pl.pallas_call(
        body, name="conv_bwd", grid=(nj, nseq),
        in_specs=[pl.BlockSpec((seq, cb), lambda j, b: (b, j)),
                  pl.BlockSpec((seq, cb), lambda j, b: (b, j)),
                  pl.BlockSpec((seq, cb), lambda j, b: (b, nj + j)),
                  pl.BlockSpec((seq, cb), lambda j, b: (b, 2 * nj + j)),
                  pl.BlockSpec((SUBLANES, cb), lambda j, b: (0, j)),
                  pl.BlockSpec((3, SUBLANES, cb), lambda j, b: (0, 0, j))],
        out_specs=[pl.BlockSpec((3, seq, cb), lambda j, b: (0, b, j)),
                   pl.BlockSpec((3, cb), lambda j, b: (0, j))],
        out_shape=[_sds((3, t, d), BF16), _sds((3, d), F32)],
        compiler_params=_params(("parallel", "arbitrary")))(dz, bcx, bcx, bcx, cw, cw_got)


def _pair_norm(x, gain_pair, low):
    sq = x * x
    ss_lo = jnp.sum(jnp.where(low, sq, 0.0), axis=-1, keepdims=True)
    ss_hi = jnp.sum(jnp.where(low, 0.0, sq), axis=-1, keepdims=True)
    r = lax.rsqrt(jnp.where(low, ss_lo, ss_hi) * (1.0 / HEAD_DIM) + EPS)
    xhat = x * r
    return xhat * gain_pair, xhat, r


KEYS = 2 * BLOCK
QK_SCALE = 1.0 / (HEAD_DIM ** 0.5)
N_PAIRS = N_Q_HEADS // 2


def _earlier_block(shape=(BLOCK, BLOCK)):
    return lax.broadcasted_iota(jnp.int32, shape, 0) > lax.broadcasted_iota(jnp.int32, shape, 1)


def _fill_bias(bias_ref):
    rows = lax.broadcasted_iota(jnp.int32, (2 * BLOCK, BLOCK), 0)
    qi = lax.broadcasted_iota(jnp.int32, (2 * BLOCK, BLOCK), 1)
    odd_head = rows >= BLOCK
    kj = jnp.where(odd_head, rows - BLOCK, rows)
    earlier = kj > qi
    dist = (jnp.where(earlier, BLOCK, 0) + qi - kj).astype(F32)
    for j in range(N_PAIRS):
        slope = jnp.where(odd_head, ALIBI_SLOPES[2 * j + 1], ALIBI_SLOPES[2 * j])
        bias = -slope * dist
        bias_ref[1, j] = bias
        bias_ref[0, j] = jnp.where(earlier, -1e30, bias)


def _merge_blocks(x_t, earlier):
    return jnp.concatenate([jnp.where(earlier, x_t[e * KEYS:e * KEYS + BLOCK], x_t[e * KEYS + BLOCK:(e + 1) * KEYS])
                            for e in range(2)], axis=0)


def _split_blocks(heads, earlier):
    parts = []
    for x in heads:
        parts += [jnp.where(earlier, x, 0.0), jnp.where(earlier, 0.0, x)]
    return jnp.concatenate(parts, axis=0).astype(BF16)


def _kv_pair_rows(kv_tile, parity, low):
    own = jnp.where(low if parity == 0 else jnp.logical_not(low), kv_tile, 0.0)
    other = pltpu.roll(own, HEAD_DIM, 1)
    lo, hi = (own, other) if parity == 0 else (other, own)
    return jnp.concatenate([lo, hi], axis=0).astype(BF16)


def _pair_softmax(s_t, sink_even, sink_odd):
    out = []
    for e, sink in enumerate((sink_even, sink_odd)):
        se = s_t[e * BLOCK:(e + 1) * BLOCK]
        m = jnp.maximum(jnp.max(se, axis=0, keepdims=True), sink)
        ee = jnp.exp(se - m)
        es = jnp.exp(sink - m)
        inv = 1.0 / (jnp.sum(ee, axis=0, keepdims=True) + es)
        out.append((ee * inv, es * inv))
    return out


def _attn_rows(n):
    q0 = pl.multiple_of(n * BLOCK, BLOCK)
    k0 = pl.multiple_of(jnp.maximum(n - 1, 0) * BLOCK, BLOCK)
    return q0, k0, jnp.minimum(n, 1)


def _key_rows(qkv_ref, k0, q0, col):
    return jnp.concatenate([qkv_ref[pl.ds(k0, BLOCK), col:col + LANES], qkv_ref[pl.ds(q0, BLOCK), col:col + LANES]],
                           axis=0).astype(F32)


def _attn_fwd(qkv, qg_pair, kg_pair, sinks, nseq, seq):
    t = qkv.shape[0]
    dq = N_Q_HEADS * HEAD_DIM
    dkv = N_KV_HEADS * HEAD_DIM

    def body(sk_ref, qkv_ref, qg_ref, kg_ref, o_ref, bias_ref):
        @pl.when(pl.program_id(0) == 0)
        def _():
            _fill_bias(bias_ref)

        low = lax.broadcasted_iota(jnp.int32, (1, LANES), 1) < HEAD_DIM
        earlier = _earlier_block()
        qg = qg_ref[...] * QK_SCALE
        kg = kg_ref[...]

        def blk(n, carry):
            q0, k0, later = _attn_rows(n)
            for kt in range(dkv // LANES):
                kraw = _key_rows(qkv_ref, k0, q0, dq + kt * LANES)
                vraw = _key_rows(qkv_ref, k0, q0, dq + dkv + kt * LANES)
                kn, _, _ = _pair_norm(kraw, kg, low)
                for par in range(2):
                    kh = 2 * kt + par
                    k_pair = _kv_pair_rows(kn, par, low)
                    v_pair = _kv_pair_rows(vraw, par, low)
                    for jj in range(2):
                        j = 2 * kh + jj
                        qraw = qkv_ref[pl.ds(q0, BLOCK), j * LANES:(j + 1) * LANES].astype(F32)
                        qn, _, _ = _pair_norm(qraw, qg, low)
                        s_t = _merge_blocks(_dot(k_pair, qn.astype(BF16), NT), earlier) + bias_ref[later, j]
                        (p0, _), (p1, _) = _pair_softmax(s_t, sk_ref[0, 2 * j], sk_ref[0, 2 * j + 1])
                        p_t = _split_blocks((p0, p1), earlier)
                        o_ref[pl.ds(q0, BLOCK), j * LANES:(j + 1) * LANES] = _dot(p_t, v_pair, TN).astype(BF16)
            return carry

        lax.fori_loop(0, seq // BLOCK, blk, 0)

    return pl.pallas_call(
        body, name="attn_fwd", grid=(nseq,),
        in_specs=[pl.BlockSpec(memory_space=pltpu.SMEM),
                  pl.BlockSpec((seq, dq + 2 * dkv), lambda b: (b, 0)),
                  pl.BlockSpec((1, LANES), lambda b: (0, 0)),
                  pl.BlockSpec((1, LANES), lambda b: (0, 0))],
        out_specs=pl.BlockSpec((seq, dq), lambda b: (b, 0)),
        out_shape=_sds((t, dq), BF16),
        scratch_shapes=[pltpu.VMEM((2, N_PAIRS, 2 * BLOCK, BLOCK), F32)],
        compiler_params=_params(("arbitrary",)))(sinks, qkv, qg_pair, kg_pair)


def _attn_bwd(do, qkv, qg_pair, kg_pair, sinks, nseq, seq):
    t = qkv.shape[0]
    dq = N_Q_HEADS * HEAD_DIM
    dkv = N_KV_HEADS * HEAD_DIM

    def body(sk_ref, do_ref, qkv_ref, qg_ref, kg_ref, o_ref, dqg_ref, dkg_ref, dsk_ref, acc_ref, bias_ref):
        @pl.when(pl.program_id(0) == 0)
        def _():
            _fill_bias(bias_ref)
            dqg_ref[...] = jnp.zeros_like(dqg_ref)
            dkg_ref[...] = jnp.zeros_like(dkg_ref)
            dsk_ref[...] = jnp.zeros_like(dsk_ref)

        acc_ref[...] = jnp.zeros_like(acc_ref)
        low = lax.broadcasted_iota(jnp.int32, (1, LANES), 1) < HEAD_DIM
        earlier = _earlier_block()
        head_row = lax.broadcasted_iota(jnp.int32, (N_Q_HEADS, LANES), 0)
        qg = qg_ref[...] * QK_SCALE
        kg = kg_ref[...]

        def blk(n, carry):
            dqg_acc, dkg_acc, dsk_acc = carry
            q0, k0, later = _attn_rows(n)
            for kt in range(dkv // LANES):
                kraw = _key_rows(qkv_ref, k0, q0, dq + kt * LANES)
                vraw = _key_rows(qkv_ref, k0, q0, dq + dkv + kt * LANES)
                kn, khat, rk = _pair_norm(kraw, kg, low)
                dk_tile = None
                dv_tile = None
                for par in range(2):
                    kh = 2 * kt + par
                    own = low if par == 0 else jnp.logical_not(low)
                    k_pair = _kv_pair_rows(kn, par, low)
                    v_pair = _kv_pair_rows(vraw, par, low)
                    dkn_rows = jnp.zeros((2 * KEYS, LANES), F32)
                    dv_rows = jnp.zeros((2 * KEYS, LANES), F32)
                    for jj in range(2):
                        j = 2 * kh + jj
                        qraw = qkv_ref[pl.ds(q0, BLOCK), j * LANES:(j + 1) * LANES].astype(F32)
                        qn, qhat, rq = _pair_norm(qraw, qg, low)
                        qn_b = qn.astype(BF16)
                        do_b = do_ref[pl.ds(q0, BLOCK), j * LANES:(j + 1) * LANES]
                        s_t = _merge_blocks(_dot(k_pair, qn_b, NT), earlier) + bias_ref[later, j]
                        dp_t = _merge_blocks(_dot(v_pair, do_b, NT), earlier)
                        ds_heads = []
                        probs = _pair_softmax(s_t, sk_ref[0, 2 * j], sk_ref[0, 2 * j + 1])
                        for e, (p, ps) in enumerate(probs):
                            dp = dp_t[e * BLOCK:(e + 1) * BLOCK]
                            dsum = jnp.sum(p * dp, axis=0, keepdims=True)
                            ds_heads.append(p * (dp - dsum))
                            dsk_acc = dsk_acc - jnp.where(head_row == 2 * j + e, ps * dsum, 0.0)
                        p_t = _split_blocks((probs[0][0], probs[1][0]), earlier)
                        ds_t = _split_blocks(ds_heads, earlier)
                        dv_rows = dv_rows + _dot(p_t, do_b, NN)
                        dkn_rows = dkn_rows + _dot(ds_t, qn_b, NN)
                        dqn = _dot(ds_t, k_pair, TN)
                        dqg_acc = dqg_acc + jnp.sum(dqn * qhat, axis=0, keepdims=True)
                        dqhat = dqn * qg
                        prod = dqhat * qhat
                        m_lo = jnp.sum(jnp.where(low, prod, 0.0), axis=-1, keepdims=True)
                        m_hi = jnp.sum(jnp.where(low, 0.0, prod), axis=-1, keepdims=True)
                        mean = jnp.where(low, m_lo, m_hi) * (1.0 / HEAD_DIM)
                        o_ref[pl.ds(q0, BLOCK), j * LANES:(j + 1) * LANES] = (rq * (dqhat - qhat * mean)).astype(BF16)
                    dkn_acc = jnp.where(low, dkn_rows[0:KEYS], dkn_rows[KEYS:2 * KEYS])
                    dv_acc = jnp.where(low, dv_rows[0:KEYS], dv_rows[KEYS:2 * KEYS])
                    dkn = dkn_acc + pltpu.roll(dkn_acc, HEAD_DIM, 1)
                    dvh = dv_acc + pltpu.roll(dv_acc, HEAD_DIM, 1)
                    khat_own = jnp.where(own, khat, 0.0)
                    khat_dup = khat_own + pltpu.roll(khat_own, HEAD_DIM, 1)
                    dkg_acc = dkg_acc + jnp.sum(jnp.where(own, dkn * khat_dup, 0.0), axis=0, keepdims=True)
                    dkhat = dkn * kg
                    mean_k = jnp.sum(dkhat * khat_dup, axis=-1, keepdims=True) * (1.0 / LANES)
                    dk_raw = rk * (dkhat - khat_dup * mean_k)
                    dk_tile = jnp.where(own, dk_raw, 0.0) if dk_tile is None else jnp.where(own, dk_raw, dk_tile)
                    dv_tile = jnp.where(own, dvh, 0.0) if dv_tile is None else jnp.where(own, dvh, dv_tile)
                for r0, part in ((k0, slice(0, BLOCK)), (q0, slice(BLOCK, KEYS))):
                    acc_ref[pl.ds(r0, BLOCK), kt * LANES:(kt + 1) * LANES] += dk_tile[part]
                    acc_ref[pl.ds(r0, BLOCK), dkv + kt * LANES:dkv + (kt + 1) * LANES] += dv_tile[part]
            return dqg_acc, dkg_acc, dsk_acc

        zero = jnp.zeros((1, LANES), F32)
        carry = (zero, zero, jnp.zeros((N_Q_HEADS, LANES), F32))
        dqg_acc, dkg_acc, dsk_acc = lax.fori_loop(0, seq // BLOCK, blk, carry)
        dqg_ref[...] += dqg_acc * QK_SCALE
        dkg_ref[...] += dkg_acc
        dsk_ref[...] += dsk_acc
        o_ref[:, dq:dq + 2 * dkv] = acc_ref[...].astype(BF16)

    small = pl.BlockSpec((1, LANES), lambda b: (0, 0))
    heads = pl.BlockSpec((N_Q_HEADS, LANES), lambda b: (0, 0))
    return pl.pallas_call(
        body, name="attn_bwd", grid=(nseq,),
        in_specs=[pl.BlockSpec(memory_space=pltpu.SMEM),
                  pl.BlockSpec((seq, dq), lambda b: (b, 0)),
                  pl.BlockSpec((seq, dq + 2 * dkv), lambda b: (b, 0)),
                  small, small],
        out_specs=[pl.BlockSpec((seq, dq + 2 * dkv), lambda b: (b, 0)), small, small, heads],
        out_shape=[_sds((t, dq + 2 * dkv), BF16), _sds((1, LANES), F32), _sds((1, LANES), F32),
                   _sds((N_Q_HEADS, LANES), F32)],
        scratch_shapes=[pltpu.VMEM((seq, 2 * dkv), F32), pltpu.VMEM((2, N_PAIRS, 2 * BLOCK, BLOCK), F32)],
        compiler_params=_params(("arbitrary",)))(sinks, do, qkv, qg_pair, kg_pair)


def _place():
    x, y, c = lax.axis_index("x"), lax.axis_index("y"), lax.axis_index("c")
    other_chips = [(1 - x, y), (x, 1 - y), (1 - x, 1 - y)]
    return x, y, c, other_chips


def _half_rows(c, rows):
    rh = rows // 2
    return pl.ds(pl.multiple_of(c * rh, BF16_ROWS), rh)


def _cast_own(name, w, place, layer=None):
    nl, r, cdim = w.shape
    first = 0
    if layer is not None:
        nl, first = 1, layer
    rt = _row_tile(r, 4 * cdim, ELEMENTWISE_BLOCK)

    def body(s_ref, w_ref, o_ref):
        o_ref[...] = w_ref[...].astype(BF16)

    grid_spec = pltpu.PrefetchScalarGridSpec(
        num_scalar_prefetch=1, grid=(nl, r // rt),
        in_specs=[pl.BlockSpec((None, rt, cdim), lambda l, i, s: (first + l, i, 0))],
        out_specs=pl.BlockSpec((None, None, rt, cdim), lambda l, i, s: (l, s[1], i, 0)))
    return pl.pallas_call(
        body, name=name, grid_spec=grid_spec, out_shape=_sds((nl, N_CHIPS, r, cdim), BF16),
        compiler_params=_params(("parallel", "parallel")))(place, w)


def _gather_protocol(outs, shapes, send_sems, recv_sems):
    n = len(outs)
    x, y, c, other_chips = _place()
    me_chip = 2 * x + y
    sibling = (x, y, 1 - c)

    def rows(u, chip, half):
        return outs[u].at[:, chip, _half_rows(half, shapes[u][2]), :]

    def copy(sem, part, to):
        return pltpu.make_async_remote_copy(src_ref=part, dst_ref=part, send_sem=send_sems.at[sem],
                                            recv_sem=recv_sems.at[sem], device_id=to, device_id_type=MESH)

    sends = []
    for u in range(n):
        for k, chip in enumerate(other_chips):
            cp = copy(6 * u + k, rows(u, me_chip, c), (*chip, c))
            cp.start()
            sends.append(cp)
    for u in range(n):
        for k, chip in enumerate(other_chips):
            got = rows(u, 2 * chip[0] + chip[1], c)
            copy(6 * u + k, got, (*chip, c)).wait_recv()
            cp = copy(6 * u + 3 + k, got, sibling)
            cp.start()
            sends.append(cp)
    for u in range(n):
        for k, chip in enumerate(other_chips):
            copy(6 * u + 3 + k, rows(u, 2 * chip[0] + chip[1], 1 - c), sibling).wait_recv()
    for cp in sends:
        cp.wait_send()


def _hbm_ref(a):
    return jax.new_ref(a, memory_space=pltpu.MemorySpace.HBM)


def _sibling_peer():
    x, y, c, _ = _place()
    return [(x, y, 1 - c)]


def _chip_peers():
    x, y, c, other_chips = _place()
    return [(*chip, c) for chip in other_chips]


def _gather_peers():
    return _chip_peers() + _sibling_peer()


def _on_sequencer(name, collective_id, n_sems, peers, protocol, operands=(), out_types=()):
    n_in, n_out = len(operands), len(out_types)

    def launch(*refs):
        send_sems, recv_sems = refs[n_in + n_out:]
        barrier = pltpu.get_barrier_semaphore()
        targets = peers()
        for peer in targets:
            pl.semaphore_signal(barrier, inc=1, device_id=peer, device_id_type=MESH)
        pl.semaphore_wait(barrier, len(targets))
        protocol(refs[:n_in], refs[n_in:n_in + n_out], send_sems, recv_sems)

    return pl.kernel(
        launch, out_type=tuple(out_types), mesh=plsc.ScalarSubcoreMesh(axis_name="sequencer", num_cores=1), name=name,
        scratch_types=(pltpu.SemaphoreType.DMA((n_sems,)), pltpu.SemaphoreType.DMA((n_sems,))),
        compiler_params=pltpu.CompilerParams(collective_id=collective_id))(*operands)


def _seq_allgather(name, collective_id, bufs):
    shapes = [b.shape for b in bufs]
    refs = [_hbm_ref(b) for b in bufs]
    _on_sequencer(name, collective_id, 6 * len(bufs), _gather_peers,
                  lambda ins, outs, send_sems, recv_sems: _gather_protocol(refs, shapes, send_sems, recv_sems))
    return [r[...] for r in refs]


def _taps_protocol(block_ref, got_ref, send_sems, recv_sems, first_sem):
    x, y, c, other_chips = _place()
    copies = []
    for k, chip in enumerate(other_chips):
        cp = pltpu.make_async_remote_copy(src_ref=block_ref, dst_ref=got_ref.at[k], send_sem=send_sems.at[first_sem + k],
                                          recv_sem=recv_sems.at[first_sem + k], device_id=(*chip, c), device_id_type=MESH)
        cp.start()
        copies.append(cp)
    return copies


def _seq_allgather_conv(collective_id, bufs, cw_block):
    shapes = [b.shape for b in bufs]
    refs = [_hbm_ref(b) for b in bufs]

    def protocol(ins, outs, send_sems, recv_sems):
        taps = _taps_protocol(ins[0], outs[0], send_sems, recv_sems, 6 * len(bufs))
        _gather_protocol(refs, shapes, send_sems, recv_sems)
        for cp in taps:
            cp.wait_recv()
        for cp in taps:
            cp.wait_send()

    (got,) = _on_sequencer("allgather_conv", collective_id, 6 * len(bufs) + 3, _gather_peers, protocol,
                           operands=(cw_block,), out_types=(_sds((3, *cw_block.shape), F32),))
    return [r[...] for r in refs], got


def _exchange_protocol(gs, outs, shapes, send_sems, recv_sems):
    x, y, c, _ = _place()
    sends = []
    for u in range(len(gs)):
        cp = pltpu.make_async_remote_copy(
            src_ref=gs[u].at[:, _half_rows(1 - c, shapes[u][1]), :], dst_ref=outs[u],
            send_sem=send_sems.at[u], recv_sem=recv_sems.at[u], device_id=(x, y, 1 - c), device_id_type=MESH)
        cp.start()
        sends.append(cp)
    for cp in sends:
        cp.wait_recv()
    for cp in sends:
        cp.wait_send()


def _seq_exchange(name, collective_id, grads):
    shapes = [g.shape for g in grads]
    return _on_sequencer(
        name, collective_id, len(grads), _sibling_peer,
        lambda gs, outs, send_sems, recv_sems: _exchange_protocol(gs, outs, shapes, send_sems, recv_sems),
        operands=grads, out_types=[_sds((s[0], s[1] // 2, s[2]), F32) for s in shapes])


def _sum_halves(name, g, got, place, after):
    _, r, cdim = g.shape
    rh = r // 2
    rt = _row_tile(rh, 4 * N_CHIPS * cdim, 2 * ELEMENTWISE_BLOCK)
    nr = rh // rt

    def body(s_ref, g_ref, got_ref, after_ref, pb_ref, pf_ref):
        pb_ref[...] = (g_ref[...] + got_ref[...]).astype(BF16)
        mine = s_ref[1]
        pf_ref[...] = g_ref[mine] + got_ref[mine]

    quarters = (N_CHIPS, rt, cdim)
    grid_spec = pltpu.PrefetchScalarGridSpec(
        num_scalar_prefetch=1, grid=(nr,),
        in_specs=[pl.BlockSpec(quarters, lambda i, s: (0, s[0] * nr + i, 0)),
                  pl.BlockSpec(quarters, lambda i, s: (0, i, 0)),
                  pl.BlockSpec(memory_space=pl.ANY)],
        out_specs=[pl.BlockSpec(quarters, lambda i, s: (0, i, 0)),
                   pl.BlockSpec((rt, cdim), lambda i, s: (i, 0))])
    return pl.pallas_call(
        body, name=name, grid_spec=grid_spec,
        out_shape=[_sds((N_CHIPS, rh, cdim), BF16), _sds((rh, cdim), F32)],
        compiler_params=_params(("parallel",)))(place, g, got, after)


def _scatter_protocol(ps, outs, send_sems, recv_sems):
    x, y, c, other_chips = _place()
    sends = []
    for u in range(len(ps)):
        for k, chip in enumerate(other_chips):
            cp = pltpu.make_async_remote_copy(
                src_ref=ps[u].at[2 * chip[0] + chip[1]], dst_ref=outs[u].at[k],
                send_sem=send_sems.at[3 * u + k], recv_sem=recv_sems.at[3 * u + k],
                device_id=(*chip, c), device_id_type=MESH)
            cp.start()
            sends.append(cp)
    for cp in sends:
        cp.wait_recv()
    for cp in sends:
        cp.wait_send()


def _seq_scatter(name, collective_id, partials):
    return _on_sequencer(
        name, collective_id, 3 * len(partials), _chip_peers, _scatter_protocol,
        operands=partials, out_types=[_sds((3, p.shape[1], p.shape[2]), BF16) for p in partials])


def _sum_partials(name, own, got, place, layer, nl, prev, after):
    rh, cdim = own.shape
    rt = _row_tile(rh, 4 * cdim, ELEMENTWISE_BLOCK)
    nr = rh // rt
    after = list(after) if isinstance(after, (list, tuple)) else [after]

    def body(s_ref, own_ref, got_ref, *rest):
        o_ref = rest[-1]
        o_ref[...] = ((own_ref[...] + got_ref[0].astype(F32)) + got_ref[1].astype(F32)) + got_ref[2].astype(F32)

    in_specs = [pl.BlockSpec((rt, cdim), lambda i, s: (i, 0)), pl.BlockSpec((3, rt, cdim), lambda i, s: (0, i, 0)),
                *[pl.BlockSpec(memory_space=pl.ANY)] * len(after)]
    args = [place, own, got, *after]
    aliases = {}
    if prev is not None:
        in_specs.append(pl.BlockSpec(memory_space=pl.ANY))
        aliases = {len(args): 0}
        args.append(prev)
    grid_spec = pltpu.PrefetchScalarGridSpec(
        num_scalar_prefetch=1, grid=(nr,), in_specs=in_specs,
        out_specs=pl.BlockSpec((None, rt, cdim), lambda i, s: (layer, s[0] * nr + i, 0)))
    return pl.pallas_call(
        body, name=name, grid_spec=grid_spec, out_shape=_sds((nl, 2 * rh, cdim), F32),
        input_output_aliases=aliases, compiler_params=_params(("parallel",)))(*args)


def _share_protocol(outs, shapes, units, send_sems, recv_sems):
    x, y, c, _ = _place()
    sends = []
    for u, (w, l) in enumerate(units):
        mine = outs[w].at[l, _half_rows(c, shapes[w][1]), :]
        cp = pltpu.make_async_remote_copy(src_ref=mine, dst_ref=mine, send_sem=send_sems.at[u],
                                          recv_sem=recv_sems.at[u], device_id=(x, y, 1 - c), device_id_type=MESH)
        cp.start()
        sends.append(cp)
    for u, (w, l) in enumerate(units):
        theirs = outs[w].at[l, _half_rows(1 - c, shapes[w][1]), :]
        pltpu.make_async_remote_copy(src_ref=theirs, dst_ref=theirs, send_sem=send_sems.at[u],
                                     recv_sem=recv_sems.at[u], device_id=(x, y, 1 - c),
                                     device_id_type=MESH).wait_recv()
    for cp in sends:
        cp.wait_send()


def _seq_share(name, collective_id, bufs):
    shapes = [b.shape for b in bufs]
    units = [(w, l) for w in range(len(bufs)) for l in range(shapes[w][0])]
    refs = [_hbm_ref(b) for b in bufs]
    _on_sequencer(name, collective_id, len(units), _sibling_peer,
                  lambda ins, outs, send_sems, recv_sems: _share_protocol(refs, shapes, units, send_sems, recv_sems))
    return [r[...] for r in refs]


def _gather_blocks(block_ref, all_ref, send_sems, recv_sems):
    x, y, c, _ = _place()
    me = 4 * x + 2 * y + c
    all_ref[me] = block_ref[...]
    sends = []
    for rel in range(1, 8):
        fx, fy, fc = (rel >> 2) & 1, (rel >> 1) & 1, rel & 1
        peer = (x ^ fx, y ^ fy, c ^ fc)
        cp = pltpu.make_async_remote_copy(src_ref=block_ref, dst_ref=all_ref.at[me], send_sem=send_sems.at[rel - 1],
                                          recv_sem=recv_sems.at[rel - 1], device_id=peer, device_id_type=MESH)
        cp.start()
        sends.append(cp)
    for cp in sends:
        cp.wait_recv()
    for cp in sends:
        cp.wait_send()


def _adam(w, g, m, v):
    m_new = ADAM_B1 * m + (1.0 - ADAM_B1) * g
    v_new = ADAM_B2 * v + (1.0 - ADAM_B2) * (g * g)
    m_hat = m_new / (1.0 - ADAM_B1 ** ADAM_STEP)
    v_hat = v_new / (1.0 - ADAM_B2 ** ADAM_STEP)
    delta = -ADAM_LR * (m_hat / (jnp.sqrt(v_hat) + ADAM_EPS) + ADAM_WD * w)
    return delta, m_new, v_new


def _small_step(dnm0, dnm1, dnf0, dnf1, dcw, dqg, dkg, dsk, loss, w_blk, m_blk, v_blk, cw_cols):
    d = w_blk.shape[1]
    vm = pl.BlockSpec(memory_space=pltpu.VMEM)

    def reduce_body(dnm0_ref, dnm1_ref, dnf0_ref, dnf1_ref, dcw_ref, dqg_ref, dkg_ref, dsk_ref, loss_ref,
                    g_ref, blk_ref, all_ref, send_sems, recv_sems):
        blk_ref[...] = jnp.zeros_like(blk_ref)
        for row, part_ref in ((SENT_NORM_MIXER, dnm0_ref), (SENT_NORM_MIXER + 1, dnm1_ref),
                              (SENT_NORM_FFN, dnf0_ref), (SENT_NORM_FFN + 1, dnf1_ref)):
            blk_ref[row:row + 1, :] = jnp.sum(part_ref[...], axis=0, keepdims=True)
        blk_ref[SENT_CONV_W:SENT_CONV_W + 3, :] = dcw_ref[...]
        misc = slice(SENT_MISC, SENT_MISC + 1)
        for tile, gain_ref in ((TILE_Q_GAIN, dqg_ref), (TILE_K_GAIN, dkg_ref)):
            pair = gain_ref[...]
            blk_ref[misc, tile * LANES:(tile + 1) * LANES] = pair + pltpu.roll(pair, HEAD_DIM, 1)
        for h in range(N_Q_HEADS):
            lane = TILE_SINKS * LANES + h
            blk_ref[misc, lane:lane + 1] = jnp.sum(dsk_ref[h:h + 1, :], axis=1, keepdims=True)
        blk_ref[misc, TILE_LOSS * LANES:(TILE_LOSS + 1) * LANES] = jnp.broadcast_to(loss_ref[...], (1, LANES))
        _gather_blocks(blk_ref, all_ref, send_sems, recv_sems)
        g = all_ref[0]
        for dev in range(1, 8):
            g = g + all_ref[dev]
        g_ref[...] = jnp.zeros_like(g_ref)
        for sent, row, n in ((SENT_NORM_MIXER, ROW_NORM_MIXER, 2), (SENT_NORM_FFN, ROW_NORM_FFN, 2),
                             (SENT_CONV_W, ROW_CONV_W, 3), (SENT_MISC, ROW_MISC, 1)):
            g_ref[row:row + n, :] = g[sent:sent + n]

    g_blk = pl.pallas_call(
        reduce_body, name="small_allreduce", in_specs=[vm] * 9, out_specs=vm, out_shape=_sds((SMALL_ROWS, d), F32),
        scratch_shapes=[pltpu.VMEM((SUBLANES, d), F32), pltpu.VMEM((8, SUBLANES, d), F32),
                        pltpu.SemaphoreType.DMA((7,)), pltpu.SemaphoreType.DMA((7,))],
    )(dnm0, dnm1, dnf0, dnf1, dcw, dqg, dkg, dsk, loss)

    def body(g_ref, w_ref, m_ref, v_ref, *out_refs):
        g = g_ref[...]
        misc = slice(ROW_MISC, ROW_MISC + 1)
        out_refs[0][...] = g[misc, TILE_LOSS * LANES:TILE_LOSS * LANES + 1]
        chip = 2 * lax.axis_index("x") + lax.axis_index("y")
        for i, blk in enumerate((g, *_adam(w_ref[...], g, m_ref[...], v_ref[...]))):
            nm_ref, nf_ref, cw_ref, qg_ref, kg_ref, sk_ref = out_refs[1 + 6 * i:7 + 6 * i]
            nm_ref[...] = blk[ROW_NORM_MIXER:ROW_NORM_MIXER + 2]
            nf_ref[...] = blk[ROW_NORM_FFN:ROW_NORM_FFN + 2]
            qg_ref[...] = blk[misc, TILE_Q_GAIN * LANES:TILE_Q_GAIN * LANES + HEAD_DIM]
            kg_ref[...] = blk[misc, TILE_K_GAIN * LANES:TILE_K_GAIN * LANES + HEAD_DIM]
            sk_ref[...] = blk[misc, TILE_SINKS * LANES:TILE_SINKS * LANES + N_Q_HEADS]
            for q in range(N_CHIPS):
                @pl.when(chip == q)
                def _(blk=blk, cw_ref=cw_ref, q=q):
                    cw_ref[0] = blk[ROW_CONV_W:ROW_CONV_W + 3, q * cw_cols:(q + 1) * cw_cols]

    group = [_sds((2, d), F32), _sds((2, d), F32), _sds((1, 3, cw_cols), F32), _sds((1, HEAD_DIM), F32),
             _sds((1, HEAD_DIM), F32), _sds((1, N_Q_HEADS), F32)]
    outs = pl.pallas_call(
        body, name="small_adam", in_specs=[vm] * 4, out_specs=[vm] * 25, out_shape=[_sds((1, 1), F32)] + group * 4,
    )(g_blk, w_blk, m_blk, v_blk)
    names = ("norm_mixer", "norm_ffn", "conv_w", "attn_q_gain", "attn_k_gain", "attn_sinks")
    return outs[0], [dict(zip(names, outs[1 + 6 * i:7 + 6 * i])) for i in range(4)]


def _adam_step(name, w, g, m, v):
    nl, r, cdim = w.shape
    rt = _row_tile(r, 4 * cdim, ELEMENTWISE_BLOCK)

    def body(w_ref, g_ref, m_ref, v_ref, go_ref, d_ref, mo_ref, vo_ref):
        gv = g_ref[...]
        go_ref[...] = gv
        delta, m_new, v_new = _adam(w_ref[...], gv, m_ref[...], v_ref[...])
        d_ref[...] = delta
        mo_ref[...] = m_new
        vo_ref[...] = v_new

    spec = pl.BlockSpec((None, rt, cdim), lambda l, i: (l, i, 0))
    return pl.pallas_call(
        body, name=name, grid=(nl, r // rt), in_specs=[spec] * 4, out_specs=[spec] * 4,
        out_shape=[_sds(w.shape, F32)] * 4,
        compiler_params=_params(("parallel", "parallel")))(w, g, m, v)


def _pad_rows(a, rows=SUBLANES):
    return jnp.pad(a, ((0, rows - a.shape[0]), (0, 0)))


def _small_block(nm, nf, cw_local, qg, kg, sk, chip):
    d = nm.shape[1]
    cw_rows = lax.dynamic_update_slice(jnp.zeros((SUBLANES, d), F32), cw_local, (0, chip * cw_local.shape[1]))
    misc = jnp.concatenate([qg, qg, kg, kg, jnp.pad(sk, ((0, 0), (0, LANES - sk.shape[1]))),
                            jnp.zeros((1, d - 3 * LANES), F32)], axis=1)
    return jnp.concatenate([_pad_rows(nm), _pad_rows(nf), cw_rows, _pad_rows(misc)], axis=0)


WEIGHT_NAMES = ("conv_w_in", "conv_w", "conv_w_out", "attn_w_qkv", "attn_q_gain", "attn_k_gain", "attn_sinks",
                "attn_w_o", "norm_mixer", "norm_ffn", "ffn_w_gate_up", "ffn_w_down")
BIG = ("conv_w_in", "conv_w_out", "attn_w_qkv", "attn_w_o", "ffn_w_gate_up", "ffn_w_down")


def kernel(x, conv_w_in, conv_w, conv_w_out, attn_w_qkv, attn_q_gain, attn_k_gain, attn_sinks, attn_w_o, norm_mixer, norm_ffn, ffn_w_gate_up, ffn_w_down, loss_target, m_conv_w_in, m_conv_w, m_conv_w_out, m_attn_w_qkv, m_attn_q_gain, m_attn_k_gain, m_attn_sinks, m_attn_w_o, m_norm_mixer, m_norm_ffn, m_ffn_w_gate_up, m_ffn_w_down, v_conv_w_in, v_conv_w, v_conv_w_out, v_attn_w_qkv, v_attn_q_gain, v_attn_k_gain, v_attn_sinks, v_attn_w_o, v_norm_mixer, v_norm_ffn, v_ffn_w_gate_up, v_ffn_w_down):
    w = dict(conv_w_in=conv_w_in, conv_w=conv_w, conv_w_out=conv_w_out, attn_w_qkv=attn_w_qkv,
             attn_q_gain=attn_q_gain, attn_k_gain=attn_k_gain, attn_sinks=attn_sinks, attn_w_o=attn_w_o,
             norm_mixer=norm_mixer, norm_ffn=norm_ffn, ffn_w_gate_up=ffn_w_gate_up, ffn_w_down=ffn_w_down)
    m = dict(conv_w_in=m_conv_w_in, conv_w=m_conv_w, conv_w_out=m_conv_w_out, attn_w_qkv=m_attn_w_qkv,
             attn_q_gain=m_attn_q_gain, attn_k_gain=m_attn_k_gain, attn_sinks=m_attn_sinks, attn_w_o=m_attn_w_o,
             norm_mixer=m_norm_mixer, norm_ffn=m_norm_ffn, ffn_w_gate_up=m_ffn_w_gate_up, ffn_w_down=m_ffn_w_down)
    v = dict(conv_w_in=v_conv_w_in, conv_w=v_conv_w, conv_w_out=v_conv_w_out, attn_w_qkv=v_attn_w_qkv,
             attn_q_gain=v_attn_q_gain, attn_k_gain=v_attn_k_gain, attn_sinks=v_attn_sinks, attn_w_o=v_attn_w_o,
             norm_mixer=v_norm_mixer, norm_ffn=v_norm_ffn, ffn_w_gate_up=v_ffn_w_gate_up, ffn_w_down=v_ffn_w_down)

    nseq, seq, d = x.shape
    t = nseq * seq
    chip = 2 * lax.axis_index("x") + lax.axis_index("y")
    core = lax.axis_index("c")
    place = jnp.stack([core, chip]).astype(jnp.int32)
    x0 = x.reshape(t, d)
    tgt = loss_target.reshape(t, d)

    cw_block = lax.dynamic_update_slice(jnp.zeros((SUBLANES, d), F32), conv_w[0], (0, chip * conv_w.shape[2]))
    def cast(k, layer=None):
        return _cast_own(f"cast_{k}" + ("" if layer is None else str(layer)), w[k], place, layer)

    (w_in,), cw_got = _seq_allgather_conv(1, [cast("conv_w_in")], cw_block)
    w_out, w_gu0, w_dn0 = _seq_allgather(
        "allgather_ffn0", 2, [cast("conv_w_out"), cast("ffn_w_gate_up", 0), cast("ffn_w_down", 0)])
    w_qkv, w_o, w_gu1, w_dn1 = _seq_allgather(
        "allgather_rest", 3, [cast("attn_w_qkv"), cast("attn_w_o"), cast("ffn_w_gate_up", 1), cast("ffn_w_down", 1)])
    w_out = w_out.reshape(1, d, d)
    w_o = w_o.reshape(1, d, d)
    w_gu = [w_gu0, w_gu1]
    w_dn = [w_dn0.reshape(1, D_FF, d), w_dn1.reshape(1, D_FF, d)]

    qg_pair = jnp.concatenate([attn_q_gain, attn_q_gain], axis=1)
    kg_pair = jnp.concatenate([attn_k_gain, attn_k_gain], axis=1)

    def ffn_bwd(i, dxo, xin, h, g, u, a):
        g_dn = _wgrad_down(f"ffn{i}_down_wgrad", a, dxo, D_FF // 2)
        dg, du = _mm_down_t_swiglu(f"ffn{i}_down_dgrad", dxo, w_dn[i], 0, g, u)
        g_gu = _wgrad_up2(f"ffn{i}_up_wgrad", h, dg, du)
        dxi, dgain = _dgrad_norm_ffn(f"ffn{i}_up_dgrad", dg, du, w_gu[i], 0, xin, norm_ffn[i:i + 1], dxo)
        return dxi, dgain, g_gu, g_dn

    h0, bcx = _mm_norm_up_joined("conv_in", x0, norm_mixer[0:1], w_in, 512)
    z = _conv_fwd(bcx, cw_block, cw_got, nseq, seq)
    x1, h1 = _mm_down_norm("conv_out", z, w_out, 0, x0, norm_ffn[0:1])
    g0, u0, a0 = _mm_up_swiglu("ffn0_up", h1, w_gu[0], 0)
    x2, h2 = _mm_down_norm("ffn0_down", a0, w_dn[0], 0, x1, norm_mixer[1:2])
    qkv = _mm_up_joined("attn_qkv", h2, w_qkv, 1024)
    o = _attn_fwd(qkv, qg_pair, kg_pair, attn_sinks, nseq, seq)
    x3, h3 = _mm_down_norm("attn_out", o, w_o, 0, x2, norm_ffn[1:2])
    g1, u1, a1 = _mm_up_swiglu("ffn1_up", h3, w_gu[1], 0)
    dy, loss_part = _mm_down_loss("ffn1_down", a1, w_dn[1], 0, x3, tgt)

    finished = {k: None for k in BIG}

    def exchange(tag, cid, units):
        return units, _seq_exchange(f"exchange_{tag}", cid, [g for _, _, g in units])

    def scatter(tag, cid, group, after):
        units, got = group
        sums = [_sum_halves(f"sum_halves_{k}{l}", g, r, place, after) for (k, l, g), r in zip(units, got)]
        return units, sums, _seq_scatter(f"scatter_{tag}", cid, [pb for pb, _ in sums])

    def finish(group, after):
        units, sums, arrived = group
        for (k, l, _), (_, pf), r in zip(units, sums, arrived):
            finished[k] = _sum_partials(f"sum_partials_{k}{l}", pf, r, place, l, w[k].shape[0], finished[k], after)

    dx3, dnf1, g_gu1, g_dn1 = ffn_bwd(1, dy, x3, h3, g1, u1, a1)
    ffn1 = exchange("ffn1", 4, [("ffn_w_down", 1, g_dn1), ("ffn_w_gate_up", 1, g_gu1)])
    g_o = _wgrad_down("attn_out_wgrad", o, dx3, d)
    do = _mm_down_t("attn_out_dgrad", dx3, w_o, 0)
    ffn1 = scatter("ffn1", 8, ffn1, do)
    dqkv, dqg, dkg, dsk = _attn_bwd(do, qkv, qg_pair, kg_pair, attn_sinks, nseq, seq)
    g_qkv = _wgrad_joined("attn_qkv_wgrad", h2, dqkv)
    attn = exchange("attn", 5, [("attn_w_o", 0, g_o), ("attn_w_qkv", 0, g_qkv)])
    dx2, dnm1 = _dgrad_norm_qkv("attn_qkv_dgrad", dqkv, w_qkv, x2, norm_mixer[1:2], dx3)
    finish(ffn1, dx2)
    attn = scatter("attn", 9, attn, dx2)
    dx1, dnf0, g_gu0, g_dn0 = ffn_bwd(0, dx2, x1, h1, g0, u0, a0)
    ffn0 = exchange("ffn0", 6, [("ffn_w_down", 0, g_dn0), ("ffn_w_gate_up", 0, g_gu0)])
    g_out = _wgrad_down("conv_out_wgrad", z, dx1, d)
    dz = _mm_down_t("conv_out_dgrad", dx1, w_out, 0)
    finish(attn, dz)
    ffn0 = scatter("ffn0", 10, ffn0, dz)
    dbcx, dcw = _conv_bwd(dz, bcx, cw_block, cw_got, nseq, seq)
    g_in = _wgrad_conv_in("conv_in_wgrad", h0, dbcx, conv_w_in.shape[2])
    conv = exchange("conv", 7, [("conv_w_out", 0, g_out), ("conv_w_in", 0, g_in)])
    dx0, dnm0 = _dgrad_norm_conv("conv_in_dgrad", dbcx, w_in, x0, norm_mixer[0:1], dx1)
    finish(ffn0, dx0)
    late = ("attn_w_qkv", "attn_w_o", "ffn_w_gate_up", "ffn_w_down")
    grads_late = _seq_share("share_late", 12, [finished[k] for k in late])
    conv = scatter("conv", 11, conv, dx0)

    grad, delta, new_m, new_v = {}, {}, {}, {}

    def adam(k, g):
        grad[k], delta[k], new_m[k], new_v[k] = _adam_step(f"adam_{k}", w[k], g, m[k], v[k])

    for k, g in zip(late, grads_late):
        adam(k, g)

    def blocks(src):
        return _small_block(src["norm_mixer"], src["norm_ffn"], src["conv_w"][0], src["attn_q_gain"],
                            src["attn_k_gain"], src["attn_sinks"], chip)

    loss, small = _small_step(dnm0, dnm1, dnf0, dnf1, dcw, dqg, dkg, dsk, loss_part,
                              blocks(w), blocks(m), blocks(v), conv_w.shape[2])
    for dst, part in zip((grad, delta, new_m, new_v), small):
        dst.update(part)

    finish(conv, [new_v[k] for k in late])
    last = ("conv_w_in", "conv_w_out")
    for k, g in zip(last, _seq_share("share_last", 13, [finished[k] for k in last])):
        adam(k, g)

    return (loss.reshape(()), dx0.reshape(nseq, seq, d), *[grad[k] for k in WEIGHT_NAMES], *[delta[k] for k in WEIGHT_NAMES],
            *[new_m[k] for k in WEIGHT_NAMES], *[new_v[k] for k in WEIGHT_NAMES])
```

```python
import jax
import jax.numpy as jnp
from jax import lax
from jax.experimental import pallas as pl
from jax.experimental.pallas import tpu as pltpu
from jax.experimental.pallas import tpu_sc as plsc

F32 = jnp.float32
BF16 = jnp.bfloat16

D_FF = 2816
N_Q_HEADS = 16
N_KV_HEADS = 4
HEAD_DIM = 64
WINDOW = 128
BLOCK = 128
EPS = 1e-6
N_CHIPS = 4
LANES = 128
SUBLANES = 8
BF16_ROWS = 16
MXU_COLS = 256
VMEM_LIMIT = 48 * 1024 * 1024
ADAM_LR, ADAM_B1, ADAM_B2, ADAM_EPS, ADAM_WD, ADAM_STEP = 0.001, 0.9, 0.999, 1e-08, 0.01, 10
ALIBI_SLOPES = tuple(2.0 ** (-8.0 * (h + 1) / N_Q_HEADS) for h in range(N_Q_HEADS))
SMALL_ROWS = 32
ROW_NORM_MIXER, ROW_NORM_FFN, ROW_CONV_W, ROW_MISC = 0, 8, 16, 24
SENT_NORM_MIXER, SENT_NORM_FFN, SENT_CONV_W, SENT_MISC = 0, 2, 4, 7
TILE_Q_GAIN, TILE_K_GAIN, TILE_SINKS, TILE_LOSS = 0, 1, 2, 3
MESH = pl.DeviceIdType.MESH

NN = ((1,), (0,))
NT = ((1,), (1,))
TN = ((0,), (0,))


def _dot(a, b, dims):
    return lax.dot_general(a, b, (dims, ((), ())), preferred_element_type=F32)


def _pick(n, cands):
    for c in cands:
        if n % c == 0:
            return c
    raise ValueError((n, cands))


def _row_tile(rows, row_bytes, cap_bytes):
    fits = [r for r in range(BF16_ROWS, rows + 1, BF16_ROWS) if rows % r == 0 and r * row_bytes <= cap_bytes]
    if not fits:
        raise ValueError((rows, row_bytes, cap_bytes))
    return fits[-1]


ELEMENTWISE_BLOCK = 3 << 19


def _resident(block_shape, index_map):
    return pl.BlockSpec(block_shape, index_map, pipeline_mode=pl.Buffered(1))


def _params(sem):
    return pltpu.CompilerParams(dimension_semantics=sem, vmem_limit_bytes=VMEM_LIMIT)


def _sds(shape, dtype):
    return jax.ShapeDtypeStruct(shape, dtype)


def _rms(xv):
    return lax.rsqrt(jnp.mean(xv * xv, axis=-1, keepdims=True) + EPS)


def _sigmoid(g):
    return 1.0 / (1.0 + jnp.exp(-g))


def _mm_up_joined(name, a, w4, tm_pref):
    t, k = a.shape
    _, _, _, nq = w4.shape
    tm = _pick(t, (tm_pref, 256, 128))

    def body(a_ref, w_ref, o_ref, wcat_ref):
        @pl.when(pl.program_id(0) == 0)
        def _():
            for q in range(N_CHIPS):
                wcat_ref[:, q * nq:(q + 1) * nq] = w_ref[q]

        o_ref[...] = _dot(a_ref[...], wcat_ref[...], NN).astype(BF16)

    return pl.pallas_call(
        body, name=name, grid=(t // tm,),
        in_specs=[pl.BlockSpec((tm, k), lambda i: (i, 0)),
                  pl.BlockSpec((None, N_CHIPS, k, nq), lambda i: (0, 0, 0, 0))],
        out_specs=pl.BlockSpec((tm, N_CHIPS * nq), lambda i: (i, 0)),
        out_shape=_sds((t, N_CHIPS * nq), BF16),
        scratch_shapes=[pltpu.VMEM((k, N_CHIPS * nq), BF16)],
        compiler_params=_params(("arbitrary",)))(a, w4)


def _mm_norm_up_joined(name, x, gain, w4, tm_pref):
    t, k = x.shape
    _, _, _, nq = w4.shape
    tm = _pick(t, (tm_pref, 256, 128))

    def body(x_ref, g_ref, w_ref, h_ref, o_ref, wcat_ref):
        @pl.when(pl.program_id(0) == 0)
        def _():
            for q in range(N_CHIPS):
                wcat_ref[:, q * nq:(q + 1) * nq] = w_ref[q]

        xv = x_ref[...]
        h = ((xv * _rms(xv)) * g_ref[...]).astype(BF16)
        h_ref[...] = h
        o_ref[...] = _dot(h, wcat_ref[...], NN).astype(BF16)

    return pl.pallas_call(
        body, name=name, grid=(t // tm,),
        in_specs=[pl.BlockSpec((tm, k), lambda i: (i, 0)), pl.BlockSpec((1, k), lambda i: (0, 0)),
                  _resident((None, N_CHIPS, k, nq), lambda i: (0, 0, 0, 0))],
        out_specs=[pl.BlockSpec((tm, k), lambda i: (i, 0)), pl.BlockSpec((tm, N_CHIPS * nq), lambda i: (i, 0))],
        out_shape=[_sds((t, k), BF16), _sds((t, N_CHIPS * nq), BF16)],
        scratch_shapes=[pltpu.VMEM((k, N_CHIPS * nq), BF16)],
        compiler_params=_params(("arbitrary",)))(x, gain, w4)


def _mm_up_swiglu(name, h, w4, layer):
    t, k = h.shape
    _, _, _, nq = w4.shape
    tm = _pick(t, (512, 256, 128))

    def body(h_ref, wg_ref, wu_ref, dag_ref, dau_ref, a_ref):
        hv = h_ref[...]
        g = _dot(hv, wg_ref[...], NN)
        u = _dot(hv, wu_ref[...], NN)
        sg = _sigmoid(g)
        silu = g * sg
        a = silu * u
        dag_ref[...] = (a + sg * (u - a)).astype(BF16)
        dau_ref[...] = silu.astype(BF16)
        a_ref[...] = a.astype(BF16)

    half = N_CHIPS // 2
    out = pl.BlockSpec((tm, nq), lambda j, i: (i, j))
    return pl.pallas_call(
        body, name=name, grid=(half, t // tm),
        in_specs=[pl.BlockSpec((tm, k), lambda j, i: (i, 0)),
                  pl.BlockSpec((None, None, k, nq), lambda j, i: (layer, j, 0, 0)),
                  pl.BlockSpec((None, None, k, nq), lambda j, i: (layer, half + j, 0, 0))],
        out_specs=[out, out, out],
        out_shape=[_sds((t, half * nq), BF16)] * 3,
        compiler_params=_params(("parallel", "parallel")))(h, w4, w4)


def _mm_down_norm(name, a, w, layer, res, gain):
    t, kf = a.shape
    _, _, n = w.shape
    tm = _pick(t, (1024, 512, 256, 128))

    def body(a_ref, w_ref, r_ref, g_ref, o_ref, h_ref):
        xo = r_ref[...] + _dot(a_ref[...], w_ref[...], NN)
        o_ref[...] = xo
        h_ref[...] = ((xo * _rms(xo)) * g_ref[...]).astype(BF16)

    row = pl.BlockSpec((tm, n), lambda i: (i, 0))
    return pl.pallas_call(
        body, name=name, grid=(t // tm,),
        in_specs=[pl.BlockSpec((tm, kf), lambda i: (i, 0)),
                  _resident((None, kf, n), lambda i: (layer, 0, 0)),
                  row, pl.BlockSpec((1, n), lambda i: (0, 0))],
        out_specs=[row, row],
        out_shape=[_sds((t, n), F32), _sds((t, n), BF16)],
        compiler_params=_params(("parallel",)))(a, w, res, gain)


def _mm_down_loss(name, a, w, layer, res, tgt):
    t, kf = a.shape
    _, _, n = w.shape
    tm = _pick(t, (1024, 512, 256, 128))
    steps = t // tm

    def body(a_ref, w_ref, r_ref, t_ref, dy_ref, l_ref, acc_ref):
        i = pl.program_id(0)

        @pl.when(i == 0)
        def _():
            acc_ref[...] = jnp.zeros_like(acc_ref)

        e = (r_ref[...] + _dot(a_ref[...], w_ref[...], NN)) - t_ref[...]
        dy_ref[...] = e * (1.0 / n)
        acc_ref[...] += (e * e).reshape(tm // SUBLANES, SUBLANES, n).sum(axis=0)

        @pl.when(i == steps - 1)
        def _():
            l_ref[...] = jnp.sum(acc_ref[...], keepdims=True) * (0.5 / n)

    row = pl.BlockSpec((tm, n), lambda i: (i, 0))
    return pl.pallas_call(
        body, name=name, grid=(steps,),
        in_specs=[pl.BlockSpec((tm, kf), lambda i: (i, 0)),
                  _resident((None, kf, n), lambda i: (layer, 0, 0)), row, row],
        out_specs=[row, pl.BlockSpec((1, 1), lambda i: (0, 0))],
        out_shape=[_sds((t, n), F32), _sds((1, 1), F32)],
        scratch_shapes=[pltpu.VMEM((SUBLANES, n), F32)],
        compiler_params=_params(("arbitrary",)))(a, w, res, tgt)


def _mm_down_t(name, dx, w, layer):
    t, n = dx.shape
    _, kf, _ = w.shape
    tm = _pick(t, (1024, 512, 256, 128))

    def body(a_ref, w_ref, o_ref):
        o_ref[...] = _dot(a_ref[...].astype(BF16), w_ref[...], NT).astype(BF16)

    return pl.pallas_call(
        body, name=name, grid=(t // tm,),
        in_specs=[pl.BlockSpec((tm, n), lambda i: (i, 0)),
                  _resident((None, kf, n), lambda i: (layer, 0, 0))],
        out_specs=pl.BlockSpec((tm, kf), lambda i: (i, 0)),
        out_shape=_sds((t, kf), BF16),
        compiler_params=_params(("parallel",)))(dx, w)


def _mm_down_t_swiglu(name, dx, w, layer, g, u):
    t, n = dx.shape
    f = g.shape[1]
    tm = _pick(t, (512, 256, 128))

    def body(a_ref, w_ref, dag_ref, dau_ref, dg_ref, du_ref):
        da = _dot(a_ref[...].astype(BF16), w_ref[...], NT)
        dg_ref[...] = (da * dag_ref[...].astype(F32)).astype(BF16)
        du_ref[...] = (da * dau_ref[...].astype(F32)).astype(BF16)

    tile = pl.BlockSpec((tm, f), lambda i: (i, 0))
    return pl.pallas_call(
        body, name=name, grid=(t // tm,),
        in_specs=[pl.BlockSpec((tm, n), lambda i: (i, 0)),
                  _resident((None, f, n), lambda i: (layer, 0, 0)), tile, tile],
        out_specs=[tile, tile],
        out_shape=[_sds((t, f), BF16)] * 2,
        compiler_params=_params(("parallel",)))(dx, w, g, u)


def _dgrad_norm(name, acts, act_blocks, pieces, w4, layer, x, gain, dres):
    t, d = x.shape
    _, _, k, nq = w4.shape
    tm = _pick(t, (512, 256, 128))
    n_act = len(acts)

    def body(*refs):
        act_refs = refs[:n_act]
        w_ref, x_ref, g_ref, dr_ref, dx_ref, dg_ref = refs[n_act:]

        @pl.when(pl.program_id(0) == 0)
        def _():
            dg_ref[...] = jnp.zeros_like(dg_ref)

        dh = None
        for a_tile, w_tile in pieces(act_refs, w_ref):
            term = _dot(a_tile, w_tile, NT)
            dh = term if dh is None else dh + term
        xv = x_ref[...]
        r = _rms(xv)
        xhat = xv * r
        gd = dh * g_ref[...]
        dx_ref[...] = dr_ref[...] + r * (gd - xhat * jnp.mean(gd * xhat, axis=-1, keepdims=True))
        dg_ref[...] += (dh * xhat).reshape(tm // SUBLANES, SUBLANES, d).sum(axis=0)

    row = pl.BlockSpec((tm, d), lambda i: (i, 0))
    return pl.pallas_call(
        body, name=name, grid=(t // tm,),
        in_specs=[*act_blocks(tm),
                  _resident((None, N_CHIPS, k, nq), lambda i: (layer, 0, 0, 0)),
                  row, pl.BlockSpec((1, d), lambda i: (0, 0)), row],
        out_specs=[row, pl.BlockSpec((SUBLANES, d), lambda i: (0, 0))],
        out_shape=[_sds((t, d), F32), _sds((SUBLANES, d), F32)],
        compiler_params=_params(("arbitrary",)))(*acts, w4, x, gain, dres)


def _dgrad_norm_ffn(name, dg, du, w4, layer, x, gain, dres):
    nq = w4.shape[3]
    f = dg.shape[1]

    def blocks(tm):
        return [pl.BlockSpec((tm, f), lambda i: (i, 0))] * 2

    def pieces(act_refs, w_ref):
        dg_ref, du_ref = act_refs
        return [(dg_ref[:, 0:nq], w_ref[0]), (dg_ref[:, nq:2 * nq], w_ref[1]),
                (du_ref[:, 0:nq], w_ref[2]), (du_ref[:, nq:2 * nq], w_ref[3])]

    return _dgrad_norm(name, [dg, du], blocks, pieces, w4, layer, x, gain, dres)


def _dgrad_norm_qkv(name, dqkv, w4, x, gain, dres):
    nq = w4.shape[3]

    def blocks(tm):
        return [pl.BlockSpec((tm, N_CHIPS * nq), lambda i: (i, 0))]

    def pieces(act_refs, w_ref):
        return [(act_refs[0][:, q * nq:(q + 1) * nq], w_ref[q]) for q in range(N_CHIPS)]

    return _dgrad_norm(name, [dqkv], blocks, pieces, w4, 0, x, gain, dres)


def _dgrad_norm_conv(name, d3, w4, x, gain, dres):
    _, _, d = d3.shape
    nq = w4.shape[3]
    per_part, per_q = d // MXU_COLS, nq // MXU_COLS

    def blocks(tm):
        return [pl.BlockSpec((3, tm, d), lambda i: (0, i, 0))]

    def pieces(act_refs, w_ref):
        out = []
        for jb in range(3 * per_part):
            ca, cw = (jb % per_part) * MXU_COLS, (jb % per_q) * MXU_COLS
            out.append((act_refs[0][jb // per_part, :, ca:ca + MXU_COLS], w_ref[jb // per_q, :, cw:cw + MXU_COLS]))
        return out

    return _dgrad_norm(name, [d3], blocks, pieces, w4, 0, x, gain, dres)


def _wgrad_up2(name, h, dg, du):
    t, k = h.shape
    nq = dg.shape[1] // 2
    tk = _pick(t, (2048, 1024, 512, 256, 128))
    steps = t // tk
    half = N_CHIPS // 2

    def body(h_ref, dg_ref, du_ref, o_ref):
        q = pl.program_id(0)

        @pl.when(pl.program_id(1) == 0)
        def _():
            o_ref[...] = jnp.zeros_like(o_ref)

        @pl.when(q < half)
        def _():
            o_ref[...] += _dot(h_ref[...], dg_ref[...], TN)

        @pl.when(q >= half)
        def _():
            o_ref[...] += _dot(h_ref[...], du_ref[...], TN)

    return pl.pallas_call(
        body, name=name, grid=(N_CHIPS, steps),
        in_specs=[pl.BlockSpec((tk, k), lambda q, s: (s, 0)),
                  pl.BlockSpec((tk, nq), lambda q, s: (jnp.where(q < half, s, steps - 1), jnp.minimum(q, half - 1))),
                  pl.BlockSpec((tk, nq), lambda q, s: (jnp.where(q >= half, s, 0), jnp.maximum(q - half, 0)))],
        out_specs=pl.BlockSpec((None, k, nq), lambda q, s: (q, 0, 0)),
        out_shape=_sds((N_CHIPS, k, nq), F32),
        compiler_params=_params(("parallel", "arbitrary")))(h, dg, du)


def _wgrad_joined(name, h, dy):
    t, k = h.shape
    nq = dy.shape[1] // N_CHIPS
    tk = _pick(t, (2048, 1024, 512, 256, 128))

    def body(h_ref, dy_ref, o_ref):
        @pl.when(pl.program_id(0) == 0)
        def _():
            o_ref[...] = jnp.zeros_like(o_ref)

        res = _dot(h_ref[...], dy_ref[...], TN)
        for q in range(N_CHIPS):
            o_ref[q] += res[:, q * nq:(q + 1) * nq]

    return pl.pallas_call(
        body, name=name, grid=(t // tk,),
        in_specs=[pl.BlockSpec((tk, k), lambda s: (s, 0)), pl.BlockSpec((tk, N_CHIPS * nq), lambda s: (s, 0))],
        out_specs=pl.BlockSpec((N_CHIPS, k, nq), lambda s: (0, 0, 0)),
        out_shape=_sds((N_CHIPS, k, nq), F32),
        compiler_params=_params(("arbitrary",)))(h, dy)


def _wgrad_conv_in(name, h, d3, nq):
    t, k = h.shape
    d = d3.shape[2]
    per_part, per_q = d // MXU_COLS, nq // MXU_COLS
    tk = _pick(t, (512, 256, 128))

    def body(h_ref, d_ref, o_ref):
        @pl.when(pl.program_id(0) == 0)
        def _():
            o_ref[...] = jnp.zeros_like(o_ref)

        hv = h_ref[...]
        for part in range(3):
            res = _dot(hv, d_ref[part], TN)
            for cc in range(per_part):
                jb = part * per_part + cc
                co = (jb % per_q) * MXU_COLS
                o_ref[jb // per_q, :, co:co + MXU_COLS] += res[:, cc * MXU_COLS:(cc + 1) * MXU_COLS]

    return pl.pallas_call(
        body, name=name, grid=(t // tk,),
        in_specs=[pl.BlockSpec((tk, k), lambda s: (s, 0)), pl.BlockSpec((3, tk, d), lambda s: (0, s, 0))],
        out_specs=pl.BlockSpec((N_CHIPS, k, nq), lambda s: (0, 0, 0)),
        out_shape=_sds((N_CHIPS, k, nq), F32),
        compiler_params=_params(("arbitrary",)))(h, d3)


def _wgrad_down(name, a, dx, tmw):
    t, kf = a.shape
    n = dx.shape[1]
    tk = _pick(t, (2048, 1024, 512, 256, 128))

    def body(a_ref, b_ref, o_ref):
        @pl.when(pl.program_id(1) == 0)
        def _():
            o_ref[...] = jnp.zeros_like(o_ref)

        o_ref[...] += _dot(a_ref[...], b_ref[...].astype(BF16), TN)

    g = pl.pallas_call(
        body, name=name, grid=(kf // tmw, t // tk),
        in_specs=[pl.BlockSpec((tk, tmw), lambda j, s: (s, j)), pl.BlockSpec((tk, n), lambda j, s: (s, 0))],
        out_specs=pl.BlockSpec((tmw, n), lambda j, s: (j, 0)),
        out_shape=_sds((kf, n), F32),
        compiler_params=_params(("parallel", "arbitrary")))(a, dx)
    return g.reshape(N_CHIPS, kf // N_CHIPS, n)


def _shift_rows(u, k, rows):
    s = u.shape[0]
    if k > 0:
        r = pltpu.roll(u, k, 0)
        return jnp.concatenate([jnp.where(rows >= k, r[0:SUBLANES], 0.0), r[SUBLANES:]], axis=0)
    r = pltpu.roll(u, s + k, 0)
    return jnp.concatenate([r[:s - SUBLANES], jnp.where(rows < SUBLANES + k, r[s - SUBLANES:], 0.0)], axis=0)


def _conv_taps(cw_ref, got_ref):
    return (cw_ref[...] + got_ref[0]) + (got_ref[1] + got_ref[2])


def _conv_fwd(bcx, cw, cw_got, nseq, seq):
    t, d3 = bcx.shape
    d = d3 // 3
    cb = 2 * MXU_COLS
    nj = d // cb

    def body(b_ref, c_ref, x_ref, cw_ref, got_ref, z_ref):
        u = b_ref[...].astype(F32) * x_ref[...].astype(F32)
        rows = lax.broadcasted_iota(jnp.int32, (SUBLANES, cb), 0)
        cwv = _conv_taps(cw_ref, got_ref)
        y = cwv[2:3] * u + cwv[1:2] * _shift_rows(u, 1, rows) + cwv[0:1] * _shift_rows(u, 2, rows)
        z_ref[...] = (c_ref[...].astype(F32) * y).astype(BF16)

    return pl.pallas_call(
        body, name="conv_fwd", grid=(nseq, nj),
        in_specs=[pl.BlockSpec((seq, cb), lambda b, j: (b, j)),
                  pl.BlockSpec((seq, cb), lambda b, j: (b, nj + j)),
                  pl.BlockSpec((seq, cb), lambda b, j: (b, 2 * nj + j)),
                  pl.BlockSpec((SUBLANES, cb), lambda b, j: (0, j)),
                  pl.BlockSpec((3, SUBLANES, cb), lambda b, j: (0, 0, j))],
        out_specs=pl.BlockSpec((seq, cb), lambda b, j: (b, j)),
        out_shape=_sds((t, d), BF16),
        compiler_params=_params(("parallel", "parallel")))(bcx, bcx, bcx, cw, cw_got)


def _conv_bwd(dz, bcx, cw, cw_got, nseq, seq):
    t, d3 = bcx.shape
    d = d3 // 3
    cb = MXU_COLS
    nj = d // cb

    def body(dz_ref, b_ref, c_ref, x_ref, cw_ref, got_ref, o_ref, dcw_ref):
        @pl.when(pl.program_id(1) == 0)
        def _():
            dcw_ref[...] = jnp.zeros_like(dcw_ref)

        b = b_ref[...].astype(F32)
        c = c_ref[...].astype(F32)
        xv = x_ref[...].astype(F32)
        dzv = dz_ref[...].astype(F32)
        u = b * xv
        rows = lax.broadcasted_iota(jnp.int32, (SUBLANES, cb), 0)
        u1 = _shift_rows(u, 1, rows)
        u2 = _shift_rows(u, 2, rows)
        cwv = _conv_taps(cw_ref, got_ref)
        y = cwv[2:3] * u + cwv[1:2] * u1 + cwv[0:1] * u2
        dyc = dzv * c
        du = cwv[2:3] * dyc + cwv[1:2] * _shift_rows(dyc, -1, rows) + cwv[0:1] * _shift_rows(dyc, -2, rows)
        o_ref[0] = (du * xv).astype(BF16)
        o_ref[1] = (dzv * y).astype(BF16)
        o_ref[2] = (du * b).astype(BF16)
        s0 = jnp.sum(dyc * u2, axis=0, keepdims=True)
        s1 = jnp.sum(dyc * u1, axis=0, keepdims=True)
        s2 = jnp.sum(dyc * u, axis=0, keepdims=True)
        tap = lax.broadcasted_iota(jnp.int32, (3, cb), 0)
        dcw_ref[...] += jnp.where(tap == 0, s0, jnp.where(tap == 1, s1, s2))

    return pl.pallas_call(
        body, name="conv_bwd", grid=(nj, nseq),
        in_specs=[pl.BlockSpec((seq, cb), lambda j, b: (b, j)),
                  pl.BlockSpec((seq, cb), lambda j, b: (b, j)),
                  pl.BlockSpec((seq, cb), lambda j, b: (b, nj + j)),
                  pl.BlockSpec((seq, cb), lambda j, b: (b, 2 * nj + j)),
                  pl.BlockSpec((SUBLANES, cb), lambda j, b: (0, j)),
                  pl.BlockSpec((3, SUBLANES, cb), lambda j, b: (0, 0, j))],
        out_specs=[pl.BlockSpec((3, seq, cb), lambda j, b: (0, b, j)),
                   pl.BlockSpec((3, cb), lambda j, b: (0, j))],
        out_shape=[_sds((3, t, d), BF16), _sds((3, d), F32)],
        compiler_params=_params(("parallel", "arbitrary")))(dz, bcx, bcx, bcx, cw, cw_got)


def _pair_norm(x, gain_pair, low):
    sq = x * x
    ss_lo = jnp.sum(jnp.where(low, sq, 0.0), axis=-1, keepdims=True)
    ss_hi = jnp.sum(jnp.where(low, 0.0, sq), axis=-1, keepdims=True)
    r = lax.rsqrt(jnp.where(low, ss_lo, ss_hi) * (1.0 / HEAD_DIM) + EPS)
    xhat = x * r
    return xhat * gain_pair, xhat, r


KEYS = 2 * BLOCK
QK_SCALE = 1.0 / (HEAD_DIM ** 0.5)
N_PAIRS = N_Q_HEADS // 2


def _earlier_block(shape=(BLOCK, BLOCK)):
    return lax.broadcasted_iota(jnp.int32, shape, 0) > lax.broadcasted_iota(jnp.int32, shape, 1)


def _fill_bias(bias_ref):
    rows = lax.broadcasted_iota(jnp.int32, (2 * BLOCK, BLOCK), 0)
    qi = lax.broadcasted_iota(jnp.int32, (2 * BLOCK, BLOCK), 1)
    odd_head = rows >= BLOCK
    kj = jnp.where(odd_head, rows - BLOCK, rows)
    earlier = kj > qi
    dist = (jnp.where(earlier, BLOCK, 0) + qi - kj).astype(F32)
    for j in range(N_PAIRS):
        slope = jnp.where(odd_head, ALIBI_SLOPES[2 * j + 1], ALIBI_SLOPES[2 * j])
        bias = -slope * dist
        bias_ref[1, j] = bias
        bias_ref[0, j] = jnp.where(earlier, -1e30, bias)


def _merge_blocks(x_t, earlier):
    return jnp.concatenate([jnp.where(earlier, x_t[e * KEYS:e * KEYS + BLOCK], x_t[e * KEYS + BLOCK:(e + 1) * KEYS])
                            for e in range(2)], axis=0)


def _split_blocks(heads, earlier):
    parts = []
    for x in heads:
        parts += [jnp.where(earlier, x, 0.0), jnp.where(earlier, 0.0, x)]
    return jnp.concatenate(parts, axis=0).astype(BF16)


def _kv_pair_rows(kv_tile, parity, low):
    own = jnp.where(low if parity == 0 else jnp.logical_not(low), kv_tile, 0.0)
    other = pltpu.roll(own, HEAD_DIM, 1)
    lo, hi = (own, other) if parity == 0 else (other, own)
    return jnp.concatenate([lo, hi], axis=0).astype(BF16)


def _pair_softmax(s_t, sink_even, sink_odd):
    out = []
    for e, sink in enumerate((sink_even, sink_odd)):
        se = s_t[e * BLOCK:(e + 1) * BLOCK]
        m = jnp.maximum(jnp.max(se, axis=0, keepdims=True), sink)
        ee = jnp.exp(se - m)
        es = jnp.exp(sink - m)
        inv = 1.0 / (jnp.sum(ee, axis=0, keepdims=True) + es)
        out.append((ee * inv, es * inv))
    return out


def _attn_rows(n):
    q0 = pl.multiple_of(n * BLOCK, BLOCK)
    k0 = pl.multiple_of(jnp.maximum(n - 1, 0) * BLOCK, BLOCK)
    return q0, k0, jnp.minimum(n, 1)


def _key_rows(qkv_ref, k0, q0, col):
    return jnp.concatenate([qkv_ref[pl.ds(k0, BLOCK), col:col + LANES], qkv_ref[pl.ds(q0, BLOCK), col:col + LANES]],
                           axis=0).astype(F32)


def _attn_fwd(qkv, qg_pair, kg_pair, sinks, nseq, seq):
    t = qkv.shape[0]
    dq = N_Q_HEADS * HEAD_DIM
    dkv = N_KV_HEADS * HEAD_DIM

    def body(sk_ref, qkv_ref, qg_ref, kg_ref, o_ref, bias_ref):
        @pl.when(pl.program_id(0) == 0)
        def _():
            _fill_bias(bias_ref)

        low = lax.broadcasted_iota(jnp.int32, (1, LANES), 1) < HEAD_DIM
        earlier = _earlier_block()
        qg = qg_ref[...] * QK_SCALE
        kg = kg_ref[...]

        def blk(n, carry):
            q0, k0, later = _attn_rows(n)
            for kt in range(dkv // LANES):
                kraw = _key_rows(qkv_ref, k0, q0, dq + kt * LANES)
                vraw = _key_rows(qkv_ref, k0, q0, dq + dkv + kt * LANES)
                kn, _, _ = _pair_norm(kraw, kg, low)
                for par in range(2):
                    kh = 2 * kt + par
                    k_pair = _kv_pair_rows(kn, par, low)
                    v_pair = _kv_pair_rows(vraw, par, low)
                    for jj in range(2):
                        j = 2 * kh + jj
                        qraw = qkv_ref[pl.ds(q0, BLOCK), j * LANES:(j + 1) * LANES].astype(F32)
                        qn, _, _ = _pair_norm(qraw, qg, low)
                        s_t = _merge_blocks(_dot(k_pair, qn.astype(BF16), NT), earlier) + bias_ref[later, j]
                        (p0, _), (p1, _) = _pair_softmax(s_t, sk_ref[0, 2 * j], sk_ref[0, 2 * j + 1])
                        p_t = _split_blocks((p0, p1), earlier)
                        o_ref[pl.ds(q0, BLOCK), j * LANES:(j + 1) * LANES] = _dot(p_t, v_pair, TN).astype(BF16)
            return carry

        lax.fori_loop(0, seq // BLOCK, blk, 0)

    return pl.pallas_call(
        body, name="attn_fwd", grid=(nseq,),
        in_specs=[pl.BlockSpec(memory_space=pltpu.SMEM),
                  pl.BlockSpec((seq, dq + 2 * dkv), lambda b: (b, 0)),
                  pl.BlockSpec((1, LANES), lambda b: (0, 0)),
                  pl.BlockSpec((1, LANES), lambda b: (0, 0))],
        out_specs=pl.BlockSpec((seq, dq), lambda b: (b, 0)),
        out_shape=_sds((t, dq), BF16),
        scratch_shapes=[pltpu.VMEM((2, N_PAIRS, 2 * BLOCK, BLOCK), F32)],
        compiler_params=_params(("arbitrary",)))(sinks, qkv, qg_pair, kg_pair)


def _attn_bwd(do, qkv, qg_pair, kg_pair, sinks, nseq, seq):
    t = qkv.shape[0]
    dq = N_Q_HEADS * HEAD_DIM
    dkv = N_KV_HEADS * HEAD_DIM

    def body(sk_ref, do_ref, qkv_ref, qg_ref, kg_ref, o_ref, dqg_ref, dkg_ref, dsk_ref, acc_ref, bias_ref):
        @pl.when(pl.program_id(0) == 0)
        def _():
            _fill_bias(bias_ref)
            dqg_ref[...] = jnp.zeros_like(dqg_ref)
            dkg_ref[...] = jnp.zeros_like(dkg_ref)
            dsk_ref[...] = jnp.zeros_like(dsk_ref)

        acc_ref[...] = jnp.zeros_like(acc_ref)
        low = lax.broadcasted_iota(jnp.int32, (1, LANES), 1) < HEAD_DIM
        earlier = _earlier_block()
        head_row = lax.broadcasted_iota(jnp.int32, (N_Q_HEADS, LANES), 0)
        qg = qg_ref[...] * QK_SCALE
        kg = kg_ref[...]

        def blk(n, carry):
            dqg_acc, dkg_acc, dsk_acc = carry
            q0, k0, later = _attn_rows(n)
            for kt in range(dkv // LANES):
                kraw = _key_rows(qkv_ref, k0, q0, dq + kt * LANES)
                vraw = _key_rows(qkv_ref, k0, q0, dq + dkv + kt * LANES)
                kn, khat, rk = _pair_norm(kraw, kg, low)
                dk_tile = None
                dv_tile = None
                for par in range(2):
                    kh = 2 * kt + par
                    own = low if par == 0 else jnp.logical_not(low)
                    k_pair = _kv_pair_rows(kn, par, low)
                    v_pair = _kv_pair_rows(vraw, par, low)
                    dkn_rows = jnp.zeros((2 * KEYS, LANES), F32)
                    dv_rows = jnp.zeros((2 * KEYS, LANES), F32)
                    for jj in range(2):
                        j = 2 * kh + jj
                        qraw = qkv_ref[pl.ds(q0, BLOCK), j * LANES:(j + 1) * LANES].astype(F32)
                        qn, qhat, rq = _pair_norm(qraw, qg, low)
                        qn_b = qn.astype(BF16)
                        do_b = do_ref[pl.ds(q0, BLOCK), j * LANES:(j + 1) * LANES]
                        s_t = _merge_blocks(_dot(k_pair, qn_b, NT), earlier) + bias_ref[later, j]
                        dp_t = _merge_blocks(_dot(v_pair, do_b, NT), earlier)
                        ds_heads = []
                        probs = _pair_softmax(s_t, sk_ref[0, 2 * j], sk_ref[0, 2 * j + 1])
                        for e, (p, ps) in enumerate(probs):
                            dp = dp_t[e * BLOCK:(e + 1) * BLOCK]
                            dsum = jnp.sum(p * dp, axis=0, keepdims=True)
                            ds_heads.append(p * (dp - dsum))
                            dsk_acc = dsk_acc - jnp.where(head_row == 2 * j + e, ps * dsum, 0.0)
                        p_t = _split_blocks((probs[0][0], probs[1][0]), earlier)
                        ds_t = _split_blocks(ds_heads, earlier)
                        dv_rows = dv_rows + _dot(p_t, do_b, NN)
                        dkn_rows = dkn_rows + _dot(ds_t, qn_b, NN)
                        dqn = _dot(ds_t, k_pair, TN)
                        dqg_acc = dqg_acc + jnp.sum(dqn * qhat, axis=0, keepdims=True)
                        dqhat = dqn * qg
                        prod = dqhat * qhat
                        m_lo = jnp.sum(jnp.where(low, prod, 0.0), axis=-1, keepdims=True)
                        m_hi = jnp.sum(jnp.where(low, 0.0, prod), axis=-1, keepdims=True)
                        mean = jnp.where(low, m_lo, m_hi) * (1.0 / HEAD_DIM)
                        o_ref[pl.ds(q0, BLOCK), j * LANES:(j + 1) * LANES] = (rq * (dqhat - qhat * mean)).astype(BF16)
                    dkn_acc = jnp.where(low, dkn_rows[0:KEYS], dkn_rows[KEYS:2 * KEYS])
                    dv_acc = jnp.where(low, dv_rows[0:KEYS], dv_rows[KEYS:2 * KEYS])
                    dkn = dkn_acc + pltpu.roll(dkn_acc, HEAD_DIM, 1)
                    dvh = dv_acc + pltpu.roll(dv_acc, HEAD_DIM, 1)
                    khat_own = jnp.where(own, khat, 0.0)
                    khat_dup = khat_own + pltpu.roll(khat_own, HEAD_DIM, 1)
                    dkg_acc = dkg_acc + jnp.sum(jnp.where(own, dkn * khat_dup, 0.0), axis=0, keepdims=True)
                    dkhat = dkn * kg
                    mean_k = jnp.sum(dkhat * khat_dup, axis=-1, keepdims=True) * (1.0 / LANES)
                    dk_raw = rk * (dkhat - khat_dup * mean_k)
                    dk_tile = jnp.where(own, dk_raw, 0.0) if dk_tile is None else jnp.where(own, dk_raw, dk_tile)
                    dv_tile = jnp.where(own, dvh, 0.0) if dv_tile is None else jnp.where(own, dvh, dv_tile)
                for r0, part in ((k0, slice(0, BLOCK)), (q0, slice(BLOCK, KEYS))):
                    acc_ref[pl.ds(r0, BLOCK), kt * LANES:(kt + 1) * LANES] += dk_tile[part]
                    acc_ref[pl.ds(r0, BLOCK), dkv + kt * LANES:dkv + (kt + 1) * LANES] += dv_tile[part]
            return dqg_acc, dkg_acc, dsk_acc

        zero = jnp.zeros((1, LANES), F32)
        carry = (zero, zero, jnp.zeros((N_Q_HEADS, LANES), F32))
        dqg_acc, dkg_acc, dsk_acc = lax.fori_loop(0, seq // BLOCK, blk, carry)
        dqg_ref[...] += dqg_acc * QK_SCALE
        dkg_ref[...] += dkg_acc
        dsk_ref[...] += dsk_acc
        o_ref[:, dq:dq + 2 * dkv] = acc_ref[...].astype(BF16)

    small = pl.BlockSpec((1, LANES), lambda b: (0, 0))
    heads = pl.BlockSpec((N_Q_HEADS, LANES), lambda b: (0, 0))
    return pl.pallas_call(
        body, name="attn_bwd", grid=(nseq,),
        in_specs=[pl.BlockSpec(memory_space=pltpu.SMEM),
                  pl.BlockSpec((seq, dq), lambda b: (b, 0)),
                  pl.BlockSpec((seq, dq + 2 * dkv), lambda b: (b, 0)),
                  small, small],
        out_specs=[pl.BlockSpec((seq, dq + 2 * dkv), lambda b: (b, 0)), small, small, heads],
        out_shape=[_sds((t, dq + 2 * dkv), BF16), _sds((1, LANES), F32), _sds((1, LANES), F32),
                   _sds((N_Q_HEADS, LANES), F32)],
        scratch_shapes=[pltpu.VMEM((seq, 2 * dkv), F32), pltpu.VMEM((2, N_PAIRS, 2 * BLOCK, BLOCK), F32)],
        compiler_params=_params(("arbitrary",)))(sinks, do, qkv, qg_pair, kg_pair)


def _place():
    x, y, c = lax.axis_index("x"), lax.axis_index("y"), lax.axis_index("c")
    other_chips = [(1 - x, y), (x, 1 - y), (1 - x, 1 - y)]
    return x, y, c, other_chips


def _half_rows(c, rows):
    rh = rows // 2
    return pl.ds(pl.multiple_of(c * rh, BF16_ROWS), rh)


def _cast_own(name, w, place, layer=None):
    nl, r, cdim = w.shape
    first = 0
    if layer is not None:
        nl, first = 1, layer
    rt = _row_tile(r, 4 * cdim, ELEMENTWISE_BLOCK)

    def body(s_ref, w_ref, o_ref):
        o_ref[...] = w_ref[...].astype(BF16)

    grid_spec = pltpu.PrefetchScalarGridSpec(
        num_scalar_prefetch=1, grid=(nl, r // rt),
        in_specs=[pl.BlockSpec((None, rt, cdim), lambda l, i, s: (first + l, i, 0))],
        out_specs=pl.BlockSpec((None, None, rt, cdim), lambda l, i, s: (l, s[1], i, 0)))
    return pl.pallas_call(
        body, name=name, grid_spec=grid_spec, out_shape=_sds((nl, N_CHIPS, r, cdim), BF16),
        compiler_params=_params(("parallel", "parallel")))(place, w)


def _gather_protocol(outs, shapes, send_sems, recv_sems):
    n = len(outs)
    x, y, c, other_chips = _place()
    me_chip = 2 * x + y
    sibling = (x, y, 1 - c)

    def rows(u, chip, half):
        return outs[u].at[:, chip, _half_rows(half, shapes[u][2]), :]

    def copy(sem, part, to):
        return pltpu.make_async_remote_copy(src_ref=part, dst_ref=part, send_sem=send_sems.at[sem],
                                            recv_sem=recv_sems.at[sem], device_id=to, device_id_type=MESH)

    sends = []
    for u in range(n):
        for k, chip in enumerate(other_chips):
            cp = copy(6 * u + k, rows(u, me_chip, c), (*chip, c))
            cp.start()
            sends.append(cp)
    for u in range(n):
        for k, chip in enumerate(other_chips):
            got = rows(u, 2 * chip[0] + chip[1], c)
            copy(6 * u + k, got, (*chip, c)).wait_recv()
            cp = copy(6 * u + 3 + k, got, sibling)
            cp.start()
            sends.append(cp)
    for u in range(n):
        for k, chip in enumerate(other_chips):
            copy(6 * u + 3 + k, rows(u, 2 * chip[0] + chip[1], 1 - c), sibling).wait_recv()
    for cp in sends:
        cp.wait_send()


def _hbm_ref(a):
    return jax.new_ref(a, memory_space=pltpu.MemorySpace.HBM)


def _sibling_peer():
    x, y, c, _ = _place()
    return [(x, y, 1 - c)]


def _chip_peers():
    x, y, c, other_chips = _place()
    return [(*chip, c) for chip in other_chips]


def _gather_peers():
    return _chip_peers() + _sibling_peer()


def _on_sequencer(name, collective_id, n_sems, peers, protocol, operands=(), out_types=()):
    n_in, n_out = len(operands), len(out_types)

    def launch(*refs):
        send_sems, recv_sems = refs[n_in + n_out:]
        barrier = pltpu.get_barrier_semaphore()
        targets = peers()
        for peer in targets:
            pl.semaphore_signal(barrier, inc=1, device_id=peer, device_id_type=MESH)
        pl.semaphore_wait(barrier, len(targets))
        protocol(refs[:n_in], refs[n_in:n_in + n_out], send_sems, recv_sems)

    return pl.kernel(
        launch, out_type=tuple(out_types), mesh=plsc.ScalarSubcoreMesh(axis_name="sequencer", num_cores=1), name=name,
        scratch_types=(pltpu.SemaphoreType.DMA((n_sems,)), pltpu.SemaphoreType.DMA((n_sems,))),
        compiler_params=pltpu.CompilerParams(collective_id=collective_id))(*operands)


def _seq_allgather(name, collective_id, bufs):
    shapes = [b.shape for b in bufs]
    refs = [_hbm_ref(b) for b in bufs]
    _on_sequencer(name, collective_id, 6 * len(bufs), _gather_peers,
                  lambda ins, outs, send_sems, recv_sems: _gather_protocol(refs, shapes, send_sems, recv_sems))
    return [r[...] for r in refs]


def _taps_protocol(block_ref, got_ref, send_sems, recv_sems, first_sem):
    x, y, c, other_chips = _place()
    copies = []
    for k, chip in enumerate(other_chips):
        cp = pltpu.make_async_remote_copy(src_ref=block_ref, dst_ref=got_ref.at[k], send_sem=send_sems.at[first_sem + k],
                                          recv_sem=recv_sems.at[first_sem + k], device_id=(*chip, c), device_id_type=MESH)
        cp.start()
        copies.append(cp)
    return copies


def _seq_allgather_conv(collective_id, bufs, cw_block):
    shapes = [b.shape for b in bufs]
    refs = [_hbm_ref(b) for b in bufs]

    def protocol(ins, outs, send_sems, recv_sems):
        taps = _taps_protocol(ins[0], outs[0], send_sems, recv_sems, 6 * len(bufs))
        _gather_protocol(refs, shapes, send_sems, recv_sems)
        for cp in taps:
            cp.wait_recv()
        for cp in taps:
            cp.wait_send()

    (got,) = _on_sequencer("allgather_conv", collective_id, 6 * len(bufs) + 3, _gather_peers, protocol,
                           operands=(cw_block,), out_types=(_sds((3, *cw_block.shape), F32),))
    return [r[...] for r in refs], got


def _exchange_protocol(gs, outs, shapes, send_sems, recv_sems):
    x, y, c, _ = _place()
    sends = []
    for u in range(len(gs)):
        cp = pltpu.make_async_remote_copy(
            src_ref=gs[u].at[:, _half_rows(1 - c, shapes[u][1]), :], dst_ref=outs[u],
            send_sem=send_sems.at[u], recv_sem=recv_sems.at[u], device_id=(x, y, 1 - c), device_id_type=MESH)
        cp.start()
        sends.append(cp)
    for cp in sends:
        cp.wait_recv()
    for cp in sends:
        cp.wait_send()


def _seq_exchange(name, collective_id, grads):
    shapes = [g.shape for g in grads]
    return _on_sequencer(
        name, collective_id, len(grads), _sibling_peer,
        lambda gs, outs, send_sems, recv_sems: _exchange_protocol(gs, outs, shapes, send_sems, recv_sems),
        operands=grads, out_types=[_sds((s[0], s[1] // 2, s[2]), F32) for s in shapes])


def _sum_halves(name, g, got, place, after):
    _, r, cdim = g.shape
    rh = r // 2
    rt = _row_tile(rh, 4 * N_CHIPS * cdim, 4 * ELEMENTWISE_BLOCK)
    nr = rh // rt

    def body(s_ref, g_ref, got_ref, after_ref, pb_ref, pf_ref):
        pb_ref[...] = (g_ref[...] + got_ref[...]).astype(BF16)
        mine = s_ref[1]
        pf_ref[...] = g_ref[mine] + got_ref[mine]

    quarters = (N_CHIPS, rt, cdim)
    grid_spec = pltpu.PrefetchScalarGridSpec(
        num_scalar_prefetch=1, grid=(nr,),
        in_specs=[pl.BlockSpec(quarters, lambda i, s: (0, s[0] * nr + i, 0)),
                  pl.BlockSpec(quarters, lambda i, s: (0, i, 0)),
                  pl.BlockSpec(memory_space=pl.ANY)],
        out_specs=[pl.BlockSpec(quarters, lambda i, s: (0, i, 0)),
                   pl.BlockSpec((rt, cdim), lambda i, s: (i, 0))])
    return pl.pallas_call(
        body, name=name, grid_spec=grid_spec,
        out_shape=[_sds((N_CHIPS, rh, cdim), BF16), _sds((rh, cdim), F32)],
        compiler_params=_params(("parallel",)))(place, g, got, after)


def _scatter_protocol(ps, outs, send_sems, recv_sems):
    x, y, c, other_chips = _place()
    sends = []
    for u in range(len(ps)):
        for k, chip in enumerate(other_chips):
            cp = pltpu.make_async_remote_copy(
                src_ref=ps[u].at[2 * chip[0] + chip[1]], dst_ref=outs[u].at[k],
                send_sem=send_sems.at[3 * u + k], recv_sem=recv_sems.at[3 * u + k],
                device_id=(*chip, c), device_id_type=MESH)
            cp.start()
            sends.append(cp)
    for cp in sends:
        cp.wait_recv()
    for cp in sends:
        cp.wait_send()


def _seq_scatter(name, collective_id, partials):
    return _on_sequencer(
        name, collective_id, 3 * len(partials), _chip_peers, _scatter_protocol,
        operands=partials, out_types=[_sds((3, p.shape[1], p.shape[2]), BF16) for p in partials])


def _sum_partials(name, own, got, place, layer, nl, prev, after):
    rh, cdim = own.shape
    rt = _row_tile(rh, 4 * cdim, ELEMENTWISE_BLOCK)
    nr = rh // rt
    after = list(after) if isinstance(after, (list, tuple)) else [after]

    def body(s_ref, own_ref, got_ref, *rest):
        o_ref = rest[-1]
        o_ref[...] = ((own_ref[...] + got_ref[0].astype(F32)) + got_ref[1].astype(F32)) + got_ref[2].astype(F32)

    in_specs = [pl.BlockSpec((rt, cdim), lambda i, s: (i, 0)), pl.BlockSpec((3, rt, cdim), lambda i, s: (0, i, 0)),
                *[pl.BlockSpec(memory_space=pl.ANY)] * len(after)]
    args = [place, own, got, *after]
    aliases = {}
    if prev is not None:
        in_specs.append(pl.BlockSpec(memory_space=pl.ANY))
        aliases = {len(args): 0}
        args.append(prev)
    grid_spec = pltpu.PrefetchScalarGridSpec(
        num_scalar_prefetch=1, grid=(nr,), in_specs=in_specs,
        out_specs=pl.BlockSpec((None, rt, cdim), lambda i, s: (layer, s[0] * nr + i, 0)))
    return pl.pallas_call(
        body, name=name, grid_spec=grid_spec, out_shape=_sds((nl, 2 * rh, cdim), F32),
        input_output_aliases=aliases, compiler_params=_params(("parallel",)))(*args)


def _share_protocol(outs, shapes, units, send_sems, recv_sems):
    x, y, c, _ = _place()
    sends = []
    for u, (w, l) in enumerate(units):
        mine = outs[w].at[l, _half_rows(c, shapes[w][1]), :]
        cp = pltpu.make_async_remote_copy(src_ref=mine, dst_ref=mine, send_sem=send_sems.at[u],
                                          recv_sem=recv_sems.at[u], device_id=(x, y, 1 - c), device_id_type=MESH)
        cp.start()
        sends.append(cp)
    for u, (w, l) in enumerate(units):
        theirs = outs[w].at[l, _half_rows(1 - c, shapes[w][1]), :]
        pltpu.make_async_remote_copy(src_ref=theirs, dst_ref=theirs, send_sem=send_sems.at[u],
                                     recv_sem=recv_sems.at[u], device_id=(x, y, 1 - c),
                                     device_id_type=MESH).wait_recv()
    for cp in sends:
        cp.wait_send()


def _seq_share(name, collective_id, bufs):
    shapes = [b.shape for b in bufs]
    units = [(w, l) for w in range(len(bufs)) for l in range(shapes[w][0])]
    refs = [_hbm_ref(b) for b in bufs]
    _on_sequencer(name, collective_id, len(units), _sibling_peer,
                  lambda ins, outs, send_sems, recv_sems: _share_protocol(refs, shapes, units, send_sems, recv_sems))
    return [r[...] for r in refs]


def _gather_blocks(block_ref, all_ref, send_sems, recv_sems):
    x, y, c, _ = _place()
    me = 4 * x + 2 * y + c
    all_ref[me] = block_ref[...]
    sends = []
    for rel in range(1, 8):
        fx, fy, fc = (rel >> 2) & 1, (rel >> 1) & 1, rel & 1
        peer = (x ^ fx, y ^ fy, c ^ fc)
        cp = pltpu.make_async_remote_copy(src_ref=block_ref, dst_ref=all_ref.at[me], send_sem=send_sems.at[rel - 1],
                                          recv_sem=recv_sems.at[rel - 1], device_id=peer, device_id_type=MESH)
        cp.start()
        sends.append(cp)
    for cp in sends:
        cp.wait_recv()
    for cp in sends:
        cp.wait_send()


def _adam(w, g, m, v):
    m_new = ADAM_B1 * m + (1.0 - ADAM_B1) * g
    v_new = ADAM_B2 * v + (1.0 - ADAM_B2) * (g * g)
    m_hat = m_new / (1.0 - ADAM_B1 ** ADAM_STEP)
    v_hat = v_new / (1.0 - ADAM_B2 ** ADAM_STEP)
    delta = -ADAM_LR * (m_hat / (jnp.sqrt(v_hat) + ADAM_EPS) + ADAM_WD * w)
    return delta, m_new, v_new


def _small_step(dnm0, dnm1, dnf0, dnf1, dcw, dqg, dkg, dsk, loss, w_blk, m_blk, v_blk, cw_cols):
    d = w_blk.shape[1]
    vm = pl.BlockSpec(memory_space=pltpu.VMEM)

    def reduce_body(dnm0_ref, dnm1_ref, dnf0_ref, dnf1_ref, dcw_ref, dqg_ref, dkg_ref, dsk_ref, loss_ref,
                    g_ref, blk_ref, all_ref, send_sems, recv_sems):
        blk_ref[...] = jnp.zeros_like(blk_ref)
        for row, part_ref in ((SENT_NORM_MIXER, dnm0_ref), (SENT_NORM_MIXER + 1, dnm1_ref),
                              (SENT_NORM_FFN, dnf0_ref), (SENT_NORM_FFN + 1, dnf1_ref)):
            blk_ref[row:row + 1, :] = jnp.sum(part_ref[...], axis=0, keepdims=True)
        blk_ref[SENT_CONV_W:SENT_CONV_W + 3, :] = dcw_ref[...]
        misc = slice(SENT_MISC, SENT_MISC + 1)
        for tile, gain_ref in ((TILE_Q_GAIN, dqg_ref), (TILE_K_GAIN, dkg_ref)):
            pair = gain_ref[...]
            blk_ref[misc, tile * LANES:(tile + 1) * LANES] = pair + pltpu.roll(pair, HEAD_DIM, 1)
        for h in range(N_Q_HEADS):
            lane = TILE_SINKS * LANES + h
            blk_ref[misc, lane:lane + 1] = jnp.sum(dsk_ref[h:h + 1, :], axis=1, keepdims=True)
        blk_ref[misc, TILE_LOSS * LANES:(TILE_LOSS + 1) * LANES] = jnp.broadcast_to(loss_ref[...], (1, LANES))
        _gather_blocks(blk_ref, all_ref, send_sems, recv_sems)
        g = all_ref[0]
        for dev in range(1, 8):
            g = g + all_ref[dev]
        g_ref[...] = jnp.zeros_like(g_ref)
        for sent, row, n in ((SENT_NORM_MIXER, ROW_NORM_MIXER, 2), (SENT_NORM_FFN, ROW_NORM_FFN, 2),
                             (SENT_CONV_W, ROW_CONV_W, 3), (SENT_MISC, ROW_MISC, 1)):
            g_ref[row:row + n, :] = g[sent:sent + n]

    g_blk = pl.pallas_call(
        reduce_body, name="small_allreduce", in_specs=[vm] * 9, out_specs=vm, out_shape=_sds((SMALL_ROWS, d), F32),
        scratch_shapes=[pltpu.VMEM((SUBLANES, d), F32), pltpu.VMEM((8, SUBLANES, d), F32),
                        pltpu.SemaphoreType.DMA((7,)), pltpu.SemaphoreType.DMA((7,))],
    )(dnm0, dnm1, dnf0, dnf1, dcw, dqg, dkg, dsk, loss)

    def body(g_ref, w_ref, m_ref, v_ref, *out_refs):
        g = g_ref[...]
        misc = slice(ROW_MISC, ROW_MISC + 1)
        out_refs[0][...] = g[misc, TILE_LOSS * LANES:TILE_LOSS * LANES + 1]
        chip = 2 * lax.axis_index("x") + lax.axis_index("y")
        for i, blk in enumerate((g, *_adam(w_ref[...], g, m_ref[...], v_ref[...]))):
            nm_ref, nf_ref, cw_ref, qg_ref, kg_ref, sk_ref = out_refs[1 + 6 * i:7 + 6 * i]
            nm_ref[...] = blk[ROW_NORM_MIXER:ROW_NORM_MIXER + 2]
            nf_ref[...] = blk[ROW_NORM_FFN:ROW_NORM_FFN + 2]
            qg_ref[...] = blk[misc, TILE_Q_GAIN * LANES:TILE_Q_GAIN * LANES + HEAD_DIM]
            kg_ref[...] = blk[misc, TILE_K_GAIN * LANES:TILE_K_GAIN * LANES + HEAD_DIM]
            sk_ref[...] = blk[misc, TILE_SINKS * LANES:TILE_SINKS * LANES + N_Q_HEADS]
            for q in range(N_CHIPS):
                @pl.when(chip == q)
                def _(blk=blk, cw_ref=cw_ref, q=q):
                    cw_ref[0] = blk[ROW_CONV_W:ROW_CONV_W + 3, q * cw_cols:(q + 1) * cw_cols]

    group = [_sds((2, d), F32), _sds((2, d), F32), _sds((1, 3, cw_cols), F32), _sds((1, HEAD_DIM), F32),
             _sds((1, HEAD_DIM), F32), _sds((1, N_Q_HEADS), F32)]
    outs = pl.pallas_call(
        body, name="small_adam", in_specs=[vm] * 4, out_specs=[vm] * 25, out_shape=[_sds((1, 1), F32)] + group * 4,
    )(g_blk, w_blk, m_blk, v_blk)
    names = ("norm_mixer", "norm_ffn", "conv_w", "attn_q_gain", "attn_k_gain", "attn_sinks")
    return outs[0], [dict(zip(names, outs[1 + 6 * i:7 + 6 * i])) for i in range(4)]


def _adam_step(name, w, g, m, v):
    nl, r, cdim = w.shape
    rt = _row_tile(r, 4 * cdim, ELEMENTWISE_BLOCK)

    def body(w_ref, g_ref, m_ref, v_ref, go_ref, d_ref, mo_ref, vo_ref):
        gv = g_ref[...]
        go_ref[...] = gv
        delta, m_new, v_new = _adam(w_ref[...], gv, m_ref[...], v_ref[...])
        d_ref[...] = delta
        mo_ref[...] = m_new
        vo_ref[...] = v_new

    spec = pl.BlockSpec((None, rt, cdim), lambda l, i: (l, i, 0))
    return pl.pallas_call(
        body, name=name, grid=(nl, r // rt), in_specs=[spec] * 4, out_specs=[spec] * 4,
        out_shape=[_sds(w.shape, F32)] * 4,
        compiler_params=_params(("parallel", "parallel")))(w, g, m, v)


def _pad_rows(a, rows=SUBLANES):
    return jnp.pad(a, ((0, rows - a.shape[0]), (0, 0)))


def _small_block(nm, nf, cw_local, qg, kg, sk, chip):
    d = nm.shape[1]
    cw_rows = lax.dynamic_update_slice(jnp.zeros((SUBLANES, d), F32), cw_local, (0, chip * cw_local.shape[1]))
    misc = jnp.concatenate([qg, qg, kg, kg, jnp.pad(sk, ((0, 0), (0, LANES - sk.shape[1]))),
                            jnp.zeros((1, d - 3 * LANES), F32)], axis=1)
    return jnp.concatenate([_pad_rows(nm), _pad_rows(nf), cw_rows, _pad_rows(misc)], axis=0)


WEIGHT_NAMES = ("conv_w_in", "conv_w", "conv_w_out", "attn_w_qkv", "attn_q_gain", "attn_k_gain", "attn_sinks",
                "attn_w_o", "norm_mixer", "norm_ffn", "ffn_w_gate_up", "ffn_w_down")
BIG = ("conv_w_in", "conv_w_out", "attn_w_qkv", "attn_w_o", "ffn_w_gate_up", "ffn_w_down")


def kernel(x, conv_w_in, conv_w, conv_w_out, attn_w_qkv, attn_q_gain, attn_k_gain, attn_sinks, attn_w_o, norm_mixer, norm_ffn, ffn_w_gate_up, ffn_w_down, loss_target, m_conv_w_in, m_conv_w, m_conv_w_out, m_attn_w_qkv, m_attn_q_gain, m_attn_k_gain, m_attn_sinks, m_attn_w_o, m_norm_mixer, m_norm_ffn, m_ffn_w_gate_up, m_ffn_w_down, v_conv_w_in, v_conv_w, v_conv_w_out, v_attn_w_qkv, v_attn_q_gain, v_attn_k_gain, v_attn_sinks, v_attn_w_o, v_norm_mixer, v_norm_ffn, v_ffn_w_gate_up, v_ffn_w_down):
    w = dict(conv_w_in=conv_w_in, conv_w=conv_w, conv_w_out=conv_w_out, attn_w_qkv=attn_w_qkv,
             attn_q_gain=attn_q_gain, attn_k_gain=attn_k_gain, attn_sinks=attn_sinks, attn_w_o=attn_w_o,
             norm_mixer=norm_mixer, norm_ffn=norm_ffn, ffn_w_gate_up=ffn_w_gate_up, ffn_w_down=ffn_w_down)
    m = dict(conv_w_in=m_conv_w_in, conv_w=m_conv_w, conv_w_out=m_conv_w_out, attn_w_qkv=m_attn_w_qkv,
             attn_q_gain=m_attn_q_gain, attn_k_gain=m_attn_k_gain, attn_sinks=m_attn_sinks, attn_w_o=m_attn_w_o,
             norm_mixer=m_norm_mixer, norm_ffn=m_norm_ffn, ffn_w_gate_up=m_ffn_w_gate_up, ffn_w_down=m_ffn_w_down)
    v = dict(conv_w_in=v_conv_w_in, conv_w=v_conv_w, conv_w_out=v_conv_w_out, attn_w_qkv=v_attn_w_qkv,
             attn_q_gain=v_attn_q_gain, attn_k_gain=v_attn_k_gain, attn_sinks=v_attn_sinks, attn_w_o=v_attn_w_o,
             norm_mixer=v_norm_mixer, norm_ffn=v_norm_ffn, ffn_w_gate_up=v_ffn_w_gate_up, ffn_w_down=v_ffn_w_down)

    nseq, seq, d = x.shape
    t = nseq * seq
    chip = 2 * lax.axis_index("x") + lax.axis_index("y")
    core = lax.axis_index("c")
    place = jnp.stack([core, chip]).astype(jnp.int32)
    x0 = x.reshape(t, d)
    tgt = loss_target.reshape(t, d)

    cw_block = lax.dynamic_update_slice(jnp.zeros((SUBLANES, d), F32), conv_w[0], (0, chip * conv_w.shape[2]))
    def cast(k, layer=None):
        return _cast_own(f"cast_{k}" + ("" if layer is None else str(layer)), w[k], place, layer)

    (w_in,), cw_got = _seq_allgather_conv(1, [cast("conv_w_in")], cw_block)
    w_out, w_gu0, w_dn0 = _seq_allgather(
        "allgather_ffn0", 2, [cast("conv_w_out"), cast("ffn_w_gate_up", 0), cast("ffn_w_down", 0)])
    w_qkv, w_o, w_gu1, w_dn1 = _seq_allgather(
        "allgather_rest", 3, [cast("attn_w_qkv"), cast("attn_w_o"), cast("ffn_w_gate_up", 1), cast("ffn_w_down", 1)])
    w_out = w_out.reshape(1, d, d)
    w_o = w_o.reshape(1, d, d)
    w_gu = [w_gu0, w_gu1]
    w_dn = [w_dn0.reshape(1, D_FF, d), w_dn1.reshape(1, D_FF, d)]

    qg_pair = jnp.concatenate([attn_q_gain, attn_q_gain], axis=1)
    kg_pair = jnp.concatenate([attn_k_gain, attn_k_gain], axis=1)

    def ffn_bwd(i, dxo, xin, h, g, u, a):
        g_dn = _wgrad_down(f"ffn{i}_down_wgrad", a, dxo, D_FF // 2)
        dg, du = _mm_down_t_swiglu(f"ffn{i}_down_dgrad", dxo, w_dn[i], 0, g, u)
        g_gu = _wgrad_up2(f"ffn{i}_up_wgrad", h, dg, du)
        dxi, dgain = _dgrad_norm_ffn(f"ffn{i}_up_dgrad", dg, du, w_gu[i], 0, xin, norm_ffn[i:i + 1], dxo)
        return dxi, dgain, g_gu, g_dn

    h0, bcx = _mm_norm_up_joined("conv_in", x0, norm_mixer[0:1], w_in, 512)
    z = _conv_fwd(bcx, cw_block, cw_got, nseq, seq)
    x1, h1 = _mm_down_norm("conv_out", z, w_out, 0, x0, norm_ffn[0:1])
    g0, u0, a0 = _mm_up_swiglu("ffn0_up", h1, w_gu[0], 0)
    x2, h2 = _mm_down_norm("ffn0_down", a0, w_dn[0], 0, x1, norm_mixer[1:2])
    qkv = _mm_up_joined("attn_qkv", h2, w_qkv, 1024)
    o = _attn_fwd(qkv, qg_pair, kg_pair, attn_sinks, nseq, seq)
    x3, h3 = _mm_down_norm("attn_out", o, w_o, 0, x2, norm_ffn[1:2])
    g1, u1, a1 = _mm_up_swiglu("ffn1_up", h3, w_gu[1], 0)
    dy, loss_part = _mm_down_loss("ffn1_down", a1, w_dn[1], 0, x3, tgt)

    finished = {k: None for k in BIG}

    def exchange(tag, cid, units):
        return units, _seq_exchange(f"exchange_{tag}", cid, [g for _, _, g in units])

    def scatter(tag, cid, group, after):
        units, got = group
        sums = [_sum_halves(f"sum_halves_{k}{l}", g, r, place, after) for (k, l, g), r in zip(units, got)]
        return units, sums, _seq_scatter(f"scatter_{tag}", cid, [pb for pb, _ in sums])

    def finish(group, after):
        units, sums, arrived = group
        for (k, l, _), (_, pf), r in zip(units, sums, arrived):
            finished[k] = _sum_partials(f"sum_partials_{k}{l}", pf, r, place, l, w[k].shape[0], finished[k], after)

    dx3, dnf1, g_gu1, g_dn1 = ffn_bwd(1, dy, x3, h3, g1, u1, a1)
    ffn1 = exchange("ffn1", 4, [("ffn_w_down", 1, g_dn1), ("ffn_w_gate_up", 1, g_gu1)])
    g_o = _wgrad_down("attn_out_wgrad", o, dx3, d)
    do = _mm_down_t("attn_out_dgrad", dx3, w_o, 0)
    ffn1 = scatter("ffn1", 8, ffn1, do)
    dqkv, dqg, dkg, dsk = _attn_bwd(do, qkv, qg_pair, kg_pair, attn_sinks, nseq, seq)
    g_qkv = _wgrad_joined("attn_qkv_wgrad", h2, dqkv)
    attn = exchange("attn", 5, [("attn_w_o", 0, g_o), ("attn_w_qkv", 0, g_qkv)])
    dx2, dnm1 = _dgrad_norm_qkv("attn_qkv_dgrad", dqkv, w_qkv, x2, norm_mixer[1:2], dx3)
    finish(ffn1, dx2)
    attn = scatter("attn", 9, attn, dx2)
    dx1, dnf0, g_gu0, g_dn0 = ffn_bwd(0, dx2, x1, h1, g0, u0, a0)
    ffn0 = exchange("ffn0", 6, [("ffn_w_down", 0, g_dn0), ("ffn_w_gate_up", 0, g_gu0)])
    g_out = _wgrad_down("conv_out_wgrad", z, dx1, d)
    dz = _mm_down_t("conv_out_dgrad", dx1, w_out, 0)
    finish(attn, dz)
    ffn0 = scatter("ffn0", 10, ffn0, dz)
    dbcx, dcw = _conv_bwd(dz, bcx, cw_block, cw_got, nseq, seq)
    g_in = _wgrad_conv_in("conv_in_wgrad", h0, dbcx, conv_w_in.shape[2])
    conv = exchange("conv", 7, [("conv_w_out", 0, g_out), ("conv_w_in", 0, g_in)])
    dx0, dnm0 = _dgrad_norm_conv("conv_in_dgrad", dbcx, w_in, x0, norm_mixer[0:1], dx1)
    finish(ffn0, dx0)
    late = ("attn_w_qkv", "attn_w_o", "ffn_w_gate_up", "ffn_w_down")
    grads_late = _seq_share("share_late", 12, [finished[k] for k in late])
    conv = scatter("conv", 11, conv, dx0)

    grad, delta, new_m, new_v = {}, {}, {}, {}

    def adam(k, g):
        grad[k], delta[k], new_m[k], new_v[k] = _adam_step(f"adam_{k}", w[k], g, m[k], v[k])

    for k, g in zip(late, grads_late):
        adam(k, g)

    def blocks(src):
        return _small_block(src["norm_mixer"], src["norm_ffn"], src["conv_w"][0], src["attn_q_gain"],
                            src["attn_k_gain"], src["attn_sinks"], chip)

    loss, small = _small_step(dnm0, dnm1, dnf0, dnf1, dcw, dqg, dkg, dsk, loss_part,
                              blocks(w), blocks(m), blocks(v), conv_w.shape[2])
    for dst, part in zip((grad, delta, new_m, new_v), small):
        dst.update(part)

    finish(conv, [new_v[k] for k in late])
    last = ("conv_w_in", "conv_w_out")
    for k, g in zip(last, _seq_share("share_last", 13, [finished[k] for k in last])):
        adam(k, g)

    return (loss.reshape(()), dx0.reshape(nseq, seq, d), *[grad[k] for k in WEIGHT_NAMES], *[delta[k] for k in WEIGHT_NAMES],
            *[new_m[k] for k in WEIGHT_NAMES], *[new_v[k] for k in WEIGHT_NAMES])
```

```python
import jax
import jax.numpy as jnp
from jax import lax
from jax.experimental import pallas as pl
from jax.experimental.pallas import tpu as pltpu
from jax.experimental.pallas import tpu_sc as plsc

F32 = jnp.float32
BF16 = jnp.bfloat16

D_FF = 2816
N_Q_HEADS = 16
N_KV_HEADS = 4
HEAD_DIM = 64
WINDOW = 128
BLOCK = 128
EPS = 1e-6
N_CHIPS = 4
LANES = 128
SUBLANES = 8
BF16_ROWS = 16
MXU_COLS = 256
VMEM_LIMIT = 48 * 1024 * 1024
ADAM_LR, ADAM_B1, ADAM_B2, ADAM_EPS, ADAM_WD, ADAM_STEP = 0.001, 0.9, 0.999, 1e-08, 0.01, 10
ALIBI_SLOPES = tuple(2.0 ** (-8.0 * (h + 1) / N_Q_HEADS) for h in range(N_Q_HEADS))
SMALL_ROWS = 32
ROW_NORM_MIXER, ROW_NORM_FFN, ROW_CONV_W, ROW_MISC = 0, 8, 16, 24
SENT_NORM_MIXER, SENT_NORM_FFN, SENT_CONV_W, SENT_MISC = 0, 2, 4, 7
TILE_Q_GAIN, TILE_K_GAIN, TILE_SINKS, TILE_LOSS = 0, 1, 2, 3
MESH = pl.DeviceIdType.MESH

NN = ((1,), (0,))
NT = ((1,), (1,))
TN = ((0,), (0,))


def _dot(a, b, dims):
    return lax.dot_general(a, b, (dims, ((), ())), preferred_element_type=F32)


def _pick(n, cands):
    for c in cands:
        if n % c == 0:
            return c
    raise ValueError((n, cands))


def _row_tile(rows, row_bytes, cap_bytes):
    fits = [r for r in range(BF16_ROWS, rows + 1, BF16_ROWS) if rows % r == 0 and r * row_bytes <= cap_bytes]
    if not fits:
        raise ValueError((rows, row_bytes, cap_bytes))
    return fits[-1]


ELEMENTWISE_BLOCK = 3 << 19


def _resident(block_shape, index_map):
    return pl.BlockSpec(block_shape, index_map, pipeline_mode=pl.Buffered(1))


def _params(sem):
    return pltpu.CompilerParams(dimension_semantics=sem, vmem_limit_bytes=VMEM_LIMIT)


def _sds(shape, dtype):
    return jax.ShapeDtypeStruct(shape, dtype)


def _rms(xv):
    return lax.rsqrt(jnp.mean(xv * xv, axis=-1, keepdims=True) + EPS)


def _sigmoid(g):
    return 1.0 / (1.0 + jnp.exp(-g))


def _mm_up_joined(name, a, w4, tm_pref):
    t, k = a.shape
    _, _, _, nq = w4.shape
    tm = _pick(t, (tm_pref, 256, 128))

    def body(a_ref, w_ref, o_ref, wcat_ref):
        @pl.when(pl.program_id(0) == 0)
        def _():
            for q in range(N_CHIPS):
                wcat_ref[:, q * nq:(q + 1) * nq] = w_ref[q]

        o_ref[...] = _dot(a_ref[...], wcat_ref[...], NN).astype(BF16)

    return pl.pallas_call(
        body, name=name, grid=(t // tm,),
        in_specs=[pl.BlockSpec((tm, k), lambda i: (i, 0)),
                  pl.BlockSpec((None, N_CHIPS, k, nq), lambda i: (0, 0, 0, 0))],
        out_specs=pl.BlockSpec((tm, N_CHIPS * nq), lambda i: (i, 0)),
        out_shape=_sds((t, N_CHIPS * nq), BF16),
        scratch_shapes=[pltpu.VMEM((k, N_CHIPS * nq), BF16)],
        compiler_params=_params(("arbitrary",)))(a, w4)


def _mm_norm_up_joined(name, x, gain, w4, tm_pref):
    t, k = x.shape
    _, _, _, nq = w4.shape
    tm = _pick(t, (tm_pref, 256, 128))

    def body(x_ref, g_ref, w_ref, h_ref, o_ref, wcat_ref):
        @pl.when(pl.program_id(0) == 0)
        def _():
            for q in range(N_CHIPS):
                wcat_ref[:, q * nq:(q + 1) * nq] = w_ref[q]

        xv = x_ref[...]
        h = ((xv * _rms(xv)) * g_ref[...]).astype(BF16)
        h_ref[...] = h
        o_ref[...] = _dot(h, wcat_ref[...], NN).astype(BF16)

    return pl.pallas_call(
        body, name=name, grid=(t // tm,),
        in_specs=[pl.BlockSpec((tm, k), lambda i: (i, 0)), pl.BlockSpec((1, k), lambda i: (0, 0)),
                  _resident((None, N_CHIPS, k, nq), lambda i: (0, 0, 0, 0))],
        out_specs=[pl.BlockSpec((tm, k), lambda i: (i, 0)), pl.BlockSpec((tm, N_CHIPS * nq), lambda i: (i, 0))],
        out_shape=[_sds((t, k), BF16), _sds((t, N_CHIPS * nq), BF16)],
        scratch_shapes=[pltpu.VMEM((k, N_CHIPS * nq), BF16)],
        compiler_params=_params(("arbitrary",)))(x, gain, w4)


def _mm_up_swiglu(name, h, w4, layer):
    t, k = h.shape
    _, _, _, nq = w4.shape
    tm = _pick(t, (512, 256, 128))

    def body(h_ref, wg_ref, wu_ref, dag_ref, dau_ref, a_ref):
        hv = h_ref[...]
        g = _dot(hv, wg_ref[...], NN)
        u = _dot(hv, wu_ref[...], NN)
        sg = _sigmoid(g)
        silu = g * sg
        a = silu * u
        dag_ref[...] = (a + sg * (u - a)).astype(BF16)
        dau_ref[...] = silu.astype(BF16)
        a_ref[...] = a.astype(BF16)

    half = N_CHIPS // 2
    out = pl.BlockSpec((tm, nq), lambda j, i: (i, j))
    return pl.pallas_call(
        body, name=name, grid=(half, t // tm),
        in_specs=[pl.BlockSpec((tm, k), lambda j, i: (i, 0)),
                  pl.BlockSpec((None, None, k, nq), lambda j, i: (layer, j, 0, 0)),
                  pl.BlockSpec((None, None, k, nq), lambda j, i: (layer, half + j, 0, 0))],
        out_specs=[out, out, out],
        out_shape=[_sds((t, half * nq), BF16)] * 3,
        compiler_params=_params(("parallel", "parallel")))(h, w4, w4)


def _mm_down_norm(name, a, w, layer, res, gain):
    t, kf = a.shape
    _, _, n = w.shape
    tm = _pick(t, (1024, 512, 256, 128))

    def body(a_ref, w_ref, r_ref, g_ref, o_ref, h_ref):
        xo = r_ref[...] + _dot(a_ref[...], w_ref[...], NN)
        o_ref[...] = xo
        h_ref[...] = ((xo * _rms(xo)) * g_ref[...]).astype(BF16)

    row = pl.BlockSpec((tm, n), lambda i: (i, 0))
    return pl.pallas_call(
        body, name=name, grid=(t // tm,),
        in_specs=[pl.BlockSpec((tm, kf), lambda i: (i, 0)),
                  _resident((None, kf, n), lambda i: (layer, 0, 0)),
                  row, pl.BlockSpec((1, n), lambda i: (0, 0))],
        out_specs=[row, row],
        out_shape=[_sds((t, n), F32), _sds((t, n), BF16)],
        compiler_params=_params(("parallel",)))(a, w, res, gain)


def _mm_down_loss(name, a, w, layer, res, tgt):
    t, kf = a.shape
    _, _, n = w.shape
    tm = _pick(t, (1024, 512, 256, 128))
    steps = t // tm

    def body(a_ref, w_ref, r_ref, t_ref, dy_ref, l_ref, acc_ref):
        i = pl.program_id(0)

        @pl.when(i == 0)
        def _():
            acc_ref[...] = jnp.zeros_like(acc_ref)

        e = (r_ref[...] + _dot(a_ref[...], w_ref[...], NN)) - t_ref[...]
        dy_ref[...] = e * (1.0 / n)
        acc_ref[...] += (e * e).reshape(tm // SUBLANES, SUBLANES, n).sum(axis=0)

        @pl.when(i == steps - 1)
        def _():
            l_ref[...] = jnp.sum(acc_ref[...], keepdims=True) * (0.5 / n)

    row = pl.BlockSpec((tm, n), lambda i: (i, 0))
    return pl.pallas_call(
        body, name=name, grid=(steps,),
        in_specs=[pl.BlockSpec((tm, kf), lambda i: (i, 0)),
                  _resident((None, kf, n), lambda i: (layer, 0, 0)), row, row],
        out_specs=[row, pl.BlockSpec((1, 1), lambda i: (0, 0))],
        out_shape=[_sds((t, n), F32), _sds((1, 1), F32)],
        scratch_shapes=[pltpu.VMEM((SUBLANES, n), F32)],
        compiler_params=_params(("arbitrary",)))(a, w, res, tgt)


def _mm_down_t(name, dx, w, layer):
    t, n = dx.shape
    _, kf, _ = w.shape
    tm = _pick(t, (1024, 512, 256, 128))

    def body(a_ref, w_ref, o_ref):
        o_ref[...] = _dot(a_ref[...].astype(BF16), w_ref[...], NT).astype(BF16)

    return pl.pallas_call(
        body, name=name, grid=(t // tm,),
        in_specs=[pl.BlockSpec((tm, n), lambda i: (i, 0)),
                  _resident((None, kf, n), lambda i: (layer, 0, 0))],
        out_specs=pl.BlockSpec((tm, kf), lambda i: (i, 0)),
        out_shape=_sds((t, kf), BF16),
        compiler_params=_params(("parallel",)))(dx, w)


def _mm_down_t_swiglu(name, dx, w, layer, g, u):
    t, n = dx.shape
    f = g.shape[1]
    tm = _pick(t, (512, 256, 128))

    def body(a_ref, w_ref, dag_ref, dau_ref, dg_ref, du_ref):
        da = _dot(a_ref[...].astype(BF16), w_ref[...], NT)
        dg_ref[...] = (da * dag_ref[...].astype(F32)).astype(BF16)
        du_ref[...] = (da * dau_ref[...].astype(F32)).astype(BF16)

    tile = pl.BlockSpec((tm, f), lambda i: (i, 0))
    return pl.pallas_call(
        body, name=name, grid=(t // tm,),
        in_specs=[pl.BlockSpec((tm, n), lambda i: (i, 0)),
                  _resident((None, f, n), lambda i: (layer, 0, 0)), tile, tile],
        out_specs=[tile, tile],
        out_shape=[_sds((t, f), BF16)] * 2,
        compiler_params=_params(("parallel",)))(dx, w, g, u)


def _dgrad_norm(name, acts, act_blocks, pieces, w4, layer, x, gain, dres):
    t, d = x.shape
    _, _, k, nq = w4.shape
    tm = _pick(t, (512, 256, 128))
    n_act = len(acts)

    def body(*refs):
        act_refs = refs[:n_act]
        w_ref, x_ref, g_ref, dr_ref, dx_ref, dg_ref = refs[n_act:]

        @pl.when(pl.program_id(0) == 0)
        def _():
            dg_ref[...] = jnp.zeros_like(dg_ref)

        dh = None
        for a_tile, w_tile in pieces(act_refs, w_ref):
            term = _dot(a_tile, w_tile, NT)
            dh = term if dh is None else dh + term
        xv = x_ref[...]
        r = _rms(xv)
        xhat = xv * r
        gd = dh * g_ref[...]
        dx_ref[...] = dr_ref[...] + r * (gd - xhat * jnp.mean(gd * xhat, axis=-1, keepdims=True))
        dg_ref[...] += (dh * xhat).reshape(tm // SUBLANES, SUBLANES, d).sum(axis=0)

    row = pl.BlockSpec((tm, d), lambda i: (i, 0))
    return pl.pallas_call(
        body, name=name, grid=(t // tm,),
        in_specs=[*act_blocks(tm),
                  _resident((None, N_CHIPS, k, nq), lambda i: (layer, 0, 0, 0)),
                  row, pl.BlockSpec((1, d), lambda i: (0, 0)), row],
        out_specs=[row, pl.BlockSpec((SUBLANES, d), lambda i: (0, 0))],
        out_shape=[_sds((t, d), F32), _sds((SUBLANES, d), F32)],
        compiler_params=_params(("arbitrary",)))(*acts, w4, x, gain, dres)


def _dgrad_norm_ffn(name, dg, du, w4, layer, x, gain, dres):
    nq = w4.shape[3]
    f = dg.shape[1]

    def blocks(tm):
        return [pl.BlockSpec((tm, f), lambda i: (i, 0))] * 2

    def pieces(act_refs, w_ref):
        dg_ref, du_ref = act_refs
        return [(dg_ref[:, 0:nq], w_ref[0]), (dg_ref[:, nq:2 * nq], w_ref[1]),
                (du_ref[:, 0:nq], w_ref[2]), (du_ref[:, nq:2 * nq], w_ref[3])]

    return _dgrad_norm(name, [dg, du], blocks, pieces, w4, layer, x, gain, dres)


def _dgrad_norm_qkv(name, dqkv, w4, x, gain, dres):
    nq = w4.shape[3]

    def blocks(tm):
        return [pl.BlockSpec((tm, N_CHIPS * nq), lambda i: (i, 0))]

    def pieces(act_refs, w_ref):
        return [(act_refs[0][:, q * nq:(q + 1) * nq], w_ref[q]) for q in range(N_CHIPS)]

    return _dgrad_norm(name, [dqkv], blocks, pieces, w4, 0, x, gain, dres)


def _dgrad_norm_conv(name, d3, w4, x, gain, dres):
    _, _, d = d3.shape
    nq = w4.shape[3]
    per_part, per_q = d // MXU_COLS, nq // MXU_COLS

    def blocks(tm):
        return [pl.BlockSpec((3, tm, d), lambda i: (0, i, 0))]

    def pieces(act_refs, w_ref):
        out = []
        for jb in range(3 * per_part):
            ca, cw = (jb % per_part) * MXU_COLS, (jb % per_q) * MXU_COLS
            out.append((act_refs[0][jb // per_part, :, ca:ca + MXU_COLS], w_ref[jb // per_q, :, cw:cw + MXU_COLS]))
        return out

    return _dgrad_norm(name, [d3], blocks, pieces, w4, 0, x, gain, dres)


def _wgrad_up2(name, h, dg, du):
    t, k = h.shape
    nq = dg.shape[1] // 2
    tk = _pick(t, (2048, 1024, 512, 256, 128))
    steps = t // tk
    half = N_CHIPS // 2

    def body(h_ref, dg_ref, du_ref, o_ref):
        q = pl.program_id(0)

        @pl.when(pl.program_id(1) == 0)
        def _():
            o_ref[...] = jnp.zeros_like(o_ref)

        @pl.when(q < half)
        def _():
            o_ref[...] += _dot(h_ref[...], dg_ref[...], TN)

        @pl.when(q >= half)
        def _():
            o_ref[...] += _dot(h_ref[...], du_ref[...], TN)

    return pl.pallas_call(
        body, name=name, grid=(N_CHIPS, steps),
        in_specs=[pl.BlockSpec((tk, k), lambda q, s: (s, 0)),
                  pl.BlockSpec((tk, nq), lambda q, s: (jnp.where(q < half, s, steps - 1), jnp.minimum(q, half - 1))),
                  pl.BlockSpec((tk, nq), lambda q, s: (jnp.where(q >= half, s, 0), jnp.maximum(q - half, 0)))],
        out_specs=pl.BlockSpec((None, k, nq), lambda q, s: (q, 0, 0)),
        out_shape=_sds((N_CHIPS, k, nq), F32),
        compiler_params=_params(("parallel", "arbitrary")))(h, dg, du)


def _wgrad_joined(name, h, dy):
    t, k = h.shape
    nq = dy.shape[1] // N_CHIPS
    tk = _pick(t, (2048, 1024, 512, 256, 128))

    def body(h_ref, dy_ref, o_ref):
        @pl.when(pl.program_id(0) == 0)
        def _():
            o_ref[...] = jnp.zeros_like(o_ref)

        res = _dot(h_ref[...], dy_ref[...], TN)
        for q in range(N_CHIPS):
            o_ref[q] += res[:, q * nq:(q + 1) * nq]

    return pl.pallas_call(
        body, name=name, grid=(t // tk,),
        in_specs=[pl.BlockSpec((tk, k), lambda s: (s, 0)), pl.BlockSpec((tk, N_CHIPS * nq), lambda s: (s, 0))],
        out_specs=pl.BlockSpec((N_CHIPS, k, nq), lambda s: (0, 0, 0)),
        out_shape=_sds((N_CHIPS, k, nq), F32),
        compiler_params=_params(("arbitrary",)))(h, dy)


def _wgrad_conv_in(name, h, d3, nq):
    t, k = h.shape
    d = d3.shape[2]
    per_part, per_q = d // MXU_COLS, nq // MXU_COLS
    tk = _pick(t, (512, 256, 128))

    def body(h_ref, d_ref, o_ref):
        @pl.when(pl.program_id(0) == 0)
        def _():
            o_ref[...] = jnp.zeros_like(o_ref)

        hv = h_ref[...]
        for part in range(3):
            res = _dot(hv, d_ref[part], TN)
            for cc in range(per_part):
                jb = part * per_part + cc
                co = (jb % per_q) * MXU_COLS
                o_ref[jb // per_q, :, co:co + MXU_COLS] += res[:, cc * MXU_COLS:(cc + 1) * MXU_COLS]

    return pl.pallas_call(
        body, name=name, grid=(t // tk,),
        in_specs=[pl.BlockSpec((tk, k), lambda s: (s, 0)), pl.BlockSpec((3, tk, d), lambda s: (0, s, 0))],
        out_specs=pl.BlockSpec((N_CHIPS, k, nq), lambda s: (0, 0, 0)),
        out_shape=_sds((N_CHIPS, k, nq), F32),
        compiler_params=_params(("arbitrary",)))(h, d3)


def _wgrad_down(name, a, dx, tmw):
    t, kf = a.shape
    n = dx.shape[1]
    tk = _pick(t, (2048, 1024, 512, 256, 128))

    def body(a_ref, b_ref, o_ref):
        @pl.when(pl.program_id(1) == 0)
        def _():
            o_ref[...] = jnp.zeros_like(o_ref)

        o_ref[...] += _dot(a_ref[...], b_ref[...].astype(BF16), TN)

    g = pl.pallas_call(
        body, name=name, grid=(kf // tmw, t // tk),
        in_specs=[pl.BlockSpec((tk, tmw), lambda j, s: (s, j)), pl.BlockSpec((tk, n), lambda j, s: (s, 0))],
        out_specs=pl.BlockSpec((tmw, n), lambda j, s: (j, 0)),
        out_shape=_sds((kf, n), F32),
        compiler_params=_params(("parallel", "arbitrary")))(a, dx)
    return g.reshape(N_CHIPS, kf // N_CHIPS, n)


def _shift_rows(u, k, rows):
    s = u.shape[0]
    if k > 0:
        r = pltpu.roll(u, k, 0)
        return jnp.concatenate([jnp.where(rows >= k, r[0:SUBLANES], 0.0), r[SUBLANES:]], axis=0)
    r = pltpu.roll(u, s + k, 0)
    return jnp.concatenate([r[:s - SUBLANES], jnp.where(rows < SUBLANES + k, r[s - SUBLANES:], 0.0)], axis=0)


def _conv_taps(cw_ref, got_ref):
    return (cw_ref[...] + got_ref[0]) + (got_ref[1] + got_ref[2])


def _conv_fwd(bcx, cw, cw_got, nseq, seq):
    t, d3 = bcx.shape
    d = d3 // 3
    cb = 2 * MXU_COLS
    nj = d // cb

    def body(b_ref, c_ref, x_ref, cw_ref, got_ref, z_ref):
        u = b_ref[...].astype(F32) * x_ref[...].astype(F32)
        rows = lax.broadcasted_iota(jnp.int32, (SUBLANES, cb), 0)
        cwv = _conv_taps(cw_ref, got_ref)
        y = cwv[2:3] * u + cwv[1:2] * _shift_rows(u, 1, rows) + cwv[0:1] * _shift_rows(u, 2, rows)
        z_ref[...] = (c_ref[...].astype(F32) * y).astype(BF16)

    return pl.pallas_call(
        body, name="conv_fwd", grid=(nseq, nj),
        in_specs=[pl.BlockSpec((seq, cb), lambda b, j: (b, j)),
                  pl.BlockSpec((seq, cb), lambda b, j: (b, nj + j)),
                  pl.BlockSpec((seq, cb), lambda b, j: (b, 2 * nj + j)),
                  pl.BlockSpec((SUBLANES, cb), lambda b, j: (0, j)),
                  pl.BlockSpec((3, SUBLANES, cb), lambda b, j: (0, 0, j))],
        out_specs=pl.BlockSpec((seq, cb), lambda b, j: (b, j)),
        out_shape=_sds((t, d), BF16),
        compiler_params=_params(("parallel", "parallel")))(bcx, bcx, bcx, cw, cw_got)


def _conv_bwd(dz, bcx, cw, cw_got, nseq, seq):
    t, d3 = bcx.shape
    d = d3 // 3
    cb = MXU_COLS
    nj = d // cb

    def body(dz_ref, b_ref, c_ref, x_ref, cw_ref, got_ref, o_ref, dcw_ref):
        @pl.when(pl.program_id(1) == 0)
        def _():
            dcw_ref[...] = jnp.zeros_like(dcw_ref)

        b = b_ref[...].astype(F32)
        c = c_ref[...].astype(F32)
        xv = x_ref[...].astype(F32)
        dzv = dz_ref[...].astype(F32)
        u = b * xv
        rows = lax.broadcasted_iota(jnp.int32, (SUBLANES, cb), 0)
        u1 = _shift_rows(u, 1, rows)
        u2 = _shift_rows(u, 2, rows)
        cwv = _conv_taps(cw_ref, got_ref)
        y = cwv[2:3] * u + cwv[1:2] * u1 + cwv[0:1] * u2
        dyc = dzv * c
        du = cwv[2:3] * dyc + cwv[1:2] * _shift_rows(dyc, -1, rows) + cwv[0:1] * _shift_rows(dyc, -2, rows)
        o_ref[0] = (du * xv).astype(BF16)
        o_ref[1] = (dzv * y).astype(BF16)
        o_ref[2] = (du * b).astype(BF16)
        s0 = jnp.sum(dyc * u2, axis=0, keepdims=True)
        s1 = jnp.sum(dyc * u1, axis=0, keepdims=True)
        s2 = jnp.sum(dyc * u, axis=0, keepdims=True)
        tap = lax.broadcasted_iota(jnp.int32, (3, cb), 0)
        dcw_ref[...] += jnp.where(tap == 0, s0, jnp.where(tap == 1, s1, s2))

    return pl.pallas_call(
        body, name="conv_bwd", grid=(nj, nseq),
        in_specs=[pl.BlockSpec((seq, cb), lambda j, b: (b, j)),
                  pl.BlockSpec((seq, cb), lambda j, b: (b, j)),
                  pl.BlockSpec((seq, cb), lambda j, b: (b, nj + j)),
                  pl.BlockSpec((seq, cb), lambda j, b: (b, 2 * nj + j)),
                  pl.BlockSpec((SUBLANES, cb), lambda j, b: (0, j)),
                  pl.BlockSpec((3, SUBLANES, cb), lambda j, b: (0, 0, j))],
        out_specs=[pl.BlockSpec((3, seq, cb), lambda j, b: (0, b, j)),
                   pl.BlockSpec((3, cb), lambda j, b: (0, j))],
        out_shape=[_sds((3, t, d), BF16), _sds((3, d), F32)],
        compiler_params=_params(("parallel", "arbitrary")))(dz, bcx, bcx, bcx, cw, cw_got)


def _pair_norm(x, gain_pair, low):
    sq = x * x
    ss_lo = jnp.sum(jnp.where(low, sq, 0.0), axis=-1, keepdims=True)
    ss_hi = jnp.sum(jnp.where(low, 0.0, sq), axis=-1, keepdims=True)
    r = lax.rsqrt(jnp.where(low, ss_lo, ss_hi) * (1.0 / HEAD_DIM) + EPS)
    xhat = x * r
    return xhat * gain_pair, xhat, r


KEYS = 2 * BLOCK
QK_SCALE = 1.0 / (HEAD_DIM ** 0.5)
N_PAIRS = N_Q_HEADS // 2


def _earlier_block(shape=(BLOCK, BLOCK)):
    return lax.broadcasted_iota(jnp.int32, shape, 0) > lax.broadcasted_iota(jnp.int32, shape, 1)


def _fill_bias(bias_ref):
    rows = lax.broadcasted_iota(jnp.int32, (2 * BLOCK, BLOCK), 0)
    qi = lax.broadcasted_iota(jnp.int32, (2 * BLOCK, BLOCK), 1)
    odd_head = rows >= BLOCK
    kj = jnp.where(odd_head, rows - BLOCK, rows)
    earlier = kj > qi
    dist = (jnp.where(earlier, BLOCK, 0) + qi - kj).astype(F32)
    for j in range(N_PAIRS):
        slope = jnp.where(odd_head, ALIBI_SLOPES[2 * j + 1], ALIBI_SLOPES[2 * j])
        bias = -slope * dist
        bias_ref[1, j] = bias
        bias_ref[0, j] = jnp.where(earlier, -1e30, bias)


def _merge_blocks(x_t, earlier):
    return jnp.concatenate([jnp.where(earlier, x_t[e * KEYS:e * KEYS + BLOCK], x_t[e * KEYS + BLOCK:(e + 1) * KEYS])
                            for e in range(2)], axis=0)


def _split_blocks(heads, earlier):
    parts = []
    for x in heads:
        parts += [jnp.where(earlier, x, 0.0), jnp.where(earlier, 0.0, x)]
    return jnp.concatenate(parts, axis=0).astype(BF16)


def _kv_pair_rows(kv_tile, parity, low):
    own = jnp.where(low if parity == 0 else jnp.logical_not(low), kv_tile, 0.0)
    other = pltpu.roll(own, HEAD_DIM, 1)
    lo, hi = (own, other) if parity == 0 else (other, own)
    return jnp.concatenate([lo, hi], axis=0).astype(BF16)


def _pair_softmax(s_t, sink_even, sink_odd):
    out = []
    for e, sink in enumerate((sink_even, sink_odd)):
        se = s_t[e * BLOCK:(e + 1) * BLOCK]
        m = jnp.maximum(jnp.max(se, axis=0, keepdims=True), sink)
        ee = jnp.exp(se - m)
        es = jnp.exp(sink - m)
        inv = 1.0 / (jnp.sum(ee, axis=0, keepdims=True) + es)
        out.append((ee * inv, es * inv))
    return out


def _attn_rows(n):
    q0 = pl.multiple_of(n * BLOCK, BLOCK)
    k0 = pl.multiple_of(jnp.maximum(n - 1, 0) * BLOCK, BLOCK)
    return q0, k0, jnp.minimum(n, 1)


def _key_rows(qkv_ref, k0, q0, col):
    return jnp.concatenate([qkv_ref[pl.ds(k0, BLOCK), col:col + LANES], qkv_ref[pl.ds(q0, BLOCK), col:col + LANES]],
                           axis=0).astype(F32)


def _attn_fwd(qkv, qg_pair, kg_pair, sinks, nseq, seq):
    t = qkv.shape[0]
    dq = N_Q_HEADS * HEAD_DIM
    dkv = N_KV_HEADS * HEAD_DIM

    def body(sk_ref, qkv_ref, qg_ref, kg_ref, o_ref, bias_ref):
        @pl.when(pl.program_id(0) == 0)
        def _():
            _fill_bias(bias_ref)

        low = lax.broadcasted_iota(jnp.int32, (1, LANES), 1) < HEAD_DIM
        earlier = _earlier_block()
        qg = qg_ref[...] * QK_SCALE
        kg = kg_ref[...]

        def blk(n, carry):
            q0, k0, later = _attn_rows(n)
            for kt in range(dkv // LANES):
                kraw = _key_rows(qkv_ref, k0, q0, dq + kt * LANES)
                vraw = _key_rows(qkv_ref, k0, q0, dq + dkv + kt * LANES)
                kn, _, _ = _pair_norm(kraw, kg, low)
                for par in range(2):
                    kh = 2 * kt + par
                    k_pair = _kv_pair_rows(kn, par, low)
                    v_pair = _kv_pair_rows(vraw, par, low)
                    for jj in range(2):
                        j = 2 * kh + jj
                        qraw = qkv_ref[pl.ds(q0, BLOCK), j * LANES:(j + 1) * LANES].astype(F32)
                        qn, _, _ = _pair_norm(qraw, qg, low)
                        s_t = _merge_blocks(_dot(k_pair, qn.astype(BF16), NT), earlier) + bias_ref[later, j]
                        (p0, _), (p1, _) = _pair_softmax(s_t, sk_ref[0, 2 * j], sk_ref[0, 2 * j + 1])
                        p_t = _split_blocks((p0, p1), earlier)
                        o_ref[pl.ds(q0, BLOCK), j * LANES:(j + 1) * LANES] = _dot(p_t, v_pair, TN).astype(BF16)
            return carry

        lax.fori_loop(0, seq // BLOCK, blk, 0)

    return pl.pallas_call(
        body, name="attn_fwd", grid=(nseq,),
        in_specs=[pl.BlockSpec(memory_space=pltpu.SMEM),
                  pl.BlockSpec((seq, dq + 2 * dkv), lambda b: (b, 0)),
                  pl.BlockSpec((1, LANES), lambda b: (0, 0)),
                  pl.BlockSpec((1, LANES), lambda b: (0, 0))],
        out_specs=pl.BlockSpec((seq, dq), lambda b: (b, 0)),
        out_shape=_sds((t, dq), BF16),
        scratch_shapes=[pltpu.VMEM((2, N_PAIRS, 2 * BLOCK, BLOCK), F32)],
        compiler_params=_params(("arbitrary",)))(sinks, qkv, qg_pair, kg_pair)


def _attn_bwd(do, qkv, qg_pair, kg_pair, sinks, nseq, seq):
    t = qkv.shape[0]
    dq = N_Q_HEADS * HEAD_DIM
    dkv = N_KV_HEADS * HEAD_DIM

    def body(sk_ref, do_ref, qkv_ref, qg_ref, kg_ref, o_ref, dqg_ref, dkg_ref, dsk_ref, acc_ref, bias_ref):
        @pl.when(pl.program_id(0) == 0)
        def _():
            _fill_bias(bias_ref)
            dqg_ref[...] = jnp.zeros_like(dqg_ref)
            dkg_ref[...] = jnp.zeros_like(dkg_ref)
            dsk_ref[...] = jnp.zeros_like(dsk_ref)

        acc_ref[...] = jnp.zeros_like(acc_ref)
        low = lax.broadcasted_iota(jnp.int32, (1, LANES), 1) < HEAD_DIM
        earlier = _earlier_block()
        head_row = lax.broadcasted_iota(jnp.int32, (N_Q_HEADS, LANES), 0)
        qg = qg_ref[...] * QK_SCALE
        kg = kg_ref[...]

        def blk(n, carry):
            dqg_acc, dkg_acc, dsk_acc = carry
            q0, k0, later = _attn_rows(n)
            for kt in range(dkv // LANES):
                kraw = _key_rows(qkv_ref, k0, q0, dq + kt * LANES)
                vraw = _key_rows(qkv_ref, k0, q0, dq + dkv + kt * LANES)
                kn, khat, rk = _pair_norm(kraw, kg, low)
                dk_tile = None
                dv_tile = None
                for par in range(2):
                    kh = 2 * kt + par
                    own = low if par == 0 else jnp.logical_not(low)
                    k_pair = _kv_pair_rows(kn, par, low)
                    v_pair = _kv_pair_rows(vraw, par, low)
                    dkn_rows = jnp.zeros((2 * KEYS, LANES), F32)
                    dv_rows = jnp.zeros((2 * KEYS, LANES), F32)
                    for jj in range(2):
                        j = 2 * kh + jj
                        qraw = qkv_ref[pl.ds(q0, BLOCK), j * LANES:(j + 1) * LANES].astype(F32)
                        qn, qhat, rq = _pair_norm(qraw, qg, low)
                        qn_b = qn.astype(BF16)
                        do_b = do_ref[pl.ds(q0, BLOCK), j * LANES:(j + 1) * LANES]
                        s_t = _merge_blocks(_dot(k_pair, qn_b, NT), earlier) + bias_ref[later, j]
                        dp_t = _merge_blocks(_dot(v_pair, do_b, NT), earlier)
                        ds_heads = []
                        probs = _pair_softmax(s_t, sk_ref[0, 2 * j], sk_ref[0, 2 * j + 1])
                        for e, (p, ps) in enumerate(probs):
                            dp = dp_t[e * BLOCK:(e + 1) * BLOCK]
                            dsum = jnp.sum(p * dp, axis=0, keepdims=True)
                            ds_heads.append(p * (dp - dsum))
                            dsk_acc = dsk_acc - jnp.where(head_row == 2 * j + e, ps * dsum, 0.0)
                        p_t = _split_blocks((probs[0][0], probs[1][0]), earlier)
                        ds_t = _split_blocks(ds_heads, earlier)
                        dv_rows = dv_rows + _dot(p_t, do_b, NN)
                        dkn_rows = dkn_rows + _dot(ds_t, qn_b, NN)
                        dqn = _dot(ds_t, k_pair, TN)
                        dqg_acc = dqg_acc + jnp.sum(dqn * qhat, axis=0, keepdims=True)
                        dqhat = dqn * qg
                        prod = dqhat * qhat
                        m_lo = jnp.sum(jnp.where(low, prod, 0.0), axis=-1, keepdims=True)
                        m_hi = jnp.sum(jnp.where(low, 0.0, prod), axis=-1, keepdims=True)
                        mean = jnp.where(low, m_lo, m_hi) * (1.0 / HEAD_DIM)
                        o_ref[pl.ds(q0, BLOCK), j * LANES:(j + 1) * LANES] = (rq * (dqhat - qhat * mean)).astype(BF16)
                    dkn_acc = jnp.where(low, dkn_rows[0:KEYS], dkn_rows[KEYS:2 * KEYS])
                    dv_acc = jnp.where(low, dv_rows[0:KEYS], dv_rows[KEYS:2 * KEYS])
                    dkn = dkn_acc + pltpu.roll(dkn_acc, HEAD_DIM, 1)
                    dvh = dv_acc + pltpu.roll(dv_acc, HEAD_DIM, 1)
                    khat_own = jnp.where(own, khat, 0.0)
                    khat_dup = khat_own + pltpu.roll(khat_own, HEAD_DIM, 1)
                    dkg_acc = dkg_acc + jnp.sum(jnp.where(own, dkn * khat_dup, 0.0), axis=0, keepdims=True)
                    dkhat = dkn * kg
                    mean_k = jnp.sum(dkhat * khat_dup, axis=-1, keepdims=True) * (1.0 / LANES)
                    dk_raw = rk * (dkhat - khat_dup * mean_k)
                    dk_tile = jnp.where(own, dk_raw, 0.0) if dk_tile is None else jnp.where(own, dk_raw, dk_tile)
                    dv_tile = jnp.where(own, dvh, 0.0) if dv_tile is None else jnp.where(own, dvh, dv_tile)
                for r0, part in ((k0, slice(0, BLOCK)), (q0, slice(BLOCK, KEYS))):
                    acc_ref[pl.ds(r0, BLOCK), kt * LANES:(kt + 1) * LANES] += dk_tile[part]
                    acc_ref[pl.ds(r0, BLOCK), dkv + kt * LANES:dkv + (kt + 1) * LANES] += dv_tile[part]
            return dqg_acc, dkg_acc, dsk_acc

        zero = jnp.zeros((1, LANES), F32)
        carry = (zero, zero, jnp.zeros((N_Q_HEADS, LANES), F32))
        dqg_acc, dkg_acc, dsk_acc = lax.fori_loop(0, seq // BLOCK, blk, carry)
        dqg_ref[...] += dqg_acc * QK_SCALE
        dkg_ref[...] += dkg_acc
        dsk_ref[...] += dsk_acc
        o_ref[:, dq:dq + 2 * dkv] = acc_ref[...].astype(BF16)

    small = pl.BlockSpec((1, LANES), lambda b: (0, 0))
    heads = pl.BlockSpec((N_Q_HEADS, LANES), lambda b: (0, 0))
    return pl.pallas_call(
        body, name="attn_bwd", grid=(nseq,),
        in_specs=[pl.BlockSpec(memory_space=pltpu.SMEM),
                  pl.BlockSpec((seq, dq), lambda b: (b, 0)),
                  pl.BlockSpec((seq, dq + 2 * dkv), lambda b: (b, 0)),
                  small, small],
        out_specs=[pl.BlockSpec((seq, dq + 2 * dkv), lambda b: (b, 0)), small, small, heads],
        out_shape=[_sds((t, dq + 2 * dkv), BF16), _sds((1, LANES), F32), _sds((1, LANES), F32),
                   _sds((N_Q_HEADS, LANES), F32)],
        scratch_shapes=[pltpu.VMEM((seq, 2 * dkv), F32), pltpu.VMEM((2, N_PAIRS, 2 * BLOCK, BLOCK), F32)],
        compiler_params=_params(("arbitrary",)))(sinks, do, qkv, qg_pair, kg_pair)


def _place():
    x, y, c = lax.axis_index("x"), lax.axis_index("y"), lax.axis_index("c")
    other_chips = [(1 - x, y), (x, 1 - y), (1 - x, 1 - y)]
    return x, y, c, other_chips


def _half_rows(c, rows):
    rh = rows // 2
    return pl.ds(pl.multiple_of(c * rh, BF16_ROWS), rh)


def _cast_own(name, w, place, layer=None):
    nl, r, cdim = w.shape
    first = 0
    if layer is not None:
        nl, first = 1, layer
    rt = _row_tile(r, 4 * cdim, 2 * ELEMENTWISE_BLOCK)

    def body(s_ref, w_ref, o_ref):
        o_ref[...] = w_ref[...].astype(BF16)

    grid_spec = pltpu.PrefetchScalarGridSpec(
        num_scalar_prefetch=1, grid=(nl, r // rt),
        in_specs=[pl.BlockSpec((None, rt, cdim), lambda l, i, s: (first + l, i, 0))],
        out_specs=pl.BlockSpec((None, None, rt, cdim), lambda l, i, s: (l, s[1], i, 0)))
    return pl.pallas_call(
        body, name=name, grid_spec=grid_spec, out_shape=_sds((nl, N_CHIPS, r, cdim), BF16),
        compiler_params=_params(("parallel", "parallel")))(place, w)


def _gather_protocol(outs, shapes, send_sems, recv_sems):
    n = len(outs)
    x, y, c, other_chips = _place()
    me_chip = 2 * x + y
    sibling = (x, y, 1 - c)

    def rows(u, chip, half):
        return outs[u].at[:, chip, _half_rows(half, shapes[u][2]), :]

    def copy(sem, part, to):
        return pltpu.make_async_remote_copy(src_ref=part, dst_ref=part, send_sem=send_sems.at[sem],
                                            recv_sem=recv_sems.at[sem], device_id=to, device_id_type=MESH)

    sends = []
    for u in range(n):
        for k, chip in enumerate(other_chips):
            cp = copy(6 * u + k, rows(u, me_chip, c), (*chip, c))
            cp.start()
            sends.append(cp)
    for u in range(n):
        for k, chip in enumerate(other_chips):
            got = rows(u, 2 * chip[0] + chip[1], c)
            copy(6 * u + k, got, (*chip, c)).wait_recv()
            cp = copy(6 * u + 3 + k, got, sibling)
            cp.start()
            sends.append(cp)
    for u in range(n):
        for k, chip in enumerate(other_chips):
            copy(6 * u + 3 + k, rows(u, 2 * chip[0] + chip[1], 1 - c), sibling).wait_recv()
    for cp in sends:
        cp.wait_send()


def _hbm_ref(a):
    return jax.new_ref(a, memory_space=pltpu.MemorySpace.HBM)


def _sibling_peer():
    x, y, c, _ = _place()
    return [(x, y, 1 - c)]


def _chip_peers():
    x, y, c, other_chips = _place()
    return [(*chip, c) for chip in other_chips]


def _gather_peers():
    return _chip_peers() + _sibling_peer()


def _on_sequencer(name, collective_id, n_sems, peers, protocol, operands=(), out_types=()):
    n_in, n_out = len(operands), len(out_types)

    def launch(*refs):
        send_sems, recv_sems = refs[n_in + n_out:]
        barrier = pltpu.get_barrier_semaphore()
        targets = peers()
        for peer in targets:
            pl.semaphore_signal(barrier, inc=1, device_id=peer, device_id_type=MESH)
        pl.semaphore_wait(barrier, len(targets))
        protocol(refs[:n_in], refs[n_in:n_in + n_out], send_sems, recv_sems)

    return pl.kernel(
        launch, out_type=tuple(out_types), mesh=plsc.ScalarSubcoreMesh(axis_name="sequencer", num_cores=1), name=name,
        scratch_types=(pltpu.SemaphoreType.DMA((n_sems,)), pltpu.SemaphoreType.DMA((n_sems,))),
        compiler_params=pltpu.CompilerParams(collective_id=collective_id))(*operands)


def _seq_allgather(name, collective_id, bufs):
    shapes = [b.shape for b in bufs]
    refs = [_hbm_ref(b) for b in bufs]
    _on_sequencer(name, collective_id, 6 * len(bufs), _gather_peers,
                  lambda ins, outs, send_sems, recv_sems: _gather_protocol(refs, shapes, send_sems, recv_sems))
    return [r[...] for r in refs]


def _taps_protocol(block_ref, got_ref, send_sems, recv_sems, first_sem):
    x, y, c, other_chips = _place()
    copies = []
    for k, chip in enumerate(other_chips):
        cp = pltpu.make_async_remote_copy(src_ref=block_ref, dst_ref=got_ref.at[k], send_sem=send_sems.at[first_sem + k],
                                          recv_sem=recv_sems.at[first_sem + k], device_id=(*chip, c), device_id_type=MESH)
        cp.start()
        copies.append(cp)
    return copies


def _seq_allgather_conv(collective_id, bufs, cw_block):
    shapes = [b.shape for b in bufs]
    refs = [_hbm_ref(b) for b in bufs]

    def protocol(ins, outs, send_sems, recv_sems):
        taps = _taps_protocol(ins[0], outs[0], send_sems, recv_sems, 6 * len(bufs))
        _gather_protocol(refs, shapes, send_sems, recv_sems)
        for cp in taps:
            cp.wait_recv()
        for cp in taps:
            cp.wait_send()

    (got,) = _on_sequencer("allgather_conv", collective_id, 6 * len(bufs) + 3, _gather_peers, protocol,
                           operands=(cw_block,), out_types=(_sds((3, *cw_block.shape), F32),))
    return [r[...] for r in refs], got


def _exchange_protocol(gs, outs, shapes, send_sems, recv_sems):
    x, y, c, _ = _place()
    sends = []
    for u in range(len(gs)):
        cp = pltpu.make_async_remote_copy(
            src_ref=gs[u].at[:, _half_rows(1 - c, shapes[u][1]), :], dst_ref=outs[u],
            send_sem=send_sems.at[u], recv_sem=recv_sems.at[u], device_id=(x, y, 1 - c), device_id_type=MESH)
        cp.start()
        sends.append(cp)
    for cp in sends:
        cp.wait_recv()
    for cp in sends:
        cp.wait_send()


def _seq_exchange(name, collective_id, grads):
    shapes = [g.shape for g in grads]
    return _on_sequencer(
        name, collective_id, len(grads), _sibling_peer,
        lambda gs, outs, send_sems, recv_sems: _exchange_protocol(gs, outs, shapes, send_sems, recv_sems),
        operands=grads, out_types=[_sds((s[0], s[1] // 2, s[2]), F32) for s in shapes])


def _sum_halves(name, g, got, place, after):
    _, r, cdim = g.shape
    rh = r // 2
    rt = _row_tile(rh, 4 * N_CHIPS * cdim, 4 * ELEMENTWISE_BLOCK)
    nr = rh // rt

    def body(s_ref, g_ref, got_ref, after_ref, pb_ref, pf_ref):
        pb_ref[...] = (g_ref[...] + got_ref[...]).astype(BF16)
        mine = s_ref[1]
        pf_ref[...] = g_ref[mine] + got_ref[mine]

    quarters = (N_CHIPS, rt, cdim)
    grid_spec = pltpu.PrefetchScalarGridSpec(
        num_scalar_prefetch=1, grid=(nr,),
        in_specs=[pl.BlockSpec(quarters, lambda i, s: (0, s[0] * nr + i, 0)),
                  pl.BlockSpec(quarters, lambda i, s: (0, i, 0)),
                  pl.BlockSpec(memory_space=pl.ANY)],
        out_specs=[pl.BlockSpec(quarters, lambda i, s: (0, i, 0)),
                   pl.BlockSpec((rt, cdim), lambda i, s: (i, 0))])
    return pl.pallas_call(
        body, name=name, grid_spec=grid_spec,
        out_shape=[_sds((N_CHIPS, rh, cdim), BF16), _sds((rh, cdim), F32)],
        compiler_params=_params(("parallel",)))(place, g, got, after)


def _scatter_protocol(ps, outs, send_sems, recv_sems):
    x, y, c, other_chips = _place()
    sends = []
    for u in range(len(ps)):
        for k, chip in enumerate(other_chips):
            cp = pltpu.make_async_remote_copy(
                src_ref=ps[u].at[2 * chip[0] + chip[1]], dst_ref=outs[u].at[k],
                send_sem=send_sems.at[3 * u + k], recv_sem=recv_sems.at[3 * u + k],
                device_id=(*chip, c), device_id_type=MESH)
            cp.start()
            sends.append(cp)
    for cp in sends:
        cp.wait_recv()
    for cp in sends:
        cp.wait_send()


def _seq_scatter(name, collective_id, partials):
    return _on_sequencer(
        name, collective_id, 3 * len(partials), _chip_peers, _scatter_protocol,
        operands=partials, out_types=[_sds((3, p.shape[1], p.shape[2]), BF16) for p in partials])


def _sum_partials(name, own, got, place, layer, nl, prev, after):
    rh, cdim = own.shape
    rt = _row_tile(rh, 4 * cdim, 2 * ELEMENTWISE_BLOCK)
    nr = rh // rt
    after = list(after) if isinstance(after, (list, tuple)) else [after]

    def body(s_ref, own_ref, got_ref, *rest):
        o_ref = rest[-1]
        o_ref[...] = ((own_ref[...] + got_ref[0].astype(F32)) + got_ref[1].astype(F32)) + got_ref[2].astype(F32)

    in_specs = [pl.BlockSpec((rt, cdim), lambda i, s: (i, 0)), pl.BlockSpec((3, rt, cdim), lambda i, s: (0, i, 0)),
                *[pl.BlockSpec(memory_space=pl.ANY)] * len(after)]
    args = [place, own, got, *after]
    aliases = {}
    if prev is not None:
        in_specs.append(pl.BlockSpec(memory_space=pl.ANY))
        aliases = {len(args): 0}
        args.append(prev)
    grid_spec = pltpu.PrefetchScalarGridSpec(
        num_scalar_prefetch=1, grid=(nr,), in_specs=in_specs,
        out_specs=pl.BlockSpec((None, rt, cdim), lambda i, s: (layer, s[0] * nr + i, 0)))
    return pl.pallas_call(
        body, name=name, grid_spec=grid_spec, out_shape=_sds((nl, 2 * rh, cdim), F32),
        input_output_aliases=aliases, compiler_params=_params(("parallel",)))(*args)


def _share_protocol(outs, shapes, units, send_sems, recv_sems):
    x, y, c, _ = _place()
    sends = []
    for u, (w, l) in enumerate(units):
        mine = outs[w].at[l, _half_rows(c, shapes[w][1]), :]
        cp = pltpu.make_async_remote_copy(src_ref=mine, dst_ref=mine, send_sem=send_sems.at[u],
                                          recv_sem=recv_sems.at[u], device_id=(x, y, 1 - c), device_id_type=MESH)
        cp.start()
        sends.append(cp)
    for u, (w, l) in enumerate(units):
        theirs = outs[w].at[l, _half_rows(1 - c, shapes[w][1]), :]
        pltpu.make_async_remote_copy(src_ref=theirs, dst_ref=theirs, send_sem=send_sems.at[u],
                                     recv_sem=recv_sems.at[u], device_id=(x, y, 1 - c),
                                     device_id_type=MESH).wait_recv()
    for cp in sends:
        cp.wait_send()


def _seq_share(name, collective_id, bufs):
    shapes = [b.shape for b in bufs]
    units = [(w, l) for w in range(len(bufs)) for l in range(shapes[w][0])]
    refs = [_hbm_ref(b) for b in bufs]
    _on_sequencer(name, collective_id, len(units), _sibling_peer,
                  lambda ins, outs, send_sems, recv_sems: _share_protocol(refs, shapes, units, send_sems, recv_sems))
    return [r[...] for r in refs]


def _gather_blocks(block_ref, all_ref, send_sems, recv_sems):
    x, y, c, _ = _place()
    me = 4 * x + 2 * y + c
    all_ref[me] = block_ref[...]
    sends = []
    for rel in range(1, 8):
        fx, fy, fc = (rel >> 2) & 1, (rel >> 1) & 1, rel & 1
        peer = (x ^ fx, y ^ fy, c ^ fc)
        cp = pltpu.make_async_remote_copy(src_ref=block_ref, dst_ref=all_ref.at[me], send_sem=send_sems.at[rel - 1],
                                          recv_sem=recv_sems.at[rel - 1], device_id=peer, device_id_type=MESH)
        cp.start()
        sends.append(cp)
    for cp in sends:
        cp.wait_recv()
    for cp in sends:
        cp.wait_send()


def _adam(w, g, m, v):
    m_new = ADAM_B1 * m + (1.0 - ADAM_B1) * g
    v_new = ADAM_B2 * v + (1.0 - ADAM_B2) * (g * g)
    m_hat = m_new / (1.0 - ADAM_B1 ** ADAM_STEP)
    v_hat = v_new / (1.0 - ADAM_B2 ** ADAM_STEP)
    delta = -ADAM_LR * (m_hat / (jnp.sqrt(v_hat) + ADAM_EPS) + ADAM_WD * w)
    return delta, m_new, v_new


def _small_step(dnm0, dnm1, dnf0, dnf1, dcw, dqg, dkg, dsk, loss, w_blk, m_blk, v_blk, cw_cols):
    d = w_blk.shape[1]
    vm = pl.BlockSpec(memory_space=pltpu.VMEM)

    def reduce_body(dnm0_ref, dnm1_ref, dnf0_ref, dnf1_ref, dcw_ref, dqg_ref, dkg_ref, dsk_ref, loss_ref,
                    g_ref, blk_ref, all_ref, send_sems, recv_sems):
        blk_ref[...] = jnp.zeros_like(blk_ref)
        for row, part_ref in ((SENT_NORM_MIXER, dnm0_ref), (SENT_NORM_MIXER + 1, dnm1_ref),
                              (SENT_NORM_FFN, dnf0_ref), (SENT_NORM_FFN + 1, dnf1_ref)):
            blk_ref[row:row + 1, :] = jnp.sum(part_ref[...], axis=0, keepdims=True)
        blk_ref[SENT_CONV_W:SENT_CONV_W + 3, :] = dcw_ref[...]
        misc = slice(SENT_MISC, SENT_MISC + 1)
        for tile, gain_ref in ((TILE_Q_GAIN, dqg_ref), (TILE_K_GAIN, dkg_ref)):
            pair = gain_ref[...]
            blk_ref[misc, tile * LANES:(tile + 1) * LANES] = pair + pltpu.roll(pair, HEAD_DIM, 1)
        for h in range(N_Q_HEADS):
            lane = TILE_SINKS * LANES + h
            blk_ref[misc, lane:lane + 1] = jnp.sum(dsk_ref[h:h + 1, :], axis=1, keepdims=True)
        blk_ref[misc, TILE_LOSS * LANES:(TILE_LOSS + 1) * LANES] = jnp.broadcast_to(loss_ref[...], (1, LANES))
        _gather_blocks(blk_ref, all_ref, send_sems, recv_sems)
        g = all_ref[0]
        for dev in range(1, 8):
            g = g + all_ref[dev]
        g_ref[...] = jnp.zeros_like(g_ref)
        for sent, row, n in ((SENT_NORM_MIXER, ROW_NORM_MIXER, 2), (SENT_NORM_FFN, ROW_NORM_FFN, 2),
                             (SENT_CONV_W, ROW_CONV_W, 3), (SENT_MISC, ROW_MISC, 1)):
            g_ref[row:row + n, :] = g[sent:sent + n]

    g_blk = pl.pallas_call(
        reduce_body, name="small_allreduce", in_specs=[vm] * 9, out_specs=vm, out_shape=_sds((SMALL_ROWS, d), F32),
        scratch_shapes=[pltpu.VMEM((SUBLANES, d), F32), pltpu.VMEM((8, SUBLANES, d), F32),
                        pltpu.SemaphoreType.DMA((7,)), pltpu.SemaphoreType.DMA((7,))],
    )(dnm0, dnm1, dnf0, dnf1, dcw, dqg, dkg, dsk, loss)

    def body(g_ref, w_ref, m_ref, v_ref, *out_refs):
        g = g_ref[...]
        misc = slice(ROW_MISC, ROW_MISC + 1)
        out_refs[0][...] = g[misc, TILE_LOSS * LANES:TILE_LOSS * LANES + 1]
        chip = 2 * lax.axis_index("x") + lax.axis_index("y")
        for i, blk in enumerate((g, *_adam(w_ref[...], g, m_ref[...], v_ref[...]))):
            nm_ref, nf_ref, cw_ref, qg_ref, kg_ref, sk_ref = out_refs[1 + 6 * i:7 + 6 * i]
            nm_ref[...] = blk[ROW_NORM_MIXER:ROW_NORM_MIXER + 2]
            nf_ref[...] = blk[ROW_NORM_FFN:ROW_NORM_FFN + 2]
            qg_ref[...] = blk[misc, TILE_Q_GAIN * LANES:TILE_Q_GAIN * LANES + HEAD_DIM]
            kg_ref[...] = blk[misc, TILE_K_GAIN * LANES:TILE_K_GAIN * LANES + HEAD_DIM]
            sk_ref[...] = blk[misc, TILE_SINKS * LANES:TILE_SINKS * LANES + N_Q_HEADS]
            for q in range(N_CHIPS):
                @pl.when(chip == q)
                def _(blk=blk, cw_ref=cw_ref, q=q):
                    cw_ref[0] = blk[ROW_CONV_W:ROW_CONV_W + 3, q * cw_cols:(q + 1) * cw_cols]

    group = [_sds((2, d), F32), _sds((2, d), F32), _sds((1, 3, cw_cols), F32), _sds((1, HEAD_DIM), F32),
             _sds((1, HEAD_DIM), F32), _sds((1, N_Q_HEADS), F32)]
    outs = pl.pallas_call(
        body, name="small_adam", in_specs=[vm] * 4, out_specs=[vm] * 25, out_shape=[_sds((1, 1), F32)] + group * 4,
    )(g_blk, w_blk, m_blk, v_blk)
    names = ("norm_mixer", "norm_ffn", "conv_w", "attn_q_gain", "attn_k_gain", "attn_sinks")
    return outs[0], [dict(zip(names, outs[1 + 6 * i:7 + 6 * i])) for i in range(4)]


def _adam_step(name, w, g, m, v):
    nl, r, cdim = w.shape
    rt = _row_tile(r, 4 * cdim, ELEMENTWISE_BLOCK)

    def body(w_ref, g_ref, m_ref, v_ref, go_ref, d_ref, mo_ref, vo_ref):
        gv = g_ref[...]
        go_ref[...] = gv
        delta, m_new, v_new = _adam(w_ref[...], gv, m_ref[...], v_ref[...])
        d_ref[...] = delta
        mo_ref[...] = m_new
        vo_ref[...] = v_new

    spec = pl.BlockSpec((None, rt, cdim), lambda l, i: (l, i, 0))
    return pl.pallas_call(
        body, name=name, grid=(nl, r // rt), in_specs=[spec] * 4, out_specs=[spec] * 4,
        out_shape=[_sds(w.shape, F32)] * 4,
        compiler_params=_params(("parallel", "parallel")))(w, g, m, v)


def _pad_rows(a, rows=SUBLANES):
    return jnp.pad(a, ((0, rows - a.shape[0]), (0, 0)))


def _small_block(nm, nf, cw_local, qg, kg, sk, chip):
    d = nm.shape[1]
    cw_rows = lax.dynamic_update_slice(jnp.zeros((SUBLANES, d), F32), cw_local, (0, chip * cw_local.shape[1]))
    misc = jnp.concatenate([qg, qg, kg, kg, jnp.pad(sk, ((0, 0), (0, LANES - sk.shape[1]))),
                            jnp.zeros((1, d - 3 * LANES), F32)], axis=1)
    return jnp.concatenate([_pad_rows(nm), _pad_rows(nf), cw_rows, _pad_rows(misc)], axis=0)


WEIGHT_NAMES = ("conv_w_in", "conv_w", "conv_w_out", "attn_w_qkv", "attn_q_gain", "attn_k_gain", "attn_sinks",
                "attn_w_o", "norm_mixer", "norm_ffn", "ffn_w_gate_up", "ffn_w_down")
BIG = ("conv_w_in", "conv_w_out", "attn_w_qkv", "attn_w_o", "ffn_w_gate_up", "ffn_w_down")


def kernel(x, conv_w_in, conv_w, conv_w_out, attn_w_qkv, attn_q_gain, attn_k_gain, attn_sinks, attn_w_o, norm_mixer, norm_ffn, ffn_w_gate_up, ffn_w_down, loss_target, m_conv_w_in, m_conv_w, m_conv_w_out, m_attn_w_qkv, m_attn_q_gain, m_attn_k_gain, m_attn_sinks, m_attn_w_o, m_norm_mixer, m_norm_ffn, m_ffn_w_gate_up, m_ffn_w_down, v_conv_w_in, v_conv_w, v_conv_w_out, v_attn_w_qkv, v_attn_q_gain, v_attn_k_gain, v_attn_sinks, v_attn_w_o, v_norm_mixer, v_norm_ffn, v_ffn_w_gate_up, v_ffn_w_down):
    w = dict(conv_w_in=conv_w_in, conv_w=conv_w, conv_w_out=conv_w_out, attn_w_qkv=attn_w_qkv,
             attn_q_gain=attn_q_gain, attn_k_gain=attn_k_gain, attn_sinks=attn_sinks, attn_w_o=attn_w_o,
             norm_mixer=norm_mixer, norm_ffn=norm_ffn, ffn_w_gate_up=ffn_w_gate_up, ffn_w_down=ffn_w_down)
    m = dict(conv_w_in=m_conv_w_in, conv_w=m_conv_w, conv_w_out=m_conv_w_out, attn_w_qkv=m_attn_w_qkv,
             attn_q_gain=m_attn_q_gain, attn_k_gain=m_attn_k_gain, attn_sinks=m_attn_sinks, attn_w_o=m_attn_w_o,
             norm_mixer=m_norm_mixer, norm_ffn=m_norm_ffn, ffn_w_gate_up=m_ffn_w_gate_up, ffn_w_down=m_ffn_w_down)
    v = dict(conv_w_in=v_conv_w_in, conv_w=v_conv_w, conv_w_out=v_conv_w_out, attn_w_qkv=v_attn_w_qkv,
             attn_q_gain=v_attn_q_gain, attn_k_gain=v_attn_k_gain, attn_sinks=v_attn_sinks, attn_w_o=v_attn_w_o,
             norm_mixer=v_norm_mixer, norm_ffn=v_norm_ffn, ffn_w_gate_up=v_ffn_w_gate_up, ffn_w_down=v_ffn_w_down)

    nseq, seq, d = x.shape
    t = nseq * seq
    chip = 2 * lax.axis_index("x") + lax.axis_index("y")
    core = lax.axis_index("c")
    place = jnp.stack([core, chip]).astype(jnp.int32)
    x0 = x.reshape(t, d)
    tgt = loss_target.reshape(t, d)

    cw_block = lax.dynamic_update_slice(jnp.zeros((SUBLANES, d), F32), conv_w[0], (0, chip * conv_w.shape[2]))
    def cast(k, layer=None):
        return _cast_own(f"cast_{k}" + ("" if layer is None else str(layer)), w[k], place, layer)

    (w_in,), cw_got = _seq_allgather_conv(1, [cast("conv_w_in")], cw_block)
    w_out, w_gu0, w_dn0 = _seq_allgather(
        "allgather_ffn0", 2, [cast("conv_w_out"), cast("ffn_w_gate_up", 0), cast("ffn_w_down", 0)])
    w_qkv, w_o, w_gu1, w_dn1 = _seq_allgather(
        "allgather_rest", 3, [cast("attn_w_qkv"), cast("attn_w_o"), cast("ffn_w_gate_up", 1), cast("ffn_w_down", 1)])
    w_out = w_out.reshape(1, d, d)
    w_o = w_o.reshape(1, d, d)
    w_gu = [w_gu0, w_gu1]
    w_dn = [w_dn0.reshape(1, D_FF, d), w_dn1.reshape(1, D_FF, d)]

    qg_pair = jnp.concatenate([attn_q_gain, attn_q_gain], axis=1)
    kg_pair = jnp.concatenate([attn_k_gain, attn_k_gain], axis=1)

    def ffn_bwd(i, dxo, xin, h, g, u, a):
        g_dn = _wgrad_down(f"ffn{i}_down_wgrad", a, dxo, D_FF // 2)
        dg, du = _mm_down_t_swiglu(f"ffn{i}_down_dgrad", dxo, w_dn[i], 0, g, u)
        g_gu = _wgrad_up2(f"ffn{i}_up_wgrad", h, dg, du)
        dxi, dgain = _dgrad_norm_ffn(f"ffn{i}_up_dgrad", dg, du, w_gu[i], 0, xin, norm_ffn[i:i + 1], dxo)
        return dxi, dgain, g_gu, g_dn

    h0, bcx = _mm_norm_up_joined("conv_in", x0, norm_mixer[0:1], w_in, 512)
    z = _conv_fwd(bcx, cw_block, cw_got, nseq, seq)
    x1, h1 = _mm_down_norm("conv_out", z, w_out, 0, x0, norm_ffn[0:1])
    g0, u0, a0 = _mm_up_swiglu("ffn0_up", h1, w_gu[0], 0)
    x2, h2 = _mm_down_norm("ffn0_down", a0, w_dn[0], 0, x1, norm_mixer[1:2])
    qkv = _mm_up_joined("attn_qkv", h2, w_qkv, 1024)
    o = _attn_fwd(qkv, qg_pair, kg_pair, attn_sinks, nseq, seq)
    x3, h3 = _mm_down_norm("attn_out", o, w_o, 0, x2, norm_ffn[1:2])
    g1, u1, a1 = _mm_up_swiglu("ffn1_up", h3, w_gu[1], 0)
    dy, loss_part = _mm_down_loss("ffn1_down", a1, w_dn[1], 0, x3, tgt)

    finished = {k: None for k in BIG}

    def exchange(tag, cid, units):
        return units, _seq_exchange(f"exchange_{tag}", cid, [g for _, _, g in units])

    def scatter(tag, cid, group, after):
        units, got = group
        sums = [_sum_halves(f"sum_halves_{k}{l}", g, r, place, after) for (k, l, g), r in zip(units, got)]
        return units, sums, _seq_scatter(f"scatter_{tag}", cid, [pb for pb, _ in sums])

    def finish(group, after):
        units, sums, arrived = group
        for (k, l, _), (_, pf), r in zip(units, sums, arrived):
            finished[k] = _sum_partials(f"sum_partials_{k}{l}", pf, r, place, l, w[k].shape[0], finished[k], after)

    dx3, dnf1, g_gu1, g_dn1 = ffn_bwd(1, dy, x3, h3, g1, u1, a1)
    ffn1 = exchange("ffn1", 4, [("ffn_w_down", 1, g_dn1), ("ffn_w_gate_up", 1, g_gu1)])
    g_o = _wgrad_down("attn_out_wgrad", o, dx3, d)
    do = _mm_down_t("attn_out_dgrad", dx3, w_o, 0)
    ffn1 = scatter("ffn1", 8, ffn1, do)
    dqkv, dqg, dkg, dsk = _attn_bwd(do, qkv, qg_pair, kg_pair, attn_sinks, nseq, seq)
    g_qkv = _wgrad_joined("attn_qkv_wgrad", h2, dqkv)
    attn = exchange("attn", 5, [("attn_w_o", 0, g_o), ("attn_w_qkv", 0, g_qkv)])
    dx2, dnm1 = _dgrad_norm_qkv("attn_qkv_dgrad", dqkv, w_qkv, x2, norm_mixer[1:2], dx3)
    finish(ffn1, dx2)
    attn = scatter("attn", 9, attn, dx2)
    dx1, dnf0, g_gu0, g_dn0 = ffn_bwd(0, dx2, x1, h1, g0, u0, a0)
    ffn0 = exchange("ffn0", 6, [("ffn_w_down", 0, g_dn0), ("ffn_w_gate_up", 0, g_gu0)])
    g_out = _wgrad_down("conv_out_wgrad", z, dx1, d)
    dz = _mm_down_t("conv_out_dgrad", dx1, w_out, 0)
    finish(attn, dz)
    ffn0 = scatter("ffn0", 10, ffn0, dz)
    dbcx, dcw = _conv_bwd(dz, bcx, cw_block, cw_got, nseq, seq)
    g_in = _wgrad_conv_in("conv_in_wgrad", h0, dbcx, conv_w_in.shape[2])
    conv = exchange("conv", 7, [("conv_w_out", 0, g_out), ("conv_w_in", 0, g_in)])
    dx0, dnm0 = _dgrad_norm_conv("conv_in_dgrad", dbcx, w_in, x0, norm_mixer[0:1], dx1)
    finish(ffn0, dx0)
    late = ("attn_w_qkv", "attn_w_o", "ffn_w_gate_up", "ffn_w_down")
    grads_late = _seq_share("share_late", 12, [finished[k] for k in late])
    conv = scatter("conv", 11, conv, dx0)

    grad, delta, new_m, new_v = {}, {}, {}, {}

    def adam(k, g):
        grad[k], delta[k], new_m[k], new_v[k] = _adam_step(f"adam_{k}", w[k], g, m[k], v[k])

    for k, g in zip(late, grads_late):
        adam(k, g)

    def blocks(src):
        return _small_block(src["norm_mixer"], src["norm_ffn"], src["conv_w"][0], src["attn_q_gain"],
                            src["attn_k_gain"], src["attn_sinks"], chip)

    loss, small = _small_step(dnm0, dnm1, dnf0, dnf1, dcw, dqg, dkg, dsk, loss_part,
                              blocks(w), blocks(m), blocks(v), conv_w.shape[2])
    for dst, part in zip((grad, delta, new_m, new_v), small):
        dst.update(part)

    finish(conv, [new_v[k] for k in late])
    last = ("conv_w_in", "conv_w_out")
    for k, g in zip(last, _seq_share("share_last", 13, [finished[k] for k in last])):
        adam(k, g)

    return (loss.reshape(()), dx0.reshape(nseq, seq, d), *[grad[k] for k in WEIGHT_NAMES], *[delta[k] for k in WEIGHT_NAMES],
            *[new_m[k] for k in WEIGHT_NAMES], *[new_v[k] for k in WEIGHT_NAMES])
```

```python
import jax
import jax.numpy as jnp
from jax import lax
from jax.experimental import pallas as pl
from jax.experimental.pallas import tpu as pltpu
from jax.experimental.pallas import tpu_sc as plsc

F32 = jnp.float32
BF16 = jnp.bfloat16

D_FF = 2816
N_Q_HEADS = 16
N_KV_HEADS = 4
HEAD_DIM = 64
WINDOW = 128
BLOCK = 128
EPS = 1e-6
N_CHIPS = 4
LANES = 128
SUBLANES = 8
BF16_ROWS = 16
MXU_COLS = 256
VMEM_LIMIT = 48 * 1024 * 1024
ADAM_LR, ADAM_B1, ADAM_B2, ADAM_EPS, ADAM_WD, ADAM_STEP = 0.001, 0.9, 0.999, 1e-08, 0.01, 10
ALIBI_SLOPES = tuple(2.0 ** (-8.0 * (h + 1) / N_Q_HEADS) for h in range(N_Q_HEADS))
SMALL_ROWS = 32
ROW_NORM_MIXER, ROW_NORM_FFN, ROW_CONV_W, ROW_MISC = 0, 8, 16, 24
SENT_NORM_MIXER, SENT_NORM_FFN, SENT_CONV_W, SENT_MISC = 0, 2, 4, 7
TILE_Q_GAIN, TILE_K_GAIN, TILE_SINKS, TILE_LOSS = 0, 1, 2, 3
MESH = pl.DeviceIdType.MESH

NN = ((1,), (0,))
NT = ((1,), (1,))
TN = ((0,), (0,))


def _dot(a, b, dims):
    return lax.dot_general(a, b, (dims, ((), ())), preferred_element_type=F32)


def _pick(n, cands):
    for c in cands:
        if n % c == 0:
            return c
    raise ValueError((n, cands))


def _row_tile(rows, row_bytes, cap_bytes):
    fits = [r for r in range(BF16_ROWS, rows + 1, BF16_ROWS) if rows % r == 0 and r * row_bytes <= cap_bytes]
    if not fits:
        raise ValueError((rows, row_bytes, cap_bytes))
    return fits[-1]


ELEMENTWISE_BLOCK = 3 << 19


def _resident(block_shape, index_map):
    return pl.BlockSpec(block_shape, index_map, pipeline_mode=pl.Buffered(1))


def _params(sem):
    return pltpu.CompilerParams(dimension_semantics=sem, vmem_limit_bytes=VMEM_LIMIT)


def _sds(shape, dtype):
    return jax.ShapeDtypeStruct(shape, dtype)


def _rms(xv):
    return lax.rsqrt(jnp.mean(xv * xv, axis=-1, keepdims=True) + EPS)


def _sigmoid(g):
    return 1.0 / (1.0 + jnp.exp(-g))


def _mm_up_joined(name, a, w4, tm_pref):
    t, k = a.shape
    _, _, _, nq = w4.shape
    tm = _pick(t, (tm_pref, 256, 128))

    def body(a_ref, w_ref, o_ref, wcat_ref):
        @pl.when(pl.program_id(0) == 0)
        def _():
            for q in range(N_CHIPS):
                wcat_ref[:, q * nq:(q + 1) * nq] = w_ref[q]

        o_ref[...] = _dot(a_ref[...], wcat_ref[...], NN).astype(BF16)

    return pl.pallas_call(
        body, name=name, grid=(t // tm,),
        in_specs=[pl.BlockSpec((tm, k), lambda i: (i, 0)),
                  pl.BlockSpec((None, N_CHIPS, k, nq), lambda i: (0, 0, 0, 0))],
        out_specs=pl.BlockSpec((tm, N_CHIPS * nq), lambda i: (i, 0)),
        out_shape=_sds((t, N_CHIPS * nq), BF16),
        scratch_shapes=[pltpu.VMEM((k, N_CHIPS * nq), BF16)],
        compiler_params=_params(("arbitrary",)))(a, w4)


def _mm_norm_up_joined(name, x, gain, w4, tm_pref):
    t, k = x.shape
    _, _, _, nq = w4.shape
    tm = _pick(t, (tm_pref, 256, 128))

    def body(x_ref, g_ref, w_ref, h_ref, o_ref, wcat_ref):
        @pl.when(pl.program_id(0) == 0)
        def _():
            for q in range(N_CHIPS):
                wcat_ref[:, q * nq:(q + 1) * nq] = w_ref[q]

        xv = x_ref[...]
        h = ((xv * _rms(xv)) * g_ref[...]).astype(BF16)
        h_ref[...] = h
        o_ref[...] = _dot(h, wcat_ref[...], NN).astype(BF16)

    return pl.pallas_call(
        body, name=name, grid=(t // tm,),
        in_specs=[pl.BlockSpec((tm, k), lambda i: (i, 0)), pl.BlockSpec((1, k), lambda i: (0, 0)),
                  _resident((None, N_CHIPS, k, nq), lambda i: (0, 0, 0, 0))],
        out_specs=[pl.BlockSpec((tm, k), lambda i: (i, 0)), pl.BlockSpec((tm, N_CHIPS * nq), lambda i: (i, 0))],
        out_shape=[_sds((t, k), BF16), _sds((t, N_CHIPS * nq), BF16)],
        scratch_shapes=[pltpu.VMEM((k, N_CHIPS * nq), BF16)],
        compiler_params=_params(("arbitrary",)))(x, gain, w4)


def _mm_up_swiglu(name, h, w4, layer):
    t, k = h.shape
    _, _, _, nq = w4.shape
    tm = _pick(t, (512, 256, 128))

    def body(h_ref, wg_ref, wu_ref, dag_ref, dau_ref, a_ref):
        hv = h_ref[...]
        g = _dot(hv, wg_ref[...], NN)
        u = _dot(hv, wu_ref[...], NN)
        sg = _sigmoid(g)
        silu = g * sg
        a = silu * u
        dag_ref[...] = (a + sg * (u - a)).astype(BF16)
        dau_ref[...] = silu.astype(BF16)
        a_ref[...] = a.astype(BF16)

    half = N_CHIPS // 2
    out = pl.BlockSpec((tm, nq), lambda j, i: (i, j))
    return pl.pallas_call(
        body, name=name, grid=(half, t // tm),
        in_specs=[pl.BlockSpec((tm, k), lambda j, i: (i, 0)),
                  pl.BlockSpec((None, None, k, nq), lambda j, i: (layer, j, 0, 0)),
                  pl.BlockSpec((None, None, k, nq), lambda j, i: (layer, half + j, 0, 0))],
        out_specs=[out, out, out],
        out_shape=[_sds((t, half * nq), BF16)] * 3,
        compiler_params=_params(("parallel", "parallel")))(h, w4, w4)


def _mm_down_norm(name, a, w, layer, res, gain):
    t, kf = a.shape
    _, _, n = w.shape
    tm = _pick(t, (1024, 512, 256, 128))

    def body(a_ref, w_ref, r_ref, g_ref, o_ref, h_ref):
        xo = r_ref[...] + _dot(a_ref[...], w_ref[...], NN)
        o_ref[...] = xo
        h_ref[...] = ((xo * _rms(xo)) * g_ref[...]).astype(BF16)

    row = pl.BlockSpec((tm, n), lambda i: (i, 0))
    return pl.pallas_call(
        body, name=name, grid=(t // tm,),
        in_specs=[pl.BlockSpec((tm, kf), lambda i: (i, 0)),
                  _resident((None, kf, n), lambda i: (layer, 0, 0)),
                  row, pl.BlockSpec((1, n), lambda i: (0, 0))],
        out_specs=[row, row],
        out_shape=[_sds((t, n), F32), _sds((t, n), BF16)],
        compiler_params=_params(("parallel",)))(a, w, res, gain)


def _mm_down_loss(name, a, w, layer, res, tgt):
    t, kf = a.shape
    _, _, n = w.shape
    tm = _pick(t, (1024, 512, 256, 128))
    steps = t // tm

    def body(a_ref, w_ref, r_ref, t_ref, dy_ref, dyb_ref, l_ref, acc_ref):
        i = pl.program_id(0)

        @pl.when(i == 0)
        def _():
            acc_ref[...] = jnp.zeros_like(acc_ref)

        e = (r_ref[...] + _dot(a_ref[...], w_ref[...], NN)) - t_ref[...]
        dy = e * (1.0 / n)
        dy_ref[...] = dy
        dyb_ref[...] = dy.astype(BF16)
        acc_ref[...] += (e * e).reshape(tm // SUBLANES, SUBLANES, n).sum(axis=0)

        @pl.when(i == steps - 1)
        def _():
            l_ref[...] = jnp.sum(acc_ref[...], keepdims=True) * (0.5 / n)

    row = pl.BlockSpec((tm, n), lambda i: (i, 0))
    return pl.pallas_call(
        body, name=name, grid=(steps,),
        in_specs=[pl.BlockSpec((tm, kf), lambda i: (i, 0)),
                  _resident((None, kf, n), lambda i: (layer, 0, 0)), row, row],
        out_specs=[row, row, pl.BlockSpec((1, 1), lambda i: (0, 0))],
        out_shape=[_sds((t, n), F32), _sds((t, n), BF16), _sds((1, 1), F32)],
        scratch_shapes=[pltpu.VMEM((SUBLANES, n), F32)],
        compiler_params=_params(("arbitrary",)))(a, w, res, tgt)


def _mm_down_t(name, dx, w, layer):
    t, n = dx.shape
    _, kf, _ = w.shape
    tm = _pick(t, (1024, 512, 256, 128))

    def body(a_ref, w_ref, o_ref):
        o_ref[...] = _dot(a_ref[...].astype(BF16), w_ref[...], NT).astype(BF16)

    return pl.pallas_call(
        body, name=name, grid=(t // tm,),
        in_specs=[pl.BlockSpec((tm, n), lambda i: (i, 0)),
                  _resident((None, kf, n), lambda i: (layer, 0, 0))],
        out_specs=pl.BlockSpec((tm, kf), lambda i: (i, 0)),
        out_shape=_sds((t, kf), BF16),
        compiler_params=_params(("parallel",)))(dx, w)


def _mm_down_t_swiglu(name, dx, w, layer, g, u):
    t, n = dx.shape
    f = g.shape[1]
    tm = _pick(t, (512, 256, 128))

    def body(a_ref, w_ref, dag_ref, dau_ref, dg_ref, du_ref):
        da = _dot(a_ref[...].astype(BF16), w_ref[...], NT)
        dg_ref[...] = (da * dag_ref[...].astype(F32)).astype(BF16)
        du_ref[...] = (da * dau_ref[...].astype(F32)).astype(BF16)

    tile = pl.BlockSpec((tm, f), lambda i: (i, 0))
    return pl.pallas_call(
        body, name=name, grid=(t // tm,),
        in_specs=[pl.BlockSpec((tm, n), lambda i: (i, 0)),
                  _resident((None, f, n), lambda i: (layer, 0, 0)), tile, tile],
        out_specs=[tile, tile],
        out_shape=[_sds((t, f), BF16)] * 2,
        compiler_params=_params(("parallel",)))(dx, w, g, u)


def _dgrad_norm(name, acts, act_blocks, pieces, w4, layer, x, gain, dres):
    t, d = x.shape
    _, _, k, nq = w4.shape
    tm = _pick(t, (512, 256, 128))
    n_act = len(acts)

    def body(*refs):
        act_refs = refs[:n_act]
        w_ref, x_ref, g_ref, dr_ref, dx_ref, dxb_ref, dg_ref = refs[n_act:]

        @pl.when(pl.program_id(0) == 0)
        def _():
            dg_ref[...] = jnp.zeros_like(dg_ref)

        dh = None
        for a_tile, w_tile in pieces(act_refs, w_ref):
            term = _dot(a_tile, w_tile, NT)
            dh = term if dh is None else dh + term
        xv = x_ref[...]
        r = _rms(xv)
        xhat = xv * r
        gd = dh * g_ref[...]
        dx = dr_ref[...] + r * (gd - xhat * jnp.mean(gd * xhat, axis=-1, keepdims=True))
        dx_ref[...] = dx
        dxb_ref[...] = dx.astype(BF16)
        dg_ref[...] += (dh * xhat).reshape(tm // SUBLANES, SUBLANES, d).sum(axis=0)

    row = pl.BlockSpec((tm, d), lambda i: (i, 0))
    return pl.pallas_call(
        body, name=name, grid=(t // tm,),
        in_specs=[*act_blocks(tm),
                  _resident((None, N_CHIPS, k, nq), lambda i: (layer, 0, 0, 0)),
                  row, pl.BlockSpec((1, d), lambda i: (0, 0)), row],
        out_specs=[row, row, pl.BlockSpec((SUBLANES, d), lambda i: (0, 0))],
        out_shape=[_sds((t, d), F32), _sds((t, d), BF16), _sds((SUBLANES, d), F32)],
        compiler_params=_params(("arbitrary",)))(*acts, w4, x, gain, dres)


def _dgrad_norm_ffn(name, dg, du, w4, layer, x, gain, dres):
    nq = w4.shape[3]
    f = dg.shape[1]

    def blocks(tm):
        return [pl.BlockSpec((tm, f), lambda i: (i, 0))] * 2

    def pieces(act_refs, w_ref):
        dg_ref, du_ref = act_refs
        return [(dg_ref[:, 0:nq], w_ref[0]), (dg_ref[:, nq:2 * nq], w_ref[1]),
                (du_ref[:, 0:nq], w_ref[2]), (du_ref[:, nq:2 * nq], w_ref[3])]

    return _dgrad_norm(name, [dg, du], blocks, pieces, w4, layer, x, gain, dres)


def _dgrad_norm_qkv(name, dqkv, w4, x, gain, dres):
    nq = w4.shape[3]

    def blocks(tm):
        return [pl.BlockSpec((tm, N_CHIPS * nq), lambda i: (i, 0))]

    def pieces(act_refs, w_ref):
        return [(act_refs[0][:, q * nq:(q + 1) * nq], w_ref[q]) for q in range(N_CHIPS)]

    return _dgrad_norm(name, [dqkv], blocks, pieces, w4, 0, x, gain, dres)


def _dgrad_norm_conv(name, d3, w4, x, gain, dres):
    _, _, d = d3.shape
    nq = w4.shape[3]
    per_part, per_q = d // MXU_COLS, nq // MXU_COLS

    def blocks(tm):
        return [pl.BlockSpec((3, tm, d), lambda i: (0, i, 0))]

    def pieces(act_refs, w_ref):
        out = []
        for jb in range(3 * per_part):
            ca, cw = (jb % per_part) * MXU_COLS, (jb % per_q) * MXU_COLS
            out.append((act_refs[0][jb // per_part, :, ca:ca + MXU_COLS], w_ref[jb // per_q, :, cw:cw + MXU_COLS]))
        return out

    return _dgrad_norm(name, [d3], blocks, pieces, w4, 0, x, gain, dres)


def _wgrad_up2(name, h, dg, du):
    t, k = h.shape
    nq = dg.shape[1] // 2
    tk = _pick(t, (2048, 1024, 512, 256, 128))
    steps = t // tk
    half = N_CHIPS // 2

    def body(h_ref, dg_ref, du_ref, o_ref):
        q = pl.program_id(0)

        @pl.when(pl.program_id(1) == 0)
        def _():
            o_ref[...] = jnp.zeros_like(o_ref)

        @pl.when(q < half)
        def _():
            o_ref[...] += _dot(h_ref[...], dg_ref[...], TN)

        @pl.when(q >= half)
        def _():
            o_ref[...] += _dot(h_ref[...], du_ref[...], TN)

    return pl.pallas_call(
        body, name=name, grid=(N_CHIPS, steps),
        in_specs=[pl.BlockSpec((tk, k), lambda q, s: (s, 0)),
                  pl.BlockSpec((tk, nq), lambda q, s: (jnp.where(q < half, s, steps - 1), jnp.minimum(q, half - 1))),
                  pl.BlockSpec((tk, nq), lambda q, s: (jnp.where(q >= half, s, 0), jnp.maximum(q - half, 0)))],
        out_specs=pl.BlockSpec((None, k, nq), lambda q, s: (q, 0, 0)),
        out_shape=_sds((N_CHIPS, k, nq), F32),
        compiler_params=_params(("parallel", "arbitrary")))(h, dg, du)


def _wgrad_joined(name, h, dy):
    t, k = h.shape
    nq = dy.shape[1] // N_CHIPS
    tk = _pick(t, (2048, 1024, 512, 256, 128))

    def body(h_ref, dy_ref, o_ref):
        @pl.when(pl.program_id(0) == 0)
        def _():
            o_ref[...] = jnp.zeros_like(o_ref)

        res = _dot(h_ref[...], dy_ref[...], TN)
        for q in range(N_CHIPS):
            o_ref[q] += res[:, q * nq:(q + 1) * nq]

    return pl.pallas_call(
        body, name=name, grid=(t // tk,),
        in_specs=[pl.BlockSpec((tk, k), lambda s: (s, 0)), pl.BlockSpec((tk, N_CHIPS * nq), lambda s: (s, 0))],
        out_specs=pl.BlockSpec((N_CHIPS, k, nq), lambda s: (0, 0, 0)),
        out_shape=_sds((N_CHIPS, k, nq), F32),
        compiler_params=_params(("arbitrary",)))(h, dy)


def _wgrad_conv_in(name, h, d3, nq):
    t, k = h.shape
    d = d3.shape[2]
    per_part, per_q = d // MXU_COLS, nq // MXU_COLS
    tk = _pick(t, (512, 256, 128))

    def body(h_ref, d_ref, o_ref):
        @pl.when(pl.program_id(0) == 0)
        def _():
            o_ref[...] = jnp.zeros_like(o_ref)

        hv = h_ref[...]
        for part in range(3):
            res = _dot(hv, d_ref[part], TN)
            for cc in range(per_part):
                jb = part * per_part + cc
                co = (jb % per_q) * MXU_COLS
                o_ref[jb // per_q, :, co:co + MXU_COLS] += res[:, cc * MXU_COLS:(cc + 1) * MXU_COLS]

    return pl.pallas_call(
        body, name=name, grid=(t // tk,),
        in_specs=[pl.BlockSpec((tk, k), lambda s: (s, 0)), pl.BlockSpec((3, tk, d), lambda s: (0, s, 0))],
        out_specs=pl.BlockSpec((N_CHIPS, k, nq), lambda s: (0, 0, 0)),
        out_shape=_sds((N_CHIPS, k, nq), F32),
        compiler_params=_params(("arbitrary",)))(h, d3)


def _wgrad_down(name, a, dx, tmw):
    t, kf = a.shape
    n = dx.shape[1]
    tk = _pick(t, (2048, 1024, 512, 256, 128))

    def body(a_ref, b_ref, o_ref):
        @pl.when(pl.program_id(1) == 0)
        def _():
            o_ref[...] = jnp.zeros_like(o_ref)

        o_ref[...] += _dot(a_ref[...], b_ref[...].astype(BF16), TN)

    g = pl.pallas_call(
        body, name=name, grid=(kf // tmw, t // tk),
        in_specs=[pl.BlockSpec((tk, tmw), lambda j, s: (s, j)), pl.BlockSpec((tk, n), lambda j, s: (s, 0))],
        out_specs=pl.BlockSpec((tmw, n), lambda j, s: (j, 0)),
        out_shape=_sds((kf, n), F32),
        compiler_params=_params(("parallel", "arbitrary")))(a, dx)
    return g.reshape(N_CHIPS, kf // N_CHIPS, n)


def _shift_rows(u, k, rows):
    s = u.shape[0]
    if k > 0:
        r = pltpu.roll(u, k, 0)
        return jnp.concatenate([jnp.where(rows >= k, r[0:SUBLANES], 0.0), r[SUBLANES:]], axis=0)
    r = pltpu.roll(u, s + k, 0)
    return jnp.concatenate([r[:s - SUBLANES], jnp.where(rows < SUBLANES + k, r[s - SUBLANES:], 0.0)], axis=0)


def _conv_taps(cw_ref, got_ref):
    return (cw_ref[...] + got_ref[0]) + (got_ref[1] + got_ref[2])


def _conv_fwd(bcx, cw, cw_got, nseq, seq):
    t, d3 = bcx.shape
    d = d3 // 3
    cb = 2 * MXU_COLS
    nj = d // cb

    def body(b_ref, c_ref, x_ref, cw_ref, got_ref, z_ref):
        u = b_ref[...].astype(F32) * x_ref[...].astype(F32)
        rows = lax.broadcasted_iota(jnp.int32, (SUBLANES, cb), 0)
        cwv = _conv_taps(cw_ref, got_ref)
        y = cwv[2:3] * u + cwv[1:2] * _shift_rows(u, 1, rows) + cwv[0:1] * _shift_rows(u, 2, rows)
        z_ref[...] = (c_ref[...].astype(F32) * y).astype(BF16)

    return pl.pallas_call(
        body, name="conv_fwd", grid=(nseq, nj),
        in_specs=[pl.BlockSpec((seq, cb), lambda b, j: (b, j)),
                  pl.BlockSpec((seq, cb), lambda b, j: (b, nj + j)),
                  pl.BlockSpec((seq, cb), lambda b, j: (b, 2 * nj + j)),
                  pl.BlockSpec((SUBLANES, cb), lambda b, j: (0, j)),
                  pl.BlockSpec((3, SUBLANES, cb), lambda b, j: (0, 0, j))],
        out_specs=pl.BlockSpec((seq, cb), lambda b, j: (b, j)),
        out_shape=_sds((t, d), BF16),
        compiler_params=_params(("parallel", "parallel")))(bcx, bcx, bcx, cw, cw_got)


def _conv_bwd(dz, bcx, cw, cw_got, nseq, seq):
    t, d3 = bcx.shape
    d = d3 // 3
    cb = MXU_COLS
    nj = d // cb

    def body(dz_ref, b_ref, c_ref, x_ref, cw_ref, got_ref, o_ref, dcw_ref):
        @pl.when(pl.program_id(1) == 0)
        def _():
            dcw_ref[...] = jnp.zeros_like(dcw_ref)

        b = b_ref[...].astype(F32)
        c = c_ref[...].astype(F32)
        xv = x_ref[...].astype(F32)
        dzv = dz_ref[...].astype(F32)
        u = b * xv
        rows = lax.broadcasted_iota(jnp.int32, (SUBLANES, cb), 0)
        u1 = _shift_rows(u, 1, rows)
        u2 = _shift_rows(u, 2, rows)
        cwv = _conv_taps(cw_ref, got_ref)
        y = cwv[2:3] * u + cwv[1:2] * u1 + cwv[0:1] * u2
        dyc = dzv * c
        du = cwv[2:3] * dyc + cwv[1:2] * _shift_rows(dyc, -1, rows) + cwv[0:1] * _shift_rows(dyc, -2, rows)
        o_ref[0] = (du * xv).astype(BF16)
        o_ref[1] = (dzv * y).astype(BF16)
        o_ref[2] = (du * b).astype(BF16)
        s0 = jnp.sum(dyc * u2, axis=0, keepdims=True)
        s1 = jnp.sum(dyc * u1, axis=0, keepdims=True)
        s2 = jnp.sum(dyc * u, axis=0, keepdims=True)
        tap = lax.broadcasted_iota(jnp.int32, (3, cb), 0)
        dcw_ref[...] += jnp.where(tap == 0, s0, jnp.where(tap == 1, s1, s2))

    return pl.pallas_call(
        body, name="conv_bwd", grid=(nj, nseq),
        in_specs=[pl.BlockSpec((seq, cb), lambda j, b: (b, j)),
                  pl.BlockSpec((seq, cb), lambda j, b: (b, j)),
                  pl.BlockSpec((seq, cb), lambda j, b: (b, nj + j)),
                  pl.BlockSpec((seq, cb), lambda j, b: (b, 2 * nj + j)),
                  pl.BlockSpec((SUBLANES, cb), lambda j, b: (0, j)),
                  pl.BlockSpec((3, SUBLANES, cb), lambda j, b: (0, 0, j))],
        out_specs=[pl.BlockSpec((3, seq, cb), lambda j, b: (0, b, j)),
                   pl.BlockSpec((3, cb), lambda j, b: (0, j))],
        out_shape=[_sds((3, t, d), BF16), _sds((3, d), F32)],
        compiler_params=_params(("parallel", "arbitrary")))(dz, bcx, bcx, bcx, cw, cw_got)


def _pair_norm(x, gain_pair, low):
    sq = x * x
    ss_lo = jnp.sum(jnp.where(low, sq, 0.0), axis=-1, keepdims=True)
    ss_hi = jnp.sum(jnp.where(low, 0.0, sq), axis=-1, keepdims=True)
    r = lax.rsqrt(jnp.where(low, ss_lo, ss_hi) * (1.0 / HEAD_DIM) + EPS)
    xhat = x * r
    return xhat * gain_pair, xhat, r


KEYS = 2 * BLOCK
QK_SCALE = 1.0 / (HEAD_DIM ** 0.5)
N_PAIRS = N_Q_HEADS // 2


def _earlier_block(shape=(BLOCK, BLOCK)):
    return lax.broadcasted_iota(jnp.int32, shape, 0) > lax.broadcasted_iota(jnp.int32, shape, 1)


def _fill_bias(bias_ref):
    rows = lax.broadcasted_iota(jnp.int32, (2 * BLOCK, BLOCK), 0)
    qi = lax.broadcasted_iota(jnp.int32, (2 * BLOCK, BLOCK), 1)
    odd_head = rows >= BLOCK
    kj = jnp.where(odd_head, rows - BLOCK, rows)
    earlier = kj > qi
    dist = (jnp.where(earlier, BLOCK, 0) + qi - kj).astype(F32)
    for j in range(N_PAIRS):
        slope = jnp.where(odd_head, ALIBI_SLOPES[2 * j + 1], ALIBI_SLOPES[2 * j])
        bias = -slope * dist
        bias_ref[1, j] = bias
        bias_ref[0, j] = jnp.where(earlier, -1e30, bias)


def _merge_blocks(x_t, earlier):
    return jnp.concatenate([jnp.where(earlier, x_t[e * KEYS:e * KEYS + BLOCK], x_t[e * KEYS + BLOCK:(e + 1) * KEYS])
                            for e in range(2)], axis=0)


def _split_blocks(heads, earlier):
    parts = []
    for x in heads:
        parts += [jnp.where(earlier, x, 0.0), jnp.where(earlier, 0.0, x)]
    return jnp.concatenate(parts, axis=0).astype(BF16)


def _kv_pair_rows(kv_tile, parity, low):
    own = jnp.where(low if parity == 0 else jnp.logical_not(low), kv_tile, 0.0)
    other = pltpu.roll(own, HEAD_DIM, 1)
    lo, hi = (own, other) if parity == 0 else (other, own)
    return jnp.concatenate([lo, hi], axis=0).astype(BF16)


def _pair_softmax(s_t, sink_even, sink_odd):
    out = []
    for e, sink in enumerate((sink_even, sink_odd)):
        se = s_t[e * BLOCK:(e + 1) * BLOCK]
        m = jnp.maximum(jnp.max(se, axis=0, keepdims=True), sink)
        ee = jnp.exp(se - m)
        es = jnp.exp(sink - m)
        inv = 1.0 / (jnp.sum(ee, axis=0, keepdims=True) + es)
        out.append((ee * inv, es * inv))
    return out


def _attn_rows(n):
    q0 = pl.multiple_of(n * BLOCK, BLOCK)
    k0 = pl.multiple_of(jnp.maximum(n - 1, 0) * BLOCK, BLOCK)
    return q0, k0, jnp.minimum(n, 1)


def _key_rows(qkv_ref, k0, q0, col):
    return jnp.concatenate([qkv_ref[pl.ds(k0, BLOCK), col:col + LANES], qkv_ref[pl.ds(q0, BLOCK), col:col + LANES]],
                           axis=0).astype(F32)


def _attn_fwd(qkv, qg_pair, kg_pair, sinks, nseq, seq):
    t = qkv.shape[0]
    dq = N_Q_HEADS * HEAD_DIM
    dkv = N_KV_HEADS * HEAD_DIM

    def body(sk_ref, qkv_ref, qg_ref, kg_ref, o_ref, bias_ref):
        @pl.when(pl.program_id(0) == 0)
        def _():
            _fill_bias(bias_ref)

        low = lax.broadcasted_iota(jnp.int32, (1, LANES), 1) < HEAD_DIM
        earlier = _earlier_block()
        qg = qg_ref[...] * QK_SCALE
        kg = kg_ref[...]

        def blk(n, carry):
            q0, k0, later = _attn_rows(n)
            for kt in range(dkv // LANES):
                kraw = _key_rows(qkv_ref, k0, q0, dq + kt * LANES)
                vraw = _key_rows(qkv_ref, k0, q0, dq + dkv + kt * LANES)
                kn, _, _ = _pair_norm(kraw, kg, low)
                for par in range(2):
                    kh = 2 * kt + par
                    k_pair = _kv_pair_rows(kn, par, low)
                    v_pair = _kv_pair_rows(vraw, par, low)
                    for jj in range(2):
                        j = 2 * kh + jj
                        qraw = qkv_ref[pl.ds(q0, BLOCK), j * LANES:(j + 1) * LANES].astype(F32)
                        qn, _, _ = _pair_norm(qraw, qg, low)
                        s_t = _merge_blocks(_dot(k_pair, qn.astype(BF16), NT), earlier) + bias_ref[later, j]
                        (p0, _), (p1, _) = _pair_softmax(s_t, sk_ref[0, 2 * j], sk_ref[0, 2 * j + 1])
                        p_t = _split_blocks((p0, p1), earlier)
                        o_ref[pl.ds(q0, BLOCK), j * LANES:(j + 1) * LANES] = _dot(p_t, v_pair, TN).astype(BF16)
            return carry

        lax.fori_loop(0, seq // BLOCK, blk, 0)

    return pl.pallas_call(
        body, name="attn_fwd", grid=(nseq,),
        in_specs=[pl.BlockSpec(memory_space=pltpu.SMEM),
                  pl.BlockSpec((seq, dq + 2 * dkv), lambda b: (b, 0)),
                  pl.BlockSpec((1, LANES), lambda b: (0, 0)),
                  pl.BlockSpec((1, LANES), lambda b: (0, 0))],
        out_specs=pl.BlockSpec((seq, dq), lambda b: (b, 0)),
        out_shape=_sds((t, dq), BF16),
        scratch_shapes=[pltpu.VMEM((2, N_PAIRS, 2 * BLOCK, BLOCK), F32)],
        compiler_params=_params(("arbitrary",)))(sinks, qkv, qg_pair, kg_pair)


def _attn_bwd(do, qkv, qg_pair, kg_pair, sinks, nseq, seq):
    t = qkv.shape[0]
    dq = N_Q_HEADS * HEAD_DIM
    dkv = N_KV_HEADS * HEAD_DIM

    def body(sk_ref, do_ref, qkv_ref, qg_ref, kg_ref, o_ref, dqg_ref, dkg_ref, dsk_ref, acc_ref, bias_ref):
        @pl.when(pl.program_id(0) == 0)
        def _():
            _fill_bias(bias_ref)
            dqg_ref[...] = jnp.zeros_like(dqg_ref)
            dkg_ref[...] = jnp.zeros_like(dkg_ref)
            dsk_ref[...] = jnp.zeros_like(dsk_ref)

        acc_ref[...] = jnp.zeros_like(acc_ref)
        low = lax.broadcasted_iota(jnp.int32, (1, LANES), 1) < HEAD_DIM
        earlier = _earlier_block()
        head_row = lax.broadcasted_iota(jnp.int32, (N_Q_HEADS, LANES), 0)
        qg = qg_ref[...] * QK_SCALE
        kg = kg_ref[...]

        def blk(n, carry):
            dqg_acc, dkg_acc, dsk_acc = carry
            q0, k0, later = _attn_rows(n)
            for kt in range(dkv // LANES):
                kraw = _key_rows(qkv_ref, k0, q0, dq + kt * LANES)
                vraw = _key_rows(qkv_ref, k0, q0, dq + dkv + kt * LANES)
                kn, khat, rk = _pair_norm(kraw, kg, low)
                dk_tile = None
                dv_tile = None
                for par in range(2):
                    kh = 2 * kt + par
                    own = low if par == 0 else jnp.logical_not(low)
                    k_pair = _kv_pair_rows(kn, par, low)
                    v_pair = _kv_pair_rows(vraw, par, low)
                    dkn_rows = jnp.zeros((2 * KEYS, LANES), F32)
                    dv_rows = jnp.zeros((2 * KEYS, LANES), F32)
                    for jj in range(2):
                        j = 2 * kh + jj
                        qraw = qkv_ref[pl.ds(q0, BLOCK), j * LANES:(j + 1) * LANES].astype(F32)
                        qn, qhat, rq = _pair_norm(qraw, qg, low)
                        qn_b = qn.astype(BF16)
                        do_b = do_ref[pl.ds(q0, BLOCK), j * LANES:(j + 1) * LANES]
                        s_t = _merge_blocks(_dot(k_pair, qn_b, NT), earlier) + bias_ref[later, j]
                        dp_t = _merge_blocks(_dot(v_pair, do_b, NT), earlier)
                        ds_heads = []
                        probs = _pair_softmax(s_t, sk_ref[0, 2 * j], sk_ref[0, 2 * j + 1])
                        for e, (p, ps) in enumerate(probs):
                            dp = dp_t[e * BLOCK:(e + 1) * BLOCK]
                            dsum = jnp.sum(p * dp, axis=0, keepdims=True)
                            ds_heads.append(p * (dp - dsum))
                            dsk_acc = dsk_acc - jnp.where(head_row == 2 * j + e, ps * dsum, 0.0)
                        p_t = _split_blocks((probs[0][0], probs[1][0]), earlier)
                        ds_t = _split_blocks(ds_heads, earlier)
                        dv_rows = dv_rows + _dot(p_t, do_b, NN)
                        dkn_rows = dkn_rows + _dot(ds_t, qn_b, NN)
                        dqn = _dot(ds_t, k_pair, TN)
                        dqg_acc = dqg_acc + jnp.sum(dqn * qhat, axis=0, keepdims=True)
                        dqhat = dqn * qg
                        prod = dqhat * qhat
                        m_lo = jnp.sum(jnp.where(low, prod, 0.0), axis=-1, keepdims=True)
                        m_hi = jnp.sum(jnp.where(low, 0.0, prod), axis=-1, keepdims=True)
                        mean = jnp.where(low, m_lo, m_hi) * (1.0 / HEAD_DIM)
                        o_ref[pl.ds(q0, BLOCK), j * LANES:(j + 1) * LANES] = (rq * (dqhat - qhat * mean)).astype(BF16)
                    dkn_acc = jnp.where(low, dkn_rows[0:KEYS], dkn_rows[KEYS:2 * KEYS])
                    dv_acc = jnp.where(low, dv_rows[0:KEYS], dv_rows[KEYS:2 * KEYS])
                    dkn = dkn_acc + pltpu.roll(dkn_acc, HEAD_DIM, 1)
                    dvh = dv_acc + pltpu.roll(dv_acc, HEAD_DIM, 1)
                    khat_own = jnp.where(own, khat, 0.0)
                    khat_dup = khat_own + pltpu.roll(khat_own, HEAD_DIM, 1)
                    dkg_acc = dkg_acc + jnp.sum(jnp.where(own, dkn * khat_dup, 0.0), axis=0, keepdims=True)
                    dkhat = dkn * kg
                    mean_k = jnp.sum(dkhat * khat_dup, axis=-1, keepdims=True) * (1.0 / LANES)
                    dk_raw = rk * (dkhat - khat_dup * mean_k)
                    dk_tile = jnp.where(own, dk_raw, 0.0) if dk_tile is None else jnp.where(own, dk_raw, dk_tile)
                    dv_tile = jnp.where(own, dvh, 0.0) if dv_tile is None else jnp.where(own, dvh, dv_tile)
                for r0, part in ((k0, slice(0, BLOCK)), (q0, slice(BLOCK, KEYS))):
                    acc_ref[pl.ds(r0, BLOCK), kt * LANES:(kt + 1) * LANES] += dk_tile[part]
                    acc_ref[pl.ds(r0, BLOCK), dkv + kt * LANES:dkv + (kt + 1) * LANES] += dv_tile[part]
            return dqg_acc, dkg_acc, dsk_acc

        zero = jnp.zeros((1, LANES), F32)
        carry = (zero, zero, jnp.zeros((N_Q_HEADS, LANES), F32))
        dqg_acc, dkg_acc, dsk_acc = lax.fori_loop(0, seq // BLOCK, blk, carry)
        dqg_ref[...] += dqg_acc * QK_SCALE
        dkg_ref[...] += dkg_acc
        dsk_ref[...] += dsk_acc
        o_ref[:, dq:dq + 2 * dkv] = acc_ref[...].astype(BF16)

    small = pl.BlockSpec((1, LANES), lambda b: (0, 0))
    heads = pl.BlockSpec((N_Q_HEADS, LANES), lambda b: (0, 0))
    return pl.pallas_call(
        body, name="attn_bwd", grid=(nseq,),
        in_specs=[pl.BlockSpec(memory_space=pltpu.SMEM),
                  pl.BlockSpec((seq, dq), lambda b: (b, 0)),
                  pl.BlockSpec((seq, dq + 2 * dkv), lambda b: (b, 0)),
                  small, small],
        out_specs=[pl.BlockSpec((seq, dq + 2 * dkv), lambda b: (b, 0)), small, small, heads],
        out_shape=[_sds((t, dq + 2 * dkv), BF16), _sds((1, LANES), F32), _sds((1, LANES), F32),
                   _sds((N_Q_HEADS, LANES), F32)],
        scratch_shapes=[pltpu.VMEM((seq, 2 * dkv), F32), pltpu.VMEM((2, N_PAIRS, 2 * BLOCK, BLOCK), F32)],
        compiler_params=_params(("arbitrary",)))(sinks, do, qkv, qg_pair, kg_pair)


def _place():
    x, y, c = lax.axis_index("x"), lax.axis_index("y"), lax.axis_index("c")
    other_chips = [(1 - x, y), (x, 1 - y), (1 - x, 1 - y)]
    return x, y, c, other_chips


def _half_rows(c, rows):
    rh = rows // 2
    return pl.ds(pl.multiple_of(c * rh, BF16_ROWS), rh)


def _cast_own(name, w, place, layer=None):
    nl, r, cdim = w.shape
    first = 0
    if layer is not None:
        nl, first = 1, layer
    rt = _row_tile(r, 4 * cdim, 2 * ELEMENTWISE_BLOCK)

    def body(s_ref, w_ref, o_ref):
        o_ref[...] = w_ref[...].astype(BF16)

    grid_spec = pltpu.PrefetchScalarGridSpec(
        num_scalar_prefetch=1, grid=(nl, r // rt),
        in_specs=[pl.BlockSpec((None, rt, cdim), lambda l, i, s: (first + l, i, 0))],
        out_specs=pl.BlockSpec((None, None, rt, cdim), lambda l, i, s: (l, s[1], i, 0)))
    return pl.pallas_call(
        body, name=name, grid_spec=grid_spec, out_shape=_sds((nl, N_CHIPS, r, cdim), BF16),
        compiler_params=_params(("parallel", "parallel")))(place, w)


def _gather_protocol(outs, shapes, send_sems, recv_sems):
    n = len(outs)
    x, y, c, other_chips = _place()
    me_chip = 2 * x + y
    sibling = (x, y, 1 - c)

    def rows(u, chip, half):
        return outs[u].at[:, chip, _half_rows(half, shapes[u][2]), :]

    def copy(sem, part, to):
        return pltpu.make_async_remote_copy(src_ref=part, dst_ref=part, send_sem=send_sems.at[sem],
                                            recv_sem=recv_sems.at[sem], device_id=to, device_id_type=MESH)

    sends = []
    for u in range(n):
        for k, chip in enumerate(other_chips):
            cp = copy(6 * u + k, rows(u, me_chip, c), (*chip, c))
            cp.start()
            sends.append(cp)
    for u in range(n):
        for k, chip in enumerate(other_chips):
            got = rows(u, 2 * chip[0] + chip[1], c)
            copy(6 * u + k, got, (*chip, c)).wait_recv()
            cp = copy(6 * u + 3 + k, got, sibling)
            cp.start()
            sends.append(cp)
    for u in range(n):
        for k, chip in enumerate(other_chips):
            copy(6 * u + 3 + k, rows(u, 2 * chip[0] + chip[1], 1 - c), sibling).wait_recv()
    for cp in sends:
        cp.wait_send()


def _hbm_ref(a):
    return jax.new_ref(a, memory_space=pltpu.MemorySpace.HBM)


def _sibling_peer():
    x, y, c, _ = _place()
    return [(x, y, 1 - c)]


def _chip_peers():
    x, y, c, other_chips = _place()
    return [(*chip, c) for chip in other_chips]


def _gather_peers():
    return _chip_peers() + _sibling_peer()


def _on_sequencer(name, collective_id, n_sems, peers, protocol, operands=(), out_types=()):
    n_in, n_out = len(operands), len(out_types)

    def launch(*refs):
        send_sems, recv_sems = refs[n_in + n_out:]
        barrier = pltpu.get_barrier_semaphore()
        targets = peers()
        for peer in targets:
            pl.semaphore_signal(barrier, inc=1, device_id=peer, device_id_type=MESH)
        pl.semaphore_wait(barrier, len(targets))
        protocol(refs[:n_in], refs[n_in:n_in + n_out], send_sems, recv_sems)

    return pl.kernel(
        launch, out_type=tuple(out_types), mesh=plsc.ScalarSubcoreMesh(axis_name="sequencer", num_cores=1), name=name,
        scratch_types=(pltpu.SemaphoreType.DMA((n_sems,)), pltpu.SemaphoreType.DMA((n_sems,))),
        compiler_params=pltpu.CompilerParams(collective_id=collective_id))(*operands)


def _seq_allgather(name, collective_id, bufs):
    shapes = [b.shape for b in bufs]
    refs = [_hbm_ref(b) for b in bufs]
    _on_sequencer(name, collective_id, 6 * len(bufs), _gather_peers,
                  lambda ins, outs, send_sems, recv_sems: _gather_protocol(refs, shapes, send_sems, recv_sems))
    return [r[...] for r in refs]


def _taps_protocol(block_ref, got_ref, send_sems, recv_sems, first_sem):
    x, y, c, other_chips = _place()
    copies = []
    for k, chip in enumerate(other_chips):
        cp = pltpu.make_async_remote_copy(src_ref=block_ref, dst_ref=got_ref.at[k], send_sem=send_sems.at[first_sem + k],
                                          recv_sem=recv_sems.at[first_sem + k], device_id=(*chip, c), device_id_type=MESH)
        cp.start()
        copies.append(cp)
    return copies


def _seq_allgather_conv(collective_id, bufs, cw_block):
    shapes = [b.shape for b in bufs]
    refs = [_hbm_ref(b) for b in bufs]

    def protocol(ins, outs, send_sems, recv_sems):
        taps = _taps_protocol(ins[0], outs[0], send_sems, recv_sems, 6 * len(bufs))
        _gather_protocol(refs, shapes, send_sems, recv_sems)
        for cp in taps:
            cp.wait_recv()
        for cp in taps:
            cp.wait_send()

    (got,) = _on_sequencer("allgather_conv", collective_id, 6 * len(bufs) + 3, _gather_peers, protocol,
                           operands=(cw_block,), out_types=(_sds((3, *cw_block.shape), F32),))
    return [r[...] for r in refs], got


def _exchange_protocol(gs, outs, shapes, send_sems, recv_sems):
    x, y, c, _ = _place()
    sends = []
    for u in range(len(gs)):
        cp = pltpu.make_async_remote_copy(
            src_ref=gs[u].at[:, _half_rows(1 - c, shapes[u][1]), :], dst_ref=outs[u],
            send_sem=send_sems.at[u], recv_sem=recv_sems.at[u], device_id=(x, y, 1 - c), device_id_type=MESH)
        cp.start()
        sends.append(cp)
    for cp in sends:
        cp.wait_recv()
    for cp in sends:
        cp.wait_send()


def _seq_exchange(name, collective_id, grads):
    shapes = [g.shape for g in grads]
    return _on_sequencer(
        name, collective_id, len(grads), _sibling_peer,
        lambda gs, outs, send_sems, recv_sems: _exchange_protocol(gs, outs, shapes, send_sems, recv_sems),
        operands=grads, out_types=[_sds((s[0], s[1] // 2, s[2]), F32) for s in shapes])


def _sum_halves(name, g, got, place, after):
    _, r, cdim = g.shape
    rh = r // 2
    rt = _row_tile(rh, 4 * N_CHIPS * cdim, 4 * ELEMENTWISE_BLOCK)
    nr = rh // rt

    def body(s_ref, g_ref, got_ref, after_ref, pb_ref, pf_ref):
        pb_ref[...] = (g_ref[...] + got_ref[...]).astype(BF16)
        mine = s_ref[1]
        pf_ref[...] = g_ref[mine] + got_ref[mine]

    quarters = (N_CHIPS, rt, cdim)
    grid_spec = pltpu.PrefetchScalarGridSpec(
        num_scalar_prefetch=1, grid=(nr,),
        in_specs=[pl.BlockSpec(quarters, lambda i, s: (0, s[0] * nr + i, 0)),
                  pl.BlockSpec(quarters, lambda i, s: (0, i, 0)),
                  pl.BlockSpec(memory_space=pl.ANY)],
        out_specs=[pl.BlockSpec(quarters, lambda i, s: (0, i, 0)),
                   pl.BlockSpec((rt, cdim), lambda i, s: (i, 0))])
    return pl.pallas_call(
        body, name=name, grid_spec=grid_spec,
        out_shape=[_sds((N_CHIPS, rh, cdim), BF16), _sds((rh, cdim), F32)],
        compiler_params=_params(("parallel",)))(place, g, got, after)


def _scatter_protocol(ps, outs, send_sems, recv_sems):
    x, y, c, other_chips = _place()
    sends = []
    for u in range(len(ps)):
        for k, chip in enumerate(other_chips):
            cp = pltpu.make_async_remote_copy(
                src_ref=ps[u].at[2 * chip[0] + chip[1]], dst_ref=outs[u].at[k],
                send_sem=send_sems.at[3 * u + k], recv_sem=recv_sems.at[3 * u + k],
                device_id=(*chip, c), device_id_type=MESH)
            cp.start()
            sends.append(cp)
    for cp in sends:
        cp.wait_recv()
    for cp in sends:
        cp.wait_send()


def _seq_scatter(name, collective_id, partials):
    return _on_sequencer(
        name, collective_id, 3 * len(partials), _chip_peers, _scatter_protocol,
        operands=partials, out_types=[_sds((3, p.shape[1], p.shape[2]), BF16) for p in partials])


def _sum_partials(name, own, got, place, layer, nl, prev, after):
    rh, cdim = own.shape
    rt = _row_tile(rh, 4 * cdim, 2 * ELEMENTWISE_BLOCK)
    nr = rh // rt
    after = list(after) if isinstance(after, (list, tuple)) else [after]

    def body(s_ref, own_ref, got_ref, *rest):
        o_ref = rest[-1]
        o_ref[...] = ((own_ref[...] + got_ref[0].astype(F32)) + got_ref[1].astype(F32)) + got_ref[2].astype(F32)

    in_specs = [pl.BlockSpec((rt, cdim), lambda i, s: (i, 0)), pl.BlockSpec((3, rt, cdim), lambda i, s: (0, i, 0)),
                *[pl.BlockSpec(memory_space=pl.ANY)] * len(after)]
    args = [place, own, got, *after]
    aliases = {}
    if prev is not None:
        in_specs.append(pl.BlockSpec(memory_space=pl.ANY))
        aliases = {len(args): 0}
        args.append(prev)
    grid_spec = pltpu.PrefetchScalarGridSpec(
        num_scalar_prefetch=1, grid=(nr,), in_specs=in_specs,
        out_specs=pl.BlockSpec((None, rt, cdim), lambda i, s: (layer, s[0] * nr + i, 0)))
    return pl.pallas_call(
        body, name=name, grid_spec=grid_spec, out_shape=_sds((nl, 2 * rh, cdim), F32),
        input_output_aliases=aliases, compiler_params=_params(("parallel",)))(*args)


def _share_protocol(outs, shapes, units, send_sems, recv_sems):
    x, y, c, _ = _place()
    sends = []
    for u, (w, l) in enumerate(units):
        mine = outs[w].at[l, _half_rows(c, shapes[w][1]), :]
        cp = pltpu.make_async_remote_copy(src_ref=mine, dst_ref=mine, send_sem=send_sems.at[u],
                                          recv_sem=recv_sems.at[u], device_id=(x, y, 1 - c), device_id_type=MESH)
        cp.start()
        sends.append(cp)
    for u, (w, l) in enumerate(units):
        theirs = outs[w].at[l, _half_rows(1 - c, shapes[w][1]), :]
        pltpu.make_async_remote_copy(src_ref=theirs, dst_ref=theirs, send_sem=send_sems.at[u],
                                     recv_sem=recv_sems.at[u], device_id=(x, y, 1 - c),
                                     device_id_type=MESH).wait_recv()
    for cp in sends:
        cp.wait_send()


def _seq_share(name, collective_id, bufs):
    shapes = [b.shape for b in bufs]
    units = [(w, l) for w in range(len(bufs)) for l in range(shapes[w][0])]
    refs = [_hbm_ref(b) for b in bufs]
    _on_sequencer(name, collective_id, len(units), _sibling_peer,
                  lambda ins, outs, send_sems, recv_sems: _share_protocol(refs, shapes, units, send_sems, recv_sems))
    return [r[...] for r in refs]


def _gather_blocks(block_ref, all_ref, send_sems, recv_sems):
    x, y, c, _ = _place()
    me = 4 * x + 2 * y + c
    all_ref[me] = block_ref[...]
    sends = []
    for rel in range(1, 8):
        fx, fy, fc = (rel >> 2) & 1, (rel >> 1) & 1, rel & 1
        peer = (x ^ fx, y ^ fy, c ^ fc)
        cp = pltpu.make_async_remote_copy(src_ref=block_ref, dst_ref=all_ref.at[me], send_sem=send_sems.at[rel - 1],
                                          recv_sem=recv_sems.at[rel - 1], device_id=peer, device_id_type=MESH)
        cp.start()
        sends.append(cp)
    for cp in sends:
        cp.wait_recv()
    for cp in sends:
        cp.wait_send()


def _adam(w, g, m, v):
    m_new = ADAM_B1 * m + (1.0 - ADAM_B1) * g
    v_new = ADAM_B2 * v + (1.0 - ADAM_B2) * (g * g)
    m_hat = m_new / (1.0 - ADAM_B1 ** ADAM_STEP)
    v_hat = v_new / (1.0 - ADAM_B2 ** ADAM_STEP)
    delta = -ADAM_LR * (m_hat / (jnp.sqrt(v_hat) + ADAM_EPS) + ADAM_WD * w)
    return delta, m_new, v_new


def _small_step(dnm0, dnm1, dnf0, dnf1, dcw, dqg, dkg, dsk, loss, w_blk, m_blk, v_blk, cw_cols):
    d = w_blk.shape[1]
    vm = pl.BlockSpec(memory_space=pltpu.VMEM)

    def reduce_body(dnm0_ref, dnm1_ref, dnf0_ref, dnf1_ref, dcw_ref, dqg_ref, dkg_ref, dsk_ref, loss_ref,
                    g_ref, blk_ref, all_ref, send_sems, recv_sems):
        blk_ref[...] = jnp.zeros_like(blk_ref)
        for row, part_ref in ((SENT_NORM_MIXER, dnm0_ref), (SENT_NORM_MIXER + 1, dnm1_ref),
                              (SENT_NORM_FFN, dnf0_ref), (SENT_NORM_FFN + 1, dnf1_ref)):
            blk_ref[row:row + 1, :] = jnp.sum(part_ref[...], axis=0, keepdims=True)
        blk_ref[SENT_CONV_W:SENT_CONV_W + 3, :] = dcw_ref[...]
        misc = slice(SENT_MISC, SENT_MISC + 1)
        for tile, gain_ref in ((TILE_Q_GAIN, dqg_ref), (TILE_K_GAIN, dkg_ref)):
            pair = gain_ref[...]
            blk_ref[misc, tile * LANES:(tile + 1) * LANES] = pair + pltpu.roll(pair, HEAD_DIM, 1)
        for h in range(N_Q_HEADS):
            lane = TILE_SINKS * LANES + h
            blk_ref[misc, lane:lane + 1] = jnp.sum(dsk_ref[h:h + 1, :], axis=1, keepdims=True)
        blk_ref[misc, TILE_LOSS * LANES:(TILE_LOSS + 1) * LANES] = jnp.broadcast_to(loss_ref[...], (1, LANES))
        _gather_blocks(blk_ref, all_ref, send_sems, recv_sems)
        g = all_ref[0]
        for dev in range(1, 8):
            g = g + all_ref[dev]
        g_ref[...] = jnp.zeros_like(g_ref)
        for sent, row, n in ((SENT_NORM_MIXER, ROW_NORM_MIXER, 2), (SENT_NORM_FFN, ROW_NORM_FFN, 2),
                             (SENT_CONV_W, ROW_CONV_W, 3), (SENT_MISC, ROW_MISC, 1)):
            g_ref[row:row + n, :] = g[sent:sent + n]

    g_blk = pl.pallas_call(
        reduce_body, name="small_allreduce", in_specs=[vm] * 9, out_specs=vm, out_shape=_sds((SMALL_ROWS, d), F32),
        scratch_shapes=[pltpu.VMEM((SUBLANES, d), F32), pltpu.VMEM((8, SUBLANES, d), F32),
                        pltpu.SemaphoreType.DMA((7,)), pltpu.SemaphoreType.DMA((7,))],
    )(dnm0, dnm1, dnf0, dnf1, dcw, dqg, dkg, dsk, loss)

    def body(g_ref, w_ref, m_ref, v_ref, *out_refs):
        g = g_ref[...]
        misc = slice(ROW_MISC, ROW_MISC + 1)
        out_refs[0][...] = g[misc, TILE_LOSS * LANES:TILE_LOSS * LANES + 1]
        chip = 2 * lax.axis_index("x") + lax.axis_index("y")
        for i, blk in enumerate((g, *_adam(w_ref[...], g, m_ref[...], v_ref[...]))):
            nm_ref, nf_ref, cw_ref, qg_ref, kg_ref, sk_ref = out_refs[1 + 6 * i:7 + 6 * i]
            nm_ref[...] = blk[ROW_NORM_MIXER:ROW_NORM_MIXER + 2]
            nf_ref[...] = blk[ROW_NORM_FFN:ROW_NORM_FFN + 2]
            qg_ref[...] = blk[misc, TILE_Q_GAIN * LANES:TILE_Q_GAIN * LANES + HEAD_DIM]
            kg_ref[...] = blk[misc, TILE_K_GAIN * LANES:TILE_K_GAIN * LANES + HEAD_DIM]
            sk_ref[...] = blk[misc, TILE_SINKS * LANES:TILE_SINKS * LANES + N_Q_HEADS]
            for q in range(N_CHIPS):
                @pl.when(chip == q)
                def _(blk=blk, cw_ref=cw_ref, q=q):
                    cw_ref[0] = blk[ROW_CONV_W:ROW_CONV_W + 3, q * cw_cols:(q + 1) * cw_cols]

    group = [_sds((2, d), F32), _sds((2, d), F32), _sds((1, 3, cw_cols), F32), _sds((1, HEAD_DIM), F32),
             _sds((1, HEAD_DIM), F32), _sds((1, N_Q_HEADS), F32)]
    outs = pl.pallas_call(
        body, name="small_adam", in_specs=[vm] * 4, out_specs=[vm] * 25, out_shape=[_sds((1, 1), F32)] + group * 4,
    )(g_blk, w_blk, m_blk, v_blk)
    names = ("norm_mixer", "norm_ffn", "conv_w", "attn_q_gain", "attn_k_gain", "attn_sinks")
    return outs[0], [dict(zip(names, outs[1 + 6 * i:7 + 6 * i])) for i in range(4)]


def _adam_step(name, w, g, m, v):
    nl, r, cdim = w.shape
    rt = _row_tile(r, 4 * cdim, ELEMENTWISE_BLOCK)

    def body(w_ref, g_ref, m_ref, v_ref, go_ref, d_ref, mo_ref, vo_ref):
        gv = g_ref[...]
        go_ref[...] = gv
        delta, m_new, v_new = _adam(w_ref[...], gv, m_ref[...], v_ref[...])
        d_ref[...] = delta
        mo_ref[...] = m_new
        vo_ref[...] = v_new

    spec = pl.BlockSpec((None, rt, cdim), lambda l, i: (l, i, 0))
    return pl.pallas_call(
        body, name=name, grid=(nl, r // rt), in_specs=[spec] * 4, out_specs=[spec] * 4,
        out_shape=[_sds(w.shape, F32)] * 4,
        compiler_params=_params(("parallel", "parallel")))(w, g, m, v)


def _pad_rows(a, rows=SUBLANES):
    return jnp.pad(a, ((0, rows - a.shape[0]), (0, 0)))


def _small_block(nm, nf, cw_local, qg, kg, sk, chip):
    d = nm.shape[1]
    cw_rows = lax.dynamic_update_slice(jnp.zeros((SUBLANES, d), F32), cw_local, (0, chip * cw_local.shape[1]))
    misc = jnp.concatenate([qg, qg, kg, kg, jnp.pad(sk, ((0, 0), (0, LANES - sk.shape[1]))),
                            jnp.zeros((1, d - 3 * LANES), F32)], axis=1)
    return jnp.concatenate([_pad_rows(nm), _pad_rows(nf), cw_rows, _pad_rows(misc)], axis=0)


WEIGHT_NAMES = ("conv_w_in", "conv_w", "conv_w_out", "attn_w_qkv", "attn_q_gain", "attn_k_gain", "attn_sinks",
                "attn_w_o", "norm_mixer", "norm_ffn", "ffn_w_gate_up", "ffn_w_down")
BIG = ("conv_w_in", "conv_w_out", "attn_w_qkv", "attn_w_o", "ffn_w_gate_up", "ffn_w_down")


def kernel(x, conv_w_in, conv_w, conv_w_out, attn_w_qkv, attn_q_gain, attn_k_gain, attn_sinks, attn_w_o, norm_mixer, norm_ffn, ffn_w_gate_up, ffn_w_down, loss_target, m_conv_w_in, m_conv_w, m_conv_w_out, m_attn_w_qkv, m_attn_q_gain, m_attn_k_gain, m_attn_sinks, m_attn_w_o, m_norm_mixer, m_norm_ffn, m_ffn_w_gate_up, m_ffn_w_down, v_conv_w_in, v_conv_w, v_conv_w_out, v_attn_w_qkv, v_attn_q_gain, v_attn_k_gain, v_attn_sinks, v_attn_w_o, v_norm_mixer, v_norm_ffn, v_ffn_w_gate_up, v_ffn_w_down):
    w = dict(conv_w_in=conv_w_in, conv_w=conv_w, conv_w_out=conv_w_out, attn_w_qkv=attn_w_qkv,
             attn_q_gain=attn_q_gain, attn_k_gain=attn_k_gain, attn_sinks=attn_sinks, attn_w_o=attn_w_o,
             norm_mixer=norm_mixer, norm_ffn=norm_ffn, ffn_w_gate_up=ffn_w_gate_up, ffn_w_down=ffn_w_down)
    m = dict(conv_w_in=m_conv_w_in, conv_w=m_conv_w, conv_w_out=m_conv_w_out, attn_w_qkv=m_attn_w_qkv,
             attn_q_gain=m_attn_q_gain, attn_k_gain=m_attn_k_gain, attn_sinks=m_attn_sinks, attn_w_o=m_attn_w_o,
             norm_mixer=m_norm_mixer, norm_ffn=m_norm_ffn, ffn_w_gate_up=m_ffn_w_gate_up, ffn_w_down=m_ffn_w_down)
    v = dict(conv_w_in=v_conv_w_in, conv_w=v_conv_w, conv_w_out=v_conv_w_out, attn_w_qkv=v_attn_w_qkv,
             attn_q_gain=v_attn_q_gain, attn_k_gain=v_attn_k_gain, attn_sinks=v_attn_sinks, attn_w_o=v_attn_w_o,
             norm_mixer=v_norm_mixer, norm_ffn=v_norm_ffn, ffn_w_gate_up=v_ffn_w_gate_up, ffn_w_down=v_ffn_w_down)

    nseq, seq, d = x.shape
    t = nseq * seq
    chip = 2 * lax.axis_index("x") + lax.axis_index("y")
    core = lax.axis_index("c")
    place = jnp.stack([core, chip]).astype(jnp.int32)
    x0 = x.reshape(t, d)
    tgt = loss_target.reshape(t, d)

    cw_block = lax.dynamic_update_slice(jnp.zeros((SUBLANES, d), F32), conv_w[0], (0, chip * conv_w.shape[2]))
    def cast(k, layer=None):
        return _cast_own(f"cast_{k}" + ("" if layer is None else str(layer)), w[k], place, layer)

    (w_in,), cw_got = _seq_allgather_conv(1, [cast("conv_w_in")], cw_block)
    w_out, w_gu0, w_dn0 = _seq_allgather(
        "allgather_ffn0", 2, [cast("conv_w_out"), cast("ffn_w_gate_up", 0), cast("ffn_w_down", 0)])
    w_qkv, w_o, w_gu1, w_dn1 = _seq_allgather(
        "allgather_rest", 3, [cast("attn_w_qkv"), cast("attn_w_o"), cast("ffn_w_gate_up", 1), cast("ffn_w_down", 1)])
    w_out = w_out.reshape(1, d, d)
    w_o = w_o.reshape(1, d, d)
    w_gu = [w_gu0, w_gu1]
    w_dn = [w_dn0.reshape(1, D_FF, d), w_dn1.reshape(1, D_FF, d)]

    qg_pair = jnp.concatenate([attn_q_gain, attn_q_gain], axis=1)
    kg_pair = jnp.concatenate([attn_k_gain, attn_k_gain], axis=1)

    def ffn_bwd(i, dxo, dxo_b, xin, h, g, u, a):
        g_dn = _wgrad_down(f"ffn{i}_down_wgrad", a, dxo_b, D_FF // 2)
        dg, du = _mm_down_t_swiglu(f"ffn{i}_down_dgrad", dxo_b, w_dn[i], 0, g, u)
        g_gu = _wgrad_up2(f"ffn{i}_up_wgrad", h, dg, du)
        dxi, dxi_b, dgain = _dgrad_norm_ffn(f"ffn{i}_up_dgrad", dg, du, w_gu[i], 0, xin, norm_ffn[i:i + 1], dxo)
        return dxi, dxi_b, dgain, g_gu, g_dn

    h0, bcx = _mm_norm_up_joined("conv_in", x0, norm_mixer[0:1], w_in, 512)
    z = _conv_fwd(bcx, cw_block, cw_got, nseq, seq)
    x1, h1 = _mm_down_norm("conv_out", z, w_out, 0, x0, norm_ffn[0:1])
    g0, u0, a0 = _mm_up_swiglu("ffn0_up", h1, w_gu[0], 0)
    x2, h2 = _mm_down_norm("ffn0_down", a0, w_dn[0], 0, x1, norm_mixer[1:2])
    qkv = _mm_up_joined("attn_qkv", h2, w_qkv, 1024)
    o = _attn_fwd(qkv, qg_pair, kg_pair, attn_sinks, nseq, seq)
    x3, h3 = _mm_down_norm("attn_out", o, w_o, 0, x2, norm_ffn[1:2])
    g1, u1, a1 = _mm_up_swiglu("ffn1_up", h3, w_gu[1], 0)
    dy, dy_b, loss_part = _mm_down_loss("ffn1_down", a1, w_dn[1], 0, x3, tgt)

    finished = {k: None for k in BIG}

    def exchange(tag, cid, units):
        return units, _seq_exchange(f"exchange_{tag}", cid, [g for _, _, g in units])

    def scatter(tag, cid, group, after):
        units, got = group
        sums = [_sum_halves(f"sum_halves_{k}{l}", g, r, place, after) for (k, l, g), r in zip(units, got)]
        return units, sums, _seq_scatter(f"scatter_{tag}", cid, [pb for pb, _ in sums])

    def finish(group, after):
        units, sums, arrived = group
        for (k, l, _), (_, pf), r in zip(units, sums, arrived):
            finished[k] = _sum_partials(f"sum_partials_{k}{l}", pf, r, place, l, w[k].shape[0], finished[k], after)

    dx3, dx3_b, dnf1, g_gu1, g_dn1 = ffn_bwd(1, dy, dy_b, x3, h3, g1, u1, a1)
    ffn1 = exchange("ffn1", 4, [("ffn_w_down", 1, g_dn1), ("ffn_w_gate_up", 1, g_gu1)])
    g_o = _wgrad_down("attn_out_wgrad", o, dx3_b, d)
    do = _mm_down_t("attn_out_dgrad", dx3_b, w_o, 0)
    ffn1 = scatter("ffn1", 8, ffn1, do)
    dqkv, dqg, dkg, dsk = _attn_bwd(do, qkv, qg_pair, kg_pair, attn_sinks, nseq, seq)
    g_qkv = _wgrad_joined("attn_qkv_wgrad", h2, dqkv)
    attn = exchange("attn", 5, [("attn_w_o", 0, g_o), ("attn_w_qkv", 0, g_qkv)])
    dx2, dx2_b, dnm1 = _dgrad_norm_qkv("attn_qkv_dgrad", dqkv, w_qkv, x2, norm_mixer[1:2], dx3)
    finish(ffn1, dx2)
    attn = scatter("attn", 9, attn, dx2)
    dx1, dx1_b, dnf0, g_gu0, g_dn0 = ffn_bwd(0, dx2, dx2_b, x1, h1, g0, u0, a0)
    ffn0 = exchange("ffn0", 6, [("ffn_w_down", 0, g_dn0), ("ffn_w_gate_up", 0, g_gu0)])
    g_out = _wgrad_down("conv_out_wgrad", z, dx1_b, d)
    dz = _mm_down_t("conv_out_dgrad", dx1_b, w_out, 0)
    finish(attn, dz)
    ffn0 = scatter("ffn0", 10, ffn0, dz)
    dbcx, dcw = _conv_bwd(dz, bcx, cw_block, cw_got, nseq, seq)
    g_in = _wgrad_conv_in("conv_in_wgrad", h0, dbcx, conv_w_in.shape[2])
    conv = exchange("conv", 7, [("conv_w_out", 0, g_out), ("conv_w_in", 0, g_in)])
    dx0, _, dnm0 = _dgrad_norm_conv("conv_in_dgrad", dbcx, w_in, x0, norm_mixer[0:1], dx1)
    finish(ffn0, dx0)
    late = ("attn_w_qkv", "attn_w_o", "ffn_w_gate_up", "ffn_w_down")
    grads_late = _seq_share("share_late", 12, [finished[k] for k in late])
    conv = scatter("conv", 11, conv, dx0)

    grad, delta, new_m, new_v = {}, {}, {}, {}

    def adam(k, g):
        grad[k], delta[k], new_m[k], new_v[k] = _adam_step(f"adam_{k}", w[k], g, m[k], v[k])

    for k, g in zip(late, grads_late):
        adam(k, g)

    def blocks(src):
        return _small_block(src["norm_mixer"], src["norm_ffn"], src["conv_w"][0], src["attn_q_gain"],
                            src["attn_k_gain"], src["attn_sinks"], chip)

    loss, small = _small_step(dnm0, dnm1, dnf0, dnf1, dcw, dqg, dkg, dsk, loss_part,
                              blocks(w), blocks(m), blocks(v), conv_w.shape[2])
    for dst, part in zip((grad, delta, new_m, new_v), small):
        dst.update(part)

    finish(conv, [new_v[k] for k in late])
    last = ("conv_w_in", "conv_w_out")
    for k, g in zip(last, _seq_share("share_last", 13, [finished[k] for k in last])):
        adam(k, g)

    return (loss.reshape(()), dx0.reshape(nseq, seq, d), *[grad[k] for k in WEIGHT_NAMES], *[delta[k] for k in WEIGHT_NAMES],
            *[new_m[k] for k in WEIGHT_NAMES], *[new_v[k] for k in WEIGHT_NAMES])
```

```python
import jax
import jax.numpy as jnp
from jax import lax
from jax.experimental import pallas as pl
from jax.experimental.pallas import tpu as pltpu
from jax.experimental.pallas import tpu_sc as plsc

F32 = jnp.float32
BF16 = jnp.bfloat16

D_FF = 2816
N_Q_HEADS = 16
N_KV_HEADS = 4
HEAD_DIM = 64
WINDOW = 128
BLOCK = 128
EPS = 1e-6
N_CHIPS = 4
LANES = 128
SUBLANES = 8
BF16_ROWS = 16
MXU_COLS = 256
VMEM_LIMIT = 48 * 1024 * 1024
ADAM_LR, ADAM_B1, ADAM_B2, ADAM_EPS, ADAM_WD, ADAM_STEP = 0.001, 0.9, 0.999, 1e-08, 0.01, 10
ALIBI_SLOPES = tuple(2.0 ** (-8.0 * (h + 1) / N_Q_HEADS) for h in range(N_Q_HEADS))
SMALL_ROWS = 32
ROW_NORM_MIXER, ROW_NORM_FFN, ROW_CONV_W, ROW_MISC = 0, 8, 16, 24
SENT_NORM_MIXER, SENT_NORM_FFN, SENT_CONV_W, SENT_MISC = 0, 2, 4, 7
TILE_Q_GAIN, TILE_K_GAIN, TILE_SINKS, TILE_LOSS = 0, 1, 2, 3
MESH = pl.DeviceIdType.MESH

NN = ((1,), (0,))
NT = ((1,), (1,))
TN = ((0,), (0,))


def _dot(a, b, dims):
    return lax.dot_general(a, b, (dims, ((), ())), preferred_element_type=F32)


def _pick(n, cands):
    for c in cands:
        if n % c == 0:
            return c
    raise ValueError((n, cands))


def _row_tile(rows, row_bytes, cap_bytes):
    fits = [r for r in range(BF16_ROWS, rows + 1, BF16_ROWS) if rows % r == 0 and r * row_bytes <= cap_bytes]
    if not fits:
        raise ValueError((rows, row_bytes, cap_bytes))
    return fits[-1]


ELEMENTWISE_BLOCK = 3 << 19


def _resident(block_shape, index_map):
    return pl.BlockSpec(block_shape, index_map, pipeline_mode=pl.Buffered(1))


def _params(sem):
    return pltpu.CompilerParams(dimension_semantics=sem, vmem_limit_bytes=VMEM_LIMIT)


def _sds(shape, dtype):
    return jax.ShapeDtypeStruct(shape, dtype)


def _rms(xv):
    return lax.rsqrt(jnp.mean(xv * xv, axis=-1, keepdims=True) + EPS)


def _sigmoid(g):
    return 1.0 / (1.0 + jnp.exp(-g))


def _mm_up_joined(name, a, w4, tm_pref):
    t, k = a.shape
    _, _, _, nq = w4.shape
    tm = _pick(t, (tm_pref, 256, 128))

    def body(a_ref, w_ref, o_ref, wcat_ref):
        @pl.when(pl.program_id(0) == 0)
        def _():
            for q in range(N_CHIPS):
                wcat_ref[:, q * nq:(q + 1) * nq] = w_ref[q]

        o_ref[...] = _dot(a_ref[...], wcat_ref[...], NN).astype(BF16)

    return pl.pallas_call(
        body, name=name, grid=(t // tm,),
        in_specs=[pl.BlockSpec((tm, k), lambda i: (i, 0)),
                  pl.BlockSpec((None, N_CHIPS, k, nq), lambda i: (0, 0, 0, 0))],
        out_specs=pl.BlockSpec((tm, N_CHIPS * nq), lambda i: (i, 0)),
        out_shape=_sds((t, N_CHIPS * nq), BF16),
        scratch_shapes=[pltpu.VMEM((k, N_CHIPS * nq), BF16)],
        compiler_params=_params(("arbitrary",)))(a, w4)


def _mm_norm_up_joined(name, x, gain, w4, tm_pref):
    t, k = x.shape
    _, _, _, nq = w4.shape
    tm = _pick(t, (tm_pref, 256, 128))

    def body(x_ref, g_ref, w_ref, h_ref, o_ref, wcat_ref):
        @pl.when(pl.program_id(0) == 0)
        def _():
            for q in range(N_CHIPS):
                wcat_ref[:, q * nq:(q + 1) * nq] = w_ref[q]

        xv = x_ref[...]
        h = ((xv * _rms(xv)) * g_ref[...]).astype(BF16)
        h_ref[...] = h
        o_ref[...] = _dot(h, wcat_ref[...], NN).astype(BF16)

    return pl.pallas_call(
        body, name=name, grid=(t // tm,),
        in_specs=[pl.BlockSpec((tm, k), lambda i: (i, 0)), pl.BlockSpec((1, k), lambda i: (0, 0)),
                  _resident((None, N_CHIPS, k, nq), lambda i: (0, 0, 0, 0))],
        out_specs=[pl.BlockSpec((tm, k), lambda i: (i, 0)), pl.BlockSpec((tm, N_CHIPS * nq), lambda i: (i, 0))],
        out_shape=[_sds((t, k), BF16), _sds((t, N_CHIPS * nq), BF16)],
        scratch_shapes=[pltpu.VMEM((k, N_CHIPS * nq), BF16)],
        compiler_params=_params(("arbitrary",)))(x, gain, w4)


def _mm_up_swiglu(name, h, w4, layer):
    t, k = h.shape
    _, _, _, nq = w4.shape
    tm = _pick(t, (512, 256, 128))

    def body(h_ref, wg_ref, wu_ref, dag_ref, dau_ref, a_ref):
        hv = h_ref[...]
        g = _dot(hv, wg_ref[...], NN)
        u = _dot(hv, wu_ref[...], NN)
        sg = _sigmoid(g)
        silu = g * sg
        a = silu * u
        dag_ref[...] = (a + sg * (u - a)).astype(BF16)
        dau_ref[...] = silu.astype(BF16)
        a_ref[...] = a.astype(BF16)

    half = N_CHIPS // 2
    out = pl.BlockSpec((tm, nq), lambda j, i: (i, j))
    return pl.pallas_call(
        body, name=name, grid=(half, t // tm),
        in_specs=[pl.BlockSpec((tm, k), lambda j, i: (i, 0)),
                  pl.BlockSpec((None, None, k, nq), lambda j, i: (layer, j, 0, 0)),
                  pl.BlockSpec((None, None, k, nq), lambda j, i: (layer, half + j, 0, 0))],
        out_specs=[out, out, out],
        out_shape=[_sds((t, half * nq), BF16)] * 3,
        compiler_params=_params(("parallel", "parallel")))(h, w4, w4)


def _mm_down_norm(name, a, w, layer, res, gain):
    t, kf = a.shape
    _, _, n = w.shape
    tm = _pick(t, (1024, 512, 256, 128))

    def body(a_ref, w_ref, r_ref, g_ref, o_ref, h_ref):
        xo = r_ref[...] + _dot(a_ref[...], w_ref[...], NN)
        o_ref[...] = xo
        h_ref[...] = ((xo * _rms(xo)) * g_ref[...]).astype(BF16)

    row = pl.BlockSpec((tm, n), lambda i: (i, 0))
    return pl.pallas_call(
        body, name=name, grid=(t // tm,),
        in_specs=[pl.BlockSpec((tm, kf), lambda i: (i, 0)),
                  _resident((None, kf, n), lambda i: (layer, 0, 0)),
                  row, pl.BlockSpec((1, n), lambda i: (0, 0))],
        out_specs=[row, row],
        out_shape=[_sds((t, n), F32), _sds((t, n), BF16)],
        compiler_params=_params(("parallel",)))(a, w, res, gain)


def _mm_down_loss(name, a, w, layer, res, tgt):
    t, kf = a.shape
    _, _, n = w.shape
    tm = _pick(t, (1024, 512, 256, 128))
    steps = t // tm

    def body(a_ref, w_ref, r_ref, t_ref, dy_ref, dyb_ref, l_ref, acc_ref):
        i = pl.program_id(0)

        @pl.when(i == 0)
        def _():
            acc_ref[...] = jnp.zeros_like(acc_ref)

        e = (r_ref[...] + _dot(a_ref[...], w_ref[...], NN)) - t_ref[...]
        dy = e * (1.0 / n)
        dy_ref[...] = dy
        dyb_ref[...] = dy.astype(BF16)
        acc_ref[...] += (e * e).reshape(tm // SUBLANES, SUBLANES, n).sum(axis=0)

        @pl.when(i == steps - 1)
        def _():
            l_ref[...] = jnp.sum(acc_ref[...], keepdims=True) * (0.5 / n)

    row = pl.BlockSpec((tm, n), lambda i: (i, 0))
    return pl.pallas_call(
        body, name=name, grid=(steps,),
        in_specs=[pl.BlockSpec((tm, kf), lambda i: (i, 0)),
                  _resident((None, kf, n), lambda i: (layer, 0, 0)), row, row],
        out_specs=[row, row, pl.BlockSpec((1, 1), lambda i: (0, 0))],
        out_shape=[_sds((t, n), F32), _sds((t, n), BF16), _sds((1, 1), F32)],
        scratch_shapes=[pltpu.VMEM((SUBLANES, n), F32)],
        compiler_params=_params(("arbitrary",)))(a, w, res, tgt)


def _mm_down_t(name, dx, w, layer):
    t, n = dx.shape
    _, kf, _ = w.shape
    tm = _pick(t, (1024, 512, 256, 128))

    def body(a_ref, w_ref, o_ref):
        o_ref[...] = _dot(a_ref[...].astype(BF16), w_ref[...], NT).astype(BF16)

    return pl.pallas_call(
        body, name=name, grid=(t // tm,),
        in_specs=[pl.BlockSpec((tm, n), lambda i: (i, 0)),
                  _resident((None, kf, n), lambda i: (layer, 0, 0))],
        out_specs=pl.BlockSpec((tm, kf), lambda i: (i, 0)),
        out_shape=_sds((t, kf), BF16),
        compiler_params=_params(("parallel",)))(dx, w)


def _mm_down_t_swiglu(name, dx, w, layer, g, u):
    t, n = dx.shape
    f = g.shape[1]
    tm = _pick(t, (512, 256, 128))

    def body(a_ref, w_ref, dag_ref, dau_ref, dg_ref, du_ref):
        da = _dot(a_ref[...].astype(BF16), w_ref[...], NT)
        dg_ref[...] = (da * dag_ref[...].astype(F32)).astype(BF16)
        du_ref[...] = (da * dau_ref[...].astype(F32)).astype(BF16)

    tile = pl.BlockSpec((tm, f), lambda i: (i, 0))
    return pl.pallas_call(
        body, name=name, grid=(t // tm,),
        in_specs=[pl.BlockSpec((tm, n), lambda i: (i, 0)),
                  _resident((None, f, n), lambda i: (layer, 0, 0)), tile, tile],
        out_specs=[tile, tile],
        out_shape=[_sds((t, f), BF16)] * 2,
        compiler_params=_params(("parallel",)))(dx, w, g, u)


def _dgrad_norm(name, acts, act_blocks, pieces, w4, layer, x, gain, dres, emit_bf16=True):
    t, d = x.shape
    _, _, k, nq = w4.shape
    tm = _pick(t, (512, 256, 128))
    n_act = len(acts)

    def body(*refs):
        act_refs = refs[:n_act]
        w_ref, x_ref, g_ref, dr_ref, dx_ref = refs[n_act:n_act + 5]
        dxb_ref, dg_ref = (refs[-2] if emit_bf16 else None), refs[-1]

        @pl.when(pl.program_id(0) == 0)
        def _():
            dg_ref[...] = jnp.zeros_like(dg_ref)

        dh = None
        for a_tile, w_tile in pieces(act_refs, w_ref):
            term = _dot(a_tile, w_tile, NT)
            dh = term if dh is None else dh + term
        xv = x_ref[...]
        r = _rms(xv)
        xhat = xv * r
        gd = dh * g_ref[...]
        dx = dr_ref[...] + r * (gd - xhat * jnp.mean(gd * xhat, axis=-1, keepdims=True))
        dx_ref[...] = dx
        if emit_bf16:
            dxb_ref[...] = dx.astype(BF16)
        dg_ref[...] += (dh * xhat).reshape(tm // SUBLANES, SUBLANES, d).sum(axis=0)

    row = pl.BlockSpec((tm, d), lambda i: (i, 0))
    copies = [(row, _sds((t, d), BF16))] if emit_bf16 else []
    outs = pl.pallas_call(
        body, name=name, grid=(t // tm,),
        in_specs=[*act_blocks(tm),
                  _resident((None, N_CHIPS, k, nq), lambda i: (layer, 0, 0, 0)),
                  row, pl.BlockSpec((1, d), lambda i: (0, 0)), row],
        out_specs=[row, *[c[0] for c in copies], pl.BlockSpec((SUBLANES, d), lambda i: (0, 0))],
        out_shape=[_sds((t, d), F32), *[c[1] for c in copies], _sds((SUBLANES, d), F32)],
        compiler_params=_params(("arbitrary",)))(*acts, w4, x, gain, dres)
    return (outs[0], outs[1] if emit_bf16 else None, outs[-1])


def _dgrad_norm_ffn(name, dg, du, w4, layer, x, gain, dres):
    nq = w4.shape[3]
    f = dg.shape[1]

    def blocks(tm):
        return [pl.BlockSpec((tm, f), lambda i: (i, 0))] * 2

    def pieces(act_refs, w_ref):
        dg_ref, du_ref = act_refs
        return [(dg_ref[:, 0:nq], w_ref[0]), (dg_ref[:, nq:2 * nq], w_ref[1]),
                (du_ref[:, 0:nq], w_ref[2]), (du_ref[:, nq:2 * nq], w_ref[3])]

    return _dgrad_norm(name, [dg, du], blocks, pieces, w4, layer, x, gain, dres)


def _dgrad_norm_qkv(name, dqkv, w4, x, gain, dres):
    nq = w4.shape[3]

    def blocks(tm):
        return [pl.BlockSpec((tm, N_CHIPS * nq), lambda i: (i, 0))]

    def pieces(act_refs, w_ref):
        return [(act_refs[0][:, q * nq:(q + 1) * nq], w_ref[q]) for q in range(N_CHIPS)]

    return _dgrad_norm(name, [dqkv], blocks, pieces, w4, 0, x, gain, dres)


def _dgrad_norm_conv(name, d3, w4, x, gain, dres):
    _, _, d = d3.shape
    nq = w4.shape[3]
    per_part, per_q = d // MXU_COLS, nq // MXU_COLS

    def blocks(tm):
        return [pl.BlockSpec((3, tm, d), lambda i: (0, i, 0))]

    def pieces(act_refs, w_ref):
        out = []
        for jb in range(3 * per_part):
            ca, cw = (jb % per_part) * MXU_COLS, (jb % per_q) * MXU_COLS
            out.append((act_refs[0][jb // per_part, :, ca:ca + MXU_COLS], w_ref[jb // per_q, :, cw:cw + MXU_COLS]))
        return out

    return _dgrad_norm(name, [d3], blocks, pieces, w4, 0, x, gain, dres, emit_bf16=False)


def _wgrad_up2(name, h, dg, du):
    t, k = h.shape
    nq = dg.shape[1] // 2
    tk = _pick(t, (2048, 1024, 512, 256, 128))
    steps = t // tk
    half = N_CHIPS // 2

    def body(h_ref, dg_ref, du_ref, o_ref):
        q = pl.program_id(0)

        @pl.when(pl.program_id(1) == 0)
        def _():
            o_ref[...] = jnp.zeros_like(o_ref)

        @pl.when(q < half)
        def _():
            o_ref[...] += _dot(h_ref[...], dg_ref[...], TN)

        @pl.when(q >= half)
        def _():
            o_ref[...] += _dot(h_ref[...], du_ref[...], TN)

    return pl.pallas_call(
        body, name=name, grid=(N_CHIPS, steps),
        in_specs=[pl.BlockSpec((tk, k), lambda q, s: (s, 0)),
                  pl.BlockSpec((tk, nq), lambda q, s: (jnp.where(q < half, s, steps - 1), jnp.minimum(q, half - 1))),
                  pl.BlockSpec((tk, nq), lambda q, s: (jnp.where(q >= half, s, 0), jnp.maximum(q - half, 0)))],
        out_specs=pl.BlockSpec((None, k, nq), lambda q, s: (q, 0, 0)),
        out_shape=_sds((N_CHIPS, k, nq), F32),
        compiler_params=_params(("parallel", "arbitrary")))(h, dg, du)


def _wgrad_joined(name, h, dy):
    t, k = h.shape
    nq = dy.shape[1] // N_CHIPS
    tk = _pick(t, (2048, 1024, 512, 256, 128))

    def body(h_ref, dy_ref, o_ref):
        @pl.when(pl.program_id(0) == 0)
        def _():
            o_ref[...] = jnp.zeros_like(o_ref)

        res = _dot(h_ref[...], dy_ref[...], TN)
        for q in range(N_CHIPS):
            o_ref[q] += res[:, q * nq:(q + 1) * nq]

    return pl.pallas_call(
        body, name=name, grid=(t // tk,),
        in_specs=[pl.BlockSpec((tk, k), lambda s: (s, 0)), pl.BlockSpec((tk, N_CHIPS * nq), lambda s: (s, 0))],
        out_specs=pl.BlockSpec((N_CHIPS, k, nq), lambda s: (0, 0, 0)),
        out_shape=_sds((N_CHIPS, k, nq), F32),
        compiler_params=_params(("arbitrary",)))(h, dy)


def _wgrad_conv_in(name, h, d3, nq):
    t, k = h.shape
    d = d3.shape[2]
    per_part, per_q = d // MXU_COLS, nq // MXU_COLS
    tk = _pick(t, (512, 256, 128))

    def body(h_ref, d_ref, o_ref):
        @pl.when(pl.program_id(0) == 0)
        def _():
            o_ref[...] = jnp.zeros_like(o_ref)

        hv = h_ref[...]
        for part in range(3):
            res = _dot(hv, d_ref[part], TN)
            for cc in range(per_part):
                jb = part * per_part + cc
                co = (jb % per_q) * MXU_COLS
                o_ref[jb // per_q, :, co:co + MXU_COLS] += res[:, cc * MXU_COLS:(cc + 1) * MXU_COLS]

    return pl.pallas_call(
        body, name=name, grid=(t // tk,),
        in_specs=[pl.BlockSpec((tk, k), lambda s: (s, 0)), pl.BlockSpec((3, tk, d), lambda s: (0, s, 0))],
        out_specs=pl.BlockSpec((N_CHIPS, k, nq), lambda s: (0, 0, 0)),
        out_shape=_sds((N_CHIPS, k, nq), F32),
        compiler_params=_params(("arbitrary",)))(h, d3)


def _wgrad_down(name, a, dx, tmw):
    t, kf = a.shape
    n = dx.shape[1]
    tk = _pick(t, (2048, 1024, 512, 256, 128))

    def body(a_ref, b_ref, o_ref):
        @pl.when(pl.program_id(1) == 0)
        def _():
            o_ref[...] = jnp.zeros_like(o_ref)

        o_ref[...] += _dot(a_ref[...], b_ref[...].astype(BF16), TN)

    g = pl.pallas_call(
        body, name=name, grid=(kf // tmw, t // tk),
        in_specs=[pl.BlockSpec((tk, tmw), lambda j, s: (s, j)), pl.BlockSpec((tk, n), lambda j, s: (s, 0))],
        out_specs=pl.BlockSpec((tmw, n), lambda j, s: (j, 0)),
        out_shape=_sds((kf, n), F32),
        compiler_params=_params(("parallel", "arbitrary")))(a, dx)
    return g.reshape(N_CHIPS, kf // N_CHIPS, n)


def _shift_rows(u, k, rows):
    s = u.shape[0]
    if k > 0:
        r = pltpu.roll(u, k, 0)
        return jnp.concatenate([jnp.where(rows >= k, r[0:SUBLANES], 0.0), r[SUBLANES:]], axis=0)
    r = pltpu.roll(u, s + k, 0)
    return jnp.concatenate([r[:s - SUBLANES], jnp.where(rows < SUBLANES + k, r[s - SUBLANES:], 0.0)], axis=0)


def _conv_taps(cw_ref, got_ref):
    return (cw_ref[...] + got_ref[0]) + (got_ref[1] + got_ref[2])


def _conv_fwd(bcx, cw, cw_got, nseq, seq):
    t, d3 = bcx.shape
    d = d3 // 3
    cb = 2 * MXU_COLS
    nj = d // cb

    def body(b_ref, c_ref, x_ref, cw_ref, got_ref, z_ref):
        u = b_ref[...].astype(F32) * x_ref[...].astype(F32)
        rows = lax.broadcasted_iota(jnp.int32, (SUBLANES, cb), 0)
        cwv = _conv_taps(cw_ref, got_ref)
        y = cwv[2:3] * u + cwv[1:2] * _shift_rows(u, 1, rows) + cwv[0:1] * _shift_rows(u, 2, rows)
        z_ref[...] = (c_ref[...].astype(F32) * y).astype(BF16)

    return pl.pallas_call(
        body, name="conv_fwd", grid=(nseq, nj),
        in_specs=[pl.BlockSpec((seq, cb), lambda b, j: (b, j)),
                  pl.BlockSpec((seq, cb), lambda b, j: (b, nj + j)),
                  pl.BlockSpec((seq, cb), lambda b, j: (b, 2 * nj + j)),
                  pl.BlockSpec((SUBLANES, cb), lambda b, j: (0, j)),
                  pl.BlockSpec((3, SUBLANES, cb), lambda b, j: (0, 0, j))],
        out_specs=pl.BlockSpec((seq, cb), lambda b, j: (b, j)),
        out_shape=_sds((t, d), BF16),
        compiler_params=_params(("parallel", "parallel")))(bcx, bcx, bcx, cw, cw_got)


def _conv_bwd(dz, bcx, cw, cw_got, nseq, seq):
    t, d3 = bcx.shape
    d = d3 // 3
    cb = MXU_COLS
    nj = d // cb

    def body(dz_ref, b_ref, c_ref, x_ref, cw_ref, got_ref, o_ref, dcw_ref):
        @pl.when(pl.program_id(1) == 0)
        def _():
            dcw_ref[...] = jnp.zeros_like(dcw_ref)

        b = b_ref[...].astype(F32)
        c = c_ref[...].astype(F32)
        xv = x_ref[...].astype(F32)
        dzv = dz_ref[...].astype(F32)
        u = b * xv
        rows = lax.broadcasted_iota(jnp.int32, (SUBLANES, cb), 0)
        u1 = _shift_rows(u, 1, rows)
        u2 = _shift_rows(u, 2, rows)
        cwv = _conv_taps(cw_ref, got_ref)
        y = cwv[2:3] * u + cwv[1:2] * u1 + cwv[0:1] * u2
        dyc = dzv * c
        du = cwv[2:3] * dyc + cwv[1:2] * _shift_rows(dyc, -1, rows) + cwv[0:1] * _shift_rows(dyc, -2, rows)
        o_ref[0] = (du * xv).astype(BF16)
        o_ref[1] = (dzv * y).astype(BF16)
        o_ref[2] = (du * b).astype(BF16)
        s0 = jnp.sum(dyc * u2, axis=0, keepdims=True)
        s1 = jnp.sum(dyc * u1, axis=0, keepdims=True)
        s2 = jnp.sum(dyc * u, axis=0, keepdims=True)
        tap = lax.broadcasted_iota(jnp.int32, (3, cb), 0)
        dcw_ref[...] += jnp.where(tap == 0, s0, jnp.where(tap == 1, s1, s2))

    return pl.pallas_call(
        body, name="conv_bwd", grid=(nj, nseq),
        in_specs=[pl.BlockSpec((seq, cb), lambda j, b: (b, j)),
                  pl.BlockSpec((seq, cb), lambda j, b: (b, j)),
                  pl.BlockSpec((seq, cb), lambda j, b: (b, nj + j)),
                  pl.BlockSpec((seq, cb), lambda j, b: (b, 2 * nj + j)),
                  pl.BlockSpec((SUBLANES, cb), lambda j, b: (0, j)),
                  pl.BlockSpec((3, SUBLANES, cb), lambda j, b: (0, 0, j))],
        out_specs=[pl.BlockSpec((3, seq, cb), lambda j, b: (0, b, j)),
                   pl.BlockSpec((3, cb), lambda j, b: (0, j))],
        out_shape=[_sds((3, t, d), BF16), _sds((3, d), F32)],
        compiler_params=_params(("parallel", "arbitrary")))(dz, bcx, bcx, bcx, cw, cw_got)


def _pair_norm(x, gain_pair, low):
    sq = x * x
    ss_lo = jnp.sum(jnp.where(low, sq, 0.0), axis=-1, keepdims=True)
    ss_hi = jnp.sum(jnp.where(low, 0.0, sq), axis=-1, keepdims=True)
    r = lax.rsqrt(jnp.where(low, ss_lo, ss_hi) * (1.0 / HEAD_DIM) + EPS)
    xhat = x * r
    return xhat * gain_pair, xhat, r


KEYS = 2 * BLOCK
QK_SCALE = 1.0 / (HEAD_DIM ** 0.5)
N_PAIRS = N_Q_HEADS // 2


def _earlier_block(shape=(BLOCK, BLOCK)):
    return lax.broadcasted_iota(jnp.int32, shape, 0) > lax.broadcasted_iota(jnp.int32, shape, 1)


def _fill_bias(bias_ref):
    rows = lax.broadcasted_iota(jnp.int32, (2 * BLOCK, BLOCK), 0)
    qi = lax.broadcasted_iota(jnp.int32, (2 * BLOCK, BLOCK), 1)
    odd_head = rows >= BLOCK
    kj = jnp.where(odd_head, rows - BLOCK, rows)
    earlier = kj > qi
    dist = (jnp.where(earlier, BLOCK, 0) + qi - kj).astype(F32)
    for j in range(N_PAIRS):
        slope = jnp.where(odd_head, ALIBI_SLOPES[2 * j + 1], ALIBI_SLOPES[2 * j])
        bias = -slope * dist
        bias_ref[1, j] = bias
        bias_ref[0, j] = jnp.where(earlier, -1e30, bias)


def _merge_blocks(x_t, earlier):
    return jnp.concatenate([jnp.where(earlier, x_t[e * KEYS:e * KEYS + BLOCK], x_t[e * KEYS + BLOCK:(e + 1) * KEYS])
                            for e in range(2)], axis=0)


def _split_blocks(heads, earlier):
    parts = []
    for x in heads:
        parts += [jnp.where(earlier, x, 0.0), jnp.where(earlier, 0.0, x)]
    return jnp.concatenate(parts, axis=0).astype(BF16)


def _kv_pair_rows(kv_tile, parity, low):
    own = jnp.where(low if parity == 0 else jnp.logical_not(low), kv_tile, 0.0)
    other = pltpu.roll(own, HEAD_DIM, 1)
    lo, hi = (own, other) if parity == 0 else (other, own)
    return jnp.concatenate([lo, hi], axis=0).astype(BF16)


def _pair_softmax(s_t, sink_even, sink_odd):
    out = []
    for e, sink in enumerate((sink_even, sink_odd)):
        se = s_t[e * BLOCK:(e + 1) * BLOCK]
        m = jnp.maximum(jnp.max(se, axis=0, keepdims=True), sink)
        ee = jnp.exp(se - m)
        es = jnp.exp(sink - m)
        inv = 1.0 / (jnp.sum(ee, axis=0, keepdims=True) + es)
        out.append((ee * inv, es * inv))
    return out


def _attn_rows(n):
    q0 = pl.multiple_of(n * BLOCK, BLOCK)
    k0 = pl.multiple_of(jnp.maximum(n - 1, 0) * BLOCK, BLOCK)
    return q0, k0, jnp.minimum(n, 1)


def _key_rows(qkv_ref, k0, q0, col):
    return jnp.concatenate([qkv_ref[pl.ds(k0, BLOCK), col:col + LANES], qkv_ref[pl.ds(q0, BLOCK), col:col + LANES]],
                           axis=0).astype(F32)


def _attn_fwd(qkv, qg_pair, kg_pair, sinks, nseq, seq):
    t = qkv.shape[0]
    dq = N_Q_HEADS * HEAD_DIM
    dkv = N_KV_HEADS * HEAD_DIM

    def body(sk_ref, qkv_ref, qg_ref, kg_ref, o_ref, bias_ref):
        @pl.when(pl.program_id(0) == 0)
        def _():
            _fill_bias(bias_ref)

        low = lax.broadcasted_iota(jnp.int32, (1, LANES), 1) < HEAD_DIM
        earlier = _earlier_block()
        qg = qg_ref[...] * QK_SCALE
        kg = kg_ref[...]

        def blk(n, carry):
            q0, k0, later = _attn_rows(n)
            for kt in range(dkv // LANES):
                kraw = _key_rows(qkv_ref, k0, q0, dq + kt * LANES)
                vraw = _key_rows(qkv_ref, k0, q0, dq + dkv + kt * LANES)
                kn, _, _ = _pair_norm(kraw, kg, low)
                for par in range(2):
                    kh = 2 * kt + par
                    k_pair = _kv_pair_rows(kn, par, low)
                    v_pair = _kv_pair_rows(vraw, par, low)
                    for jj in range(2):
                        j = 2 * kh + jj
                        qraw = qkv_ref[pl.ds(q0, BLOCK), j * LANES:(j + 1) * LANES].astype(F32)
                        qn, _, _ = _pair_norm(qraw, qg, low)
                        s_t = _merge_blocks(_dot(k_pair, qn.astype(BF16), NT), earlier) + bias_ref[later, j]
                        (p0, _), (p1, _) = _pair_softmax(s_t, sk_ref[0, 2 * j], sk_ref[0, 2 * j + 1])
                        p_t = _split_blocks((p0, p1), earlier)
                        o_ref[pl.ds(q0, BLOCK), j * LANES:(j + 1) * LANES] = _dot(p_t, v_pair, TN).astype(BF16)
            return carry

        lax.fori_loop(0, seq // BLOCK, blk, 0)

    return pl.pallas_call(
        body, name="attn_fwd", grid=(nseq,),
        in_specs=[pl.BlockSpec(memory_space=pltpu.SMEM),
                  pl.BlockSpec((seq, dq + 2 * dkv), lambda b: (b, 0)),
                  pl.BlockSpec((1, LANES), lambda b: (0, 0)),
                  pl.BlockSpec((1, LANES), lambda b: (0, 0))],
        out_specs=pl.BlockSpec((seq, dq), lambda b: (b, 0)),
        out_shape=_sds((t, dq), BF16),
        scratch_shapes=[pltpu.VMEM((2, N_PAIRS, 2 * BLOCK, BLOCK), F32)],
        compiler_params=_params(("arbitrary",)))(sinks, qkv, qg_pair, kg_pair)


def _attn_bwd(do, qkv, qg_pair, kg_pair, sinks, nseq, seq):
    t = qkv.shape[0]
    dq = N_Q_HEADS * HEAD_DIM
    dkv = N_KV_HEADS * HEAD_DIM

    def body(sk_ref, do_ref, qkv_ref, qg_ref, kg_ref, o_ref, dqg_ref, dkg_ref, dsk_ref, acc_ref, bias_ref):
        @pl.when(pl.program_id(0) == 0)
        def _():
            _fill_bias(bias_ref)
            dqg_ref[...] = jnp.zeros_like(dqg_ref)
            dkg_ref[...] = jnp.zeros_like(dkg_ref)
            dsk_ref[...] = jnp.zeros_like(dsk_ref)

        acc_ref[...] = jnp.zeros_like(acc_ref)
        low = lax.broadcasted_iota(jnp.int32, (1, LANES), 1) < HEAD_DIM
        earlier = _earlier_block()
        head_row = lax.broadcasted_iota(jnp.int32, (N_Q_HEADS, LANES), 0)
        qg = qg_ref[...] * QK_SCALE
        kg = kg_ref[...]

        def blk(n, carry):
            dqg_acc, dkg_acc, dsk_acc = carry
            q0, k0, later = _attn_rows(n)
            for kt in range(dkv // LANES):
                kraw = _key_rows(qkv_ref, k0, q0, dq + kt * LANES)
                vraw = _key_rows(qkv_ref, k0, q0, dq + dkv + kt * LANES)
                kn, khat, rk = _pair_norm(kraw, kg, low)
                dk_tile = None
                dv_tile = None
                for par in range(2):
                    kh = 2 * kt + par
                    own = low if par == 0 else jnp.logical_not(low)
                    k_pair = _kv_pair_rows(kn, par, low)
                    v_pair = _kv_pair_rows(vraw, par, low)
                    dkn_rows = jnp.zeros((2 * KEYS, LANES), F32)
                    dv_rows = jnp.zeros((2 * KEYS, LANES), F32)
                    for jj in range(2):
                        j = 2 * kh + jj
                        qraw = qkv_ref[pl.ds(q0, BLOCK), j * LANES:(j + 1) * LANES].astype(F32)
                        qn, qhat, rq = _pair_norm(qraw, qg, low)
                        qn_b = qn.astype(BF16)
                        do_b = do_ref[pl.ds(q0, BLOCK), j * LANES:(j + 1) * LANES]
                        s_t = _merge_blocks(_dot(k_pair, qn_b, NT), earlier) + bias_ref[later, j]
                        dp_t = _merge_blocks(_dot(v_pair, do_b, NT), earlier)
                        ds_heads = []
                        probs = _pair_softmax(s_t, sk_ref[0, 2 * j], sk_ref[0, 2 * j + 1])
                        for e, (p, ps) in enumerate(probs):
                            dp = dp_t[e * BLOCK:(e + 1) * BLOCK]
                            dsum = jnp.sum(p * dp, axis=0, keepdims=True)
                            ds_heads.append(p * (dp - dsum))
                            dsk_acc = dsk_acc - jnp.where(head_row == 2 * j + e, ps * dsum, 0.0)
                        p_t = _split_blocks((probs[0][0], probs[1][0]), earlier)
                        ds_t = _split_blocks(ds_heads, earlier)
                        dv_rows = dv_rows + _dot(p_t, do_b, NN)
                        dkn_rows = dkn_rows + _dot(ds_t, qn_b, NN)
                        dqn = _dot(ds_t, k_pair, TN)
                        dqg_acc = dqg_acc + jnp.sum(dqn * qhat, axis=0, keepdims=True)
                        dqhat = dqn * qg
                        prod = dqhat * qhat
                        m_lo = jnp.sum(jnp.where(low, prod, 0.0), axis=-1, keepdims=True)
                        m_hi = jnp.sum(jnp.where(low, 0.0, prod), axis=-1, keepdims=True)
                        mean = jnp.where(low, m_lo, m_hi) * (1.0 / HEAD_DIM)
                        o_ref[pl.ds(q0, BLOCK), j * LANES:(j + 1) * LANES] = (rq * (dqhat - qhat * mean)).astype(BF16)
                    dkn_acc = jnp.where(low, dkn_rows[0:KEYS], dkn_rows[KEYS:2 * KEYS])
                    dv_acc = jnp.where(low, dv_rows[0:KEYS], dv_rows[KEYS:2 * KEYS])
                    dkn = dkn_acc + pltpu.roll(dkn_acc, HEAD_DIM, 1)
                    dvh = dv_acc + pltpu.roll(dv_acc, HEAD_DIM, 1)
                    khat_own = jnp.where(own, khat, 0.0)
                    khat_dup = khat_own + pltpu.roll(khat_own, HEAD_DIM, 1)
                    dkg_acc = dkg_acc + jnp.sum(jnp.where(own, dkn * khat_dup, 0.0), axis=0, keepdims=True)
                    dkhat = dkn * kg
                    mean_k = jnp.sum(dkhat * khat_dup, axis=-1, keepdims=True) * (1.0 / LANES)
                    dk_raw = rk * (dkhat - khat_dup * mean_k)
                    dk_tile = jnp.where(own, dk_raw, 0.0) if dk_tile is None else jnp.where(own, dk_raw, dk_tile)
                    dv_tile = jnp.where(own, dvh, 0.0) if dv_tile is None else jnp.where(own, dvh, dv_tile)
                for r0, part in ((k0, slice(0, BLOCK)), (q0, slice(BLOCK, KEYS))):
                    acc_ref[pl.ds(r0, BLOCK), kt * LANES:(kt + 1) * LANES] += dk_tile[part]
                    acc_ref[pl.ds(r0, BLOCK), dkv + kt * LANES:dkv + (kt + 1) * LANES] += dv_tile[part]
            return dqg_acc, dkg_acc, dsk_acc

        zero = jnp.zeros((1, LANES), F32)
        carry = (zero, zero, jnp.zeros((N_Q_HEADS, LANES), F32))
        dqg_acc, dkg_acc, dsk_acc = lax.fori_loop(0, seq // BLOCK, blk, carry)
        dqg_ref[...] += dqg_acc * QK_SCALE
        dkg_ref[...] += dkg_acc
        dsk_ref[...] += dsk_acc
        o_ref[:, dq:dq + 2 * dkv] = acc_ref[...].astype(BF16)

    small = pl.BlockSpec((1, LANES), lambda b: (0, 0))
    heads = pl.BlockSpec((N_Q_HEADS, LANES), lambda b: (0, 0))
    return pl.pallas_call(
        body, name="attn_bwd", grid=(nseq,),
        in_specs=[pl.BlockSpec(memory_space=pltpu.SMEM),
                  pl.BlockSpec((seq, dq), lambda b: (b, 0)),
                  pl.BlockSpec((seq, dq + 2 * dkv), lambda b: (b, 0)),
                  small, small],
        out_specs=[pl.BlockSpec((seq, dq + 2 * dkv), lambda b: (b, 0)), small, small, heads],
        out_shape=[_sds((t, dq + 2 * dkv), BF16), _sds((1, LANES), F32), _sds((1, LANES), F32),
                   _sds((N_Q_HEADS, LANES), F32)],
        scratch_shapes=[pltpu.VMEM((seq, 2 * dkv), F32), pltpu.VMEM((2, N_PAIRS, 2 * BLOCK, BLOCK), F32)],
        compiler_params=_params(("arbitrary",)))(sinks, do, qkv, qg_pair, kg_pair)


def _place():
    x, y, c = lax.axis_index("x"), lax.axis_index("y"), lax.axis_index("c")
    other_chips = [(1 - x, y), (x, 1 - y), (1 - x, 1 - y)]
    return x, y, c, other_chips


def _half_rows(c, rows):
    rh = rows // 2
    return pl.ds(pl.multiple_of(c * rh, BF16_ROWS), rh)


def _cast_own(name, w, place, layer=None):
    nl, r, cdim = w.shape
    first = 0
    if layer is not None:
        nl, first = 1, layer
    rt = _row_tile(r, 4 * cdim, 2 * ELEMENTWISE_BLOCK)

    def body(s_ref, w_ref, o_ref):
        o_ref[...] = w_ref[...].astype(BF16)

    grid_spec = pltpu.PrefetchScalarGridSpec(
        num_scalar_prefetch=1, grid=(nl, r // rt),
        in_specs=[pl.BlockSpec((None, rt, cdim), lambda l, i, s: (first + l, i, 0))],
        out_specs=pl.BlockSpec((None, None, rt, cdim), lambda l, i, s: (l, s[1], i, 0)))
    return pl.pallas_call(
        body, name=name, grid_spec=grid_spec, out_shape=_sds((nl, N_CHIPS, r, cdim), BF16),
        compiler_params=_params(("parallel", "parallel")))(place, w)


def _gather_protocol(outs, shapes, send_sems, recv_sems):
    n = len(outs)
    x, y, c, other_chips = _place()
    me_chip = 2 * x + y
    sibling = (x, y, 1 - c)

    def rows(u, chip, half):
        return outs[u].at[:, chip, _half_rows(half, shapes[u][2]), :]

    def copy(sem, part, to):
        return pltpu.make_async_remote_copy(src_ref=part, dst_ref=part, send_sem=send_sems.at[sem],
                                            recv_sem=recv_sems.at[sem], device_id=to, device_id_type=MESH)

    sends = []
    for u in range(n):
        for k, chip in enumerate(other_chips):
            cp = copy(6 * u + k, rows(u, me_chip, c), (*chip, c))
            cp.start()
            sends.append(cp)
    for u in range(n):
        for k, chip in enumerate(other_chips):
            got = rows(u, 2 * chip[0] + chip[1], c)
            copy(6 * u + k, got, (*chip, c)).wait_recv()
            cp = copy(6 * u + 3 + k, got, sibling)
            cp.start()
            sends.append(cp)
    for u in range(n):
        for k, chip in enumerate(other_chips):
            copy(6 * u + 3 + k, rows(u, 2 * chip[0] + chip[1], 1 - c), sibling).wait_recv()
    for cp in sends:
        cp.wait_send()


def _hbm_ref(a):
    return jax.new_ref(a, memory_space=pltpu.MemorySpace.HBM)


def _sibling_peer():
    x, y, c, _ = _place()
    return [(x, y, 1 - c)]


def _chip_peers():
    x, y, c, other_chips = _place()
    return [(*chip, c) for chip in other_chips]


def _gather_peers():
    return _chip_peers() + _sibling_peer()


def _on_sequencer(name, collective_id, n_sems, peers, protocol, operands=(), out_types=()):
    n_in, n_out = len(operands), len(out_types)

    def launch(*refs):
        send_sems, recv_sems = refs[n_in + n_out:]
        barrier = pltpu.get_barrier_semaphore()
        targets = peers()
        for peer in targets:
            pl.semaphore_signal(barrier, inc=1, device_id=peer, device_id_type=MESH)
        pl.semaphore_wait(barrier, len(targets))
        protocol(refs[:n_in], refs[n_in:n_in + n_out], send_sems, recv_sems)

    return pl.kernel(
        launch, out_type=tuple(out_types), mesh=plsc.ScalarSubcoreMesh(axis_name="sequencer", num_cores=1), name=name,
        scratch_types=(pltpu.SemaphoreType.DMA((n_sems,)), pltpu.SemaphoreType.DMA((n_sems,))),
        compiler_params=pltpu.CompilerParams(collective_id=collective_id))(*operands)


def _seq_allgather(name, collective_id, bufs):
    shapes = [b.shape for b in bufs]
    refs = [_hbm_ref(b) for b in bufs]
    _on_sequencer(name, collective_id, 6 * len(bufs), _gather_peers,
                  lambda ins, outs, send_sems, recv_sems: _gather_protocol(refs, shapes, send_sems, recv_sems))
    return [r[...] for r in refs]


def _taps_protocol(block_ref, got_ref, send_sems, recv_sems, first_sem):
    x, y, c, other_chips = _place()
    copies = []
    for k, chip in enumerate(other_chips):
        cp = pltpu.make_async_remote_copy(src_ref=block_ref, dst_ref=got_ref.at[k], send_sem=send_sems.at[first_sem + k],
                                          recv_sem=recv_sems.at[first_sem + k], device_id=(*chip, c), device_id_type=MESH)
        cp.start()
        copies.append(cp)
    return copies


def _seq_allgather_conv(collective_id, bufs, cw_block):
    shapes = [b.shape for b in bufs]
    refs = [_hbm_ref(b) for b in bufs]

    def protocol(ins, outs, send_sems, recv_sems):
        taps = _taps_protocol(ins[0], outs[0], send_sems, recv_sems, 6 * len(bufs))
        _gather_protocol(refs, shapes, send_sems, recv_sems)
        for cp in taps:
            cp.wait_recv()
        for cp in taps:
            cp.wait_send()

    (got,) = _on_sequencer("allgather_conv", collective_id, 6 * len(bufs) + 3, _gather_peers, protocol,
                           operands=(cw_block,), out_types=(_sds((3, *cw_block.shape), F32),))
    return [r[...] for r in refs], got


def _exchange_protocol(gs, outs, shapes, send_sems, recv_sems):
    x, y, c, _ = _place()
    sends = []
    for u in range(len(gs)):
        cp = pltpu.make_async_remote_copy(
            src_ref=gs[u].at[:, _half_rows(1 - c, shapes[u][1]), :], dst_ref=outs[u],
            send_sem=send_sems.at[u], recv_sem=recv_sems.at[u], device_id=(x, y, 1 - c), device_id_type=MESH)
        cp.start()
        sends.append(cp)
    for cp in sends:
        cp.wait_recv()
    for cp in sends:
        cp.wait_send()


def _seq_exchange(name, collective_id, grads):
    shapes = [g.shape for g in grads]
    return _on_sequencer(
        name, collective_id, len(grads), _sibling_peer,
        lambda gs, outs, send_sems, recv_sems: _exchange_protocol(gs, outs, shapes, send_sems, recv_sems),
        operands=grads, out_types=[_sds((s[0], s[1] // 2, s[2]), F32) for s in shapes])


def _sum_halves(name, g, got, place, after):
    _, r, cdim = g.shape
    rh = r // 2
    rt = _row_tile(rh, 4 * N_CHIPS * cdim, 4 * ELEMENTWISE_BLOCK)
    nr = rh // rt

    def body(s_ref, g_ref, got_ref, after_ref, pb_ref, pf_ref):
        pb_ref[...] = (g_ref[...] + got_ref[...]).astype(BF16)
        mine = s_ref[1]
        pf_ref[...] = g_ref[mine] + got_ref[mine]

    quarters = (N_CHIPS, rt, cdim)
    grid_spec = pltpu.PrefetchScalarGridSpec(
        num_scalar_prefetch=1, grid=(nr,),
        in_specs=[pl.BlockSpec(quarters, lambda i, s: (0, s[0] * nr + i, 0)),
                  pl.BlockSpec(quarters, lambda i, s: (0, i, 0)),
                  pl.BlockSpec(memory_space=pl.ANY)],
        out_specs=[pl.BlockSpec(quarters, lambda i, s: (0, i, 0)),
                   pl.BlockSpec((rt, cdim), lambda i, s: (i, 0))])
    return pl.pallas_call(
        body, name=name, grid_spec=grid_spec,
        out_shape=[_sds((N_CHIPS, rh, cdim), BF16), _sds((rh, cdim), F32)],
        compiler_params=_params(("parallel",)))(place, g, got, after)


def _scatter_protocol(ps, outs, send_sems, recv_sems):
    x, y, c, other_chips = _place()
    sends = []
    for u in range(len(ps)):
        for k, chip in enumerate(other_chips):
            cp = pltpu.make_async_remote_copy(
                src_ref=ps[u].at[2 * chip[0] + chip[1]], dst_ref=outs[u].at[k],
                send_sem=send_sems.at[3 * u + k], recv_sem=recv_sems.at[3 * u + k],
                device_id=(*chip, c), device_id_type=MESH)
            cp.start()
            sends.append(cp)
    for cp in sends:
        cp.wait_recv()
    for cp in sends:
        cp.wait_send()


def _seq_scatter(name, collective_id, partials):
    return _on_sequencer(
        name, collective_id, 3 * len(partials), _chip_peers, _scatter_protocol,
        operands=partials, out_types=[_sds((3, p.shape[1], p.shape[2]), BF16) for p in partials])


def _sum_partials(name, own, got, place, layer, nl, prev, after):
    rh, cdim = own.shape
    rt = _row_tile(rh, 4 * cdim, 2 * ELEMENTWISE_BLOCK)
    nr = rh // rt
    after = list(after) if isinstance(after, (list, tuple)) else [after]

    def body(s_ref, own_ref, got_ref, *rest):
        o_ref = rest[-1]
        o_ref[...] = ((own_ref[...] + got_ref[0].astype(F32)) + got_ref[1].astype(F32)) + got_ref[2].astype(F32)

    in_specs = [pl.BlockSpec((rt, cdim), lambda i, s: (i, 0)), pl.BlockSpec((3, rt, cdim), lambda i, s: (0, i, 0)),
                *[pl.BlockSpec(memory_space=pl.ANY)] * len(after)]
    args = [place, own, got, *after]
    aliases = {}
    if prev is not None:
        in_specs.append(pl.BlockSpec(memory_space=pl.ANY))
        aliases = {len(args): 0}
        args.append(prev)
    grid_spec = pltpu.PrefetchScalarGridSpec(
        num_scalar_prefetch=1, grid=(nr,), in_specs=in_specs,
        out_specs=pl.BlockSpec((None, rt, cdim), lambda i, s: (layer, s[0] * nr + i, 0)))
    return pl.pallas_call(
        body, name=name, grid_spec=grid_spec, out_shape=_sds((nl, 2 * rh, cdim), F32),
        input_output_aliases=aliases, compiler_params=_params(("parallel",)))(*args)


def _share_protocol(outs, shapes, units, send_sems, recv_sems):
    x, y, c, _ = _place()
    sends = []
    for u, (w, l) in enumerate(units):
        mine = outs[w].at[l, _half_rows(c, shapes[w][1]), :]
        cp = pltpu.make_async_remote_copy(src_ref=mine, dst_ref=mine, send_sem=send_sems.at[u],
                                          recv_sem=recv_sems.at[u], device_id=(x, y, 1 - c), device_id_type=MESH)
        cp.start()
        sends.append(cp)
    for u, (w, l) in enumerate(units):
        theirs = outs[w].at[l, _half_rows(1 - c, shapes[w][1]), :]
        pltpu.make_async_remote_copy(src_ref=theirs, dst_ref=theirs, send_sem=send_sems.at[u],
                                     recv_sem=recv_sems.at[u], device_id=(x, y, 1 - c),
                                     device_id_type=MESH).wait_recv()
    for cp in sends:
        cp.wait_send()


def _seq_share(name, collective_id, bufs):
    shapes = [b.shape for b in bufs]
    units = [(w, l) for w in range(len(bufs)) for l in range(shapes[w][0])]
    refs = [_hbm_ref(b) for b in bufs]
    _on_sequencer(name, collective_id, len(units), _sibling_peer,
                  lambda ins, outs, send_sems, recv_sems: _share_protocol(refs, shapes, units, send_sems, recv_sems))
    return [r[...] for r in refs]


def _gather_blocks(block_ref, all_ref, send_sems, recv_sems):
    x, y, c, _ = _place()
    me = 4 * x + 2 * y + c
    all_ref[me] = block_ref[...]
    sends = []
    for rel in range(1, 8):
        fx, fy, fc = (rel >> 2) & 1, (rel >> 1) & 1, rel & 1
        peer = (x ^ fx, y ^ fy, c ^ fc)
        cp = pltpu.make_async_remote_copy(src_ref=block_ref, dst_ref=all_ref.at[me], send_sem=send_sems.at[rel - 1],
                                          recv_sem=recv_sems.at[rel - 1], device_id=peer, device_id_type=MESH)
        cp.start()
        sends.append(cp)
    for cp in sends:
        cp.wait_recv()
    for cp in sends:
        cp.wait_send()


def _adam(w, g, m, v):
    m_new = ADAM_B1 * m + (1.0 - ADAM_B1) * g
    v_new = ADAM_B2 * v + (1.0 - ADAM_B2) * (g * g)
    m_hat = m_new / (1.0 - ADAM_B1 ** ADAM_STEP)
    v_hat = v_new / (1.0 - ADAM_B2 ** ADAM_STEP)
    delta = -ADAM_LR * (m_hat / (jnp.sqrt(v_hat) + ADAM_EPS) + ADAM_WD * w)
    return delta, m_new, v_new


def _small_step(dnm0, dnm1, dnf0, dnf1, dcw, dqg, dkg, dsk, loss, w_blk, m_blk, v_blk, cw_cols):
    d = w_blk.shape[1]
    vm = pl.BlockSpec(memory_space=pltpu.VMEM)

    def reduce_body(dnm0_ref, dnm1_ref, dnf0_ref, dnf1_ref, dcw_ref, dqg_ref, dkg_ref, dsk_ref, loss_ref,
                    g_ref, blk_ref, all_ref, send_sems, recv_sems):
        blk_ref[...] = jnp.zeros_like(blk_ref)
        for row, part_ref in ((SENT_NORM_MIXER, dnm0_ref), (SENT_NORM_MIXER + 1, dnm1_ref),
                              (SENT_NORM_FFN, dnf0_ref), (SENT_NORM_FFN + 1, dnf1_ref)):
            blk_ref[row:row + 1, :] = jnp.sum(part_ref[...], axis=0, keepdims=True)
        blk_ref[SENT_CONV_W:SENT_CONV_W + 3, :] = dcw_ref[...]
        misc = slice(SENT_MISC, SENT_MISC + 1)
        for tile, gain_ref in ((TILE_Q_GAIN, dqg_ref), (TILE_K_GAIN, dkg_ref)):
            pair = gain_ref[...]
            blk_ref[misc, tile * LANES:(tile + 1) * LANES] = pair + pltpu.roll(pair, HEAD_DIM, 1)
        for h in range(N_Q_HEADS):
            lane = TILE_SINKS * LANES + h
            blk_ref[misc, lane:lane + 1] = jnp.sum(dsk_ref[h:h + 1, :], axis=1, keepdims=True)
        blk_ref[misc, TILE_LOSS * LANES:(TILE_LOSS + 1) * LANES] = jnp.broadcast_to(loss_ref[...], (1, LANES))
        _gather_blocks(blk_ref, all_ref, send_sems, recv_sems)
        g = all_ref[0]
        for dev in range(1, 8):
            g = g + all_ref[dev]
        g_ref[...] = jnp.zeros_like(g_ref)
        for sent, row, n in ((SENT_NORM_MIXER, ROW_NORM_MIXER, 2), (SENT_NORM_FFN, ROW_NORM_FFN, 2),
                             (SENT_CONV_W, ROW_CONV_W, 3), (SENT_MISC, ROW_MISC, 1)):
            g_ref[row:row + n, :] = g[sent:sent + n]

    g_blk = pl.pallas_call(
        reduce_body, name="small_allreduce", in_specs=[vm] * 9, out_specs=vm, out_shape=_sds((SMALL_ROWS, d), F32),
        scratch_shapes=[pltpu.VMEM((SUBLANES, d), F32), pltpu.VMEM((8, SUBLANES, d), F32),
                        pltpu.SemaphoreType.DMA((7,)), pltpu.SemaphoreType.DMA((7,))],
    )(dnm0, dnm1, dnf0, dnf1, dcw, dqg, dkg, dsk, loss)

    def body(g_ref, w_ref, m_ref, v_ref, *out_refs):
        g = g_ref[...]
        misc = slice(ROW_MISC, ROW_MISC + 1)
        out_refs[0][...] = g[misc, TILE_LOSS * LANES:TILE_LOSS * LANES + 1]
        chip = 2 * lax.axis_index("x") + lax.axis_index("y")
        for i, blk in enumerate((g, *_adam(w_ref[...], g, m_ref[...], v_ref[...]))):
            nm_ref, nf_ref, cw_ref, qg_ref, kg_ref, sk_ref = out_refs[1 + 6 * i:7 + 6 * i]
            nm_ref[...] = blk[ROW_NORM_MIXER:ROW_NORM_MIXER + 2]
            nf_ref[...] = blk[ROW_NORM_FFN:ROW_NORM_FFN + 2]
            qg_ref[...] = blk[misc, TILE_Q_GAIN * LANES:TILE_Q_GAIN * LANES + HEAD_DIM]
            kg_ref[...] = blk[misc, TILE_K_GAIN * LANES:TILE_K_GAIN * LANES + HEAD_DIM]
            sk_ref[...] = blk[misc, TILE_SINKS * LANES:TILE_SINKS * LANES + N_Q_HEADS]
            for q in range(N_CHIPS):
                @pl.when(chip == q)
                def _(blk=blk, cw_ref=cw_ref, q=q):
                    cw_ref[0] = blk[ROW_CONV_W:ROW_CONV_W + 3, q * cw_cols:(q + 1) * cw_cols]

    group = [_sds((2, d), F32), _sds((2, d), F32), _sds((1, 3, cw_cols), F32), _sds((1, HEAD_DIM), F32),
             _sds((1, HEAD_DIM), F32), _sds((1, N_Q_HEADS), F32)]
    outs = pl.pallas_call(
        body, name="small_adam", in_specs=[vm] * 4, out_specs=[vm] * 25, out_shape=[_sds((1, 1), F32)] + group * 4,
    )(g_blk, w_blk, m_blk, v_blk)
    names = ("norm_mixer", "norm_ffn", "conv_w", "attn_q_gain", "attn_k_gain", "attn_sinks")
    return outs[0], [dict(zip(names, outs[1 + 6 * i:7 + 6 * i])) for i in range(4)]


def _adam_step(name, w, g, m, v):
    nl, r, cdim = w.shape
    rt = _row_tile(r, 4 * cdim, ELEMENTWISE_BLOCK)

    def body(w_ref, g_ref, m_ref, v_ref, go_ref, d_ref, mo_ref, vo_ref):
        gv = g_ref[...]
        go_ref[...] = gv
        delta, m_new, v_new = _adam(w_ref[...], gv, m_ref[...], v_ref[...])
        d_ref[...] = delta
        mo_ref[...] = m_new
        vo_ref[...] = v_new

    spec = pl.BlockSpec((None, rt, cdim), lambda l, i: (l, i, 0))
    return pl.pallas_call(
        body, name=name, grid=(nl, r // rt), in_specs=[spec] * 4, out_specs=[spec] * 4,
        out_shape=[_sds(w.shape, F32)] * 4,
        compiler_params=_params(("parallel", "parallel")))(w, g, m, v)


def _pad_rows(a, rows=SUBLANES):
    return jnp.pad(a, ((0, rows - a.shape[0]), (0, 0)))


def _small_block(nm, nf, cw_local, qg, kg, sk, chip):
    d = nm.shape[1]
    cw_rows = lax.dynamic_update_slice(jnp.zeros((SUBLANES, d), F32), cw_local, (0, chip * cw_local.shape[1]))
    misc = jnp.concatenate([qg, qg, kg, kg, jnp.pad(sk, ((0, 0), (0, LANES - sk.shape[1]))),
                            jnp.zeros((1, d - 3 * LANES), F32)], axis=1)
    return jnp.concatenate([_pad_rows(nm), _pad_rows(nf), cw_rows, _pad_rows(misc)], axis=0)


WEIGHT_NAMES = ("conv_w_in", "conv_w", "conv_w_out", "attn_w_qkv", "attn_q_gain", "attn_k_gain", "attn_sinks",
                "attn_w_o", "norm_mixer", "norm_ffn", "ffn_w_gate_up", "ffn_w_down")
BIG = ("conv_w_in", "conv_w_out", "attn_w_qkv", "attn_w_o", "ffn_w_gate_up", "ffn_w_down")


def kernel(x, conv_w_in, conv_w, conv_w_out, attn_w_qkv, attn_q_gain, attn_k_gain, attn_sinks, attn_w_o, norm_mixer, norm_ffn, ffn_w_gate_up, ffn_w_down, loss_target, m_conv_w_in, m_conv_w, m_conv_w_out, m_attn_w_qkv, m_attn_q_gain, m_attn_k_gain, m_attn_sinks, m_attn_w_o, m_norm_mixer, m_norm_ffn, m_ffn_w_gate_up, m_ffn_w_down, v_conv_w_in, v_conv_w, v_conv_w_out, v_attn_w_qkv, v_attn_q_gain, v_attn_k_gain, v_attn_sinks, v_attn_w_o, v_norm_mixer, v_norm_ffn, v_ffn_w_gate_up, v_ffn_w_down):
    w = dict(conv_w_in=conv_w_in, conv_w=conv_w, conv_w_out=conv_w_out, attn_w_qkv=attn_w_qkv,
             attn_q_gain=attn_q_gain, attn_k_gain=attn_k_gain, attn_sinks=attn_sinks, attn_w_o=attn_w_o,
             norm_mixer=norm_mixer, norm_ffn=norm_ffn, ffn_w_gate_up=ffn_w_gate_up, ffn_w_down=ffn_w_down)
    m = dict(conv_w_in=m_conv_w_in, conv_w=m_conv_w, conv_w_out=m_conv_w_out, attn_w_qkv=m_attn_w_qkv,
             attn_q_gain=m_attn_q_gain, attn_k_gain=m_attn_k_gain, attn_sinks=m_attn_sinks, attn_w_o=m_attn_w_o,
             norm_mixer=m_norm_mixer, norm_ffn=m_norm_ffn, ffn_w_gate_up=m_ffn_w_gate_up, ffn_w_down=m_ffn_w_down)
    v = dict(conv_w_in=v_conv_w_in, conv_w=v_conv_w, conv_w_out=v_conv_w_out, attn_w_qkv=v_attn_w_qkv,
             attn_q_gain=v_attn_q_gain, attn_k_gain=v_attn_k_gain, attn_sinks=v_attn_sinks, attn_w_o=v_attn_w_o,
             norm_mixer=v_norm_mixer, norm_ffn=v_norm_ffn, ffn_w_gate_up=v_ffn_w_gate_up, ffn_w_down=v_ffn_w_down)

    nseq, seq, d = x.shape
    t = nseq * seq
    chip = 2 * lax.axis_index("x") + lax.axis_index("y")
    core = lax.axis_index("c")
    place = jnp.stack([core, chip]).astype(jnp.int32)
    x0 = x.reshape(t, d)
    tgt = loss_target.reshape(t, d)

    cw_block = lax.dynamic_update_slice(jnp.zeros((SUBLANES, d), F32), conv_w[0], (0, chip * conv_w.shape[2]))
    def cast(k, layer=None):
        return _cast_own(f"cast_{k}" + ("" if layer is None else str(layer)), w[k], place, layer)

    (w_in,), cw_got = _seq_allgather_conv(1, [cast("conv_w_in")], cw_block)
    w_out, w_gu0, w_dn0 = _seq_allgather(
        "allgather_ffn0", 2, [cast("conv_w_out"), cast("ffn_w_gate_up", 0), cast("ffn_w_down", 0)])
    w_qkv, w_o, w_gu1, w_dn1 = _seq_allgather(
        "allgather_rest", 3, [cast("attn_w_qkv"), cast("attn_w_o"), cast("ffn_w_gate_up", 1), cast("ffn_w_down", 1)])
    w_out = w_out.reshape(1, d, d)
    w_o = w_o.reshape(1, d, d)
    w_gu = [w_gu0, w_gu1]
    w_dn = [w_dn0.reshape(1, D_FF, d), w_dn1.reshape(1, D_FF, d)]

    qg_pair = jnp.concatenate([attn_q_gain, attn_q_gain], axis=1)
    kg_pair = jnp.concatenate([attn_k_gain, attn_k_gain], axis=1)

    def ffn_bwd(i, dxo, dxo_b, xin, h, g, u, a):
        g_dn = _wgrad_down(f"ffn{i}_down_wgrad", a, dxo_b, D_FF // 2)
        dg, du = _mm_down_t_swiglu(f"ffn{i}_down_dgrad", dxo_b, w_dn[i], 0, g, u)
        g_gu = _wgrad_up2(f"ffn{i}_up_wgrad", h, dg, du)
        dxi, dxi_b, dgain = _dgrad_norm_ffn(f"ffn{i}_up_dgrad", dg, du, w_gu[i], 0, xin, norm_ffn[i:i + 1], dxo)
        return dxi, dxi_b, dgain, g_gu, g_dn

    h0, bcx = _mm_norm_up_joined("conv_in", x0, norm_mixer[0:1], w_in, 512)
    z = _conv_fwd(bcx, cw_block, cw_got, nseq, seq)
    x1, h1 = _mm_down_norm("conv_out", z, w_out, 0, x0, norm_ffn[0:1])
    g0, u0, a0 = _mm_up_swiglu("ffn0_up", h1, w_gu[0], 0)
    x2, h2 = _mm_down_norm("ffn0_down", a0, w_dn[0], 0, x1, norm_mixer[1:2])
    qkv = _mm_up_joined("attn_qkv", h2, w_qkv, 1024)
    o = _attn_fwd(qkv, qg_pair, kg_pair, attn_sinks, nseq, seq)
    x3, h3 = _mm_down_norm("attn_out", o, w_o, 0, x2, norm_ffn[1:2])
    g1, u1, a1 = _mm_up_swiglu("ffn1_up", h3, w_gu[1], 0)
    dy, dy_b, loss_part = _mm_down_loss("ffn1_down", a1, w_dn[1], 0, x3, tgt)

    finished = {k: None for k in BIG}

    def exchange(tag, cid, units):
        return units, _seq_exchange(f"exchange_{tag}", cid, [g for _, _, g in units])

    def scatter(tag, cid, group, after):
        units, got = group
        sums = [_sum_halves(f"sum_halves_{k}{l}", g, r, place, after) for (k, l, g), r in zip(units, got)]
        return units, sums, _seq_scatter(f"scatter_{tag}", cid, [pb for pb, _ in sums])

    def finish(group, after):
        units, sums, arrived = group
        for (k, l, _), (_, pf), r in zip(units, sums, arrived):
            finished[k] = _sum_partials(f"sum_partials_{k}{l}", pf, r, place, l, w[k].shape[0], finished[k], after)

    dx3, dx3_b, dnf1, g_gu1, g_dn1 = ffn_bwd(1, dy, dy_b, x3, h3, g1, u1, a1)
    ffn1 = exchange("ffn1", 4, [("ffn_w_down", 1, g_dn1), ("ffn_w_gate_up", 1, g_gu1)])
    g_o = _wgrad_down("attn_out_wgrad", o, dx3_b, d)
    do = _mm_down_t("attn_out_dgrad", dx3_b, w_o, 0)
    ffn1 = scatter("ffn1", 8, ffn1, do)
    dqkv, dqg, dkg, dsk = _attn_bwd(do, qkv, qg_pair, kg_pair, attn_sinks, nseq, seq)
    g_qkv = _wgrad_joined("attn_qkv_wgrad", h2, dqkv)
    attn = exchange("attn", 5, [("attn_w_o", 0, g_o), ("attn_w_qkv", 0, g_qkv)])
    dx2, dx2_b, dnm1 = _dgrad_norm_qkv("attn_qkv_dgrad", dqkv, w_qkv, x2, norm_mixer[1:2], dx3)
    finish(ffn1, dx2)
    attn = scatter("attn", 9, attn, dx2)
    dx1, dx1_b, dnf0, g_gu0, g_dn0 = ffn_bwd(0, dx2, dx2_b, x1, h1, g0, u0, a0)
    ffn0 = exchange("ffn0", 6, [("ffn_w_down", 0, g_dn0), ("ffn_w_gate_up", 0, g_gu0)])
    g_out = _wgrad_down("conv_out_wgrad", z, dx1_b, d)
    dz = _mm_down_t("conv_out_dgrad", dx1_b, w_out, 0)
    finish(attn, dz)
    ffn0 = scatter("ffn0", 10, ffn0, dz)
    dbcx, dcw = _conv_bwd(dz, bcx, cw_block, cw_got, nseq, seq)
    g_in = _wgrad_conv_in("conv_in_wgrad", h0, dbcx, conv_w_in.shape[2])
    conv = exchange("conv", 7, [("conv_w_out", 0, g_out), ("conv_w_in", 0, g_in)])
    dx0, _, dnm0 = _dgrad_norm_conv("conv_in_dgrad", dbcx, w_in, x0, norm_mixer[0:1], dx1)
    finish(ffn0, dx0)
    late = ("attn_w_qkv", "attn_w_o", "ffn_w_gate_up", "ffn_w_down")
    grads_late = _seq_share("share_late", 12, [finished[k] for k in late])
    conv = scatter("conv", 11, conv, dx0)

    grad, delta, new_m, new_v = {}, {}, {}, {}

    def adam(k, g):
        grad[k], delta[k], new_m[k], new_v[k] = _adam_step(f"adam_{k}", w[k], g, m[k], v[k])

    for k, g in zip(late, grads_late):
        adam(k, g)

    def blocks(src):
        return _small_block(src["norm_mixer"], src["norm_ffn"], src["conv_w"][0], src["attn_q_gain"],
                            src["attn_k_gain"], src["attn_sinks"], chip)

    loss, small = _small_step(dnm0, dnm1, dnf0, dnf1, dcw, dqg, dkg, dsk, loss_part,
                              blocks(w), blocks(m), blocks(v), conv_w.shape[2])
    for dst, part in zip((grad, delta, new_m, new_v), small):
        dst.update(part)

    finish(conv, [new_v[k] for k in late])
    last = ("conv_w_in", "conv_w_out")
    for k, g in zip(last, _seq_share("share_last", 13, [finished[k] for k in last])):
        adam(k, g)

    return (loss.reshape(()), dx0.reshape(nseq, seq, d), *[grad[k] for k in WEIGHT_NAMES], *[delta[k] for k in WEIGHT_NAMES],
            *[new_m[k] for k in WEIGHT_NAMES], *[new_v[k] for k in WEIGHT_NAMES])
```
